```python
import math
import jax, jax.numpy as jnp
from jax import lax
import numpy as np

D_MODEL = 1024
BATCH = 2
SEQ = 8192
DEPTH = 1

GDN_HEADS = 4
GDN_HEAD_DIM = 128
GDN_WIDTH = GDN_HEADS * GDN_HEAD_DIM
ATT_HEADS = 4
ATT_HEAD_DIM = 128
ATT_WIDTH = ATT_HEADS * ATT_HEAD_DIM
CONV_WIDTH = 4
GDN_CHUNK = 64
DILATED_PATTERNS = ((128, 1), (512, 4), (2048, 16))
ATT_BLOCK = 128
ROPE_THETA = 500000.0
ROPE_DIMS = ATT_HEAD_DIM // 4
IN_SPLITS = (3 * GDN_WIDTH,
             4 * GDN_WIDTH,
             4 * GDN_WIDTH + GDN_HEADS,
             4 * GDN_WIDTH + 2 * GDN_HEADS,
             4 * GDN_WIDTH + 2 * GDN_HEADS + ATT_WIDTH,
             4 * GDN_WIDTH + 2 * GDN_HEADS + 2 * ATT_WIDTH)
IN_COLS = 4 * GDN_WIDTH + 2 * GDN_HEADS + 3 * ATT_WIDTH
N_GROUPS = 4
EXPERTS_PER_GROUP = 8
N_EXPERTS = N_GROUPS * EXPERTS_PER_GROUP
TOP_K_IN_GROUP = 2
EXPERT_FF = 512
MOE_BLOCK = 128
DEEPNORM_ALPHA = (2.0 * DEPTH) ** 0.25
DEEPNORM_BETA = (8.0 * DEPTH) ** -0.25
LN_EPS = 1e-5
RMS_EPS = 1e-6

kernel_name = "hymba_gdn_dilated_hmoe_deepnorm_adaln"

F32 = jnp.float32


def layer_norm(x, g, b):
    xf = x.astype(F32)
    mu = jnp.mean(xf, -1, keepdims=True)
    var = jnp.mean(jnp.square(xf - mu), -1, keepdims=True)
    return ((xf - mu) * lax.rsqrt(var + LN_EPS) * g + b).astype(x.dtype)


def rms_norm(x, g):
    xf = x.astype(F32)
    return xf * lax.rsqrt(jnp.mean(jnp.square(xf), -1, keepdims=True) + RMS_EPS) * g


def l2_normalize(x):
    xf = x.astype(F32)
    return xf * lax.rsqrt(jnp.sum(jnp.square(xf), -1, keepdims=True) + RMS_EPS)


def split_heads(t, n_heads):
    b, s, _ = t.shape
    return t.reshape(b, s, n_heads, -1).transpose(0, 2, 1, 3)


def causal_depthwise_conv(x, w):
    k, ch = w.shape
    return lax.conv_general_dilated(
        x, w[:, None, :].astype(x.dtype), window_strides=(1,), padding=[(k - 1, 0)],
        dimension_numbers=("NWC", "WIO", "NWC"), feature_group_count=ch)


def partial_rope(x, positions):
    half = ROPE_DIMS // 2
    inv_freq = ROPE_THETA ** (-jnp.arange(half, dtype=F32) * 2.0 / ROPE_DIMS)
    ang = positions.astype(F32)[:, None, :, None] * inv_freq
    cos, sin = jnp.cos(ang), jnp.sin(ang)
    xr = x[..., :ROPE_DIMS].astype(F32)
    x1, x2 = xr[..., :half], xr[..., half:]
    rot = jnp.concatenate([x1 * cos - x2 * sin, x2 * cos + x1 * sin], -1)
    return jnp.concatenate([rot, x[..., ROPE_DIMS:].astype(F32)], -1)


def gated_delta_rule_chunked(q, k, v, g, beta):
    b, h, s, dk = q.shape
    dv = v.shape[-1]
    nc = s // GDN_CHUNK
    q = q.reshape(b, h, nc, GDN_CHUNK, dk)
    k = k.reshape(b, h, nc, GDN_CHUNK, dk)
    v = v.reshape(b, h, nc, GDN_CHUNK, dv)
    g = g.reshape(b, h, nc, GDN_CHUNK)
    beta = beta.reshape(b, h, nc, GDN_CHUNK)
    gc = jnp.cumsum(g, -1)
    idx = jnp.arange(GDN_CHUNK)
    lower = idx[:, None] >= idx[None, :]
    strict = idx[:, None] > idx[None, :]
    decay = jnp.exp(jnp.where(lower, gc[..., :, None] - gc[..., None, :], -jnp.inf))
    kb = k * beta[..., None]
    vb = v * beta[..., None]
    a_mat = jnp.where(strict, jnp.einsum("bhnid,bhnjd->bhnij", kb, k) * decay, 0.0)
    t_mat = a_mat + jnp.eye(GDN_CHUNK, dtype=F32)
    rhs = jnp.concatenate([vb, kb * jnp.exp(gc)[..., None]], -1)
    sol = lax.linalg.triangular_solve(t_mat, rhs, left_side=True, lower=True, unit_diagonal=True)
    u, w = sol[..., :dv], sol[..., dv:]
    qk = jnp.where(lower, jnp.einsum("bhnid,bhnjd->bhnij", q, k) * decay, 0.0)
    q_dec = q * jnp.exp(gc)[..., None]
    k_dec = k * jnp.exp(gc[..., -1:] - gc)[..., None]
    g_last = jnp.exp(gc[..., -1])

    def step(state, inp):
        qk_c, qd_c, w_c, u_c, kd_c, gl_c = inp
        v_new = u_c - jnp.einsum("bhcd,bhdv->bhcv", w_c, state)
        o = jnp.einsum("bhcd,bhdv->bhcv", qd_c, state) + jnp.einsum("bhij,bhjv->bhiv", qk_c, v_new)
        state = state * gl_c[..., None, None] + jnp.einsum("bhcd,bhcv->bhdv", kd_c, v_new)
        return state, o

    xs = tuple(jnp.moveaxis(t, 2, 0) for t in (qk, q_dec, w, u, k_dec, g_last))
    _, o = lax.scan(step, jnp.zeros((b, h, dk, dv), F32), xs)
    return jnp.moveaxis(o, 0, 2).reshape(b, h, s, dv)


def banded_attention_stats(q, k, v, span):
    lead = q.shape[:-2]
    length, dh = q.shape[-2], q.shape[-1]
    nb = -(-length // ATT_BLOCK)
    lp = nb * ATT_BLOCK
    padw = [(0, 0)] * len(lead)
    q = jnp.pad(q, padw + [(0, lp - length), (0, 0)])
    k = jnp.pad(k, padw + [(ATT_BLOCK, lp - length), (0, 0)])
    v = jnp.pad(v, padw + [(ATT_BLOCK, lp - length), (0, 0)])
    qb = q.reshape(*lead, nb, ATT_BLOCK, dh)

    def prev_and_current(t):
        prev = t[..., :lp, :].reshape(*lead, nb, ATT_BLOCK, dh)
        cur = t[..., ATT_BLOCK:, :].reshape(*lead, nb, ATT_BLOCK, dh)
        return jnp.concatenate([prev, cur], -2)

    kb, vb = prev_and_current(k), prev_and_current(v)
    scores = jnp.einsum("...nqd,...nkd->...nqk", qb, kb, preferred_element_type=F32)
    qi = jnp.arange(ATT_BLOCK)[:, None] + ATT_BLOCK
    kj = jnp.arange(2 * ATT_BLOCK)[None, :]
    dist = qi - kj
    key_pos = jnp.arange(nb)[:, None, None] * ATT_BLOCK + kj - ATT_BLOCK
    mask = (dist >= 0) & (dist <= span) & (key_pos >= 0)
    scores = jnp.where(mask, scores, -jnp.inf)
    m = jnp.max(scores, -1)
    p = jnp.exp(scores - m[..., None])
    s = jnp.sum(p, -1)
    o = jnp.einsum("...nqk,...nkd->...nqd", p, vb.astype(F32))
    m = m.reshape(*lead, lp)[..., :length]
    s = s.reshape(*lead, lp)[..., :length]
    o = o.reshape(*lead, lp, dh)[..., :length, :]
    return m, s, o


def dilated_sliding_attention(q, k, v):
    b, h, s, dh = q.shape
    ms, ss, os_ = [], [], []
    for window, dil in DILATED_PATTERNS:
        span = window // dil

        def to_classes(t):
            return t.reshape(b, h, s // dil, dil, dh).swapaxes(2, 3)

        m, sd, o = banded_attention_stats(to_classes(q), to_classes(k), to_classes(v), span)
        ms.append(m.swapaxes(2, 3).reshape(b, h, s))
        ss.append(sd.swapaxes(2, 3).reshape(b, h, s))
        os_.append(o.swapaxes(2, 3).reshape(b, h, s, dh))
    m_all = jnp.stack(ms)
    wts = jnp.exp(m_all - jnp.max(m_all, 0, keepdims=True))
    num = jnp.sum(wts[..., None] * jnp.stack(os_), 0)
    den = jnp.sum(wts * jnp.stack(ss), 0)
    return num / den[..., None]


def hybrid_mixer(h, positions, w_in, conv_w, a_log, dt_bias, gdn_norm_w, attn_norm_w, w_o):
    b, s, _ = h.shape
    proj = h @ w_in
    qkv_a, z_a, beta_logit, decay_logit, q_b, k_b, v_b = jnp.split(proj, IN_SPLITS, -1)
    qkv_a = jax.nn.silu(causal_depthwise_conv(qkv_a, conv_w))
    q_a, k_a, v_a = jnp.split(qkv_a, 3, -1)
    q_a = l2_normalize(split_heads(q_a, GDN_HEADS)) * (GDN_HEAD_DIM ** -0.5)
    k_a = l2_normalize(split_heads(k_a, GDN_HEADS))
    v_a = split_heads(v_a, GDN_HEADS).astype(F32)
    beta = jax.nn.sigmoid(beta_logit.astype(F32)).transpose(0, 2, 1)
    g = (-jnp.exp(a_log) * jax.nn.softplus(decay_logit.astype(F32) + dt_bias)).transpose(0, 2, 1)
    o_a = gated_delta_rule_chunked(q_a, k_a, v_a, g, beta).transpose(0, 2, 1, 3)
    z = z_a.reshape(b, s, GDN_HEADS, GDN_HEAD_DIM).astype(F32)
    o_a = rms_norm(o_a, gdn_norm_w) * jax.nn.silu(z)
    q_b = partial_rope(split_heads(q_b, ATT_HEADS), positions) * (ATT_HEAD_DIM ** -0.5)
    k_b = partial_rope(split_heads(k_b, ATT_HEADS), positions)
    v_b = split_heads(v_b, ATT_HEADS).astype(F32)
    o_b = dilated_sliding_attention(q_b, k_b, v_b).transpose(0, 2, 1, 3)
    o_b = rms_norm(o_b, attn_norm_w)
    mixed = jnp.concatenate([o_a.reshape(b, s, GDN_WIDTH), o_b.reshape(b, s, ATT_WIDTH)], -1)
    return mixed.astype(h.dtype) @ w_o


def grouped_expert_mlp(xf, expert_ids, gates, w_gate, w_up, w_down):
    t, d = xf.shape
    n = expert_ids.shape[0]
    top = n // t
    token_ids = jnp.arange(n, dtype=jnp.int32) // top
    order = jnp.argsort(expert_ids)
    e_sorted = expert_ids[order]
    counts = jnp.zeros((N_EXPERTS,), jnp.int32).at[expert_ids].add(1)
    padded = (counts + MOE_BLOCK - 1) // MOE_BLOCK * MOE_BLOCK
    pad_end = jnp.cumsum(padded)
    pad_start = pad_end - padded
    start = jnp.cumsum(counts) - counts
    dest = pad_start[e_sorted] + jnp.arange(n, dtype=jnp.int32) - start[e_sorted]
    n_slots = -(-n // MOE_BLOCK) * MOE_BLOCK + N_EXPERTS * MOE_BLOCK
    n_blocks = n_slots // MOE_BLOCK
    slot_tok = jnp.full((n_slots,), t, jnp.int32).at[dest].set(token_ids[order])
    slot_gate = jnp.zeros((n_slots,), F32).at[dest].set(gates[order])
    blk_expert = jnp.minimum(
        jnp.searchsorted(pad_end, jnp.arange(n_blocks, dtype=jnp.int32) * MOE_BLOCK, side="right"),
        N_EXPERTS - 1)
    x_pad = jnp.concatenate([xf, jnp.zeros((1, d), xf.dtype)], 0)
    xs = x_pad[slot_tok].reshape(n_blocks, MOE_BLOCK, d)

    def expert_block(args):
        xb, e = args
        hid = jax.nn.silu(xb @ w_gate[e]) * (xb @ w_up[e])
        return hid @ w_down[e]

    ys = lax.map(expert_block, (xs, blk_expert)).reshape(n_slots, d)
    y = jnp.zeros((t + 1, d), F32).at[slot_tok].add(ys.astype(F32) * slot_gate[:, None])
    return y[:t]


def hierarchical_moe(h, w_rg, b_rg, w_re, b_re, w_gate, w_up, w_down):
    b, s, d = h.shape
    hf = h.reshape(b * s, d)
    lg = (hf @ w_rg).astype(F32) + b_rg
    pg = jax.nn.softmax(lg, -1)
    grp = jnp.argmax(lg, -1).astype(jnp.int32)
    gate_grp = jnp.take_along_axis(pg, grp[:, None], -1)
    le = ((hf @ w_re).astype(F32) + b_re).reshape(b * s, N_GROUPS, EXPERTS_PER_GROUP)
    le_sel = jnp.take_along_axis(le, grp[:, None, None], 1)[:, 0]
    top_v, top_i = lax.top_k(le_sel, TOP_K_IN_GROUP)
    gates = gate_grp * jax.nn.softmax(top_v, -1)
    experts = grp[:, None] * EXPERTS_PER_GROUP + top_i.astype(jnp.int32)
    y = grouped_expert_mlp(hf, experts.reshape(-1), gates.reshape(-1), w_gate, w_up, w_down)
    return y.reshape(b, s, d).astype(h.dtype)


def setup_inputs(seed: int = 0) -> dict:
    key = jax.random.key(seed)
    ks = jax.random.split(key, 24)
    D, L = D_MODEL, DEPTH
    nrm = jax.random.normal
    x = nrm(ks[0], (BATCH, SEQ, D), F32)
    c = nrm(ks[1], (BATCH, D), F32)
    offs = jax.random.randint(ks[2], (BATCH, 1), 0, 4096, dtype=jnp.int32)
    positions = jnp.arange(SEQ, dtype=jnp.int32)[None, :] + offs
    w_ada = nrm(ks[3], (L, D, 6 * D), F32) * (0.1 * D ** -0.5)
    b_ada = nrm(ks[4], (L, 6 * D), F32) * 0.01
    w_in = nrm(ks[5], (L, D, IN_COLS), F32) * D ** -0.5
    conv_w = nrm(ks[6], (L, CONV_WIDTH, 3 * GDN_WIDTH), F32) * CONV_WIDTH ** -0.5
    a_log = jnp.log(jax.random.uniform(ks[7], (L, GDN_HEADS), F32, 1.0, 16.0))
    dt = jnp.exp(jax.random.uniform(ks[8], (L, GDN_HEADS), F32, math.log(1e-3), math.log(1e-1)))
    dt_bias = dt + jnp.log(-jnp.expm1(-dt))
    gdn_norm_w = 1.0 + 0.02 * nrm(ks[9], (L, GDN_HEAD_DIM), F32)
    attn_norm_w = 1.0 + 0.02 * nrm(ks[10], (L, ATT_HEAD_DIM), F32)
    w_o = nrm(ks[11], (L, D, D), F32) * (D ** -0.5 * DEEPNORM_BETA)
    ln1_g = 1.0 + 0.02 * nrm(ks[12], (L, D), F32)
    ln1_b = 0.02 * nrm(ks[13], (L, D), F32)
    w_router_group = nrm(ks[14], (L, D, N_GROUPS), F32) * D ** -0.5
    b_router_group = 0.01 * nrm(ks[15], (L, N_GROUPS), F32)
    w_router_expert = nrm(ks[16], (L, D, N_EXPERTS), F32) * D ** -0.5
    b_router_expert = 0.01 * nrm(ks[17], (L, N_EXPERTS), F32)
    w_gate = nrm(ks[18], (L, N_EXPERTS, D, EXPERT_FF), F32) * D ** -0.5
    w_up = nrm(ks[19], (L, N_EXPERTS, D, EXPERT_FF), F32) * D ** -0.5
    w_down = nrm(ks[20], (L, N_EXPERTS, EXPERT_FF, D), F32) * (EXPERT_FF ** -0.5 * DEEPNORM_BETA)
    ln2_g = 1.0 + 0.02 * nrm(ks[21], (L, D), F32)
    ln2_b = 0.02 * nrm(ks[22], (L, D), F32)
    return {"x": x, "c": c, "positions": positions, "w_ada": w_ada, "b_ada": b_ada,
            "w_in": w_in, "conv_w": conv_w, "a_log": a_log, "dt_bias": dt_bias,
            "gdn_norm_w": gdn_norm_w, "attn_norm_w": attn_norm_w, "w_o": w_o,
            "ln1_g": ln1_g, "ln1_b": ln1_b, "w_router_group": w_router_group,
            "b_router_group": b_router_group, "w_router_expert": w_router_expert,
            "b_router_expert": b_router_expert, "w_gate": w_gate, "w_up": w_up,
            "w_down": w_down, "ln2_g": ln2_g, "ln2_b": ln2_b}


def reference(x, c, positions, w_ada, b_ada, w_in, conv_w, a_log, dt_bias, gdn_norm_w,
              attn_norm_w, w_o, ln1_g, ln1_b, w_router_group, b_router_group, w_router_expert,
              b_router_expert, w_gate, w_up, w_down, ln2_g, ln2_b):
    for l in range(DEPTH):
        mod = jax.nn.silu(c) @ w_ada[l] + b_ada[l]
        shift1, scale1, gate1, shift2, scale2, gate2 = jnp.split(mod[:, None, :], 6, axis=-1)
        h = x * (1 + scale1) + shift1
        mix = hybrid_mixer(h, positions, w_in[l], conv_w[l], a_log[l], dt_bias[l],
                           gdn_norm_w[l], attn_norm_w[l], w_o[l])
        x = layer_norm(DEEPNORM_ALPHA * x + (1 + gate1) * mix, ln1_g[l], ln1_b[l])
        h = x * (1 + scale2) + shift2
        y = hierarchical_moe(h, w_router_group[l], b_router_group[l], w_router_expert[l],
                             b_router_expert[l], w_gate[l], w_up[l], w_down[l])
        x = layer_norm(DEEPNORM_ALPHA * x + (1 + gate2) * y, ln2_g[l], ln2_b[l])
    return x
```

```python
import functools
import math

import jax
import jax.numpy as jnp
from jax import lax
from jax.experimental import pallas as pl
from jax.experimental.pallas import tpu as pltpu

F32 = jnp.float32
BF16 = jnp.bfloat16
HIGHEST = lax.Precision.HIGHEST

GDN_HEADS = 4
ATT_HEADS = 4
HEAD_DIM = 128
CONV_WIDTH = 4
DILATED_PATTERNS = ((128, 1), (512, 4), (2048, 16))
ROPE_THETA = 500000.0
ROPE_DIMS = HEAD_DIM // 4
N_GROUPS = 4
EXPERTS_PER_GROUP = 8
N_EXPERTS = N_GROUPS * EXPERTS_PER_GROUP
DEPTH = 1
DEEPNORM_ALPHA = (2.0 * DEPTH) ** 0.25
LN_EPS = 1e-5
RMS_EPS = 1e-6

LANES = 128
SUBLANES = 8
VMEM_LIMIT = 48 * 1024 * 1024

GDN_BLOCK = 128
GDN_TILE = 256
ATT_BLOCK = 128
ATT_TILE = 2048
ROW_TILE = 256
EXPERT_BLOCK = 256
NEG = -1e30


def _bf(x):
    return x.astype(BF16)


def _mm(a, b):
    return jnp.dot(_bf(a), _bf(b), preferred_element_type=F32)


def _mm_nt(a, b):
    return lax.dot_general(_bf(a), _bf(b), (((1,), (1,)), ((), ())), preferred_element_type=F32)


def _mm_tn(a, b):
    return lax.dot_general(_bf(a), _bf(b), (((0,), (0,)), ((), ())), preferred_element_type=F32)


def _mm_f32(a, b):
    return jnp.dot(a, b, preferred_element_type=F32, precision=HIGHEST)


def _sigmoid(x):
    return 1.0 / (1.0 + jnp.exp(-x))


def _silu(x):
    return x * _sigmoid(x)


def _softplus(x):
    return jnp.maximum(x, 0.0) + jnp.log(1.0 + jnp.exp(-jnp.abs(x)))


def _params(sem):
    return pltpu.CompilerParams(dimension_semantics=sem, vmem_limit_bytes=VMEM_LIMIT)


def _ada_kernel(c_ref, w_ref, b_ref, o_ref):
    o_ref[...] = _mm_f32(_silu(c_ref[...]), w_ref[...]) + b_ref[...]


def _ada(c, w_ada, b_ada):
    bsz, d = c.shape
    n = w_ada.shape[1]
    tn = 512
    cp = jnp.zeros((SUBLANES, d), F32).at[:bsz].set(c)
    out = pl.pallas_call(
        _ada_kernel,
        out_shape=jax.ShapeDtypeStruct((SUBLANES, n), F32),
        grid=(n // tn,),
        in_specs=[pl.BlockSpec((SUBLANES, d), lambda j: (0, 0)),
                  pl.BlockSpec((d, tn), lambda j: (0, j)),
                  pl.BlockSpec((1, tn), lambda j: (0, j))],
        out_specs=pl.BlockSpec((SUBLANES, tn), lambda j: (0, j)),
        compiler_params=_params(("parallel",)),
        name="ada",
    )(cp, w_ada, b_ada.reshape(1, n))
    return out[:bsz].reshape(bsz, 6, d)


def _inproj_kernel(x_ref, mod_ref, pos_ref, invf_ref, wqkv_ref, wz_ref, wbd_ref, wbdt_ref,
                   wq_ref, wk_ref, wv_ref,
                   qkv_ref, z_ref, bdc_ref, bdr_ref, qb_ref, kb_ref, vb_ref):
    shift = mod_ref[0, 0:1, :]
    scale = mod_ref[0, 1:2, :]
    h = _bf(x_ref[...] * (1.0 + scale) + shift)
    qkv_ref[...] = jnp.dot(h, wqkv_ref[...], preferred_element_type=F32)
    z_ref[...] = jnp.dot(h, wz_ref[...], preferred_element_type=F32)
    bdc_ref[...] = jnp.dot(h, wbd_ref[...], preferred_element_type=F32)
    bdr_ref[...] = lax.dot_general(wbdt_ref[...], h, (((1,), (1,)), ((), ())),
                                   preferred_element_type=F32)
    ang = pos_ref[...].astype(F32) * invf_ref[...]
    cosv = jnp.cos(ang)
    sinv = jnp.sin(ang)
    lane = lax.broadcasted_iota(jnp.int32, (1, LANES), 1)
    half = ROPE_DIMS // 2
    first = lane < half
    sin_signed = jnp.where(first, -sinv, sinv)

    def rope(y):
        outs = []
        for hh in range(ATT_HEADS):
            yh = y[:, hh * HEAD_DIM:(hh + 1) * HEAD_DIM]
            partner = jnp.where(first, pltpu.roll(yh, LANES - half, 1), pltpu.roll(yh, half, 1))
            outs.append(yh * cosv + partner * sin_signed)
        return jnp.concatenate(outs, axis=1)

    qb_ref[...] = rope(jnp.dot(h, wq_ref[...], preferred_element_type=F32)) * (HEAD_DIM ** -0.5)
    kb_ref[...] = rope(jnp.dot(h, wk_ref[...], preferred_element_type=F32))
    vb_ref[...] = jnp.dot(h, wv_ref[...], preferred_element_type=F32)


def _inproj(xf, mod, pos, w_in, seq):
    t, d = xf.shape
    tm = ROW_TILE
    gw = GDN_HEADS * HEAD_DIM
    aw = ATT_HEADS * HEAD_DIM
    o0 = 3 * gw
    o1 = o0 + gw
    o2 = o1 + 2 * GDN_HEADS
    wb = _bf(w_in)
    wqkv, wz = wb[:, :o0], wb[:, o0:o1]
    wbd_n = wb[:, o1:o2]
    wbd = jnp.zeros((d, LANES), BF16).at[:, :2 * GDN_HEADS].set(wbd_n)
    wbdt = wbd_n.T
    wq, wk, wv = wb[:, o2:o2 + aw], wb[:, o2 + aw:o2 + 2 * aw], wb[:, o2 + 2 * aw:o2 + 3 * aw]
    half = ROPE_DIMS // 2
    inv_freq = ROPE_THETA ** (-jnp.arange(half, dtype=F32) * 2.0 / ROPE_DIMS)
    invf = jnp.zeros((1, LANES), F32).at[0, :ROPE_DIMS].set(jnp.concatenate([inv_freq, inv_freq]))
    spb = seq // tm
    row = lambda i: (i, 0)
    const = lambda i: (0, 0)
    return pl.pallas_call(
        _inproj_kernel,
        out_shape=(jax.ShapeDtypeStruct((t, o0), F32), jax.ShapeDtypeStruct((t, gw), F32),
                   jax.ShapeDtypeStruct((t, LANES), F32), jax.ShapeDtypeStruct((SUBLANES, t), F32),
                   jax.ShapeDtypeStruct((t, aw), F32), jax.ShapeDtypeStruct((t, aw), F32),
                   jax.ShapeDtypeStruct((t, aw), F32)),
        grid=(t // tm,),
        in_specs=[pl.BlockSpec((tm, d), row),
                  pl.BlockSpec((1, 6, d), lambda i: (i // spb, 0, 0)),
                  pl.BlockSpec((tm, 1), row),
                  pl.BlockSpec((1, LANES), const),
                  pl.BlockSpec((d, o0), const), pl.BlockSpec((d, gw), const),
                  pl.BlockSpec((d, LANES), const), pl.BlockSpec((2 * GDN_HEADS, d), const),
                  pl.BlockSpec((d, aw), const), pl.BlockSpec((d, aw), const),
                  pl.BlockSpec((d, aw), const)],
        out_specs=(pl.BlockSpec((tm, o0), row), pl.BlockSpec((tm, gw), row),
                   pl.BlockSpec((tm, LANES), row), pl.BlockSpec((SUBLANES, tm), lambda i: (0, i)),
                   pl.BlockSpec((tm, aw), row), pl.BlockSpec((tm, aw), row),
                   pl.BlockSpec((tm, aw), row)),
        compiler_params=_params(("parallel",)),
        name="inproj",
    )(xf, mod, pos, invf, wqkv, wz, wbd, wbdt, wq, wk, wv)


def _gdn_kernel(qkv_ref, z_ref, bdc_ref, bdr_ref, convw_ref, gpc_ref, gpr_ref, nw_ref, o_ref,
                cbuf, q_s, k_s, v_s, u_s, w_s, qd_s, kd_s, qk_s, st_ref):
    ts = qkv_ref.shape[0]
    nb = ts // GDN_BLOCK
    gw = GDN_HEADS * HEAD_DIM
    halo = SUBLANES

    @pl.when(pl.program_id(1) == 0)
    def _():
        cbuf[0:halo, :] = jnp.zeros((halo, cbuf.shape[1]), F32)
        st_ref[...] = jnp.zeros(st_ref.shape, F32)

    cbuf[halo:halo + ts, :] = qkv_ref[...]
    for s in range(3 * GDN_HEADS):
        sl = slice(s * HEAD_DIM, (s + 1) * HEAD_DIM)
        off = halo - (CONV_WIDTH - 1)
        acc = convw_ref[0:1, sl] * cbuf[off:off + ts, sl]
        for j in range(1, CONV_WIDTH):
            acc = acc + convw_ref[j:j + 1, sl] * cbuf[off + j:off + j + ts, sl]
        y = _silu(acc)
        if s < 2 * GDN_HEADS:
            y = y * lax.rsqrt(jnp.sum(y * y, axis=-1, keepdims=True) + RMS_EPS)
        if s < GDN_HEADS:
            y = y * (HEAD_DIM ** -0.5)
        dst = (q_s, k_s, v_s)[s // GDN_HEADS]
        hs = s % GDN_HEADS
        dst[:, hs * HEAD_DIM:(hs + 1) * HEAD_DIM] = y
    cbuf[0:halo, :] = cbuf[ts:ts + halo, :]

    bdc = bdc_ref[...]
    beta_c = _sigmoid(bdc)
    g_c = -jnp.exp(gpc_ref[0:1, :]) * _softplus(bdc + gpc_ref[1:2, :])
    g_r = -jnp.exp(gpr_ref[:, 0:1]) * _softplus(bdr_ref[...] + gpr_ref[:, 1:2])
    ti = lax.broadcasted_iota(jnp.int32, (ts, ts), 0)
    tj = lax.broadcasted_iota(jnp.int32, (ts, ts), 1)
    same = (ti // GDN_BLOCK) == (tj // GDN_BLOCK)
    m_low = jnp.where(same & (tj <= ti), 1.0, 0.0).astype(F32)
    m_up = jnp.where(same & (ti <= tj), 1.0, 0.0).astype(F32)
    m_same = jnp.where(same, 1.0, 0.0).astype(F32)
    gc_c = _mm_f32(m_low, g_c)
    gt_c = _mm_f32(m_same, g_c)
    gc_r = _mm_f32(g_r, m_up)

    ii = lax.broadcasted_iota(jnp.int32, (GDN_BLOCK, GDN_BLOCK), 0)
    jj = lax.broadcasted_iota(jnp.int32, (GDN_BLOCK, GDN_BLOCK), 1)
    lower = jj <= ii
    strict = jj < ii
    eye = jnp.where(ii == jj, 1.0, 0.0).astype(F32)
    levels = []
    b = 1
    while b < GDN_BLOCK:
        levels.append(((ii // b) == (jj // b) + 1) & (((jj // b) % 2) == 0))
        b *= 2

    for j in range(nb):
        rows = slice(j * GDN_BLOCK, (j + 1) * GDN_BLOCK)
        for hh in range(GDN_HEADS):
            cols = slice(hh * HEAD_DIM, (hh + 1) * HEAD_DIM)
            q = q_s[rows, cols]
            k = k_s[rows, cols]
            v = v_s[rows, cols]
            beta = beta_c[rows, hh:hh + 1]
            gcc = gc_c[rows, GDN_HEADS + hh:GDN_HEADS + hh + 1]
            gtc = gt_c[rows, GDN_HEADS + hh:GDN_HEADS + hh + 1]
            gcr = gc_r[GDN_HEADS + hh:GDN_HEADS + hh + 1, rows]
            kb = k * beta
            vb = v * beta
            dm = jnp.where(lower, jnp.exp(gcc - gcr), 0.0)
            a = jnp.where(strict, _mm_nt(kb, k) * dm, 0.0)
            xinv = eye - jnp.where(levels[0], a, 0.0)
            for lm in levels[1:]:
                xinv = xinv - _mm(_mm(xinv, jnp.where(lm, a, 0.0)), xinv)
            eg = jnp.exp(gcc)
            sol = _mm(xinv, jnp.concatenate([vb, kb * eg], axis=1))
            u_s[rows, cols] = sol[:, :HEAD_DIM]
            w_s[rows, cols] = sol[:, HEAD_DIM:]
            qd_s[rows, cols] = q * eg
            kd_s[rows, cols] = k * jnp.exp(gtc - gcc)
            qk_s[rows, cols] = _mm_nt(q, k) * dm

    for j in range(nb):
        rows = slice(j * GDN_BLOCK, (j + 1) * GDN_BLOCK)
        for hh in range(GDN_HEADS):
            cols = slice(hh * HEAD_DIM, (hh + 1) * HEAD_DIM)
            state = st_ref[hh]
            proj = _mm(jnp.concatenate([w_s[rows, cols], qd_s[rows, cols]], axis=0), state)
            v_new = u_s[rows, cols] - proj[:GDN_BLOCK]
            o = proj[GDN_BLOCK:] + _mm(qk_s[rows, cols], v_new)
            g_last = jnp.exp(gt_c[j * GDN_BLOCK:j * GDN_BLOCK + 1, GDN_HEADS + hh:GDN_HEADS + hh + 1])
            st_ref[hh] = state * g_last + _mm_tn(kd_s[rows, cols], v_new)
            zz = z_ref[rows, cols]
            on = o * lax.rsqrt(jnp.mean(o * o, axis=-1, keepdims=True) + RMS_EPS) * nw_ref[...]
            o_ref[rows, cols] = (on * _silu(zz)).astype(o_ref.dtype)


def _gdn(qkv, z, bdc, bdr, conv_w, a_log, dt_bias, norm_w, bsz, seq):
    t = qkv.shape[0]
    ts = GDN_TILE
    gw = GDN_HEADS * HEAD_DIM
    spb = seq // ts
    zeros4 = jnp.zeros((GDN_HEADS,), F32)
    al = jnp.concatenate([zeros4, a_log])
    db = jnp.concatenate([zeros4, dt_bias])
    gpc = jnp.zeros((2, LANES), F32).at[0, :2 * GDN_HEADS].set(al).at[1, :2 * GDN_HEADS].set(db)
    gpr = jnp.stack([al, db], axis=1)
    row = lambda b, i: (b * spb + i, 0)
    const = lambda b, i: (0, 0)
    tile = pltpu.VMEM((ts, gw), F32)
    return pl.pallas_call(
        _gdn_kernel,
        out_shape=jax.ShapeDtypeStruct((t, gw), BF16),
        grid=(bsz, spb),
        in_specs=[pl.BlockSpec((ts, 3 * gw), row), pl.BlockSpec((ts, gw), row),
                  pl.BlockSpec((ts, LANES), row),
                  pl.BlockSpec((SUBLANES, ts), lambda b, i: (0, b * spb + i)),
                  pl.BlockSpec((CONV_WIDTH, 3 * gw), const),
                  pl.BlockSpec((2, LANES), const), pl.BlockSpec((2 * GDN_HEADS, 2), const),
                  pl.BlockSpec((1, HEAD_DIM), const)],
        out_specs=pl.BlockSpec((ts, gw), row),
        scratch_shapes=[pltpu.VMEM((ts + 2 * SUBLANES, 3 * gw), F32),
                        tile, tile, tile, tile, tile, tile, tile, tile,
                        pltpu.VMEM((GDN_HEADS, HEAD_DIM, HEAD_DIM), F32)],
        compiler_params=_params(("parallel", "arbitrary")),
        name="gdn",
    )(qkv, z, bdc, bdr, conv_w, gpc, gpr, norm_w.reshape(1, HEAD_DIM))


def _attn_kernel(q_ref, kp_ref, kc_ref, vp_ref, vc_ref, nw_ref, o_ref, m_s, l_s, acc_s):
    tq = q_ref.shape[0]
    blk = ATT_BLOCK
    qi = lax.broadcasted_iota(jnp.int32, (blk, 2 * blk), 0)
    kj = lax.broadcasted_iota(jnp.int32, (blk, 2 * blk), 1)
    band = (kj >= qi) & (kj <= qi + blk)
    first_lo = jnp.where(pl.program_id(2) > 0, 0, blk)

    def block_stats(q, kprev, kcur, vprev, vcur, from_prev_tile):
        s = _mm_nt(q, jnp.concatenate([kprev, kcur], axis=0))
        mask = band & (kj >= first_lo) if from_prev_tile else band
        s = jnp.where(mask, s, NEG)
        m = jnp.max(s, axis=-1, keepdims=True)
        p = jnp.exp(s - m)
        l = jnp.sum(p, axis=-1, keepdims=True)
        o = _mm(p, jnp.concatenate([vprev, vcur], axis=0))
        return m, l, o

    for window, dil in DILATED_PATTERNS:
        span_tokens = blk * dil
        for jb in range(tq // span_tokens):
            for r in range(dil):
                base = jb * span_tokens + r
                cur = pl.ds(base, blk, stride=dil) if dil > 1 else pl.ds(base, blk)
                if jb > 0:
                    pbase = base - span_tokens
                    prev = pl.ds(pbase, blk, stride=dil) if dil > 1 else pl.ds(pbase, blk)
                    kprev, vprev = kc_ref[prev, :], vc_ref[prev, :]
                else:
                    pbase = tq - span_tokens + r
                    prev = pl.ds(pbase, blk, stride=dil) if dil > 1 else pl.ds(pbase, blk)
                    kprev, vprev = kp_ref[prev, :], vp_ref[prev, :]
                m, l, o = block_stats(q_ref[cur, :], kprev, kc_ref[cur, :], vprev, vc_ref[cur, :],
                                      jb == 0)
                if dil == 1:
                    m_s[cur, :] = jnp.broadcast_to(m, (blk, HEAD_DIM))
                    l_s[cur, :] = jnp.broadcast_to(l, (blk, HEAD_DIM))
                    acc_s[cur, :] = o
                else:
                    m_old = m_s[cur, :]
                    m_new = jnp.maximum(m_old, m)
                    w_old = jnp.exp(m_old - m_new)
                    w_cur = jnp.exp(m - m_new)
                    m_s[cur, :] = m_new
                    l_s[cur, :] = w_old * l_s[cur, :] + w_cur * l
                    acc_s[cur, :] = w_old * acc_s[cur, :] + w_cur * o

    out = acc_s[...] / l_s[...]
    out = out * lax.rsqrt(jnp.mean(out * out, axis=-1, keepdims=True) + RMS_EPS) * nw_ref[...]
    o_ref[...] = out.astype(o_ref.dtype)


def _attn(qb, kb, vb, norm_w, bsz, seq):
    t = qb.shape[0]
    tq = ATT_TILE
    spb = seq // tq
    cur = lambda b, h, i: (b * spb + i, h)
    prev = lambda b, h, i: (b * spb + jnp.maximum(i - 1, 0), h)
    blk = lambda f: pl.BlockSpec((tq, HEAD_DIM), f)
    acc = pltpu.VMEM((tq, HEAD_DIM), F32)
    return pl.pallas_call(
        _attn_kernel,
        out_shape=jax.ShapeDtypeStruct((t, ATT_HEADS * HEAD_DIM), BF16),
        grid=(bsz, ATT_HEADS, spb),
        in_specs=[blk(cur), blk(prev), blk(cur), blk(prev), blk(cur),
                  pl.BlockSpec((1, HEAD_DIM), lambda b, h, i: (0, 0))],
        out_specs=blk(cur),
        scratch_shapes=[acc, acc, acc],
        compiler_params=_params(("parallel", "parallel", "arbitrary")),
        name="attn",
    )(qb, kb, kb, vb, vb, norm_w.reshape(1, HEAD_DIM))


def _layer_norm(y, g, b):
    mu = jnp.mean(y, axis=-1, keepdims=True)
    yc = y - mu
    var = jnp.mean(yc * yc, axis=-1, keepdims=True)
    return yc * lax.rsqrt(var + LN_EPS) * g + b


def _outproj_kernel(oa_ref, ob_ref, x_ref, mod_ref, woa_ref, wob_ref, g_ref, b_ref,
                    wrh_ref, wrl_ref, br_ref,
                    x1_ref, h2_ref, ri_ref, rg_ref, cnt_ref, run_s):
    @pl.when(pl.program_id(0) == 0)
    def _():
        run_s[...] = jnp.zeros(run_s.shape, F32)

    gate1 = mod_ref[0, 2:3, :]
    shift2 = mod_ref[0, 3:4, :]
    scale2 = mod_ref[0, 4:5, :]
    mix = (jnp.dot(oa_ref[...], woa_ref[...], preferred_element_type=F32)
           + jnp.dot(ob_ref[...], wob_ref[...], preferred_element_type=F32))
    x1 = _layer_norm(DEEPNORM_ALPHA * x_ref[...] + (1.0 + gate1) * mix, g_ref[...], b_ref[...])
    x1_ref[...] = x1
    h2 = x1 * (1.0 + scale2) + shift2
    h2_ref[...] = h2

    hi = _bf(h2)
    lo = _bf(h2 - hi.astype(F32))
    logits = (jnp.dot(hi, wrh_ref[...], preferred_element_type=F32)
              + jnp.dot(hi, wrl_ref[...], preferred_element_type=F32)
              + jnp.dot(lo, wrh_ref[...], preferred_element_type=F32)) + br_ref[...]
    tm = logits.shape[0]
    lane = lax.broadcasted_iota(jnp.int32, (tm, LANES), 1)
    lg = jnp.where(lane < N_GROUPS, logits, NEG)
    mg = jnp.max(lg, axis=-1, keepdims=True)
    grp = jnp.min(jnp.where(lg == mg, lane, LANES), axis=-1, keepdims=True)
    gate_grp = 1.0 / jnp.sum(jnp.exp(lg - mg), axis=-1, keepdims=True)
    eidx = lane - N_GROUPS
    sel = (eidx >= 0) & (eidx < N_EXPERTS) & ((eidx // EXPERTS_PER_GROUP) == grp)
    le = jnp.where(sel, logits, NEG)
    v1 = jnp.max(le, axis=-1, keepdims=True)
    i1 = jnp.min(jnp.where(le == v1, lane, LANES), axis=-1, keepdims=True)
    le2 = jnp.where(lane == i1, NEG, le)
    v2 = jnp.max(le2, axis=-1, keepdims=True)
    i2 = jnp.min(jnp.where(le2 == v2, lane, LANES), axis=-1, keepdims=True)
    e21 = jnp.exp(v2 - v1)
    g1 = gate_grp / (1.0 + e21)
    g2 = gate_grp * e21 / (1.0 + e21)

    oh1 = lane == i1
    oh2 = lane == i2
    onehot = jnp.where(oh1 | oh2, 1.0, 0.0).astype(F32)
    ti = lax.broadcasted_iota(jnp.int32, (tm, tm), 0)
    tj = lax.broadcasted_iota(jnp.int32, (tm, tm), 1)
    before = jnp.where(tj < ti, 1.0, 0.0).astype(F32)
    tot = _mm(before, onehot) + run_s[...]
    r1 = jnp.sum(jnp.where(oh1, tot, 0.0), axis=-1, keepdims=True)
    r2 = jnp.sum(jnp.where(oh2, tot, 0.0), axis=-1, keepdims=True)
    run_s[...] = run_s[...] + jnp.sum(onehot, axis=0, keepdims=True)
    cnt_ref[...] = run_s[...]

    ri = jnp.where(lane == 0, i1 - N_GROUPS, 0)
    ri = jnp.where(lane == 1, i2 - N_GROUPS, ri)
    ri = jnp.where(lane == 2, r1.astype(jnp.int32), ri)
    ri = jnp.where(lane == 3, r2.astype(jnp.int32), ri)
    ri_ref[...] = ri
    rg_ref[...] = jnp.where(lane == 0, g1, jnp.where(lane == 1, g2, 0.0))


def _outproj(oa, ob, xf, mod, w_o, ln_g, ln_b, w_rg, b_rg, w_re, b_re, seq):
    t, d = xf.shape
    tm = ROW_TILE
    gw = oa.shape[1]
    wo = _bf(w_o)
    wr = jnp.zeros((d, LANES), F32).at[:, :N_GROUPS].set(w_rg).at[:, N_GROUPS:N_GROUPS + N_EXPERTS].set(w_re)
    wrh = _bf(wr)
    wrl = _bf(wr - wrh.astype(F32))
    br = jnp.zeros((1, LANES), F32).at[0, :N_GROUPS].set(b_rg).at[0, N_GROUPS:N_GROUPS + N_EXPERTS].set(b_re)
    spb = seq // tm
    row = lambda i: (i, 0)
    const = lambda i: (0, 0)
    return pl.pallas_call(
        _outproj_kernel,
        out_shape=(jax.ShapeDtypeStruct((t, d), F32), jax.ShapeDtypeStruct((t, d), F32),
                   jax.ShapeDtypeStruct((t, LANES), jnp.int32), jax.ShapeDtypeStruct((t, LANES), F32),
                   jax.ShapeDtypeStruct((1, LANES), F32)),
        grid=(t // tm,),
        in_specs=[pl.BlockSpec((tm, gw), row), pl.BlockSpec((tm, gw), row), pl.BlockSpec((tm, d), row),
                  pl.BlockSpec((1, 6, d), lambda i: (i // spb, 0, 0)),
                  pl.BlockSpec((gw, d), const), pl.BlockSpec((gw, d), const),
                  pl.BlockSpec((1, d), const), pl.BlockSpec((1, d), const),
                  pl.BlockSpec((d, LANES), const), pl.BlockSpec((d, LANES), const),
                  pl.BlockSpec((1, LANES), const)],
        out_specs=(pl.BlockSpec((tm, d), row), pl.BlockSpec((tm, d), row),
                   pl.BlockSpec((tm, LANES), row), pl.BlockSpec((tm, LANES), row),
                   pl.BlockSpec((1, LANES), const)),
        scratch_shapes=[pltpu.VMEM((1, LANES), F32)],
        compiler_params=_params(("arbitrary",)),
        name="outproj",
    )(oa, ob, xf, mod, wo[:gw], wo[gw:], ln_g.reshape(1, d), ln_b.reshape(1, d), wrh, wrl, br)


def _dispatch_kernel(e1_ref, e2_ref, r1_ref, r2_ref, ps_ref, h_ref, xs_ref, sem):
    tm = h_ref.shape[0]
    i = pl.program_id(0)

    def row_copy(tk, dest):
        return pltpu.make_async_copy(h_ref.at[pl.ds(tk, 1), :], xs_ref.at[pl.ds(dest, 1), :], sem)

    def issue(tk, carry):
        tok = i * tm + tk
        row_copy(tk, ps_ref[e1_ref[tok]] + r1_ref[tok]).start()
        row_copy(tk, ps_ref[e2_ref[tok]] + r2_ref[tok]).start()
        return carry

    lax.fori_loop(0, tm, issue, 0)

    def drain(tk, carry):
        row_copy(0, 0).wait()
        row_copy(0, 0).wait()
        return carry

    lax.fori_loop(0, tm, drain, 0)


def _dispatch(h2, e1, e2, r1, r2, seg_start):
    t, d = h2.shape
    tm = ROW_TILE
    return pl.pallas_call(
        _dispatch_kernel,
        out_shape=jax.ShapeDtypeStruct((2 * t, d), F32),
        grid_spec=pltpu.PrefetchScalarGridSpec(
            num_scalar_prefetch=5,
            grid=(t // tm,),
            in_specs=[pl.BlockSpec((tm, d), lambda i, *_: (i, 0))],
            out_specs=pl.BlockSpec(memory_space=pl.ANY),
            scratch_shapes=[pltpu.SemaphoreType.DMA]),
        compiler_params=_params(("arbitrary",)),
        name="dispatch",
    )(e1, e2, r1, r2, seg_start, h2)


def _experts_kernel(wb_ref, we_ref, lo_ref, hi_ref, nw_ref, xs_ref, wg_ref, wu_ref, wd_ref, ys_ref):
    w = pl.program_id(0)

    @pl.when(w < nw_ref[0])
    def _():
        x = _bf(xs_ref[...])
        hid = (_silu(jnp.dot(x, wg_ref[0], preferred_element_type=F32))
               * jnp.dot(x, wu_ref[0], preferred_element_type=F32))
        y = jnp.dot(_bf(hid), wd_ref[0], preferred_element_type=F32)
        row = lax.broadcasted_iota(jnp.int32, (y.shape[0], 1), 0)
        mine = (row >= lo_ref[w]) & (row < hi_ref[w])

        @pl.when(lo_ref[w] == 0)
        def _():
            ys_ref[...] = jnp.where(mine, y, 0.0)

        @pl.when(lo_ref[w] > 0)
        def _():
            ys_ref[...] = jnp.where(mine, y, ys_ref[...])


def _experts(xs, item_block, item_expert, item_lo, item_hi, n_items, w_gate, w_up, w_down):
    n_slots, d = xs.shape
    ff = w_gate.shape[2]
    bm = EXPERT_BLOCK
    slot = lambda w, wb, we, lo, hi, nw: (wb[w], 0)
    wsel = lambda w, wb, we, lo, hi, nw: (we[w], 0, 0)
    return pl.pallas_call(
        _experts_kernel,
        out_shape=jax.ShapeDtypeStruct((n_slots, d), F32),
        grid_spec=pltpu.PrefetchScalarGridSpec(
            num_scalar_prefetch=5,
            grid=(item_block.shape[0],),
            in_specs=[pl.BlockSpec((bm, d), slot),
                      pl.BlockSpec((1, d, ff), wsel), pl.BlockSpec((1, d, ff), wsel),
                      pl.BlockSpec((1, ff, d), wsel)],
            out_specs=pl.BlockSpec((bm, d), slot)),
        compiler_params=_params(("arbitrary",)),
        name="experts",
    )(item_block, item_expert, item_lo, item_hi, n_items, xs, _bf(w_gate), _bf(w_up), _bf(w_down))


def _combine_kernel(e1_ref, e2_ref, r1_ref, r2_ref, ps_ref, ys_ref, rg_ref, x1_ref, mod_ref,
                    g_ref, b_ref, o_ref, ya, yb, sem):
    tm = x1_ref.shape[0]
    i = pl.program_id(0)

    def row_copy(dest, buf, tk):
        return pltpu.make_async_copy(ys_ref.at[pl.ds(dest, 1), :], buf.at[pl.ds(tk, 1), :], sem)

    def issue(tk, carry):
        tok = i * tm + tk
        row_copy(ps_ref[e1_ref[tok]] + r1_ref[tok], ya, tk).start()
        row_copy(ps_ref[e2_ref[tok]] + r2_ref[tok], yb, tk).start()
        return carry

    lax.fori_loop(0, tm, issue, 0)

    def drain(tk, carry):
        row_copy(0, ya, 0).wait()
        row_copy(0, yb, 0).wait()
        return carry

    lax.fori_loop(0, tm, drain, 0)

    gate2 = mod_ref[0, 5:6, :]
    rg = rg_ref[...]
    y = rg[:, 0:1] * ya[...] + rg[:, 1:2] * yb[...]
    o_ref[...] = _layer_norm(DEEPNORM_ALPHA * x1_ref[...] + (1.0 + gate2) * y, g_ref[...], b_ref[...])


def _combine(ys, e1, e2, r1, r2, pad_start, rg, x1, mod, ln_g, ln_b, seq):
    t, d = x1.shape
    tm = ROW_TILE
    spb = seq // tm
    row = lambda i, *_: (i, 0)
    const = lambda i, *_: (0, 0)
    buf = pltpu.VMEM((tm, d), F32)
    return pl.pallas_call(
        _combine_kernel,
        out_shape=jax.ShapeDtypeStruct((t, d), F32),
        grid_spec=pltpu.PrefetchScalarGridSpec(
            num_scalar_prefetch=5,
            grid=(t // tm,),
            in_specs=[pl.BlockSpec(memory_space=pl.ANY),
                      pl.BlockSpec((tm, LANES), row), pl.BlockSpec((tm, d), row),
                      pl.BlockSpec((1, 6, d), lambda i, *_: (i // spb, 0, 0)),
                      pl.BlockSpec((1, d), const), pl.BlockSpec((1, d), const)],
            out_specs=pl.BlockSpec((tm, d), row),
            scratch_shapes=[buf, buf, pltpu.SemaphoreType.DMA]),
        compiler_params=_params(("arbitrary",)),
        name="combine",
    )(e1, e2, r1, r2, pad_start, ys, rg, x1, mod, ln_g.reshape(1, d), ln_b.reshape(1, d))


def _layer(x, c, positions, w_ada, b_ada, w_in, conv_w, a_log, dt_bias, gdn_norm_w, attn_norm_w,
           w_o, ln1_g, ln1_b, w_rg, b_rg, w_re, b_re, w_gate, w_up, w_down, ln2_g, ln2_b):
    bsz, seq, d = x.shape
    t = bsz * seq
    xf = x.reshape(t, d)
    mod = _ada(c, w_ada, b_ada)
    qkv, z, bdc, bdr, qb, kb, vb = _inproj(xf, mod, positions.reshape(t, 1), w_in, seq)
    oa = _gdn(qkv, z, bdc, bdr, conv_w, a_log, dt_bias, gdn_norm_w, bsz, seq)
    ob = _attn(qb, kb, vb, attn_norm_w, bsz, seq)
    x1, h2, ri, rg, cnt = _outproj(oa, ob, xf, mod, w_o, ln1_g, ln1_b, w_rg, b_rg, w_re, b_re, seq)

    bm = EXPERT_BLOCK
    counts = cnt[0, N_GROUPS:N_GROUPS + N_EXPERTS].astype(jnp.int32)
    seg_end = jnp.cumsum(counts)
    seg_start = seg_end - counts
    first_blk = seg_start // bm
    n_per = jnp.where(counts > 0, (seg_end - 1) // bm - first_blk + 1, 0)
    item_end = jnp.cumsum(n_per)
    n_items = item_end[-1:]
    max_items = (2 * t) // bm + N_EXPERTS - 1
    w_idx = jnp.minimum(jnp.arange(max_items, dtype=jnp.int32), n_items[0] - 1)
    item_expert = jnp.minimum(jnp.searchsorted(item_end, w_idx, side="right"), N_EXPERTS - 1).astype(jnp.int32)
    item_block = first_blk[item_expert] + w_idx - (item_end - n_per)[item_expert]
    item_lo = jnp.maximum(seg_start[item_expert] - item_block * bm, 0)
    item_hi = jnp.minimum(seg_end[item_expert] - item_block * bm, bm)
    e1, e2, r1, r2 = ri[:, 0], ri[:, 1], ri[:, 2], ri[:, 3]

    xs = _dispatch(h2, e1, e2, r1, r2, seg_start)
    ys = _experts(xs, item_block, item_expert, item_lo, item_hi, n_items, w_gate, w_up, w_down)
    out = _combine(ys, e1, e2, r1, r2, seg_start, rg, x1, mod, ln2_g, ln2_b, seq)
    return out.reshape(bsz, seq, d)


def kernel(x, c, positions, w_ada, b_ada, w_in, conv_w, a_log, dt_bias, gdn_norm_w, attn_norm_w, w_o, ln1_g, ln1_b, w_router_group, b_router_group, w_router_expert, b_router_expert, w_gate, w_up, w_down, ln2_g, ln2_b):
    assert w_ada.shape[0] == DEPTH
    return _layer(x, c, positions, w_ada[0], b_ada[0], w_in[0], conv_w[0], a_log[0], dt_bias[0],
                  gdn_norm_w[0], attn_norm_w[0], w_o[0], ln1_g[0], ln1_b[0],
                  w_router_group[0], b_router_group[0], w_router_expert[0], b_router_expert[0],
                  w_gate[0], w_up[0], w_down[0], ln2_g[0], ln2_b[0])
```

```python
import functools
import math

import jax
import jax.numpy as jnp
from jax import lax
from jax.experimental import pallas as pl
from jax.experimental.pallas import tpu as pltpu

F32 = jnp.float32
BF16 = jnp.bfloat16
HIGHEST = lax.Precision.HIGHEST

GDN_HEADS = 4
ATT_HEADS = 4
HEAD_DIM = 128
CONV_WIDTH = 4
DILATED_PATTERNS = ((128, 1), (512, 4), (2048, 16))
ROPE_THETA = 500000.0
ROPE_DIMS = HEAD_DIM // 4
N_GROUPS = 4
EXPERTS_PER_GROUP = 8
N_EXPERTS = N_GROUPS * EXPERTS_PER_GROUP
DEPTH = 1
DEEPNORM_ALPHA = (2.0 * DEPTH) ** 0.25
LN_EPS = 1e-5
RMS_EPS = 1e-6

LANES = 128
SUBLANES = 8
VMEM_LIMIT = 48 * 1024 * 1024

GDN_BLOCK = 128
GDN_TILE = 256
ATT_BLOCK = 128
ATT_TILE = 2048
ROW_TILE = 256
EXPERT_BLOCK = 256
NEG = -1e30


def _bf(x):
    return x.astype(BF16)


def _mm(a, b):
    return jnp.dot(_bf(a), _bf(b), preferred_element_type=F32)


def _mm_nt(a, b):
    return lax.dot_general(_bf(a), _bf(b), (((1,), (1,)), ((), ())), preferred_element_type=F32)


def _mm_tn(a, b):
    return lax.dot_general(_bf(a), _bf(b), (((0,), (0,)), ((), ())), preferred_element_type=F32)


def _mm_f32(a, b):
    return jnp.dot(a, b, preferred_element_type=F32, precision=HIGHEST)


def _sigmoid(x):
    return 1.0 / (1.0 + jnp.exp(-x))


def _silu(x):
    return x * _sigmoid(x)


def _softplus(x):
    return jnp.maximum(x, 0.0) + jnp.log(1.0 + jnp.exp(-jnp.abs(x)))


def _params(sem):
    return pltpu.CompilerParams(dimension_semantics=sem, vmem_limit_bytes=VMEM_LIMIT)


def _ada_kernel(c_ref, w_ref, b_ref, o_ref):
    o_ref[...] = _mm_f32(_silu(c_ref[...]), w_ref[...]) + b_ref[...]


def _ada(c, w_ada, b_ada):
    bsz, d = c.shape
    n = w_ada.shape[1]
    tn = 512
    cp = jnp.zeros((SUBLANES, d), F32).at[:bsz].set(c)
    out = pl.pallas_call(
        _ada_kernel,
        out_shape=jax.ShapeDtypeStruct((SUBLANES, n), F32),
        grid=(n // tn,),
        in_specs=[pl.BlockSpec((SUBLANES, d), lambda j: (0, 0)),
                  pl.BlockSpec((d, tn), lambda j: (0, j)),
                  pl.BlockSpec((1, tn), lambda j: (0, j))],
        out_specs=pl.BlockSpec((SUBLANES, tn), lambda j: (0, j)),
        compiler_params=_params(("parallel",)),
        name="ada",
    )(cp, w_ada, b_ada.reshape(1, n))
    return out[:bsz].reshape(bsz, 6, d)


def _inproj_kernel(x_ref, mod_ref, pos_ref, invf_ref, wqkv_ref, wz_ref, wbd_ref, wbdt_ref,
                   wq_ref, wk_ref, wv_ref,
                   qkv_ref, z_ref, bdc_ref, bdr_ref, qb_ref, kb_ref, vb_ref):
    shift = mod_ref[0, 0:1, :]
    scale = mod_ref[0, 1:2, :]
    h = _bf(x_ref[...] * (1.0 + scale) + shift)
    qkv_ref[...] = jnp.dot(h, wqkv_ref[...], preferred_element_type=F32)
    z_ref[...] = jnp.dot(h, wz_ref[...], preferred_element_type=F32)
    bdc_ref[...] = jnp.dot(h, wbd_ref[...], preferred_element_type=F32)
    bdr_ref[0] = lax.dot_general(wbdt_ref[...], h, (((1,), (1,)), ((), ())),
                                 preferred_element_type=F32)
    ang = pos_ref[...].astype(F32) * invf_ref[...]
    cosv = jnp.cos(ang)
    sinv = jnp.sin(ang)
    lane = lax.broadcasted_iota(jnp.int32, (1, LANES), 1)
    half = ROPE_DIMS // 2
    first = lane < half
    sin_signed = jnp.where(first, -sinv, sinv)

    def rope(y):
        outs = []
        for hh in range(ATT_HEADS):
            yh = y[:, hh * HEAD_DIM:(hh + 1) * HEAD_DIM]
            partner = jnp.where(first, pltpu.roll(yh, LANES - half, 1), pltpu.roll(yh, half, 1))
            outs.append(yh * cosv + partner * sin_signed)
        return jnp.concatenate(outs, axis=1)

    qb_ref[...] = rope(jnp.dot(h, wq_ref[...], preferred_element_type=F32)) * (HEAD_DIM ** -0.5)
    kb_ref[...] = rope(jnp.dot(h, wk_ref[...], preferred_element_type=F32))
    vb_ref[...] = jnp.dot(h, wv_ref[...], preferred_element_type=F32)


def _inproj(xf, mod, pos, w_in, seq):
    t, d = xf.shape
    tm = ROW_TILE
    gw = GDN_HEADS * HEAD_DIM
    aw = ATT_HEADS * HEAD_DIM
    o0 = 3 * gw
    o1 = o0 + gw
    o2 = o1 + 2 * GDN_HEADS
    wb = _bf(w_in)
    wqkv, wz = wb[:, :o0], wb[:, o0:o1]
    wbd_n = wb[:, o1:o2]
    wbd = jnp.zeros((d, LANES), BF16).at[:, :2 * GDN_HEADS].set(wbd_n)
    wbdt = wbd_n.T
    wq, wk, wv = wb[:, o2:o2 + aw], wb[:, o2 + aw:o2 + 2 * aw], wb[:, o2 + 2 * aw:o2 + 3 * aw]
    half = ROPE_DIMS // 2
    inv_freq = ROPE_THETA ** (-jnp.arange(half, dtype=F32) * 2.0 / ROPE_DIMS)
    invf = jnp.zeros((1, LANES), F32).at[0, :ROPE_DIMS].set(jnp.concatenate([inv_freq, inv_freq]))
    spb = seq // tm
    row = lambda i: (i, 0)
    const = lambda i: (0, 0)
    return pl.pallas_call(
        _inproj_kernel,
        out_shape=(jax.ShapeDtypeStruct((t, o0), F32), jax.ShapeDtypeStruct((t, gw), F32),
                   jax.ShapeDtypeStruct((t, LANES), F32), jax.ShapeDtypeStruct((t // seq, SUBLANES, seq), F32),
                   jax.ShapeDtypeStruct((t, aw), F32), jax.ShapeDtypeStruct((t, aw), F32),
                   jax.ShapeDtypeStruct((t, aw), F32)),
        grid=(t // tm,),
        in_specs=[pl.BlockSpec((tm, d), row),
                  pl.BlockSpec((1, 6, d), lambda i: (i // spb, 0, 0)),
                  pl.BlockSpec((tm, 1), row),
                  pl.BlockSpec((1, LANES), const),
                  pl.BlockSpec((d, o0), const), pl.BlockSpec((d, gw), const),
                  pl.BlockSpec((d, LANES), const), pl.BlockSpec((2 * GDN_HEADS, d), const),
                  pl.BlockSpec((d, aw), const), pl.BlockSpec((d, aw), const),
                  pl.BlockSpec((d, aw), const)],
        out_specs=(pl.BlockSpec((tm, o0), row), pl.BlockSpec((tm, gw), row),
                   pl.BlockSpec((tm, LANES), row),
                   pl.BlockSpec((1, SUBLANES, tm), lambda i: (i // spb, 0, i % spb)),
                   pl.BlockSpec((tm, aw), row), pl.BlockSpec((tm, aw), row),
                   pl.BlockSpec((tm, aw), row)),
        compiler_params=_params(("parallel",)),
        name="inproj",
    )(xf, mod, pos, invf, wqkv, wz, wbd, wbdt, wq, wk, wv)


def _gdn_kernel(qkv_ref, z_ref, bdc_ref, bdr_ref, convw_ref, gpc_ref, gpr_ref, nw_ref, o_ref,
                cbuf, q_s, k_s, v_s, u_s, w_s, qd_s, kd_s, qk_s, a_s, x_s, y_s, st_ref):
    nbat, ts = qkv_ref.shape[0], qkv_ref.shape[1]
    nb = ts // GDN_BLOCK
    halo = SUBLANES

    @pl.when(pl.program_id(0) == 0)
    def _():
        cbuf[:, 0:halo, :] = jnp.zeros((nbat, halo, cbuf.shape[2]), F32)
        st_ref[...] = jnp.zeros(st_ref.shape, F32)

    ti = lax.broadcasted_iota(jnp.int32, (ts, ts), 0)
    tj = lax.broadcasted_iota(jnp.int32, (ts, ts), 1)
    same = (ti // GDN_BLOCK) == (tj // GDN_BLOCK)
    m_low = jnp.where(same & (tj <= ti), 1.0, 0.0).astype(F32)
    m_up = jnp.where(same & (ti <= tj), 1.0, 0.0).astype(F32)
    m_same = jnp.where(same, 1.0, 0.0).astype(F32)

    beta_c, gc_c, gt_c, gc_r = [], [], [], []
    for bb in range(nbat):
        cbuf[bb, halo:halo + ts, :] = qkv_ref[bb]
        for s in range(3 * GDN_HEADS):
            sl = slice(s * HEAD_DIM, (s + 1) * HEAD_DIM)
            off = halo - (CONV_WIDTH - 1)
            acc = convw_ref[0:1, sl] * cbuf[bb, off:off + ts, sl]
            for j in range(1, CONV_WIDTH):
                acc = acc + convw_ref[j:j + 1, sl] * cbuf[bb, off + j:off + j + ts, sl]
            y = _silu(acc)
            if s < 2 * GDN_HEADS:
                y = y * lax.rsqrt(jnp.sum(y * y, axis=-1, keepdims=True) + RMS_EPS)
            if s < GDN_HEADS:
                y = y * (HEAD_DIM ** -0.5)
            dst = (q_s, k_s, v_s)[s // GDN_HEADS]
            hs = s % GDN_HEADS
            dst[bb, :, hs * HEAD_DIM:(hs + 1) * HEAD_DIM] = y
        cbuf[bb, 0:halo, :] = cbuf[bb, ts:ts + halo, :]

        bdc = bdc_ref[bb]
        beta_c.append(_sigmoid(bdc))
        g_c = -jnp.exp(gpc_ref[0:1, :]) * _softplus(bdc + gpc_ref[1:2, :])
        g_r = -jnp.exp(gpr_ref[:, 0:1]) * _softplus(bdr_ref[bb] + gpr_ref[:, 1:2])
        gc_c.append(_mm_f32(m_low, g_c))
        gt_c.append(_mm_f32(m_same, g_c))
        gc_r.append(_mm_f32(g_r, m_up))

    ii = lax.broadcasted_iota(jnp.int32, (GDN_BLOCK, GDN_BLOCK), 0)
    jj = lax.broadcasted_iota(jnp.int32, (GDN_BLOCK, GDN_BLOCK), 1)
    lower = jj <= ii
    strict = jj < ii
    eye = jnp.where(ii == jj, 1.0, 0.0).astype(F32)
    levels = []
    b = 1
    while b < GDN_BLOCK:
        levels.append(((ii // b) == (jj // b) + 1) & (((jj // b) % 2) == 0))
        b *= 2

    chains = [(bb, j, hh) for bb in range(nbat) for j in range(nb) for hh in range(GDN_HEADS)]

    def tile_of(bb, j, hh):
        return bb, slice(j * GDN_BLOCK, (j + 1) * GDN_BLOCK), slice(hh * HEAD_DIM, (hh + 1) * HEAD_DIM)

    for c, (bb, j, hh) in enumerate(chains):
        blk = tile_of(bb, j, hh)
        rows = blk[1]
        q = q_s[blk]
        k = k_s[blk]
        beta = beta_c[bb][rows, hh:hh + 1]
        gcc = gc_c[bb][rows, GDN_HEADS + hh:GDN_HEADS + hh + 1]
        gtc = gt_c[bb][rows, GDN_HEADS + hh:GDN_HEADS + hh + 1]
        gcr = gc_r[bb][GDN_HEADS + hh:GDN_HEADS + hh + 1, rows]
        kb = k * beta
        eg = jnp.exp(gcc)
        dm = jnp.where(lower, jnp.exp(gcc - gcr), 0.0)
        a = jnp.where(strict, _mm_nt(kb, k) * dm, 0.0)
        a_s[c] = a
        x_s[c] = eye - jnp.where(levels[0], a, 0.0)
        u_s[blk] = v_s[blk] * beta
        w_s[blk] = kb * eg
        qd_s[blk] = q * eg
        kd_s[blk] = k * jnp.exp(gtc - gcc)
        qk_s[blk] = _mm_nt(q, k) * dm

    for lm in levels[1:]:
        for c in range(len(chains)):
            y_s[c] = _mm(x_s[c], jnp.where(lm, a_s[c], 0.0))
        for c in range(len(chains)):
            xc = x_s[c]
            x_s[c] = xc - _mm(y_s[c], xc)

    for c, (bb, j, hh) in enumerate(chains):
        blk = tile_of(bb, j, hh)
        sol = _mm(x_s[c], jnp.concatenate([u_s[blk], w_s[blk]], axis=1))
        u_s[blk] = sol[:, :HEAD_DIM]
        w_s[blk] = sol[:, HEAD_DIM:]

    for j in range(nb):
        for bb in range(nbat):
            for hh in range(GDN_HEADS):
                blk = tile_of(bb, j, hh)
                si = bb * GDN_HEADS + hh
                state = st_ref[si]
                proj = _mm(jnp.concatenate([w_s[blk], qd_s[blk]], axis=0), state)
                v_new = u_s[blk] - proj[:GDN_BLOCK]
                o = proj[GDN_BLOCK:] + _mm(qk_s[blk], v_new)
                g_last = jnp.exp(gt_c[bb][j * GDN_BLOCK:j * GDN_BLOCK + 1,
                                          GDN_HEADS + hh:GDN_HEADS + hh + 1])
                st_ref[si] = state * g_last + _mm_tn(kd_s[blk], v_new)
                on = o * lax.rsqrt(jnp.mean(o * o, axis=-1, keepdims=True) + RMS_EPS) * nw_ref[...]
                o_ref[blk] = (on * _silu(z_ref[blk])).astype(o_ref.dtype)


def _gdn(qkv, z, bdc, bdr, conv_w, a_log, dt_bias, norm_w):
    bsz, seq, _ = qkv.shape
    ts = GDN_TILE
    gw = GDN_HEADS * HEAD_DIM
    zeros4 = jnp.zeros((GDN_HEADS,), F32)
    al = jnp.concatenate([zeros4, a_log])
    db = jnp.concatenate([zeros4, dt_bias])
    gpc = jnp.zeros((2, LANES), F32).at[0, :2 * GDN_HEADS].set(al).at[1, :2 * GDN_HEADS].set(db)
    gpr = jnp.stack([al, db], axis=1)
    row = lambda i: (0, i, 0)
    const = lambda i: (0, 0)
    n_chains = bsz * (ts // GDN_BLOCK) * GDN_HEADS
    tile = pltpu.VMEM((bsz, ts, gw), F32)
    mats = pltpu.VMEM((n_chains, GDN_BLOCK, GDN_BLOCK), F32)
    return pl.pallas_call(
        _gdn_kernel,
        out_shape=jax.ShapeDtypeStruct((bsz, seq, gw), BF16),
        grid=(seq // ts,),
        in_specs=[pl.BlockSpec((bsz, ts, 3 * gw), row), pl.BlockSpec((bsz, ts, gw), row),
                  pl.BlockSpec((bsz, ts, LANES), row),
                  pl.BlockSpec((bsz, SUBLANES, ts), lambda i: (0, 0, i)),
                  pl.BlockSpec((CONV_WIDTH, 3 * gw), const),
                  pl.BlockSpec((2, LANES), const), pl.BlockSpec((2 * GDN_HEADS, 2), const),
                  pl.BlockSpec((1, HEAD_DIM), const)],
        out_specs=pl.BlockSpec((bsz, ts, gw), row),
        scratch_shapes=[pltpu.VMEM((bsz, ts + 2 * SUBLANES, 3 * gw), F32),
                        tile, tile, tile, tile, tile, tile, tile, tile,
                        mats, mats, mats,
                        pltpu.VMEM((bsz * GDN_HEADS, HEAD_DIM, HEAD_DIM), F32)],
        compiler_params=_params(("arbitrary",)),
        name="gdn",
    )(qkv, z, bdc, bdr, conv_w, gpc, gpr, norm_w.reshape(1, HEAD_DIM))


def _attn_kernel(q_ref, kp_ref, kc_ref, vp_ref, vc_ref, nw_ref, o_ref, m_s, l_s, acc_s):
    tq = q_ref.shape[0]
    blk = ATT_BLOCK
    qi = lax.broadcasted_iota(jnp.int32, (blk, 2 * blk), 0)
    kj = lax.broadcasted_iota(jnp.int32, (blk, 2 * blk), 1)
    band = (kj >= qi) & (kj <= qi + blk)
    first_lo = jnp.where(pl.program_id(2) > 0, 0, blk)

    def block_stats(q, kprev, kcur, vprev, vcur, from_prev_tile):
        s = _mm_nt(q, jnp.concatenate([kprev, kcur], axis=0))
        mask = band & (kj >= first_lo) if from_prev_tile else band
        s = jnp.where(mask, s, NEG)
        m = jnp.max(s, axis=-1, keepdims=True)
        p = jnp.exp(s - m)
        l = jnp.sum(p, axis=-1, keepdims=True)
        o = _mm(p, jnp.concatenate([vprev, vcur], axis=0))
        return m, l, o

    for window, dil in DILATED_PATTERNS:
        span_tokens = blk * dil
        for jb in range(tq // span_tokens):
            for r in range(dil):
                base = jb * span_tokens + r
                cur = pl.ds(base, blk, stride=dil) if dil > 1 else pl.ds(base, blk)
                if jb > 0:
                    pbase = base - span_tokens
                    prev = pl.ds(pbase, blk, stride=dil) if dil > 1 else pl.ds(pbase, blk)
                    kprev, vprev = kc_ref[prev, :], vc_ref[prev, :]
                else:
                    pbase = tq - span_tokens + r
                    prev = pl.ds(pbase, blk, stride=dil) if dil > 1 else pl.ds(pbase, blk)
                    kprev, vprev = kp_ref[prev, :], vp_ref[prev, :]
                m, l, o = block_stats(q_ref[cur, :], kprev, kc_ref[cur, :], vprev, vc_ref[cur, :],
                                      jb == 0)
                if dil == 1:
                    m_s[cur, :] = jnp.broadcast_to(m, (blk, HEAD_DIM))
                    l_s[cur, :] = jnp.broadcast_to(l, (blk, HEAD_DIM))
                    acc_s[cur, :] = o
                else:
                    m_old = m_s[cur, :]
                    m_new = jnp.maximum(m_old, m)
                    w_old = jnp.exp(m_old - m_new)
                    w_cur = jnp.exp(m - m_new)
                    m_s[cur, :] = m_new
                    l_s[cur, :] = w_old * l_s[cur, :] + w_cur * l
                    acc_s[cur, :] = w_old * acc_s[cur, :] + w_cur * o

    out = acc_s[...] / l_s[...]
    out = out * lax.rsqrt(jnp.mean(out * out, axis=-1, keepdims=True) + RMS_EPS) * nw_ref[...]
    o_ref[...] = out.astype(o_ref.dtype)


def _attn(qb, kb, vb, norm_w, bsz, seq):
    t = qb.shape[0]
    tq = ATT_TILE
    spb = seq // tq
    cur = lambda b, h, i: (b * spb + i, h)
    prev = lambda b, h, i: (b * spb + jnp.maximum(i - 1, 0), h)
    blk = lambda f: pl.BlockSpec((tq, HEAD_DIM), f)
    acc = pltpu.VMEM((tq, HEAD_DIM), F32)
    return pl.pallas_call(
        _attn_kernel,
        out_shape=jax.ShapeDtypeStruct((t, ATT_HEADS * HEAD_DIM), BF16),
        grid=(bsz, ATT_HEADS, spb),
        in_specs=[blk(cur), blk(prev), blk(cur), blk(prev), blk(cur),
                  pl.BlockSpec((1, HEAD_DIM), lambda b, h, i: (0, 0))],
        out_specs=blk(cur),
        scratch_shapes=[acc, acc, acc],
        compiler_params=_params(("parallel", "parallel", "arbitrary")),
        name="attn",
    )(qb, kb, kb, vb, vb, norm_w.reshape(1, HEAD_DIM))


def _layer_norm(y, g, b):
    mu = jnp.mean(y, axis=-1, keepdims=True)
    yc = y - mu
    var = jnp.mean(yc * yc, axis=-1, keepdims=True)
    return yc * lax.rsqrt(var + LN_EPS) * g + b


def _outproj_kernel(oa_ref, ob_ref, x_ref, mod_ref, woa_ref, wob_ref, g_ref, b_ref,
                    wrh_ref, wrl_ref, br_ref,
                    x1_ref, h2_ref, ri_ref, rg_ref, cnt_ref, run_s):
    @pl.when(pl.program_id(0) == 0)
    def _():
        run_s[...] = jnp.zeros(run_s.shape, F32)

    gate1 = mod_ref[0, 2:3, :]
    shift2 = mod_ref[0, 3:4, :]
    scale2 = mod_ref[0, 4:5, :]
    mix = (jnp.dot(oa_ref[...], woa_ref[...], preferred_element_type=F32)
           + jnp.dot(ob_ref[...], wob_ref[...], preferred_element_type=F32))
    x1 = _layer_norm(DEEPNORM_ALPHA * x_ref[...] + (1.0 + gate1) * mix, g_ref[...], b_ref[...])
    x1_ref[...] = x1
    h2 = x1 * (1.0 + scale2) + shift2
    h2_ref[...] = h2

    hi = _bf(h2)
    lo = _bf(h2 - hi.astype(F32))
    logits = (jnp.dot(hi, wrh_ref[...], preferred_element_type=F32)
              + jnp.dot(hi, wrl_ref[...], preferred_element_type=F32)
              + jnp.dot(lo, wrh_ref[...], preferred_element_type=F32)) + br_ref[...]
    tm = logits.shape[0]
    lane = lax.broadcasted_iota(jnp.int32, (tm, LANES), 1)
    lg = jnp.where(lane < N_GROUPS, logits, NEG)
    mg = jnp.max(lg, axis=-1, keepdims=True)
    grp = jnp.min(jnp.where(lg == mg, lane, LANES), axis=-1, keepdims=True)
    gate_grp = 1.0 / jnp.sum(jnp.exp(lg - mg), axis=-1, keepdims=True)
    eidx = lane - N_GROUPS
    sel = (eidx >= 0) & (eidx < N_EXPERTS) & ((eidx // EXPERTS_PER_GROUP) == grp)
    le = jnp.where(sel, logits, NEG)
    v1 = jnp.max(le, axis=-1, keepdims=True)
    i1 = jnp.min(jnp.where(le == v1, lane, LANES), axis=-1, keepdims=True)
    le2 = jnp.where(lane == i1, NEG, le)
    v2 = jnp.max(le2, axis=-1, keepdims=True)
    i2 = jnp.min(jnp.where(le2 == v2, lane, LANES), axis=-1, keepdims=True)
    e21 = jnp.exp(v2 - v1)
    g1 = gate_grp / (1.0 + e21)
    g2 = gate_grp * e21 / (1.0 + e21)

    oh1 = lane == i1
    oh2 = lane == i2
    onehot = jnp.where(oh1 | oh2, 1.0, 0.0).astype(F32)
    ti = lax.broadcasted_iota(jnp.int32, (tm, tm), 0)
    tj = lax.broadcasted_iota(jnp.int32, (tm, tm), 1)
    before = jnp.where(tj < ti, 1.0, 0.0).astype(F32)
    tot = _mm(before, onehot) + run_s[...]
    r1 = jnp.sum(jnp.where(oh1, tot, 0.0), axis=-1, keepdims=True)
    r2 = jnp.sum(jnp.where(oh2, tot, 0.0), axis=-1, keepdims=True)
    run_s[...] = run_s[...] + jnp.sum(onehot, axis=0, keepdims=True)
    cnt_ref[...] = run_s[...]

    ri = jnp.where(lane == 0, i1 - N_GROUPS, 0)
    ri = jnp.where(lane == 1, i2 - N_GROUPS, ri)
    ri = jnp.where(lane == 2, r1.astype(jnp.int32), ri)
    ri = jnp.where(lane == 3, r2.astype(jnp.int32), ri)
    ri_ref[...] = ri
    rg_ref[...] = jnp.where(lane == 0, g1, jnp.where(lane == 1, g2, 0.0))


def _outproj(oa, ob, xf, mod, w_o, ln_g, ln_b, w_rg, b_rg, w_re, b_re, seq):
    t, d = xf.shape
    tm = ROW_TILE
    gw = oa.shape[1]
    wo = _bf(w_o)
    wr = jnp.zeros((d, LANES), F32).at[:, :N_GROUPS].set(w_rg).at[:, N_GROUPS:N_GROUPS + N_EXPERTS].set(w_re)
    wrh = _bf(wr)
    wrl = _bf(wr - wrh.astype(F32))
    br = jnp.zeros((1, LANES), F32).at[0, :N_GROUPS].set(b_rg).at[0, N_GROUPS:N_GROUPS + N_EXPERTS].set(b_re)
    spb = seq // tm
    row = lambda i: (i, 0)
    const = lambda i: (0, 0)
    return pl.pallas_call(
        _outproj_kernel,
        out_shape=(jax.ShapeDtypeStruct((t, d), F32), jax.ShapeDtypeStruct((t, d), F32),
                   jax.ShapeDtypeStruct((t, LANES), jnp.int32), jax.ShapeDtypeStruct((t, LANES), F32),
                   jax.ShapeDtypeStruct((1, LANES), F32)),
        grid=(t // tm,),
        in_specs=[pl.BlockSpec((tm, gw), row), pl.BlockSpec((tm, gw), row), pl.BlockSpec((tm, d), row),
                  pl.BlockSpec((1, 6, d), lambda i: (i // spb, 0, 0)),
                  pl.BlockSpec((gw, d), const), pl.BlockSpec((gw, d), const),
                  pl.BlockSpec((1, d), const), pl.BlockSpec((1, d), const),
                  pl.BlockSpec((d, LANES), const), pl.BlockSpec((d, LANES), const),
                  pl.BlockSpec((1, LANES), const)],
        out_specs=(pl.BlockSpec((tm, d), row), pl.BlockSpec((tm, d), row),
                   pl.BlockSpec((tm, LANES), row), pl.BlockSpec((tm, LANES), row),
                   pl.BlockSpec((1, LANES), const)),
        scratch_shapes=[pltpu.VMEM((1, LANES), F32)],
        compiler_params=_params(("arbitrary",)),
        name="outproj",
    )(oa, ob, xf, mod, wo[:gw], wo[gw:], ln_g.reshape(1, d), ln_b.reshape(1, d), wrh, wrl, br)


def _dispatch_kernel(e1_ref, e2_ref, r1_ref, r2_ref, ps_ref, h_ref, xs_ref, sem):
    tm = h_ref.shape[0]
    i = pl.program_id(0)

    def row_copy(tk, dest):
        return pltpu.make_async_copy(h_ref.at[pl.ds(tk, 1), :], xs_ref.at[pl.ds(dest, 1), :], sem)

    def issue(tk, carry):
        tok = i * tm + tk
        row_copy(tk, ps_ref[e1_ref[tok]] + r1_ref[tok]).start()
        row_copy(tk, ps_ref[e2_ref[tok]] + r2_ref[tok]).start()
        return carry

    lax.fori_loop(0, tm, issue, 0)

    def drain(tk, carry):
        row_copy(0, 0).wait()
        row_copy(0, 0).wait()
        return carry

    lax.fori_loop(0, tm, drain, 0)


def _dispatch(h2, e1, e2, r1, r2, seg_start):
    t, d = h2.shape
    tm = ROW_TILE
    return pl.pallas_call(
        _dispatch_kernel,
        out_shape=jax.ShapeDtypeStruct((2 * t, d), F32),
        grid_spec=pltpu.PrefetchScalarGridSpec(
            num_scalar_prefetch=5,
            grid=(t // tm,),
            in_specs=[pl.BlockSpec((tm, d), lambda i, *_: (i, 0))],
            out_specs=pl.BlockSpec(memory_space=pl.ANY),
            scratch_shapes=[pltpu.SemaphoreType.DMA]),
        compiler_params=_params(("arbitrary",)),
        name="dispatch",
    )(e1, e2, r1, r2, seg_start, h2)


def _experts_kernel(wb_ref, we_ref, lo_ref, hi_ref, nw_ref, xs_ref, wg_ref, wu_ref, wd_ref, ys_ref):
    w = pl.program_id(0)

    @pl.when(w < nw_ref[0])
    def _():
        x = _bf(xs_ref[...])
        hid = (_silu(jnp.dot(x, wg_ref[0], preferred_element_type=F32))
               * jnp.dot(x, wu_ref[0], preferred_element_type=F32))
        y = jnp.dot(_bf(hid), wd_ref[0], preferred_element_type=F32)
        row = lax.broadcasted_iota(jnp.int32, (y.shape[0], 1), 0)
        mine = (row >= lo_ref[w]) & (row < hi_ref[w])

        @pl.when(lo_ref[w] == 0)
        def _():
            ys_ref[...] = jnp.where(mine, y, 0.0)

        @pl.when(lo_ref[w] > 0)
        def _():
            ys_ref[...] = jnp.where(mine, y, ys_ref[...])


def _experts(xs, item_block, item_expert, item_lo, item_hi, n_items, w_gate, w_up, w_down):
    n_slots, d = xs.shape
    ff = w_gate.shape[2]
    bm = EXPERT_BLOCK
    slot = lambda w, wb, we, lo, hi, nw: (wb[w], 0)
    wsel = lambda w, wb, we, lo, hi, nw: (we[w], 0, 0)
    return pl.pallas_call(
        _experts_kernel,
        out_shape=jax.ShapeDtypeStruct((n_slots, d), F32),
        grid_spec=pltpu.PrefetchScalarGridSpec(
            num_scalar_prefetch=5,
            grid=(item_block.shape[0],),
            in_specs=[pl.BlockSpec((bm, d), slot),
                      pl.BlockSpec((1, d, ff), wsel), pl.BlockSpec((1, d, ff), wsel),
                      pl.BlockSpec((1, ff, d), wsel)],
            out_specs=pl.BlockSpec((bm, d), slot)),
        compiler_params=_params(("arbitrary",)),
        name="experts",
    )(item_block, item_expert, item_lo, item_hi, n_items, xs, _bf(w_gate), _bf(w_up), _bf(w_down))


def _combine_kernel(e1_ref, e2_ref, r1_ref, r2_ref, ps_ref, ys_ref, rg_ref, x1_ref, mod_ref,
                    g_ref, b_ref, o_ref, ya, yb, sem):
    tm = x1_ref.shape[0]
    i = pl.program_id(0)

    def row_copy(dest, buf, tk):
        return pltpu.make_async_copy(ys_ref.at[pl.ds(dest, 1), :], buf.at[pl.ds(tk, 1), :], sem)

    def issue(tk, carry):
        tok = i * tm + tk
        row_copy(ps_ref[e1_ref[tok]] + r1_ref[tok], ya, tk).start()
        row_copy(ps_ref[e2_ref[tok]] + r2_ref[tok], yb, tk).start()
        return carry

    lax.fori_loop(0, tm, issue, 0)

    def drain(tk, carry):
        row_copy(0, ya, 0).wait()
        row_copy(0, yb, 0).wait()
        return carry

    lax.fori_loop(0, tm, drain, 0)

    gate2 = mod_ref[0, 5:6, :]
    rg = rg_ref[...]
    y = rg[:, 0:1] * ya[...] + rg[:, 1:2] * yb[...]
    o_ref[...] = _layer_norm(DEEPNORM_ALPHA * x1_ref[...] + (1.0 + gate2) * y, g_ref[...], b_ref[...])


def _combine(ys, e1, e2, r1, r2, pad_start, rg, x1, mod, ln_g, ln_b, seq):
    t, d = x1.shape
    tm = ROW_TILE
    spb = seq // tm
    row = lambda i, *_: (i, 0)
    const = lambda i, *_: (0, 0)
    buf = pltpu.VMEM((tm, d), F32)
    return pl.pallas_call(
        _combine_kernel,
        out_shape=jax.ShapeDtypeStruct((t, d), F32),
        grid_spec=pltpu.PrefetchScalarGridSpec(
            num_scalar_prefetch=5,
            grid=(t // tm,),
            in_specs=[pl.BlockSpec(memory_space=pl.ANY),
                      pl.BlockSpec((tm, LANES), row), pl.BlockSpec((tm, d), row),
                      pl.BlockSpec((1, 6, d), lambda i, *_: (i // spb, 0, 0)),
                      pl.BlockSpec((1, d), const), pl.BlockSpec((1, d), const)],
            out_specs=pl.BlockSpec((tm, d), row),
            scratch_shapes=[buf, buf, pltpu.SemaphoreType.DMA]),
        compiler_params=_params(("arbitrary",)),
        name="combine",
    )(e1, e2, r1, r2, pad_start, ys, rg, x1, mod, ln_g.reshape(1, d), ln_b.reshape(1, d))


def _layer(x, c, positions, w_ada, b_ada, w_in, conv_w, a_log, dt_bias, gdn_norm_w, attn_norm_w,
           w_o, ln1_g, ln1_b, w_rg, b_rg, w_re, b_re, w_gate, w_up, w_down, ln2_g, ln2_b):
    bsz, seq, d = x.shape
    t = bsz * seq
    xf = x.reshape(t, d)
    mod = _ada(c, w_ada, b_ada)
    qkv, z, bdc, bdr, qb, kb, vb = _inproj(xf, mod, positions.reshape(t, 1), w_in, seq)
    gw = GDN_HEADS * HEAD_DIM
    oa = _gdn(qkv.reshape(bsz, seq, 3 * gw), z.reshape(bsz, seq, gw), bdc.reshape(bsz, seq, LANES), bdr,
              conv_w, a_log, dt_bias, gdn_norm_w).reshape(t, gw)
    ob = _attn(qb, kb, vb, attn_norm_w, bsz, seq)
    x1, h2, ri, rg, cnt = _outproj(oa, ob, xf, mod, w_o, ln1_g, ln1_b, w_rg, b_rg, w_re, b_re, seq)

    bm = EXPERT_BLOCK
    counts = cnt[0, N_GROUPS:N_GROUPS + N_EXPERTS].astype(jnp.int32)
    seg_end = jnp.cumsum(counts)
    seg_start = seg_end - counts
    first_blk = seg_start // bm
    n_per = jnp.where(counts > 0, (seg_end - 1) // bm - first_blk + 1, 0)
    item_end = jnp.cumsum(n_per)
    n_items = item_end[-1:]
    max_items = (2 * t) // bm + N_EXPERTS - 1
    w_idx = jnp.minimum(jnp.arange(max_items, dtype=jnp.int32), n_items[0] - 1)
    item_expert = jnp.minimum(jnp.searchsorted(item_end, w_idx, side="right"), N_EXPERTS - 1).astype(jnp.int32)
    item_block = first_blk[item_expert] + w_idx - (item_end - n_per)[item_expert]
    item_lo = jnp.maximum(seg_start[item_expert] - item_block * bm, 0)
    item_hi = jnp.minimum(seg_end[item_expert] - item_block * bm, bm)
    e1, e2, r1, r2 = ri[:, 0], ri[:, 1], ri[:, 2], ri[:, 3]

    xs = _dispatch(h2, e1, e2, r1, r2, seg_start)
    ys = _experts(xs, item_block, item_expert, item_lo, item_hi, n_items, w_gate, w_up, w_down)
    out = _combine(ys, e1, e2, r1, r2, seg_start, rg, x1, mod, ln2_g, ln2_b, seq)
    return out.reshape(bsz, seq, d)


def kernel(x, c, positions, w_ada, b_ada, w_in, conv_w, a_log, dt_bias, gdn_norm_w, attn_norm_w, w_o, ln1_g, ln1_b, w_router_group, b_router_group, w_router_expert, b_router_expert, w_gate, w_up, w_down, ln2_g, ln2_b):
    assert w_ada.shape[0] == DEPTH
    return _layer(x, c, positions, w_ada[0], b_ada[0], w_in[0], conv_w[0], a_log[0], dt_bias[0],
                  gdn_norm_w[0], attn_norm_w[0], w_o[0], ln1_g[0], ln1_b[0],
                  w_router_group[0], b_router_group[0], w_router_expert[0], b_router_expert[0],
                  w_gate[0], w_up[0], w_down[0], ln2_g[0], ln2_b[0])
```

```python
import functools
import math

import jax
import jax.numpy as jnp
from jax import lax
from jax.experimental import pallas as pl
from jax.experimental.pallas import tpu as pltpu

F32 = jnp.float32
BF16 = jnp.bfloat16
HIGHEST = lax.Precision.HIGHEST

GDN_HEADS = 4
ATT_HEADS = 4
HEAD_DIM = 128
CONV_WIDTH = 4
DILATED_PATTERNS = ((128, 1), (512, 4), (2048, 16))
ROPE_THETA = 500000.0
ROPE_DIMS = HEAD_DIM // 4
N_GROUPS = 4
EXPERTS_PER_GROUP = 8
N_EXPERTS = N_GROUPS * EXPERTS_PER_GROUP
DEPTH = 1
DEEPNORM_ALPHA = (2.0 * DEPTH) ** 0.25
LN_EPS = 1e-5
RMS_EPS = 1e-6

LANES = 128
SUBLANES = 8
VMEM_LIMIT = 48 * 1024 * 1024

GDN_BLOCK = 128
GDN_TILE = 256
ATT_BLOCK = 128
ATT_TILE = 2048
ROW_TILE = 256
EXPERT_BLOCK = 256
DMA_UNROLL = 8
NEG = -1e30


def _bf(x):
    return x.astype(BF16)


def _mm(a, b):
    return jnp.dot(_bf(a), _bf(b), preferred_element_type=F32)


def _mm_nt(a, b):
    return lax.dot_general(_bf(a), _bf(b), (((1,), (1,)), ((), ())), preferred_element_type=F32)


def _mm_tn(a, b):
    return lax.dot_general(_bf(a), _bf(b), (((0,), (0,)), ((), ())), preferred_element_type=F32)


def _mm_f32(a, b):
    return jnp.dot(a, b, preferred_element_type=F32, precision=HIGHEST)


def _sigmoid(x):
    return 1.0 / (1.0 + jnp.exp(-x))


def _silu(x):
    return x * _sigmoid(x)


def _softplus(x):
    return jnp.maximum(x, 0.0) + jnp.log(1.0 + jnp.exp(-jnp.abs(x)))


def _params(sem):
    return pltpu.CompilerParams(dimension_semantics=sem, vmem_limit_bytes=VMEM_LIMIT)


def _ada_kernel(c_ref, w_ref, b_ref, o_ref):
    o_ref[...] = _mm_f32(_silu(c_ref[...]), w_ref[...]) + b_ref[...]


def _ada(c, w_ada, b_ada):
    bsz, d = c.shape
    n = w_ada.shape[1]
    tn = 512
    cp = jnp.zeros((SUBLANES, d), F32).at[:bsz].set(c)
    out = pl.pallas_call(
        _ada_kernel,
        out_shape=jax.ShapeDtypeStruct((SUBLANES, n), F32),
        grid=(n // tn,),
        in_specs=[pl.BlockSpec((SUBLANES, d), lambda j: (0, 0)),
                  pl.BlockSpec((d, tn), lambda j: (0, j)),
                  pl.BlockSpec((1, tn), lambda j: (0, j))],
        out_specs=pl.BlockSpec((SUBLANES, tn), lambda j: (0, j)),
        compiler_params=_params(("parallel",)),
        name="ada",
    )(cp, w_ada, b_ada.reshape(1, n))
    return out[:bsz].reshape(bsz, 6, d)


def _inproj_kernel(x_ref, mod_ref, pos_ref, invf_ref, wqkv_ref, wz_ref, wbd_ref, wbdt_ref,
                   wq_ref, wk_ref, wv_ref,
                   qkv_ref, z_ref, bdc_ref, bdr_ref, qb_ref, kb_ref, vb_ref):
    shift = mod_ref[0, 0:1, :]
    scale = mod_ref[0, 1:2, :]
    h = _bf(x_ref[...] * (1.0 + scale) + shift)
    qkv_ref[...] = jnp.dot(h, wqkv_ref[...], preferred_element_type=F32)
    z_ref[...] = jnp.dot(h, wz_ref[...], preferred_element_type=F32)
    bdc_ref[...] = jnp.dot(h, wbd_ref[...], preferred_element_type=F32)
    bdr_ref[0] = lax.dot_general(wbdt_ref[...], h, (((1,), (1,)), ((), ())),
                                 preferred_element_type=F32)
    ang = pos_ref[...].astype(F32) * invf_ref[...]
    cosv = jnp.cos(ang)
    sinv = jnp.sin(ang)
    lane = lax.broadcasted_iota(jnp.int32, (1, LANES), 1)
    half = ROPE_DIMS // 2
    first = lane < half
    sin_signed = jnp.where(first, -sinv, sinv)

    def rope(y):
        outs = []
        for hh in range(ATT_HEADS):
            yh = y[:, hh * HEAD_DIM:(hh + 1) * HEAD_DIM]
            partner = jnp.where(first, pltpu.roll(yh, LANES - half, 1), pltpu.roll(yh, half, 1))
            outs.append(yh * cosv + partner * sin_signed)
        return jnp.concatenate(outs, axis=1)

    qb_ref[...] = rope(jnp.dot(h, wq_ref[...], preferred_element_type=F32)) * (HEAD_DIM ** -0.5)
    kb_ref[...] = rope(jnp.dot(h, wk_ref[...], preferred_element_type=F32))
    vb_ref[...] = jnp.dot(h, wv_ref[...], preferred_element_type=F32)


def _inproj(xf, mod, pos, w_in, seq):
    t, d = xf.shape
    tm = ROW_TILE
    gw = GDN_HEADS * HEAD_DIM
    aw = ATT_HEADS * HEAD_DIM
    o0 = 3 * gw
    o1 = o0 + gw
    o2 = o1 + 2 * GDN_HEADS
    wb = _bf(w_in)
    wqkv, wz = wb[:, :o0], wb[:, o0:o1]
    wbd_n = wb[:, o1:o2]
    wbd = jnp.zeros((d, LANES), BF16).at[:, :2 * GDN_HEADS].set(wbd_n)
    wbdt = wbd_n.T
    wq, wk, wv = wb[:, o2:o2 + aw], wb[:, o2 + aw:o2 + 2 * aw], wb[:, o2 + 2 * aw:o2 + 3 * aw]
    half = ROPE_DIMS // 2
    inv_freq = ROPE_THETA ** (-jnp.arange(half, dtype=F32) * 2.0 / ROPE_DIMS)
    invf = jnp.zeros((1, LANES), F32).at[0, :ROPE_DIMS].set(jnp.concatenate([inv_freq, inv_freq]))
    spb = seq // tm
    row = lambda i: (i, 0)
    const = lambda i: (0, 0)
    return pl.pallas_call(
        _inproj_kernel,
        out_shape=(jax.ShapeDtypeStruct((t, o0), F32), jax.ShapeDtypeStruct((t, gw), F32),
                   jax.ShapeDtypeStruct((t, LANES), F32), jax.ShapeDtypeStruct((t // seq, SUBLANES, seq), F32),
                   jax.ShapeDtypeStruct((t, aw), F32), jax.ShapeDtypeStruct((t, aw), F32),
                   jax.ShapeDtypeStruct((t, aw), F32)),
        grid=(t // tm,),
        in_specs=[pl.BlockSpec((tm, d), row),
                  pl.BlockSpec((1, 6, d), lambda i: (i // spb, 0, 0)),
                  pl.BlockSpec((tm, 1), row),
                  pl.BlockSpec((1, LANES), const),
                  pl.BlockSpec((d, o0), const), pl.BlockSpec((d, gw), const),
                  pl.BlockSpec((d, LANES), const), pl.BlockSpec((2 * GDN_HEADS, d), const),
                  pl.BlockSpec((d, aw), const), pl.BlockSpec((d, aw), const),
                  pl.BlockSpec((d, aw), const)],
        out_specs=(pl.BlockSpec((tm, o0), row), pl.BlockSpec((tm, gw), row),
                   pl.BlockSpec((tm, LANES), row),
                   pl.BlockSpec((1, SUBLANES, tm), lambda i: (i // spb, 0, i % spb)),
                   pl.BlockSpec((tm, aw), row), pl.BlockSpec((tm, aw), row),
                   pl.BlockSpec((tm, aw), row)),
        compiler_params=_params(("parallel",)),
        name="inproj",
    )(xf, mod, pos, invf, wqkv, wz, wbd, wbdt, wq, wk, wv)


def _gdn_kernel(qkv_ref, z_ref, bdc_ref, bdr_ref, convw_ref, gpc_ref, gpr_ref, nw_ref, o_ref,
                cbuf, q_s, k_s, v_s, u_s, w_s, qd_s, kd_s, qk_s, a_s, x_s, y_s, st_ref):
    nbat, ts = qkv_ref.shape[0], qkv_ref.shape[1]
    nb = ts // GDN_BLOCK
    halo = SUBLANES

    @pl.when(pl.program_id(0) == 0)
    def _():
        cbuf[:, 0:halo, :] = jnp.zeros((nbat, halo, cbuf.shape[2]), F32)
        st_ref[...] = jnp.zeros(st_ref.shape, F32)

    ti = lax.broadcasted_iota(jnp.int32, (ts, ts), 0)
    tj = lax.broadcasted_iota(jnp.int32, (ts, ts), 1)
    same = (ti // GDN_BLOCK) == (tj // GDN_BLOCK)
    m_low = jnp.where(same & (tj <= ti), 1.0, 0.0).astype(F32)
    m_up = jnp.where(same & (ti <= tj), 1.0, 0.0).astype(F32)
    m_same = jnp.where(same, 1.0, 0.0).astype(F32)

    beta_c, gc_c, gt_c, gc_r = [], [], [], []
    for bb in range(nbat):
        cbuf[bb, halo:halo + ts, :] = qkv_ref[bb]
        for s in range(3 * GDN_HEADS):
            sl = slice(s * HEAD_DIM, (s + 1) * HEAD_DIM)
            off = halo - (CONV_WIDTH - 1)
            acc = convw_ref[0:1, sl] * cbuf[bb, off:off + ts, sl]
            for j in range(1, CONV_WIDTH):
                acc = acc + convw_ref[j:j + 1, sl] * cbuf[bb, off + j:off + j + ts, sl]
            y = _silu(acc)
            if s < 2 * GDN_HEADS:
                y = y * lax.rsqrt(jnp.sum(y * y, axis=-1, keepdims=True) + RMS_EPS)
            if s < GDN_HEADS:
                y = y * (HEAD_DIM ** -0.5)
            dst = (q_s, k_s, v_s)[s // GDN_HEADS]
            hs = s % GDN_HEADS
            dst[bb, :, hs * HEAD_DIM:(hs + 1) * HEAD_DIM] = y
        cbuf[bb, 0:halo, :] = cbuf[bb, ts:ts + halo, :]

        bdc = bdc_ref[bb]
        beta_c.append(_sigmoid(bdc))
        g_c = -jnp.exp(gpc_ref[0:1, :]) * _softplus(bdc + gpc_ref[1:2, :])
        g_r = -jnp.exp(gpr_ref[:, 0:1]) * _softplus(bdr_ref[bb] + gpr_ref[:, 1:2])
        gc_c.append(_mm_f32(m_low, g_c))
        gt_c.append(_mm_f32(m_same, g_c))
        gc_r.append(_mm_f32(g_r, m_up))

    ii = lax.broadcasted_iota(jnp.int32, (GDN_BLOCK, GDN_BLOCK), 0)
    jj = lax.broadcasted_iota(jnp.int32, (GDN_BLOCK, GDN_BLOCK), 1)
    lower = jj <= ii
    strict = jj < ii
    eye = jnp.where(ii == jj, 1.0, 0.0).astype(F32)
    levels = []
    b = 1
    while b < GDN_BLOCK:
        levels.append(((ii // b) == (jj // b) + 1) & (((jj // b) % 2) == 0))
        b *= 2

    chains = [(bb, j, hh) for bb in range(nbat) for j in range(nb) for hh in range(GDN_HEADS)]

    def tile_of(bb, j, hh):
        return bb, slice(j * GDN_BLOCK, (j + 1) * GDN_BLOCK), slice(hh * HEAD_DIM, (hh + 1) * HEAD_DIM)

    for c, (bb, j, hh) in enumerate(chains):
        blk = tile_of(bb, j, hh)
        rows = blk[1]
        q = q_s[blk]
        k = k_s[blk]
        beta = beta_c[bb][rows, hh:hh + 1]
        gcc = gc_c[bb][rows, GDN_HEADS + hh:GDN_HEADS + hh + 1]
        gtc = gt_c[bb][rows, GDN_HEADS + hh:GDN_HEADS + hh + 1]
        gcr = gc_r[bb][GDN_HEADS + hh:GDN_HEADS + hh + 1, rows]
        kb = k * beta
        eg = jnp.exp(gcc)
        dm = jnp.where(lower, jnp.exp(gcc - gcr), 0.0)
        a = jnp.where(strict, _mm_nt(kb, k) * dm, 0.0)
        a_s[c] = a
        x_s[c] = eye - jnp.where(levels[0], a, 0.0)
        u_s[blk] = v_s[blk] * beta
        w_s[blk] = kb * eg
        qd_s[blk] = q * eg
        kd_s[blk] = k * jnp.exp(gtc - gcc)
        qk_s[blk] = _mm_nt(q, k) * dm

    for lm in levels[1:]:
        for c in range(len(chains)):
            y_s[c] = _mm(x_s[c], jnp.where(lm, a_s[c], 0.0))
        for c in range(len(chains)):
            xc = x_s[c]
            x_s[c] = xc - _mm(y_s[c], xc)

    for c, (bb, j, hh) in enumerate(chains):
        blk = tile_of(bb, j, hh)
        sol = _mm(x_s[c], jnp.concatenate([u_s[blk], w_s[blk]], axis=1))
        u_s[blk] = sol[:, :HEAD_DIM]
        w_s[blk] = sol[:, HEAD_DIM:]

    for j in range(nb):
        for bb in range(nbat):
            for hh in range(GDN_HEADS):
                blk = tile_of(bb, j, hh)
                si = bb * GDN_HEADS + hh
                state = st_ref[si]
                proj = _mm(jnp.concatenate([w_s[blk], qd_s[blk]], axis=0), state)
                v_new = u_s[blk] - proj[:GDN_BLOCK]
                o = proj[GDN_BLOCK:] + _mm(qk_s[blk], v_new)
                g_last = jnp.exp(gt_c[bb][j * GDN_BLOCK:j * GDN_BLOCK + 1,
                                          GDN_HEADS + hh:GDN_HEADS + hh + 1])
                st_ref[si] = state * g_last + _mm_tn(kd_s[blk], v_new)
                on = o * lax.rsqrt(jnp.mean(o * o, axis=-1, keepdims=True) + RMS_EPS) * nw_ref[...]
                o_ref[blk] = (on * _silu(z_ref[blk])).astype(o_ref.dtype)


def _gdn(qkv, z, bdc, bdr, conv_w, a_log, dt_bias, norm_w):
    bsz, seq, _ = qkv.shape
    ts = GDN_TILE
    gw = GDN_HEADS * HEAD_DIM
    zeros4 = jnp.zeros((GDN_HEADS,), F32)
    al = jnp.concatenate([zeros4, a_log])
    db = jnp.concatenate([zeros4, dt_bias])
    gpc = jnp.zeros((2, LANES), F32).at[0, :2 * GDN_HEADS].set(al).at[1, :2 * GDN_HEADS].set(db)
    gpr = jnp.stack([al, db], axis=1)
    row = lambda i: (0, i, 0)
    const = lambda i: (0, 0)
    n_chains = bsz * (ts // GDN_BLOCK) * GDN_HEADS
    tile = pltpu.VMEM((bsz, ts, gw), F32)
    mats = pltpu.VMEM((n_chains, GDN_BLOCK, GDN_BLOCK), F32)
    return pl.pallas_call(
        _gdn_kernel,
        out_shape=jax.ShapeDtypeStruct((bsz, seq, gw), BF16),
        grid=(seq // ts,),
        in_specs=[pl.BlockSpec((bsz, ts, 3 * gw), row), pl.BlockSpec((bsz, ts, gw), row),
                  pl.BlockSpec((bsz, ts, LANES), row),
                  pl.BlockSpec((bsz, SUBLANES, ts), lambda i: (0, 0, i)),
                  pl.BlockSpec((CONV_WIDTH, 3 * gw), const),
                  pl.BlockSpec((2, LANES), const), pl.BlockSpec((2 * GDN_HEADS, 2), const),
                  pl.BlockSpec((1, HEAD_DIM), const)],
        out_specs=pl.BlockSpec((bsz, ts, gw), row),
        scratch_shapes=[pltpu.VMEM((bsz, ts + 2 * SUBLANES, 3 * gw), F32),
                        tile, tile, tile, tile, tile, tile, tile, tile,
                        mats, mats, mats,
                        pltpu.VMEM((bsz * GDN_HEADS, HEAD_DIM, HEAD_DIM), F32)],
        compiler_params=_params(("arbitrary",)),
        name="gdn",
    )(qkv, z, bdc, bdr, conv_w, gpc, gpr, norm_w.reshape(1, HEAD_DIM))


def _attn_kernel(q_ref, kp_ref, kc_ref, vp_ref, vc_ref, nw_ref, o_ref, m_s, l_s, acc_s):
    tq = q_ref.shape[0]
    blk = ATT_BLOCK
    qi = lax.broadcasted_iota(jnp.int32, (blk, 2 * blk), 0)
    kj = lax.broadcasted_iota(jnp.int32, (blk, 2 * blk), 1)
    band = (kj >= qi) & (kj <= qi + blk)
    first_lo = jnp.where(pl.program_id(2) > 0, 0, blk)

    def block_stats(q, kprev, kcur, vprev, vcur, from_prev_tile):
        s = _mm_nt(q, jnp.concatenate([kprev, kcur], axis=0))
        mask = band & (kj >= first_lo) if from_prev_tile else band
        s = jnp.where(mask, s, NEG)
        m = jnp.max(s, axis=-1, keepdims=True)
        p = jnp.exp(s - m)
        l = jnp.sum(p, axis=-1, keepdims=True)
        o = _mm(p, jnp.concatenate([vprev, vcur], axis=0))
        return m, l, o

    for window, dil in DILATED_PATTERNS:
        span_tokens = blk * dil
        for jb in range(tq // span_tokens):
            for r in range(dil):
                base = jb * span_tokens + r
                cur = pl.ds(base, blk, stride=dil) if dil > 1 else pl.ds(base, blk)
                if jb > 0:
                    pbase = base - span_tokens
                    prev = pl.ds(pbase, blk, stride=dil) if dil > 1 else pl.ds(pbase, blk)
                    kprev, vprev = kc_ref[prev, :], vc_ref[prev, :]
                else:
                    pbase = tq - span_tokens + r
                    prev = pl.ds(pbase, blk, stride=dil) if dil > 1 else pl.ds(pbase, blk)
                    kprev, vprev = kp_ref[prev, :], vp_ref[prev, :]
                m, l, o = block_stats(q_ref[cur, :], kprev, kc_ref[cur, :], vprev, vc_ref[cur, :],
                                      jb == 0)
                if dil == 1:
                    m_s[cur, :] = jnp.broadcast_to(m, (blk, HEAD_DIM))
                    l_s[cur, :] = jnp.broadcast_to(l, (blk, HEAD_DIM))
                    acc_s[cur, :] = o
                else:
                    m_old = m_s[cur, :]
                    m_new = jnp.maximum(m_old, m)
                    w_old = jnp.exp(m_old - m_new)
                    w_cur = jnp.exp(m - m_new)
                    m_s[cur, :] = m_new
                    l_s[cur, :] = w_old * l_s[cur, :] + w_cur * l
                    acc_s[cur, :] = w_old * acc_s[cur, :] + w_cur * o

    out = acc_s[...] / l_s[...]
    out = out * lax.rsqrt(jnp.mean(out * out, axis=-1, keepdims=True) + RMS_EPS) * nw_ref[...]
    o_ref[...] = out.astype(o_ref.dtype)


def _attn(qb, kb, vb, norm_w, bsz, seq):
    t = qb.shape[0]
    tq = ATT_TILE
    spb = seq // tq
    cur = lambda b, h, i: (b * spb + i, h)
    prev = lambda b, h, i: (b * spb + jnp.maximum(i - 1, 0), h)
    blk = lambda f: pl.BlockSpec((tq, HEAD_DIM), f)
    acc = pltpu.VMEM((tq, HEAD_DIM), F32)
    return pl.pallas_call(
        _attn_kernel,
        out_shape=jax.ShapeDtypeStruct((t, ATT_HEADS * HEAD_DIM), BF16),
        grid=(bsz, ATT_HEADS, spb),
        in_specs=[blk(cur), blk(prev), blk(cur), blk(prev), blk(cur),
                  pl.BlockSpec((1, HEAD_DIM), lambda b, h, i: (0, 0))],
        out_specs=blk(cur),
        scratch_shapes=[acc, acc, acc],
        compiler_params=_params(("parallel", "parallel", "arbitrary")),
        name="attn",
    )(qb, kb, kb, vb, vb, norm_w.reshape(1, HEAD_DIM))


def _layer_norm(y, g, b):
    mu = jnp.mean(y, axis=-1, keepdims=True)
    yc = y - mu
    var = jnp.mean(yc * yc, axis=-1, keepdims=True)
    return yc * lax.rsqrt(var + LN_EPS) * g + b


def _outproj_kernel(oa_ref, ob_ref, x_ref, mod_ref, woa_ref, wob_ref, g_ref, b_ref,
                    wrh_ref, wrl_ref, br_ref,
                    x1_ref, h2_ref, ri_ref, rg_ref, cnt_ref, run_s):
    @pl.when(pl.program_id(0) == 0)
    def _():
        run_s[...] = jnp.zeros(run_s.shape, F32)

    gate1 = mod_ref[0, 2:3, :]
    shift2 = mod_ref[0, 3:4, :]
    scale2 = mod_ref[0, 4:5, :]
    mix = (jnp.dot(oa_ref[...], woa_ref[...], preferred_element_type=F32)
           + jnp.dot(ob_ref[...], wob_ref[...], preferred_element_type=F32))
    x1 = _layer_norm(DEEPNORM_ALPHA * x_ref[...] + (1.0 + gate1) * mix, g_ref[...], b_ref[...])
    x1_ref[...] = x1
    h2 = x1 * (1.0 + scale2) + shift2
    h2_ref[...] = h2

    hi = _bf(h2)
    lo = _bf(h2 - hi.astype(F32))
    logits = (jnp.dot(hi, wrh_ref[...], preferred_element_type=F32)
              + jnp.dot(hi, wrl_ref[...], preferred_element_type=F32)
              + jnp.dot(lo, wrh_ref[...], preferred_element_type=F32)) + br_ref[...]
    tm = logits.shape[0]
    lane = lax.broadcasted_iota(jnp.int32, (tm, LANES), 1)
    lg = jnp.where(lane < N_GROUPS, logits, NEG)
    mg = jnp.max(lg, axis=-1, keepdims=True)
    grp = jnp.min(jnp.where(lg == mg, lane, LANES), axis=-1, keepdims=True)
    gate_grp = 1.0 / jnp.sum(jnp.exp(lg - mg), axis=-1, keepdims=True)
    eidx = lane - N_GROUPS
    sel = (eidx >= 0) & (eidx < N_EXPERTS) & ((eidx // EXPERTS_PER_GROUP) == grp)
    le = jnp.where(sel, logits, NEG)
    v1 = jnp.max(le, axis=-1, keepdims=True)
    i1 = jnp.min(jnp.where(le == v1, lane, LANES), axis=-1, keepdims=True)
    le2 = jnp.where(lane == i1, NEG, le)
    v2 = jnp.max(le2, axis=-1, keepdims=True)
    i2 = jnp.min(jnp.where(le2 == v2, lane, LANES), axis=-1, keepdims=True)
    e21 = jnp.exp(v2 - v1)
    g1 = gate_grp / (1.0 + e21)
    g2 = gate_grp * e21 / (1.0 + e21)

    oh1 = lane == i1
    oh2 = lane == i2
    onehot = jnp.where(oh1 | oh2, 1.0, 0.0).astype(F32)
    ti = lax.broadcasted_iota(jnp.int32, (tm, tm), 0)
    tj = lax.broadcasted_iota(jnp.int32, (tm, tm), 1)
    before = jnp.where(tj < ti, 1.0, 0.0).astype(F32)
    tot = _mm(before, onehot) + run_s[...]
    r1 = jnp.sum(jnp.where(oh1, tot, 0.0), axis=-1, keepdims=True)
    r2 = jnp.sum(jnp.where(oh2, tot, 0.0), axis=-1, keepdims=True)
    run_s[...] = run_s[...] + jnp.sum(onehot, axis=0, keepdims=True)
    cnt_ref[...] = run_s[...]

    cols = jnp.where(lane == 0, (i1 - N_GROUPS).astype(F32), 0.0)
    cols = jnp.where(lane == 1, (i2 - N_GROUPS).astype(F32), cols)
    r1_hi = jnp.floor(r1 * (1.0 / 256.0))
    r2_hi = jnp.floor(r2 * (1.0 / 256.0))
    cols = jnp.where(lane == 2, r1_hi, cols)
    cols = jnp.where(lane == 3, r1 - 256.0 * r1_hi, cols)
    cols = jnp.where(lane == 4, r2_hi, cols)
    cols = jnp.where(lane == 5, r2 - 256.0 * r2_hi, cols)
    pick = jnp.where(lax.broadcasted_iota(jnp.int32, (SUBLANES, LANES), 0)
                     == lax.broadcasted_iota(jnp.int32, (SUBLANES, LANES), 1), 1.0, 0.0).astype(F32)
    ri_ref[...] = _mm_nt(pick, cols).astype(jnp.int32)
    rg_ref[...] = jnp.where(lane == 0, g1, jnp.where(lane == 1, g2, 0.0))


def _outproj(oa, ob, xf, mod, w_o, ln_g, ln_b, w_rg, b_rg, w_re, b_re, seq):
    t, d = xf.shape
    tm = ROW_TILE
    gw = oa.shape[1]
    wo = _bf(w_o)
    wr = jnp.zeros((d, LANES), F32).at[:, :N_GROUPS].set(w_rg).at[:, N_GROUPS:N_GROUPS + N_EXPERTS].set(w_re)
    wrh = _bf(wr)
    wrl = _bf(wr - wrh.astype(F32))
    br = jnp.zeros((1, LANES), F32).at[0, :N_GROUPS].set(b_rg).at[0, N_GROUPS:N_GROUPS + N_EXPERTS].set(b_re)
    spb = seq // tm
    row = lambda i: (i, 0)
    const = lambda i: (0, 0)
    return pl.pallas_call(
        _outproj_kernel,
        out_shape=(jax.ShapeDtypeStruct((t, d), F32), jax.ShapeDtypeStruct((t, d), F32),
                   jax.ShapeDtypeStruct((SUBLANES, t), jnp.int32), jax.ShapeDtypeStruct((t, LANES), F32),
                   jax.ShapeDtypeStruct((1, LANES), F32)),
        grid=(t // tm,),
        in_specs=[pl.BlockSpec((tm, gw), row), pl.BlockSpec((tm, gw), row), pl.BlockSpec((tm, d), row),
                  pl.BlockSpec((1, 6, d), lambda i: (i // spb, 0, 0)),
                  pl.BlockSpec((gw, d), const), pl.BlockSpec((gw, d), const),
                  pl.BlockSpec((1, d), const), pl.BlockSpec((1, d), const),
                  pl.BlockSpec((d, LANES), const), pl.BlockSpec((d, LANES), const),
                  pl.BlockSpec((1, LANES), const)],
        out_specs=(pl.BlockSpec((tm, d), row), pl.BlockSpec((tm, d), row),
                   pl.BlockSpec((SUBLANES, tm), lambda i: (0, i)), pl.BlockSpec((tm, LANES), row),
                   pl.BlockSpec((1, LANES), const)),
        scratch_shapes=[pltpu.VMEM((1, LANES), F32)],
        compiler_params=_params(("arbitrary",)),
        name="outproj",
    )(oa, ob, xf, mod, wo[:gw], wo[gw:], ln_g.reshape(1, d), ln_b.reshape(1, d), wrh, wrl, br)


def _dispatch_kernel(d1_ref, d2_ref, h_ref, xs_ref, sem):
    tm = h_ref.shape[0]
    i = pl.program_id(0)

    def row_copy(tk, dest):
        return pltpu.make_async_copy(h_ref.at[pl.ds(tk, 1), :], xs_ref.at[pl.ds(dest, 1), :], sem)

    def issue(tk, carry):
        tok = i * tm + tk
        row_copy(tk, d1_ref[tok]).start()
        row_copy(tk, d2_ref[tok]).start()
        return carry

    lax.fori_loop(0, tm, issue, 0, unroll=DMA_UNROLL)

    tile_copy = pltpu.make_async_copy(h_ref, xs_ref.at[pl.ds(0, tm), :], sem)
    tile_copy.wait()
    tile_copy.wait()


def _dispatch(h2, d1, d2):
    t, d = h2.shape
    tm = ROW_TILE
    return pl.pallas_call(
        _dispatch_kernel,
        out_shape=jax.ShapeDtypeStruct((2 * t, d), F32),
        grid_spec=pltpu.PrefetchScalarGridSpec(
            num_scalar_prefetch=2,
            grid=(t // tm,),
            in_specs=[pl.BlockSpec((tm, d), lambda i, *_: (i, 0))],
            out_specs=pl.BlockSpec(memory_space=pl.ANY),
            scratch_shapes=[pltpu.SemaphoreType.DMA]),
        compiler_params=_params(("arbitrary",)),
        name="dispatch",
    )(d1, d2, h2)


def _experts_kernel(wb_ref, we_ref, lo_ref, hi_ref, nw_ref, xs_ref, wg_ref, wu_ref, wd_ref, ys_ref,
                    wg_s, wu_s, wd_s):
    w = pl.program_id(0)

    @pl.when((w == 0) | (we_ref[w] != we_ref[jnp.maximum(w - 1, 0)]))
    def _():
        wg_s[...] = _bf(wg_ref[0])
        wu_s[...] = _bf(wu_ref[0])
        wd_s[...] = _bf(wd_ref[0])

    @pl.when(w < nw_ref[0])
    def _():
        x = _bf(xs_ref[...])
        hid = (_silu(jnp.dot(x, wg_s[...], preferred_element_type=F32))
               * jnp.dot(x, wu_s[...], preferred_element_type=F32))
        y = jnp.dot(_bf(hid), wd_s[...], preferred_element_type=F32)
        row = lax.broadcasted_iota(jnp.int32, (y.shape[0], 1), 0)
        mine = (row >= lo_ref[w]) & (row < hi_ref[w])

        @pl.when(lo_ref[w] == 0)
        def _():
            ys_ref[...] = jnp.where(mine, y, 0.0)

        @pl.when(lo_ref[w] > 0)
        def _():
            ys_ref[...] = jnp.where(mine, y, ys_ref[...])


def _experts(xs, item_block, item_expert, item_lo, item_hi, n_items, w_gate, w_up, w_down):
    n_slots, d = xs.shape
    ff = w_gate.shape[2]
    bm = EXPERT_BLOCK
    slot = lambda w, wb, we, lo, hi, nw: (wb[w], 0)
    wsel = lambda w, wb, we, lo, hi, nw: (we[w], 0, 0)
    return pl.pallas_call(
        _experts_kernel,
        out_shape=jax.ShapeDtypeStruct((n_slots, d), F32),
        grid_spec=pltpu.PrefetchScalarGridSpec(
            num_scalar_prefetch=5,
            grid=(item_block.shape[0],),
            in_specs=[pl.BlockSpec((bm, d), slot),
                      pl.BlockSpec((1, d, ff), wsel), pl.BlockSpec((1, d, ff), wsel),
                      pl.BlockSpec((1, ff, d), wsel)],
            out_specs=pl.BlockSpec((bm, d), slot),
            scratch_shapes=[pltpu.VMEM((d, ff), BF16), pltpu.VMEM((d, ff), BF16),
                            pltpu.VMEM((ff, d), BF16)]),
        compiler_params=_params(("arbitrary",)),
        name="experts",
    )(item_block, item_expert, item_lo, item_hi, n_items, xs, w_gate, w_up, w_down)


def _combine_kernel(d1_ref, d2_ref, ys_ref, rg_ref, x1_ref, mod_ref, g_ref, b_ref, o_ref, ya, yb, sem):
    tm = x1_ref.shape[0]
    i = pl.program_id(0)
    n = pl.num_programs(0)

    def row_copy(dest, buf, slot, tk):
        return pltpu.make_async_copy(ys_ref.at[pl.ds(dest, 1), :], buf.at[slot, pl.ds(tk, 1), :],
                                     sem.at[slot])

    def gather_tile(step, slot):
        def issue(tk, carry):
            tok = step * tm + tk
            row_copy(d1_ref[tok], ya, slot, tk).start()
            row_copy(d2_ref[tok], yb, slot, tk).start()
            return carry

        lax.fori_loop(0, tm, issue, 0, unroll=DMA_UNROLL)

    @pl.when(i == 0)
    def _():
        gather_tile(0, 0)

    @pl.when(i + 1 < n)
    def _():
        gather_tile(i + 1, (i + 1) % 2)

    slot = i % 2
    pltpu.make_async_copy(ys_ref.at[pl.ds(0, tm), :], ya.at[slot], sem.at[slot]).wait()
    pltpu.make_async_copy(ys_ref.at[pl.ds(0, tm), :], yb.at[slot], sem.at[slot]).wait()

    gate2 = mod_ref[0, 5:6, :]
    rg = rg_ref[...]
    y = rg[:, 0:1] * ya[slot] + rg[:, 1:2] * yb[slot]
    o_ref[...] = _layer_norm(DEEPNORM_ALPHA * x1_ref[...] + (1.0 + gate2) * y, g_ref[...], b_ref[...])


def _combine(ys, d1, d2, rg, x1, mod, ln_g, ln_b, seq):
    t, d = x1.shape
    tm = ROW_TILE
    spb = seq // tm
    row = lambda i, *_: (i, 0)
    const = lambda i, *_: (0, 0)
    buf = pltpu.VMEM((2, tm, d), F32)
    return pl.pallas_call(
        _combine_kernel,
        out_shape=jax.ShapeDtypeStruct((t, d), F32),
        grid_spec=pltpu.PrefetchScalarGridSpec(
            num_scalar_prefetch=2,
            grid=(t // tm,),
            in_specs=[pl.BlockSpec(memory_space=pl.ANY),
                      pl.BlockSpec((tm, LANES), row), pl.BlockSpec((tm, d), row),
                      pl.BlockSpec((1, 6, d), lambda i, *_: (i // spb, 0, 0)),
                      pl.BlockSpec((1, d), const), pl.BlockSpec((1, d), const)],
            out_specs=pl.BlockSpec((tm, d), row),
            scratch_shapes=[buf, buf, pltpu.SemaphoreType.DMA((2,))]),
        compiler_params=_params(("arbitrary",)),
        name="combine",
    )(d1, d2, ys, rg, x1, mod, ln_g.reshape(1, d), ln_b.reshape(1, d))


def _layer(x, c, positions, w_ada, b_ada, w_in, conv_w, a_log, dt_bias, gdn_norm_w, attn_norm_w,
           w_o, ln1_g, ln1_b, w_rg, b_rg, w_re, b_re, w_gate, w_up, w_down, ln2_g, ln2_b):
    bsz, seq, d = x.shape
    t = bsz * seq
    xf = x.reshape(t, d)
    mod = _ada(c, w_ada, b_ada)
    qkv, z, bdc, bdr, qb, kb, vb = _inproj(xf, mod, positions.reshape(t, 1), w_in, seq)
    gw = GDN_HEADS * HEAD_DIM
    oa = _gdn(qkv.reshape(bsz, seq, 3 * gw), z.reshape(bsz, seq, gw), bdc.reshape(bsz, seq, LANES), bdr,
              conv_w, a_log, dt_bias, gdn_norm_w).reshape(t, gw)
    ob = _attn(qb, kb, vb, attn_norm_w, bsz, seq)
    x1, h2, ri, rg, cnt = _outproj(oa, ob, xf, mod, w_o, ln1_g, ln1_b, w_rg, b_rg, w_re, b_re, seq)

    bm = EXPERT_BLOCK
    counts = cnt[0, N_GROUPS:N_GROUPS + N_EXPERTS].astype(jnp.int32)
    seg_end = jnp.cumsum(counts)
    seg_start = seg_end - counts
    first_blk = seg_start // bm
    n_per = jnp.where(counts > 0, (seg_end - 1) // bm - first_blk + 1, 0)
    item_end = jnp.cumsum(n_per)
    n_items = item_end[-1:]
    max_items = (2 * t) // bm + N_EXPERTS - 1
    w_idx = jnp.minimum(jnp.arange(max_items, dtype=jnp.int32), n_items[0] - 1)
    item_expert = jnp.minimum(jnp.sum(item_end[None, :] <= w_idx[:, None], axis=1), N_EXPERTS - 1).astype(jnp.int32)
    item_block = first_blk[item_expert] + w_idx - (item_end - n_per)[item_expert]
    item_lo = jnp.maximum(seg_start[item_expert] - item_block * bm, 0)
    item_hi = jnp.minimum(seg_end[item_expert] - item_block * bm, bm)
    d1 = seg_start[ri[0]] + ri[2] * 256 + ri[3]
    d2 = seg_start[ri[1]] + ri[4] * 256 + ri[5]

    xs = _dispatch(h2, d1, d2)
    ys = _experts(xs, item_block, item_expert, item_lo, item_hi, n_items, w_gate, w_up, w_down)
    out = _combine(ys, d1, d2, rg, x1, mod, ln2_g, ln2_b, seq)
    return out.reshape(bsz, seq, d)


def kernel(x, c, positions, w_ada, b_ada, w_in, conv_w, a_log, dt_bias, gdn_norm_w, attn_norm_w, w_o, ln1_g, ln1_b, w_router_group, b_router_group, w_router_expert, b_router_expert, w_gate, w_up, w_down, ln2_g, ln2_b):
    assert w_ada.shape[0] == DEPTH
    return _layer(x, c, positions, w_ada[0], b_ada[0], w_in[0], conv_w[0], a_log[0], dt_bias[0],
                  gdn_norm_w[0], attn_norm_w[0], w_o[0], ln1_g[0], ln1_b[0],
                  w_router_group[0], b_router_group[0], w_router_expert[0], b_router_expert[0],
                  w_gate[0], w_up[0], w_down[0], ln2_g[0], ln2_b[0])
```

```python
import functools
import math

import jax
import jax.numpy as jnp
from jax import lax
from jax.experimental import pallas as pl
from jax.experimental.pallas import tpu as pltpu

F32 = jnp.float32
BF16 = jnp.bfloat16
HIGHEST = lax.Precision.HIGHEST

GDN_HEADS = 4
ATT_HEADS = 4
HEAD_DIM = 128
CONV_WIDTH = 4
DILATED_PATTERNS = ((128, 1), (512, 4), (2048, 16))
ROPE_THETA = 500000.0
ROPE_DIMS = HEAD_DIM // 4
N_GROUPS = 4
EXPERTS_PER_GROUP = 8
N_EXPERTS = N_GROUPS * EXPERTS_PER_GROUP
DEPTH = 1
DEEPNORM_ALPHA = (2.0 * DEPTH) ** 0.25
LN_EPS = 1e-5
RMS_EPS = 1e-6

LANES = 128
SUBLANES = 8
VMEM_LIMIT = 48 * 1024 * 1024

GDN_BLOCK = 128
GDN_TILE = 256
ATT_BLOCK = 128
ATT_TILE = 2048
ROW_TILE = 256
DISPATCH_TILE = 1024
EXPERT_BLOCK = 256
DMA_UNROLL = 8
NEG = -1e30


def _bf(x):
    return x.astype(BF16)


def _mm(a, b):
    return jnp.dot(_bf(a), _bf(b), preferred_element_type=F32)


def _mm_nt(a, b):
    return lax.dot_general(_bf(a), _bf(b), (((1,), (1,)), ((), ())), preferred_element_type=F32)


def _mm_tn(a, b):
    return lax.dot_general(_bf(a), _bf(b), (((0,), (0,)), ((), ())), preferred_element_type=F32)


def _mm_f32(a, b):
    return jnp.dot(a, b, preferred_element_type=F32, precision=HIGHEST)


def _sigmoid(x):
    return 1.0 / (1.0 + jnp.exp(-x))


def _silu(x):
    return x * _sigmoid(x)


def _softplus(x):
    return jnp.maximum(x, 0.0) + jnp.log(1.0 + jnp.exp(-jnp.abs(x)))


def _params(sem):
    return pltpu.CompilerParams(dimension_semantics=sem, vmem_limit_bytes=VMEM_LIMIT)


def _ada_kernel(c_ref, w_ref, b_ref, o_ref):
    o_ref[...] = _mm_f32(_silu(c_ref[...]), w_ref[...]) + b_ref[...]


def _ada(c, w_ada, b_ada):
    bsz, d = c.shape
    n = w_ada.shape[1]
    tn = 512
    cp = jnp.zeros((SUBLANES, d), F32).at[:bsz].set(c)
    out = pl.pallas_call(
        _ada_kernel,
        out_shape=jax.ShapeDtypeStruct((SUBLANES, n), F32),
        grid=(n // tn,),
        in_specs=[pl.BlockSpec((SUBLANES, d), lambda j: (0, 0)),
                  pl.BlockSpec((d, tn), lambda j: (0, j)),
                  pl.BlockSpec((1, tn), lambda j: (0, j))],
        out_specs=pl.BlockSpec((SUBLANES, tn), lambda j: (0, j)),
        compiler_params=_params(("parallel",)),
        name="ada",
    )(cp, w_ada, b_ada.reshape(1, n))
    return out[:bsz].reshape(bsz, 6, d)


def _inproj_kernel(x_ref, mod_ref, pos_ref, invf_ref, wqkv_ref, wz_ref, wbd_ref, wbdt_ref,
                   wq_ref, wk_ref, wv_ref,
                   qkv_ref, z_ref, bdc_ref, bdr_ref, qb_ref, kb_ref, vb_ref):
    ang = pos_ref[...].astype(F32) * invf_ref[...]
    cosv = jnp.cos(ang)
    sinv = jnp.sin(ang)
    lane = lax.broadcasted_iota(jnp.int32, (1, LANES), 1)
    half = ROPE_DIMS // 2
    first = lane < half
    sin_signed = jnp.where(first, -sinv, sinv)

    shift = mod_ref[0, 0:1, :]
    scale = mod_ref[0, 1:2, :]
    h = _bf(x_ref[...] * (1.0 + scale) + shift)
    qkv_ref[...] = jnp.dot(h, wqkv_ref[...], preferred_element_type=F32)
    z_ref[...] = jnp.dot(h, wz_ref[...], preferred_element_type=F32)
    bdc_ref[...] = jnp.dot(h, wbd_ref[...], preferred_element_type=F32)
    bdr_ref[0] = lax.dot_general(wbdt_ref[...], h, (((1,), (1,)), ((), ())),
                                 preferred_element_type=F32)

    def rope(y):
        outs = []
        for hh in range(ATT_HEADS):
            yh = y[:, hh * HEAD_DIM:(hh + 1) * HEAD_DIM]
            partner = jnp.where(first, pltpu.roll(yh, LANES - half, 1), pltpu.roll(yh, half, 1))
            outs.append(yh * cosv + partner * sin_signed)
        return jnp.concatenate(outs, axis=1)

    qb_ref[...] = rope(jnp.dot(h, wq_ref[...], preferred_element_type=F32)) * (HEAD_DIM ** -0.5)
    kb_ref[...] = rope(jnp.dot(h, wk_ref[...], preferred_element_type=F32))
    vb_ref[...] = jnp.dot(h, wv_ref[...], preferred_element_type=F32)


def _inproj(xf, mod, pos, w_in, seq):
    t, d = xf.shape
    tm = ROW_TILE
    gw = GDN_HEADS * HEAD_DIM
    aw = ATT_HEADS * HEAD_DIM
    o0 = 3 * gw
    o1 = o0 + gw
    o2 = o1 + 2 * GDN_HEADS
    wb = _bf(w_in)
    wqkv, wz = wb[:, :o0], wb[:, o0:o1]
    wbd_n = wb[:, o1:o2]
    wbd = jnp.zeros((d, LANES), BF16).at[:, :2 * GDN_HEADS].set(wbd_n)
    wbdt = wbd_n.T
    wq, wk, wv = wb[:, o2:o2 + aw], wb[:, o2 + aw:o2 + 2 * aw], wb[:, o2 + 2 * aw:o2 + 3 * aw]
    half = ROPE_DIMS // 2
    inv_freq = ROPE_THETA ** (-jnp.arange(half, dtype=F32) * 2.0 / ROPE_DIMS)
    invf = jnp.zeros((1, LANES), F32).at[0, :ROPE_DIMS].set(jnp.concatenate([inv_freq, inv_freq]))
    spb = seq // tm
    row = lambda i: (i, 0)
    const = lambda i: (0, 0)
    return pl.pallas_call(
        _inproj_kernel,
        out_shape=(jax.ShapeDtypeStruct((t, o0), F32), jax.ShapeDtypeStruct((t, gw), F32),
                   jax.ShapeDtypeStruct((t, LANES), F32), jax.ShapeDtypeStruct((t // seq, SUBLANES, seq), F32),
                   jax.ShapeDtypeStruct((t, aw), F32), jax.ShapeDtypeStruct((t, aw), F32),
                   jax.ShapeDtypeStruct((t, aw), F32)),
        grid=(t // tm,),
        in_specs=[pl.BlockSpec((tm, d), row),
                  pl.BlockSpec((1, 6, d), lambda i: (i // spb, 0, 0)),
                  pl.BlockSpec((tm, 1), row),
                  pl.BlockSpec((1, LANES), const),
                  pl.BlockSpec((d, o0), const), pl.BlockSpec((d, gw), const),
                  pl.BlockSpec((d, LANES), const), pl.BlockSpec((2 * GDN_HEADS, d), const),
                  pl.BlockSpec((d, aw), const), pl.BlockSpec((d, aw), const),
                  pl.BlockSpec((d, aw), const)],
        out_specs=(pl.BlockSpec((tm, o0), row), pl.BlockSpec((tm, gw), row),
                   pl.BlockSpec((tm, LANES), row),
                   pl.BlockSpec((1, SUBLANES, tm), lambda i: (i // spb, 0, i % spb)),
                   pl.BlockSpec((tm, aw), row), pl.BlockSpec((tm, aw), row),
                   pl.BlockSpec((tm, aw), row)),
        compiler_params=_params(("parallel",)),
        name="inproj",
    )(xf, mod, pos, invf, wqkv, wz, wbd, wbdt, wq, wk, wv)


def _gdn_kernel(qkv_ref, z_ref, bdc_ref, bdr_ref, convw_ref, gpc_ref, gpr_ref, nw_ref, o_ref,
                cbuf, q_s, k_s, v_s, u_s, w_s, qd_s, kd_s, qk_s, a_s, x_s, y_s, st_ref):
    nbat, ts = qkv_ref.shape[0], qkv_ref.shape[1]
    nb = ts // GDN_BLOCK
    halo = SUBLANES

    @pl.when(pl.program_id(0) == 0)
    def _():
        cbuf[:, 0:halo, :] = jnp.zeros((nbat, halo, cbuf.shape[2]), F32)
        st_ref[...] = jnp.zeros(st_ref.shape, F32)

    ti = lax.broadcasted_iota(jnp.int32, (ts, ts), 0)
    tj = lax.broadcasted_iota(jnp.int32, (ts, ts), 1)
    same = (ti // GDN_BLOCK) == (tj // GDN_BLOCK)
    m_low = jnp.where(same & (tj <= ti), 1.0, 0.0).astype(F32)
    m_up = jnp.where(same & (ti <= tj), 1.0, 0.0).astype(F32)
    m_same = jnp.where(same, 1.0, 0.0).astype(F32)

    beta_c, gc_c, gt_c, gc_r = [], [], [], []
    for bb in range(nbat):
        cbuf[bb, halo:halo + ts, :] = qkv_ref[bb]
        for s in range(3 * GDN_HEADS):
            sl = slice(s * HEAD_DIM, (s + 1) * HEAD_DIM)
            off = halo - (CONV_WIDTH - 1)
            acc = convw_ref[0:1, sl] * cbuf[bb, off:off + ts, sl]
            for j in range(1, CONV_WIDTH):
                acc = acc + convw_ref[j:j + 1, sl] * cbuf[bb, off + j:off + j + ts, sl]
            y = _silu(acc)
            if s < 2 * GDN_HEADS:
                y = y * lax.rsqrt(jnp.sum(y * y, axis=-1, keepdims=True) + RMS_EPS)
            if s < GDN_HEADS:
                y = y * (HEAD_DIM ** -0.5)
            dst = (q_s, k_s, v_s)[s // GDN_HEADS]
            hs = s % GDN_HEADS
            dst[bb, :, hs * HEAD_DIM:(hs + 1) * HEAD_DIM] = y
        cbuf[bb, 0:halo, :] = cbuf[bb, ts:ts + halo, :]

        bdc = bdc_ref[bb]
        beta_c.append(_sigmoid(bdc))
        g_c = -jnp.exp(gpc_ref[0:1, :]) * _softplus(bdc + gpc_ref[1:2, :])
        g_r = -jnp.exp(gpr_ref[:, 0:1]) * _softplus(bdr_ref[bb] + gpr_ref[:, 1:2])
        gc_c.append(_mm_f32(m_low, g_c))
        gt_c.append(_mm_f32(m_same, g_c))
        gc_r.append(_mm_f32(g_r, m_up))

    ii = lax.broadcasted_iota(jnp.int32, (GDN_BLOCK, GDN_BLOCK), 0)
    jj = lax.broadcasted_iota(jnp.int32, (GDN_BLOCK, GDN_BLOCK), 1)
    lower = jj <= ii
    strict = jj < ii
    eye = jnp.where(ii == jj, 1.0, 0.0).astype(F32)
    levels = []
    b = 1
    while b < GDN_BLOCK:
        levels.append(((ii // b) == (jj // b) + 1) & (((jj // b) % 2) == 0))
        b *= 2

    chains = [(bb, j, hh) for bb in range(nbat) for j in range(nb) for hh in range(GDN_HEADS)]

    def tile_of(bb, j, hh):
        return bb, slice(j * GDN_BLOCK, (j + 1) * GDN_BLOCK), slice(hh * HEAD_DIM, (hh + 1) * HEAD_DIM)

    for c, (bb, j, hh) in enumerate(chains):
        blk = tile_of(bb, j, hh)
        rows = blk[1]
        q = q_s[blk]
        k = k_s[blk]
        beta = beta_c[bb][rows, hh:hh + 1]
        gcc = gc_c[bb][rows, GDN_HEADS + hh:GDN_HEADS + hh + 1]
        gtc = gt_c[bb][rows, GDN_HEADS + hh:GDN_HEADS + hh + 1]
        gcr = gc_r[bb][GDN_HEADS + hh:GDN_HEADS + hh + 1, rows]
        kb = k * beta
        eg = jnp.exp(gcc)
        dm = jnp.where(lower, jnp.exp(gcc - gcr), 0.0)
        a = jnp.where(strict, _mm_nt(kb, k) * dm, 0.0)
        a_s[c] = a
        x_s[c] = eye - jnp.where(levels[0], a, 0.0)
        u_s[blk] = v_s[blk] * beta
        w_s[blk] = kb * eg
        qd_s[blk] = q * eg
        kd_s[blk] = k * jnp.exp(gtc - gcc)
        qk_s[blk] = _mm_nt(q, k) * dm

    for lm in levels[1:]:
        for c in range(len(chains)):
            y_s[c] = _mm(x_s[c], jnp.where(lm, a_s[c], 0.0))
        for c in range(len(chains)):
            xc = x_s[c]
            x_s[c] = xc - _mm(y_s[c], xc)

    for c, (bb, j, hh) in enumerate(chains):
        blk = tile_of(bb, j, hh)
        sol = _mm(x_s[c], jnp.concatenate([u_s[blk], w_s[blk]], axis=1))
        u_s[blk] = sol[:, :HEAD_DIM]
        w_s[blk] = sol[:, HEAD_DIM:]

    for j in range(nb):
        for bb in range(nbat):
            for hh in range(GDN_HEADS):
                blk = tile_of(bb, j, hh)
                si = bb * GDN_HEADS + hh
                state = st_ref[si]
                proj = _mm(jnp.concatenate([w_s[blk], qd_s[blk]], axis=0), state)
                v_new = u_s[blk] - proj[:GDN_BLOCK]
                o = proj[GDN_BLOCK:] + _mm(qk_s[blk], v_new)
                g_last = jnp.exp(gt_c[bb][j * GDN_BLOCK:j * GDN_BLOCK + 1,
                                          GDN_HEADS + hh:GDN_HEADS + hh + 1])
                st_ref[si] = state * g_last + _mm_tn(kd_s[blk], v_new)
                on = o * lax.rsqrt(jnp.mean(o * o, axis=-1, keepdims=True) + RMS_EPS) * nw_ref[...]
                o_ref[blk] = (on * _silu(z_ref[blk])).astype(o_ref.dtype)


def _gdn(qkv, z, bdc, bdr, conv_w, a_log, dt_bias, norm_w):
    bsz, seq, _ = qkv.shape
    ts = GDN_TILE
    gw = GDN_HEADS * HEAD_DIM
    zeros4 = jnp.zeros((GDN_HEADS,), F32)
    al = jnp.concatenate([zeros4, a_log])
    db = jnp.concatenate([zeros4, dt_bias])
    gpc = jnp.zeros((2, LANES), F32).at[0, :2 * GDN_HEADS].set(al).at[1, :2 * GDN_HEADS].set(db)
    gpr = jnp.stack([al, db], axis=1)
    row = lambda i: (0, i, 0)
    const = lambda i: (0, 0)
    n_chains = bsz * (ts // GDN_BLOCK) * GDN_HEADS
    tile = pltpu.VMEM((bsz, ts, gw), F32)
    mats = pltpu.VMEM((n_chains, GDN_BLOCK, GDN_BLOCK), F32)
    return pl.pallas_call(
        _gdn_kernel,
        out_shape=jax.ShapeDtypeStruct((bsz, seq, gw), BF16),
        grid=(seq // ts,),
        in_specs=[pl.BlockSpec((bsz, ts, 3 * gw), row), pl.BlockSpec((bsz, ts, gw), row),
                  pl.BlockSpec((bsz, ts, LANES), row),
                  pl.BlockSpec((bsz, SUBLANES, ts), lambda i: (0, 0, i)),
                  pl.BlockSpec((CONV_WIDTH, 3 * gw), const),
                  pl.BlockSpec((2, LANES), const), pl.BlockSpec((2 * GDN_HEADS, 2), const),
                  pl.BlockSpec((1, HEAD_DIM), const)],
        out_specs=pl.BlockSpec((bsz, ts, gw), row),
        scratch_shapes=[pltpu.VMEM((bsz, ts + 2 * SUBLANES, 3 * gw), F32),
                        tile, tile, tile, tile, tile, tile, tile, tile,
                        mats, mats, mats,
                        pltpu.VMEM((bsz * GDN_HEADS, HEAD_DIM, HEAD_DIM), F32)],
        compiler_params=_params(("arbitrary",)),
        name="gdn",
    )(qkv, z, bdc, bdr, conv_w, gpc, gpr, norm_w.reshape(1, HEAD_DIM))


def _attn_kernel(q_ref, k_ref, v_ref, nw_ref, o_ref,
                 q4, k4, v4, ktail, vtail, m_s, l_s, acc_s, tmp_s, nat_s):
    tq = q_ref.shape[0]
    blk = ATT_BLOCK
    sub = 4
    nq = tq // sub
    t = pl.program_id(2)
    slot = t % 2
    other = 1 - slot
    qi = lax.broadcasted_iota(jnp.int32, (blk, 2 * blk), 0)
    kj = lax.broadcasted_iota(jnp.int32, (blk, 2 * blk), 1)
    band = (kj >= qi) & (kj <= qi + blk)
    first_lo = jnp.where(t > 0, 0, blk)

    @pl.when(t == 0)
    def _():
        k4[...] = jnp.zeros(k4.shape, F32)
        v4[...] = jnp.zeros(v4.shape, F32)
        ktail[...] = jnp.zeros(ktail.shape, F32)
        vtail[...] = jnp.zeros(vtail.shape, F32)

    for r in range(sub):
        rows = slice(r * nq, (r + 1) * nq)
        src = pl.ds(r, nq, stride=sub)
        q4[rows, :] = q_ref[src, :]
        k4[slot, rows, :] = k_ref[src, :]
        v4[slot, rows, :] = v_ref[src, :]

    def block_stats(q, kcat, vcat, from_prev_tile):
        s = _mm_nt(q, kcat)
        mask = band & (kj >= first_lo) if from_prev_tile else band
        s = jnp.where(mask, s, NEG)
        m = jnp.max(s, axis=-1, keepdims=True)
        p = jnp.exp(s - m)
        l = jnp.sum(p, axis=-1, keepdims=True)
        return m, l, _mm(p, vcat)

    def merge(dst, m, l, o):
        m_old = m_s[dst, :]
        m_new = jnp.maximum(m_old, m)
        w_old = jnp.exp(m_old - m_new)
        w_cur = jnp.exp(m - m_new)
        m_s[dst, :] = m_new
        l_s[dst, :] = w_old * l_s[dst, :] + w_cur * l
        acc_s[dst, :] = w_old * acc_s[dst, :] + w_cur * o

    for jb in range(tq // blk):
        cur = slice(jb * blk, (jb + 1) * blk)
        if jb > 0:
            kcat = k_ref[(jb - 1) * blk:(jb + 1) * blk, :]
            vcat = v_ref[(jb - 1) * blk:(jb + 1) * blk, :]
        else:
            kcat = jnp.concatenate([ktail[...], k_ref[cur, :]], axis=0)
            vcat = jnp.concatenate([vtail[...], v_ref[cur, :]], axis=0)
        m, l, o = block_stats(q_ref[cur, :], kcat, vcat, jb == 0)
        tmp_s[0] = jnp.broadcast_to(m, (blk, HEAD_DIM))
        tmp_s[1] = jnp.broadcast_to(l, (blk, HEAD_DIM))
        tmp_s[2] = o
        per = blk // sub
        for r in range(sub):
            dst = slice(r * nq + jb * per, r * nq + (jb + 1) * per)
            src = pl.ds(r, per, stride=sub)
            m_s[dst, :] = tmp_s[0, src, :]
            l_s[dst, :] = tmp_s[1, src, :]
            acc_s[dst, :] = tmp_s[2, src, :]

    for r in range(sub):
        for jb in range(nq // blk):
            base = r * nq + jb * blk
            cur = slice(base, base + blk)
            if jb > 0:
                kcat = k4[slot, base - blk:base + blk, :]
                vcat = v4[slot, base - blk:base + blk, :]
            else:
                last = slice((r + 1) * nq - blk, (r + 1) * nq)
                kcat = jnp.concatenate([k4[other, last, :], k4[slot, cur, :]], axis=0)
                vcat = jnp.concatenate([v4[other, last, :], v4[slot, cur, :]], axis=0)
            m, l, o = block_stats(q4[cur, :], kcat, vcat, jb == 0)
            merge(cur, m, l, o)

    for c in range(16):
        sl = pl.ds((c % sub) * nq + c // sub, blk, stride=sub)
        kcat = jnp.concatenate([k4[other, sl, :], k4[slot, sl, :]], axis=0)
        vcat = jnp.concatenate([v4[other, sl, :], v4[slot, sl, :]], axis=0)
        m, l, o = block_stats(q4[sl, :], kcat, vcat, True)
        merge(sl, m, l, o)

    out = acc_s[...] / l_s[...]
    out = out * lax.rsqrt(jnp.mean(out * out, axis=-1, keepdims=True) + RMS_EPS) * nw_ref[...]
    for r in range(sub):
        nat_s[pl.ds(r, nq, stride=sub), :] = out[r * nq:(r + 1) * nq]
    o_ref[...] = nat_s[...].astype(o_ref.dtype)
    ktail[...] = k_ref[tq - blk:tq, :]
    vtail[...] = v_ref[tq - blk:tq, :]


def _attn(qb, kb, vb, norm_w, bsz, seq):
    t = qb.shape[0]
    tq = ATT_TILE
    spb = seq // tq
    cur = lambda b, h, i: (b * spb + i, h)
    blk = pl.BlockSpec((tq, HEAD_DIM), cur)
    tile = pltpu.VMEM((tq, HEAD_DIM), F32)
    ring = pltpu.VMEM((2, tq, HEAD_DIM), F32)
    tail = pltpu.VMEM((ATT_BLOCK, HEAD_DIM), F32)
    return pl.pallas_call(
        _attn_kernel,
        out_shape=jax.ShapeDtypeStruct((t, ATT_HEADS * HEAD_DIM), BF16),
        grid=(bsz, ATT_HEADS, spb),
        in_specs=[blk, blk, blk, pl.BlockSpec((1, HEAD_DIM), lambda b, h, i: (0, 0))],
        out_specs=blk,
        scratch_shapes=[tile, ring, ring, tail, tail, tile, tile, tile,
                        pltpu.VMEM((3, ATT_BLOCK, HEAD_DIM), F32), tile],
        compiler_params=_params(("parallel", "parallel", "arbitrary")),
        name="attn",
    )(qb, kb, vb, norm_w.reshape(1, HEAD_DIM))


def _layer_norm(y, g, b):
    mu = jnp.mean(y, axis=-1, keepdims=True)
    yc = y - mu
    var = jnp.mean(yc * yc, axis=-1, keepdims=True)
    return yc * lax.rsqrt(var + LN_EPS) * g + b


def _outproj_kernel(oa_ref, ob_ref, x_ref, mod_ref, woa_ref, wob_ref, g_ref, b_ref,
                    wrh_ref, wrl_ref, br_ref,
                    x1_ref, h2_ref, ri_ref, rg_ref, cnt_ref, run_s):
    @pl.when(pl.program_id(0) == 0)
    def _():
        run_s[...] = jnp.zeros(run_s.shape, F32)

    gate1 = mod_ref[0, 2:3, :]
    shift2 = mod_ref[0, 3:4, :]
    scale2 = mod_ref[0, 4:5, :]
    mix = (jnp.dot(oa_ref[...], woa_ref[...], preferred_element_type=F32)
           + jnp.dot(ob_ref[...], wob_ref[...], preferred_element_type=F32))
    x1 = _layer_norm(DEEPNORM_ALPHA * x_ref[...] + (1.0 + gate1) * mix, g_ref[...], b_ref[...])
    x1_ref[...] = x1
    h2 = x1 * (1.0 + scale2) + shift2
    h2_ref[...] = h2

    hi = _bf(h2)
    lo = _bf(h2 - hi.astype(F32))
    logits = (jnp.dot(hi, wrh_ref[...], preferred_element_type=F32)
              + jnp.dot(hi, wrl_ref[...], preferred_element_type=F32)
              + jnp.dot(lo, wrh_ref[...], preferred_element_type=F32)) + br_ref[...]
    tm = logits.shape[0]
    lane = lax.broadcasted_iota(jnp.int32, (tm, LANES), 1)
    lg = jnp.where(lane < N_GROUPS, logits, NEG)
    mg = jnp.max(lg, axis=-1, keepdims=True)
    grp = jnp.min(jnp.where(lg == mg, lane, LANES), axis=-1, keepdims=True)
    gate_grp = 1.0 / jnp.sum(jnp.exp(lg - mg), axis=-1, keepdims=True)
    eidx = lane - N_GROUPS
    sel = (eidx >= 0) & (eidx < N_EXPERTS) & ((eidx // EXPERTS_PER_GROUP) == grp)
    le = jnp.where(sel, logits, NEG)
    v1 = jnp.max(le, axis=-1, keepdims=True)
    i1 = jnp.min(jnp.where(le == v1, lane, LANES), axis=-1, keepdims=True)
    le2 = jnp.where(lane == i1, NEG, le)
    v2 = jnp.max(le2, axis=-1, keepdims=True)
    i2 = jnp.min(jnp.where(le2 == v2, lane, LANES), axis=-1, keepdims=True)
    e21 = jnp.exp(v2 - v1)
    g1 = gate_grp / (1.0 + e21)
    g2 = gate_grp * e21 / (1.0 + e21)

    oh1 = lane == i1
    oh2 = lane == i2
    onehot = jnp.where(oh1 | oh2, 1.0, 0.0).astype(F32)
    ti = lax.broadcasted_iota(jnp.int32, (tm, tm), 0)
    tj = lax.broadcasted_iota(jnp.int32, (tm, tm), 1)
    before = jnp.where(tj < ti, 1.0, 0.0).astype(F32)
    tot = _mm(before, onehot) + run_s[...]
    r1 = jnp.sum(jnp.where(oh1, tot, 0.0), axis=-1, keepdims=True)
    r2 = jnp.sum(jnp.where(oh2, tot, 0.0), axis=-1, keepdims=True)
    run_s[...] = run_s[...] + jnp.sum(onehot, axis=0, keepdims=True)
    cnt_ref[...] = run_s[...]

    cols = jnp.where(lane == 0, (i1 - N_GROUPS).astype(F32), 0.0)
    cols = jnp.where(lane == 1, (i2 - N_GROUPS).astype(F32), cols)
    r1_hi = jnp.floor(r1 * (1.0 / 256.0))
    r2_hi = jnp.floor(r2 * (1.0 / 256.0))
    cols = jnp.where(lane == 2, r1_hi, cols)
    cols = jnp.where(lane == 3, r1 - 256.0 * r1_hi, cols)
    cols = jnp.where(lane == 4, r2_hi, cols)
    cols = jnp.where(lane == 5, r2 - 256.0 * r2_hi, cols)
    pick = jnp.where(lax.broadcasted_iota(jnp.int32, (SUBLANES, LANES), 0)
                     == lax.broadcasted_iota(jnp.int32, (SUBLANES, LANES), 1), 1.0, 0.0).astype(F32)
    ri_ref[...] = _mm_nt(pick, cols).astype(jnp.int32)
    rg_ref[...] = jnp.where(lane == 0, g1, jnp.where(lane == 1, g2, 0.0))


def _outproj(oa, ob, xf, mod, w_o, ln_g, ln_b, w_rg, b_rg, w_re, b_re, seq):
    t, d = xf.shape
    tm = ROW_TILE
    gw = oa.shape[1]
    wo = _bf(w_o)
    wr = jnp.zeros((d, LANES), F32).at[:, :N_GROUPS].set(w_rg).at[:, N_GROUPS:N_GROUPS + N_EXPERTS].set(w_re)
    wrh = _bf(wr)
    wrl = _bf(wr - wrh.astype(F32))
    br = jnp.zeros((1, LANES), F32).at[0, :N_GROUPS].set(b_rg).at[0, N_GROUPS:N_GROUPS + N_EXPERTS].set(b_re)
    spb = seq // tm
    row = lambda i: (i, 0)
    const = lambda i: (0, 0)
    return pl.pallas_call(
        _outproj_kernel,
        out_shape=(jax.ShapeDtypeStruct((t, d), F32), jax.ShapeDtypeStruct((t, d), F32),
                   jax.ShapeDtypeStruct((SUBLANES, t), jnp.int32), jax.ShapeDtypeStruct((t, LANES), F32),
                   jax.ShapeDtypeStruct((1, LANES), F32)),
        grid=(t // tm,),
        in_specs=[pl.BlockSpec((tm, gw), row), pl.BlockSpec((tm, gw), row), pl.BlockSpec((tm, d), row),
                  pl.BlockSpec((1, 6, d), lambda i: (i // spb, 0, 0)),
                  pl.BlockSpec((gw, d), const), pl.BlockSpec((gw, d), const),
                  pl.BlockSpec((1, d), const), pl.BlockSpec((1, d), const),
                  pl.BlockSpec((d, LANES), const), pl.BlockSpec((d, LANES), const),
                  pl.BlockSpec((1, LANES), const)],
        out_specs=(pl.BlockSpec((tm, d), row), pl.BlockSpec((tm, d), row),
                   pl.BlockSpec((SUBLANES, tm), lambda i: (0, i)), pl.BlockSpec((tm, LANES), row),
                   pl.BlockSpec((1, LANES), const)),
        scratch_shapes=[pltpu.VMEM((1, LANES), F32)],
        compiler_params=_params(("arbitrary",)),
        name="outproj",
    )(oa, ob, xf, mod, wo[:gw], wo[gw:], ln_g.reshape(1, d), ln_b.reshape(1, d), wrh, wrl, br)


def _dispatch_kernel(d1_ref, d2_ref, h_ref, xs_ref, sem):
    tm = h_ref.shape[0]
    i = pl.program_id(0)

    def row_copy(tk, dest):
        return pltpu.make_async_copy(h_ref.at[pl.ds(tk, 1), :], xs_ref.at[pl.ds(dest, 1), :], sem)

    def issue(g, carry):
        for u in range(DMA_UNROLL):
            tk = g * DMA_UNROLL + u
            tok = i * tm + tk
            row_copy(tk, d1_ref[tok]).start(priority=0)
            row_copy(tk, d2_ref[tok]).start(priority=1)
        return carry

    lax.fori_loop(0, tm // DMA_UNROLL, issue, 0)

    tile_copy = pltpu.make_async_copy(h_ref, xs_ref.at[pl.ds(0, tm), :], sem)
    tile_copy.wait()
    tile_copy.wait()


def _dispatch(h2, d1, d2):
    t, d = h2.shape
    tm = DISPATCH_TILE
    return pl.pallas_call(
        _dispatch_kernel,
        out_shape=jax.ShapeDtypeStruct((2 * t, d), F32),
        grid_spec=pltpu.PrefetchScalarGridSpec(
            num_scalar_prefetch=2,
            grid=(t // tm,),
            in_specs=[pl.BlockSpec((tm, d), lambda i, *_: (i, 0))],
            out_specs=pl.BlockSpec(memory_space=pl.ANY),
            scratch_shapes=[pltpu.SemaphoreType.DMA]),
        compiler_params=_params(("arbitrary",)),
        name="dispatch",
    )(d1, d2, h2)


def _experts_kernel(wb_ref, we_ref, lo_ref, hi_ref, nw_ref, xs_ref, wg_ref, wu_ref, wd_ref, ys_ref,
                    wg_s, wu_s, wd_s):
    w = pl.program_id(0)

    @pl.when((w == 0) | (we_ref[w] != we_ref[jnp.maximum(w - 1, 0)]))
    def _():
        wg_s[...] = _bf(wg_ref[0])
        wu_s[...] = _bf(wu_ref[0])
        wd_s[...] = _bf(wd_ref[0])

    @pl.when(w < nw_ref[0])
    def _():
        x = _bf(xs_ref[...])
        hid = (_silu(jnp.dot(x, wg_s[...], preferred_element_type=F32))
               * jnp.dot(x, wu_s[...], preferred_element_type=F32))
        y = jnp.dot(_bf(hid), wd_s[...], preferred_element_type=F32)
        row = lax.broadcasted_iota(jnp.int32, (y.shape[0], 1), 0)
        mine = (row >= lo_ref[w]) & (row < hi_ref[w])

        @pl.when(lo_ref[w] == 0)
        def _():
            ys_ref[...] = jnp.where(mine, y, 0.0)

        @pl.when(lo_ref[w] > 0)
        def _():
            ys_ref[...] = jnp.where(mine, y, ys_ref[...])


def _experts(xs, item_block, item_expert, item_lo, item_hi, n_items, w_gate, w_up, w_down):
    n_slots, d = xs.shape
    ff = w_gate.shape[2]
    bm = EXPERT_BLOCK
    slot = lambda w, wb, we, lo, hi, nw: (wb[w], 0)
    wsel = lambda w, wb, we, lo, hi, nw: (we[w], 0, 0)
    return pl.pallas_call(
        _experts_kernel,
        out_shape=jax.ShapeDtypeStruct((n_slots, d), F32),
        grid_spec=pltpu.PrefetchScalarGridSpec(
            num_scalar_prefetch=5,
            grid=(item_block.shape[0],),
            in_specs=[pl.BlockSpec((bm, d), slot),
                      pl.BlockSpec((1, d, ff), wsel), pl.BlockSpec((1, d, ff), wsel),
                      pl.BlockSpec((1, ff, d), wsel)],
            out_specs=pl.BlockSpec((bm, d), slot),
            scratch_shapes=[pltpu.VMEM((d, ff), BF16), pltpu.VMEM((d, ff), BF16),
                            pltpu.VMEM((ff, d), BF16)]),
        compiler_params=_params(("arbitrary",)),
        name="experts",
    )(item_block, item_expert, item_lo, item_hi, n_items, xs, w_gate, w_up, w_down)


def _combine_kernel(d1_ref, d2_ref, ys_ref, rg_ref, x1_ref, mod_ref, g_ref, b_ref, o_ref, ya, yb, sem):
    tm = x1_ref.shape[0]
    i = pl.program_id(0)
    n = pl.num_programs(0)

    def row_copy(dest, buf, slot, tk):
        return pltpu.make_async_copy(ys_ref.at[pl.ds(dest, 1), :], buf.at[slot, pl.ds(tk, 1), :],
                                     sem.at[slot])

    def gather_tile(step, slot):
        def issue(g, carry):
            for u in range(DMA_UNROLL):
                tk = g * DMA_UNROLL + u
                tok = step * tm + tk
                row_copy(d1_ref[tok], ya, slot, tk).start(priority=0)
                row_copy(d2_ref[tok], yb, slot, tk).start(priority=1)
            return carry

        lax.fori_loop(0, tm // DMA_UNROLL, issue, 0)

    @pl.when(i == 0)
    def _():
        gather_tile(0, 0)

    @pl.when(i + 1 < n)
    def _():
        gather_tile(i + 1, (i + 1) % 2)

    slot = i % 2
    pltpu.make_async_copy(ys_ref.at[pl.ds(0, tm), :], ya.at[slot], sem.at[slot]).wait()
    pltpu.make_async_copy(ys_ref.at[pl.ds(0, tm), :], yb.at[slot], sem.at[slot]).wait()

    gate2 = mod_ref[0, 5:6, :]
    rg = rg_ref[...]
    y = rg[:, 0:1] * ya[slot] + rg[:, 1:2] * yb[slot]
    o_ref[...] = _layer_norm(DEEPNORM_ALPHA * x1_ref[...] + (1.0 + gate2) * y, g_ref[...], b_ref[...])


def _combine(ys, d1, d2, rg, x1, mod, ln_g, ln_b, seq):
    t, d = x1.shape
    tm = ROW_TILE
    spb = seq // tm
    row = lambda i, *_: (i, 0)
    const = lambda i, *_: (0, 0)
    buf = pltpu.VMEM((2, tm, d), F32)
    return pl.pallas_call(
        _combine_kernel,
        out_shape=jax.ShapeDtypeStruct((t, d), F32),
        grid_spec=pltpu.PrefetchScalarGridSpec(
            num_scalar_prefetch=2,
            grid=(t // tm,),
            in_specs=[pl.BlockSpec(memory_space=pl.ANY),
                      pl.BlockSpec((tm, LANES), row), pl.BlockSpec((tm, d), row),
                      pl.BlockSpec((1, 6, d), lambda i, *_: (i // spb, 0, 0)),
                      pl.BlockSpec((1, d), const), pl.BlockSpec((1, d), const)],
            out_specs=pl.BlockSpec((tm, d), row),
            scratch_shapes=[buf, buf, pltpu.SemaphoreType.DMA((2,))]),
        compiler_params=_params(("arbitrary",)),
        name="combine",
    )(d1, d2, ys, rg, x1, mod, ln_g.reshape(1, d), ln_b.reshape(1, d))


def _layer(x, c, positions, w_ada, b_ada, w_in, conv_w, a_log, dt_bias, gdn_norm_w, attn_norm_w,
           w_o, ln1_g, ln1_b, w_rg, b_rg, w_re, b_re, w_gate, w_up, w_down, ln2_g, ln2_b):
    bsz, seq, d = x.shape
    t = bsz * seq
    xf = x.reshape(t, d)
    mod = _ada(c, w_ada, b_ada)
    qkv, z, bdc, bdr, qb, kb, vb = _inproj(xf, mod, positions.reshape(t, 1), w_in, seq)
    gw = GDN_HEADS * HEAD_DIM
    oa = _gdn(qkv.reshape(bsz, seq, 3 * gw), z.reshape(bsz, seq, gw), bdc.reshape(bsz, seq, LANES), bdr,
              conv_w, a_log, dt_bias, gdn_norm_w).reshape(t, gw)
    ob = _attn(qb, kb, vb, attn_norm_w, bsz, seq)
    x1, h2, ri, rg, cnt = _outproj(oa, ob, xf, mod, w_o, ln1_g, ln1_b, w_rg, b_rg, w_re, b_re, seq)

    bm = EXPERT_BLOCK
    counts = cnt[0, N_GROUPS:N_GROUPS + N_EXPERTS].astype(jnp.int32)
    seg_end = jnp.cumsum(counts)
    seg_start = seg_end - counts
    first_blk = seg_start // bm
    n_per = jnp.where(counts > 0, (seg_end - 1) // bm - first_blk + 1, 0)
    item_end = jnp.cumsum(n_per)
    n_items = item_end[-1:]
    max_items = (2 * t) // bm + N_EXPERTS - 1
    w_idx = jnp.minimum(jnp.arange(max_items, dtype=jnp.int32), n_items[0] - 1)
    item_expert = jnp.minimum(jnp.sum(item_end[None, :] <= w_idx[:, None], axis=1), N_EXPERTS - 1).astype(jnp.int32)
    item_block = first_blk[item_expert] + w_idx - (item_end - n_per)[item_expert]
    item_lo = jnp.maximum(seg_start[item_expert] - item_block * bm, 0)
    item_hi = jnp.minimum(seg_end[item_expert] - item_block * bm, bm)
    d1 = seg_start[ri[0]] + ri[2] * 256 + ri[3]
    d2 = seg_start[ri[1]] + ri[4] * 256 + ri[5]

    xs = _dispatch(h2, d1, d2)
    ys = _experts(xs, item_block, item_expert, item_lo, item_hi, n_items, w_gate, w_up, w_down)
    out = _combine(ys, d1, d2, rg, x1, mod, ln2_g, ln2_b, seq)
    return out.reshape(bsz, seq, d)


def kernel(x, c, positions, w_ada, b_ada, w_in, conv_w, a_log, dt_bias, gdn_norm_w, attn_norm_w, w_o, ln1_g, ln1_b, w_router_group, b_router_group, w_router_expert, b_router_expert, w_gate, w_up, w_down, ln2_g, ln2_b):
    assert w_ada.shape[0] == DEPTH
    return _layer(x, c, positions, w_ada[0], b_ada[0], w_in[0], conv_w[0], a_log[0], dt_bias[0],
                  gdn_norm_w[0], attn_norm_w[0], w_o[0], ln1_g[0], ln1_b[0],
                  w_router_group[0], b_router_group[0], w_router_expert[0], b_router_expert[0],
                  w_gate[0], w_up[0], w_down[0], ln2_g[0], ln2_b[0])
```

```python
import functools
import math

import jax
import jax.numpy as jnp
from jax import lax
from jax.experimental import pallas as pl
from jax.experimental.pallas import tpu as pltpu

F32 = jnp.float32
BF16 = jnp.bfloat16
HIGHEST = lax.Precision.HIGHEST

GDN_HEADS = 4
ATT_HEADS = 4
HEAD_DIM = 128
CONV_WIDTH = 4
DILATED_PATTERNS = ((128, 1), (512, 4), (2048, 16))
ROPE_THETA = 500000.0
ROPE_DIMS = HEAD_DIM // 4
N_GROUPS = 4
EXPERTS_PER_GROUP = 8
N_EXPERTS = N_GROUPS * EXPERTS_PER_GROUP
DEPTH = 1
DEEPNORM_ALPHA = (2.0 * DEPTH) ** 0.25
LN_EPS = 1e-5
RMS_EPS = 1e-6

LANES = 128
SUBLANES = 8
VMEM_LIMIT = 48 * 1024 * 1024

GDN_BLOCK = 128
GDN_TILE = 256
ATT_BLOCK = 128
ATT_TILE = 2048
PROJ_TILE = 512
ROW_TILE = 256
DISPATCH_TILE = 1024
EXPERT_BLOCK = 256
DMA_UNROLL = 8
NEG = -1e30


def _bf(x):
    return x.astype(BF16)


def _mm(a, b):
    return jnp.dot(_bf(a), _bf(b), preferred_element_type=F32)


def _mm_nt(a, b):
    return lax.dot_general(_bf(a), _bf(b), (((1,), (1,)), ((), ())), preferred_element_type=F32)


def _mm_tn(a, b):
    return lax.dot_general(_bf(a), _bf(b), (((0,), (0,)), ((), ())), preferred_element_type=F32)


def _mm_f32(a, b):
    return jnp.dot(a, b, preferred_element_type=F32, precision=HIGHEST)


def _sigmoid(x):
    return 1.0 / (1.0 + jnp.exp(-x))


def _silu(x):
    return x * _sigmoid(x)


def _softplus(x):
    return jnp.maximum(x, 0.0) + jnp.log(1.0 + jnp.exp(-jnp.abs(x)))


def _params(sem):
    return pltpu.CompilerParams(dimension_semantics=sem, vmem_limit_bytes=VMEM_LIMIT)


def _ada_kernel(c_ref, w_ref, b_ref, o_ref):
    o_ref[...] = _mm_f32(_silu(c_ref[...]), w_ref[...]) + b_ref[...]


def _ada(c, w_ada, b_ada):
    bsz, d = c.shape
    n = w_ada.shape[1]
    tn = 512
    cp = jnp.zeros((SUBLANES, d), F32).at[:bsz].set(c)
    out = pl.pallas_call(
        _ada_kernel,
        out_shape=jax.ShapeDtypeStruct((SUBLANES, n), F32),
        grid=(n // tn,),
        in_specs=[pl.BlockSpec((SUBLANES, d), lambda j: (0, 0)),
                  pl.BlockSpec((d, tn), lambda j: (0, j)),
                  pl.BlockSpec((1, tn), lambda j: (0, j))],
        out_specs=pl.BlockSpec((SUBLANES, tn), lambda j: (0, j)),
        compiler_params=_params(("parallel",)),
        name="ada",
    )(cp, w_ada, b_ada.reshape(1, n))
    return out[:bsz].reshape(bsz, 6, d)


def _inproj_kernel(x_ref, mod_ref, pos_ref, invf_ref, convw_ref, wqkv_ref, wz_ref, wbd_ref, wbdt_ref,
                   wq_ref, wk_ref, wv_ref,
                   qkv_ref, z_ref, bdc_ref, bdr_ref, qb_ref, kb_ref, vb_ref, cbuf, *, steps_per_seq):
    tm = x_ref.shape[0]
    halo = SUBLANES

    @pl.when(pl.program_id(0) % steps_per_seq == 0)
    def _():
        cbuf[0:halo, :] = jnp.zeros((halo, cbuf.shape[1]), F32)

    ang = pos_ref[...].astype(F32) * invf_ref[...]
    cosv = jnp.cos(ang)
    sinv = jnp.sin(ang)
    lane = lax.broadcasted_iota(jnp.int32, (1, LANES), 1)
    half = ROPE_DIMS // 2
    first = lane < half
    sin_signed = jnp.where(first, -sinv, sinv)

    def rope(y):
        outs = []
        for hh in range(ATT_HEADS):
            yh = y[:, hh * HEAD_DIM:(hh + 1) * HEAD_DIM]
            partner = jnp.where(first, pltpu.roll(yh, LANES - half, 1), pltpu.roll(yh, half, 1))
            outs.append(yh * cosv + partner * sin_signed)
        return jnp.concatenate(outs, axis=1)

    def conv_slice(s):
        sl = slice(s * HEAD_DIM, (s + 1) * HEAD_DIM)
        off = halo - (CONV_WIDTH - 1)
        acc = convw_ref[0:1, sl] * cbuf[off:off + tm, sl]
        for j in range(1, CONV_WIDTH):
            acc = acc + convw_ref[j:j + 1, sl] * cbuf[off + j:off + j + tm, sl]
        y = _silu(acc)
        if s < 2 * GDN_HEADS:
            y = y * lax.rsqrt(jnp.sum(y * y, axis=-1, keepdims=True) + RMS_EPS)
        if s < GDN_HEADS:
            y = y * (HEAD_DIM ** -0.5)
        qkv_ref[:, sl] = y

    shift = mod_ref[0, 0:1, :]
    scale = mod_ref[0, 1:2, :]
    h = _bf(x_ref[...] * (1.0 + scale) + shift)
    cbuf[halo:halo + tm, :] = jnp.dot(h, wqkv_ref[...], preferred_element_type=F32)
    z_ref[...] = jnp.dot(h, wz_ref[...], preferred_element_type=F32)
    for s in range(0, 3):
        conv_slice(s)
    bdc_ref[...] = jnp.dot(h, wbd_ref[...], preferred_element_type=F32)
    bdr_ref[0] = lax.dot_general(wbdt_ref[...], h, (((1,), (1,)), ((), ())),
                                 preferred_element_type=F32)
    vb_ref[...] = jnp.dot(h, wv_ref[...], preferred_element_type=F32)
    for s in range(3, 6):
        conv_slice(s)
    qb = jnp.dot(h, wq_ref[...], preferred_element_type=F32)
    for s in range(6, 9):
        conv_slice(s)
    kb = jnp.dot(h, wk_ref[...], preferred_element_type=F32)
    for s in range(9, 12):
        conv_slice(s)
    cbuf[0:halo, :] = cbuf[tm:tm + halo, :]
    qb_ref[...] = rope(qb) * (HEAD_DIM ** -0.5)
    kb_ref[...] = rope(kb)


def _inproj(xf, mod, pos, w_in, conv_w, seq):
    t, d = xf.shape
    tm = PROJ_TILE
    gw = GDN_HEADS * HEAD_DIM
    aw = ATT_HEADS * HEAD_DIM
    o0 = 3 * gw
    o1 = o0 + gw
    o2 = o1 + 2 * GDN_HEADS
    wb = _bf(w_in)
    wqkv, wz = wb[:, :o0], wb[:, o0:o1]
    wbd_n = wb[:, o1:o2]
    wbd = jnp.zeros((d, LANES), BF16).at[:, :2 * GDN_HEADS].set(wbd_n)
    wbdt = wbd_n.T
    wq, wk, wv = wb[:, o2:o2 + aw], wb[:, o2 + aw:o2 + 2 * aw], wb[:, o2 + 2 * aw:o2 + 3 * aw]
    half = ROPE_DIMS // 2
    inv_freq = ROPE_THETA ** (-jnp.arange(half, dtype=F32) * 2.0 / ROPE_DIMS)
    invf = jnp.zeros((1, LANES), F32).at[0, :ROPE_DIMS].set(jnp.concatenate([inv_freq, inv_freq]))
    spb = seq // tm
    row = lambda i: (i, 0)
    const = lambda i: (0, 0)
    return pl.pallas_call(
        functools.partial(_inproj_kernel, steps_per_seq=spb),
        out_shape=(jax.ShapeDtypeStruct((t, o0), F32), jax.ShapeDtypeStruct((t, gw), F32),
                   jax.ShapeDtypeStruct((t, LANES), F32), jax.ShapeDtypeStruct((t // seq, SUBLANES, seq), F32),
                   jax.ShapeDtypeStruct((t, aw), F32), jax.ShapeDtypeStruct((t, aw), F32),
                   jax.ShapeDtypeStruct((t, aw), F32)),
        grid=(t // tm,),
        in_specs=[pl.BlockSpec((tm, d), row),
                  pl.BlockSpec((1, 6, d), lambda i: (i // spb, 0, 0)),
                  pl.BlockSpec((tm, 1), row),
                  pl.BlockSpec((1, LANES), const), pl.BlockSpec((CONV_WIDTH, o0), const),
                  pl.BlockSpec((d, o0), const), pl.BlockSpec((d, gw), const),
                  pl.BlockSpec((d, LANES), const), pl.BlockSpec((2 * GDN_HEADS, d), const),
                  pl.BlockSpec((d, aw), const), pl.BlockSpec((d, aw), const),
                  pl.BlockSpec((d, aw), const)],
        out_specs=(pl.BlockSpec((tm, o0), row), pl.BlockSpec((tm, gw), row),
                   pl.BlockSpec((tm, LANES), row),
                   pl.BlockSpec((1, SUBLANES, tm), lambda i: (i // spb, 0, i % spb)),
                   pl.BlockSpec((tm, aw), row), pl.BlockSpec((tm, aw), row),
                   pl.BlockSpec((tm, aw), row)),
        scratch_shapes=[pltpu.VMEM((tm + 2 * SUBLANES, o0), F32)],
        compiler_params=_params(("arbitrary",)),
        name="inproj",
    )(xf, mod, pos, invf, conv_w, wqkv, wz, wbd, wbdt, wq, wk, wv)


def _gdn_kernel(qkv_ref, z_ref, bdc_ref, bdr_ref, gpc_ref, gpr_ref, nw_ref, o_ref,
                u_s, w_s, qd_s, kd_s, qk_s, a_s, x_s, y_s, st_ref):
    nbat, ts = qkv_ref.shape[0], qkv_ref.shape[1]
    nb = ts // GDN_BLOCK
    gw = GDN_HEADS * HEAD_DIM

    @pl.when(pl.program_id(0) == 0)
    def _():
        st_ref[...] = jnp.zeros(st_ref.shape, F32)

    ti = lax.broadcasted_iota(jnp.int32, (ts, ts), 0)
    tj = lax.broadcasted_iota(jnp.int32, (ts, ts), 1)
    same = (ti // GDN_BLOCK) == (tj // GDN_BLOCK)
    m_low = jnp.where(same & (tj <= ti), 1.0, 0.0).astype(F32)
    m_up = jnp.where(same & (ti <= tj), 1.0, 0.0).astype(F32)
    m_same = jnp.where(same, 1.0, 0.0).astype(F32)

    beta_c, gc_c, gt_c, gc_r = [], [], [], []
    for bb in range(nbat):
        bdc = bdc_ref[bb]
        beta_c.append(_sigmoid(bdc))
        g_c = -jnp.exp(gpc_ref[0:1, :]) * _softplus(bdc + gpc_ref[1:2, :])
        g_r = -jnp.exp(gpr_ref[:, 0:1]) * _softplus(bdr_ref[bb] + gpr_ref[:, 1:2])
        gc_c.append(_mm_f32(m_low, g_c))
        gt_c.append(_mm_f32(m_same, g_c))
        gc_r.append(_mm_f32(g_r, m_up))

    ii = lax.broadcasted_iota(jnp.int32, (GDN_BLOCK, GDN_BLOCK), 0)
    jj = lax.broadcasted_iota(jnp.int32, (GDN_BLOCK, GDN_BLOCK), 1)
    lower = jj <= ii
    strict = jj < ii
    eye = jnp.where(ii == jj, 1.0, 0.0).astype(F32)
    levels = []
    b = 1
    while b < GDN_BLOCK:
        levels.append(((ii // b) == (jj // b) + 1) & (((jj // b) % 2) == 0))
        b *= 2

    chains = [(bb, j, hh) for bb in range(nbat) for j in range(nb) for hh in range(GDN_HEADS)]

    def tile_of(bb, j, hh):
        return bb, slice(j * GDN_BLOCK, (j + 1) * GDN_BLOCK), slice(hh * HEAD_DIM, (hh + 1) * HEAD_DIM)

    for c, (bb, j, hh) in enumerate(chains):
        blk = tile_of(bb, j, hh)
        rows = blk[1]
        col0 = hh * HEAD_DIM
        q = qkv_ref[bb, rows, col0:col0 + HEAD_DIM]
        k = qkv_ref[bb, rows, gw + col0:gw + col0 + HEAD_DIM]
        v = qkv_ref[bb, rows, 2 * gw + col0:2 * gw + col0 + HEAD_DIM]
        beta = beta_c[bb][rows, hh:hh + 1]
        gcc = gc_c[bb][rows, GDN_HEADS + hh:GDN_HEADS + hh + 1]
        gtc = gt_c[bb][rows, GDN_HEADS + hh:GDN_HEADS + hh + 1]
        gcr = gc_r[bb][GDN_HEADS + hh:GDN_HEADS + hh + 1, rows]
        kb = k * beta
        eg = jnp.exp(gcc)
        dm = jnp.where(lower, jnp.exp(gcc - gcr), 0.0)
        a = jnp.where(strict, _mm_nt(kb, k) * dm, 0.0)
        a_s[c] = a
        x_s[c] = eye - jnp.where(levels[0], a, 0.0)
        u_s[blk] = v * beta
        w_s[blk] = kb * eg
        qd_s[blk] = q * eg
        kd_s[blk] = k * jnp.exp(gtc - gcc)
        qk_s[blk] = _mm_nt(q, k) * dm

    for lm in levels[1:]:
        for c in range(len(chains)):
            y_s[c] = _mm(x_s[c], jnp.where(lm, a_s[c], 0.0))
        for c in range(len(chains)):
            xc = x_s[c]
            x_s[c] = xc - _mm(y_s[c], xc)

    for c, (bb, j, hh) in enumerate(chains):
        blk = tile_of(bb, j, hh)
        sol = _mm(x_s[c], jnp.concatenate([u_s[blk], w_s[blk]], axis=1))
        u_s[blk] = sol[:, :HEAD_DIM]
        w_s[blk] = sol[:, HEAD_DIM:]

    for j in range(nb):
        for bb in range(nbat):
            for hh in range(GDN_HEADS):
                blk = tile_of(bb, j, hh)
                si = bb * GDN_HEADS + hh
                state = st_ref[si]
                proj = _mm(jnp.concatenate([w_s[blk], qd_s[blk]], axis=0), state)
                v_new = u_s[blk] - proj[:GDN_BLOCK]
                o = proj[GDN_BLOCK:] + _mm(qk_s[blk], v_new)
                g_last = jnp.exp(gt_c[bb][j * GDN_BLOCK:j * GDN_BLOCK + 1,
                                          GDN_HEADS + hh:GDN_HEADS + hh + 1])
                st_ref[si] = state * g_last + _mm_tn(kd_s[blk], v_new)
                on = o * lax.rsqrt(jnp.mean(o * o, axis=-1, keepdims=True) + RMS_EPS) * nw_ref[...]
                o_ref[blk] = (on * _silu(z_ref[blk])).astype(o_ref.dtype)


def _gdn(qkv, z, bdc, bdr, a_log, dt_bias, norm_w):
    bsz, seq, _ = qkv.shape
    ts = GDN_TILE
    gw = GDN_HEADS * HEAD_DIM
    zeros4 = jnp.zeros((GDN_HEADS,), F32)
    al = jnp.concatenate([zeros4, a_log])
    db = jnp.concatenate([zeros4, dt_bias])
    gpc = jnp.zeros((2, LANES), F32).at[0, :2 * GDN_HEADS].set(al).at[1, :2 * GDN_HEADS].set(db)
    gpr = jnp.stack([al, db], axis=1)
    row = lambda i: (0, i, 0)
    const = lambda i: (0, 0)
    n_chains = bsz * (ts // GDN_BLOCK) * GDN_HEADS
    tile = pltpu.VMEM((bsz, ts, gw), F32)
    mats = pltpu.VMEM((n_chains, GDN_BLOCK, GDN_BLOCK), F32)
    return pl.pallas_call(
        _gdn_kernel,
        out_shape=jax.ShapeDtypeStruct((bsz, seq, gw), BF16),
        grid=(seq // ts,),
        in_specs=[pl.BlockSpec((bsz, ts, 3 * gw), row), pl.BlockSpec((bsz, ts, gw), row),
                  pl.BlockSpec((bsz, ts, LANES), row),
                  pl.BlockSpec((bsz, SUBLANES, ts), lambda i: (0, 0, i)),
                  pl.BlockSpec((2, LANES), const), pl.BlockSpec((2 * GDN_HEADS, 2), const),
                  pl.BlockSpec((1, HEAD_DIM), const)],
        out_specs=pl.BlockSpec((bsz, ts, gw), row),
        scratch_shapes=[tile, tile, tile, tile, tile,
                        mats, mats, mats,
                        pltpu.VMEM((bsz * GDN_HEADS, HEAD_DIM, HEAD_DIM), F32)],
        compiler_params=_params(("arbitrary",)),
        name="gdn",
    )(qkv, z, bdc, bdr, gpc, gpr, norm_w.reshape(1, HEAD_DIM))


def _attn_kernel(q_ref, k_ref, v_ref, nw_ref, o_ref,
                 q4, k4, v4, ktail, vtail, m_s, l_s, acc_s, tmp_s, nat_s):
    tq = q_ref.shape[0]
    blk = ATT_BLOCK
    sub = 4
    nq = tq // sub
    t = pl.program_id(2)
    slot = t % 2
    other = 1 - slot
    qi = lax.broadcasted_iota(jnp.int32, (blk, 2 * blk), 0)
    kj = lax.broadcasted_iota(jnp.int32, (blk, 2 * blk), 1)
    band = (kj >= qi) & (kj <= qi + blk)
    first_lo = jnp.where(t > 0, 0, blk)

    @pl.when(t == 0)
    def _():
        k4[...] = jnp.zeros(k4.shape, F32)
        v4[...] = jnp.zeros(v4.shape, F32)
        ktail[...] = jnp.zeros(ktail.shape, F32)
        vtail[...] = jnp.zeros(vtail.shape, F32)

    for r in range(sub):
        rows = slice(r * nq, (r + 1) * nq)
        src = pl.ds(r, nq, stride=sub)
        q4[rows, :] = q_ref[src, :]
        k4[slot, rows, :] = k_ref[src, :]
        v4[slot, rows, :] = v_ref[src, :]

    def block_stats(q, kcat, vcat, from_prev_tile):
        s = _mm_nt(q, kcat)
        mask = band & (kj >= first_lo) if from_prev_tile else band
        s = jnp.where(mask, s, NEG)
        m = jnp.max(s, axis=-1, keepdims=True)
        p = jnp.exp(s - m)
        l = jnp.sum(p, axis=-1, keepdims=True)
        return m, l, _mm(p, vcat)

    def merge(dst, m, l, o):
        m_old = m_s[dst, :]
        m_new = jnp.maximum(m_old, m)
        w_old = jnp.exp(m_old - m_new)
        w_cur = jnp.exp(m - m_new)
        m_s[dst, :] = m_new
        l_s[dst, :] = w_old * l_s[dst, :] + w_cur * l
        acc_s[dst, :] = w_old * acc_s[dst, :] + w_cur * o

    for jb in range(tq // blk):
        cur = slice(jb * blk, (jb + 1) * blk)
        if jb > 0:
            kcat = k_ref[(jb - 1) * blk:(jb + 1) * blk, :]
            vcat = v_ref[(jb - 1) * blk:(jb + 1) * blk, :]
        else:
            kcat = jnp.concatenate([ktail[...], k_ref[cur, :]], axis=0)
            vcat = jnp.concatenate([vtail[...], v_ref[cur, :]], axis=0)
        m, l, o = block_stats(q_ref[cur, :], kcat, vcat, jb == 0)
        tmp_s[0] = jnp.broadcast_to(m, (blk, HEAD_DIM))
        tmp_s[1] = jnp.broadcast_to(l, (blk, HEAD_DIM))
        tmp_s[2] = o
        per = blk // sub
        for r in range(sub):
            dst = slice(r * nq + jb * per, r * nq + (jb + 1) * per)
            src = pl.ds(r, per, stride=sub)
            m_s[dst, :] = tmp_s[0, src, :]
            l_s[dst, :] = tmp_s[1, src, :]
            acc_s[dst, :] = tmp_s[2, src, :]

    for r in range(sub):
        for jb in range(nq // blk):
            base = r * nq + jb * blk
            cur = slice(base, base + blk)
            if jb > 0:
                kcat = k4[slot, base - blk:base + blk, :]
                vcat = v4[slot, base - blk:base + blk, :]
            else:
                last = slice((r + 1) * nq - blk, (r + 1) * nq)
                kcat = jnp.concatenate([k4[other, last, :], k4[slot, cur, :]], axis=0)
                vcat = jnp.concatenate([v4[other, last, :], v4[slot, cur, :]], axis=0)
            m, l, o = block_stats(q4[cur, :], kcat, vcat, jb == 0)
            merge(cur, m, l, o)

    for c in range(16):
        sl = pl.ds((c % sub) * nq + c // sub, blk, stride=sub)
        kcat = jnp.concatenate([k4[other, sl, :], k4[slot, sl, :]], axis=0)
        vcat = jnp.concatenate([v4[other, sl, :], v4[slot, sl, :]], axis=0)
        m, l, o = block_stats(q4[sl, :], kcat, vcat, True)
        merge(sl, m, l, o)

    out = acc_s[...] / l_s[...]
    out = out * lax.rsqrt(jnp.mean(out * out, axis=-1, keepdims=True) + RMS_EPS) * nw_ref[...]
    for r in range(sub):
        nat_s[pl.ds(r, nq, stride=sub), :] = out[r * nq:(r + 1) * nq]
    o_ref[...] = nat_s[...].astype(o_ref.dtype)
    ktail[...] = k_ref[tq - blk:tq, :]
    vtail[...] = v_ref[tq - blk:tq, :]


def _attn(qb, kb, vb, norm_w, bsz, seq):
    t = qb.shape[0]
    tq = ATT_TILE
    spb = seq // tq
    cur = lambda b, h, i: (b * spb + i, h)
    blk = pl.BlockSpec((tq, HEAD_DIM), cur)
    tile = pltpu.VMEM((tq, HEAD_DIM), F32)
    ring = pltpu.VMEM((2, tq, HEAD_DIM), F32)
    tail = pltpu.VMEM((ATT_BLOCK, HEAD_DIM), F32)
    return pl.pallas_call(
        _attn_kernel,
        out_shape=jax.ShapeDtypeStruct((t, ATT_HEADS * HEAD_DIM), BF16),
        grid=(bsz, ATT_HEADS, spb),
        in_specs=[blk, blk, blk, pl.BlockSpec((1, HEAD_DIM), lambda b, h, i: (0, 0))],
        out_specs=blk,
        scratch_shapes=[tile, ring, ring, tail, tail, tile, tile, tile,
                        pltpu.VMEM((3, ATT_BLOCK, HEAD_DIM), F32), tile],
        compiler_params=_params(("parallel", "parallel", "arbitrary")),
        name="attn",
    )(qb, kb, vb, norm_w.reshape(1, HEAD_DIM))


def _layer_norm(y, g, b):
    mu = jnp.mean(y, axis=-1, keepdims=True)
    yc = y - mu
    var = jnp.mean(yc * yc, axis=-1, keepdims=True)
    return yc * lax.rsqrt(var + LN_EPS) * g + b


def _outproj_kernel(oa_ref, ob_ref, x_ref, mod_ref, woa_ref, wob_ref, g_ref, b_ref,
                    wrh_ref, wrl_ref, br_ref,
                    x1_ref, h2_ref, ri_ref, rg_ref, cnt_ref, run_s):
    @pl.when(pl.program_id(0) == 0)
    def _():
        run_s[...] = jnp.zeros(run_s.shape, F32)

    gate1 = mod_ref[0, 2:3, :]
    shift2 = mod_ref[0, 3:4, :]
    scale2 = mod_ref[0, 4:5, :]
    mix = (jnp.dot(oa_ref[...], woa_ref[...], preferred_element_type=F32)
           + jnp.dot(ob_ref[...], wob_ref[...], preferred_element_type=F32))
    x1 = _layer_norm(DEEPNORM_ALPHA * x_ref[...] + (1.0 + gate1) * mix, g_ref[...], b_ref[...])
    x1_ref[...] = x1
    h2 = x1 * (1.0 + scale2) + shift2
    h2_ref[...] = h2

    hi = _bf(h2)
    lo = _bf(h2 - hi.astype(F32))
    logits = (jnp.dot(hi, wrh_ref[...], preferred_element_type=F32)
              + jnp.dot(hi, wrl_ref[...], preferred_element_type=F32)
              + jnp.dot(lo, wrh_ref[...], preferred_element_type=F32)) + br_ref[...]
    tm = logits.shape[0]
    lane = lax.broadcasted_iota(jnp.int32, (tm, LANES), 1)
    lg = jnp.where(lane < N_GROUPS, logits, NEG)
    mg = jnp.max(lg, axis=-1, keepdims=True)
    grp = jnp.min(jnp.where(lg == mg, lane, LANES), axis=-1, keepdims=True)
    gate_grp = 1.0 / jnp.sum(jnp.exp(lg - mg), axis=-1, keepdims=True)
    eidx = lane - N_GROUPS
    sel = (eidx >= 0) & (eidx < N_EXPERTS) & ((eidx // EXPERTS_PER_GROUP) == grp)
    le = jnp.where(sel, logits, NEG)
    v1 = jnp.max(le, axis=-1, keepdims=True)
    i1 = jnp.min(jnp.where(le == v1, lane, LANES), axis=-1, keepdims=True)
    le2 = jnp.where(lane == i1, NEG, le)
    v2 = jnp.max(le2, axis=-1, keepdims=True)
    i2 = jnp.min(jnp.where(le2 == v2, lane, LANES), axis=-1, keepdims=True)
    e21 = jnp.exp(v2 - v1)
    g1 = gate_grp / (1.0 + e21)
    g2 = gate_grp * e21 / (1.0 + e21)

    oh1 = lane == i1
    oh2 = lane == i2
    onehot = jnp.where(oh1 | oh2, 1.0, 0.0).astype(F32)
    ti = lax.broadcasted_iota(jnp.int32, (tm, tm), 0)
    tj = lax.broadcasted_iota(jnp.int32, (tm, tm), 1)
    before = jnp.where(tj < ti, 1.0, 0.0).astype(F32)
    tot = _mm(before, onehot) + run_s[...]
    r1 = jnp.sum(jnp.where(oh1, tot, 0.0), axis=-1, keepdims=True)
    r2 = jnp.sum(jnp.where(oh2, tot, 0.0), axis=-1, keepdims=True)
    run_s[...] = run_s[...] + jnp.sum(onehot, axis=0, keepdims=True)
    cnt_ref[...] = run_s[...]

    cols = jnp.where(lane == 0, (i1 - N_GROUPS).astype(F32), 0.0)
    cols = jnp.where(lane == 1, (i2 - N_GROUPS).astype(F32), cols)
    r1_hi = jnp.floor(r1 * (1.0 / 256.0))
    r2_hi = jnp.floor(r2 * (1.0 / 256.0))
    cols = jnp.where(lane == 2, r1_hi, cols)
    cols = jnp.where(lane == 3, r1 - 256.0 * r1_hi, cols)
    cols = jnp.where(lane == 4, r2_hi, cols)
    cols = jnp.where(lane == 5, r2 - 256.0 * r2_hi, cols)
    pick = jnp.where(lax.broadcasted_iota(jnp.int32, (SUBLANES, LANES), 0)
                     == lax.broadcasted_iota(jnp.int32, (SUBLANES, LANES), 1), 1.0, 0.0).astype(F32)
    ri_ref[...] = _mm_nt(pick, cols).astype(jnp.int32)
    rg_ref[...] = jnp.where(lane == 0, g1, jnp.where(lane == 1, g2, 0.0))


def _outproj(oa, ob, xf, mod, w_o, ln_g, ln_b, w_rg, b_rg, w_re, b_re, seq):
    t, d = xf.shape
    tm = PROJ_TILE
    gw = oa.shape[1]
    wo = _bf(w_o)
    wr = jnp.zeros((d, LANES), F32).at[:, :N_GROUPS].set(w_rg).at[:, N_GROUPS:N_GROUPS + N_EXPERTS].set(w_re)
    wrh = _bf(wr)
    wrl = _bf(wr - wrh.astype(F32))
    br = jnp.zeros((1, LANES), F32).at[0, :N_GROUPS].set(b_rg).at[0, N_GROUPS:N_GROUPS + N_EXPERTS].set(b_re)
    spb = seq // tm
    row = lambda i: (i, 0)
    const = lambda i: (0, 0)
    return pl.pallas_call(
        _outproj_kernel,
        out_shape=(jax.ShapeDtypeStruct((t, d), F32), jax.ShapeDtypeStruct((t, d), F32),
                   jax.ShapeDtypeStruct((SUBLANES, t), jnp.int32), jax.ShapeDtypeStruct((t, LANES), F32),
                   jax.ShapeDtypeStruct((1, LANES), F32)),
        grid=(t // tm,),
        in_specs=[pl.BlockSpec((tm, gw), row), pl.BlockSpec((tm, gw), row), pl.BlockSpec((tm, d), row),
                  pl.BlockSpec((1, 6, d), lambda i: (i // spb, 0, 0)),
                  pl.BlockSpec((gw, d), const), pl.BlockSpec((gw, d), const),
                  pl.BlockSpec((1, d), const), pl.BlockSpec((1, d), const),
                  pl.BlockSpec((d, LANES), const), pl.BlockSpec((d, LANES), const),
                  pl.BlockSpec((1, LANES), const)],
        out_specs=(pl.BlockSpec((tm, d), row), pl.BlockSpec((tm, d), row),
                   pl.BlockSpec((SUBLANES, tm), lambda i: (0, i)), pl.BlockSpec((tm, LANES), row),
                   pl.BlockSpec((1, LANES), const)),
        scratch_shapes=[pltpu.VMEM((1, LANES), F32)],
        compiler_params=_params(("arbitrary",)),
        name="outproj",
    )(oa, ob, xf, mod, wo[:gw], wo[gw:], ln_g.reshape(1, d), ln_b.reshape(1, d), wrh, wrl, br)


def _dispatch_kernel(d1_ref, d2_ref, h_ref, xs_ref, sem):
    tm = h_ref.shape[0]
    i = pl.program_id(0)

    def row_copy(tk, dest):
        return pltpu.make_async_copy(h_ref.at[pl.ds(tk, 1), :], xs_ref.at[pl.ds(dest, 1), :], sem)

    def issue(g, carry):
        for u in range(DMA_UNROLL):
            tk = g * DMA_UNROLL + u
            tok = i * tm + tk
            row_copy(tk, d1_ref[tok]).start(priority=0)
            row_copy(tk, d2_ref[tok]).start(priority=1)
        return carry

    lax.fori_loop(0, tm // DMA_UNROLL, issue, 0)

    tile_copy = pltpu.make_async_copy(h_ref, xs_ref.at[pl.ds(0, tm), :], sem)
    tile_copy.wait()
    tile_copy.wait()


def _dispatch(h2, d1, d2):
    t, d = h2.shape
    tm = DISPATCH_TILE
    return pl.pallas_call(
        _dispatch_kernel,
        out_shape=jax.ShapeDtypeStruct((2 * t, d), F32),
        grid_spec=pltpu.PrefetchScalarGridSpec(
            num_scalar_prefetch=2,
            grid=(t // tm,),
            in_specs=[pl.BlockSpec((tm, d), lambda i, *_: (i, 0))],
            out_specs=pl.BlockSpec(memory_space=pl.ANY),
            scratch_shapes=[pltpu.SemaphoreType.DMA]),
        compiler_params=_params(("arbitrary",)),
        name="dispatch",
    )(d1, d2, h2)


def _experts_kernel(wb_ref, we_ref, lo_ref, hi_ref, nw_ref, first_ref, ring_ref, next_ref,
                    xs_ref, wg_hbm, wu_hbm, wd_hbm, ys_ref,
                    wg_f, wu_f, wd_f, wg_s, wu_s, wd_s, sem):
    w = pl.program_id(0)

    def fetch(expert, slot):
        return (pltpu.make_async_copy(wg_hbm.at[expert], wg_f.at[slot], sem.at[slot]),
                pltpu.make_async_copy(wu_hbm.at[expert], wu_f.at[slot], sem.at[slot]),
                pltpu.make_async_copy(wd_hbm.at[expert], wd_f.at[slot], sem.at[slot]))

    @pl.when(w == 0)
    def _():
        for cp in fetch(we_ref[0], 0):
            cp.start()

    @pl.when((first_ref[w] == 1) & (w < nw_ref[0]))
    def _():
        slot = ring_ref[w]
        for cp in fetch(we_ref[w], slot):
            cp.wait()

        @pl.when(next_ref[w] >= 0)
        def _():
            for cp in fetch(next_ref[w], 1 - slot):
                cp.start()

        wg_s[...] = _bf(wg_f[slot])
        wu_s[...] = _bf(wu_f[slot])
        wd_s[...] = _bf(wd_f[slot])

    @pl.when(w < nw_ref[0])
    def _():
        x = _bf(xs_ref[...])
        hid = (_silu(jnp.dot(x, wg_s[...], preferred_element_type=F32))
               * jnp.dot(x, wu_s[...], preferred_element_type=F32))
        y = jnp.dot(_bf(hid), wd_s[...], preferred_element_type=F32)
        row = lax.broadcasted_iota(jnp.int32, (y.shape[0], 1), 0)
        mine = (row >= lo_ref[w]) & (row < hi_ref[w])

        @pl.when(lo_ref[w] == 0)
        def _():
            ys_ref[...] = jnp.where(mine, y, 0.0)

        @pl.when(lo_ref[w] > 0)
        def _():
            ys_ref[...] = jnp.where(mine, y, ys_ref[...])


def _experts(xs, item_block, item_expert, item_lo, item_hi, n_items, w_gate, w_up, w_down):
    n_slots, d = xs.shape
    ff = w_gate.shape[2]
    bm = EXPERT_BLOCK
    n = item_block.shape[0]
    idx = jnp.arange(n, dtype=jnp.int32)
    first = jnp.concatenate([jnp.ones((1,), jnp.int32),
                             (item_expert[1:] != item_expert[:-1]).astype(jnp.int32)])
    ring = (jnp.cumsum(first) - 1) % 2
    next_first = lax.cummin(jnp.where(first == 1, idx, n), reverse=True)
    next_first = jnp.concatenate([next_first[1:], jnp.full((1,), n, jnp.int32)])
    nxt = jnp.where(next_first < n, item_expert[jnp.minimum(next_first, n - 1)], -1).astype(jnp.int32)
    slot = lambda w, *_: (_[0][w], 0)
    return pl.pallas_call(
        _experts_kernel,
        out_shape=jax.ShapeDtypeStruct((n_slots, d), F32),
        grid_spec=pltpu.PrefetchScalarGridSpec(
            num_scalar_prefetch=8,
            grid=(n,),
            in_specs=[pl.BlockSpec((bm, d), slot),
                      pl.BlockSpec(memory_space=pl.ANY), pl.BlockSpec(memory_space=pl.ANY),
                      pl.BlockSpec(memory_space=pl.ANY)],
            out_specs=pl.BlockSpec((bm, d), slot),
            scratch_shapes=[pltpu.VMEM((2, d, ff), F32), pltpu.VMEM((2, d, ff), F32),
                            pltpu.VMEM((2, ff, d), F32),
                            pltpu.VMEM((d, ff), BF16), pltpu.VMEM((d, ff), BF16),
                            pltpu.VMEM((ff, d), BF16), pltpu.SemaphoreType.DMA((2,))]),
        compiler_params=_params(("arbitrary",)),
        name="experts",
    )(item_block, item_expert, item_lo, item_hi, n_items, first, ring.astype(jnp.int32), nxt,
      xs, w_gate, w_up, w_down)


def _combine_kernel(d1_ref, d2_ref, ys_ref, rg_ref, x1_ref, mod_ref, g_ref, b_ref, o_ref, ya, yb, sem):
    tm = x1_ref.shape[0]
    i = pl.program_id(0)
    n = pl.num_programs(0)

    def row_copy(dest, buf, slot, tk):
        return pltpu.make_async_copy(ys_ref.at[pl.ds(dest, 1), :], buf.at[slot, pl.ds(tk, 1), :],
                                     sem.at[slot])

    def gather_tile(step, slot):
        def issue(g, carry):
            for u in range(DMA_UNROLL):
                tk = g * DMA_UNROLL + u
                tok = step * tm + tk
                row_copy(d1_ref[tok], ya, slot, tk).start(priority=0)
                row_copy(d2_ref[tok], yb, slot, tk).start(priority=1)
            return carry

        lax.fori_loop(0, tm // DMA_UNROLL, issue, 0)

    @pl.when(i == 0)
    def _():
        gather_tile(0, 0)

    @pl.when(i + 1 < n)
    def _():
        gather_tile(i + 1, (i + 1) % 2)

    slot = i % 2
    pltpu.make_async_copy(ys_ref.at[pl.ds(0, tm), :], ya.at[slot], sem.at[slot]).wait()
    pltpu.make_async_copy(ys_ref.at[pl.ds(0, tm), :], yb.at[slot], sem.at[slot]).wait()

    gate2 = mod_ref[0, 5:6, :]
    rg = rg_ref[...]
    y = rg[:, 0:1] * ya[slot] + rg[:, 1:2] * yb[slot]
    o_ref[...] = _layer_norm(DEEPNORM_ALPHA * x1_ref[...] + (1.0 + gate2) * y, g_ref[...], b_ref[...])


def _combine(ys, d1, d2, rg, x1, mod, ln_g, ln_b, seq):
    t, d = x1.shape
    tm = ROW_TILE
    spb = seq // tm
    row = lambda i, *_: (i, 0)
    const = lambda i, *_: (0, 0)
    buf = pltpu.VMEM((2, tm, d), F32)
    return pl.pallas_call(
        _combine_kernel,
        out_shape=jax.ShapeDtypeStruct((t, d), F32),
        grid_spec=pltpu.PrefetchScalarGridSpec(
            num_scalar_prefetch=2,
            grid=(t // tm,),
            in_specs=[pl.BlockSpec(memory_space=pl.ANY),
                      pl.BlockSpec((tm, LANES), row), pl.BlockSpec((tm, d), row),
                      pl.BlockSpec((1, 6, d), lambda i, *_: (i // spb, 0, 0)),
                      pl.BlockSpec((1, d), const), pl.BlockSpec((1, d), const)],
            out_specs=pl.BlockSpec((tm, d), row),
            scratch_shapes=[buf, buf, pltpu.SemaphoreType.DMA((2,))]),
        compiler_params=_params(("arbitrary",)),
        name="combine",
    )(d1, d2, ys, rg, x1, mod, ln_g.reshape(1, d), ln_b.reshape(1, d))


def _layer(x, c, positions, w_ada, b_ada, w_in, conv_w, a_log, dt_bias, gdn_norm_w, attn_norm_w,
           w_o, ln1_g, ln1_b, w_rg, b_rg, w_re, b_re, w_gate, w_up, w_down, ln2_g, ln2_b):
    bsz, seq, d = x.shape
    t = bsz * seq
    xf = x.reshape(t, d)
    mod = _ada(c, w_ada, b_ada)
    qkv, z, bdc, bdr, qb, kb, vb = _inproj(xf, mod, positions.reshape(t, 1), w_in, conv_w, seq)
    gw = GDN_HEADS * HEAD_DIM
    oa = _gdn(qkv.reshape(bsz, seq, 3 * gw), z.reshape(bsz, seq, gw), bdc.reshape(bsz, seq, LANES), bdr,
              a_log, dt_bias, gdn_norm_w).reshape(t, gw)
    ob = _attn(qb, kb, vb, attn_norm_w, bsz, seq)
    x1, h2, ri, rg, cnt = _outproj(oa, ob, xf, mod, w_o, ln1_g, ln1_b, w_rg, b_rg, w_re, b_re, seq)

    bm = EXPERT_BLOCK
    counts = cnt[0, N_GROUPS:N_GROUPS + N_EXPERTS].astype(jnp.int32)
    seg_end = jnp.cumsum(counts)
    seg_start = seg_end - counts
    first_blk = seg_start // bm
    n_per = jnp.where(counts > 0, (seg_end - 1) // bm - first_blk + 1, 0)
    item_end = jnp.cumsum(n_per)
    n_items = item_end[-1:]
    max_items = (2 * t) // bm + N_EXPERTS - 1
    w_idx = jnp.minimum(jnp.arange(max_items, dtype=jnp.int32), n_items[0] - 1)
    item_expert = jnp.minimum(jnp.sum(item_end[None, :] <= w_idx[:, None], axis=1), N_EXPERTS - 1).astype(jnp.int32)
    item_block = first_blk[item_expert] + w_idx - (item_end - n_per)[item_expert]
    item_lo = jnp.maximum(seg_start[item_expert] - item_block * bm, 0)
    item_hi = jnp.minimum(seg_end[item_expert] - item_block * bm, bm)
    d1 = seg_start[ri[0]] + ri[2] * 256 + ri[3]
    d2 = seg_start[ri[1]] + ri[4] * 256 + ri[5]

    xs = _dispatch(h2, d1, d2)
    ys = _experts(xs, item_block, item_expert, item_lo, item_hi, n_items, w_gate, w_up, w_down)
    out = _combine(ys, d1, d2, rg, x1, mod, ln2_g, ln2_b, seq)
    return out.reshape(bsz, seq, d)


def kernel(x, c, positions, w_ada, b_ada, w_in, conv_w, a_log, dt_bias, gdn_norm_w, attn_norm_w, w_o, ln1_g, ln1_b, w_router_group, b_router_group, w_router_expert, b_router_expert, w_gate, w_up, w_down, ln2_g, ln2_b):
    assert w_ada.shape[0] == DEPTH
    return _layer(x, c, positions, w_ada[0], b_ada[0], w_in[0], conv_w[0], a_log[0], dt_bias[0],
                  gdn_norm_w[0], attn_norm_w[0], w_o[0], ln1_g[0], ln1_b[0],
                  w_router_group[0], b_router_group[0], w_router_expert[0], b_router_expert[0],
                  w_gate[0], w_up[0], w_down[0], ln2_g[0], ln2_b[0])
```

```python
import functools
import math

import jax
import jax.numpy as jnp
from jax import lax
from jax.experimental import pallas as pl
from jax.experimental.pallas import tpu as pltpu

F32 = jnp.float32
BF16 = jnp.bfloat16
HIGHEST = lax.Precision.HIGHEST

GDN_HEADS = 4
ATT_HEADS = 4
HEAD_DIM = 128
CONV_WIDTH = 4
DILATED_PATTERNS = ((128, 1), (512, 4), (2048, 16))
ROPE_THETA = 500000.0
ROPE_DIMS = HEAD_DIM // 4
N_GROUPS = 4
EXPERTS_PER_GROUP = 8
N_EXPERTS = N_GROUPS * EXPERTS_PER_GROUP
DEPTH = 1
DEEPNORM_ALPHA = (2.0 * DEPTH) ** 0.25
LN_EPS = 1e-5
RMS_EPS = 1e-6

LANES = 128
SUBLANES = 8
VMEM_LIMIT = 48 * 1024 * 1024

GDN_BLOCK = 128
GDN_TILE = 256
ATT_BLOCK = 128
ATT_TILE = 2048
PROJ_TILE = 512
ROW_TILE = 256
DISPATCH_TILE = 1024
EXPERT_BLOCK = 256
DMA_UNROLL = 8
NEG = -1e30


def _bf(x):
    return x.astype(BF16)


def _mm(a, b):
    return jnp.dot(_bf(a), _bf(b), preferred_element_type=F32)


def _mm_nt(a, b):
    return lax.dot_general(_bf(a), _bf(b), (((1,), (1,)), ((), ())), preferred_element_type=F32)


def _mm_tn(a, b):
    return lax.dot_general(_bf(a), _bf(b), (((0,), (0,)), ((), ())), preferred_element_type=F32)


def _mm_f32(a, b):
    return jnp.dot(a, b, preferred_element_type=F32, precision=HIGHEST)


def _sigmoid(x):
    return 1.0 / (1.0 + jnp.exp(-x))


def _silu(x):
    return x * _sigmoid(x)


def _softplus(x):
    return jnp.maximum(x, 0.0) + jnp.log(1.0 + jnp.exp(-jnp.abs(x)))


def _params(sem):
    return pltpu.CompilerParams(dimension_semantics=sem, vmem_limit_bytes=VMEM_LIMIT)


def _ada_kernel(c_ref, w_ref, b_ref, o_ref):
    o_ref[...] = _mm_f32(_silu(c_ref[...]), w_ref[...]) + b_ref[...]


def _ada(c, w_ada, b_ada):
    bsz, d = c.shape
    n = w_ada.shape[1]
    tn = 512
    cp = jnp.zeros((SUBLANES, d), F32).at[:bsz].set(c)
    out = pl.pallas_call(
        _ada_kernel,
        out_shape=jax.ShapeDtypeStruct((SUBLANES, n), F32),
        grid=(n // tn,),
        in_specs=[pl.BlockSpec((SUBLANES, d), lambda j: (0, 0)),
                  pl.BlockSpec((d, tn), lambda j: (0, j)),
                  pl.BlockSpec((1, tn), lambda j: (0, j))],
        out_specs=pl.BlockSpec((SUBLANES, tn), lambda j: (0, j)),
        compiler_params=_params(("parallel",)),
        name="ada",
    )(cp, w_ada, b_ada.reshape(1, n))
    return out[:bsz].reshape(bsz, 6, d)


def _inproj_kernel(x_ref, mod_ref, pos_ref, invf_ref, convw_ref, wqkv_ref, wz_ref, wbd_ref, wbdt_ref,
                   wq_ref, wk_ref, wv_ref,
                   qkv_ref, z_ref, bdc_ref, bdr_ref, qb_ref, kb_ref, vb_ref, cbuf, *, steps_per_seq):
    tm = x_ref.shape[0]
    halo = SUBLANES
    n_slabs = cbuf.shape[0]

    @pl.when(pl.program_id(0) % steps_per_seq == 0)
    def _():
        cbuf[:, 0:halo, :] = jnp.zeros((n_slabs, halo, LANES), F32)

    half = ROPE_DIMS // 2
    groups = LANES // half
    lane = lax.broadcasted_iota(jnp.int32, (1, LANES), 1)
    first = lane < half
    rot = lane < ROPE_DIMS
    ang = pos_ref[0].astype(F32) * invf_ref[...]
    cos_c = jnp.cos(ang)
    sin_c = jnp.sin(ang)
    cos_parts, sin_parts = [], []
    for j in range(groups):
        lo_sh = (LANES - half * j) % LANES
        hi_sh = (LANES - half * j + half) % LANES
        c_lo = pltpu.roll(cos_c, lo_sh, 1) if lo_sh else cos_c
        c_hi = pltpu.roll(cos_c, hi_sh, 1) if hi_sh else cos_c
        s_lo = pltpu.roll(sin_c, lo_sh, 1) if lo_sh else sin_c
        s_hi = pltpu.roll(sin_c, hi_sh, 1) if hi_sh else sin_c
        cos_parts.append(jnp.where(first, c_lo, jnp.where(rot, c_hi, 1.0)))
        sin_parts.append(jnp.where(first, -s_lo, jnp.where(rot, s_hi, 0.0)))
    cosv = jnp.concatenate(cos_parts, axis=0)
    sin_signed = jnp.concatenate(sin_parts, axis=0)

    def rope(y):
        outs = []
        for hh in range(ATT_HEADS):
            yh = y[:, hh * HEAD_DIM:(hh + 1) * HEAD_DIM]
            partner = jnp.where(first, pltpu.roll(yh, LANES - half, 1), pltpu.roll(yh, half, 1))
            outs.append(yh * cosv + partner * sin_signed)
        return jnp.concatenate(outs, axis=1)

    def conv_slice(s):
        sl = slice(s * HEAD_DIM, (s + 1) * HEAD_DIM)
        off = halo - (CONV_WIDTH - 1)
        acc = convw_ref[0:1, sl] * cbuf[s, off:off + tm, :]
        for j in range(1, CONV_WIDTH):
            acc = acc + convw_ref[j:j + 1, sl] * cbuf[s, off + j:off + j + tm, :]
        cbuf[s, 0:halo, :] = cbuf[s, tm:tm + halo, :]
        y = _silu(acc)
        if s < 2 * GDN_HEADS:
            y = y * lax.rsqrt(jnp.sum(y * y, axis=-1, keepdims=True) + RMS_EPS)
        if s < GDN_HEADS:
            y = y * (HEAD_DIM ** -0.5)
        qkv_ref[:, sl] = y

    shift = mod_ref[0, 0:1, :]
    scale = mod_ref[0, 1:2, :]
    h = _bf(x_ref[...] * (1.0 + scale) + shift)
    chunk = 2 * HEAD_DIM
    n_chunks = n_slabs // 2

    def project_chunk(c):
        pre = jnp.dot(h, wqkv_ref[:, c * chunk:(c + 1) * chunk], preferred_element_type=F32)
        cbuf[2 * c, halo:halo + tm, :] = pre[:, :HEAD_DIM]
        cbuf[2 * c + 1, halo:halo + tm, :] = pre[:, HEAD_DIM:]

    def conv_chunk(c):
        conv_slice(2 * c)
        conv_slice(2 * c + 1)

    project_chunk(0)
    for c in range(1, n_chunks):
        project_chunk(c)
        conv_chunk(c - 1)
    qb = jnp.dot(h, wq_ref[...], preferred_element_type=F32)
    conv_chunk(n_chunks - 1)
    kb = jnp.dot(h, wk_ref[...], preferred_element_type=F32)
    qb_ref[...] = rope(qb) * (HEAD_DIM ** -0.5)
    vb_ref[...] = jnp.dot(h, wv_ref[...], preferred_element_type=F32)
    kb_ref[...] = rope(kb)
    z_ref[...] = jnp.dot(h, wz_ref[...], preferred_element_type=F32)
    bdc_ref[...] = jnp.dot(h, wbd_ref[...], preferred_element_type=F32)
    bdr_ref[0] = lax.dot_general(wbdt_ref[...], h, (((1,), (1,)), ((), ())),
                                 preferred_element_type=F32)


def _inproj(xf, mod, pos, w_in, conv_w, seq):
    t, d = xf.shape
    tm = PROJ_TILE
    gw = GDN_HEADS * HEAD_DIM
    aw = ATT_HEADS * HEAD_DIM
    o0 = 3 * gw
    o1 = o0 + gw
    o2 = o1 + 2 * GDN_HEADS
    wb = _bf(w_in)
    wqkv, wz = wb[:, :o0], wb[:, o0:o1]
    wbd_n = wb[:, o1:o2]
    wbd = jnp.zeros((d, LANES), BF16).at[:, :2 * GDN_HEADS].set(wbd_n)
    wbdt = wbd_n.T
    wq, wk, wv = wb[:, o2:o2 + aw], wb[:, o2 + aw:o2 + 2 * aw], wb[:, o2 + 2 * aw:o2 + 3 * aw]
    half = ROPE_DIMS // 2
    groups = LANES // half
    inv_freq = ROPE_THETA ** (-jnp.arange(half, dtype=F32) * 2.0 / ROPE_DIMS)
    invf = jnp.tile(inv_freq, groups).reshape(1, LANES)
    pos = jnp.repeat(pos.reshape(t // tm, groups, tm // groups).transpose(0, 2, 1), half, axis=2)
    spb = seq // tm
    row = lambda i: (i, 0)
    const = lambda i: (0, 0)
    return pl.pallas_call(
        functools.partial(_inproj_kernel, steps_per_seq=spb),
        out_shape=(jax.ShapeDtypeStruct((t, o0), F32), jax.ShapeDtypeStruct((t, gw), F32),
                   jax.ShapeDtypeStruct((t, LANES), F32), jax.ShapeDtypeStruct((t // seq, SUBLANES, seq), F32),
                   jax.ShapeDtypeStruct((t, aw), F32), jax.ShapeDtypeStruct((t, aw), F32),
                   jax.ShapeDtypeStruct((t, aw), F32)),
        grid=(t // tm,),
        in_specs=[pl.BlockSpec((tm, d), row),
                  pl.BlockSpec((1, 6, d), lambda i: (i // spb, 0, 0)),
                  pl.BlockSpec((1, tm // groups, LANES), lambda i: (i, 0, 0)),
                  pl.BlockSpec((1, LANES), const), pl.BlockSpec((CONV_WIDTH, o0), const),
                  pl.BlockSpec((d, o0), const), pl.BlockSpec((d, gw), const),
                  pl.BlockSpec((d, LANES), const), pl.BlockSpec((2 * GDN_HEADS, d), const),
                  pl.BlockSpec((d, aw), const), pl.BlockSpec((d, aw), const),
                  pl.BlockSpec((d, aw), const)],
        out_specs=(pl.BlockSpec((tm, o0), row), pl.BlockSpec((tm, gw), row),
                   pl.BlockSpec((tm, LANES), row),
                   pl.BlockSpec((1, SUBLANES, tm), lambda i: (i // spb, 0, i % spb)),
                   pl.BlockSpec((tm, aw), row), pl.BlockSpec((tm, aw), row),
                   pl.BlockSpec((tm, aw), row)),
        scratch_shapes=[pltpu.VMEM((o0 // HEAD_DIM, tm + 2 * SUBLANES, HEAD_DIM), F32)],
        compiler_params=_params(("arbitrary",)),
        name="inproj",
    )(xf, mod, pos, invf, conv_w, wqkv, wz, wbd, wbdt, wq, wk, wv)


def _gdn_kernel(qkv_ref, z_ref, bdc_ref, bdr_ref, gpc_ref, gpr_ref, nw_ref, o_ref,
                u_s, w_s, qd_s, kd_s, qk_s, a_s, x_s, y_s, st_ref):
    nbat, ts = qkv_ref.shape[0], qkv_ref.shape[1]
    nb = ts // GDN_BLOCK
    gw = GDN_HEADS * HEAD_DIM

    @pl.when(pl.program_id(0) == 0)
    def _():
        st_ref[...] = jnp.zeros(st_ref.shape, F32)

    blk_n = GDN_BLOCK
    hd = HEAD_DIM
    ti = lax.broadcasted_iota(jnp.int32, (ts, ts), 0)
    tj = lax.broadcasted_iota(jnp.int32, (ts, ts), 1)
    same = (ti // blk_n) == (tj // blk_n)
    m_low = jnp.where(same & (tj <= ti), 1.0, 0.0).astype(BF16)
    m_up = jnp.where(same & (ti <= tj), 1.0, 0.0).astype(BF16)

    def split3(x):
        x1 = _bf(x)
        r1 = x - x1.astype(F32)
        x2 = _bf(r1)
        return x1, x2, _bf(r1 - x2.astype(F32))

    beta_c = [_sigmoid(bdc_ref[bb]) for bb in range(nbat)]
    g_c = jnp.concatenate([-jnp.exp(gpc_ref[0:1, :]) * _softplus(bdc_ref[bb] + gpc_ref[1:2, :])
                           for bb in range(nbat)], axis=1)
    g_r = jnp.concatenate([-jnp.exp(gpr_ref[:, 0:1]) * _softplus(bdr_ref[bb] + gpr_ref[:, 1:2])
                           for bb in range(nbat)], axis=0)
    gc_c = sum(jnp.dot(m_low, part, preferred_element_type=F32) for part in split3(g_c))
    gc_r = sum(jnp.dot(part, m_up, preferred_element_type=F32) for part in split3(g_r))

    def cat2(m):
        return jnp.concatenate([m, m], axis=1)

    ii = lax.broadcasted_iota(jnp.int32, (blk_n, blk_n), 0)
    jj = lax.broadcasted_iota(jnp.int32, (blk_n, blk_n), 1)
    lower = cat2(jj <= ii)
    strict = cat2(jj < ii)
    eye = cat2(jnp.where(ii == jj, 1.0, 0.0).astype(F32))
    levels = []
    b = 1
    while b < blk_n:
        levels.append(cat2(((ii // b) == (jj // b) + 1) & (((jj // b) % 2) == 0)))
        b *= 2

    def block_diag(rp):
        n, m = rp.shape[0], rp.shape[1] // 2
        z = jnp.zeros((n, m), rp.dtype)
        return jnp.concatenate([jnp.concatenate([rp[:, :m], z], axis=1),
                                jnp.concatenate([z, rp[:, m:]], axis=1)], axis=0)

    def mm2(lp, rp):
        return jnp.dot(_bf(lp), block_diag(_bf(rp)), preferred_element_type=F32)

    def mm2_nt(lp, rp):
        return lax.dot_general(_bf(lp), block_diag(_bf(rp)), (((1,), (1,)), ((), ())),
                               preferred_element_type=F32)

    chains = [(bb, j, pp) for bb in range(nbat) for j in range(nb) for pp in range(GDN_HEADS // 2)]

    def tile_of(bb, j, pp):
        return bb, slice(j * blk_n, (j + 1) * blk_n), slice(2 * pp * hd, 2 * (pp + 1) * hd)

    def col_pair(arr, rows, col):
        return jnp.concatenate([jnp.broadcast_to(arr[rows, col:col + 1], (blk_n, hd)),
                                jnp.broadcast_to(arr[rows, col + 1:col + 2], (blk_n, hd))], axis=1)

    for c, (bb, j, pp) in enumerate(chains):
        blk = tile_of(bb, j, pp)
        rows, cols = blk[1], blk[2]
        last = slice((j + 1) * blk_n - 1, (j + 1) * blk_n)
        gcol = bb * LANES + GDN_HEADS + 2 * pp
        grow = bb * SUBLANES + GDN_HEADS + 2 * pp
        q = qkv_ref[bb, rows, cols]
        k = qkv_ref[bb, rows, slice(gw + cols.start, gw + cols.stop)]
        v = qkv_ref[bb, rows, slice(2 * gw + cols.start, 2 * gw + cols.stop)]
        beta = col_pair(beta_c[bb], rows, 2 * pp)
        gcc = col_pair(gc_c, rows, gcol)
        gtc = col_pair(gc_c, last, gcol)
        gcr = jnp.concatenate([jnp.broadcast_to(gc_r[grow:grow + 1, rows], (blk_n, hd)),
                               jnp.broadcast_to(gc_r[grow + 1:grow + 2, rows], (blk_n, hd))], axis=1)
        kb = k * beta
        eg = jnp.exp(gcc)
        dm = jnp.where(lower, jnp.exp(gcc - gcr), 0.0)
        a = jnp.where(strict, mm2_nt(kb, k) * dm, 0.0)
        a_s[c] = a
        x_s[c] = eye - jnp.where(levels[0], a, 0.0)
        u_s[blk] = v * beta
        w_s[blk] = kb * eg
        qd_s[blk] = q * eg
        kd_s[blk] = k * jnp.exp(gtc - gcc)
        qk_s[blk] = mm2_nt(q, k) * dm

    for lm in levels[1:]:
        for c in range(len(chains)):
            y_s[c] = mm2(x_s[c], jnp.where(lm, a_s[c], 0.0))
        for c in range(len(chains)):
            xc = x_s[c]
            x_s[c] = xc - mm2(y_s[c], xc)

    for c, (bb, j, pp) in enumerate(chains):
        blk = tile_of(bb, j, pp)
        u, w = _bf(u_s[blk]), _bf(w_s[blk])
        z = jnp.zeros((blk_n, 2 * hd), BF16)
        rhs = jnp.concatenate([jnp.concatenate([u[:, :hd], w[:, :hd], z], axis=1),
                               jnp.concatenate([z, u[:, hd:], w[:, hd:]], axis=1)], axis=0)
        sol = jnp.dot(_bf(x_s[c]), rhs, preferred_element_type=F32)
        u_s[blk] = jnp.concatenate([sol[:, 0:hd], sol[:, 2 * hd:3 * hd]], axis=1)
        w_s[blk] = jnp.concatenate([sol[:, hd:2 * hd], sol[:, 3 * hd:]], axis=1)

    for j in range(nb):
        for bb in range(nbat):
            for pp in range(GDN_HEADS // 2):
                blk = tile_of(bb, j, pp)
                si = bb * (GDN_HEADS // 2) + pp
                last = slice((j + 1) * blk_n - 1, (j + 1) * blk_n)
                gcol = bb * LANES + GDN_HEADS + 2 * pp
                state = st_ref[si]
                proj = mm2(jnp.concatenate([w_s[blk], qd_s[blk]], axis=0), state)
                v_new = u_s[blk] - proj[:blk_n]
                o = proj[blk_n:] + mm2(qk_s[blk], v_new)
                g_last = jnp.exp(jnp.concatenate(
                    [jnp.broadcast_to(gc_c[last, gcol:gcol + 1], (1, hd)),
                     jnp.broadcast_to(gc_c[last, gcol + 1:gcol + 2], (1, hd))], axis=1))
                kd = kd_s[blk]
                upd = jnp.concatenate([_mm_tn(kd[:, :hd], v_new[:, :hd]),
                                       _mm_tn(kd[:, hd:], v_new[:, hd:])], axis=1)
                st_ref[si] = state * g_last + upd
                zz = z_ref[blk]
                halves = []
                for hf in range(2):
                    oh = o[:, hf * hd:(hf + 1) * hd]
                    halves.append(oh * lax.rsqrt(jnp.mean(oh * oh, axis=-1, keepdims=True) + RMS_EPS)
                                  * nw_ref[...])
                o_ref[blk] = (jnp.concatenate(halves, axis=1) * _silu(zz)).astype(o_ref.dtype)


def _gdn(qkv, z, bdc, bdr, a_log, dt_bias, norm_w):
    bsz, seq, _ = qkv.shape
    ts = GDN_TILE
    gw = GDN_HEADS * HEAD_DIM
    zeros4 = jnp.zeros((GDN_HEADS,), F32)
    al = jnp.concatenate([zeros4, a_log])
    db = jnp.concatenate([zeros4, dt_bias])
    gpc = jnp.zeros((2, LANES), F32).at[0, :2 * GDN_HEADS].set(al).at[1, :2 * GDN_HEADS].set(db)
    gpr = jnp.stack([al, db], axis=1)
    row = lambda i: (0, i, 0)
    const = lambda i: (0, 0)
    n_chains = bsz * (ts // GDN_BLOCK) * (GDN_HEADS // 2)
    tile = pltpu.VMEM((bsz, ts, gw), F32)
    mats = pltpu.VMEM((n_chains, GDN_BLOCK, 2 * GDN_BLOCK), F32)
    return pl.pallas_call(
        _gdn_kernel,
        out_shape=jax.ShapeDtypeStruct((bsz, seq, gw), BF16),
        grid=(seq // ts,),
        in_specs=[pl.BlockSpec((bsz, ts, 3 * gw), row), pl.BlockSpec((bsz, ts, gw), row),
                  pl.BlockSpec((bsz, ts, LANES), row),
                  pl.BlockSpec((bsz, SUBLANES, ts), lambda i: (0, 0, i)),
                  pl.BlockSpec((2, LANES), const), pl.BlockSpec((2 * GDN_HEADS, 2), const),
                  pl.BlockSpec((1, HEAD_DIM), const)],
        out_specs=pl.BlockSpec((bsz, ts, gw), row),
        scratch_shapes=[tile, tile, tile, tile, tile,
                        mats, mats, mats,
                        pltpu.VMEM((bsz * GDN_HEADS // 2, HEAD_DIM, 2 * HEAD_DIM), F32)],
        compiler_params=_params(("arbitrary",)),
        name="gdn",
    )(qkv, z, bdc, bdr, gpc, gpr, norm_w.reshape(1, HEAD_DIM))


def _attn_kernel(q_ref, k_ref, v_ref, nw_ref, o_ref,
                 q4, k4, v4, ktail, vtail, m_s, l_s, acc_s, tmp_s, nat_s):
    tq = q_ref.shape[0]
    blk = ATT_BLOCK
    sub = 4
    nq = tq // sub
    t = pl.program_id(2)
    slot = t % 2
    other = 1 - slot
    qi = lax.broadcasted_iota(jnp.int32, (blk, 2 * blk), 0)
    kj = lax.broadcasted_iota(jnp.int32, (blk, 2 * blk), 1)
    band = (kj >= qi) & (kj <= qi + blk)
    first_lo = jnp.where(t > 0, 0, blk)

    @pl.when(t == 0)
    def _():
        k4[...] = jnp.zeros(k4.shape, F32)
        v4[...] = jnp.zeros(v4.shape, F32)
        ktail[...] = jnp.zeros(ktail.shape, F32)
        vtail[...] = jnp.zeros(vtail.shape, F32)

    for r in range(sub):
        rows = slice(r * nq, (r + 1) * nq)
        src = pl.ds(r, nq, stride=sub)
        q4[rows, :] = q_ref[src, :]
        k4[slot, rows, :] = k_ref[src, :]
        v4[slot, rows, :] = v_ref[src, :]

    def block_stats(q, kcat, vcat, from_prev_tile):
        s = _mm_nt(q, kcat)
        mask = band & (kj >= first_lo) if from_prev_tile else band
        s = jnp.where(mask, s, NEG)
        m = jnp.max(s, axis=-1, keepdims=True)
        p = jnp.exp(s - m)
        l = jnp.sum(p, axis=-1, keepdims=True)
        return m, l, _mm(p, vcat)

    def merge(dst, m, l, o):
        m_old = m_s[dst, :]
        m_new = jnp.maximum(m_old, m)
        w_old = jnp.exp(m_old - m_new)
        w_cur = jnp.exp(m - m_new)
        m_s[dst, :] = m_new
        l_s[dst, :] = w_old * l_s[dst, :] + w_cur * l
        acc_s[dst, :] = w_old * acc_s[dst, :] + w_cur * o

    for jb in range(tq // blk):
        cur = slice(jb * blk, (jb + 1) * blk)
        if jb > 0:
            kcat = k_ref[(jb - 1) * blk:(jb + 1) * blk, :]
            vcat = v_ref[(jb - 1) * blk:(jb + 1) * blk, :]
        else:
            kcat = jnp.concatenate([ktail[...], k_ref[cur, :]], axis=0)
            vcat = jnp.concatenate([vtail[...], v_ref[cur, :]], axis=0)
        m, l, o = block_stats(q_ref[cur, :], kcat, vcat, jb == 0)
        tmp_s[0] = jnp.broadcast_to(m, (blk, HEAD_DIM))
        tmp_s[1] = jnp.broadcast_to(l, (blk, HEAD_DIM))
        tmp_s[2] = o
        per = blk // sub
        for r in range(sub):
            dst = slice(r * nq + jb * per, r * nq + (jb + 1) * per)
            src = pl.ds(r, per, stride=sub)
            m_s[dst, :] = tmp_s[0, src, :]
            l_s[dst, :] = tmp_s[1, src, :]
            acc_s[dst, :] = tmp_s[2, src, :]

    for r in range(sub):
        for jb in range(nq // blk):
            base = r * nq + jb * blk
            cur = slice(base, base + blk)
            if jb > 0:
                kcat = k4[slot, base - blk:base + blk, :]
                vcat = v4[slot, base - blk:base + blk, :]
            else:
                last = slice((r + 1) * nq - blk, (r + 1) * nq)
                kcat = jnp.concatenate([k4[other, last, :], k4[slot, cur, :]], axis=0)
                vcat = jnp.concatenate([v4[other, last, :], v4[slot, cur, :]], axis=0)
            m, l, o = block_stats(q4[cur, :], kcat, vcat, jb == 0)
            merge(cur, m, l, o)

    for c in range(16):
        sl = pl.ds((c % sub) * nq + c // sub, blk, stride=sub)
        kcat = jnp.concatenate([k4[other, sl, :], k4[slot, sl, :]], axis=0)
        vcat = jnp.concatenate([v4[other, sl, :], v4[slot, sl, :]], axis=0)
        m, l, o = block_stats(q4[sl, :], kcat, vcat, True)
        merge(sl, m, l, o)

    out = acc_s[...] / l_s[...]
    out = out * lax.rsqrt(jnp.mean(out * out, axis=-1, keepdims=True) + RMS_EPS) * nw_ref[...]
    for r in range(sub):
        nat_s[pl.ds(r, nq, stride=sub), :] = out[r * nq:(r + 1) * nq]
    o_ref[...] = nat_s[...].astype(o_ref.dtype)
    ktail[...] = k_ref[tq - blk:tq, :]
    vtail[...] = v_ref[tq - blk:tq, :]


def _attn(qb, kb, vb, norm_w, bsz, seq):
    t = qb.shape[0]
    tq = ATT_TILE
    spb = seq // tq
    cur = lambda b, h, i: (b * spb + i, h)
    blk = pl.BlockSpec((tq, HEAD_DIM), cur)
    tile = pltpu.VMEM((tq, HEAD_DIM), F32)
    ring = pltpu.VMEM((2, tq, HEAD_DIM), F32)
    tail = pltpu.VMEM((ATT_BLOCK, HEAD_DIM), F32)
    return pl.pallas_call(
        _attn_kernel,
        out_shape=jax.ShapeDtypeStruct((t, ATT_HEADS * HEAD_DIM), BF16),
        grid=(bsz, ATT_HEADS, spb),
        in_specs=[blk, blk, blk, pl.BlockSpec((1, HEAD_DIM), lambda b, h, i: (0, 0))],
        out_specs=blk,
        scratch_shapes=[tile, ring, ring, tail, tail, tile, tile, tile,
                        pltpu.VMEM((3, ATT_BLOCK, HEAD_DIM), F32), tile],
        compiler_params=_params(("parallel", "parallel", "arbitrary")),
        name="attn",
    )(qb, kb, vb, norm_w.reshape(1, HEAD_DIM))


def _layer_norm(y, g, b):
    mu = jnp.mean(y, axis=-1, keepdims=True)
    yc = y - mu
    var = jnp.mean(yc * yc, axis=-1, keepdims=True)
    return yc * lax.rsqrt(var + LN_EPS) * g + b


def _outproj_kernel(oa_ref, ob_ref, x_ref, mod_ref, woa_ref, wob_ref, g_ref, b_ref,
                    wrh_ref, wrl_ref, br_ref,
                    x1_ref, h2_ref, ri_ref, rg_ref, cnt_ref, run_s):
    @pl.when(pl.program_id(0) == 0)
    def _():
        run_s[...] = jnp.zeros(run_s.shape, F32)

    gate1 = mod_ref[0, 2:3, :]
    shift2 = mod_ref[0, 3:4, :]
    scale2 = mod_ref[0, 4:5, :]
    mix = (jnp.dot(oa_ref[...], woa_ref[...], preferred_element_type=F32)
           + jnp.dot(ob_ref[...], wob_ref[...], preferred_element_type=F32))
    x1 = _layer_norm(DEEPNORM_ALPHA * x_ref[...] + (1.0 + gate1) * mix, g_ref[...], b_ref[...])
    x1_ref[...] = x1
    h2 = x1 * (1.0 + scale2) + shift2
    h2_ref[...] = h2

    hi = _bf(h2)
    lo = _bf(h2 - hi.astype(F32))
    logits = (jnp.dot(hi, wrh_ref[...], preferred_element_type=F32)
              + jnp.dot(hi, wrl_ref[...], preferred_element_type=F32)
              + jnp.dot(lo, wrh_ref[...], preferred_element_type=F32)) + br_ref[...]
    tm = logits.shape[0]
    lane = lax.broadcasted_iota(jnp.int32, (tm, LANES), 1)
    lg = jnp.where(lane < N_GROUPS, logits, NEG)
    mg = jnp.max(lg, axis=-1, keepdims=True)
    grp = jnp.min(jnp.where(lg == mg, lane, LANES), axis=-1, keepdims=True)
    gate_grp = 1.0 / jnp.sum(jnp.exp(lg - mg), axis=-1, keepdims=True)
    eidx = lane - N_GROUPS
    sel = (eidx >= 0) & (eidx < N_EXPERTS) & ((eidx // EXPERTS_PER_GROUP) == grp)
    le = jnp.where(sel, logits, NEG)
    v1 = jnp.max(le, axis=-1, keepdims=True)
    i1 = jnp.min(jnp.where(le == v1, lane, LANES), axis=-1, keepdims=True)
    le2 = jnp.where(lane == i1, NEG, le)
    v2 = jnp.max(le2, axis=-1, keepdims=True)
    i2 = jnp.min(jnp.where(le2 == v2, lane, LANES), axis=-1, keepdims=True)
    e21 = jnp.exp(v2 - v1)
    g1 = gate_grp / (1.0 + e21)
    g2 = gate_grp * e21 / (1.0 + e21)

    oh1 = lane == i1
    oh2 = lane == i2
    onehot = jnp.where(oh1 | oh2, 1.0, 0.0).astype(F32)
    ti = lax.broadcasted_iota(jnp.int32, (tm, tm), 0)
    tj = lax.broadcasted_iota(jnp.int32, (tm, tm), 1)
    before = jnp.where(tj < ti, 1.0, 0.0).astype(F32)
    tot = _mm(before, onehot) + run_s[...]
    r1 = jnp.sum(jnp.where(oh1, tot, 0.0), axis=-1, keepdims=True)
    r2 = jnp.sum(jnp.where(oh2, tot, 0.0), axis=-1, keepdims=True)
    run_s[...] = run_s[...] + jnp.sum(onehot, axis=0, keepdims=True)
    cnt_ref[...] = run_s[...]

    cols = jnp.where(lane == 0, (i1 - N_GROUPS).astype(F32), 0.0)
    cols = jnp.where(lane == 1, (i2 - N_GROUPS).astype(F32), cols)
    r1_hi = jnp.floor(r1 * (1.0 / 256.0))
    r2_hi = jnp.floor(r2 * (1.0 / 256.0))
    cols = jnp.where(lane == 2, r1_hi, cols)
    cols = jnp.where(lane == 3, r1 - 256.0 * r1_hi, cols)
    cols = jnp.where(lane == 4, r2_hi, cols)
    cols = jnp.where(lane == 5, r2 - 256.0 * r2_hi, cols)
    pick = jnp.where(lax.broadcasted_iota(jnp.int32, (SUBLANES, LANES), 0)
                     == lax.broadcasted_iota(jnp.int32, (SUBLANES, LANES), 1), 1.0, 0.0).astype(F32)
    ri_ref[...] = _mm_nt(pick, cols).astype(jnp.int32)
    rg_ref[...] = jnp.where(lane == 0, g1, jnp.where(lane == 1, g2, 0.0))


def _outproj(oa, ob, xf, mod, w_o, ln_g, ln_b, w_rg, b_rg, w_re, b_re, seq):
    t, d = xf.shape
    tm = PROJ_TILE
    gw = oa.shape[1]
    wo = _bf(w_o)
    wr = jnp.zeros((d, LANES), F32).at[:, :N_GROUPS].set(w_rg).at[:, N_GROUPS:N_GROUPS + N_EXPERTS].set(w_re)
    wrh = _bf(wr)
    wrl = _bf(wr - wrh.astype(F32))
    br = jnp.zeros((1, LANES), F32).at[0, :N_GROUPS].set(b_rg).at[0, N_GROUPS:N_GROUPS + N_EXPERTS].set(b_re)
    spb = seq // tm
    row = lambda i: (i, 0)
    const = lambda i: (0, 0)
    return pl.pallas_call(
        _outproj_kernel,
        out_shape=(jax.ShapeDtypeStruct((t, d), F32), jax.ShapeDtypeStruct((t, d), F32),
                   jax.ShapeDtypeStruct((SUBLANES, t), jnp.int32), jax.ShapeDtypeStruct((t, LANES), F32),
                   jax.ShapeDtypeStruct((1, LANES), F32)),
        grid=(t // tm,),
        in_specs=[pl.BlockSpec((tm, gw), row), pl.BlockSpec((tm, gw), row), pl.BlockSpec((tm, d), row),
                  pl.BlockSpec((1, 6, d), lambda i: (i // spb, 0, 0)),
                  pl.BlockSpec((gw, d), const), pl.BlockSpec((gw, d), const),
                  pl.BlockSpec((1, d), const), pl.BlockSpec((1, d), const),
                  pl.BlockSpec((d, LANES), const), pl.BlockSpec((d, LANES), const),
                  pl.BlockSpec((1, LANES), const)],
        out_specs=(pl.BlockSpec((tm, d), row), pl.BlockSpec((tm, d), row),
                   pl.BlockSpec((SUBLANES, tm), lambda i: (0, i)), pl.BlockSpec((tm, LANES), row),
                   pl.BlockSpec((1, LANES), const)),
        scratch_shapes=[pltpu.VMEM((1, LANES), F32)],
        compiler_params=_params(("arbitrary",)),
        name="outproj",
    )(oa, ob, xf, mod, wo[:gw], wo[gw:], ln_g.reshape(1, d), ln_b.reshape(1, d), wrh, wrl, br)


def _dispatch_kernel(d1_ref, d2_ref, h_ref, xs_ref, sem):
    tm = h_ref.shape[0]
    i = pl.program_id(0)

    def row_copy(tk, dest):
        return pltpu.make_async_copy(h_ref.at[pl.ds(tk, 1), :], xs_ref.at[pl.ds(dest, 1), :], sem)

    def issue(g, carry):
        for u in range(DMA_UNROLL):
            tk = g * DMA_UNROLL + u
            tok = i * tm + tk
            row_copy(tk, d1_ref[tok]).start(priority=0)
            row_copy(tk, d2_ref[tok]).start(priority=1)
        return carry

    lax.fori_loop(0, tm // DMA_UNROLL, issue, 0)

    tile_copy = pltpu.make_async_copy(h_ref, xs_ref.at[pl.ds(0, tm), :], sem)
    tile_copy.wait()
    tile_copy.wait()


def _dispatch(h2, d1, d2):
    t, d = h2.shape
    tm = DISPATCH_TILE
    return pl.pallas_call(
        _dispatch_kernel,
        out_shape=jax.ShapeDtypeStruct((2 * t, d), F32),
        grid_spec=pltpu.PrefetchScalarGridSpec(
            num_scalar_prefetch=2,
            grid=(t // tm,),
            in_specs=[pl.BlockSpec((tm, d), lambda i, *_: (i, 0))],
            out_specs=pl.BlockSpec(memory_space=pl.ANY),
            scratch_shapes=[pltpu.SemaphoreType.DMA]),
        compiler_params=_params(("arbitrary",)),
        name="dispatch",
    )(d1, d2, h2)


def _experts_kernel(wb_ref, we_ref, lo_ref, hi_ref, nw_ref, first_ref, ring_ref, next_ref,
                    xs_ref, wg_hbm, wu_hbm, wd_hbm, ys_ref,
                    wg_f, wu_f, wd_f, wg_s, wu_s, wd_s, sem):
    w = pl.program_id(0)

    def fetch(expert, slot):
        return (pltpu.make_async_copy(wg_hbm.at[expert], wg_f.at[slot], sem.at[slot]),
                pltpu.make_async_copy(wu_hbm.at[expert], wu_f.at[slot], sem.at[slot]),
                pltpu.make_async_copy(wd_hbm.at[expert], wd_f.at[slot], sem.at[slot]))

    @pl.when(w == 0)
    def _():
        for cp in fetch(we_ref[0], 0):
            cp.start()

    @pl.when((first_ref[w] == 1) & (w < nw_ref[0]))
    def _():
        slot = ring_ref[w]
        for cp in fetch(we_ref[w], slot):
            cp.wait()

        @pl.when(next_ref[w] >= 0)
        def _():
            for cp in fetch(next_ref[w], 1 - slot):
                cp.start()

        wg_s[...] = _bf(wg_f[slot])
        wu_s[...] = _bf(wu_f[slot])
        wd_s[...] = _bf(wd_f[slot])

    @pl.when(w < nw_ref[0])
    def _():
        x = _bf(xs_ref[...])
        hid = (_silu(jnp.dot(x, wg_s[...], preferred_element_type=F32))
               * jnp.dot(x, wu_s[...], preferred_element_type=F32))
        y = jnp.dot(_bf(hid), wd_s[...], preferred_element_type=F32)
        row = lax.broadcasted_iota(jnp.int32, (y.shape[0], 1), 0)
        mine = (row >= lo_ref[w]) & (row < hi_ref[w])

        @pl.when(lo_ref[w] == 0)
        def _():
            ys_ref[...] = jnp.where(mine, y, 0.0)

        @pl.when(lo_ref[w] > 0)
        def _():
            ys_ref[...] = jnp.where(mine, y, ys_ref[...])


def _experts(xs, item_block, item_expert, item_lo, item_hi, n_items, w_gate, w_up, w_down):
    n_slots, d = xs.shape
    ff = w_gate.shape[2]
    bm = EXPERT_BLOCK
    n = item_block.shape[0]
    idx = jnp.arange(n, dtype=jnp.int32)
    first = jnp.concatenate([jnp.ones((1,), jnp.int32),
                             (item_expert[1:] != item_expert[:-1]).astype(jnp.int32)])
    ring = (jnp.cumsum(first) - 1) % 2
    next_first = lax.cummin(jnp.where(first == 1, idx, n), reverse=True)
    next_first = jnp.concatenate([next_first[1:], jnp.full((1,), n, jnp.int32)])
    nxt = jnp.where(next_first < n, item_expert[jnp.minimum(next_first, n - 1)], -1).astype(jnp.int32)
    slot = lambda w, *_: (_[0][w], 0)
    return pl.pallas_call(
        _experts_kernel,
        out_shape=jax.ShapeDtypeStruct((n_slots, d), F32),
        grid_spec=pltpu.PrefetchScalarGridSpec(
            num_scalar_prefetch=8,
            grid=(n,),
            in_specs=[pl.BlockSpec((bm, d), slot),
                      pl.BlockSpec(memory_space=pl.ANY), pl.BlockSpec(memory_space=pl.ANY),
                      pl.BlockSpec(memory_space=pl.ANY)],
            out_specs=pl.BlockSpec((bm, d), slot),
            scratch_shapes=[pltpu.VMEM((2, d, ff), F32), pltpu.VMEM((2, d, ff), F32),
                            pltpu.VMEM((2, ff, d), F32),
                            pltpu.VMEM((d, ff), BF16), pltpu.VMEM((d, ff), BF16),
                            pltpu.VMEM((ff, d), BF16), pltpu.SemaphoreType.DMA((2,))]),
        compiler_params=_params(("arbitrary",)),
        name="experts",
    )(item_block, item_expert, item_lo, item_hi, n_items, first, ring.astype(jnp.int32), nxt,
      xs, w_gate, w_up, w_down)


def _combine_kernel(d1_ref, d2_ref, ys_ref, rg_ref, x1_ref, mod_ref, g_ref, b_ref, o_ref, ya, yb, sem):
    tm = x1_ref.shape[0]
    i = pl.program_id(0)
    n = pl.num_programs(0)

    def row_copy(dest, buf, slot, tk):
        return pltpu.make_async_copy(ys_ref.at[pl.ds(dest, 1), :], buf.at[slot, pl.ds(tk, 1), :],
                                     sem.at[slot])

    def gather_tile(step, slot):
        def issue(g, carry):
            for u in range(DMA_UNROLL):
                tk = g * DMA_UNROLL + u
                tok = step * tm + tk
                row_copy(d1_ref[tok], ya, slot, tk).start(priority=0)
                row_copy(d2_ref[tok], yb, slot, tk).start(priority=1)
            return carry

        lax.fori_loop(0, tm // DMA_UNROLL, issue, 0)

    @pl.when(i == 0)
    def _():
        gather_tile(0, 0)

    @pl.when(i + 1 < n)
    def _():
        gather_tile(i + 1, (i + 1) % 2)

    slot = i % 2
    pltpu.make_async_copy(ys_ref.at[pl.ds(0, tm), :], ya.at[slot], sem.at[slot]).wait()
    pltpu.make_async_copy(ys_ref.at[pl.ds(0, tm), :], yb.at[slot], sem.at[slot]).wait()

    gate2 = mod_ref[0, 5:6, :]
    rg = rg_ref[...]
    y = rg[:, 0:1] * ya[slot] + rg[:, 1:2] * yb[slot]
    o_ref[...] = _layer_norm(DEEPNORM_ALPHA * x1_ref[...] + (1.0 + gate2) * y, g_ref[...], b_ref[...])


def _combine(ys, d1, d2, rg, x1, mod, ln_g, ln_b, seq):
    t, d = x1.shape
    tm = ROW_TILE
    spb = seq // tm
    row = lambda i, *_: (i, 0)
    const = lambda i, *_: (0, 0)
    buf = pltpu.VMEM((2, tm, d), F32)
    return pl.pallas_call(
        _combine_kernel,
        out_shape=jax.ShapeDtypeStruct((t, d), F32),
        grid_spec=pltpu.PrefetchScalarGridSpec(
            num_scalar_prefetch=2,
            grid=(t // tm,),
            in_specs=[pl.BlockSpec(memory_space=pl.ANY),
                      pl.BlockSpec((tm, LANES), row), pl.BlockSpec((tm, d), row),
                      pl.BlockSpec((1, 6, d), lambda i, *_: (i // spb, 0, 0)),
                      pl.BlockSpec((1, d), const), pl.BlockSpec((1, d), const)],
            out_specs=pl.BlockSpec((tm, d), row),
            scratch_shapes=[buf, buf, pltpu.SemaphoreType.DMA((2,))]),
        compiler_params=_params(("arbitrary",)),
        name="combine",
    )(d1, d2, ys, rg, x1, mod, ln_g.reshape(1, d), ln_b.reshape(1, d))


def _layer(x, c, positions, w_ada, b_ada, w_in, conv_w, a_log, dt_bias, gdn_norm_w, attn_norm_w,
           w_o, ln1_g, ln1_b, w_rg, b_rg, w_re, b_re, w_gate, w_up, w_down, ln2_g, ln2_b):
    bsz, seq, d = x.shape
    t = bsz * seq
    xf = x.reshape(t, d)
    mod = _ada(c, w_ada, b_ada)
    qkv, z, bdc, bdr, qb, kb, vb = _inproj(xf, mod, positions.reshape(t, 1), w_in, conv_w, seq)
    gw = GDN_HEADS * HEAD_DIM
    oa = _gdn(qkv.reshape(bsz, seq, 3 * gw), z.reshape(bsz, seq, gw), bdc.reshape(bsz, seq, LANES), bdr,
              a_log, dt_bias, gdn_norm_w).reshape(t, gw)
    ob = _attn(qb, kb, vb, attn_norm_w, bsz, seq)
    x1, h2, ri, rg, cnt = _outproj(oa, ob, xf, mod, w_o, ln1_g, ln1_b, w_rg, b_rg, w_re, b_re, seq)

    bm = EXPERT_BLOCK
    counts = cnt[0, N_GROUPS:N_GROUPS + N_EXPERTS].astype(jnp.int32)
    seg_end = jnp.cumsum(counts)
    seg_start = seg_end - counts
    first_blk = seg_start // bm
    n_per = jnp.where(counts > 0, (seg_end - 1) // bm - first_blk + 1, 0)
    item_end = jnp.cumsum(n_per)
    n_items = item_end[-1:]
    max_items = (2 * t) // bm + N_EXPERTS - 1
    w_idx = jnp.minimum(jnp.arange(max_items, dtype=jnp.int32), n_items[0] - 1)
    item_expert = jnp.minimum(jnp.sum(item_end[None, :] <= w_idx[:, None], axis=1), N_EXPERTS - 1).astype(jnp.int32)
    item_block = first_blk[item_expert] + w_idx - (item_end - n_per)[item_expert]
    item_lo = jnp.maximum(seg_start[item_expert] - item_block * bm, 0)
    item_hi = jnp.minimum(seg_end[item_expert] - item_block * bm, bm)
    d1 = seg_start[ri[0]] + ri[2] * 256 + ri[3]
    d2 = seg_start[ri[1]] + ri[4] * 256 + ri[5]

    xs = _dispatch(h2, d1, d2)
    ys = _experts(xs, item_block, item_expert, item_lo, item_hi, n_items, w_gate, w_up, w_down)
    out = _combine(ys, d1, d2, rg, x1, mod, ln2_g, ln2_b, seq)
    return out.reshape(bsz, seq, d)


def kernel(x, c, positions, w_ada, b_ada, w_in, conv_w, a_log, dt_bias, gdn_norm_w, attn_norm_w, w_o, ln1_g, ln1_b, w_router_group, b_router_group, w_router_expert, b_router_expert, w_gate, w_up, w_down, ln2_g, ln2_b):
    assert w_ada.shape[0] == DEPTH
    return _layer(x, c, positions, w_ada[0], b_ada[0], w_in[0], conv_w[0], a_log[0], dt_bias[0],
                  gdn_norm_w[0], attn_norm_w[0], w_o[0], ln1_g[0], ln1_b[0],
                  w_router_group[0], b_router_group[0], w_router_expert[0], b_router_expert[0],
                  w_gate[0], w_up[0], w_down[0], ln2_g[0], ln2_b[0])
```

```python
import functools
import math

import jax
import jax.numpy as jnp
from jax import lax
from jax.experimental import pallas as pl
from jax.experimental.pallas import tpu as pltpu

F32 = jnp.float32
BF16 = jnp.bfloat16
HIGHEST = lax.Precision.HIGHEST

GDN_HEADS = 4
ATT_HEADS = 4
HEAD_DIM = 128
CONV_WIDTH = 4
DILATED_PATTERNS = ((128, 1), (512, 4), (2048, 16))
ROPE_THETA = 500000.0
ROPE_DIMS = HEAD_DIM // 4
N_GROUPS = 4
EXPERTS_PER_GROUP = 8
N_EXPERTS = N_GROUPS * EXPERTS_PER_GROUP
DEPTH = 1
DEEPNORM_ALPHA = (2.0 * DEPTH) ** 0.25
LN_EPS = 1e-5
RMS_EPS = 1e-6

LANES = 128
SUBLANES = 8
VMEM_LIMIT = 48 * 1024 * 1024

GDN_BLOCK = 128
GDN_TILE = 256
ATT_BLOCK = 128
ATT_TILE = 2048
PROJ_TILE = 512
ROW_TILE = 256
DISPATCH_TILE = 1024
EXPERT_BLOCK = 256
DMA_UNROLL = 8
NEG = -1e30


def _bf(x):
    return x.astype(BF16)


def _mm(a, b):
    return jnp.dot(_bf(a), _bf(b), preferred_element_type=F32)


def _mm_nt(a, b):
    return lax.dot_general(_bf(a), _bf(b), (((1,), (1,)), ((), ())), preferred_element_type=F32)


def _mm_tn(a, b):
    return lax.dot_general(_bf(a), _bf(b), (((0,), (0,)), ((), ())), preferred_element_type=F32)


def _mm_f32(a, b):
    return jnp.dot(a, b, preferred_element_type=F32, precision=HIGHEST)


def _sigmoid(x):
    return 1.0 / (1.0 + jnp.exp(-x))


def _silu(x):
    return x * _sigmoid(x)


def _softplus(x):
    return jnp.maximum(x, 0.0) + jnp.log(1.0 + jnp.exp(-jnp.abs(x)))


def _params(sem):
    return pltpu.CompilerParams(dimension_semantics=sem, vmem_limit_bytes=VMEM_LIMIT)


def _ada_kernel(c_ref, w_ref, b_ref, o_ref):
    o_ref[...] = _mm_f32(_silu(c_ref[...]), w_ref[...]) + b_ref[...]


def _ada(c, w_ada, b_ada):
    bsz, d = c.shape
    n = w_ada.shape[1]
    tn = 512
    cp = jnp.zeros((SUBLANES, d), F32).at[:bsz].set(c)
    out = pl.pallas_call(
        _ada_kernel,
        out_shape=jax.ShapeDtypeStruct((SUBLANES, n), F32),
        grid=(n // tn,),
        in_specs=[pl.BlockSpec((SUBLANES, d), lambda j: (0, 0)),
                  pl.BlockSpec((d, tn), lambda j: (0, j)),
                  pl.BlockSpec((1, tn), lambda j: (0, j))],
        out_specs=pl.BlockSpec((SUBLANES, tn), lambda j: (0, j)),
        compiler_params=_params(("parallel",)),
        name="ada",
    )(cp, w_ada, b_ada.reshape(1, n))
    return out[:bsz].reshape(bsz, 6, d)


def _inproj_kernel(x_ref, mod_ref, pos_ref, invf_ref, convw_ref, wqkv_ref, wz_ref, wbd_ref, wbdt_ref,
                   wq_ref, wk_ref, wv_ref,
                   qkv_ref, z_ref, bdc_ref, bdr_ref, qb_ref, kb_ref, vb_ref, cbuf, *, steps_per_seq):
    tm = x_ref.shape[0]
    halo = SUBLANES
    n_slabs = cbuf.shape[0]

    @pl.when(pl.program_id(0) % steps_per_seq == 0)
    def _():
        cbuf[:, 0:halo, :] = jnp.zeros((n_slabs, halo, LANES), F32)

    half = ROPE_DIMS // 2
    groups = LANES // half
    lane = lax.broadcasted_iota(jnp.int32, (1, LANES), 1)
    first = lane < half
    rot = lane < ROPE_DIMS
    ang = pos_ref[0].astype(F32) * invf_ref[...]
    cos_c = jnp.cos(ang)
    sin_c = jnp.sin(ang)
    cos_parts, sin_parts = [], []
    for j in range(groups):
        lo_sh = (LANES - half * j) % LANES
        hi_sh = (LANES - half * j + half) % LANES
        c_lo = pltpu.roll(cos_c, lo_sh, 1) if lo_sh else cos_c
        c_hi = pltpu.roll(cos_c, hi_sh, 1) if hi_sh else cos_c
        s_lo = pltpu.roll(sin_c, lo_sh, 1) if lo_sh else sin_c
        s_hi = pltpu.roll(sin_c, hi_sh, 1) if hi_sh else sin_c
        cos_parts.append(jnp.where(first, c_lo, jnp.where(rot, c_hi, 1.0)))
        sin_parts.append(jnp.where(first, -s_lo, jnp.where(rot, s_hi, 0.0)))
    cosv = jnp.concatenate(cos_parts, axis=0)
    sin_signed = jnp.concatenate(sin_parts, axis=0)

    def rope(y):
        outs = []
        for hh in range(ATT_HEADS):
            yh = y[:, hh * HEAD_DIM:(hh + 1) * HEAD_DIM]
            partner = jnp.where(first, pltpu.roll(yh, LANES - half, 1), pltpu.roll(yh, half, 1))
            outs.append(yh * cosv + partner * sin_signed)
        return jnp.concatenate(outs, axis=1)

    def conv_slice(s):
        sl = slice(s * HEAD_DIM, (s + 1) * HEAD_DIM)
        off = halo - (CONV_WIDTH - 1)
        acc = convw_ref[0:1, sl] * cbuf[s, off:off + tm, :]
        for j in range(1, CONV_WIDTH):
            acc = acc + convw_ref[j:j + 1, sl] * cbuf[s, off + j:off + j + tm, :]
        cbuf[s, 0:halo, :] = cbuf[s, tm:tm + halo, :]
        y = _silu(acc)
        if s < 2 * GDN_HEADS:
            y = y * lax.rsqrt(jnp.sum(y * y, axis=-1, keepdims=True) + RMS_EPS)
        if s < GDN_HEADS:
            y = y * (HEAD_DIM ** -0.5)
        qkv_ref[:, sl] = y

    shift = mod_ref[0, 0:1, :]
    scale = mod_ref[0, 1:2, :]
    h = _bf(x_ref[...] * (1.0 + scale) + shift)
    chunk = 2 * HEAD_DIM
    n_chunks = n_slabs // 2

    def project_chunk(c):
        pre = jnp.dot(h, wqkv_ref[:, c * chunk:(c + 1) * chunk], preferred_element_type=F32)
        cbuf[2 * c, halo:halo + tm, :] = pre[:, :HEAD_DIM]
        cbuf[2 * c + 1, halo:halo + tm, :] = pre[:, HEAD_DIM:]

    def conv_chunk(c):
        conv_slice(2 * c)
        conv_slice(2 * c + 1)

    project_chunk(0)
    for c in range(1, n_chunks):
        project_chunk(c)
        conv_chunk(c - 1)
    qb = jnp.dot(h, wq_ref[...], preferred_element_type=F32)
    conv_chunk(n_chunks - 1)
    kb = jnp.dot(h, wk_ref[...], preferred_element_type=F32)
    qb_ref[...] = rope(qb) * (HEAD_DIM ** -0.5)
    vb_ref[...] = jnp.dot(h, wv_ref[...], preferred_element_type=F32)
    kb_ref[...] = rope(kb)
    z_ref[...] = jnp.dot(h, wz_ref[...], preferred_element_type=F32)
    bdc_ref[...] = jnp.dot(h, wbd_ref[...], preferred_element_type=F32)
    bdr_ref[0] = lax.dot_general(wbdt_ref[...], h, (((1,), (1,)), ((), ())),
                                 preferred_element_type=F32)


def _inproj(xf, mod, pos, w_in, conv_w, seq):
    t, d = xf.shape
    tm = PROJ_TILE
    gw = GDN_HEADS * HEAD_DIM
    aw = ATT_HEADS * HEAD_DIM
    o0 = 3 * gw
    o1 = o0 + gw
    o2 = o1 + 2 * GDN_HEADS
    wb = _bf(w_in)
    wqkv, wz = wb[:, :o0], wb[:, o0:o1]
    wbd_n = wb[:, o1:o2]
    wbd = jnp.zeros((d, LANES), BF16).at[:, :2 * GDN_HEADS].set(wbd_n)
    wbdt = wbd_n.T
    wq, wk, wv = wb[:, o2:o2 + aw], wb[:, o2 + aw:o2 + 2 * aw], wb[:, o2 + 2 * aw:o2 + 3 * aw]
    half = ROPE_DIMS // 2
    groups = LANES // half
    inv_freq = ROPE_THETA ** (-jnp.arange(half, dtype=F32) * 2.0 / ROPE_DIMS)
    invf = jnp.tile(inv_freq, groups).reshape(1, LANES)
    pos = jnp.repeat(pos.reshape(t // tm, groups, tm // groups).transpose(0, 2, 1), half, axis=2)
    spb = seq // tm
    row = lambda i: (i, 0)
    const = lambda i: (0, 0)
    return pl.pallas_call(
        functools.partial(_inproj_kernel, steps_per_seq=spb),
        out_shape=(jax.ShapeDtypeStruct((t, o0), F32), jax.ShapeDtypeStruct((t, gw), F32),
                   jax.ShapeDtypeStruct((t, LANES), F32), jax.ShapeDtypeStruct((t // seq, SUBLANES, seq), F32),
                   jax.ShapeDtypeStruct((t, aw), F32), jax.ShapeDtypeStruct((t, aw), F32),
                   jax.ShapeDtypeStruct((t, aw), F32)),
        grid=(t // tm,),
        in_specs=[pl.BlockSpec((tm, d), row),
                  pl.BlockSpec((1, 6, d), lambda i: (i // spb, 0, 0)),
                  pl.BlockSpec((1, tm // groups, LANES), lambda i: (i, 0, 0)),
                  pl.BlockSpec((1, LANES), const), pl.BlockSpec((CONV_WIDTH, o0), const),
                  pl.BlockSpec((d, o0), const), pl.BlockSpec((d, gw), const),
                  pl.BlockSpec((d, LANES), const), pl.BlockSpec((2 * GDN_HEADS, d), const),
                  pl.BlockSpec((d, aw), const), pl.BlockSpec((d, aw), const),
                  pl.BlockSpec((d, aw), const)],
        out_specs=(pl.BlockSpec((tm, o0), row), pl.BlockSpec((tm, gw), row),
                   pl.BlockSpec((tm, LANES), row),
                   pl.BlockSpec((1, SUBLANES, tm), lambda i: (i // spb, 0, i % spb)),
                   pl.BlockSpec((tm, aw), row), pl.BlockSpec((tm, aw), row),
                   pl.BlockSpec((tm, aw), row)),
        scratch_shapes=[pltpu.VMEM((o0 // HEAD_DIM, tm + 2 * SUBLANES, HEAD_DIM), F32)],
        compiler_params=_params(("arbitrary",)),
        name="inproj",
    )(xf, mod, pos, invf, conv_w, wqkv, wz, wbd, wbdt, wq, wk, wv)


def _gdn_kernel(qkv_ref, z_ref, bdc_ref, bdr_ref, gpc_ref, gpr_ref, nw_ref, o_ref,
                u_s, w_s, qd_s, kd_s, qk_s, a_s, x_s, y_s, st_ref):
    nbat, ts = qkv_ref.shape[0], qkv_ref.shape[1]
    nb = ts // GDN_BLOCK
    gw = GDN_HEADS * HEAD_DIM

    @pl.when(pl.program_id(0) == 0)
    def _():
        st_ref[...] = jnp.zeros(st_ref.shape, F32)

    blk_n = GDN_BLOCK
    hd = HEAD_DIM
    ti = lax.broadcasted_iota(jnp.int32, (ts, ts), 0)
    tj = lax.broadcasted_iota(jnp.int32, (ts, ts), 1)
    same = (ti // blk_n) == (tj // blk_n)
    m_low = jnp.where(same & (tj <= ti), 1.0, 0.0).astype(BF16)
    m_up = jnp.where(same & (ti <= tj), 1.0, 0.0).astype(BF16)

    def split3(x):
        x1 = _bf(x)
        r1 = x - x1.astype(F32)
        x2 = _bf(r1)
        return x1, x2, _bf(r1 - x2.astype(F32))

    beta_c = [_sigmoid(bdc_ref[bb]) for bb in range(nbat)]
    g_c = jnp.concatenate([-jnp.exp(gpc_ref[0:1, :]) * _softplus(bdc_ref[bb] + gpc_ref[1:2, :])
                           for bb in range(nbat)], axis=1)
    g_r = jnp.concatenate([-jnp.exp(gpr_ref[:, 0:1]) * _softplus(bdr_ref[bb] + gpr_ref[:, 1:2])
                           for bb in range(nbat)], axis=0)
    gc_c = sum(jnp.dot(m_low, part, preferred_element_type=F32) for part in split3(g_c))
    gc_r = sum(jnp.dot(part, m_up, preferred_element_type=F32) for part in split3(g_r))

    def cat2(m):
        return jnp.concatenate([m, m], axis=1)

    ii = lax.broadcasted_iota(jnp.int32, (blk_n, blk_n), 0)
    jj = lax.broadcasted_iota(jnp.int32, (blk_n, blk_n), 1)
    lower = cat2(jj <= ii)
    strict = cat2(jj < ii)
    eye = cat2(jnp.where(ii == jj, 1.0, 0.0).astype(F32))
    levels = []
    b = 1
    while b < blk_n:
        levels.append(cat2(((ii // b) == (jj // b) + 1) & (((jj // b) % 2) == 0)))
        b *= 2

    def block_diag(rp):
        n, m = rp.shape[0], rp.shape[1] // 2
        z = jnp.zeros((n, m), rp.dtype)
        return jnp.concatenate([jnp.concatenate([rp[:, :m], z], axis=1),
                                jnp.concatenate([z, rp[:, m:]], axis=1)], axis=0)

    def mm2(lp, rp):
        return jnp.dot(_bf(lp), block_diag(_bf(rp)), preferred_element_type=F32)

    def mm2_nt(lp, rp):
        return lax.dot_general(_bf(lp), block_diag(_bf(rp)), (((1,), (1,)), ((), ())),
                               preferred_element_type=F32)

    chains = [(bb, j, pp) for bb in range(nbat) for j in range(nb) for pp in range(GDN_HEADS // 2)]

    def tile_of(bb, j, pp):
        return bb, slice(j * blk_n, (j + 1) * blk_n), slice(2 * pp * hd, 2 * (pp + 1) * hd)

    def col_pair(arr, rows, col):
        return jnp.concatenate([jnp.broadcast_to(arr[rows, col:col + 1], (blk_n, hd)),
                                jnp.broadcast_to(arr[rows, col + 1:col + 2], (blk_n, hd))], axis=1)

    for c, (bb, j, pp) in enumerate(chains):
        blk = tile_of(bb, j, pp)
        rows, cols = blk[1], blk[2]
        last = slice((j + 1) * blk_n - 1, (j + 1) * blk_n)
        gcol = bb * LANES + GDN_HEADS + 2 * pp
        grow = bb * SUBLANES + GDN_HEADS + 2 * pp
        q = qkv_ref[bb, rows, cols]
        k = qkv_ref[bb, rows, slice(gw + cols.start, gw + cols.stop)]
        v = qkv_ref[bb, rows, slice(2 * gw + cols.start, 2 * gw + cols.stop)]
        beta = col_pair(beta_c[bb], rows, 2 * pp)
        gcc = col_pair(gc_c, rows, gcol)
        gtc = col_pair(gc_c, last, gcol)
        gcr = jnp.concatenate([jnp.broadcast_to(gc_r[grow:grow + 1, rows], (blk_n, hd)),
                               jnp.broadcast_to(gc_r[grow + 1:grow + 2, rows], (blk_n, hd))], axis=1)
        kb = k * beta
        eg = jnp.exp(gcc)
        dm = jnp.where(lower, jnp.exp(gcc - gcr), 0.0)
        a = jnp.where(strict, mm2_nt(kb, k) * dm, 0.0)
        a_s[c] = a
        x_s[c] = eye - jnp.where(levels[0], a, 0.0)
        u_s[blk] = v * beta
        w_s[blk] = kb * eg
        qd_s[blk] = q * eg
        kd_s[blk] = k * jnp.exp(gtc - gcc)
        qk_s[blk] = mm2_nt(q, k) * dm

    for lm in levels[1:]:
        for c in range(len(chains)):
            y_s[c] = mm2(x_s[c], jnp.where(lm, a_s[c], 0.0))
        for c in range(len(chains)):
            xc = x_s[c]
            x_s[c] = xc - mm2(y_s[c], xc)

    for c, (bb, j, pp) in enumerate(chains):
        blk = tile_of(bb, j, pp)
        u, w = _bf(u_s[blk]), _bf(w_s[blk])
        z = jnp.zeros((blk_n, 2 * hd), BF16)
        rhs = jnp.concatenate([jnp.concatenate([u[:, :hd], w[:, :hd], z], axis=1),
                               jnp.concatenate([z, u[:, hd:], w[:, hd:]], axis=1)], axis=0)
        sol = jnp.dot(_bf(x_s[c]), rhs, preferred_element_type=F32)
        u_s[blk] = jnp.concatenate([sol[:, 0:hd], sol[:, 2 * hd:3 * hd]], axis=1)
        w_s[blk] = jnp.concatenate([sol[:, hd:2 * hd], sol[:, 3 * hd:]], axis=1)

    for j in range(nb):
        for bb in range(nbat):
            for pp in range(GDN_HEADS // 2):
                blk = tile_of(bb, j, pp)
                si = bb * (GDN_HEADS // 2) + pp
                last = slice((j + 1) * blk_n - 1, (j + 1) * blk_n)
                gcol = bb * LANES + GDN_HEADS + 2 * pp
                state = st_ref[si]
                proj = mm2(jnp.concatenate([w_s[blk], qd_s[blk]], axis=0), state)
                v_new = u_s[blk] - proj[:blk_n]
                o = proj[blk_n:] + mm2(qk_s[blk], v_new)
                g_last = jnp.exp(jnp.concatenate(
                    [jnp.broadcast_to(gc_c[last, gcol:gcol + 1], (1, hd)),
                     jnp.broadcast_to(gc_c[last, gcol + 1:gcol + 2], (1, hd))], axis=1))
                kd = kd_s[blk]
                upd = jnp.concatenate([_mm_tn(kd[:, :hd], v_new[:, :hd]),
                                       _mm_tn(kd[:, hd:], v_new[:, hd:])], axis=1)
                st_ref[si] = state * g_last + upd
                zz = z_ref[blk]
                halves = []
                for hf in range(2):
                    oh = o[:, hf * hd:(hf + 1) * hd]
                    halves.append(oh * lax.rsqrt(jnp.mean(oh * oh, axis=-1, keepdims=True) + RMS_EPS)
                                  * nw_ref[...])
                o_ref[blk] = (jnp.concatenate(halves, axis=1) * _silu(zz)).astype(o_ref.dtype)


def _gdn(qkv, z, bdc, bdr, a_log, dt_bias, norm_w):
    bsz, seq, _ = qkv.shape
    ts = GDN_TILE
    gw = GDN_HEADS * HEAD_DIM
    zeros4 = jnp.zeros((GDN_HEADS,), F32)
    al = jnp.concatenate([zeros4, a_log])
    db = jnp.concatenate([zeros4, dt_bias])
    gpc = jnp.zeros((2, LANES), F32).at[0, :2 * GDN_HEADS].set(al).at[1, :2 * GDN_HEADS].set(db)
    gpr = jnp.stack([al, db], axis=1)
    row = lambda i: (0, i, 0)
    const = lambda i: (0, 0)
    n_chains = bsz * (ts // GDN_BLOCK) * (GDN_HEADS // 2)
    tile = pltpu.VMEM((bsz, ts, gw), F32)
    mats = pltpu.VMEM((n_chains, GDN_BLOCK, 2 * GDN_BLOCK), F32)
    return pl.pallas_call(
        _gdn_kernel,
        out_shape=jax.ShapeDtypeStruct((bsz, seq, gw), BF16),
        grid=(seq // ts,),
        in_specs=[pl.BlockSpec((bsz, ts, 3 * gw), row), pl.BlockSpec((bsz, ts, gw), row),
                  pl.BlockSpec((bsz, ts, LANES), row),
                  pl.BlockSpec((bsz, SUBLANES, ts), lambda i: (0, 0, i)),
                  pl.BlockSpec((2, LANES), const), pl.BlockSpec((2 * GDN_HEADS, 2), const),
                  pl.BlockSpec((1, HEAD_DIM), const)],
        out_specs=pl.BlockSpec((bsz, ts, gw), row),
        scratch_shapes=[tile, tile, tile, tile, tile,
                        mats, mats, mats,
                        pltpu.VMEM((bsz * GDN_HEADS // 2, HEAD_DIM, 2 * HEAD_DIM), F32)],
        compiler_params=_params(("arbitrary",)),
        name="gdn",
    )(qkv, z, bdc, bdr, gpc, gpr, norm_w.reshape(1, HEAD_DIM))


def _attn_kernel(q_ref, k_ref, v_ref, nw_ref, o_ref,
                 q4, k4, v4, ktail, vtail, m_s, l_s, acc_s, tmp_s, nat_s):
    tq = q_ref.shape[0]
    blk = ATT_BLOCK
    sub = 4
    nq = tq // sub
    t = pl.program_id(2)
    slot = t % 2
    other = 1 - slot
    qi = lax.broadcasted_iota(jnp.int32, (blk, 2 * blk), 0)
    kj = lax.broadcasted_iota(jnp.int32, (blk, 2 * blk), 1)
    band = (kj >= qi) & (kj <= qi + blk)
    first_lo = jnp.where(t > 0, 0, blk)

    @pl.when(t == 0)
    def _():
        k4[...] = jnp.zeros(k4.shape, F32)
        v4[...] = jnp.zeros(v4.shape, F32)
        ktail[...] = jnp.zeros(ktail.shape, F32)
        vtail[...] = jnp.zeros(vtail.shape, F32)

    for r in range(sub):
        rows = slice(r * nq, (r + 1) * nq)
        src = pl.ds(r, nq, stride=sub)
        q4[rows, :] = q_ref[src, :]
        k4[slot, rows, :] = k_ref[src, :]
        v4[slot, rows, :] = v_ref[src, :]

    def block_stats(q, kcat, vcat, from_prev_tile):
        s = _mm_nt(q, kcat)
        mask = band & (kj >= first_lo) if from_prev_tile else band
        s = jnp.where(mask, s, NEG)
        m = jnp.max(s, axis=-1, keepdims=True)
        p = jnp.exp(s - m)
        l = jnp.sum(p, axis=-1, keepdims=True)
        return m, l, _mm(p, vcat)

    def merge(dst, m, l, o):
        m_old = m_s[dst, :]
        m_new = jnp.maximum(m_old, m)
        w_old = jnp.exp(m_old - m_new)
        w_cur = jnp.exp(m - m_new)
        m_s[dst, :] = m_new
        l_s[dst, :] = w_old * l_s[dst, :] + w_cur * l
        acc_s[dst, :] = w_old * acc_s[dst, :] + w_cur * o

    for jb in range(tq // blk):
        cur = slice(jb * blk, (jb + 1) * blk)
        if jb > 0:
            kcat = k_ref[(jb - 1) * blk:(jb + 1) * blk, :]
            vcat = v_ref[(jb - 1) * blk:(jb + 1) * blk, :]
        else:
            kcat = jnp.concatenate([ktail[...], k_ref[cur, :]], axis=0)
            vcat = jnp.concatenate([vtail[...], v_ref[cur, :]], axis=0)
        m, l, o = block_stats(q_ref[cur, :], kcat, vcat, jb == 0)
        tmp_s[0] = jnp.broadcast_to(m, (blk, HEAD_DIM))
        tmp_s[1] = jnp.broadcast_to(l, (blk, HEAD_DIM))
        tmp_s[2] = o
        per = blk // sub
        for r in range(sub):
            dst = slice(r * nq + jb * per, r * nq + (jb + 1) * per)
            src = pl.ds(r, per, stride=sub)
            m_s[dst, :] = tmp_s[0, src, :]
            l_s[dst, :] = tmp_s[1, src, :]
            acc_s[dst, :] = tmp_s[2, src, :]

    for r in range(sub):
        for jb in range(nq // blk):
            base = r * nq + jb * blk
            cur = slice(base, base + blk)
            if jb > 0:
                kcat = k4[slot, base - blk:base + blk, :]
                vcat = v4[slot, base - blk:base + blk, :]
            else:
                last = slice((r + 1) * nq - blk, (r + 1) * nq)
                kcat = jnp.concatenate([k4[other, last, :], k4[slot, cur, :]], axis=0)
                vcat = jnp.concatenate([v4[other, last, :], v4[slot, cur, :]], axis=0)
            m, l, o = block_stats(q4[cur, :], kcat, vcat, jb == 0)
            merge(cur, m, l, o)

    for c in range(16):
        sl = pl.ds((c % sub) * nq + c // sub, blk, stride=sub)
        kcat = jnp.concatenate([k4[other, sl, :], k4[slot, sl, :]], axis=0)
        vcat = jnp.concatenate([v4[other, sl, :], v4[slot, sl, :]], axis=0)
        m, l, o = block_stats(q4[sl, :], kcat, vcat, True)
        merge(sl, m, l, o)

    out = acc_s[...] / l_s[...]
    out = out * lax.rsqrt(jnp.mean(out * out, axis=-1, keepdims=True) + RMS_EPS) * nw_ref[...]
    for r in range(sub):
        nat_s[pl.ds(r, nq, stride=sub), :] = out[r * nq:(r + 1) * nq]
    o_ref[...] = nat_s[...].astype(o_ref.dtype)
    ktail[...] = k_ref[tq - blk:tq, :]
    vtail[...] = v_ref[tq - blk:tq, :]


def _attn(qb, kb, vb, norm_w, bsz, seq):
    t = qb.shape[0]
    tq = ATT_TILE
    spb = seq // tq
    cur = lambda b, h, i: (b * spb + i, h)
    blk = pl.BlockSpec((tq, HEAD_DIM), cur)
    tile = pltpu.VMEM((tq, HEAD_DIM), F32)
    ring = pltpu.VMEM((2, tq, HEAD_DIM), F32)
    tail = pltpu.VMEM((ATT_BLOCK, HEAD_DIM), F32)
    return pl.pallas_call(
        _attn_kernel,
        out_shape=jax.ShapeDtypeStruct((t, ATT_HEADS * HEAD_DIM), BF16),
        grid=(bsz, ATT_HEADS, spb),
        in_specs=[blk, blk, blk, pl.BlockSpec((1, HEAD_DIM), lambda b, h, i: (0, 0))],
        out_specs=blk,
        scratch_shapes=[tile, ring, ring, tail, tail, tile, tile, tile,
                        pltpu.VMEM((3, ATT_BLOCK, HEAD_DIM), F32), tile],
        compiler_params=_params(("parallel", "parallel", "arbitrary")),
        name="attn",
    )(qb, kb, vb, norm_w.reshape(1, HEAD_DIM))


def _layer_norm(y, g, b):
    mu = jnp.mean(y, axis=-1, keepdims=True)
    yc = y - mu
    var = jnp.mean(yc * yc, axis=-1, keepdims=True)
    return yc * lax.rsqrt(var + LN_EPS) * g + b


def _outproj_kernel(oa_ref, ob_ref, x_ref, mod_ref, woa_ref, wob_ref, g_ref, b_ref,
                    wrh_ref, wrl_ref, br_ref,
                    x1_ref, h2_ref, ri_ref, rg_ref, cnt_ref, run_s):
    @pl.when(pl.program_id(0) == 0)
    def _():
        run_s[...] = jnp.zeros(run_s.shape, F32)

    gate1 = mod_ref[0, 2:3, :]
    shift2 = mod_ref[0, 3:4, :]
    scale2 = mod_ref[0, 4:5, :]
    mix = (jnp.dot(oa_ref[...], woa_ref[...], preferred_element_type=F32)
           + jnp.dot(ob_ref[...], wob_ref[...], preferred_element_type=F32))
    x1 = _layer_norm(DEEPNORM_ALPHA * x_ref[...] + (1.0 + gate1) * mix, g_ref[...], b_ref[...])
    x1_ref[...] = x1
    h2 = x1 * (1.0 + scale2) + shift2
    h2_ref[...] = h2

    hi = _bf(h2)
    lo = _bf(h2 - hi.astype(F32))
    logits = (jnp.dot(hi, wrh_ref[...], preferred_element_type=F32)
              + jnp.dot(hi, wrl_ref[...], preferred_element_type=F32)
              + jnp.dot(lo, wrh_ref[...], preferred_element_type=F32)) + br_ref[...]
    tm = logits.shape[0]
    lane_i = lax.broadcasted_iota(jnp.int32, (tm, LANES), 1)
    lane = lane_i.astype(F32)
    lg = jnp.where(lane_i < N_GROUPS, logits, NEG)
    mg = jnp.max(lg, axis=-1, keepdims=True)
    grp = jnp.min(jnp.where(lg == mg, lane, float(LANES)), axis=-1, keepdims=True)
    gate_grp = 1.0 / jnp.sum(jnp.exp(lg - mg), axis=-1, keepdims=True)
    first_lane = N_GROUPS + EXPERTS_PER_GROUP * grp
    sel = (lane >= first_lane) & (lane < first_lane + EXPERTS_PER_GROUP)
    le = jnp.where(sel, logits, NEG)
    v1 = jnp.max(le, axis=-1, keepdims=True)
    i1 = jnp.min(jnp.where(le == v1, lane, float(LANES)), axis=-1, keepdims=True)
    le2 = jnp.where(lane == i1, NEG, le)
    v2 = jnp.max(le2, axis=-1, keepdims=True)
    i2 = jnp.min(jnp.where(le2 == v2, lane, float(LANES)), axis=-1, keepdims=True)
    e21 = jnp.exp(v2 - v1)
    g1 = gate_grp / (1.0 + e21)
    g2 = gate_grp * e21 / (1.0 + e21)

    oh1 = lane == i1
    oh2 = lane == i2
    onehot = jnp.where(oh1 | oh2, 1.0, 0.0).astype(F32)
    ti = lax.broadcasted_iota(jnp.int32, (tm, tm), 0)
    tj = lax.broadcasted_iota(jnp.int32, (tm, tm), 1)
    before = jnp.where(tj < ti, 1.0, 0.0).astype(F32)
    tot = _mm(before, onehot) + run_s[...]
    r1 = jnp.sum(jnp.where(oh1, tot, 0.0), axis=-1, keepdims=True)
    r2 = jnp.sum(jnp.where(oh2, tot, 0.0), axis=-1, keepdims=True)
    run_s[...] = run_s[...] + jnp.sum(onehot, axis=0, keepdims=True)
    cnt_ref[...] = run_s[...]

    cols = jnp.where(lane_i == 0, i1 - N_GROUPS, 0.0)
    cols = jnp.where(lane_i == 1, i2 - N_GROUPS, cols)
    r1_hi = jnp.floor(r1 * (1.0 / 256.0))
    r2_hi = jnp.floor(r2 * (1.0 / 256.0))
    cols = jnp.where(lane_i == 2, r1_hi, cols)
    cols = jnp.where(lane_i == 3, r1 - 256.0 * r1_hi, cols)
    cols = jnp.where(lane_i == 4, r2_hi, cols)
    cols = jnp.where(lane_i == 5, r2 - 256.0 * r2_hi, cols)
    pick = jnp.where(lax.broadcasted_iota(jnp.int32, (SUBLANES, LANES), 0)
                     == lax.broadcasted_iota(jnp.int32, (SUBLANES, LANES), 1), 1.0, 0.0).astype(F32)
    ri_ref[...] = _mm_nt(pick, cols).astype(jnp.int32)
    rg_ref[...] = jnp.where(lane_i == 0, g1, jnp.where(lane_i == 1, g2, 0.0))


def _outproj(oa, ob, xf, mod, w_o, ln_g, ln_b, w_rg, b_rg, w_re, b_re, seq):
    t, d = xf.shape
    tm = PROJ_TILE
    gw = oa.shape[1]
    wo = _bf(w_o)
    wr = jnp.zeros((d, LANES), F32).at[:, :N_GROUPS].set(w_rg).at[:, N_GROUPS:N_GROUPS + N_EXPERTS].set(w_re)
    wrh = _bf(wr)
    wrl = _bf(wr - wrh.astype(F32))
    br = jnp.zeros((1, LANES), F32).at[0, :N_GROUPS].set(b_rg).at[0, N_GROUPS:N_GROUPS + N_EXPERTS].set(b_re)
    spb = seq // tm
    row = lambda i: (i, 0)
    const = lambda i: (0, 0)
    return pl.pallas_call(
        _outproj_kernel,
        out_shape=(jax.ShapeDtypeStruct((t, d), F32), jax.ShapeDtypeStruct((t, d), F32),
                   jax.ShapeDtypeStruct((SUBLANES, t), jnp.int32), jax.ShapeDtypeStruct((t, LANES), F32),
                   jax.ShapeDtypeStruct((1, LANES), F32)),
        grid=(t // tm,),
        in_specs=[pl.BlockSpec((tm, gw), row), pl.BlockSpec((tm, gw), row), pl.BlockSpec((tm, d), row),
                  pl.BlockSpec((1, 6, d), lambda i: (i // spb, 0, 0)),
                  pl.BlockSpec((gw, d), const), pl.BlockSpec((gw, d), const),
                  pl.BlockSpec((1, d), const), pl.BlockSpec((1, d), const),
                  pl.BlockSpec((d, LANES), const), pl.BlockSpec((d, LANES), const),
                  pl.BlockSpec((1, LANES), const)],
        out_specs=(pl.BlockSpec((tm, d), row), pl.BlockSpec((tm, d), row),
                   pl.BlockSpec((SUBLANES, tm), lambda i: (0, i)), pl.BlockSpec((tm, LANES), row),
                   pl.BlockSpec((1, LANES), const)),
        scratch_shapes=[pltpu.VMEM((1, LANES), F32)],
        compiler_params=_params(("arbitrary",)),
        name="outproj",
    )(oa, ob, xf, mod, wo[:gw], wo[gw:], ln_g.reshape(1, d), ln_b.reshape(1, d), wrh, wrl, br)


def _dispatch_kernel(d1_ref, d2_ref, h_ref, xs_ref, sem):
    tm = h_ref.shape[0]
    i = pl.program_id(0)

    def row_copy(tk, dest):
        return pltpu.make_async_copy(h_ref.at[pl.ds(tk, 1), :], xs_ref.at[pl.ds(dest, 1), :], sem)

    def issue(g, carry):
        for u in range(DMA_UNROLL):
            tk = g * DMA_UNROLL + u
            tok = i * tm + tk
            row_copy(tk, d1_ref[tok]).start(priority=0)
            row_copy(tk, d2_ref[tok]).start(priority=1)
        return carry

    lax.fori_loop(0, tm // DMA_UNROLL, issue, 0)

    tile_copy = pltpu.make_async_copy(h_ref, xs_ref.at[pl.ds(0, tm), :], sem)
    tile_copy.wait()
    tile_copy.wait()


def _dispatch(h2, d1, d2):
    t, d = h2.shape
    tm = DISPATCH_TILE
    return pl.pallas_call(
        _dispatch_kernel,
        out_shape=jax.ShapeDtypeStruct((2 * t, d), F32),
        grid_spec=pltpu.PrefetchScalarGridSpec(
            num_scalar_prefetch=2,
            grid=(t // tm,),
            in_specs=[pl.BlockSpec((tm, d), lambda i, *_: (i, 0))],
            out_specs=pl.BlockSpec(memory_space=pl.ANY),
            scratch_shapes=[pltpu.SemaphoreType.DMA]),
        compiler_params=_params(("arbitrary",)),
        name="dispatch",
    )(d1, d2, h2)


def _experts_kernel(wb_ref, we_ref, lo_ref, hi_ref, nw_ref, first_ref, ring_ref, next_ref,
                    xs_ref, wg_hbm, wu_hbm, wd_hbm, ys_ref,
                    wg_f, wu_f, wd_f, wg_s, wu_s, wd_s, sem):
    w = pl.program_id(0)

    def fetch(expert, slot):
        return (pltpu.make_async_copy(wg_hbm.at[expert], wg_f.at[slot], sem.at[slot]),
                pltpu.make_async_copy(wu_hbm.at[expert], wu_f.at[slot], sem.at[slot]),
                pltpu.make_async_copy(wd_hbm.at[expert], wd_f.at[slot], sem.at[slot]))

    @pl.when(w == 0)
    def _():
        for cp in fetch(we_ref[0], 0):
            cp.start()

    @pl.when((first_ref[w] == 1) & (w < nw_ref[0]))
    def _():
        slot = ring_ref[w]
        for cp in fetch(we_ref[w], slot):
            cp.wait()

        @pl.when(next_ref[w] >= 0)
        def _():
            for cp in fetch(next_ref[w], 1 - slot):
                cp.start()

        wg_s[...] = _bf(wg_f[slot])
        wu_s[...] = _bf(wu_f[slot])
        wd_s[...] = _bf(wd_f[slot])

    @pl.when(w < nw_ref[0])
    def _():
        x = _bf(xs_ref[...])
        hid = (_silu(jnp.dot(x, wg_s[...], preferred_element_type=F32))
               * jnp.dot(x, wu_s[...], preferred_element_type=F32))
        y = jnp.dot(_bf(hid), wd_s[...], preferred_element_type=F32)
        row = lax.broadcasted_iota(jnp.int32, (y.shape[0], 1), 0)
        mine = (row >= lo_ref[w]) & (row < hi_ref[w])

        @pl.when(lo_ref[w] == 0)
        def _():
            ys_ref[...] = jnp.where(mine, y, 0.0)

        @pl.when(lo_ref[w] > 0)
        def _():
            ys_ref[...] = jnp.where(mine, y, ys_ref[...])


def _experts(xs, item_block, item_expert, item_lo, item_hi, n_items, w_gate, w_up, w_down):
    n_slots, d = xs.shape
    ff = w_gate.shape[2]
    bm = EXPERT_BLOCK
    n = item_block.shape[0]
    idx = jnp.arange(n, dtype=jnp.int32)
    first = jnp.concatenate([jnp.ones((1,), jnp.int32),
                             (item_expert[1:] != item_expert[:-1]).astype(jnp.int32)])
    ring = (jnp.cumsum(first) - 1) % 2
    next_first = lax.cummin(jnp.where(first == 1, idx, n), reverse=True)
    next_first = jnp.concatenate([next_first[1:], jnp.full((1,), n, jnp.int32)])
    nxt = jnp.where(next_first < n, item_expert[jnp.minimum(next_first, n - 1)], -1).astype(jnp.int32)
    slot = lambda w, *_: (_[0][w], 0)
    return pl.pallas_call(
        _experts_kernel,
        out_shape=jax.ShapeDtypeStruct((n_slots, d), F32),
        grid_spec=pltpu.PrefetchScalarGridSpec(
            num_scalar_prefetch=8,
            grid=(n,),
            in_specs=[pl.BlockSpec((bm, d), slot),
                      pl.BlockSpec(memory_space=pl.ANY), pl.BlockSpec(memory_space=pl.ANY),
                      pl.BlockSpec(memory_space=pl.ANY)],
            out_specs=pl.BlockSpec((bm, d), slot),
            scratch_shapes=[pltpu.VMEM((2, d, ff), F32), pltpu.VMEM((2, d, ff), F32),
                            pltpu.VMEM((2, ff, d), F32),
                            pltpu.VMEM((d, ff), BF16), pltpu.VMEM((d, ff), BF16),
                            pltpu.VMEM((ff, d), BF16), pltpu.SemaphoreType.DMA((2,))]),
        compiler_params=_params(("arbitrary",)),
        name="experts",
    )(item_block, item_expert, item_lo, item_hi, n_items, first, ring.astype(jnp.int32), nxt,
      xs, w_gate, w_up, w_down)


def _combine_kernel(d1_ref, d2_ref, ys_ref, rg_ref, x1_ref, mod_ref, g_ref, b_ref, o_ref, ya, yb, sem):
    tm = x1_ref.shape[0]
    i = pl.program_id(0)
    n = pl.num_programs(0)

    def row_copy(dest, buf, slot, tk):
        return pltpu.make_async_copy(ys_ref.at[pl.ds(dest, 1), :], buf.at[slot, pl.ds(tk, 1), :],
                                     sem.at[slot])

    slot = i % 2
    gate2 = mod_ref[0, 5:6, :]

    def issue_group(step, dst_slot, g):
        for u in range(DMA_UNROLL):
            tk = g * DMA_UNROLL + u
            tok = step * tm + tk
            row_copy(d1_ref[tok], ya, dst_slot, tk).start(priority=0)
            row_copy(d2_ref[tok], yb, dst_slot, tk).start(priority=1)

    def gather_tile(step, dst_slot):
        def issue(g, carry):
            issue_group(step, dst_slot, g)
            return carry

        lax.fori_loop(0, tm // DMA_UNROLL, issue, 0)

    @pl.when(i == 0)
    def _():
        gather_tile(0, 0)

    @pl.when(i + 1 < n)
    def _():
        gather_tile(i + 1, 1 - slot)

    pltpu.make_async_copy(ys_ref.at[pl.ds(0, tm), :], ya.at[slot], sem.at[slot]).wait()
    pltpu.make_async_copy(ys_ref.at[pl.ds(0, tm), :], yb.at[slot], sem.at[slot]).wait()

    rg = rg_ref[...]
    y = rg[:, 0:1] * ya[slot] + rg[:, 1:2] * yb[slot]
    o_ref[...] = _layer_norm(DEEPNORM_ALPHA * x1_ref[...] + (1.0 + gate2) * y, g_ref[...], b_ref[...])


def _combine(ys, d1, d2, rg, x1, mod, ln_g, ln_b, seq):
    t, d = x1.shape
    tm = ROW_TILE
    spb = seq // tm
    row = lambda i, *_: (i, 0)
    const = lambda i, *_: (0, 0)
    buf = pltpu.VMEM((2, tm, d), F32)
    return pl.pallas_call(
        _combine_kernel,
        out_shape=jax.ShapeDtypeStruct((t, d), F32),
        grid_spec=pltpu.PrefetchScalarGridSpec(
            num_scalar_prefetch=2,
            grid=(t // tm,),
            in_specs=[pl.BlockSpec(memory_space=pl.ANY),
                      pl.BlockSpec((tm, LANES), row), pl.BlockSpec((tm, d), row),
                      pl.BlockSpec((1, 6, d), lambda i, *_: (i // spb, 0, 0)),
                      pl.BlockSpec((1, d), const), pl.BlockSpec((1, d), const)],
            out_specs=pl.BlockSpec((tm, d), row),
            scratch_shapes=[buf, buf, pltpu.SemaphoreType.DMA((2,))]),
        compiler_params=_params(("arbitrary",)),
        name="combine",
    )(d1, d2, ys, rg, x1, mod, ln_g.reshape(1, d), ln_b.reshape(1, d))


def _layer(x, c, positions, w_ada, b_ada, w_in, conv_w, a_log, dt_bias, gdn_norm_w, attn_norm_w,
           w_o, ln1_g, ln1_b, w_rg, b_rg, w_re, b_re, w_gate, w_up, w_down, ln2_g, ln2_b):
    bsz, seq, d = x.shape
    t = bsz * seq
    xf = x.reshape(t, d)
    mod = _ada(c, w_ada, b_ada)
    qkv, z, bdc, bdr, qb, kb, vb = _inproj(xf, mod, positions.reshape(t, 1), w_in, conv_w, seq)
    gw = GDN_HEADS * HEAD_DIM
    oa = _gdn(qkv.reshape(bsz, seq, 3 * gw), z.reshape(bsz, seq, gw), bdc.reshape(bsz, seq, LANES), bdr,
              a_log, dt_bias, gdn_norm_w).reshape(t, gw)
    ob = _attn(qb, kb, vb, attn_norm_w, bsz, seq)
    x1, h2, ri, rg, cnt = _outproj(oa, ob, xf, mod, w_o, ln1_g, ln1_b, w_rg, b_rg, w_re, b_re, seq)

    bm = EXPERT_BLOCK
    counts = cnt[0, N_GROUPS:N_GROUPS + N_EXPERTS].astype(jnp.int32)
    seg_end = jnp.cumsum(counts)
    seg_start = seg_end - counts
    first_blk = seg_start // bm
    n_per = jnp.where(counts > 0, (seg_end - 1) // bm - first_blk + 1, 0)
    item_end = jnp.cumsum(n_per)
    n_items = item_end[-1:]
    max_items = (2 * t) // bm + N_EXPERTS - 1
    w_idx = jnp.minimum(jnp.arange(max_items, dtype=jnp.int32), n_items[0] - 1)
    item_expert = jnp.minimum(jnp.sum(item_end[None, :] <= w_idx[:, None], axis=1), N_EXPERTS - 1).astype(jnp.int32)
    item_block = first_blk[item_expert] + w_idx - (item_end - n_per)[item_expert]
    item_lo = jnp.maximum(seg_start[item_expert] - item_block * bm, 0)
    item_hi = jnp.minimum(seg_end[item_expert] - item_block * bm, bm)
    expert_ids = jnp.arange(N_EXPERTS, dtype=jnp.int32)[:, None]

    def seg_of(e):
        return jnp.sum(jnp.where(e[None, :] == expert_ids, seg_start[:, None], 0), axis=0)

    d1 = seg_of(ri[0]) + ri[2] * 256 + ri[3]
    d2 = seg_of(ri[1]) + ri[4] * 256 + ri[5]

    xs = _dispatch(h2, d1, d2)
    ys = _experts(xs, item_block, item_expert, item_lo, item_hi, n_items, w_gate, w_up, w_down)
    out = _combine(ys, d1, d2, rg, x1, mod, ln2_g, ln2_b, seq)
    return out.reshape(bsz, seq, d)


def kernel(x, c, positions, w_ada, b_ada, w_in, conv_w, a_log, dt_bias, gdn_norm_w, attn_norm_w, w_o, ln1_g, ln1_b, w_router_group, b_router_group, w_router_expert, b_router_expert, w_gate, w_up, w_down, ln2_g, ln2_b):
    assert w_ada.shape[0] == DEPTH
    return _layer(x, c, positions, w_ada[0], b_ada[0], w_in[0], conv_w[0], a_log[0], dt_bias[0],
                  gdn_norm_w[0], attn_norm_w[0], w_o[0], ln1_g[0], ln1_b[0],
                  w_router_group[0], b_router_group[0], w_router_expert[0], b_router_expert[0],
                  w_gate[0], w_up[0], w_down[0], ln2_g[0], ln2_b[0])
```

```python
import functools
import math

import jax
import jax.numpy as jnp
from jax import lax
from jax.experimental import pallas as pl
from jax.experimental.pallas import tpu as pltpu

F32 = jnp.float32
BF16 = jnp.bfloat16
HIGHEST = lax.Precision.HIGHEST

GDN_HEADS = 4
ATT_HEADS = 4
HEAD_DIM = 128
CONV_WIDTH = 4
DILATED_PATTERNS = ((128, 1), (512, 4), (2048, 16))
ROPE_THETA = 500000.0
ROPE_DIMS = HEAD_DIM // 4
N_GROUPS = 4
EXPERTS_PER_GROUP = 8
N_EXPERTS = N_GROUPS * EXPERTS_PER_GROUP
DEPTH = 1
DEEPNORM_ALPHA = (2.0 * DEPTH) ** 0.25
LN_EPS = 1e-5
RMS_EPS = 1e-6

LANES = 128
SUBLANES = 8
VMEM_LIMIT = 48 * 1024 * 1024

GDN_BLOCK = 128
GDN_TILE = 256
ATT_BLOCK = 128
ATT_TILE = 2048
PROJ_TILE = 512
ROW_TILE = 256
DISPATCH_TILE = 1024
EXPERT_BLOCK = 256
OUTPROJ_PARTS = 2
DMA_UNROLL = 8
NEG = -1e30


def _bf(x):
    return x.astype(BF16)


def _mm(a, b):
    return jnp.dot(_bf(a), _bf(b), preferred_element_type=F32)


def _mm_nt(a, b):
    return lax.dot_general(_bf(a), _bf(b), (((1,), (1,)), ((), ())), preferred_element_type=F32)


def _mm_tn(a, b):
    return lax.dot_general(_bf(a), _bf(b), (((0,), (0,)), ((), ())), preferred_element_type=F32)


def _mm_f32(a, b):
    return jnp.dot(a, b, preferred_element_type=F32, precision=HIGHEST)


def _sigmoid(x):
    return 1.0 / (1.0 + jnp.exp(-x))


def _silu(x):
    return x * _sigmoid(x)


def _softplus(x):
    return jnp.maximum(x, 0.0) + jnp.log(1.0 + jnp.exp(-jnp.abs(x)))


def _params(sem):
    return pltpu.CompilerParams(dimension_semantics=sem, vmem_limit_bytes=VMEM_LIMIT)


def _ada_kernel(ct_ref, w_ref, b_ref, o_ref, *, bsz):
    sc = _silu(ct_ref[...])
    w = w_ref[...]
    rows = [jnp.sum(w * sc[:, b:b + 1], axis=0, keepdims=True) for b in range(bsz)]
    rows.append(jnp.zeros((o_ref.shape[0] - bsz, w.shape[1]), F32))
    o_ref[...] = jnp.concatenate(rows, axis=0) + b_ref[...]


def _ada(c, w_ada, b_ada):
    bsz, d = c.shape
    n = w_ada.shape[1]
    tn = 512
    assert bsz <= SUBLANES
    ct = jnp.zeros((d, LANES), F32).at[:, :bsz].set(c.T)
    out = pl.pallas_call(
        functools.partial(_ada_kernel, bsz=bsz),
        out_shape=jax.ShapeDtypeStruct((SUBLANES, n), F32),
        grid=(n // tn,),
        in_specs=[pl.BlockSpec((d, LANES), lambda j: (0, 0)),
                  pl.BlockSpec((d, tn), lambda j: (0, j)),
                  pl.BlockSpec((1, tn), lambda j: (0, j))],
        out_specs=pl.BlockSpec((SUBLANES, tn), lambda j: (0, j)),
        compiler_params=_params(("parallel",)),
        name="ada",
    )(ct, w_ada, b_ada.reshape(1, n))
    return out[:bsz].reshape(bsz, 6, d)


def _inproj_kernel(x_ref, mod_ref, pos_ref, invf_ref, convw_ref, wqkv_ref, wz_ref, wbd_ref, wbdt_ref,
                   wq_ref, wk_ref, wv_ref,
                   qkv_ref, z_ref, bdc_ref, bdr_ref, qb_ref, kb_ref, vb_ref, cbuf, *, steps_per_seq):
    tm = x_ref.shape[0]
    halo = SUBLANES
    n_slabs = cbuf.shape[0]

    @pl.when(pl.program_id(0) % steps_per_seq == 0)
    def _():
        cbuf[:, 0:halo, :] = jnp.zeros((n_slabs, halo, LANES), F32)

    half = ROPE_DIMS // 2
    groups = LANES // half
    lane = lax.broadcasted_iota(jnp.int32, (1, LANES), 1)
    first = lane < half
    rot = lane < ROPE_DIMS
    ang = pos_ref[0].astype(F32) * invf_ref[...]
    cos_c = jnp.cos(ang)
    sin_c = jnp.sin(ang)
    cos_parts, sin_parts = [], []
    for j in range(groups):
        lo_sh = (LANES - half * j) % LANES
        hi_sh = (LANES - half * j + half) % LANES
        c_lo = pltpu.roll(cos_c, lo_sh, 1) if lo_sh else cos_c
        c_hi = pltpu.roll(cos_c, hi_sh, 1) if hi_sh else cos_c
        s_lo = pltpu.roll(sin_c, lo_sh, 1) if lo_sh else sin_c
        s_hi = pltpu.roll(sin_c, hi_sh, 1) if hi_sh else sin_c
        cos_parts.append(jnp.where(first, c_lo, jnp.where(rot, c_hi, 1.0)))
        sin_parts.append(jnp.where(first, -s_lo, jnp.where(rot, s_hi, 0.0)))
    cosv = jnp.concatenate(cos_parts, axis=0)
    sin_signed = jnp.concatenate(sin_parts, axis=0)

    def rope(y):
        outs = []
        for hh in range(ATT_HEADS):
            yh = y[:, hh * HEAD_DIM:(hh + 1) * HEAD_DIM]
            partner = jnp.where(first, pltpu.roll(yh, LANES - half, 1), pltpu.roll(yh, half, 1))
            outs.append(yh * cosv + partner * sin_signed)
        return jnp.concatenate(outs, axis=1)

    def conv_slice(s):
        sl = slice(s * HEAD_DIM, (s + 1) * HEAD_DIM)
        off = halo - (CONV_WIDTH - 1)
        acc = convw_ref[0:1, sl] * cbuf[s, off:off + tm, :]
        for j in range(1, CONV_WIDTH):
            acc = acc + convw_ref[j:j + 1, sl] * cbuf[s, off + j:off + j + tm, :]
        cbuf[s, 0:halo, :] = cbuf[s, tm:tm + halo, :]
        y = _silu(acc)
        if s < 2 * GDN_HEADS:
            y = y * lax.rsqrt(jnp.sum(y * y, axis=-1, keepdims=True) + RMS_EPS)
        if s < GDN_HEADS:
            y = y * (HEAD_DIM ** -0.5)
        qkv_ref[:, sl] = y

    shift = mod_ref[0, 0:1, :]
    scale = mod_ref[0, 1:2, :]
    h = _bf(x_ref[...] * (1.0 + scale) + shift)
    chunk = 2 * HEAD_DIM
    n_chunks = n_slabs // 2

    def project_chunk(c):
        pre = jnp.dot(h, wqkv_ref[:, c * chunk:(c + 1) * chunk], preferred_element_type=F32)
        cbuf[2 * c, halo:halo + tm, :] = pre[:, :HEAD_DIM]
        cbuf[2 * c + 1, halo:halo + tm, :] = pre[:, HEAD_DIM:]

    def conv_chunk(c):
        conv_slice(2 * c)
        conv_slice(2 * c + 1)

    project_chunk(0)
    for c in range(1, n_chunks):
        project_chunk(c)
        conv_chunk(c - 1)
    qb = jnp.dot(h, wq_ref[...], preferred_element_type=F32)
    conv_chunk(n_chunks - 1)
    kb = jnp.dot(h, wk_ref[...], preferred_element_type=F32)
    qb_ref[...] = rope(qb) * (HEAD_DIM ** -0.5)
    vb_ref[...] = jnp.dot(h, wv_ref[...], preferred_element_type=F32)
    kb_ref[...] = rope(kb)
    z_ref[...] = jnp.dot(h, wz_ref[...], preferred_element_type=F32)
    bdc_ref[...] = jnp.dot(h, wbd_ref[...], preferred_element_type=F32)
    bdr_ref[0] = lax.dot_general(wbdt_ref[...], h, (((1,), (1,)), ((), ())),
                                 preferred_element_type=F32)


def _inproj(xf, mod, pos, w_in, conv_w, seq):
    t, d = xf.shape
    tm = PROJ_TILE
    gw = GDN_HEADS * HEAD_DIM
    aw = ATT_HEADS * HEAD_DIM
    o0 = 3 * gw
    o1 = o0 + gw
    o2 = o1 + 2 * GDN_HEADS
    wb = _bf(w_in)
    wqkv, wz = wb[:, :o0], wb[:, o0:o1]
    wbd_n = wb[:, o1:o2]
    wbd = jnp.zeros((d, LANES), BF16).at[:, :2 * GDN_HEADS].set(wbd_n)
    wbdt = wbd_n.T
    wq, wk, wv = wb[:, o2:o2 + aw], wb[:, o2 + aw:o2 + 2 * aw], wb[:, o2 + 2 * aw:o2 + 3 * aw]
    half = ROPE_DIMS // 2
    groups = LANES // half
    inv_freq = ROPE_THETA ** (-jnp.arange(half, dtype=F32) * 2.0 / ROPE_DIMS)
    invf = jnp.tile(inv_freq, groups).reshape(1, LANES)
    pos = jnp.repeat(pos.reshape(t // tm, groups, tm // groups).transpose(0, 2, 1), half, axis=2)
    spb = seq // tm
    row = lambda i: (i, 0)
    const = lambda i: (0, 0)
    return pl.pallas_call(
        functools.partial(_inproj_kernel, steps_per_seq=spb),
        out_shape=(jax.ShapeDtypeStruct((t, o0), F32), jax.ShapeDtypeStruct((t, gw), F32),
                   jax.ShapeDtypeStruct((t, LANES), F32), jax.ShapeDtypeStruct((t // seq, SUBLANES, seq), F32),
                   jax.ShapeDtypeStruct((t, aw), F32), jax.ShapeDtypeStruct((t, aw), F32),
                   jax.ShapeDtypeStruct((t, aw), F32)),
        grid=(t // tm,),
        in_specs=[pl.BlockSpec((tm, d), row),
                  pl.BlockSpec((1, 6, d), lambda i: (i // spb, 0, 0)),
                  pl.BlockSpec((1, tm // groups, LANES), lambda i: (i, 0, 0)),
                  pl.BlockSpec((1, LANES), const), pl.BlockSpec((CONV_WIDTH, o0), const),
                  pl.BlockSpec((d, o0), const), pl.BlockSpec((d, gw), const),
                  pl.BlockSpec((d, LANES), const), pl.BlockSpec((2 * GDN_HEADS, d), const),
                  pl.BlockSpec((d, aw), const), pl.BlockSpec((d, aw), const),
                  pl.BlockSpec((d, aw), const)],
        out_specs=(pl.BlockSpec((tm, o0), row), pl.BlockSpec((tm, gw), row),
                   pl.BlockSpec((tm, LANES), row),
                   pl.BlockSpec((1, SUBLANES, tm), lambda i: (i // spb, 0, i % spb)),
                   pl.BlockSpec((tm, aw), row), pl.BlockSpec((tm, aw), row),
                   pl.BlockSpec((tm, aw), row)),
        scratch_shapes=[pltpu.VMEM((o0 // HEAD_DIM, tm + 2 * SUBLANES, HEAD_DIM), F32)],
        compiler_params=_params(("arbitrary",)),
        name="inproj",
    )(xf, mod, pos, invf, conv_w, wqkv, wz, wbd, wbdt, wq, wk, wv)


def _gdn_kernel(qkv_ref, z_ref, bdc_ref, bdr_ref, gpc_ref, gpr_ref, nw_ref, o_ref,
                u_s, w_s, qd_s, kd_s, qk_s, a_s, x_s, y_s, st_ref):
    nbat, ts = qkv_ref.shape[0], qkv_ref.shape[1]
    nb = ts // GDN_BLOCK
    gw = GDN_HEADS * HEAD_DIM

    @pl.when(pl.program_id(0) == 0)
    def _():
        st_ref[...] = jnp.zeros(st_ref.shape, F32)

    blk_n = GDN_BLOCK
    hd = HEAD_DIM
    ti = lax.broadcasted_iota(jnp.int32, (ts, ts), 0)
    tj = lax.broadcasted_iota(jnp.int32, (ts, ts), 1)
    same = (ti // blk_n) == (tj // blk_n)
    m_low = jnp.where(same & (tj <= ti), 1.0, 0.0).astype(BF16)
    m_up = jnp.where(same & (ti <= tj), 1.0, 0.0).astype(BF16)

    def split3(x):
        x1 = _bf(x)
        r1 = x - x1.astype(F32)
        x2 = _bf(r1)
        return x1, x2, _bf(r1 - x2.astype(F32))

    beta_c = [_sigmoid(bdc_ref[bb]) for bb in range(nbat)]
    g_c = jnp.concatenate([-jnp.exp(gpc_ref[0:1, :]) * _softplus(bdc_ref[bb] + gpc_ref[1:2, :])
                           for bb in range(nbat)], axis=1)
    g_r = jnp.concatenate([-jnp.exp(gpr_ref[:, 0:1]) * _softplus(bdr_ref[bb] + gpr_ref[:, 1:2])
                           for bb in range(nbat)], axis=0)
    gc_c = sum(jnp.dot(m_low, part, preferred_element_type=F32) for part in split3(g_c))
    gc_r = sum(jnp.dot(part, m_up, preferred_element_type=F32) for part in split3(g_r))

    def cat2(m):
        return jnp.concatenate([m, m], axis=1)

    ii = lax.broadcasted_iota(jnp.int32, (blk_n, blk_n), 0)
    jj = lax.broadcasted_iota(jnp.int32, (blk_n, blk_n), 1)
    lower = cat2(jj <= ii)
    strict = cat2(jj < ii)
    eye = cat2(jnp.where(ii == jj, 1.0, 0.0).astype(F32))
    levels = []
    b = 1
    while b < blk_n:
        levels.append(cat2(((ii // b) == (jj // b) + 1) & (((jj // b) % 2) == 0)))
        b *= 2

    def block_diag(rp):
        n, m = rp.shape[0], rp.shape[1] // 2
        z = jnp.zeros((n, m), rp.dtype)
        return jnp.concatenate([jnp.concatenate([rp[:, :m], z], axis=1),
                                jnp.concatenate([z, rp[:, m:]], axis=1)], axis=0)

    def mm2(lp, rp):
        return jnp.dot(_bf(lp), block_diag(_bf(rp)), preferred_element_type=F32)

    def mm2_nt(lp, rp):
        return lax.dot_general(_bf(lp), block_diag(_bf(rp)), (((1,), (1,)), ((), ())),
                               preferred_element_type=F32)

    chains = [(bb, j, pp) for bb in range(nbat) for j in range(nb) for pp in range(GDN_HEADS // 2)]

    def tile_of(bb, j, pp):
        return bb, slice(j * blk_n, (j + 1) * blk_n), slice(2 * pp * hd, 2 * (pp + 1) * hd)

    def col_pair(arr, rows, col):
        return jnp.concatenate([jnp.broadcast_to(arr[rows, col:col + 1], (blk_n, hd)),
                                jnp.broadcast_to(arr[rows, col + 1:col + 2], (blk_n, hd))], axis=1)

    for c, (bb, j, pp) in enumerate(chains):
        blk = tile_of(bb, j, pp)
        rows, cols = blk[1], blk[2]
        last = slice((j + 1) * blk_n - 1, (j + 1) * blk_n)
        gcol = bb * LANES + GDN_HEADS + 2 * pp
        grow = bb * SUBLANES + GDN_HEADS + 2 * pp
        q = qkv_ref[bb, rows, cols]
        k = qkv_ref[bb, rows, slice(gw + cols.start, gw + cols.stop)]
        v = qkv_ref[bb, rows, slice(2 * gw + cols.start, 2 * gw + cols.stop)]
        beta = col_pair(beta_c[bb], rows, 2 * pp)
        gcc = col_pair(gc_c, rows, gcol)
        gtc = col_pair(gc_c, last, gcol)
        gcr = jnp.concatenate([jnp.broadcast_to(gc_r[grow:grow + 1, rows], (blk_n, hd)),
                               jnp.broadcast_to(gc_r[grow + 1:grow + 2, rows], (blk_n, hd))], axis=1)
        kb = k * beta
        eg = jnp.exp(gcc)
        dm = jnp.where(lower, jnp.exp(gcc - gcr), 0.0)
        a = jnp.where(strict, mm2_nt(kb, k) * dm, 0.0)
        a_s[c] = a
        x_s[c] = eye - jnp.where(levels[0], a, 0.0)
        u_s[blk] = v * beta
        w_s[blk] = kb * eg
        qd_s[blk] = q * eg
        kd_s[blk] = k * jnp.exp(gtc - gcc)
        qk_s[blk] = mm2_nt(q, k) * dm

    for lm in levels[1:]:
        for c in range(len(chains)):
            y_s[c] = mm2(x_s[c], jnp.where(lm, a_s[c], 0.0))
        for c in range(len(chains)):
            xc = x_s[c]
            x_s[c] = xc - mm2(y_s[c], xc)

    for c, (bb, j, pp) in enumerate(chains):
        blk = tile_of(bb, j, pp)
        u, w = _bf(u_s[blk]), _bf(w_s[blk])
        z = jnp.zeros((blk_n, 2 * hd), BF16)
        rhs = jnp.concatenate([jnp.concatenate([u[:, :hd], w[:, :hd], z], axis=1),
                               jnp.concatenate([z, u[:, hd:], w[:, hd:]], axis=1)], axis=0)
        sol = jnp.dot(_bf(x_s[c]), rhs, preferred_element_type=F32)
        u_s[blk] = jnp.concatenate([sol[:, 0:hd], sol[:, 2 * hd:3 * hd]], axis=1)
        w_s[blk] = jnp.concatenate([sol[:, hd:2 * hd], sol[:, 3 * hd:]], axis=1)

    for j in range(nb):
        for bb in range(nbat):
            for pp in range(GDN_HEADS // 2):
                blk = tile_of(bb, j, pp)
                si = bb * (GDN_HEADS // 2) + pp
                last = slice((j + 1) * blk_n - 1, (j + 1) * blk_n)
                gcol = bb * LANES + GDN_HEADS + 2 * pp
                state = st_ref[si]
                proj = mm2(jnp.concatenate([w_s[blk], qd_s[blk]], axis=0), state)
                v_new = u_s[blk] - proj[:blk_n]
                o = proj[blk_n:] + mm2(qk_s[blk], v_new)
                g_last = jnp.exp(jnp.concatenate(
                    [jnp.broadcast_to(gc_c[last, gcol:gcol + 1], (1, hd)),
                     jnp.broadcast_to(gc_c[last, gcol + 1:gcol + 2], (1, hd))], axis=1))
                kd = kd_s[blk]
                upd = jnp.concatenate([_mm_tn(kd[:, :hd], v_new[:, :hd]),
                                       _mm_tn(kd[:, hd:], v_new[:, hd:])], axis=1)
                st_ref[si] = state * g_last + upd
                zz = z_ref[blk]
                halves = []
                for hf in range(2):
                    oh = o[:, hf * hd:(hf + 1) * hd]
                    halves.append(oh * lax.rsqrt(jnp.mean(oh * oh, axis=-1, keepdims=True) + RMS_EPS)
                                  * nw_ref[...])
                o_ref[blk] = (jnp.concatenate(halves, axis=1) * _silu(zz)).astype(o_ref.dtype)


def _gdn(qkv, z, bdc, bdr, a_log, dt_bias, norm_w):
    bsz, seq, _ = qkv.shape
    ts = GDN_TILE
    gw = GDN_HEADS * HEAD_DIM
    zeros4 = jnp.zeros((GDN_HEADS,), F32)
    al = jnp.concatenate([zeros4, a_log])
    db = jnp.concatenate([zeros4, dt_bias])
    gpc = jnp.zeros((2, LANES), F32).at[0, :2 * GDN_HEADS].set(al).at[1, :2 * GDN_HEADS].set(db)
    gpr = jnp.stack([al, db], axis=1)
    row = lambda i: (0, i, 0)
    const = lambda i: (0, 0)
    n_chains = bsz * (ts // GDN_BLOCK) * (GDN_HEADS // 2)
    tile = pltpu.VMEM((bsz, ts, gw), F32)
    mats = pltpu.VMEM((n_chains, GDN_BLOCK, 2 * GDN_BLOCK), F32)
    return pl.pallas_call(
        _gdn_kernel,
        out_shape=jax.ShapeDtypeStruct((bsz, seq, gw), BF16),
        grid=(seq // ts,),
        in_specs=[pl.BlockSpec((bsz, ts, 3 * gw), row), pl.BlockSpec((bsz, ts, gw), row),
                  pl.BlockSpec((bsz, ts, LANES), row),
                  pl.BlockSpec((bsz, SUBLANES, ts), lambda i: (0, 0, i)),
                  pl.BlockSpec((2, LANES), const), pl.BlockSpec((2 * GDN_HEADS, 2), const),
                  pl.BlockSpec((1, HEAD_DIM), const)],
        out_specs=pl.BlockSpec((bsz, ts, gw), row),
        scratch_shapes=[tile, tile, tile, tile, tile,
                        mats, mats, mats,
                        pltpu.VMEM((bsz * GDN_HEADS // 2, HEAD_DIM, 2 * HEAD_DIM), F32)],
        compiler_params=_params(("arbitrary",)),
        name="gdn",
    )(qkv, z, bdc, bdr, gpc, gpr, norm_w.reshape(1, HEAD_DIM))


def _attn_kernel(q_ref, k_ref, v_ref, nw_ref, o_ref,
                 q4, k4, v4, ktail, vtail, m_s, l_s, acc_s, tmp_s, nat_s):
    tq = q_ref.shape[0]
    blk = ATT_BLOCK
    sub = 4
    nq = tq // sub
    t = pl.program_id(2)
    slot = t % 2
    other = 1 - slot
    qi = lax.broadcasted_iota(jnp.int32, (blk, 2 * blk), 0)
    kj = lax.broadcasted_iota(jnp.int32, (blk, 2 * blk), 1)
    band = (kj >= qi) & (kj <= qi + blk)
    first_lo = jnp.where(t > 0, 0, blk)

    @pl.when(t == 0)
    def _():
        k4[...] = jnp.zeros(k4.shape, F32)
        v4[...] = jnp.zeros(v4.shape, F32)
        ktail[...] = jnp.zeros(ktail.shape, F32)
        vtail[...] = jnp.zeros(vtail.shape, F32)

    for r in range(sub):
        rows = slice(r * nq, (r + 1) * nq)
        src = pl.ds(r, nq, stride=sub)
        q4[rows, :] = q_ref[src, :]
        k4[slot, rows, :] = k_ref[src, :]
        v4[slot, rows, :] = v_ref[src, :]

    def block_stats(q, kcat, vcat, from_prev_tile):
        s = _mm_nt(q, kcat)
        mask = band & (kj >= first_lo) if from_prev_tile else band
        s = jnp.where(mask, s, NEG)
        m = jnp.max(s, axis=-1, keepdims=True)
        p = jnp.exp(s - m)
        l = jnp.sum(p, axis=-1, keepdims=True)
        return m, l, _mm(p, vcat)

    def merge(dst, m, l, o):
        m_old = m_s[dst, :]
        m_new = jnp.maximum(m_old, m)
        w_old = jnp.exp(m_old - m_new)
        w_cur = jnp.exp(m - m_new)
        m_s[dst, :] = m_new
        l_s[dst, :] = w_old * l_s[dst, :] + w_cur * l
        acc_s[dst, :] = w_old * acc_s[dst, :] + w_cur * o

    for jb in range(tq // blk):
        cur = slice(jb * blk, (jb + 1) * blk)
        if jb > 0:
            kcat = k_ref[(jb - 1) * blk:(jb + 1) * blk, :]
            vcat = v_ref[(jb - 1) * blk:(jb + 1) * blk, :]
        else:
            kcat = jnp.concatenate([ktail[...], k_ref[cur, :]], axis=0)
            vcat = jnp.concatenate([vtail[...], v_ref[cur, :]], axis=0)
        m, l, o = block_stats(q_ref[cur, :], kcat, vcat, jb == 0)
        tmp_s[0] = jnp.broadcast_to(m, (blk, HEAD_DIM))
        tmp_s[1] = jnp.broadcast_to(l, (blk, HEAD_DIM))
        tmp_s[2] = o
        per = blk // sub
        for r in range(sub):
            dst = slice(r * nq + jb * per, r * nq + (jb + 1) * per)
            src = pl.ds(r, per, stride=sub)
            m_s[dst, :] = tmp_s[0, src, :]
            l_s[dst, :] = tmp_s[1, src, :]
            acc_s[dst, :] = tmp_s[2, src, :]

    for r in range(sub):
        for jb in range(nq // blk):
            base = r * nq + jb * blk
            cur = slice(base, base + blk)
            if jb > 0:
                kcat = k4[slot, base - blk:base + blk, :]
                vcat = v4[slot, base - blk:base + blk, :]
            else:
                last = slice((r + 1) * nq - blk, (r + 1) * nq)
                kcat = jnp.concatenate([k4[other, last, :], k4[slot, cur, :]], axis=0)
                vcat = jnp.concatenate([v4[other, last, :], v4[slot, cur, :]], axis=0)
            m, l, o = block_stats(q4[cur, :], kcat, vcat, jb == 0)
            merge(cur, m, l, o)

    for c in range(16):
        sl = pl.ds((c % sub) * nq + c // sub, blk, stride=sub)
        kcat = jnp.concatenate([k4[other, sl, :], k4[slot, sl, :]], axis=0)
        vcat = jnp.concatenate([v4[other, sl, :], v4[slot, sl, :]], axis=0)
        m, l, o = block_stats(q4[sl, :], kcat, vcat, True)
        merge(sl, m, l, o)

    out = acc_s[...] / l_s[...]
    out = out * lax.rsqrt(jnp.mean(out * out, axis=-1, keepdims=True) + RMS_EPS) * nw_ref[...]
    for r in range(sub):
        nat_s[pl.ds(r, nq, stride=sub), :] = out[r * nq:(r + 1) * nq]
    o_ref[...] = nat_s[...].astype(o_ref.dtype)
    ktail[...] = k_ref[tq - blk:tq, :]
    vtail[...] = v_ref[tq - blk:tq, :]


def _attn(qb, kb, vb, norm_w, bsz, seq):
    t = qb.shape[0]
    tq = ATT_TILE
    spb = seq // tq
    cur = lambda b, h, i: (b * spb + i, h)
    blk = pl.BlockSpec((tq, HEAD_DIM), cur)
    tile = pltpu.VMEM((tq, HEAD_DIM), F32)
    ring = pltpu.VMEM((2, tq, HEAD_DIM), F32)
    tail = pltpu.VMEM((ATT_BLOCK, HEAD_DIM), F32)
    return pl.pallas_call(
        _attn_kernel,
        out_shape=jax.ShapeDtypeStruct((t, ATT_HEADS * HEAD_DIM), BF16),
        grid=(bsz, ATT_HEADS, spb),
        in_specs=[blk, blk, blk, pl.BlockSpec((1, HEAD_DIM), lambda b, h, i: (0, 0))],
        out_specs=blk,
        scratch_shapes=[tile, ring, ring, tail, tail, tile, tile, tile,
                        pltpu.VMEM((3, ATT_BLOCK, HEAD_DIM), F32), tile],
        compiler_params=_params(("parallel", "parallel", "arbitrary")),
        name="attn",
    )(qb, kb, vb, norm_w.reshape(1, HEAD_DIM))


def _layer_norm(y, g, b):
    mu = jnp.mean(y, axis=-1, keepdims=True)
    yc = y - mu
    var = jnp.mean(yc * yc, axis=-1, keepdims=True)
    return yc * lax.rsqrt(var + LN_EPS) * g + b


def _outproj_kernel(oa_ref, ob_ref, x_ref, mod_ref, wo_ref, g_ref, b_ref,
                    wrh_ref, wrl_ref, br_ref,
                    x1_ref, h2_ref, ri_ref, rg_ref, cnt_ref, run_s, mix_s, hi_s, lo_s):
    @pl.when(pl.program_id(0) == 0)
    def _():
        run_s[...] = jnp.zeros(run_s.shape, F32)

    tm = x_ref.shape[0]
    gate1 = mod_ref[0, 2:3, :]
    shift2 = mod_ref[0, 3:4, :]
    scale2 = mod_ref[0, 4:5, :]
    n_parts = OUTPROJ_PARTS
    part = tm // n_parts
    grp_rows = 2 * SUBLANES

    def project(p):
        rows = slice(p * part, (p + 1) * part)
        mix_s[rows, :] = jnp.dot(jnp.concatenate([oa_ref[rows, :], ob_ref[rows, :]], axis=1), wo_ref[...],
                                 preferred_element_type=F32)

    def normalise(p):
        for r0 in range(p * part, (p + 1) * part, grp_rows):
            rows = slice(r0, r0 + grp_rows)
            x1 = _layer_norm(DEEPNORM_ALPHA * x_ref[rows, :] + (1.0 + gate1) * mix_s[rows, :],
                             g_ref[...], b_ref[...])
            x1_ref[rows, :] = x1
            h2 = x1 * (1.0 + scale2) + shift2
            h2_ref[rows, :] = h2
            hi = _bf(h2)
            hi_s[rows, :] = hi
            lo_s[rows, :] = _bf(h2 - hi.astype(F32))

    def route(p):
        rows = slice(p * part, (p + 1) * part)
        hi = hi_s[rows, :]
        return (jnp.dot(hi, wrh_ref[...], preferred_element_type=F32)
                + jnp.dot(hi, wrl_ref[...], preferred_element_type=F32)
                + jnp.dot(lo_s[rows, :], wrh_ref[...], preferred_element_type=F32))

    project(0)
    parts = []
    for p in range(n_parts):
        if p + 1 < n_parts:
            project(p + 1)
        normalise(p)
        parts.append(route(p))
    logits = jnp.concatenate(parts, axis=0) + br_ref[...]
    lane_i = lax.broadcasted_iota(jnp.int32, (tm, LANES), 1)
    lane = lane_i.astype(F32)
    lg = jnp.where(lane_i < N_GROUPS, logits, NEG)
    mg = jnp.max(lg, axis=-1, keepdims=True)
    grp = jnp.min(jnp.where(lg == mg, lane, float(LANES)), axis=-1, keepdims=True)
    gate_grp = 1.0 / jnp.sum(jnp.exp(lg - mg), axis=-1, keepdims=True)
    first_lane = N_GROUPS + EXPERTS_PER_GROUP * grp
    sel = (lane >= first_lane) & (lane < first_lane + EXPERTS_PER_GROUP)
    le = jnp.where(sel, logits, NEG)
    v1 = jnp.max(le, axis=-1, keepdims=True)
    i1 = jnp.min(jnp.where(le == v1, lane, float(LANES)), axis=-1, keepdims=True)
    le2 = jnp.where(lane == i1, NEG, le)
    v2 = jnp.max(le2, axis=-1, keepdims=True)
    i2 = jnp.min(jnp.where(le2 == v2, lane, float(LANES)), axis=-1, keepdims=True)
    e21 = jnp.exp(v2 - v1)
    g1 = gate_grp / (1.0 + e21)
    g2 = gate_grp * e21 / (1.0 + e21)

    oh1 = lane == i1
    oh2 = lane == i2
    onehot = jnp.where(oh1 | oh2, 1.0, 0.0).astype(F32)
    ti = lax.broadcasted_iota(jnp.int32, (tm, tm), 0)
    tj = lax.broadcasted_iota(jnp.int32, (tm, tm), 1)
    before = jnp.where(tj < ti, 1.0, 0.0).astype(F32)
    tot = _mm(before, onehot) + run_s[...]
    r1 = jnp.sum(jnp.where(oh1, tot, 0.0), axis=-1, keepdims=True)
    r2 = jnp.sum(jnp.where(oh2, tot, 0.0), axis=-1, keepdims=True)
    run_s[...] = run_s[...] + jnp.sum(onehot, axis=0, keepdims=True)
    cnt_ref[...] = run_s[...]

    cols = jnp.where(lane_i == 0, i1 - N_GROUPS, 0.0)
    cols = jnp.where(lane_i == 1, i2 - N_GROUPS, cols)
    r1_hi = jnp.floor(r1 * (1.0 / 256.0))
    r2_hi = jnp.floor(r2 * (1.0 / 256.0))
    cols = jnp.where(lane_i == 2, r1_hi, cols)
    cols = jnp.where(lane_i == 3, r1 - 256.0 * r1_hi, cols)
    cols = jnp.where(lane_i == 4, r2_hi, cols)
    cols = jnp.where(lane_i == 5, r2 - 256.0 * r2_hi, cols)
    pick = jnp.where(lax.broadcasted_iota(jnp.int32, (SUBLANES, LANES), 0)
                     == lax.broadcasted_iota(jnp.int32, (SUBLANES, LANES), 1), 1.0, 0.0).astype(F32)
    ri_ref[...] = _mm_nt(pick, cols).astype(jnp.int32)
    rg_ref[...] = jnp.where(lane_i == 0, g1, jnp.where(lane_i == 1, g2, 0.0))


def _outproj(oa, ob, xf, mod, w_o, ln_g, ln_b, w_rg, b_rg, w_re, b_re, seq):
    t, d = xf.shape
    tm = PROJ_TILE
    gw = oa.shape[1]
    wo = _bf(w_o)
    wr = jnp.zeros((d, LANES), F32).at[:, :N_GROUPS].set(w_rg).at[:, N_GROUPS:N_GROUPS + N_EXPERTS].set(w_re)
    wrh = _bf(wr)
    wrl = _bf(wr - wrh.astype(F32))
    br = jnp.zeros((1, LANES), F32).at[0, :N_GROUPS].set(b_rg).at[0, N_GROUPS:N_GROUPS + N_EXPERTS].set(b_re)
    spb = seq // tm
    row = lambda i: (i, 0)
    const = lambda i: (0, 0)
    return pl.pallas_call(
        _outproj_kernel,
        out_shape=(jax.ShapeDtypeStruct((t, d), F32), jax.ShapeDtypeStruct((t, d), F32),
                   jax.ShapeDtypeStruct((SUBLANES, t), jnp.int32), jax.ShapeDtypeStruct((t, LANES), F32),
                   jax.ShapeDtypeStruct((1, LANES), F32)),
        grid=(t // tm,),
        in_specs=[pl.BlockSpec((tm, gw), row), pl.BlockSpec((tm, gw), row), pl.BlockSpec((tm, d), row),
                  pl.BlockSpec((1, 6, d), lambda i: (i // spb, 0, 0)),
                  pl.BlockSpec((2 * gw, d), const),
                  pl.BlockSpec((1, d), const), pl.BlockSpec((1, d), const),
                  pl.BlockSpec((d, LANES), const), pl.BlockSpec((d, LANES), const),
                  pl.BlockSpec((1, LANES), const)],
        out_specs=(pl.BlockSpec((tm, d), row), pl.BlockSpec((tm, d), row),
                   pl.BlockSpec((SUBLANES, tm), lambda i: (0, i)), pl.BlockSpec((tm, LANES), row),
                   pl.BlockSpec((1, LANES), const)),
        scratch_shapes=[pltpu.VMEM((1, LANES), F32), pltpu.VMEM((tm, d), F32),
                        pltpu.VMEM((tm, d), BF16), pltpu.VMEM((tm, d), BF16)],
        compiler_params=_params(("arbitrary",)),
        name="outproj",
    )(oa, ob, xf, mod, wo, ln_g.reshape(1, d), ln_b.reshape(1, d), wrh, wrl, br)


def _dispatch_kernel(d1_ref, d2_ref, h_ref, xs_ref, sem):
    tm = h_ref.shape[0]
    i = pl.program_id(0)

    def row_copy(tk, dest):
        return pltpu.make_async_copy(h_ref.at[pl.ds(tk, 1), :], xs_ref.at[pl.ds(dest, 1), :], sem)

    def issue(g, carry):
        for u in range(DMA_UNROLL):
            tk = g * DMA_UNROLL + u
            tok = i * tm + tk
            row_copy(tk, d1_ref[tok]).start(priority=0)
            row_copy(tk, d2_ref[tok]).start(priority=1)
        return carry

    lax.fori_loop(0, tm // DMA_UNROLL, issue, 0)

    tile_copy = pltpu.make_async_copy(h_ref, xs_ref.at[pl.ds(0, tm), :], sem)
    tile_copy.wait()
    tile_copy.wait()


def _dispatch(h2, d1, d2):
    t, d = h2.shape
    tm = DISPATCH_TILE
    return pl.pallas_call(
        _dispatch_kernel,
        out_shape=jax.ShapeDtypeStruct((2 * t, d), F32),
        grid_spec=pltpu.PrefetchScalarGridSpec(
            num_scalar_prefetch=2,
            grid=(t // tm,),
            in_specs=[pl.BlockSpec((tm, d), lambda i, *_: (i, 0))],
            out_specs=pl.BlockSpec(memory_space=pl.ANY),
            scratch_shapes=[pltpu.SemaphoreType.DMA]),
        compiler_params=_params(("arbitrary",)),
        name="dispatch",
    )(d1, d2, h2)


def _experts_kernel(wb_ref, we_ref, lo_ref, hi_ref, nw_ref, first_ref, ring_ref, next_ref,
                    xs_ref, wg_hbm, wu_hbm, wd_hbm, ys_ref,
                    wg_f, wu_f, wd_f, wg_s, wu_s, wd_s, sem):
    w = pl.program_id(0)

    def fetch(expert, slot):
        return (pltpu.make_async_copy(wg_hbm.at[expert], wg_f.at[slot], sem.at[slot]),
                pltpu.make_async_copy(wu_hbm.at[expert], wu_f.at[slot], sem.at[slot]),
                pltpu.make_async_copy(wd_hbm.at[expert], wd_f.at[slot], sem.at[slot]))

    @pl.when(w == 0)
    def _():
        for cp in fetch(we_ref[0], 0):
            cp.start()

    @pl.when((first_ref[w] == 1) & (w < nw_ref[0]))
    def _():
        slot = ring_ref[w]
        for cp in fetch(we_ref[w], slot):
            cp.wait()

        @pl.when(next_ref[w] >= 0)
        def _():
            for cp in fetch(next_ref[w], 1 - slot):
                cp.start()

        wg_s[...] = _bf(wg_f[slot])
        wu_s[...] = _bf(wu_f[slot])
        wd_s[...] = _bf(wd_f[slot])

    @pl.when(w < nw_ref[0])
    def _():
        x = _bf(xs_ref[...])
        hid = (_silu(jnp.dot(x, wg_s[...], preferred_element_type=F32))
               * jnp.dot(x, wu_s[...], preferred_element_type=F32))
        y = jnp.dot(_bf(hid), wd_s[...], preferred_element_type=F32)
        row = lax.broadcasted_iota(jnp.int32, (y.shape[0], 1), 0)
        mine = (row >= lo_ref[w]) & (row < hi_ref[w])

        @pl.when(lo_ref[w] == 0)
        def _():
            ys_ref[...] = jnp.where(mine, y, 0.0)

        @pl.when(lo_ref[w] > 0)
        def _():
            ys_ref[...] = jnp.where(mine, y, ys_ref[...])


def _experts(xs, item_block, item_expert, item_lo, item_hi, n_items, w_gate, w_up, w_down):
    n_slots, d = xs.shape
    ff = w_gate.shape[2]
    bm = EXPERT_BLOCK
    n = item_block.shape[0]
    idx = jnp.arange(n, dtype=jnp.int32)
    first = jnp.concatenate([jnp.ones((1,), jnp.int32),
                             (item_expert[1:] != item_expert[:-1]).astype(jnp.int32)])
    ring = (jnp.cumsum(first) - 1) % 2
    next_first = lax.cummin(jnp.where(first == 1, idx, n), reverse=True)
    next_first = jnp.concatenate([next_first[1:], jnp.full((1,), n, jnp.int32)])
    nxt = jnp.where(next_first < n, item_expert[jnp.minimum(next_first, n - 1)], -1).astype(jnp.int32)
    slot = lambda w, *_: (_[0][w], 0)
    return pl.pallas_call(
        _experts_kernel,
        out_shape=jax.ShapeDtypeStruct((n_slots, d), F32),
        grid_spec=pltpu.PrefetchScalarGridSpec(
            num_scalar_prefetch=8,
            grid=(n,),
            in_specs=[pl.BlockSpec((bm, d), slot),
                      pl.BlockSpec(memory_space=pl.ANY), pl.BlockSpec(memory_space=pl.ANY),
                      pl.BlockSpec(memory_space=pl.ANY)],
            out_specs=pl.BlockSpec((bm, d), slot),
            scratch_shapes=[pltpu.VMEM((2, d, ff), F32), pltpu.VMEM((2, d, ff), F32),
                            pltpu.VMEM((2, ff, d), F32),
                            pltpu.VMEM((d, ff), BF16), pltpu.VMEM((d, ff), BF16),
                            pltpu.VMEM((ff, d), BF16), pltpu.SemaphoreType.DMA((2,))]),
        compiler_params=_params(("arbitrary",)),
        name="experts",
    )(item_block, item_expert, item_lo, item_hi, n_items, first, ring.astype(jnp.int32), nxt,
      xs, w_gate, w_up, w_down)


def _combine_kernel(d1_ref, d2_ref, ys_ref, rg_ref, x1_ref, mod_ref, g_ref, b_ref, o_ref, ya, yb, sem):
    tm = x1_ref.shape[0]
    i = pl.program_id(0)
    n = pl.num_programs(0)

    def row_copy(dest, buf, slot, tk):
        return pltpu.make_async_copy(ys_ref.at[pl.ds(dest, 1), :], buf.at[slot, pl.ds(tk, 1), :],
                                     sem.at[slot])

    slot = i % 2
    gate2 = mod_ref[0, 5:6, :]

    def issue_group(step, dst_slot, g):
        for u in range(DMA_UNROLL):
            tk = g * DMA_UNROLL + u
            tok = step * tm + tk
            row_copy(d1_ref[tok], ya, dst_slot, tk).start(priority=0)
            row_copy(d2_ref[tok], yb, dst_slot, tk).start(priority=1)

    def gather_tile(step, dst_slot):
        def issue(g, carry):
            issue_group(step, dst_slot, g)
            return carry

        lax.fori_loop(0, tm // DMA_UNROLL, issue, 0)

    @pl.when(i == 0)
    def _():
        gather_tile(0, 0)

    @pl.when(i + 1 < n)
    def _():
        gather_tile(i + 1, 1 - slot)

    pltpu.make_async_copy(ys_ref.at[pl.ds(0, tm), :], ya.at[slot], sem.at[slot]).wait()
    pltpu.make_async_copy(ys_ref.at[pl.ds(0, tm), :], yb.at[slot], sem.at[slot]).wait()

    rg = rg_ref[...]
    y = rg[:, 0:1] * ya[slot] + rg[:, 1:2] * yb[slot]
    o_ref[...] = _layer_norm(DEEPNORM_ALPHA * x1_ref[...] + (1.0 + gate2) * y, g_ref[...], b_ref[...])


def _combine(ys, d1, d2, rg, x1, mod, ln_g, ln_b, seq):
    t, d = x1.shape
    tm = ROW_TILE
    spb = seq // tm
    row = lambda i, *_: (i, 0)
    const = lambda i, *_: (0, 0)
    buf = pltpu.VMEM((2, tm, d), F32)
    return pl.pallas_call(
        _combine_kernel,
        out_shape=jax.ShapeDtypeStruct((t, d), F32),
        grid_spec=pltpu.PrefetchScalarGridSpec(
            num_scalar_prefetch=2,
            grid=(t // tm,),
            in_specs=[pl.BlockSpec(memory_space=pl.ANY),
                      pl.BlockSpec((tm, LANES), row), pl.BlockSpec((tm, d), row),
                      pl.BlockSpec((1, 6, d), lambda i, *_: (i // spb, 0, 0)),
                      pl.BlockSpec((1, d), const), pl.BlockSpec((1, d), const)],
            out_specs=pl.BlockSpec((tm, d), row),
            scratch_shapes=[buf, buf, pltpu.SemaphoreType.DMA((2,))]),
        compiler_params=_params(("arbitrary",)),
        name="combine",
    )(d1, d2, ys, rg, x1, mod, ln_g.reshape(1, d), ln_b.reshape(1, d))


def _layer(x, c, positions, w_ada, b_ada, w_in, conv_w, a_log, dt_bias, gdn_norm_w, attn_norm_w,
           w_o, ln1_g, ln1_b, w_rg, b_rg, w_re, b_re, w_gate, w_up, w_down, ln2_g, ln2_b):
    bsz, seq, d = x.shape
    t = bsz * seq
    xf = x.reshape(t, d)
    mod = _ada(c, w_ada, b_ada)
    qkv, z, bdc, bdr, qb, kb, vb = _inproj(xf, mod, positions.reshape(t, 1), w_in, conv_w, seq)
    gw = GDN_HEADS * HEAD_DIM
    oa = _gdn(qkv.reshape(bsz, seq, 3 * gw), z.reshape(bsz, seq, gw), bdc.reshape(bsz, seq, LANES), bdr,
              a_log, dt_bias, gdn_norm_w).reshape(t, gw)
    ob = _attn(qb, kb, vb, attn_norm_w, bsz, seq)
    x1, h2, ri, rg, cnt = _outproj(oa, ob, xf, mod, w_o, ln1_g, ln1_b, w_rg, b_rg, w_re, b_re, seq)

    bm = EXPERT_BLOCK
    counts = cnt[0, N_GROUPS:N_GROUPS + N_EXPERTS].astype(jnp.int32)
    seg_end = jnp.cumsum(counts)
    seg_start = seg_end - counts
    first_blk = seg_start // bm
    n_per = jnp.where(counts > 0, (seg_end - 1) // bm - first_blk + 1, 0)
    item_end = jnp.cumsum(n_per)
    n_items = item_end[-1:]
    max_items = (2 * t) // bm + N_EXPERTS - 1
    w_idx = jnp.minimum(jnp.arange(max_items, dtype=jnp.int32), n_items[0] - 1)
    item_expert = jnp.minimum(jnp.sum(item_end[None, :] <= w_idx[:, None], axis=1), N_EXPERTS - 1).astype(jnp.int32)
    item_block = first_blk[item_expert] + w_idx - (item_end - n_per)[item_expert]
    item_lo = jnp.maximum(seg_start[item_expert] - item_block * bm, 0)
    item_hi = jnp.minimum(seg_end[item_expert] - item_block * bm, bm)
    expert_ids = jnp.arange(N_EXPERTS, dtype=jnp.int32)[:, None]

    def seg_of(e):
        return jnp.sum(jnp.where(e[None, :] == expert_ids, seg_start[:, None], 0), axis=0)

    d1 = seg_of(ri[0]) + ri[2] * 256 + ri[3]
    d2 = seg_of(ri[1]) + ri[4] * 256 + ri[5]

    xs = _dispatch(h2, d1, d2)
    ys = _experts(xs, item_block, item_expert, item_lo, item_hi, n_items, w_gate, w_up, w_down)
    out = _combine(ys, d1, d2, rg, x1, mod, ln2_g, ln2_b, seq)
    return out.reshape(bsz, seq, d)


def kernel(x, c, positions, w_ada, b_ada, w_in, conv_w, a_log, dt_bias, gdn_norm_w, attn_norm_w, w_o, ln1_g, ln1_b, w_router_group, b_router_group, w_router_expert, b_router_expert, w_gate, w_up, w_down, ln2_g, ln2_b):
    assert w_ada.shape[0] == DEPTH
    return _layer(x, c, positions, w_ada[0], b_ada[0], w_in[0], conv_w[0], a_log[0], dt_bias[0],
                  gdn_norm_w[0], attn_norm_w[0], w_o[0], ln1_g[0], ln1_b[0],
                  w_router_group[0], b_router_group[0], w_router_expert[0], b_router_expert[0],
                  w_gate[0], w_up[0], w_down[0], ln2_g[0], ln2_b[0])
```

```python
import functools
import math

import jax
import jax.numpy as jnp
from jax import lax
from jax.experimental import pallas as pl
from jax.experimental.pallas import tpu as pltpu

F32 = jnp.float32
BF16 = jnp.bfloat16
HIGHEST = lax.Precision.HIGHEST

GDN_HEADS = 4
ATT_HEADS = 4
HEAD_DIM = 128
CONV_WIDTH = 4
DILATED_PATTERNS = ((128, 1), (512, 4), (2048, 16))
ROPE_THETA = 500000.0
ROPE_DIMS = HEAD_DIM // 4
N_GROUPS = 4
EXPERTS_PER_GROUP = 8
N_EXPERTS = N_GROUPS * EXPERTS_PER_GROUP
DEPTH = 1
DEEPNORM_ALPHA = (2.0 * DEPTH) ** 0.25
LN_EPS = 1e-5
RMS_EPS = 1e-6

LANES = 128
SUBLANES = 8
VMEM_LIMIT = 48 * 1024 * 1024

GDN_BLOCK = 128
GDN_TILE = 256
ATT_BLOCK = 128
ATT_TILE = 2048
PROJ_TILE = 512
ROW_TILE = 256
DISPATCH_TILE = 1024
EXPERT_BLOCK = 256
OUTPROJ_PARTS = 2
DMA_UNROLL = 8
NEG = -1e30


def _bf(x):
    return x.astype(BF16)


def _mm(a, b):
    return jnp.dot(_bf(a), _bf(b), preferred_element_type=F32)


def _mm_nt(a, b):
    return lax.dot_general(_bf(a), _bf(b), (((1,), (1,)), ((), ())), preferred_element_type=F32)


def _mm_tn(a, b):
    return lax.dot_general(_bf(a), _bf(b), (((0,), (0,)), ((), ())), preferred_element_type=F32)


def _mm_f32(a, b):
    return jnp.dot(a, b, preferred_element_type=F32, precision=HIGHEST)


def _sigmoid(x):
    return 1.0 / (1.0 + jnp.exp(-x))


def _silu(x):
    return x * _sigmoid(x)


def _softplus(x):
    return jnp.maximum(x, 0.0) + jnp.log(1.0 + jnp.exp(-jnp.abs(x)))


def _params(sem):
    return pltpu.CompilerParams(dimension_semantics=sem, vmem_limit_bytes=VMEM_LIMIT)


def _ada_kernel(ct_ref, w_ref, b_ref, o_ref, *, bsz):
    sc = _silu(ct_ref[...])
    w = w_ref[...]
    rows = [jnp.sum(w * sc[:, b:b + 1], axis=0, keepdims=True) for b in range(bsz)]
    rows.append(jnp.zeros((o_ref.shape[0] - bsz, w.shape[1]), F32))
    o_ref[...] = jnp.concatenate(rows, axis=0) + b_ref[...]


def _ada(c, w_ada, b_ada):
    bsz, d = c.shape
    n = w_ada.shape[1]
    tn = 512
    assert bsz <= SUBLANES
    ct = jnp.zeros((d, LANES), F32).at[:, :bsz].set(c.T)
    out = pl.pallas_call(
        functools.partial(_ada_kernel, bsz=bsz),
        out_shape=jax.ShapeDtypeStruct((SUBLANES, n), F32),
        grid=(n // tn,),
        in_specs=[pl.BlockSpec((d, LANES), lambda j: (0, 0)),
                  pl.BlockSpec((d, tn), lambda j: (0, j)),
                  pl.BlockSpec((1, tn), lambda j: (0, j))],
        out_specs=pl.BlockSpec((SUBLANES, tn), lambda j: (0, j)),
        compiler_params=_params(("parallel",)),
        name="ada",
    )(ct, w_ada, b_ada.reshape(1, n))
    return out[:bsz].reshape(bsz, 6, d)


def _inproj_kernel(x_ref, mod_ref, pos_ref, invf_ref, convw_ref, wqkv_ref, wz_ref, wbd_ref, wbdt_ref,
                   wq_ref, wk_ref, wv_ref,
                   qkv_ref, z_ref, bdc_ref, bdr_ref, qb_ref, kb_ref, vb_ref, cbuf, *, steps_per_seq):
    tm = x_ref.shape[0]
    halo = SUBLANES
    n_slabs = cbuf.shape[0]

    @pl.when(pl.program_id(0) % steps_per_seq == 0)
    def _():
        cbuf[:, 0:halo, :] = jnp.zeros((n_slabs, halo, LANES), F32)

    half = ROPE_DIMS // 2
    groups = LANES // half
    lane = lax.broadcasted_iota(jnp.int32, (1, LANES), 1)
    first = lane < half
    rot = lane < ROPE_DIMS
    ang = pos_ref[0].astype(F32) * invf_ref[...]
    cos_c = jnp.cos(ang)
    sin_c = jnp.sin(ang)
    cos_parts, sin_parts = [], []
    for j in range(groups):
        lo_sh = (LANES - half * j) % LANES
        hi_sh = (LANES - half * j + half) % LANES
        c_lo = pltpu.roll(cos_c, lo_sh, 1) if lo_sh else cos_c
        c_hi = pltpu.roll(cos_c, hi_sh, 1) if hi_sh else cos_c
        s_lo = pltpu.roll(sin_c, lo_sh, 1) if lo_sh else sin_c
        s_hi = pltpu.roll(sin_c, hi_sh, 1) if hi_sh else sin_c
        cos_parts.append(jnp.where(first, c_lo, jnp.where(rot, c_hi, 1.0)))
        sin_parts.append(jnp.where(first, -s_lo, jnp.where(rot, s_hi, 0.0)))
    cosv = jnp.concatenate(cos_parts, axis=0)
    sin_signed = jnp.concatenate(sin_parts, axis=0)

    def rope(y):
        outs = []
        for hh in range(ATT_HEADS):
            yh = y[:, hh * HEAD_DIM:(hh + 1) * HEAD_DIM]
            partner = jnp.where(first, pltpu.roll(yh, LANES - half, 1), pltpu.roll(yh, half, 1))
            outs.append(yh * cosv + partner * sin_signed)
        return jnp.concatenate(outs, axis=1)

    def conv_slice(s):
        sl = slice(s * HEAD_DIM, (s + 1) * HEAD_DIM)
        off = halo - (CONV_WIDTH - 1)
        acc = convw_ref[0:1, sl] * cbuf[s, off:off + tm, :]
        for j in range(1, CONV_WIDTH):
            acc = acc + convw_ref[j:j + 1, sl] * cbuf[s, off + j:off + j + tm, :]
        cbuf[s, 0:halo, :] = cbuf[s, tm:tm + halo, :]
        y = _silu(acc)
        if s < 2 * GDN_HEADS:
            y = y * lax.rsqrt(jnp.sum(y * y, axis=-1, keepdims=True) + RMS_EPS)
        if s < GDN_HEADS:
            y = y * (HEAD_DIM ** -0.5)
        qkv_ref[:, sl] = y

    shift = mod_ref[0, 0:1, :]
    scale = mod_ref[0, 1:2, :]
    h = _bf(x_ref[...] * (1.0 + scale) + shift)
    chunk = 2 * HEAD_DIM
    n_chunks = n_slabs // 2

    def project_chunk(c):
        pre = jnp.dot(h, wqkv_ref[:, c * chunk:(c + 1) * chunk], preferred_element_type=F32)
        cbuf[2 * c, halo:halo + tm, :] = pre[:, :HEAD_DIM]
        cbuf[2 * c + 1, halo:halo + tm, :] = pre[:, HEAD_DIM:]

    def conv_chunk(c):
        conv_slice(2 * c)
        conv_slice(2 * c + 1)

    project_chunk(0)
    for c in range(1, n_chunks):
        project_chunk(c)
        conv_chunk(c - 1)
    qb = jnp.dot(h, wq_ref[...], preferred_element_type=F32)
    conv_chunk(n_chunks - 1)
    kb = jnp.dot(h, wk_ref[...], preferred_element_type=F32)
    qb_ref[...] = rope(qb) * (HEAD_DIM ** -0.5)
    vb_ref[...] = jnp.dot(h, wv_ref[...], preferred_element_type=F32)
    kb_ref[...] = rope(kb)
    z_ref[...] = jnp.dot(h, wz_ref[...], preferred_element_type=F32)
    bdc_ref[...] = jnp.dot(h, wbd_ref[...], preferred_element_type=F32)
    bdr_ref[0] = lax.dot_general(wbdt_ref[...], h, (((1,), (1,)), ((), ())),
                                 preferred_element_type=F32)


def _inproj(xf, mod, pos, w_in, conv_w, seq):
    t, d = xf.shape
    tm = PROJ_TILE
    gw = GDN_HEADS * HEAD_DIM
    aw = ATT_HEADS * HEAD_DIM
    o0 = 3 * gw
    o1 = o0 + gw
    o2 = o1 + 2 * GDN_HEADS
    wb = _bf(w_in)
    wqkv, wz = wb[:, :o0], wb[:, o0:o1]
    wbd_n = wb[:, o1:o2]
    wbd = jnp.zeros((d, LANES), BF16).at[:, :2 * GDN_HEADS].set(wbd_n)
    wbdt = wbd_n.T
    wq, wk, wv = wb[:, o2:o2 + aw], wb[:, o2 + aw:o2 + 2 * aw], wb[:, o2 + 2 * aw:o2 + 3 * aw]
    half = ROPE_DIMS // 2
    groups = LANES // half
    inv_freq = ROPE_THETA ** (-jnp.arange(half, dtype=F32) * 2.0 / ROPE_DIMS)
    invf = jnp.tile(inv_freq, groups).reshape(1, LANES)
    pos = jnp.repeat(pos.reshape(t // tm, groups, tm // groups).transpose(0, 2, 1), half, axis=2)
    spb = seq // tm
    row = lambda i: (i, 0)
    const = lambda i: (0, 0)
    return pl.pallas_call(
        functools.partial(_inproj_kernel, steps_per_seq=spb),
        out_shape=(jax.ShapeDtypeStruct((t, o0), F32), jax.ShapeDtypeStruct((t, gw), F32),
                   jax.ShapeDtypeStruct((t, LANES), F32), jax.ShapeDtypeStruct((t // seq, SUBLANES, seq), F32),
                   jax.ShapeDtypeStruct((t, aw), F32), jax.ShapeDtypeStruct((t, aw), F32),
                   jax.ShapeDtypeStruct((t, aw), F32)),
        grid=(t // tm,),
        in_specs=[pl.BlockSpec((tm, d), row),
                  pl.BlockSpec((1, 6, d), lambda i: (i // spb, 0, 0)),
                  pl.BlockSpec((1, tm // groups, LANES), lambda i: (i, 0, 0)),
                  pl.BlockSpec((1, LANES), const), pl.BlockSpec((CONV_WIDTH, o0), const),
                  pl.BlockSpec((d, o0), const), pl.BlockSpec((d, gw), const),
                  pl.BlockSpec((d, LANES), const), pl.BlockSpec((2 * GDN_HEADS, d), const),
                  pl.BlockSpec((d, aw), const), pl.BlockSpec((d, aw), const),
                  pl.BlockSpec((d, aw), const)],
        out_specs=(pl.BlockSpec((tm, o0), row), pl.BlockSpec((tm, gw), row),
                   pl.BlockSpec((tm, LANES), row),
                   pl.BlockSpec((1, SUBLANES, tm), lambda i: (i // spb, 0, i % spb)),
                   pl.BlockSpec((tm, aw), row), pl.BlockSpec((tm, aw), row),
                   pl.BlockSpec((tm, aw), row)),
        scratch_shapes=[pltpu.VMEM((o0 // HEAD_DIM, tm + 2 * SUBLANES, HEAD_DIM), F32)],
        compiler_params=_params(("arbitrary",)),
        name="inproj",
    )(xf, mod, pos, invf, conv_w, wqkv, wz, wbd, wbdt, wq, wk, wv)


def _gdn_kernel(qkv_ref, z_ref, bdc_ref, bdr_ref, gpc_ref, gpr_ref, nw_ref, o_ref,
                u_s, w_s, qd_s, kd_s, qk_s, a_s, x_s, y_s, st_ref):
    nbat, ts = qkv_ref.shape[0], qkv_ref.shape[1]
    nb = ts // GDN_BLOCK
    gw = GDN_HEADS * HEAD_DIM

    @pl.when(pl.program_id(0) == 0)
    def _():
        st_ref[...] = jnp.zeros(st_ref.shape, F32)

    blk_n = GDN_BLOCK
    hd = HEAD_DIM
    ti = lax.broadcasted_iota(jnp.int32, (ts, ts), 0)
    tj = lax.broadcasted_iota(jnp.int32, (ts, ts), 1)
    same = (ti // blk_n) == (tj // blk_n)
    m_low = jnp.where(same & (tj <= ti), 1.0, 0.0).astype(BF16)
    m_up = jnp.where(same & (ti <= tj), 1.0, 0.0).astype(BF16)

    def split3(x):
        x1 = _bf(x)
        r1 = x - x1.astype(F32)
        x2 = _bf(r1)
        return x1, x2, _bf(r1 - x2.astype(F32))

    beta_c = [_sigmoid(bdc_ref[bb]) for bb in range(nbat)]
    g_c = jnp.concatenate([-jnp.exp(gpc_ref[0:1, :]) * _softplus(bdc_ref[bb] + gpc_ref[1:2, :])
                           for bb in range(nbat)], axis=1)
    g_r = jnp.concatenate([-jnp.exp(gpr_ref[:, 0:1]) * _softplus(bdr_ref[bb] + gpr_ref[:, 1:2])
                           for bb in range(nbat)], axis=0)
    gc_c = sum(jnp.dot(m_low, part, preferred_element_type=F32) for part in split3(g_c))
    gc_r = sum(jnp.dot(part, m_up, preferred_element_type=F32) for part in split3(g_r))

    def cat2(m):
        return jnp.concatenate([m, m], axis=1)

    ii = lax.broadcasted_iota(jnp.int32, (blk_n, blk_n), 0)
    jj = lax.broadcasted_iota(jnp.int32, (blk_n, blk_n), 1)
    lower = cat2(jj <= ii)
    strict = cat2(jj < ii)
    eye = cat2(jnp.where(ii == jj, 1.0, 0.0).astype(F32))
    levels = []
    b = 1
    while b < blk_n:
        levels.append(cat2(((ii // b) == (jj // b) + 1) & (((jj // b) % 2) == 0)))
        b *= 2

    def block_diag(rp):
        n, m = rp.shape[0], rp.shape[1] // 2
        z = jnp.zeros((n, m), rp.dtype)
        return jnp.concatenate([jnp.concatenate([rp[:, :m], z], axis=1),
                                jnp.concatenate([z, rp[:, m:]], axis=1)], axis=0)

    def mm2(lp, rp):
        return jnp.dot(_bf(lp), block_diag(_bf(rp)), preferred_element_type=F32)

    def mm2_nt(lp, rp):
        return lax.dot_general(_bf(lp), block_diag(_bf(rp)), (((1,), (1,)), ((), ())),
                               preferred_element_type=F32)

    chains = [(bb, j, pp) for bb in range(nbat) for j in range(nb) for pp in range(GDN_HEADS // 2)]

    def tile_of(bb, j, pp):
        return bb, slice(j * blk_n, (j + 1) * blk_n), slice(2 * pp * hd, 2 * (pp + 1) * hd)

    def col_pair(arr, rows, col):
        return jnp.concatenate([jnp.broadcast_to(arr[rows, col:col + 1], (blk_n, hd)),
                                jnp.broadcast_to(arr[rows, col + 1:col + 2], (blk_n, hd))], axis=1)

    for c, (bb, j, pp) in enumerate(chains):
        blk = tile_of(bb, j, pp)
        rows, cols = blk[1], blk[2]
        last = slice((j + 1) * blk_n - 1, (j + 1) * blk_n)
        gcol = bb * LANES + GDN_HEADS + 2 * pp
        grow = bb * SUBLANES + GDN_HEADS + 2 * pp
        q = qkv_ref[bb, rows, cols]
        k = qkv_ref[bb, rows, slice(gw + cols.start, gw + cols.stop)]
        v = qkv_ref[bb, rows, slice(2 * gw + cols.start, 2 * gw + cols.stop)]
        beta = col_pair(beta_c[bb], rows, 2 * pp)
        gcc = col_pair(gc_c, rows, gcol)
        gtc = col_pair(gc_c, last, gcol)
        gcr = jnp.concatenate([jnp.broadcast_to(gc_r[grow:grow + 1, rows], (blk_n, hd)),
                               jnp.broadcast_to(gc_r[grow + 1:grow + 2, rows], (blk_n, hd))], axis=1)
        kb = k * beta
        eg = jnp.exp(gcc)
        dm = jnp.where(lower, jnp.exp(gcc - gcr), 0.0)
        a = jnp.where(strict, mm2_nt(kb, k) * dm, 0.0)
        a_s[c] = a
        x_s[c] = eye - jnp.where(levels[0], a, 0.0)
        u_s[blk] = v * beta
        w_s[blk] = kb * eg
        qd_s[blk] = q * eg
        kd_s[blk] = k * jnp.exp(gtc - gcc)
        qk_s[blk] = mm2_nt(q, k) * dm

    for lm in levels[1:]:
        for c in range(len(chains)):
            y_s[c] = mm2(x_s[c], jnp.where(lm, a_s[c], 0.0))
        for c in range(len(chains)):
            xc = x_s[c]
            x_s[c] = xc - mm2(y_s[c], xc)

    for c, (bb, j, pp) in enumerate(chains):
        blk = tile_of(bb, j, pp)
        u, w = _bf(u_s[blk]), _bf(w_s[blk])
        z = jnp.zeros((blk_n, 2 * hd), BF16)
        rhs = jnp.concatenate([jnp.concatenate([u[:, :hd], w[:, :hd], z], axis=1),
                               jnp.concatenate([z, u[:, hd:], w[:, hd:]], axis=1)], axis=0)
        sol = jnp.dot(_bf(x_s[c]), rhs, preferred_element_type=F32)
        u_s[blk] = jnp.concatenate([sol[:, 0:hd], sol[:, 2 * hd:3 * hd]], axis=1)
        w_s[blk] = jnp.concatenate([sol[:, hd:2 * hd], sol[:, 3 * hd:]], axis=1)

    for j in range(nb):
        for bb in range(nbat):
            for pp in range(GDN_HEADS // 2):
                blk = tile_of(bb, j, pp)
                si = bb * (GDN_HEADS // 2) + pp
                last = slice((j + 1) * blk_n - 1, (j + 1) * blk_n)
                gcol = bb * LANES + GDN_HEADS + 2 * pp
                state = st_ref[si]
                proj = mm2(jnp.concatenate([w_s[blk], qd_s[blk]], axis=0), state)
                v_new = u_s[blk] - proj[:blk_n]
                o = proj[blk_n:] + mm2(qk_s[blk], v_new)
                g_last = jnp.exp(jnp.concatenate(
                    [jnp.broadcast_to(gc_c[last, gcol:gcol + 1], (1, hd)),
                     jnp.broadcast_to(gc_c[last, gcol + 1:gcol + 2], (1, hd))], axis=1))
                kd = kd_s[blk]
                upd = jnp.concatenate([_mm_tn(kd[:, :hd], v_new[:, :hd]),
                                       _mm_tn(kd[:, hd:], v_new[:, hd:])], axis=1)
                st_ref[si] = state * g_last + upd
                zz = z_ref[blk]
                halves = []
                for hf in range(2):
                    oh = o[:, hf * hd:(hf + 1) * hd]
                    halves.append(oh * lax.rsqrt(jnp.mean(oh * oh, axis=-1, keepdims=True) + RMS_EPS)
                                  * nw_ref[...])
                o_ref[blk] = (jnp.concatenate(halves, axis=1) * _silu(zz)).astype(o_ref.dtype)


def _gdn(qkv, z, bdc, bdr, a_log, dt_bias, norm_w):
    bsz, seq, _ = qkv.shape
    ts = GDN_TILE
    gw = GDN_HEADS * HEAD_DIM
    zeros4 = jnp.zeros((GDN_HEADS,), F32)
    al = jnp.concatenate([zeros4, a_log])
    db = jnp.concatenate([zeros4, dt_bias])
    gpc = jnp.zeros((2, LANES), F32).at[0, :2 * GDN_HEADS].set(al).at[1, :2 * GDN_HEADS].set(db)
    gpr = jnp.stack([al, db], axis=1)
    row = lambda i: (0, i, 0)
    const = lambda i: (0, 0)
    n_chains = bsz * (ts // GDN_BLOCK) * (GDN_HEADS // 2)
    tile = pltpu.VMEM((bsz, ts, gw), F32)
    mats = pltpu.VMEM((n_chains, GDN_BLOCK, 2 * GDN_BLOCK), F32)
    return pl.pallas_call(
        _gdn_kernel,
        out_shape=jax.ShapeDtypeStruct((bsz, seq, gw), BF16),
        grid=(seq // ts,),
        in_specs=[pl.BlockSpec((bsz, ts, 3 * gw), row), pl.BlockSpec((bsz, ts, gw), row),
                  pl.BlockSpec((bsz, ts, LANES), row),
                  pl.BlockSpec((bsz, SUBLANES, ts), lambda i: (0, 0, i)),
                  pl.BlockSpec((2, LANES), const), pl.BlockSpec((2 * GDN_HEADS, 2), const),
                  pl.BlockSpec((1, HEAD_DIM), const)],
        out_specs=pl.BlockSpec((bsz, ts, gw), row),
        scratch_shapes=[tile, tile, tile, tile, tile,
                        mats, mats, mats,
                        pltpu.VMEM((bsz * GDN_HEADS // 2, HEAD_DIM, 2 * HEAD_DIM), F32)],
        compiler_params=_params(("arbitrary",)),
        name="gdn",
    )(qkv, z, bdc, bdr, gpc, gpr, norm_w.reshape(1, HEAD_DIM))


def _attn_kernel(q_ref, k_ref, v_ref, nw_ref, o_ref,
                 q4, k4, v4, ktail, vtail, m_s, l_s, acc_s, tmp_s, nat_s):
    tq = q_ref.shape[0]
    blk = ATT_BLOCK
    sub = 4
    nq = tq // sub
    t = pl.program_id(2)
    slot = t % 2
    other = 1 - slot
    qi = lax.broadcasted_iota(jnp.int32, (blk, 2 * blk), 0)
    kj = lax.broadcasted_iota(jnp.int32, (blk, 2 * blk), 1)
    band = (kj >= qi) & (kj <= qi + blk)
    first_lo = jnp.where(t > 0, 0, blk)

    @pl.when(t == 0)
    def _():
        k4[...] = jnp.zeros(k4.shape, F32)
        v4[...] = jnp.zeros(v4.shape, F32)
        ktail[...] = jnp.zeros(ktail.shape, F32)
        vtail[...] = jnp.zeros(vtail.shape, F32)

    for r in range(sub):
        rows = slice(r * nq, (r + 1) * nq)
        src = pl.ds(r, nq, stride=sub)
        q4[rows, :] = q_ref[src, :]
        k4[slot, rows, :] = k_ref[src, :]
        v4[slot, rows, :] = v_ref[src, :]

    def block_stats(q, kcat, vcat, from_prev_tile):
        s = _mm_nt(q, kcat)
        mask = band & (kj >= first_lo) if from_prev_tile else band
        s = jnp.where(mask, s, NEG)
        m = jnp.max(s, axis=-1, keepdims=True)
        p = jnp.exp(s - m)
        l = jnp.sum(p, axis=-1, keepdims=True)
        return m, l, _mm(p, vcat)

    def merge(dst, m, l, o):
        m_old = m_s[dst, :]
        m_new = jnp.maximum(m_old, m)
        w_old = jnp.exp(m_old - m_new)
        w_cur = jnp.exp(m - m_new)
        m_s[dst, :] = m_new
        l_s[dst, :] = w_old * l_s[dst, :] + w_cur * l
        acc_s[dst, :] = w_old * acc_s[dst, :] + w_cur * o

    for jb in range(tq // blk):
        cur = slice(jb * blk, (jb + 1) * blk)
        if jb > 0:
            kcat = k_ref[(jb - 1) * blk:(jb + 1) * blk, :]
            vcat = v_ref[(jb - 1) * blk:(jb + 1) * blk, :]
        else:
            kcat = jnp.concatenate([ktail[...], k_ref[cur, :]], axis=0)
            vcat = jnp.concatenate([vtail[...], v_ref[cur, :]], axis=0)
        m, l, o = block_stats(q_ref[cur, :], kcat, vcat, jb == 0)
        tmp_s[0] = jnp.broadcast_to(m, (blk, HEAD_DIM))
        tmp_s[1] = jnp.broadcast_to(l, (blk, HEAD_DIM))
        tmp_s[2] = o
        per = blk // sub
        for r in range(sub):
            dst = slice(r * nq + jb * per, r * nq + (jb + 1) * per)
            src = pl.ds(r, per, stride=sub)
            m_s[dst, :] = tmp_s[0, src, :]
            l_s[dst, :] = tmp_s[1, src, :]
            acc_s[dst, :] = tmp_s[2, src, :]

    for r in range(sub):
        for jb in range(nq // blk):
            base = r * nq + jb * blk
            cur = slice(base, base + blk)
            if jb > 0:
                kcat = k4[slot, base - blk:base + blk, :]
                vcat = v4[slot, base - blk:base + blk, :]
            else:
                last = slice((r + 1) * nq - blk, (r + 1) * nq)
                kcat = jnp.concatenate([k4[other, last, :], k4[slot, cur, :]], axis=0)
                vcat = jnp.concatenate([v4[other, last, :], v4[slot, cur, :]], axis=0)
            m, l, o = block_stats(q4[cur, :], kcat, vcat, jb == 0)
            merge(cur, m, l, o)

    for c in range(16):
        sl = pl.ds((c % sub) * nq + c // sub, blk, stride=sub)
        kcat = jnp.concatenate([k4[other, sl, :], k4[slot, sl, :]], axis=0)
        vcat = jnp.concatenate([v4[other, sl, :], v4[slot, sl, :]], axis=0)
        m, l, o = block_stats(q4[sl, :], kcat, vcat, True)
        merge(sl, m, l, o)

    out = acc_s[...] / l_s[...]
    out = out * lax.rsqrt(jnp.mean(out * out, axis=-1, keepdims=True) + RMS_EPS) * nw_ref[...]
    for r in range(sub):
        nat_s[pl.ds(r, nq, stride=sub), :] = out[r * nq:(r + 1) * nq]
    o_ref[...] = nat_s[...].astype(o_ref.dtype)
    ktail[...] = k_ref[tq - blk:tq, :]
    vtail[...] = v_ref[tq - blk:tq, :]


def _attn(qb, kb, vb, norm_w, bsz, seq):
    t = qb.shape[0]
    tq = ATT_TILE
    spb = seq // tq
    cur = lambda b, h, i: (b * spb + i, h)
    blk = pl.BlockSpec((tq, HEAD_DIM), cur)
    tile = pltpu.VMEM((tq, HEAD_DIM), F32)
    ring = pltpu.VMEM((2, tq, HEAD_DIM), F32)
    tail = pltpu.VMEM((ATT_BLOCK, HEAD_DIM), F32)
    return pl.pallas_call(
        _attn_kernel,
        out_shape=jax.ShapeDtypeStruct((t, ATT_HEADS * HEAD_DIM), BF16),
        grid=(bsz, ATT_HEADS, spb),
        in_specs=[blk, blk, blk, pl.BlockSpec((1, HEAD_DIM), lambda b, h, i: (0, 0))],
        out_specs=blk,
        scratch_shapes=[tile, ring, ring, tail, tail, tile, tile, tile,
                        pltpu.VMEM((3, ATT_BLOCK, HEAD_DIM), F32), tile],
        compiler_params=_params(("parallel", "parallel", "arbitrary")),
        name="attn",
    )(qb, kb, vb, norm_w.reshape(1, HEAD_DIM))


def _layer_norm(y, g, b):
    mu = jnp.mean(y, axis=-1, keepdims=True)
    yc = y - mu
    var = jnp.mean(yc * yc, axis=-1, keepdims=True)
    return yc * lax.rsqrt(var + LN_EPS) * g + b


def _outproj_kernel(oa_ref, ob_ref, x_ref, mod_ref, wo_ref, g_ref, b_ref,
                    wrh_ref, wrl_ref, br_ref,
                    x1_ref, h2_ref, ri_ref, rg_ref, cnt_ref, run_s, mix_s, hi_s, lo_s):
    @pl.when(pl.program_id(0) == 0)
    def _():
        run_s[...] = jnp.zeros(run_s.shape, F32)

    tm = x_ref.shape[0]
    gate1 = mod_ref[0, 2:3, :]
    shift2 = mod_ref[0, 3:4, :]
    scale2 = mod_ref[0, 4:5, :]
    n_parts = OUTPROJ_PARTS
    part = tm // n_parts
    grp_rows = 2 * SUBLANES

    def project(p):
        rows = slice(p * part, (p + 1) * part)
        mix_s[rows, :] = jnp.dot(jnp.concatenate([oa_ref[rows, :], ob_ref[rows, :]], axis=1), wo_ref[...],
                                 preferred_element_type=F32)

    def normalise(p):
        for r0 in range(p * part, (p + 1) * part, grp_rows):
            rows = slice(r0, r0 + grp_rows)
            x1 = _layer_norm(DEEPNORM_ALPHA * x_ref[rows, :] + (1.0 + gate1) * mix_s[rows, :],
                             g_ref[...], b_ref[...])
            x1_ref[rows, :] = x1
            h2 = x1 * (1.0 + scale2) + shift2
            h2_ref[rows] = h2.reshape(grp_rows, SUBLANES, LANES)
            hi = _bf(h2)
            hi_s[rows, :] = hi
            lo_s[rows, :] = _bf(h2 - hi.astype(F32))

    def route(p):
        rows = slice(p * part, (p + 1) * part)
        hi = hi_s[rows, :]
        return (jnp.dot(hi, wrh_ref[...], preferred_element_type=F32)
                + jnp.dot(hi, wrl_ref[...], preferred_element_type=F32)
                + jnp.dot(lo_s[rows, :], wrh_ref[...], preferred_element_type=F32))

    project(0)
    parts = []
    for p in range(n_parts):
        if p + 1 < n_parts:
            project(p + 1)
        normalise(p)
        parts.append(route(p))
    logits = jnp.concatenate(parts, axis=0) + br_ref[...]
    lane_i = lax.broadcasted_iota(jnp.int32, (tm, LANES), 1)
    lane = lane_i.astype(F32)
    lg = jnp.where(lane_i < N_GROUPS, logits, NEG)
    mg = jnp.max(lg, axis=-1, keepdims=True)
    grp = jnp.min(jnp.where(lg == mg, lane, float(LANES)), axis=-1, keepdims=True)
    gate_grp = 1.0 / jnp.sum(jnp.exp(lg - mg), axis=-1, keepdims=True)
    first_lane = N_GROUPS + EXPERTS_PER_GROUP * grp
    sel = (lane >= first_lane) & (lane < first_lane + EXPERTS_PER_GROUP)
    le = jnp.where(sel, logits, NEG)
    v1 = jnp.max(le, axis=-1, keepdims=True)
    i1 = jnp.min(jnp.where(le == v1, lane, float(LANES)), axis=-1, keepdims=True)
    le2 = jnp.where(lane == i1, NEG, le)
    v2 = jnp.max(le2, axis=-1, keepdims=True)
    i2 = jnp.min(jnp.where(le2 == v2, lane, float(LANES)), axis=-1, keepdims=True)
    e21 = jnp.exp(v2 - v1)
    g1 = gate_grp / (1.0 + e21)
    g2 = gate_grp * e21 / (1.0 + e21)

    oh1 = lane == i1
    oh2 = lane == i2
    onehot = jnp.where(oh1 | oh2, 1.0, 0.0).astype(F32)
    ti = lax.broadcasted_iota(jnp.int32, (tm, tm), 0)
    tj = lax.broadcasted_iota(jnp.int32, (tm, tm), 1)
    before = jnp.where(tj < ti, 1.0, 0.0).astype(F32)
    tot = _mm(before, onehot) + run_s[...]
    r1 = jnp.sum(jnp.where(oh1, tot, 0.0), axis=-1, keepdims=True)
    r2 = jnp.sum(jnp.where(oh2, tot, 0.0), axis=-1, keepdims=True)
    run_s[...] = run_s[...] + jnp.sum(onehot, axis=0, keepdims=True)
    cnt_ref[...] = run_s[...]

    cols = jnp.where(lane_i == 0, i1 - N_GROUPS, 0.0)
    cols = jnp.where(lane_i == 1, i2 - N_GROUPS, cols)
    r1_hi = jnp.floor(r1 * (1.0 / 256.0))
    r2_hi = jnp.floor(r2 * (1.0 / 256.0))
    cols = jnp.where(lane_i == 2, r1_hi, cols)
    cols = jnp.where(lane_i == 3, r1 - 256.0 * r1_hi, cols)
    cols = jnp.where(lane_i == 4, r2_hi, cols)
    cols = jnp.where(lane_i == 5, r2 - 256.0 * r2_hi, cols)
    pick = jnp.where(lax.broadcasted_iota(jnp.int32, (SUBLANES, LANES), 0)
                     == lax.broadcasted_iota(jnp.int32, (SUBLANES, LANES), 1), 1.0, 0.0).astype(F32)
    ri_ref[...] = _mm_nt(pick, cols).astype(jnp.int32)
    rg_ref[...] = jnp.where(lane_i == 0, g1, jnp.where(lane_i == 1, g2, 0.0))


def _outproj(oa, ob, xf, mod, w_o, ln_g, ln_b, w_rg, b_rg, w_re, b_re, seq):
    t, d = xf.shape
    tm = PROJ_TILE
    gw = oa.shape[1]
    wo = _bf(w_o)
    wr = jnp.zeros((d, LANES), F32).at[:, :N_GROUPS].set(w_rg).at[:, N_GROUPS:N_GROUPS + N_EXPERTS].set(w_re)
    wrh = _bf(wr)
    wrl = _bf(wr - wrh.astype(F32))
    br = jnp.zeros((1, LANES), F32).at[0, :N_GROUPS].set(b_rg).at[0, N_GROUPS:N_GROUPS + N_EXPERTS].set(b_re)
    spb = seq // tm
    row = lambda i: (i, 0)
    const = lambda i: (0, 0)
    return pl.pallas_call(
        _outproj_kernel,
        out_shape=(jax.ShapeDtypeStruct((t, d), F32), jax.ShapeDtypeStruct((t, d // LANES, LANES), F32),
                   jax.ShapeDtypeStruct((SUBLANES, t), jnp.int32), jax.ShapeDtypeStruct((t, LANES), F32),
                   jax.ShapeDtypeStruct((1, LANES), F32)),
        grid=(t // tm,),
        in_specs=[pl.BlockSpec((tm, gw), row), pl.BlockSpec((tm, gw), row), pl.BlockSpec((tm, d), row),
                  pl.BlockSpec((1, 6, d), lambda i: (i // spb, 0, 0)),
                  pl.BlockSpec((2 * gw, d), const),
                  pl.BlockSpec((1, d), const), pl.BlockSpec((1, d), const),
                  pl.BlockSpec((d, LANES), const), pl.BlockSpec((d, LANES), const),
                  pl.BlockSpec((1, LANES), const)],
        out_specs=(pl.BlockSpec((tm, d), row), pl.BlockSpec((tm, d // LANES, LANES), lambda i: (i, 0, 0)),
                   pl.BlockSpec((SUBLANES, tm), lambda i: (0, i)), pl.BlockSpec((tm, LANES), row),
                   pl.BlockSpec((1, LANES), const)),
        scratch_shapes=[pltpu.VMEM((1, LANES), F32), pltpu.VMEM((tm, d), F32),
                        pltpu.VMEM((tm, d), BF16), pltpu.VMEM((tm, d), BF16)],
        compiler_params=_params(("arbitrary",)),
        name="outproj",
    )(oa, ob, xf, mod, wo, ln_g.reshape(1, d), ln_b.reshape(1, d), wrh, wrl, br)


def _dispatch_kernel(d1_ref, d2_ref, h_ref, xs_ref, sem):
    tm = h_ref.shape[0]
    i = pl.program_id(0)

    def row_copy(tk, dest):
        return pltpu.make_async_copy(h_ref.at[tk], xs_ref.at[dest], sem)

    def issue(g, carry):
        for u in range(DMA_UNROLL):
            tk = g * DMA_UNROLL + u
            tok = i * tm + tk
            row_copy(tk, d1_ref[tok]).start(priority=0)
            row_copy(tk, d2_ref[tok]).start(priority=1)
        return carry

    lax.fori_loop(0, tm // DMA_UNROLL, issue, 0)

    tile_copy = pltpu.make_async_copy(h_ref, xs_ref.at[pl.ds(0, tm)], sem)
    tile_copy.wait()
    tile_copy.wait()


def _dispatch(h2, d1, d2):
    t, sub, lanes = h2.shape
    tm = DISPATCH_TILE
    return pl.pallas_call(
        _dispatch_kernel,
        out_shape=jax.ShapeDtypeStruct((2 * t, sub, lanes), F32),
        grid_spec=pltpu.PrefetchScalarGridSpec(
            num_scalar_prefetch=2,
            grid=(t // tm,),
            in_specs=[pl.BlockSpec((tm, sub, lanes), lambda i, *_: (i, 0, 0))],
            out_specs=pl.BlockSpec(memory_space=pl.ANY),
            scratch_shapes=[pltpu.SemaphoreType.DMA]),
        compiler_params=_params(("arbitrary",)),
        name="dispatch",
    )(d1, d2, h2)


def _experts_kernel(wb_ref, we_ref, lo_ref, hi_ref, nw_ref, first_ref, ring_ref, next_ref,
                    xs_ref, wg_hbm, wu_hbm, wd_hbm, ys_ref,
                    wg_f, wu_f, wd_f, wg_s, wu_s, wd_s, sem):
    w = pl.program_id(0)

    def fetch(expert, slot):
        return (pltpu.make_async_copy(wg_hbm.at[expert], wg_f.at[slot], sem.at[slot]),
                pltpu.make_async_copy(wu_hbm.at[expert], wu_f.at[slot], sem.at[slot]),
                pltpu.make_async_copy(wd_hbm.at[expert], wd_f.at[slot], sem.at[slot]))

    @pl.when(w == 0)
    def _():
        for cp in fetch(we_ref[0], 0):
            cp.start()

    @pl.when((first_ref[w] == 1) & (w < nw_ref[0]))
    def _():
        slot = ring_ref[w]
        for cp in fetch(we_ref[w], slot):
            cp.wait()

        @pl.when(next_ref[w] >= 0)
        def _():
            for cp in fetch(next_ref[w], 1 - slot):
                cp.start()

        wg_s[...] = _bf(wg_f[slot])
        wu_s[...] = _bf(wu_f[slot])
        wd_s[...] = _bf(wd_f[slot])

    @pl.when(w < nw_ref[0])
    def _():
        bm, sub, lanes = xs_ref.shape
        x = _bf(xs_ref[...].reshape(bm, sub * lanes))
        hid = (_silu(jnp.dot(x, wg_s[...], preferred_element_type=F32))
               * jnp.dot(x, wu_s[...], preferred_element_type=F32))
        y = jnp.dot(_bf(hid), wd_s[...], preferred_element_type=F32).reshape(bm, sub, lanes)
        row = lax.broadcasted_iota(jnp.int32, (bm, 1, 1), 0)
        mine = (row >= lo_ref[w]) & (row < hi_ref[w])

        @pl.when(lo_ref[w] == 0)
        def _():
            ys_ref[...] = jnp.where(mine, y, 0.0)

        @pl.when(lo_ref[w] > 0)
        def _():
            ys_ref[...] = jnp.where(mine, y, ys_ref[...])


def _experts(xs, item_block, item_expert, item_lo, item_hi, n_items, w_gate, w_up, w_down):
    n_slots, sub, lanes = xs.shape
    d = sub * lanes
    ff = w_gate.shape[2]
    bm = EXPERT_BLOCK
    n = item_block.shape[0]
    idx = jnp.arange(n, dtype=jnp.int32)
    first = jnp.concatenate([jnp.ones((1,), jnp.int32),
                             (item_expert[1:] != item_expert[:-1]).astype(jnp.int32)])
    ring = (jnp.cumsum(first) - 1) % 2
    next_first = lax.cummin(jnp.where(first == 1, idx, n), reverse=True)
    next_first = jnp.concatenate([next_first[1:], jnp.full((1,), n, jnp.int32)])
    nxt = jnp.where(next_first < n, item_expert[jnp.minimum(next_first, n - 1)], -1).astype(jnp.int32)
    slot = lambda w, *_: (_[0][w], 0, 0)
    return pl.pallas_call(
        _experts_kernel,
        out_shape=jax.ShapeDtypeStruct((n_slots, sub, lanes), F32),
        grid_spec=pltpu.PrefetchScalarGridSpec(
            num_scalar_prefetch=8,
            grid=(n,),
            in_specs=[pl.BlockSpec((bm, sub, lanes), slot),
                      pl.BlockSpec(memory_space=pl.ANY), pl.BlockSpec(memory_space=pl.ANY),
                      pl.BlockSpec(memory_space=pl.ANY)],
            out_specs=pl.BlockSpec((bm, sub, lanes), slot),
            scratch_shapes=[pltpu.VMEM((2, d, ff), F32), pltpu.VMEM((2, d, ff), F32),
                            pltpu.VMEM((2, ff, d), F32),
                            pltpu.VMEM((d, ff), BF16), pltpu.VMEM((d, ff), BF16),
                            pltpu.VMEM((ff, d), BF16), pltpu.SemaphoreType.DMA((2,))]),
        compiler_params=_params(("arbitrary",)),
        name="experts",
    )(item_block, item_expert, item_lo, item_hi, n_items, first, ring.astype(jnp.int32), nxt,
      xs, w_gate, w_up, w_down)


def _combine_kernel(d1_ref, d2_ref, ys_ref, rg_ref, x1_ref, mod_ref, g_ref, b_ref, o_ref, ya, yb, sem):
    tm = x1_ref.shape[0]
    i = pl.program_id(0)
    n = pl.num_programs(0)

    def row_copy(dest, buf, slot, tk):
        return pltpu.make_async_copy(ys_ref.at[dest], buf.at[slot, tk], sem.at[slot])

    slot = i % 2
    gate2 = mod_ref[0, 5:6, :]

    def issue_group(step, dst_slot, g):
        for u in range(DMA_UNROLL):
            tk = g * DMA_UNROLL + u
            tok = step * tm + tk
            row_copy(d1_ref[tok], ya, dst_slot, tk).start(priority=0)
            row_copy(d2_ref[tok], yb, dst_slot, tk).start(priority=1)

    def gather_tile(step, dst_slot):
        def issue(g, carry):
            issue_group(step, dst_slot, g)
            return carry

        lax.fori_loop(0, tm // DMA_UNROLL, issue, 0)

    @pl.when(i == 0)
    def _():
        gather_tile(0, 0)

    @pl.when(i + 1 < n)
    def _():
        gather_tile(i + 1, 1 - slot)

    pltpu.make_async_copy(ys_ref.at[pl.ds(0, tm)], ya.at[slot], sem.at[slot]).wait()
    pltpu.make_async_copy(ys_ref.at[pl.ds(0, tm)], yb.at[slot], sem.at[slot]).wait()

    rg = rg_ref[...]
    d = x1_ref.shape[1]
    y = rg[:, 0:1] * ya[slot].reshape(tm, d) + rg[:, 1:2] * yb[slot].reshape(tm, d)
    o_ref[...] = _layer_norm(DEEPNORM_ALPHA * x1_ref[...] + (1.0 + gate2) * y, g_ref[...], b_ref[...])


def _combine(ys, d1, d2, rg, x1, mod, ln_g, ln_b, seq):
    t, d = x1.shape
    tm = ROW_TILE
    spb = seq // tm
    row = lambda i, *_: (i, 0)
    const = lambda i, *_: (0, 0)
    buf = pltpu.VMEM((2, tm) + ys.shape[1:], F32)
    return pl.pallas_call(
        _combine_kernel,
        out_shape=jax.ShapeDtypeStruct((t, d), F32),
        grid_spec=pltpu.PrefetchScalarGridSpec(
            num_scalar_prefetch=2,
            grid=(t // tm,),
            in_specs=[pl.BlockSpec(memory_space=pl.ANY),
                      pl.BlockSpec((tm, LANES), row), pl.BlockSpec((tm, d), row),
                      pl.BlockSpec((1, 6, d), lambda i, *_: (i // spb, 0, 0)),
                      pl.BlockSpec((1, d), const), pl.BlockSpec((1, d), const)],
            out_specs=pl.BlockSpec((tm, d), row),
            scratch_shapes=[buf, buf, pltpu.SemaphoreType.DMA((2,))]),
        compiler_params=_params(("arbitrary",)),
        name="combine",
    )(d1, d2, ys, rg, x1, mod, ln_g.reshape(1, d), ln_b.reshape(1, d))


def _layer(x, c, positions, w_ada, b_ada, w_in, conv_w, a_log, dt_bias, gdn_norm_w, attn_norm_w,
           w_o, ln1_g, ln1_b, w_rg, b_rg, w_re, b_re, w_gate, w_up, w_down, ln2_g, ln2_b):
    bsz, seq, d = x.shape
    t = bsz * seq
    xf = x.reshape(t, d)
    mod = _ada(c, w_ada, b_ada)
    qkv, z, bdc, bdr, qb, kb, vb = _inproj(xf, mod, positions.reshape(t, 1), w_in, conv_w, seq)
    gw = GDN_HEADS * HEAD_DIM
    oa = _gdn(qkv.reshape(bsz, seq, 3 * gw), z.reshape(bsz, seq, gw), bdc.reshape(bsz, seq, LANES), bdr,
              a_log, dt_bias, gdn_norm_w).reshape(t, gw)
    ob = _attn(qb, kb, vb, attn_norm_w, bsz, seq)
    x1, h2, ri, rg, cnt = _outproj(oa, ob, xf, mod, w_o, ln1_g, ln1_b, w_rg, b_rg, w_re, b_re, seq)

    bm = EXPERT_BLOCK
    counts = cnt[0, N_GROUPS:N_GROUPS + N_EXPERTS].astype(jnp.int32)
    seg_end = jnp.cumsum(counts)
    seg_start = seg_end - counts
    first_blk = seg_start // bm
    n_per = jnp.where(counts > 0, (seg_end - 1) // bm - first_blk + 1, 0)
    item_end = jnp.cumsum(n_per)
    n_items = item_end[-1:]
    max_items = (2 * t) // bm + N_EXPERTS - 1
    w_idx = jnp.minimum(jnp.arange(max_items, dtype=jnp.int32), n_items[0] - 1)
    item_expert = jnp.minimum(jnp.sum(item_end[None, :] <= w_idx[:, None], axis=1), N_EXPERTS - 1).astype(jnp.int32)
    item_block = first_blk[item_expert] + w_idx - (item_end - n_per)[item_expert]
    item_lo = jnp.maximum(seg_start[item_expert] - item_block * bm, 0)
    item_hi = jnp.minimum(seg_end[item_expert] - item_block * bm, bm)
    expert_ids = jnp.arange(N_EXPERTS, dtype=jnp.int32)[:, None]

    def seg_of(e):
        return jnp.sum(jnp.where(e[None, :] == expert_ids, seg_start[:, None], 0), axis=0)

    d1 = seg_of(ri[0]) + ri[2] * 256 + ri[3]
    d2 = seg_of(ri[1]) + ri[4] * 256 + ri[5]

    xs = _dispatch(h2, d1, d2)
    ys = _experts(xs, item_block, item_expert, item_lo, item_hi, n_items, w_gate, w_up, w_down)
    out = _combine(ys, d1, d2, rg, x1, mod, ln2_g, ln2_b, seq)
    return out.reshape(bsz, seq, d)


def kernel(x, c, positions, w_ada, b_ada, w_in, conv_w, a_log, dt_bias, gdn_norm_w, attn_norm_w, w_o, ln1_g, ln1_b, w_router_group, b_router_group, w_router_expert, b_router_expert, w_gate, w_up, w_down, ln2_g, ln2_b):
    assert w_ada.shape[0] == DEPTH
    return _layer(x, c, positions, w_ada[0], b_ada[0], w_in[0], conv_w[0], a_log[0], dt_bias[0],
                  gdn_norm_w[0], attn_norm_w[0], w_o[0], ln1_g[0], ln1_b[0],
                  w_router_group[0], b_router_group[0], w_router_expert[0], b_router_expert[0],
                  w_gate[0], w_up[0], w_down[0], ln2_g[0], ln2_b[0])
```

```python
import functools
import math

import jax
import jax.numpy as jnp
from jax import lax
from jax.experimental import pallas as pl
from jax.experimental.pallas import tpu as pltpu

F32 = jnp.float32
BF16 = jnp.bfloat16
LOG2E = math.log2(math.e)

GDN_HEADS = 4
ATT_HEADS = 4
HEAD_DIM = 128
CONV_WIDTH = 4
DILATED_PATTERNS = ((128, 1), (512, 4), (2048, 16))
ROPE_THETA = 500000.0
ROPE_DIMS = HEAD_DIM // 4
N_GROUPS = 4
EXPERTS_PER_GROUP = 8
N_EXPERTS = N_GROUPS * EXPERTS_PER_GROUP
DEPTH = 1
DEEPNORM_ALPHA = (2.0 * DEPTH) ** 0.25
LN_EPS = 1e-5
RMS_EPS = 1e-6

LANES = 128
SUBLANES = 8
VMEM_LIMIT = 48 * 1024 * 1024

GDN_BLOCK = 128
GDN_TILE = 256
ATT_BLOCK = 128
ATT_TILE = 2048
PROJ_TILE = 512
ROW_TILE = 256
DISPATCH_TILE = 1024
EXPERT_BLOCK = 256
OUTPROJ_PARTS = 2
DMA_UNROLL = 8
NEG = -1e30


def _bf(x):
    return x.astype(BF16)


def _mm(a, b):
    return jnp.dot(_bf(a), _bf(b), preferred_element_type=F32)


def _mm_nt(a, b):
    return lax.dot_general(_bf(a), _bf(b), (((1,), (1,)), ((), ())), preferred_element_type=F32)


def _mm_tn(a, b):
    return lax.dot_general(_bf(a), _bf(b), (((0,), (0,)), ((), ())), preferred_element_type=F32)


def _sigmoid(x):
    return 1.0 / (1.0 + jnp.exp(-x))


def _silu(x):
    return x * _sigmoid(x)


def _softplus(x):
    return jnp.maximum(x, 0.0) + jnp.log(1.0 + jnp.exp(-jnp.abs(x)))


def _params(sem):
    return pltpu.CompilerParams(dimension_semantics=sem, vmem_limit_bytes=VMEM_LIMIT)


def _ada_kernel(ct_ref, w_ref, b_ref, o_ref, *, bsz):
    sc = _silu(ct_ref[...])
    w = w_ref[...]
    rows = [jnp.sum(w * sc[:, b:b + 1], axis=0, keepdims=True) for b in range(bsz)]
    rows.append(jnp.zeros((o_ref.shape[0] - bsz, w.shape[1]), F32))
    o_ref[...] = jnp.concatenate(rows, axis=0) + b_ref[...]


def _ada(c, w_ada, b_ada):
    bsz, d = c.shape
    n = w_ada.shape[1]
    tn = 512
    assert bsz <= SUBLANES
    ct = jnp.zeros((d, LANES), F32).at[:, :bsz].set(c.T)
    out = pl.pallas_call(
        functools.partial(_ada_kernel, bsz=bsz),
        out_shape=jax.ShapeDtypeStruct((SUBLANES, n), F32),
        grid=(n // tn,),
        in_specs=[pl.BlockSpec((d, LANES), lambda j: (0, 0)),
                  pl.BlockSpec((d, tn), lambda j: (0, j)),
                  pl.BlockSpec((1, tn), lambda j: (0, j))],
        out_specs=pl.BlockSpec((SUBLANES, tn), lambda j: (0, j)),
        compiler_params=_params(("parallel",)),
        name="ada",
    )(ct, w_ada, b_ada.reshape(1, n))
    return out[:bsz].reshape(bsz, 6, d)


def _inproj_kernel(x_ref, mod_ref, pos_ref, invf_ref, convw_ref, wqkv_ref, wz_ref, wbd_ref, wbdt_ref,
                   wq_ref, wk_ref, wv_ref,
                   qkv_ref, z_ref, bdc_ref, bdr_ref, qb_ref, kb_ref, vb_ref, cbuf, *, steps_per_seq):
    tm = x_ref.shape[0]
    halo = SUBLANES
    n_slabs = cbuf.shape[0]

    @pl.when(pl.program_id(0) % steps_per_seq == 0)
    def _():
        cbuf[:, 0:halo, :] = jnp.zeros((n_slabs, halo, LANES), F32)

    half = ROPE_DIMS // 2
    groups = LANES // half
    lane = lax.broadcasted_iota(jnp.int32, (1, LANES), 1)
    first = lane < half
    rot = lane < ROPE_DIMS
    ang = pos_ref[0].astype(F32) * invf_ref[...]
    cos_c = jnp.cos(ang)
    sin_c = jnp.sin(ang)
    cos_parts, sin_parts = [], []
    for j in range(groups):
        lo_sh = (LANES - half * j) % LANES
        hi_sh = (LANES - half * j + half) % LANES
        c_lo = pltpu.roll(cos_c, lo_sh, 1) if lo_sh else cos_c
        c_hi = pltpu.roll(cos_c, hi_sh, 1) if hi_sh else cos_c
        s_lo = pltpu.roll(sin_c, lo_sh, 1) if lo_sh else sin_c
        s_hi = pltpu.roll(sin_c, hi_sh, 1) if hi_sh else sin_c
        cos_parts.append(jnp.where(first, c_lo, jnp.where(rot, c_hi, 1.0)))
        sin_parts.append(jnp.where(first, -s_lo, jnp.where(rot, s_hi, 0.0)))
    cosv = jnp.concatenate(cos_parts, axis=0)
    sin_signed = jnp.concatenate(sin_parts, axis=0)

    def rope(y):
        outs = []
        for hh in range(ATT_HEADS):
            yh = y[:, hh * HEAD_DIM:(hh + 1) * HEAD_DIM]
            partner = jnp.where(first, pltpu.roll(yh, LANES - half, 1), pltpu.roll(yh, half, 1))
            outs.append(yh * cosv + partner * sin_signed)
        return jnp.concatenate(outs, axis=1)

    def conv_slice(s):
        sl = slice(s * HEAD_DIM, (s + 1) * HEAD_DIM)
        off = halo - (CONV_WIDTH - 1)
        acc = convw_ref[0:1, sl] * cbuf[s, off:off + tm, :]
        for j in range(1, CONV_WIDTH):
            acc = acc + convw_ref[j:j + 1, sl] * cbuf[s, off + j:off + j + tm, :]
        cbuf[s, 0:halo, :] = cbuf[s, tm:tm + halo, :]
        y = _silu(acc)
        if s < 2 * GDN_HEADS:
            y = y * lax.rsqrt(jnp.sum(y * y, axis=-1, keepdims=True) + RMS_EPS)
        if s < GDN_HEADS:
            y = y * (HEAD_DIM ** -0.5)
        qkv_ref[:, sl] = y

    shift = mod_ref[0, 0:1, :]
    scale = mod_ref[0, 1:2, :]
    h = _bf(x_ref[...] * (1.0 + scale) + shift)
    chunk = 2 * HEAD_DIM
    n_chunks = n_slabs // 2

    def project_chunk(c):
        pre = jnp.dot(h, wqkv_ref[:, c * chunk:(c + 1) * chunk], preferred_element_type=F32)
        cbuf[2 * c, halo:halo + tm, :] = pre[:, :HEAD_DIM]
        cbuf[2 * c + 1, halo:halo + tm, :] = pre[:, HEAD_DIM:]

    def conv_chunk(c):
        conv_slice(2 * c)
        conv_slice(2 * c + 1)

    project_chunk(0)
    for c in range(1, n_chunks):
        project_chunk(c)
        conv_chunk(c - 1)
    qb = jnp.dot(h, wq_ref[...], preferred_element_type=F32)
    conv_chunk(n_chunks - 1)
    kb = jnp.dot(h, wk_ref[...], preferred_element_type=F32)
    qb_ref[...] = rope(qb) * (HEAD_DIM ** -0.5 * LOG2E)
    vb_ref[...] = jnp.dot(h, wv_ref[...], preferred_element_type=F32)
    kb_ref[...] = rope(kb)
    z_ref[...] = jnp.dot(h, wz_ref[...], preferred_element_type=F32)
    bdc_ref[...] = jnp.dot(h, wbd_ref[...], preferred_element_type=F32)
    bdr_ref[0] = lax.dot_general(wbdt_ref[...], h, (((1,), (1,)), ((), ())),
                                 preferred_element_type=F32)


def _inproj(xf, mod, pos, w_in, conv_w, seq):
    t, d = xf.shape
    tm = PROJ_TILE
    gw = GDN_HEADS * HEAD_DIM
    aw = ATT_HEADS * HEAD_DIM
    o0 = 3 * gw
    o1 = o0 + gw
    o2 = o1 + 2 * GDN_HEADS
    wb = _bf(w_in)
    wqkv, wz = wb[:, :o0], wb[:, o0:o1]
    wbd_n = wb[:, o1:o2]
    wbd = jnp.zeros((d, LANES), BF16).at[:, :2 * GDN_HEADS].set(wbd_n)
    wbdt = wbd_n.T
    wq, wk, wv = wb[:, o2:o2 + aw], wb[:, o2 + aw:o2 + 2 * aw], wb[:, o2 + 2 * aw:o2 + 3 * aw]
    half = ROPE_DIMS // 2
    groups = LANES // half
    inv_freq = ROPE_THETA ** (-jnp.arange(half, dtype=F32) * 2.0 / ROPE_DIMS)
    invf = jnp.tile(inv_freq, groups).reshape(1, LANES)
    pos = jnp.repeat(pos.reshape(t // tm, groups, tm // groups).transpose(0, 2, 1), half, axis=2)
    spb = seq // tm
    row = lambda i: (i, 0)
    const = lambda i: (0, 0)
    return pl.pallas_call(
        functools.partial(_inproj_kernel, steps_per_seq=spb),
        out_shape=(jax.ShapeDtypeStruct((t, o0), F32), jax.ShapeDtypeStruct((t, gw), F32),
                   jax.ShapeDtypeStruct((t, LANES), F32), jax.ShapeDtypeStruct((t // seq, SUBLANES, seq), F32),
                   jax.ShapeDtypeStruct((t, aw), F32), jax.ShapeDtypeStruct((t, aw), F32),
                   jax.ShapeDtypeStruct((t, aw), F32)),
        grid=(t // tm,),
        in_specs=[pl.BlockSpec((tm, d), row),
                  pl.BlockSpec((1, 6, d), lambda i: (i // spb, 0, 0)),
                  pl.BlockSpec((1, tm // groups, LANES), lambda i: (i, 0, 0)),
                  pl.BlockSpec((1, LANES), const), pl.BlockSpec((CONV_WIDTH, o0), const),
                  pl.BlockSpec((d, o0), const), pl.BlockSpec((d, gw), const),
                  pl.BlockSpec((d, LANES), const), pl.BlockSpec((2 * GDN_HEADS, d), const),
                  pl.BlockSpec((d, aw), const), pl.BlockSpec((d, aw), const),
                  pl.BlockSpec((d, aw), const)],
        out_specs=(pl.BlockSpec((tm, o0), row), pl.BlockSpec((tm, gw), row),
                   pl.BlockSpec((tm, LANES), row),
                   pl.BlockSpec((1, SUBLANES, tm), lambda i: (i // spb, 0, i % spb)),
                   pl.BlockSpec((tm, aw), row), pl.BlockSpec((tm, aw), row),
                   pl.BlockSpec((tm, aw), row)),
        scratch_shapes=[pltpu.VMEM((o0 // HEAD_DIM, tm + 2 * SUBLANES, HEAD_DIM), F32)],
        compiler_params=_params(("arbitrary",)),
        name="inproj",
    )(xf, mod, pos, invf, conv_w, wqkv, wz, wbd, wbdt, wq, wk, wv)


def _gdn_kernel(qkv_ref, z_ref, bdc_ref, bdr_ref, gpc_ref, gpr_ref, nw_ref, o_ref,
                u_s, w_s, qd_s, kd_s, qk_s, a_s, x_s, y_s, st_ref):
    nbat, ts = qkv_ref.shape[0], qkv_ref.shape[1]
    nb = ts // GDN_BLOCK
    gw = GDN_HEADS * HEAD_DIM

    @pl.when(pl.program_id(0) == 0)
    def _():
        st_ref[...] = jnp.zeros(st_ref.shape, F32)

    blk_n = GDN_BLOCK
    hd = HEAD_DIM
    ti = lax.broadcasted_iota(jnp.int32, (ts, ts), 0)
    tj = lax.broadcasted_iota(jnp.int32, (ts, ts), 1)
    same = (ti // blk_n) == (tj // blk_n)
    m_low = jnp.where(same & (tj <= ti), 1.0, 0.0).astype(BF16)
    m_up = jnp.where(same & (ti <= tj), 1.0, 0.0).astype(BF16)

    def split3(x):
        x1 = _bf(x)
        r1 = x - x1.astype(F32)
        x2 = _bf(r1)
        return x1, x2, _bf(r1 - x2.astype(F32))

    beta_c = [_sigmoid(bdc_ref[bb]) for bb in range(nbat)]
    g_c = jnp.concatenate([-jnp.exp(gpc_ref[0:1, :]) * _softplus(bdc_ref[bb] + gpc_ref[1:2, :])
                           for bb in range(nbat)], axis=1)
    g_r = jnp.concatenate([-jnp.exp(gpr_ref[:, 0:1]) * _softplus(bdr_ref[bb] + gpr_ref[:, 1:2])
                           for bb in range(nbat)], axis=0)
    gc_c = sum(jnp.dot(m_low, part, preferred_element_type=F32) for part in split3(g_c))
    gc_r = sum(jnp.dot(part, m_up, preferred_element_type=F32) for part in split3(g_r))

    def cat2(m):
        return jnp.concatenate([m, m], axis=1)

    ii = lax.broadcasted_iota(jnp.int32, (blk_n, blk_n), 0)
    jj = lax.broadcasted_iota(jnp.int32, (blk_n, blk_n), 1)
    lower = cat2(jj <= ii)
    strict = cat2(jj < ii)
    eye = cat2(jnp.where(ii == jj, 1.0, 0.0).astype(F32))
    levels = []
    b = 1
    while b < blk_n:
        levels.append(cat2(((ii // b) == (jj // b) + 1) & (((jj // b) % 2) == 0)))
        b *= 2

    def block_diag(rp):
        n, m = rp.shape[0], rp.shape[1] // 2
        z = jnp.zeros((n, m), rp.dtype)
        return jnp.concatenate([jnp.concatenate([rp[:, :m], z], axis=1),
                                jnp.concatenate([z, rp[:, m:]], axis=1)], axis=0)

    def mm2(lp, rp):
        return jnp.dot(_bf(lp), block_diag(_bf(rp)), preferred_element_type=F32)

    def mm2_nt(lp, rp):
        return lax.dot_general(_bf(lp), block_diag(_bf(rp)), (((1,), (1,)), ((), ())),
                               preferred_element_type=F32)

    chains = [(bb, j, pp) for bb in range(nbat) for j in range(nb) for pp in range(GDN_HEADS // 2)]

    def tile_of(bb, j, pp):
        return bb, slice(j * blk_n, (j + 1) * blk_n), slice(2 * pp * hd, 2 * (pp + 1) * hd)

    def col_pair(arr, rows, col):
        return jnp.concatenate([jnp.broadcast_to(arr[rows, col:col + 1], (blk_n, hd)),
                                jnp.broadcast_to(arr[rows, col + 1:col + 2], (blk_n, hd))], axis=1)

    for c, (bb, j, pp) in enumerate(chains):
        blk = tile_of(bb, j, pp)
        rows, cols = blk[1], blk[2]
        last = slice((j + 1) * blk_n - 1, (j + 1) * blk_n)
        gcol = bb * LANES + GDN_HEADS + 2 * pp
        grow = bb * SUBLANES + GDN_HEADS + 2 * pp
        q = qkv_ref[bb, rows, cols]
        k = qkv_ref[bb, rows, slice(gw + cols.start, gw + cols.stop)]
        v = qkv_ref[bb, rows, slice(2 * gw + cols.start, 2 * gw + cols.stop)]
        beta = col_pair(beta_c[bb], rows, 2 * pp)
        gcc = col_pair(gc_c, rows, gcol)
        gtc = col_pair(gc_c, last, gcol)
        gcr = jnp.concatenate([jnp.broadcast_to(gc_r[grow:grow + 1, rows], (blk_n, hd)),
                               jnp.broadcast_to(gc_r[grow + 1:grow + 2, rows], (blk_n, hd))], axis=1)
        kb = k * beta
        eg = jnp.exp(gcc)
        dm = jnp.where(lower, jnp.exp(gcc - gcr), 0.0)
        a = jnp.where(strict, mm2_nt(kb, k) * dm, 0.0)
        a_s[c] = a
        x_s[c] = eye - jnp.where(levels[0], a, 0.0)
        u_s[blk] = v * beta
        w_s[blk] = kb * eg
        qd_s[blk] = q * eg
        kd_s[blk] = k * jnp.exp(gtc - gcc)
        qk_s[blk] = mm2_nt(q, k) * dm

    for lm in levels[1:]:
        for c in range(len(chains)):
            y_s[c] = mm2(x_s[c], jnp.where(lm, a_s[c], 0.0))
        for c in range(len(chains)):
            xc = x_s[c]
            x_s[c] = xc - mm2(y_s[c], xc)

    for c, (bb, j, pp) in enumerate(chains):
        blk = tile_of(bb, j, pp)
        u, w = _bf(u_s[blk]), _bf(w_s[blk])
        z = jnp.zeros((blk_n, 2 * hd), BF16)
        rhs = jnp.concatenate([jnp.concatenate([u[:, :hd], w[:, :hd], z], axis=1),
                               jnp.concatenate([z, u[:, hd:], w[:, hd:]], axis=1)], axis=0)
        sol = jnp.dot(_bf(x_s[c]), rhs, preferred_element_type=F32)
        u_s[blk] = jnp.concatenate([sol[:, 0:hd], sol[:, 2 * hd:3 * hd]], axis=1)
        w_s[blk] = jnp.concatenate([sol[:, hd:2 * hd], sol[:, 3 * hd:]], axis=1)

    for j in range(nb):
        for bb in range(nbat):
            for pp in range(GDN_HEADS // 2):
                blk = tile_of(bb, j, pp)
                si = bb * (GDN_HEADS // 2) + pp
                last = slice((j + 1) * blk_n - 1, (j + 1) * blk_n)
                gcol = bb * LANES + GDN_HEADS + 2 * pp
                state = st_ref[si]
                proj = mm2(jnp.concatenate([w_s[blk], qd_s[blk]], axis=0), state)
                v_new = u_s[blk] - proj[:blk_n]
                o = proj[blk_n:] + mm2(qk_s[blk], v_new)
                g_last = jnp.exp(jnp.concatenate(
                    [jnp.broadcast_to(gc_c[last, gcol:gcol + 1], (1, hd)),
                     jnp.broadcast_to(gc_c[last, gcol + 1:gcol + 2], (1, hd))], axis=1))
                kd = kd_s[blk]
                upd = jnp.concatenate([_mm_tn(kd[:, :hd], v_new[:, :hd]),
                                       _mm_tn(kd[:, hd:], v_new[:, hd:])], axis=1)
                st_ref[si] = state * g_last + upd
                zz = z_ref[blk]
                halves = []
                for hf in range(2):
                    oh = o[:, hf * hd:(hf + 1) * hd]
                    halves.append(oh * lax.rsqrt(jnp.mean(oh * oh, axis=-1, keepdims=True) + RMS_EPS)
                                  * nw_ref[...])
                o_ref[blk] = (jnp.concatenate(halves, axis=1) * _silu(zz)).astype(o_ref.dtype)


def _gdn(qkv, z, bdc, bdr, a_log, dt_bias, norm_w):
    bsz, seq, _ = qkv.shape
    ts = GDN_TILE
    gw = GDN_HEADS * HEAD_DIM
    zeros4 = jnp.zeros((GDN_HEADS,), F32)
    al = jnp.concatenate([zeros4, a_log])
    db = jnp.concatenate([zeros4, dt_bias])
    gpc = jnp.zeros((2, LANES), F32).at[0, :2 * GDN_HEADS].set(al).at[1, :2 * GDN_HEADS].set(db)
    gpr = jnp.stack([al, db], axis=1)
    row = lambda i: (0, i, 0)
    const = lambda i: (0, 0)
    n_chains = bsz * (ts // GDN_BLOCK) * (GDN_HEADS // 2)
    tile = pltpu.VMEM((bsz, ts, gw), F32)
    mats = pltpu.VMEM((n_chains, GDN_BLOCK, 2 * GDN_BLOCK), F32)
    return pl.pallas_call(
        _gdn_kernel,
        out_shape=jax.ShapeDtypeStruct((bsz, seq, gw), BF16),
        grid=(seq // ts,),
        in_specs=[pl.BlockSpec((bsz, ts, 3 * gw), row), pl.BlockSpec((bsz, ts, gw), row),
                  pl.BlockSpec((bsz, ts, LANES), row),
                  pl.BlockSpec((bsz, SUBLANES, ts), lambda i: (0, 0, i)),
                  pl.BlockSpec((2, LANES), const), pl.BlockSpec((2 * GDN_HEADS, 2), const),
                  pl.BlockSpec((1, HEAD_DIM), const)],
        out_specs=pl.BlockSpec((bsz, ts, gw), row),
        scratch_shapes=[tile, tile, tile, tile, tile,
                        mats, mats, mats,
                        pltpu.VMEM((bsz * GDN_HEADS // 2, HEAD_DIM, 2 * HEAD_DIM), F32)],
        compiler_params=_params(("arbitrary",)),
        name="gdn",
    )(qkv, z, bdc, bdr, gpc, gpr, norm_w.reshape(1, HEAD_DIM))


def _attn_kernel(q_ref, k_ref, v_ref, nw_ref, o_ref,
                 q4, k4, v4, ktail, vtail, m_s, l_s, acc_s, tmp_s, nat_s):
    tq = q_ref.shape[0]
    blk = ATT_BLOCK
    sub = 4
    nq = tq // sub
    t = pl.program_id(2)
    slot = t % 2
    other = 1 - slot
    qi = lax.broadcasted_iota(jnp.int32, (blk, 2 * blk), 0)
    kj = lax.broadcasted_iota(jnp.int32, (blk, 2 * blk), 1)
    band = (kj >= qi) & (kj <= qi + blk)
    first_lo = jnp.where(t > 0, 0, blk)
    band_first = band & (kj >= first_lo)

    @pl.when(t == 0)
    def _():
        k4[...] = jnp.zeros(k4.shape, F32)
        v4[...] = jnp.zeros(v4.shape, F32)
        ktail[...] = jnp.zeros(ktail.shape, F32)
        vtail[...] = jnp.zeros(vtail.shape, F32)

    for r in range(sub):
        rows = slice(r * nq, (r + 1) * nq)
        src = pl.ds(r, nq, stride=sub)
        q4[rows, :] = q_ref[src, :]
        k4[slot, rows, :] = k_ref[src, :]
        v4[slot, rows, :] = v_ref[src, :]

    def block_stats(q, kcat, vcat, from_prev_tile):
        s = _mm_nt(q, kcat)
        s = jnp.where(band_first if from_prev_tile else band, s, NEG)
        m = jnp.max(s, axis=-1, keepdims=True)
        p = jnp.exp2(s - m)
        l = jnp.sum(p, axis=-1, keepdims=True)
        return m, l, _mm(p, vcat)

    def merge(dst, m, l, o):
        m_old = m_s[dst, :]
        m_new = jnp.maximum(m_old, m)
        w_old = jnp.exp2(m_old - m_new)
        w_cur = jnp.exp2(m - m_new)
        m_s[dst, :] = m_new
        l_s[dst, :] = w_old * l_s[dst, :] + w_cur * l
        acc_s[dst, :] = w_old * acc_s[dst, :] + w_cur * o

    for jb in range(tq // blk):
        cur = slice(jb * blk, (jb + 1) * blk)
        if jb > 0:
            kcat = k_ref[(jb - 1) * blk:(jb + 1) * blk, :]
            vcat = v_ref[(jb - 1) * blk:(jb + 1) * blk, :]
        else:
            kcat = jnp.concatenate([ktail[...], k_ref[cur, :]], axis=0)
            vcat = jnp.concatenate([vtail[...], v_ref[cur, :]], axis=0)
        m, l, o = block_stats(q_ref[cur, :], kcat, vcat, jb == 0)
        tmp_s[0] = jnp.broadcast_to(m, (blk, HEAD_DIM))
        tmp_s[1] = jnp.broadcast_to(l, (blk, HEAD_DIM))
        tmp_s[2] = o
        per = blk // sub
        for r in range(sub):
            dst = slice(r * nq + jb * per, r * nq + (jb + 1) * per)
            src = pl.ds(r, per, stride=sub)
            m_s[dst, :] = tmp_s[0, src, :]
            l_s[dst, :] = tmp_s[1, src, :]
            acc_s[dst, :] = tmp_s[2, src, :]

    for r in range(sub):
        for jb in range(nq // blk):
            base = r * nq + jb * blk
            cur = slice(base, base + blk)
            if jb > 0:
                kcat = k4[slot, base - blk:base + blk, :]
                vcat = v4[slot, base - blk:base + blk, :]
            else:
                last = slice((r + 1) * nq - blk, (r + 1) * nq)
                kcat = jnp.concatenate([k4[other, last, :], k4[slot, cur, :]], axis=0)
                vcat = jnp.concatenate([v4[other, last, :], v4[slot, cur, :]], axis=0)
            m, l, o = block_stats(q4[cur, :], kcat, vcat, jb == 0)
            merge(cur, m, l, o)

    for c in range(16):
        sl = pl.ds((c % sub) * nq + c // sub, blk, stride=sub)
        kcat = jnp.concatenate([k4[other, sl, :], k4[slot, sl, :]], axis=0)
        vcat = jnp.concatenate([v4[other, sl, :], v4[slot, sl, :]], axis=0)
        m, l, o = block_stats(q4[sl, :], kcat, vcat, True)
        merge(sl, m, l, o)

    out = acc_s[...] / l_s[...]
    out = out * lax.rsqrt(jnp.mean(out * out, axis=-1, keepdims=True) + RMS_EPS) * nw_ref[...]
    for r in range(sub):
        nat_s[pl.ds(r, nq, stride=sub), :] = out[r * nq:(r + 1) * nq]
    o_ref[...] = nat_s[...].astype(o_ref.dtype)
    ktail[...] = k_ref[tq - blk:tq, :]
    vtail[...] = v_ref[tq - blk:tq, :]


def _attn(qb, kb, vb, norm_w, bsz, seq):
    t = qb.shape[0]
    tq = ATT_TILE
    spb = seq // tq
    cur = lambda b, h, i: (b * spb + i, h)
    blk = pl.BlockSpec((tq, HEAD_DIM), cur)
    tile = pltpu.VMEM((tq, HEAD_DIM), F32)
    ring = pltpu.VMEM((2, tq, HEAD_DIM), F32)
    tail = pltpu.VMEM((ATT_BLOCK, HEAD_DIM), F32)
    return pl.pallas_call(
        _attn_kernel,
        out_shape=jax.ShapeDtypeStruct((t, ATT_HEADS * HEAD_DIM), BF16),
        grid=(bsz, ATT_HEADS, spb),
        in_specs=[blk, blk, blk, pl.BlockSpec((1, HEAD_DIM), lambda b, h, i: (0, 0))],
        out_specs=blk,
        scratch_shapes=[tile, ring, ring, tail, tail, tile, tile, tile,
                        pltpu.VMEM((3, ATT_BLOCK, HEAD_DIM), F32), tile],
        compiler_params=_params(("parallel", "parallel", "arbitrary")),
        name="attn",
    )(qb, kb, vb, norm_w.reshape(1, HEAD_DIM))


def _layer_norm(y, g, b):
    mu = jnp.mean(y, axis=-1, keepdims=True)
    yc = y - mu
    var = jnp.mean(yc * yc, axis=-1, keepdims=True)
    return yc * lax.rsqrt(var + LN_EPS) * g + b


def _outproj_kernel(oa_ref, ob_ref, x_ref, mod_ref, wo_ref, g_ref, b_ref,
                    wrh_ref, wrl_ref, br_ref,
                    x1_ref, h2_ref, ri_ref, rg_ref, cnt_ref, run_s, mix_s, hi_s, lo_s):
    @pl.when(pl.program_id(0) == 0)
    def _():
        run_s[...] = jnp.zeros(run_s.shape, F32)

    tm = x_ref.shape[0]
    gate1 = mod_ref[0, 2:3, :]
    shift2 = mod_ref[0, 3:4, :]
    scale2 = mod_ref[0, 4:5, :]
    n_parts = OUTPROJ_PARTS
    part = tm // n_parts
    grp_rows = 2 * SUBLANES

    def project(p):
        rows = slice(p * part, (p + 1) * part)
        mix_s[rows, :] = jnp.dot(jnp.concatenate([oa_ref[rows, :], ob_ref[rows, :]], axis=1), wo_ref[...],
                                 preferred_element_type=F32)

    def normalise(p):
        for r0 in range(p * part, (p + 1) * part, grp_rows):
            rows = slice(r0, r0 + grp_rows)
            x1 = _layer_norm(DEEPNORM_ALPHA * x_ref[rows, :] + (1.0 + gate1) * mix_s[rows, :],
                             g_ref[...], b_ref[...])
            x1_ref[rows, :] = x1
            h2 = x1 * (1.0 + scale2) + shift2
            h2_ref[rows] = h2.reshape(grp_rows, SUBLANES, LANES)
            hi = _bf(h2)
            hi_s[rows, :] = hi
            lo_s[rows, :] = _bf(h2 - hi.astype(F32))

    def route(p):
        rows = slice(p * part, (p + 1) * part)
        hi = hi_s[rows, :]
        return (jnp.dot(hi, wrh_ref[...], preferred_element_type=F32)
                + jnp.dot(hi, wrl_ref[...], preferred_element_type=F32)
                + jnp.dot(lo_s[rows, :], wrh_ref[...], preferred_element_type=F32))

    project(0)
    parts = []
    for p in range(n_parts):
        if p + 1 < n_parts:
            project(p + 1)
        normalise(p)
        parts.append(route(p))
    logits = jnp.concatenate(parts, axis=0) + br_ref[...]
    lane_i = lax.broadcasted_iota(jnp.int32, (tm, LANES), 1)
    lane = lane_i.astype(F32)
    lg = jnp.where(lane_i < N_GROUPS, logits, NEG)
    mg = jnp.max(lg, axis=-1, keepdims=True)
    grp = jnp.min(jnp.where(lg == mg, lane, float(LANES)), axis=-1, keepdims=True)
    gate_grp = 1.0 / jnp.sum(jnp.exp(lg - mg), axis=-1, keepdims=True)
    first_lane = N_GROUPS + EXPERTS_PER_GROUP * grp
    sel = (lane >= first_lane) & (lane < first_lane + EXPERTS_PER_GROUP)
    le = jnp.where(sel, logits, NEG)
    v1 = jnp.max(le, axis=-1, keepdims=True)
    i1 = jnp.min(jnp.where(le == v1, lane, float(LANES)), axis=-1, keepdims=True)
    le2 = jnp.where(lane == i1, NEG, le)
    v2 = jnp.max(le2, axis=-1, keepdims=True)
    i2 = jnp.min(jnp.where(le2 == v2, lane, float(LANES)), axis=-1, keepdims=True)
    e21 = jnp.exp(v2 - v1)
    g1 = gate_grp / (1.0 + e21)
    g2 = gate_grp * e21 / (1.0 + e21)

    oh1 = lane == i1
    oh2 = lane == i2
    onehot = jnp.where(oh1 | oh2, 1.0, 0.0).astype(F32)
    ti = lax.broadcasted_iota(jnp.int32, (tm, tm), 0)
    tj = lax.broadcasted_iota(jnp.int32, (tm, tm), 1)
    before = jnp.where(tj < ti, 1.0, 0.0).astype(F32)
    tot = _mm(before, onehot) + run_s[...]
    r1 = jnp.sum(jnp.where(oh1, tot, 0.0), axis=-1, keepdims=True)
    r2 = jnp.sum(jnp.where(oh2, tot, 0.0), axis=-1, keepdims=True)
    run_s[...] = run_s[...] + jnp.sum(onehot, axis=0, keepdims=True)
    cnt_ref[...] = run_s[...]

    cols = jnp.where(lane_i == 0, i1 - N_GROUPS, 0.0)
    cols = jnp.where(lane_i == 1, i2 - N_GROUPS, cols)
    r1_hi = jnp.floor(r1 * (1.0 / 256.0))
    r2_hi = jnp.floor(r2 * (1.0 / 256.0))
    cols = jnp.where(lane_i == 2, r1_hi, cols)
    cols = jnp.where(lane_i == 3, r1 - 256.0 * r1_hi, cols)
    cols = jnp.where(lane_i == 4, r2_hi, cols)
    cols = jnp.where(lane_i == 5, r2 - 256.0 * r2_hi, cols)
    pick = jnp.where(lax.broadcasted_iota(jnp.int32, (SUBLANES, LANES), 0)
                     == lax.broadcasted_iota(jnp.int32, (SUBLANES, LANES), 1), 1.0, 0.0).astype(F32)
    ri_ref[...] = _mm_nt(pick, cols).astype(jnp.int32)
    rg_ref[...] = jnp.where(lane_i == 0, g1, jnp.where(lane_i == 1, g2, 0.0))


def _outproj(oa, ob, xf, mod, w_o, ln_g, ln_b, w_rg, b_rg, w_re, b_re, seq):
    t, d = xf.shape
    tm = PROJ_TILE
    gw = oa.shape[1]
    wo = _bf(w_o)
    wr = jnp.zeros((d, LANES), F32).at[:, :N_GROUPS].set(w_rg).at[:, N_GROUPS:N_GROUPS + N_EXPERTS].set(w_re)
    wrh = _bf(wr)
    wrl = _bf(wr - wrh.astype(F32))
    br = jnp.zeros((1, LANES), F32).at[0, :N_GROUPS].set(b_rg).at[0, N_GROUPS:N_GROUPS + N_EXPERTS].set(b_re)
    spb = seq // tm
    row = lambda i: (i, 0)
    const = lambda i: (0, 0)
    return pl.pallas_call(
        _outproj_kernel,
        out_shape=(jax.ShapeDtypeStruct((t, d), F32), jax.ShapeDtypeStruct((t, d // LANES, LANES), F32),
                   jax.ShapeDtypeStruct((SUBLANES, t), jnp.int32), jax.ShapeDtypeStruct((t, LANES), F32),
                   jax.ShapeDtypeStruct((1, LANES), F32)),
        grid=(t // tm,),
        in_specs=[pl.BlockSpec((tm, gw), row), pl.BlockSpec((tm, gw), row), pl.BlockSpec((tm, d), row),
                  pl.BlockSpec((1, 6, d), lambda i: (i // spb, 0, 0)),
                  pl.BlockSpec((2 * gw, d), const),
                  pl.BlockSpec((1, d), const), pl.BlockSpec((1, d), const),
                  pl.BlockSpec((d, LANES), const), pl.BlockSpec((d, LANES), const),
                  pl.BlockSpec((1, LANES), const)],
        out_specs=(pl.BlockSpec((tm, d), row), pl.BlockSpec((tm, d // LANES, LANES), lambda i: (i, 0, 0)),
                   pl.BlockSpec((SUBLANES, tm), lambda i: (0, i)), pl.BlockSpec((tm, LANES), row),
                   pl.BlockSpec((1, LANES), const)),
        scratch_shapes=[pltpu.VMEM((1, LANES), F32), pltpu.VMEM((tm, d), F32),
                        pltpu.VMEM((tm, d), BF16), pltpu.VMEM((tm, d), BF16)],
        compiler_params=_params(("arbitrary",)),
        name="outproj",
    )(oa, ob, xf, mod, wo, ln_g.reshape(1, d), ln_b.reshape(1, d), wrh, wrl, br)


def _dispatch_kernel(d1_ref, d2_ref, h_ref, xs_ref, sem):
    tm = h_ref.shape[0]
    i = pl.program_id(0)

    def row_copy(tk, dest):
        return pltpu.make_async_copy(h_ref.at[tk], xs_ref.at[dest], sem)

    def issue(g, carry):
        for u in range(DMA_UNROLL):
            tk = g * DMA_UNROLL + u
            tok = i * tm + tk
            row_copy(tk, d1_ref[tok]).start(priority=0)
            row_copy(tk, d2_ref[tok]).start(priority=1)
        return carry

    lax.fori_loop(0, tm // DMA_UNROLL, issue, 0)

    tile_copy = pltpu.make_async_copy(h_ref, xs_ref.at[pl.ds(0, tm)], sem)
    tile_copy.wait()
    tile_copy.wait()


def _dispatch(h2, d1, d2):
    t, sub, lanes = h2.shape
    tm = DISPATCH_TILE
    return pl.pallas_call(
        _dispatch_kernel,
        out_shape=jax.ShapeDtypeStruct((2 * t, sub, lanes), F32),
        grid_spec=pltpu.PrefetchScalarGridSpec(
            num_scalar_prefetch=2,
            grid=(t // tm,),
            in_specs=[pl.BlockSpec((tm, sub, lanes), lambda i, *_: (i, 0, 0))],
            out_specs=pl.BlockSpec(memory_space=pl.ANY),
            scratch_shapes=[pltpu.SemaphoreType.DMA]),
        compiler_params=_params(("arbitrary",)),
        name="dispatch",
    )(d1, d2, h2)


def _experts_kernel(wb_ref, we_ref, lo_ref, hi_ref, nw_ref, first_ref, ring_ref, next_ref,
                    xs_ref, wg_hbm, wu_hbm, wd_hbm, ys_ref,
                    wg_f, wu_f, wd_f, wg_s, wu_s, wd_s, sem):
    w = pl.program_id(0)

    def fetch(expert, slot):
        return (pltpu.make_async_copy(wg_hbm.at[expert], wg_f.at[slot], sem.at[slot]),
                pltpu.make_async_copy(wu_hbm.at[expert], wu_f.at[slot], sem.at[slot]),
                pltpu.make_async_copy(wd_hbm.at[expert], wd_f.at[slot], sem.at[slot]))

    @pl.when(w == 0)
    def _():
        for cp in fetch(we_ref[0], 0):
            cp.start()

    @pl.when((first_ref[w] == 1) & (w < nw_ref[0]))
    def _():
        slot = ring_ref[w]
        for cp in fetch(we_ref[w], slot):
            cp.wait()

        @pl.when(next_ref[w] >= 0)
        def _():
            for cp in fetch(next_ref[w], 1 - slot):
                cp.start()

        wg_s[...] = _bf(wg_f[slot])
        wu_s[...] = _bf(wu_f[slot])
        wd_s[...] = _bf(wd_f[slot])

    @pl.when(w < nw_ref[0])
    def _():
        bm, sub, lanes = xs_ref.shape
        x = _bf(xs_ref[...].reshape(bm, sub * lanes))
        hid = (_silu(jnp.dot(x, wg_s[...], preferred_element_type=F32))
               * jnp.dot(x, wu_s[...], preferred_element_type=F32))
        y = jnp.dot(_bf(hid), wd_s[...], preferred_element_type=F32).reshape(bm, sub, lanes)
        row = lax.broadcasted_iota(jnp.int32, (bm, 1, 1), 0)
        mine = (row >= lo_ref[w]) & (row < hi_ref[w])

        @pl.when(lo_ref[w] == 0)
        def _():
            ys_ref[...] = jnp.where(mine, y, 0.0)

        @pl.when(lo_ref[w] > 0)
        def _():
            ys_ref[...] = jnp.where(mine, y, ys_ref[...])


def _experts(xs, item_block, item_expert, item_lo, item_hi, n_items, w_gate, w_up, w_down):
    n_slots, sub, lanes = xs.shape
    d = sub * lanes
    ff = w_gate.shape[2]
    bm = EXPERT_BLOCK
    n = item_block.shape[0]
    idx = jnp.arange(n, dtype=jnp.int32)
    first = jnp.concatenate([jnp.ones((1,), jnp.int32),
                             (item_expert[1:] != item_expert[:-1]).astype(jnp.int32)])
    ring = (jnp.cumsum(first) - 1) % 2
    next_first = lax.cummin(jnp.where(first == 1, idx, n), reverse=True)
    next_first = jnp.concatenate([next_first[1:], jnp.full((1,), n, jnp.int32)])
    nxt = jnp.where(next_first < n, item_expert[jnp.minimum(next_first, n - 1)], -1).astype(jnp.int32)
    slot = lambda w, *_: (_[0][w], 0, 0)
    return pl.pallas_call(
        _experts_kernel,
        out_shape=jax.ShapeDtypeStruct((n_slots, sub, lanes), F32),
        grid_spec=pltpu.PrefetchScalarGridSpec(
            num_scalar_prefetch=8,
            grid=(n,),
            in_specs=[pl.BlockSpec((bm, sub, lanes), slot),
                      pl.BlockSpec(memory_space=pl.ANY), pl.BlockSpec(memory_space=pl.ANY),
                      pl.BlockSpec(memory_space=pl.ANY)],
            out_specs=pl.BlockSpec((bm, sub, lanes), slot),
            scratch_shapes=[pltpu.VMEM((2, d, ff), F32), pltpu.VMEM((2, d, ff), F32),
                            pltpu.VMEM((2, ff, d), F32),
                            pltpu.VMEM((d, ff), BF16), pltpu.VMEM((d, ff), BF16),
                            pltpu.VMEM((ff, d), BF16), pltpu.SemaphoreType.DMA((2,))]),
        compiler_params=_params(("arbitrary",)),
        name="experts",
    )(item_block, item_expert, item_lo, item_hi, n_items, first, ring.astype(jnp.int32), nxt,
      xs, w_gate, w_up, w_down)


def _combine_kernel(d1_ref, d2_ref, ys_ref, rg_ref, x1_ref, mod_ref, g_ref, b_ref, o_ref, ya, yb, sem):
    tm = x1_ref.shape[0]
    i = pl.program_id(0)
    n = pl.num_programs(0)

    def row_copy(dest, buf, slot, tk):
        return pltpu.make_async_copy(ys_ref.at[dest], buf.at[slot, tk], sem.at[slot])

    slot = i % 2
    gate2 = mod_ref[0, 5:6, :]

    def issue_group(step, dst_slot, g):
        for u in range(DMA_UNROLL):
            tk = g * DMA_UNROLL + u
            tok = step * tm + tk
            row_copy(d1_ref[tok], ya, dst_slot, tk).start(priority=0)
            row_copy(d2_ref[tok], yb, dst_slot, tk).start(priority=1)

    def gather_tile(step, dst_slot):
        def issue(g, carry):
            issue_group(step, dst_slot, g)
            return carry

        lax.fori_loop(0, tm // DMA_UNROLL, issue, 0)

    @pl.when(i == 0)
    def _():
        gather_tile(0, 0)

    @pl.when(i + 1 < n)
    def _():
        gather_tile(i + 1, 1 - slot)

    pltpu.make_async_copy(ys_ref.at[pl.ds(0, tm)], ya.at[slot], sem.at[slot]).wait()
    pltpu.make_async_copy(ys_ref.at[pl.ds(0, tm)], yb.at[slot], sem.at[slot]).wait()

    rg = rg_ref[...]
    d = x1_ref.shape[1]
    y = rg[:, 0:1] * ya[slot].reshape(tm, d) + rg[:, 1:2] * yb[slot].reshape(tm, d)
    o_ref[...] = _layer_norm(DEEPNORM_ALPHA * x1_ref[...] + (1.0 + gate2) * y, g_ref[...], b_ref[...])


def _combine(ys, d1, d2, rg, x1, mod, ln_g, ln_b, seq):
    t, d = x1.shape
    tm = ROW_TILE
    spb = seq // tm
    row = lambda i, *_: (i, 0)
    const = lambda i, *_: (0, 0)
    buf = pltpu.VMEM((2, tm) + ys.shape[1:], F32)
    return pl.pallas_call(
        _combine_kernel,
        out_shape=jax.ShapeDtypeStruct((t, d), F32),
        grid_spec=pltpu.PrefetchScalarGridSpec(
            num_scalar_prefetch=2,
            grid=(t // tm,),
            in_specs=[pl.BlockSpec(memory_space=pl.ANY),
                      pl.BlockSpec((tm, LANES), row), pl.BlockSpec((tm, d), row),
                      pl.BlockSpec((1, 6, d), lambda i, *_: (i // spb, 0, 0)),
                      pl.BlockSpec((1, d), const), pl.BlockSpec((1, d), const)],
            out_specs=pl.BlockSpec((tm, d), row),
            scratch_shapes=[buf, buf, pltpu.SemaphoreType.DMA((2,))]),
        compiler_params=_params(("arbitrary",)),
        name="combine",
    )(d1, d2, ys, rg, x1, mod, ln_g.reshape(1, d), ln_b.reshape(1, d))


def _layer(x, c, positions, w_ada, b_ada, w_in, conv_w, a_log, dt_bias, gdn_norm_w, attn_norm_w,
           w_o, ln1_g, ln1_b, w_rg, b_rg, w_re, b_re, w_gate, w_up, w_down, ln2_g, ln2_b):
    bsz, seq, d = x.shape
    t = bsz * seq
    xf = x.reshape(t, d)
    mod = _ada(c, w_ada, b_ada)
    qkv, z, bdc, bdr, qb, kb, vb = _inproj(xf, mod, positions.reshape(t, 1), w_in, conv_w, seq)
    gw = GDN_HEADS * HEAD_DIM
    oa = _gdn(qkv.reshape(bsz, seq, 3 * gw), z.reshape(bsz, seq, gw), bdc.reshape(bsz, seq, LANES), bdr,
              a_log, dt_bias, gdn_norm_w).reshape(t, gw)
    ob = _attn(qb, kb, vb, attn_norm_w, bsz, seq)
    x1, h2, ri, rg, cnt = _outproj(oa, ob, xf, mod, w_o, ln1_g, ln1_b, w_rg, b_rg, w_re, b_re, seq)

    bm = EXPERT_BLOCK
    counts = cnt[0, N_GROUPS:N_GROUPS + N_EXPERTS].astype(jnp.int32)
    seg_end = jnp.cumsum(counts)
    seg_start = seg_end - counts
    first_blk = seg_start // bm
    n_per = jnp.where(counts > 0, (seg_end - 1) // bm - first_blk + 1, 0)
    item_end = jnp.cumsum(n_per)
    n_items = item_end[-1:]
    max_items = (2 * t) // bm + N_EXPERTS - 1
    w_idx = jnp.minimum(jnp.arange(max_items, dtype=jnp.int32), n_items[0] - 1)
    item_expert = jnp.minimum(jnp.sum(item_end[None, :] <= w_idx[:, None], axis=1), N_EXPERTS - 1).astype(jnp.int32)
    item_block = first_blk[item_expert] + w_idx - (item_end - n_per)[item_expert]
    item_lo = jnp.maximum(seg_start[item_expert] - item_block * bm, 0)
    item_hi = jnp.minimum(seg_end[item_expert] - item_block * bm, bm)
    expert_ids = jnp.arange(N_EXPERTS, dtype=jnp.int32)[:, None]

    def seg_of(e):
        return jnp.sum(jnp.where(e[None, :] == expert_ids, seg_start[:, None], 0), axis=0)

    d1 = seg_of(ri[0]) + ri[2] * 256 + ri[3]
    d2 = seg_of(ri[1]) + ri[4] * 256 + ri[5]

    xs = _dispatch(h2, d1, d2)
    ys = _experts(xs, item_block, item_expert, item_lo, item_hi, n_items, w_gate, w_up, w_down)
    out = _combine(ys, d1, d2, rg, x1, mod, ln2_g, ln2_b, seq)
    return out.reshape(bsz, seq, d)


def kernel(x, c, positions, w_ada, b_ada, w_in, conv_w, a_log, dt_bias, gdn_norm_w, attn_norm_w, w_o, ln1_g, ln1_b, w_router_group, b_router_group, w_router_expert, b_router_expert, w_gate, w_up, w_down, ln2_g, ln2_b):
    assert w_ada.shape[0] == DEPTH
    return _layer(x, c, positions, w_ada[0], b_ada[0], w_in[0], conv_w[0], a_log[0], dt_bias[0],
                  gdn_norm_w[0], attn_norm_w[0], w_o[0], ln1_g[0], ln1_b[0],
                  w_router_group[0], b_router_group[0], w_router_expert[0], b_router_expert[0],
                  w_gate[0], w_up[0], w_down[0], ln2_g[0], ln2_b[0])
```

```python
import functools
import math

import jax
import jax.numpy as jnp
from jax import lax
from jax.experimental import pallas as pl
from jax.experimental.pallas import tpu as pltpu

F32 = jnp.float32
BF16 = jnp.bfloat16
LOG2E = math.log2(math.e)

GDN_HEADS = 4
ATT_HEADS = 4
HEAD_DIM = 128
CONV_WIDTH = 4
DILATED_PATTERNS = ((128, 1), (512, 4), (2048, 16))
ROPE_THETA = 500000.0
ROPE_DIMS = HEAD_DIM // 4
N_GROUPS = 4
EXPERTS_PER_GROUP = 8
N_EXPERTS = N_GROUPS * EXPERTS_PER_GROUP
ROUTER_ROWS = -(-(N_GROUPS + N_EXPERTS) // 8) * 8
DEPTH = 1
DEEPNORM_ALPHA = (2.0 * DEPTH) ** 0.25
LN_EPS = 1e-5
RMS_EPS = 1e-6

LANES = 128
SUBLANES = 8
VMEM_LIMIT = 48 * 1024 * 1024

GDN_BLOCK = 128
GDN_TILE = 256
ATT_BLOCK = 128
ATT_TILE = 2048
PROJ_TILE = 512
ROW_TILE = 256
DISPATCH_TILE = 1024
EXPERT_BLOCK = 256
OUTPROJ_PARTS = 2
DMA_UNROLL = 8
NEG = -1e30


def _bf(x):
    return x.astype(BF16)


def _mm(a, b):
    return jnp.dot(_bf(a), _bf(b), preferred_element_type=F32)


def _mm_nt(a, b):
    return lax.dot_general(_bf(a), _bf(b), (((1,), (1,)), ((), ())), preferred_element_type=F32)


def _mm_tn(a, b):
    return lax.dot_general(_bf(a), _bf(b), (((0,), (0,)), ((), ())), preferred_element_type=F32)


def _split3(x):
    x1 = _bf(x)
    r1 = x - x1.astype(F32)
    x2 = _bf(r1)
    return x1, x2, _bf(r1 - x2.astype(F32))


def _sigmoid(x):
    return 1.0 / (1.0 + jnp.exp(-x))


def _silu(x):
    return x * _sigmoid(x)


def _softplus(x):
    return jnp.maximum(x, 0.0) + jnp.log(1.0 + jnp.exp(-jnp.abs(x)))


def _params(sem):
    return pltpu.CompilerParams(dimension_semantics=sem, vmem_limit_bytes=VMEM_LIMIT)


def _ada_kernel(ct_ref, w_ref, b_ref, o_ref, *, bsz):
    sc = _silu(ct_ref[...])
    w = w_ref[...]
    rows = [jnp.sum(w * sc[:, b:b + 1], axis=0, keepdims=True) for b in range(bsz)]
    rows.append(jnp.zeros((o_ref.shape[0] - bsz, w.shape[1]), F32))
    o_ref[...] = jnp.concatenate(rows, axis=0) + b_ref[...]


def _ada(c, w_ada, b_ada):
    bsz, d = c.shape
    n = w_ada.shape[1]
    tn = 512
    assert bsz <= SUBLANES
    ct = jnp.zeros((d, LANES), F32).at[:, :bsz].set(c.T)
    out = pl.pallas_call(
        functools.partial(_ada_kernel, bsz=bsz),
        out_shape=jax.ShapeDtypeStruct((SUBLANES, n), F32),
        grid=(n // tn,),
        in_specs=[pl.BlockSpec((d, LANES), lambda j: (0, 0)),
                  pl.BlockSpec((d, tn), lambda j: (0, j)),
                  pl.BlockSpec((1, tn), lambda j: (0, j))],
        out_specs=pl.BlockSpec((SUBLANES, tn), lambda j: (0, j)),
        compiler_params=_params(("parallel",)),
        name="ada",
    )(ct, w_ada, b_ada.reshape(1, n))
    return out[:bsz].reshape(bsz, 6, d)


def _inproj_kernel(x_ref, mod_ref, pos_ref, invf_ref, convw_ref, wqkv_ref, wz_ref, wbd_ref, wbdt_ref,
                   wq_ref, wk_ref, wv_ref,
                   qkv_ref, z_ref, bdc_ref, bdr_ref, qb_ref, kb_ref, vb_ref, cbuf, *, steps_per_seq):
    tm = x_ref.shape[0]
    halo = SUBLANES
    n_slabs = cbuf.shape[0]

    @pl.when(pl.program_id(0) % steps_per_seq == 0)
    def _():
        cbuf[:, 0:halo, :] = jnp.zeros((n_slabs, halo, LANES), F32)

    half = ROPE_DIMS // 2
    groups = LANES // half
    lane = lax.broadcasted_iota(jnp.int32, (1, LANES), 1)
    first = lane < half
    rot = lane < ROPE_DIMS
    ang = pos_ref[0].astype(F32) * invf_ref[...]
    cos_c = jnp.cos(ang)
    sin_c = jnp.sin(ang)
    cos_parts, sin_parts = [], []
    for j in range(groups):
        lo_sh = (LANES - half * j) % LANES
        hi_sh = (LANES - half * j + half) % LANES
        c_lo = pltpu.roll(cos_c, lo_sh, 1) if lo_sh else cos_c
        c_hi = pltpu.roll(cos_c, hi_sh, 1) if hi_sh else cos_c
        s_lo = pltpu.roll(sin_c, lo_sh, 1) if lo_sh else sin_c
        s_hi = pltpu.roll(sin_c, hi_sh, 1) if hi_sh else sin_c
        cos_parts.append(jnp.where(first, c_lo, jnp.where(rot, c_hi, 1.0)))
        sin_parts.append(jnp.where(first, -s_lo, jnp.where(rot, s_hi, 0.0)))
    cosv = jnp.concatenate(cos_parts, axis=0)
    sin_signed = jnp.concatenate(sin_parts, axis=0)

    def rope(y):
        outs = []
        for hh in range(ATT_HEADS):
            yh = y[:, hh * HEAD_DIM:(hh + 1) * HEAD_DIM]
            partner = jnp.where(first, pltpu.roll(yh, LANES - half, 1), pltpu.roll(yh, half, 1))
            outs.append(yh * cosv + partner * sin_signed)
        return jnp.concatenate(outs, axis=1)

    def conv_slice(s):
        sl = slice(s * HEAD_DIM, (s + 1) * HEAD_DIM)
        off = halo - (CONV_WIDTH - 1)
        acc = convw_ref[0:1, sl] * cbuf[s, off:off + tm, :]
        for j in range(1, CONV_WIDTH):
            acc = acc + convw_ref[j:j + 1, sl] * cbuf[s, off + j:off + j + tm, :]
        cbuf[s, 0:halo, :] = cbuf[s, tm:tm + halo, :]
        y = _silu(acc)
        if s < 2 * GDN_HEADS:
            y = y * lax.rsqrt(jnp.sum(y * y, axis=-1, keepdims=True) + RMS_EPS)
        if s < GDN_HEADS:
            y = y * (HEAD_DIM ** -0.5)
        qkv_ref[:, sl] = y

    shift = mod_ref[0, 0:1, :]
    scale = mod_ref[0, 1:2, :]
    h = _bf(x_ref[...] * (1.0 + scale) + shift)
    chunk = 2 * HEAD_DIM
    n_chunks = n_slabs // 2

    def project_chunk(c):
        pre = jnp.dot(h, wqkv_ref[:, c * chunk:(c + 1) * chunk], preferred_element_type=F32)
        cbuf[2 * c, halo:halo + tm, :] = pre[:, :HEAD_DIM]
        cbuf[2 * c + 1, halo:halo + tm, :] = pre[:, HEAD_DIM:]

    def conv_chunk(c):
        conv_slice(2 * c)
        conv_slice(2 * c + 1)

    project_chunk(0)
    for c in range(1, n_chunks):
        project_chunk(c)
        conv_chunk(c - 1)
    qb = jnp.dot(h, wq_ref[...], preferred_element_type=F32)
    conv_chunk(n_chunks - 1)
    kb = jnp.dot(h, wk_ref[...], preferred_element_type=F32)
    qb_ref[...] = rope(qb) * (HEAD_DIM ** -0.5 * LOG2E)
    vb_ref[...] = jnp.dot(h, wv_ref[...], preferred_element_type=F32)
    kb_ref[...] = rope(kb)
    z_ref[...] = jnp.dot(h, wz_ref[...], preferred_element_type=F32)
    bdc_ref[...] = jnp.dot(h, wbd_ref[...], preferred_element_type=F32)
    bdr_ref[0] = lax.dot_general(wbdt_ref[...], h, (((1,), (1,)), ((), ())),
                                 preferred_element_type=F32)


def _inproj(xf, mod, pos, w_in, conv_w, seq):
    t, d = xf.shape
    tm = PROJ_TILE
    gw = GDN_HEADS * HEAD_DIM
    aw = ATT_HEADS * HEAD_DIM
    o0 = 3 * gw
    o1 = o0 + gw
    o2 = o1 + 2 * GDN_HEADS
    wb = _bf(w_in)
    wqkv, wz = wb[:, :o0], wb[:, o0:o1]
    wbd_n = wb[:, o1:o2]
    wbd = jnp.zeros((d, LANES), BF16).at[:, :2 * GDN_HEADS].set(wbd_n)
    wbdt = wbd_n.T
    wq, wk, wv = wb[:, o2:o2 + aw], wb[:, o2 + aw:o2 + 2 * aw], wb[:, o2 + 2 * aw:o2 + 3 * aw]
    half = ROPE_DIMS // 2
    groups = LANES // half
    inv_freq = ROPE_THETA ** (-jnp.arange(half, dtype=F32) * 2.0 / ROPE_DIMS)
    invf = jnp.tile(inv_freq, groups).reshape(1, LANES)
    pos = jnp.repeat(pos.reshape(t // tm, groups, tm // groups).transpose(0, 2, 1), half, axis=2)
    spb = seq // tm
    row = lambda i: (i, 0)
    const = lambda i: (0, 0)
    return pl.pallas_call(
        functools.partial(_inproj_kernel, steps_per_seq=spb),
        out_shape=(jax.ShapeDtypeStruct((t, o0), F32), jax.ShapeDtypeStruct((t, gw), F32),
                   jax.ShapeDtypeStruct((t, LANES), F32), jax.ShapeDtypeStruct((t // seq, SUBLANES, seq), F32),
                   jax.ShapeDtypeStruct((t, aw), F32), jax.ShapeDtypeStruct((t, aw), F32),
                   jax.ShapeDtypeStruct((t, aw), F32)),
        grid=(t // tm,),
        in_specs=[pl.BlockSpec((tm, d), row),
                  pl.BlockSpec((1, 6, d), lambda i: (i // spb, 0, 0)),
                  pl.BlockSpec((1, tm // groups, LANES), lambda i: (i, 0, 0)),
                  pl.BlockSpec((1, LANES), const), pl.BlockSpec((CONV_WIDTH, o0), const),
                  pl.BlockSpec((d, o0), const), pl.BlockSpec((d, gw), const),
                  pl.BlockSpec((d, LANES), const), pl.BlockSpec((2 * GDN_HEADS, d), const),
                  pl.BlockSpec((d, aw), const), pl.BlockSpec((d, aw), const),
                  pl.BlockSpec((d, aw), const)],
        out_specs=(pl.BlockSpec((tm, o0), row), pl.BlockSpec((tm, gw), row),
                   pl.BlockSpec((tm, LANES), row),
                   pl.BlockSpec((1, SUBLANES, tm), lambda i: (i // spb, 0, i % spb)),
                   pl.BlockSpec((tm, aw), row), pl.BlockSpec((tm, aw), row),
                   pl.BlockSpec((tm, aw), row)),
        scratch_shapes=[pltpu.VMEM((o0 // HEAD_DIM, tm + 2 * SUBLANES, HEAD_DIM), F32)],
        compiler_params=_params(("arbitrary",)),
        name="inproj",
    )(xf, mod, pos, invf, conv_w, wqkv, wz, wbd, wbdt, wq, wk, wv)


def _gdn_kernel(qkv_ref, z_ref, bdc_ref, bdr_ref, gpc_ref, gpr_ref, nw_ref, o_ref,
                u_s, w_s, qd_s, kd_s, qk_s, a_s, x_s, y_s, st_ref):
    nbat, ts = qkv_ref.shape[0], qkv_ref.shape[1]
    nb = ts // GDN_BLOCK
    gw = GDN_HEADS * HEAD_DIM

    @pl.when(pl.program_id(0) == 0)
    def _():
        st_ref[...] = jnp.zeros(st_ref.shape, F32)

    blk_n = GDN_BLOCK
    hd = HEAD_DIM
    ti = lax.broadcasted_iota(jnp.int32, (ts, ts), 0)
    tj = lax.broadcasted_iota(jnp.int32, (ts, ts), 1)
    same = (ti // blk_n) == (tj // blk_n)
    m_low = jnp.where(same & (tj <= ti), 1.0, 0.0).astype(BF16)
    m_up = jnp.where(same & (ti <= tj), 1.0, 0.0).astype(BF16)

    split3 = _split3

    beta_c = [_sigmoid(bdc_ref[bb]) for bb in range(nbat)]
    g_c = jnp.concatenate([-jnp.exp(gpc_ref[0:1, :]) * _softplus(bdc_ref[bb] + gpc_ref[1:2, :])
                           for bb in range(nbat)], axis=1)
    g_r = jnp.concatenate([-jnp.exp(gpr_ref[:, 0:1]) * _softplus(bdr_ref[bb] + gpr_ref[:, 1:2])
                           for bb in range(nbat)], axis=0)
    gc_c = sum(jnp.dot(m_low, part, preferred_element_type=F32) for part in split3(g_c))
    gc_r = sum(jnp.dot(part, m_up, preferred_element_type=F32) for part in split3(g_r))

    def cat2(m):
        return jnp.concatenate([m, m], axis=1)

    ii = lax.broadcasted_iota(jnp.int32, (blk_n, blk_n), 0)
    jj = lax.broadcasted_iota(jnp.int32, (blk_n, blk_n), 1)
    lower = cat2(jj <= ii)
    strict = cat2(jj < ii)
    eye = cat2(jnp.where(ii == jj, 1.0, 0.0).astype(F32))
    levels = []
    b = 1
    while b < blk_n:
        levels.append(cat2(((ii // b) == (jj // b) + 1) & (((jj // b) % 2) == 0)))
        b *= 2

    def block_diag(rp):
        n, m = rp.shape[0], rp.shape[1] // 2
        z = jnp.zeros((n, m), rp.dtype)
        return jnp.concatenate([jnp.concatenate([rp[:, :m], z], axis=1),
                                jnp.concatenate([z, rp[:, m:]], axis=1)], axis=0)

    def mm2(lp, rp):
        return jnp.dot(_bf(lp), block_diag(_bf(rp)), preferred_element_type=F32)

    def mm2_nt(lp, rp):
        return lax.dot_general(_bf(lp), block_diag(_bf(rp)), (((1,), (1,)), ((), ())),
                               preferred_element_type=F32)

    chains = [(bb, j, pp) for bb in range(nbat) for j in range(nb) for pp in range(GDN_HEADS // 2)]

    def tile_of(bb, j, pp):
        return bb, slice(j * blk_n, (j + 1) * blk_n), slice(2 * pp * hd, 2 * (pp + 1) * hd)

    def col_pair(arr, rows, col):
        return jnp.concatenate([jnp.broadcast_to(arr[rows, col:col + 1], (blk_n, hd)),
                                jnp.broadcast_to(arr[rows, col + 1:col + 2], (blk_n, hd))], axis=1)

    for c, (bb, j, pp) in enumerate(chains):
        blk = tile_of(bb, j, pp)
        rows, cols = blk[1], blk[2]
        last = slice((j + 1) * blk_n - 1, (j + 1) * blk_n)
        gcol = bb * LANES + GDN_HEADS + 2 * pp
        grow = bb * SUBLANES + GDN_HEADS + 2 * pp
        q = qkv_ref[bb, rows, cols]
        k = qkv_ref[bb, rows, slice(gw + cols.start, gw + cols.stop)]
        v = qkv_ref[bb, rows, slice(2 * gw + cols.start, 2 * gw + cols.stop)]
        beta = col_pair(beta_c[bb], rows, 2 * pp)
        gcc = col_pair(gc_c, rows, gcol)
        gtc = col_pair(gc_c, last, gcol)
        gcr = jnp.concatenate([jnp.broadcast_to(gc_r[grow:grow + 1, rows], (blk_n, hd)),
                               jnp.broadcast_to(gc_r[grow + 1:grow + 2, rows], (blk_n, hd))], axis=1)
        kb = k * beta
        eg = jnp.exp(gcc)
        dm = jnp.where(lower, jnp.exp(gcc - gcr), 0.0)
        a = jnp.where(strict, mm2_nt(kb, k) * dm, 0.0)
        a_s[c] = a
        x_s[c] = eye - jnp.where(levels[0], a, 0.0)
        u_s[blk] = v * beta
        w_s[blk] = kb * eg
        qd_s[blk] = q * eg
        kd_s[blk] = k * jnp.exp(gtc - gcc)
        qk_s[blk] = mm2_nt(q, k) * dm

    for lm in levels[1:]:
        for c in range(len(chains)):
            y_s[c] = mm2(x_s[c], jnp.where(lm, a_s[c], 0.0))
        for c in range(len(chains)):
            xc = x_s[c]
            x_s[c] = xc - mm2(y_s[c], xc)

    for c, (bb, j, pp) in enumerate(chains):
        blk = tile_of(bb, j, pp)
        u, w = _bf(u_s[blk]), _bf(w_s[blk])
        z = jnp.zeros((blk_n, 2 * hd), BF16)
        rhs = jnp.concatenate([jnp.concatenate([u[:, :hd], w[:, :hd], z], axis=1),
                               jnp.concatenate([z, u[:, hd:], w[:, hd:]], axis=1)], axis=0)
        sol = jnp.dot(_bf(x_s[c]), rhs, preferred_element_type=F32)
        u_s[blk] = jnp.concatenate([sol[:, 0:hd], sol[:, 2 * hd:3 * hd]], axis=1)
        w_s[blk] = jnp.concatenate([sol[:, hd:2 * hd], sol[:, 3 * hd:]], axis=1)

    for j in range(nb):
        for bb in range(nbat):
            for pp in range(GDN_HEADS // 2):
                blk = tile_of(bb, j, pp)
                si = bb * (GDN_HEADS // 2) + pp
                last = slice((j + 1) * blk_n - 1, (j + 1) * blk_n)
                gcol = bb * LANES + GDN_HEADS + 2 * pp
                state = st_ref[si]
                proj = mm2(jnp.concatenate([w_s[blk], qd_s[blk]], axis=0), state)
                v_new = u_s[blk] - proj[:blk_n]
                o = proj[blk_n:] + mm2(qk_s[blk], v_new)
                g_last = jnp.exp(jnp.concatenate(
                    [jnp.broadcast_to(gc_c[last, gcol:gcol + 1], (1, hd)),
                     jnp.broadcast_to(gc_c[last, gcol + 1:gcol + 2], (1, hd))], axis=1))
                kd = kd_s[blk]
                upd = jnp.concatenate([_mm_tn(kd[:, :hd], v_new[:, :hd]),
                                       _mm_tn(kd[:, hd:], v_new[:, hd:])], axis=1)
                st_ref[si] = state * g_last + upd
                zz = z_ref[blk]
                halves = []
                for hf in range(2):
                    oh = o[:, hf * hd:(hf + 1) * hd]
                    halves.append(oh * lax.rsqrt(jnp.mean(oh * oh, axis=-1, keepdims=True) + RMS_EPS)
                                  * nw_ref[...])
                o_ref[blk] = (jnp.concatenate(halves, axis=1) * _silu(zz)).astype(o_ref.dtype)


def _gdn(qkv, z, bdc, bdr, a_log, dt_bias, norm_w):
    bsz, seq, _ = qkv.shape
    ts = GDN_TILE
    gw = GDN_HEADS * HEAD_DIM
    zeros4 = jnp.zeros((GDN_HEADS,), F32)
    al = jnp.concatenate([zeros4, a_log])
    db = jnp.concatenate([zeros4, dt_bias])
    gpc = jnp.zeros((2, LANES), F32).at[0, :2 * GDN_HEADS].set(al).at[1, :2 * GDN_HEADS].set(db)
    gpr = jnp.stack([al, db], axis=1)
    row = lambda i: (0, i, 0)
    const = lambda i: (0, 0)
    n_chains = bsz * (ts // GDN_BLOCK) * (GDN_HEADS // 2)
    tile = pltpu.VMEM((bsz, ts, gw), F32)
    mats = pltpu.VMEM((n_chains, GDN_BLOCK, 2 * GDN_BLOCK), F32)
    return pl.pallas_call(
        _gdn_kernel,
        out_shape=jax.ShapeDtypeStruct((bsz, seq, gw), BF16),
        grid=(seq // ts,),
        in_specs=[pl.BlockSpec((bsz, ts, 3 * gw), row), pl.BlockSpec((bsz, ts, gw), row),
                  pl.BlockSpec((bsz, ts, LANES), row),
                  pl.BlockSpec((bsz, SUBLANES, ts), lambda i: (0, 0, i)),
                  pl.BlockSpec((2, LANES), const), pl.BlockSpec((2 * GDN_HEADS, 2), const),
                  pl.BlockSpec((1, HEAD_DIM), const)],
        out_specs=pl.BlockSpec((bsz, ts, gw), row),
        scratch_shapes=[tile, tile, tile, tile, tile,
                        mats, mats, mats,
                        pltpu.VMEM((bsz * GDN_HEADS // 2, HEAD_DIM, 2 * HEAD_DIM), F32)],
        compiler_params=_params(("arbitrary",)),
        name="gdn",
    )(qkv, z, bdc, bdr, gpc, gpr, norm_w.reshape(1, HEAD_DIM))


def _attn_kernel(q_ref, k_ref, v_ref, nw_ref, o_ref,
                 q4, k4, v4, ktail, vtail, m_s, l_s, acc_s, tmp_s, nat_s):
    tq = q_ref.shape[0]
    blk = ATT_BLOCK
    sub = 4
    nq = tq // sub
    t = pl.program_id(2)
    slot = t % 2
    other = 1 - slot
    qi = lax.broadcasted_iota(jnp.int32, (blk, 2 * blk), 0)
    kj = lax.broadcasted_iota(jnp.int32, (blk, 2 * blk), 1)
    band = (kj >= qi) & (kj <= qi + blk)
    first_lo = jnp.where(t > 0, 0, blk)
    band_first = band & (kj >= first_lo)

    @pl.when(t == 0)
    def _():
        k4[...] = jnp.zeros(k4.shape, F32)
        v4[...] = jnp.zeros(v4.shape, F32)
        ktail[...] = jnp.zeros(ktail.shape, F32)
        vtail[...] = jnp.zeros(vtail.shape, F32)

    for r in range(sub):
        rows = slice(r * nq, (r + 1) * nq)
        src = pl.ds(r, nq, stride=sub)
        q4[rows, :] = q_ref[src, :]
        k4[slot, rows, :] = k_ref[src, :]
        v4[slot, rows, :] = v_ref[src, :]

    def block_stats(q, kcat, vcat, from_prev_tile):
        s = _mm_nt(q, kcat)
        s = jnp.where(band_first if from_prev_tile else band, s, NEG)
        m = jnp.max(s, axis=-1, keepdims=True)
        p = jnp.exp2(s - m)
        l = jnp.sum(p, axis=-1, keepdims=True)
        return m, l, _mm(p, vcat)

    def merge(dst, m, l, o):
        m_old = m_s[dst, :]
        m_new = jnp.maximum(m_old, m)
        w_old = jnp.exp2(m_old - m_new)
        w_cur = jnp.exp2(m - m_new)
        m_s[dst, :] = m_new
        l_s[dst, :] = w_old * l_s[dst, :] + w_cur * l
        acc_s[dst, :] = w_old * acc_s[dst, :] + w_cur * o

    for jb in range(tq // blk):
        cur = slice(jb * blk, (jb + 1) * blk)
        if jb > 0:
            kcat = k_ref[(jb - 1) * blk:(jb + 1) * blk, :]
            vcat = v_ref[(jb - 1) * blk:(jb + 1) * blk, :]
        else:
            kcat = jnp.concatenate([ktail[...], k_ref[cur, :]], axis=0)
            vcat = jnp.concatenate([vtail[...], v_ref[cur, :]], axis=0)
        m, l, o = block_stats(q_ref[cur, :], kcat, vcat, jb == 0)
        tmp_s[0] = jnp.broadcast_to(m, (blk, HEAD_DIM))
        tmp_s[1] = jnp.broadcast_to(l, (blk, HEAD_DIM))
        tmp_s[2] = o
        per = blk // sub
        for r in range(sub):
            dst = slice(r * nq + jb * per, r * nq + (jb + 1) * per)
            src = pl.ds(r, per, stride=sub)
            m_s[dst, :] = tmp_s[0, src, :]
            l_s[dst, :] = tmp_s[1, src, :]
            acc_s[dst, :] = tmp_s[2, src, :]

    for r in range(sub):
        for jb in range(nq // blk):
            base = r * nq + jb * blk
            cur = slice(base, base + blk)
            if jb > 0:
                kcat = k4[slot, base - blk:base + blk, :]
                vcat = v4[slot, base - blk:base + blk, :]
            else:
                last = slice((r + 1) * nq - blk, (r + 1) * nq)
                kcat = jnp.concatenate([k4[other, last, :], k4[slot, cur, :]], axis=0)
                vcat = jnp.concatenate([v4[other, last, :], v4[slot, cur, :]], axis=0)
            m, l, o = block_stats(q4[cur, :], kcat, vcat, jb == 0)
            merge(cur, m, l, o)

    for c in range(16):
        sl = pl.ds((c % sub) * nq + c // sub, blk, stride=sub)
        kcat = jnp.concatenate([k4[other, sl, :], k4[slot, sl, :]], axis=0)
        vcat = jnp.concatenate([v4[other, sl, :], v4[slot, sl, :]], axis=0)
        m, l, o = block_stats(q4[sl, :], kcat, vcat, True)
        merge(sl, m, l, o)

    out = acc_s[...] / l_s[...]
    out = out * lax.rsqrt(jnp.mean(out * out, axis=-1, keepdims=True) + RMS_EPS) * nw_ref[...]
    for r in range(sub):
        nat_s[pl.ds(r, nq, stride=sub), :] = out[r * nq:(r + 1) * nq]
    o_ref[...] = nat_s[...].astype(o_ref.dtype)
    ktail[...] = k_ref[tq - blk:tq, :]
    vtail[...] = v_ref[tq - blk:tq, :]


def _attn(qb, kb, vb, norm_w, bsz, seq):
    t = qb.shape[0]
    tq = ATT_TILE
    spb = seq // tq
    cur = lambda b, h, i: (b * spb + i, h)
    blk = pl.BlockSpec((tq, HEAD_DIM), cur)
    tile = pltpu.VMEM((tq, HEAD_DIM), F32)
    ring = pltpu.VMEM((2, tq, HEAD_DIM), F32)
    tail = pltpu.VMEM((ATT_BLOCK, HEAD_DIM), F32)
    return pl.pallas_call(
        _attn_kernel,
        out_shape=jax.ShapeDtypeStruct((t, ATT_HEADS * HEAD_DIM), BF16),
        grid=(bsz, ATT_HEADS, spb),
        in_specs=[blk, blk, blk, pl.BlockSpec((1, HEAD_DIM), lambda b, h, i: (0, 0))],
        out_specs=blk,
        scratch_shapes=[tile, ring, ring, tail, tail, tile, tile, tile,
                        pltpu.VMEM((3, ATT_BLOCK, HEAD_DIM), F32), tile],
        compiler_params=_params(("parallel", "parallel", "arbitrary")),
        name="attn",
    )(qb, kb, vb, norm_w.reshape(1, HEAD_DIM))


def _layer_norm(y, g, b):
    mu = jnp.mean(y, axis=-1, keepdims=True)
    yc = y - mu
    var = jnp.mean(yc * yc, axis=-1, keepdims=True)
    return yc * lax.rsqrt(var + LN_EPS) * g + b


def _outproj_kernel(oa_ref, ob_ref, x_ref, mod_ref, wo_ref, g_ref, b_ref,
                    wrh_ref, wrl_ref, br_ref,
                    x1_ref, h2_ref, ri_ref, rg_ref, cnt_ref, run_s, mix_s, hi_s, lo_s):
    @pl.when(pl.program_id(0) == 0)
    def _():
        run_s[...] = jnp.zeros(run_s.shape, F32)

    tm = x_ref.shape[0]
    gate1 = mod_ref[0, 2:3, :]
    shift2 = mod_ref[0, 3:4, :]
    scale2 = mod_ref[0, 4:5, :]
    n_parts = OUTPROJ_PARTS
    part = tm // n_parts
    grp_rows = 2 * SUBLANES

    def project(p):
        rows = slice(p * part, (p + 1) * part)
        mix_s[rows, :] = jnp.dot(jnp.concatenate([oa_ref[rows, :], ob_ref[rows, :]], axis=1), wo_ref[...],
                                 preferred_element_type=F32)

    def normalise(p):
        for r0 in range(p * part, (p + 1) * part, grp_rows):
            rows = slice(r0, r0 + grp_rows)
            x1 = _layer_norm(DEEPNORM_ALPHA * x_ref[rows, :] + (1.0 + gate1) * mix_s[rows, :],
                             g_ref[...], b_ref[...])
            x1_ref[rows, :] = x1
            h2 = x1 * (1.0 + scale2) + shift2
            h2_ref[rows] = h2.reshape(grp_rows, SUBLANES, LANES)
            hi = _bf(h2)
            hi_s[rows, :] = hi
            lo_s[rows, :] = _bf(h2 - hi.astype(F32))

    nt = (((1,), (1,)), ((), ()))

    def route(p):
        rows = slice(p * part, (p + 1) * part)
        hi = hi_s[rows, :]
        by_token = (jnp.dot(hi, wrh_ref[...], preferred_element_type=F32)
                    + jnp.dot(hi, wrl_ref[...], preferred_element_type=F32)
                    + jnp.dot(lo_s[rows, :], wrh_ref[...], preferred_element_type=F32))
        return by_token.T[:ROUTER_ROWS]

    project(0)
    parts = []
    for p in range(n_parts):
        if p + 1 < n_parts:
            project(p + 1)
        normalise(p)
        parts.append(route(p))
    logits = jnp.concatenate(parts, axis=1) + br_ref[:, 0:1]
    nr = logits.shape[0]
    row = lax.broadcasted_iota(jnp.int32, (nr, tm), 0).astype(F32)
    lg = jnp.where(row < N_GROUPS, logits, NEG)
    mg = jnp.max(lg, axis=0, keepdims=True)
    grp = jnp.min(jnp.where(lg == mg, row, float(nr)), axis=0, keepdims=True)
    gate_grp = 1.0 / jnp.sum(jnp.exp(lg - mg), axis=0, keepdims=True)
    first_row = N_GROUPS + EXPERTS_PER_GROUP * grp
    sel = (row >= first_row) & (row < first_row + EXPERTS_PER_GROUP)
    le = jnp.where(sel, logits, NEG)
    v1 = jnp.max(le, axis=0, keepdims=True)
    i1 = jnp.min(jnp.where(le == v1, row, float(nr)), axis=0, keepdims=True)
    le2 = jnp.where(row == i1, NEG, le)
    v2 = jnp.max(le2, axis=0, keepdims=True)
    i2 = jnp.min(jnp.where(le2 == v2, row, float(nr)), axis=0, keepdims=True)
    e21 = jnp.exp(v2 - v1)
    g1 = gate_grp / (1.0 + e21)
    g2 = gate_grp * e21 / (1.0 + e21)

    oh1 = row == i1
    oh2 = row == i2
    onehot = jnp.where(oh1 | oh2, 1.0, 0.0).astype(F32)
    ti = lax.broadcasted_iota(jnp.int32, (tm, tm), 0)
    tj = lax.broadcasted_iota(jnp.int32, (tm, tm), 1)
    earlier = jnp.where(ti < tj, 1.0, 0.0).astype(F32)
    tot = _mm(onehot, earlier) + run_s[:, 0:1]
    r1 = jnp.sum(jnp.where(oh1, tot, 0.0), axis=0, keepdims=True)
    r2 = jnp.sum(jnp.where(oh2, tot, 0.0), axis=0, keepdims=True)
    run_s[...] = run_s[...] + jnp.sum(onehot, axis=1, keepdims=True)
    cnt_ref[...] = run_s[...]

    sub_i = lax.broadcasted_iota(jnp.int32, (SUBLANES, tm), 0)
    ri = jnp.where(sub_i == 0, i1 - N_GROUPS, 0.0)
    ri = jnp.where(sub_i == 1, i2 - N_GROUPS, ri)
    ri = jnp.where(sub_i == 2, r1, ri)
    ri = jnp.where(sub_i == 3, r2, ri)
    ri_ref[...] = ri.astype(jnp.int32)

    gates = jnp.where(sub_i == 0, g1, jnp.where(sub_i == 1, g2, 0.0))
    pick = jnp.where(lax.broadcasted_iota(jnp.int32, (SUBLANES, LANES), 0)
                     == lax.broadcasted_iota(jnp.int32, (SUBLANES, LANES), 1), 1.0, 0.0).astype(BF16)
    tn = (((0,), (0,)), ((), ()))
    rg_ref[...] = sum(lax.dot_general(term, pick, tn, preferred_element_type=F32)
                      for term in _split3(gates))


def _outproj(oa, ob, xf, mod, w_o, ln_g, ln_b, w_rg, b_rg, w_re, b_re, seq):
    t, d = xf.shape
    tm = PROJ_TILE
    gw = oa.shape[1]
    wo = _bf(w_o)
    nr = ROUTER_ROWS
    wr = jnp.zeros((d, LANES), F32).at[:, :N_GROUPS].set(w_rg).at[:, N_GROUPS:N_GROUPS + N_EXPERTS].set(w_re)
    wrh = _bf(wr)
    wrl = _bf(wr - wrh.astype(F32))
    br = jnp.zeros((nr,), F32).at[:N_GROUPS].set(b_rg).at[N_GROUPS:N_GROUPS + N_EXPERTS].set(b_re)
    br = jnp.broadcast_to(br[:, None], (nr, LANES))
    spb = seq // tm
    row = lambda i: (i, 0)
    const = lambda i: (0, 0)
    return pl.pallas_call(
        _outproj_kernel,
        out_shape=(jax.ShapeDtypeStruct((t, d), F32), jax.ShapeDtypeStruct((t, d // LANES, LANES), F32),
                   jax.ShapeDtypeStruct((SUBLANES, t), jnp.int32), jax.ShapeDtypeStruct((t, LANES), F32),
                   jax.ShapeDtypeStruct((nr, LANES), F32)),
        grid=(t // tm,),
        in_specs=[pl.BlockSpec((tm, gw), row), pl.BlockSpec((tm, gw), row), pl.BlockSpec((tm, d), row),
                  pl.BlockSpec((1, 6, d), lambda i: (i // spb, 0, 0)),
                  pl.BlockSpec((2 * gw, d), const),
                  pl.BlockSpec((1, d), const), pl.BlockSpec((1, d), const),
                  pl.BlockSpec((d, LANES), const), pl.BlockSpec((d, LANES), const),
                  pl.BlockSpec((nr, LANES), const)],
        out_specs=(pl.BlockSpec((tm, d), row), pl.BlockSpec((tm, d // LANES, LANES), lambda i: (i, 0, 0)),
                   pl.BlockSpec((SUBLANES, tm), lambda i: (0, i)), pl.BlockSpec((tm, LANES), row),
                   pl.BlockSpec((nr, LANES), const)),
        scratch_shapes=[pltpu.VMEM((nr, LANES), F32), pltpu.VMEM((tm, d), F32),
                        pltpu.VMEM((tm, d), BF16), pltpu.VMEM((tm, d), BF16)],
        compiler_params=_params(("arbitrary",)),
        name="outproj",
    )(oa, ob, xf, mod, wo, ln_g.reshape(1, d), ln_b.reshape(1, d), wrh, wrl, br)


def _dispatch_kernel(d1_ref, d2_ref, h_ref, xs_ref, sem):
    tm = h_ref.shape[0]
    i = pl.program_id(0)

    def row_copy(tk, dest):
        return pltpu.make_async_copy(h_ref.at[tk], xs_ref.at[dest], sem)

    def issue(g, carry):
        for u in range(DMA_UNROLL):
            tk = g * DMA_UNROLL + u
            tok = i * tm + tk
            row_copy(tk, d1_ref[tok]).start(priority=0)
            row_copy(tk, d2_ref[tok]).start(priority=1)
        return carry

    lax.fori_loop(0, tm // DMA_UNROLL, issue, 0)

    tile_copy = pltpu.make_async_copy(h_ref, xs_ref.at[pl.ds(0, tm)], sem)
    tile_copy.wait()
    tile_copy.wait()


def _dispatch(h2, d1, d2):
    t, sub, lanes = h2.shape
    tm = DISPATCH_TILE
    return pl.pallas_call(
        _dispatch_kernel,
        out_shape=jax.ShapeDtypeStruct((2 * t, sub, lanes), F32),
        grid_spec=pltpu.PrefetchScalarGridSpec(
            num_scalar_prefetch=2,
            grid=(t // tm,),
            in_specs=[pl.BlockSpec((tm, sub, lanes), lambda i, *_: (i, 0, 0))],
            out_specs=pl.BlockSpec(memory_space=pl.ANY),
            scratch_shapes=[pltpu.SemaphoreType.DMA]),
        compiler_params=_params(("arbitrary",)),
        name="dispatch",
    )(d1, d2, h2)


def _experts_kernel(wb_ref, we_ref, lo_ref, hi_ref, nw_ref, first_ref, ring_ref, next_ref,
                    xs_ref, wg_hbm, wu_hbm, wd_hbm, ys_ref,
                    wg_f, wu_f, wd_f, wg_s, wu_s, wd_s, sem):
    w = pl.program_id(0)

    def fetch(expert, slot):
        return (pltpu.make_async_copy(wg_hbm.at[expert], wg_f.at[slot], sem.at[slot]),
                pltpu.make_async_copy(wu_hbm.at[expert], wu_f.at[slot], sem.at[slot]),
                pltpu.make_async_copy(wd_hbm.at[expert], wd_f.at[slot], sem.at[slot]))

    @pl.when(w == 0)
    def _():
        for cp in fetch(we_ref[0], 0):
            cp.start()

    @pl.when((first_ref[w] == 1) & (w < nw_ref[0]))
    def _():
        slot = ring_ref[w]
        for cp in fetch(we_ref[w], slot):
            cp.wait()

        @pl.when(next_ref[w] >= 0)
        def _():
            for cp in fetch(next_ref[w], 1 - slot):
                cp.start()

        wg_s[...] = _bf(wg_f[slot])
        wu_s[...] = _bf(wu_f[slot])
        wd_s[...] = _bf(wd_f[slot])

    @pl.when(w < nw_ref[0])
    def _():
        bm, sub, lanes = xs_ref.shape
        x = _bf(xs_ref[...].reshape(bm, sub * lanes))
        hid = (_silu(jnp.dot(x, wg_s[...], preferred_element_type=F32))
               * jnp.dot(x, wu_s[...], preferred_element_type=F32))
        y = jnp.dot(_bf(hid), wd_s[...], preferred_element_type=F32).reshape(bm, sub, lanes)
        row = lax.broadcasted_iota(jnp.int32, (bm, 1, 1), 0)
        mine = (row >= lo_ref[w]) & (row < hi_ref[w])

        @pl.when(lo_ref[w] == 0)
        def _():
            ys_ref[...] = jnp.where(mine, y, 0.0)

        @pl.when(lo_ref[w] > 0)
        def _():
            ys_ref[...] = jnp.where(mine, y, ys_ref[...])


def _experts(xs, item_block, item_expert, item_lo, item_hi, n_items, w_gate, w_up, w_down):
    n_slots, sub, lanes = xs.shape
    d = sub * lanes
    ff = w_gate.shape[2]
    bm = EXPERT_BLOCK
    n = item_block.shape[0]
    idx = jnp.arange(n, dtype=jnp.int32)
    first = jnp.concatenate([jnp.ones((1,), jnp.int32),
                             (item_expert[1:] != item_expert[:-1]).astype(jnp.int32)])
    ring = (jnp.cumsum(first) - 1) % 2
    next_first = lax.cummin(jnp.where(first == 1, idx, n), reverse=True)
    next_first = jnp.concatenate([next_first[1:], jnp.full((1,), n, jnp.int32)])
    nxt = jnp.where(next_first < n, item_expert[jnp.minimum(next_first, n - 1)], -1).astype(jnp.int32)
    slot = lambda w, *_: (_[0][w], 0, 0)
    return pl.pallas_call(
        _experts_kernel,
        out_shape=jax.ShapeDtypeStruct((n_slots, sub, lanes), F32),
        grid_spec=pltpu.PrefetchScalarGridSpec(
            num_scalar_prefetch=8,
            grid=(n,),
            in_specs=[pl.BlockSpec((bm, sub, lanes), slot),
                      pl.BlockSpec(memory_space=pl.ANY), pl.BlockSpec(memory_space=pl.ANY),
                      pl.BlockSpec(memory_space=pl.ANY)],
            out_specs=pl.BlockSpec((bm, sub, lanes), slot),
            scratch_shapes=[pltpu.VMEM((2, d, ff), F32), pltpu.VMEM((2, d, ff), F32),
                            pltpu.VMEM((2, ff, d), F32),
                            pltpu.VMEM((d, ff), BF16), pltpu.VMEM((d, ff), BF16),
                            pltpu.VMEM((ff, d), BF16), pltpu.SemaphoreType.DMA((2,))]),
        compiler_params=_params(("arbitrary",)),
        name="experts",
    )(item_block, item_expert, item_lo, item_hi, n_items, first, ring.astype(jnp.int32), nxt,
      xs, w_gate, w_up, w_down)


def _combine_kernel(d1_ref, d2_ref, ys_ref, rg_ref, x1_ref, mod_ref, g_ref, b_ref, o_ref, ya, yb, sem):
    tm = x1_ref.shape[0]
    i = pl.program_id(0)
    n = pl.num_programs(0)

    def row_copy(dest, buf, slot, tk):
        return pltpu.make_async_copy(ys_ref.at[dest], buf.at[slot, tk], sem.at[slot])

    slot = i % 2
    gate2 = mod_ref[0, 5:6, :]

    def issue_group(step, dst_slot, g):
        for u in range(DMA_UNROLL):
            tk = g * DMA_UNROLL + u
            tok = step * tm + tk
            row_copy(d1_ref[tok], ya, dst_slot, tk).start(priority=0)
            row_copy(d2_ref[tok], yb, dst_slot, tk).start(priority=1)

    def gather_tile(step, dst_slot):
        def issue(g, carry):
            issue_group(step, dst_slot, g)
            return carry

        lax.fori_loop(0, tm // DMA_UNROLL, issue, 0)

    @pl.when(i == 0)
    def _():
        gather_tile(0, 0)

    @pl.when(i + 1 < n)
    def _():
        gather_tile(i + 1, 1 - slot)

    pltpu.make_async_copy(ys_ref.at[pl.ds(0, tm)], ya.at[slot], sem.at[slot]).wait()
    pltpu.make_async_copy(ys_ref.at[pl.ds(0, tm)], yb.at[slot], sem.at[slot]).wait()

    rg = rg_ref[...]
    d = x1_ref.shape[1]
    y = rg[:, 0:1] * ya[slot].reshape(tm, d) + rg[:, 1:2] * yb[slot].reshape(tm, d)
    o_ref[...] = _layer_norm(DEEPNORM_ALPHA * x1_ref[...] + (1.0 + gate2) * y, g_ref[...], b_ref[...])


def _combine(ys, d1, d2, rg, x1, mod, ln_g, ln_b, seq):
    t, d = x1.shape
    tm = ROW_TILE
    spb = seq // tm
    row = lambda i, *_: (i, 0)
    const = lambda i, *_: (0, 0)
    buf = pltpu.VMEM((2, tm) + ys.shape[1:], F32)
    return pl.pallas_call(
        _combine_kernel,
        out_shape=jax.ShapeDtypeStruct((t, d), F32),
        grid_spec=pltpu.PrefetchScalarGridSpec(
            num_scalar_prefetch=2,
            grid=(t // tm,),
            in_specs=[pl.BlockSpec(memory_space=pl.ANY),
                      pl.BlockSpec((tm, LANES), row), pl.BlockSpec((tm, d), row),
                      pl.BlockSpec((1, 6, d), lambda i, *_: (i // spb, 0, 0)),
                      pl.BlockSpec((1, d), const), pl.BlockSpec((1, d), const)],
            out_specs=pl.BlockSpec((tm, d), row),
            scratch_shapes=[buf, buf, pltpu.SemaphoreType.DMA((2,))]),
        compiler_params=_params(("arbitrary",)),
        name="combine",
    )(d1, d2, ys, rg, x1, mod, ln_g.reshape(1, d), ln_b.reshape(1, d))


def _layer(x, c, positions, w_ada, b_ada, w_in, conv_w, a_log, dt_bias, gdn_norm_w, attn_norm_w,
           w_o, ln1_g, ln1_b, w_rg, b_rg, w_re, b_re, w_gate, w_up, w_down, ln2_g, ln2_b):
    bsz, seq, d = x.shape
    t = bsz * seq
    xf = x.reshape(t, d)
    mod = _ada(c, w_ada, b_ada)
    qkv, z, bdc, bdr, qb, kb, vb = _inproj(xf, mod, positions.reshape(t, 1), w_in, conv_w, seq)
    gw = GDN_HEADS * HEAD_DIM
    oa = _gdn(qkv.reshape(bsz, seq, 3 * gw), z.reshape(bsz, seq, gw), bdc.reshape(bsz, seq, LANES), bdr,
              a_log, dt_bias, gdn_norm_w).reshape(t, gw)
    ob = _attn(qb, kb, vb, attn_norm_w, bsz, seq)
    x1, h2, ri, rg, cnt = _outproj(oa, ob, xf, mod, w_o, ln1_g, ln1_b, w_rg, b_rg, w_re, b_re, seq)

    bm = EXPERT_BLOCK
    counts = cnt[N_GROUPS:N_GROUPS + N_EXPERTS, 0].astype(jnp.int32)
    seg_end = jnp.cumsum(counts)
    seg_start = seg_end - counts
    first_blk = seg_start // bm
    n_per = jnp.where(counts > 0, (seg_end - 1) // bm - first_blk + 1, 0)
    item_end = jnp.cumsum(n_per)
    n_items = item_end[-1:]
    max_items = (2 * t) // bm + N_EXPERTS - 1
    w_idx = jnp.minimum(jnp.arange(max_items, dtype=jnp.int32), n_items[0] - 1)
    item_expert = jnp.minimum(jnp.sum(item_end[None, :] <= w_idx[:, None], axis=1), N_EXPERTS - 1).astype(jnp.int32)
    item_block = first_blk[item_expert] + w_idx - (item_end - n_per)[item_expert]
    item_lo = jnp.maximum(seg_start[item_expert] - item_block * bm, 0)
    item_hi = jnp.minimum(seg_end[item_expert] - item_block * bm, bm)
    expert_ids = jnp.arange(N_EXPERTS, dtype=jnp.int32)[:, None]

    def seg_of(e):
        return jnp.sum(jnp.where(e[None, :] == expert_ids, seg_start[:, None], 0), axis=0)

    d1 = seg_of(ri[0]) + ri[2]
    d2 = seg_of(ri[1]) + ri[3]

    xs = _dispatch(h2, d1, d2)
    ys = _experts(xs, item_block, item_expert, item_lo, item_hi, n_items, w_gate, w_up, w_down)
    out = _combine(ys, d1, d2, rg, x1, mod, ln2_g, ln2_b, seq)
    return out.reshape(bsz, seq, d)


def kernel(x, c, positions, w_ada, b_ada, w_in, conv_w, a_log, dt_bias, gdn_norm_w, attn_norm_w, w_o, ln1_g, ln1_b, w_router_group, b_router_group, w_router_expert, b_router_expert, w_gate, w_up, w_down, ln2_g, ln2_b):
    assert w_ada.shape[0] == DEPTH
    return _layer(x, c, positions, w_ada[0], b_ada[0], w_in[0], conv_w[0], a_log[0], dt_bias[0],
                  gdn_norm_w[0], attn_norm_w[0], w_o[0], ln1_g[0], ln1_b[0],
                  w_router_group[0], b_router_group[0], w_router_expert[0], b_router_expert[0],
                  w_gate[0], w_up[0], w_down[0], ln2_g[0], ln2_b[0])
```

```python
import functools
import math

import jax
import jax.numpy as jnp
from jax import lax
from jax.experimental import pallas as pl
from jax.experimental.pallas import tpu as pltpu

F32 = jnp.float32
BF16 = jnp.bfloat16
LOG2E = math.log2(math.e)

GDN_HEADS = 4
ATT_HEADS = 4
HEAD_DIM = 128
CONV_WIDTH = 4
DILATED_PATTERNS = ((128, 1), (512, 4), (2048, 16))
ROPE_THETA = 500000.0
ROPE_DIMS = HEAD_DIM // 4
N_GROUPS = 4
EXPERTS_PER_GROUP = 8
N_EXPERTS = N_GROUPS * EXPERTS_PER_GROUP
ROUTER_ROWS = -(-(N_GROUPS + N_EXPERTS) // 8) * 8
DEPTH = 1
DEEPNORM_ALPHA = (2.0 * DEPTH) ** 0.25
LN_EPS = 1e-5
RMS_EPS = 1e-6

LANES = 128
SUBLANES = 8
VMEM_LIMIT = 48 * 1024 * 1024

GDN_BLOCK = 128
GDN_TILE = 256
ATT_BLOCK = 128
ATT_TILE = 2048
ADA_COLS = 1024
PROJ_TILE = 512
ROW_TILE = 512
DISPATCH_TILE = 2048
EXPERT_BLOCK = 256
OUTPROJ_PARTS = 2
DMA_UNROLL = 8
NEG = -1e30


def _bf(x):
    return x.astype(BF16)


def _mm(a, b):
    return jnp.dot(_bf(a), _bf(b), preferred_element_type=F32)


def _mm_nt(a, b):
    return lax.dot_general(_bf(a), _bf(b), (((1,), (1,)), ((), ())), preferred_element_type=F32)


def _mm_tn(a, b):
    return lax.dot_general(_bf(a), _bf(b), (((0,), (0,)), ((), ())), preferred_element_type=F32)


def _split3(x):
    x1 = _bf(x)
    r1 = x - x1.astype(F32)
    x2 = _bf(r1)
    return x1, x2, _bf(r1 - x2.astype(F32))


def _sigmoid(x):
    return 1.0 / (1.0 + jnp.exp(-x))


def _silu(x):
    return x * _sigmoid(x)


def _softplus(x):
    return jnp.maximum(x, 0.0) + jnp.log(1.0 + jnp.exp(-jnp.abs(x)))


def _params(sem):
    return pltpu.CompilerParams(dimension_semantics=sem, vmem_limit_bytes=VMEM_LIMIT)


def _ada_kernel(ct_ref, w_ref, b_ref, o_ref, *, bsz):
    sc = _silu(ct_ref[...])
    w = w_ref[...]
    rows = [jnp.sum(w * sc[:, b:b + 1], axis=0, keepdims=True) for b in range(bsz)]
    rows.append(jnp.zeros((o_ref.shape[0] - bsz, w.shape[1]), F32))
    o_ref[...] = jnp.concatenate(rows, axis=0) + b_ref[...]


def _ada(c, w_ada, b_ada):
    bsz, d = c.shape
    n = w_ada.shape[1]
    tn = ADA_COLS
    assert bsz <= SUBLANES
    ct = jnp.zeros((d, LANES), F32).at[:, :bsz].set(c.T)
    out = pl.pallas_call(
        functools.partial(_ada_kernel, bsz=bsz),
        out_shape=jax.ShapeDtypeStruct((SUBLANES, n), F32),
        grid=(n // tn,),
        in_specs=[pl.BlockSpec((d, LANES), lambda j: (0, 0)),
                  pl.BlockSpec((d, tn), lambda j: (0, j)),
                  pl.BlockSpec((1, tn), lambda j: (0, j))],
        out_specs=pl.BlockSpec((SUBLANES, tn), lambda j: (0, j)),
        compiler_params=_params(("parallel",)),
        name="ada",
    )(ct, w_ada, b_ada.reshape(1, n))
    return out[:bsz].reshape(bsz, 6, d)


def _inproj_kernel(x_ref, mod_ref, pos_ref, invf_ref, convw_ref, wqkv_ref, wz_ref, wbd_ref, wbdt_ref,
                   wq_ref, wk_ref, wv_ref,
                   qkv_ref, z_ref, bdc_ref, bdr_ref, qb_ref, kb_ref, vb_ref, cbuf, *, steps_per_seq):
    tm = x_ref.shape[0]
    halo = SUBLANES
    n_slabs = cbuf.shape[0]

    @pl.when(pl.program_id(0) % steps_per_seq == 0)
    def _():
        cbuf[:, 0:halo, :] = jnp.zeros((n_slabs, halo, LANES), F32)

    half = ROPE_DIMS // 2
    groups = LANES // half
    lane = lax.broadcasted_iota(jnp.int32, (1, LANES), 1)
    first = lane < half
    rot = lane < ROPE_DIMS
    ang = pos_ref[0].astype(F32) * invf_ref[...]
    cos_c = jnp.cos(ang)
    sin_c = jnp.sin(ang)
    cos_parts, sin_parts = [], []
    for j in range(groups):
        lo_sh = (LANES - half * j) % LANES
        hi_sh = (LANES - half * j + half) % LANES
        c_lo = pltpu.roll(cos_c, lo_sh, 1) if lo_sh else cos_c
        c_hi = pltpu.roll(cos_c, hi_sh, 1) if hi_sh else cos_c
        s_lo = pltpu.roll(sin_c, lo_sh, 1) if lo_sh else sin_c
        s_hi = pltpu.roll(sin_c, hi_sh, 1) if hi_sh else sin_c
        cos_parts.append(jnp.where(first, c_lo, jnp.where(rot, c_hi, 1.0)))
        sin_parts.append(jnp.where(first, -s_lo, jnp.where(rot, s_hi, 0.0)))
    cosv = jnp.concatenate(cos_parts, axis=0)
    sin_signed = jnp.concatenate(sin_parts, axis=0)

    def rope(y):
        outs = []
        for hh in range(ATT_HEADS):
            yh = y[:, hh * HEAD_DIM:(hh + 1) * HEAD_DIM]
            partner = jnp.where(first, pltpu.roll(yh, LANES - half, 1), pltpu.roll(yh, half, 1))
            outs.append(yh * cosv + partner * sin_signed)
        return jnp.concatenate(outs, axis=1)

    def conv_slice(s):
        sl = slice(s * HEAD_DIM, (s + 1) * HEAD_DIM)
        off = halo - (CONV_WIDTH - 1)
        acc = convw_ref[0:1, sl] * cbuf[s, off:off + tm, :]
        for j in range(1, CONV_WIDTH):
            acc = acc + convw_ref[j:j + 1, sl] * cbuf[s, off + j:off + j + tm, :]
        cbuf[s, 0:halo, :] = cbuf[s, tm:tm + halo, :]
        y = _silu(acc)
        if s < 2 * GDN_HEADS:
            y = y * lax.rsqrt(jnp.sum(y * y, axis=-1, keepdims=True) + RMS_EPS)
        if s < GDN_HEADS:
            y = y * (HEAD_DIM ** -0.5)
        qkv_ref[:, sl] = y

    shift = mod_ref[0, 0:1, :]
    scale = mod_ref[0, 1:2, :]
    h = _bf(x_ref[...] * (1.0 + scale) + shift)
    chunk = 2 * HEAD_DIM
    n_chunks = n_slabs // 2

    def project_chunk(c):
        pre = jnp.dot(h, wqkv_ref[:, c * chunk:(c + 1) * chunk], preferred_element_type=F32)
        cbuf[2 * c, halo:halo + tm, :] = pre[:, :HEAD_DIM]
        cbuf[2 * c + 1, halo:halo + tm, :] = pre[:, HEAD_DIM:]

    def conv_chunk(c):
        conv_slice(2 * c)
        conv_slice(2 * c + 1)

    project_chunk(0)
    for c in range(1, n_chunks):
        project_chunk(c)
        conv_chunk(c - 1)
    qb = jnp.dot(h, wq_ref[...], preferred_element_type=F32)
    conv_chunk(n_chunks - 1)
    kb = jnp.dot(h, wk_ref[...], preferred_element_type=F32)
    qb_ref[...] = rope(qb) * (HEAD_DIM ** -0.5 * LOG2E)
    vb_ref[...] = jnp.dot(h, wv_ref[...], preferred_element_type=F32)
    kb_ref[...] = rope(kb)
    z_ref[...] = jnp.dot(h, wz_ref[...], preferred_element_type=F32)
    bdc_ref[...] = jnp.dot(h, wbd_ref[...], preferred_element_type=F32)
    bdr_ref[0] = lax.dot_general(wbdt_ref[...], h, (((1,), (1,)), ((), ())),
                                 preferred_element_type=F32)


def _inproj(xf, mod, pos, w_in, conv_w, seq):
    t, d = xf.shape
    tm = PROJ_TILE
    gw = GDN_HEADS * HEAD_DIM
    aw = ATT_HEADS * HEAD_DIM
    o0 = 3 * gw
    o1 = o0 + gw
    o2 = o1 + 2 * GDN_HEADS
    wb = _bf(w_in)
    wqkv, wz = wb[:, :o0], wb[:, o0:o1]
    wbd_n = wb[:, o1:o2]
    wbd = jnp.zeros((d, LANES), BF16).at[:, :2 * GDN_HEADS].set(wbd_n)
    wbdt = wbd_n.T
    wq, wk, wv = wb[:, o2:o2 + aw], wb[:, o2 + aw:o2 + 2 * aw], wb[:, o2 + 2 * aw:o2 + 3 * aw]
    half = ROPE_DIMS // 2
    groups = LANES // half
    inv_freq = ROPE_THETA ** (-jnp.arange(half, dtype=F32) * 2.0 / ROPE_DIMS)
    invf = jnp.tile(inv_freq, groups).reshape(1, LANES)
    pos = jnp.repeat(pos.reshape(t // tm, groups, tm // groups).transpose(0, 2, 1), half, axis=2)
    spb = seq // tm
    row = lambda i: (i, 0)
    const = lambda i: (0, 0)
    return pl.pallas_call(
        functools.partial(_inproj_kernel, steps_per_seq=spb),
        out_shape=(jax.ShapeDtypeStruct((t, o0), F32), jax.ShapeDtypeStruct((t, gw), F32),
                   jax.ShapeDtypeStruct((t, LANES), F32), jax.ShapeDtypeStruct((t // seq, SUBLANES, seq), F32),
                   jax.ShapeDtypeStruct((t, aw), F32), jax.ShapeDtypeStruct((t, aw), F32),
                   jax.ShapeDtypeStruct((t, aw), F32)),
        grid=(t // tm,),
        in_specs=[pl.BlockSpec((tm, d), row),
                  pl.BlockSpec((1, 6, d), lambda i: (i // spb, 0, 0)),
                  pl.BlockSpec((1, tm // groups, LANES), lambda i: (i, 0, 0)),
                  pl.BlockSpec((1, LANES), const), pl.BlockSpec((CONV_WIDTH, o0), const),
                  pl.BlockSpec((d, o0), const), pl.BlockSpec((d, gw), const),
                  pl.BlockSpec((d, LANES), const), pl.BlockSpec((2 * GDN_HEADS, d), const),
                  pl.BlockSpec((d, aw), const), pl.BlockSpec((d, aw), const),
                  pl.BlockSpec((d, aw), const)],
        out_specs=(pl.BlockSpec((tm, o0), row), pl.BlockSpec((tm, gw), row),
                   pl.BlockSpec((tm, LANES), row),
                   pl.BlockSpec((1, SUBLANES, tm), lambda i: (i // spb, 0, i % spb)),
                   pl.BlockSpec((tm, aw), row), pl.BlockSpec((tm, aw), row),
                   pl.BlockSpec((tm, aw), row)),
        scratch_shapes=[pltpu.VMEM((o0 // HEAD_DIM, tm + 2 * SUBLANES, HEAD_DIM), F32)],
        compiler_params=_params(("arbitrary",)),
        name="inproj",
    )(xf, mod, pos, invf, conv_w, wqkv, wz, wbd, wbdt, wq, wk, wv)


def _gdn_kernel(qkv_ref, z_ref, bdc_ref, bdr_ref, gpc_ref, gpr_ref, nw_ref, o_ref,
                u_s, w_s, qd_s, kd_s, qk_s, a_s, x_s, y_s, st_ref):
    nbat, ts = qkv_ref.shape[0], qkv_ref.shape[1]
    nb = ts // GDN_BLOCK
    gw = GDN_HEADS * HEAD_DIM

    @pl.when(pl.program_id(0) == 0)
    def _():
        st_ref[...] = jnp.zeros(st_ref.shape, F32)

    blk_n = GDN_BLOCK
    hd = HEAD_DIM
    ti = lax.broadcasted_iota(jnp.int32, (ts, ts), 0)
    tj = lax.broadcasted_iota(jnp.int32, (ts, ts), 1)
    same = (ti // blk_n) == (tj // blk_n)
    m_low = jnp.where(same & (tj <= ti), 1.0, 0.0).astype(BF16)
    m_up = jnp.where(same & (ti <= tj), 1.0, 0.0).astype(BF16)

    split3 = _split3

    beta_c = [_sigmoid(bdc_ref[bb]) for bb in range(nbat)]
    g_c = jnp.concatenate([-jnp.exp(gpc_ref[0:1, :]) * _softplus(bdc_ref[bb] + gpc_ref[1:2, :])
                           for bb in range(nbat)], axis=1)
    g_r = jnp.concatenate([-jnp.exp(gpr_ref[:, 0:1]) * _softplus(bdr_ref[bb] + gpr_ref[:, 1:2])
                           for bb in range(nbat)], axis=0)
    gc_c = sum(jnp.dot(m_low, part, preferred_element_type=F32) for part in split3(g_c))
    gc_r = sum(jnp.dot(part, m_up, preferred_element_type=F32) for part in split3(g_r))

    def cat2(m):
        return jnp.concatenate([m, m], axis=1)

    ii = lax.broadcasted_iota(jnp.int32, (blk_n, blk_n), 0)
    jj = lax.broadcasted_iota(jnp.int32, (blk_n, blk_n), 1)
    lower = cat2(jj <= ii)
    strict = cat2(jj < ii)
    eye = cat2(jnp.where(ii == jj, 1.0, 0.0).astype(F32))
    levels = []
    b = 1
    while b < blk_n:
        levels.append(cat2(((ii // b) == (jj // b) + 1) & (((jj // b) % 2) == 0)))
        b *= 2

    def block_diag(rp):
        n, m = rp.shape[0], rp.shape[1] // 2
        z = jnp.zeros((n, m), rp.dtype)
        return jnp.concatenate([jnp.concatenate([rp[:, :m], z], axis=1),
                                jnp.concatenate([z, rp[:, m:]], axis=1)], axis=0)

    def mm2(lp, rp):
        return jnp.dot(_bf(lp), block_diag(_bf(rp)), preferred_element_type=F32)

    def mm2_nt(lp, rp):
        return lax.dot_general(_bf(lp), block_diag(_bf(rp)), (((1,), (1,)), ((), ())),
                               preferred_element_type=F32)

    chains = [(bb, j, pp) for bb in range(nbat) for j in range(nb) for pp in range(GDN_HEADS // 2)]

    def tile_of(bb, j, pp):
        return bb, slice(j * blk_n, (j + 1) * blk_n), slice(2 * pp * hd, 2 * (pp + 1) * hd)

    def col_pair(arr, rows, col):
        return jnp.concatenate([jnp.broadcast_to(arr[rows, col:col + 1], (blk_n, hd)),
                                jnp.broadcast_to(arr[rows, col + 1:col + 2], (blk_n, hd))], axis=1)

    for c, (bb, j, pp) in enumerate(chains):
        blk = tile_of(bb, j, pp)
        rows, cols = blk[1], blk[2]
        last = slice((j + 1) * blk_n - 1, (j + 1) * blk_n)
        gcol = bb * LANES + GDN_HEADS + 2 * pp
        grow = bb * SUBLANES + GDN_HEADS + 2 * pp
        q = qkv_ref[bb, rows, cols]
        k = qkv_ref[bb, rows, slice(gw + cols.start, gw + cols.stop)]
        v = qkv_ref[bb, rows, slice(2 * gw + cols.start, 2 * gw + cols.stop)]
        beta = col_pair(beta_c[bb], rows, 2 * pp)
        gcc = col_pair(gc_c, rows, gcol)
        gtc = col_pair(gc_c, last, gcol)
        gcr = jnp.concatenate([jnp.broadcast_to(gc_r[grow:grow + 1, rows], (blk_n, hd)),
                               jnp.broadcast_to(gc_r[grow + 1:grow + 2, rows], (blk_n, hd))], axis=1)
        kb = k * beta
        eg = jnp.exp(gcc)
        dm = jnp.where(lower, jnp.exp(gcc - gcr), 0.0)
        a = jnp.where(strict, mm2_nt(kb, k) * dm, 0.0)
        a_s[c] = a
        x_s[c] = eye - jnp.where(levels[0], a, 0.0)
        u_s[blk] = v * beta
        w_s[blk] = kb * eg
        qd_s[blk] = q * eg
        kd_s[blk] = k * jnp.exp(gtc - gcc)
        qk_s[blk] = mm2_nt(q, k) * dm

    for lm in levels[1:]:
        for c in range(len(chains)):
            y_s[c] = mm2(x_s[c], jnp.where(lm, a_s[c], 0.0))
        for c in range(len(chains)):
            xc = x_s[c]
            x_s[c] = xc - mm2(y_s[c], xc)

    for c, (bb, j, pp) in enumerate(chains):
        blk = tile_of(bb, j, pp)
        u, w = _bf(u_s[blk]), _bf(w_s[blk])
        z = jnp.zeros((blk_n, 2 * hd), BF16)
        rhs = jnp.concatenate([jnp.concatenate([u[:, :hd], w[:, :hd], z], axis=1),
                               jnp.concatenate([z, u[:, hd:], w[:, hd:]], axis=1)], axis=0)
        sol = jnp.dot(_bf(x_s[c]), rhs, preferred_element_type=F32)
        u_s[blk] = jnp.concatenate([sol[:, 0:hd], sol[:, 2 * hd:3 * hd]], axis=1)
        w_s[blk] = jnp.concatenate([sol[:, hd:2 * hd], sol[:, 3 * hd:]], axis=1)

    for j in range(nb):
        for bb in range(nbat):
            for pp in range(GDN_HEADS // 2):
                blk = tile_of(bb, j, pp)
                si = bb * (GDN_HEADS // 2) + pp
                last = slice((j + 1) * blk_n - 1, (j + 1) * blk_n)
                gcol = bb * LANES + GDN_HEADS + 2 * pp
                state = st_ref[si]
                proj = mm2(jnp.concatenate([w_s[blk], qd_s[blk]], axis=0), state)
                v_new = u_s[blk] - proj[:blk_n]
                o = proj[blk_n:] + mm2(qk_s[blk], v_new)
                g_last = jnp.exp(jnp.concatenate(
                    [jnp.broadcast_to(gc_c[last, gcol:gcol + 1], (1, hd)),
                     jnp.broadcast_to(gc_c[last, gcol + 1:gcol + 2], (1, hd))], axis=1))
                kd = kd_s[blk]
                upd = jnp.concatenate([_mm_tn(kd[:, :hd], v_new[:, :hd]),
                                       _mm_tn(kd[:, hd:], v_new[:, hd:])], axis=1)
                st_ref[si] = state * g_last + upd
                zz = z_ref[blk]
                halves = []
                for hf in range(2):
                    oh = o[:, hf * hd:(hf + 1) * hd]
                    halves.append(oh * lax.rsqrt(jnp.mean(oh * oh, axis=-1, keepdims=True) + RMS_EPS)
                                  * nw_ref[...])
                o_ref[blk] = (jnp.concatenate(halves, axis=1) * _silu(zz)).astype(o_ref.dtype)


def _gdn(qkv, z, bdc, bdr, a_log, dt_bias, norm_w):
    bsz, seq, _ = qkv.shape
    ts = GDN_TILE
    gw = GDN_HEADS * HEAD_DIM
    zeros4 = jnp.zeros((GDN_HEADS,), F32)
    al = jnp.concatenate([zeros4, a_log])
    db = jnp.concatenate([zeros4, dt_bias])
    gpc = jnp.zeros((2, LANES), F32).at[0, :2 * GDN_HEADS].set(al).at[1, :2 * GDN_HEADS].set(db)
    gpr = jnp.stack([al, db], axis=1)
    row = lambda i: (0, i, 0)
    const = lambda i: (0, 0)
    n_chains = bsz * (ts // GDN_BLOCK) * (GDN_HEADS // 2)
    tile = pltpu.VMEM((bsz, ts, gw), F32)
    mats = pltpu.VMEM((n_chains, GDN_BLOCK, 2 * GDN_BLOCK), F32)
    return pl.pallas_call(
        _gdn_kernel,
        out_shape=jax.ShapeDtypeStruct((bsz, seq, gw), BF16),
        grid=(seq // ts,),
        in_specs=[pl.BlockSpec((bsz, ts, 3 * gw), row), pl.BlockSpec((bsz, ts, gw), row),
                  pl.BlockSpec((bsz, ts, LANES), row),
                  pl.BlockSpec((bsz, SUBLANES, ts), lambda i: (0, 0, i)),
                  pl.BlockSpec((2, LANES), const), pl.BlockSpec((2 * GDN_HEADS, 2), const),
                  pl.BlockSpec((1, HEAD_DIM), const)],
        out_specs=pl.BlockSpec((bsz, ts, gw), row),
        scratch_shapes=[tile, tile, tile, tile, tile,
                        mats, mats, mats,
                        pltpu.VMEM((bsz * GDN_HEADS // 2, HEAD_DIM, 2 * HEAD_DIM), F32)],
        compiler_params=_params(("arbitrary",)),
        name="gdn",
    )(qkv, z, bdc, bdr, gpc, gpr, norm_w.reshape(1, HEAD_DIM))


def _attn_kernel(q_ref, k_ref, v_ref, nw_ref, o_ref,
                 q4, k4, v4, ktail, vtail, m_s, l_s, acc_s, tmp_s, nat_s):
    tq = q_ref.shape[0]
    blk = ATT_BLOCK
    sub = 4
    nq = tq // sub
    t = pl.program_id(2)
    slot = t % 2
    other = 1 - slot
    qi = lax.broadcasted_iota(jnp.int32, (blk, 2 * blk), 0)
    kj = lax.broadcasted_iota(jnp.int32, (blk, 2 * blk), 1)
    band = (kj >= qi) & (kj <= qi + blk)
    first_lo = jnp.where(t > 0, 0, blk)
    band_first = band & (kj >= first_lo)

    @pl.when(t == 0)
    def _():
        k4[...] = jnp.zeros(k4.shape, F32)
        v4[...] = jnp.zeros(v4.shape, F32)
        ktail[...] = jnp.zeros(ktail.shape, F32)
        vtail[...] = jnp.zeros(vtail.shape, F32)

    for r in range(sub):
        rows = slice(r * nq, (r + 1) * nq)
        src = pl.ds(r, nq, stride=sub)
        q4[rows, :] = q_ref[src, :]
        k4[slot, rows, :] = k_ref[src, :]
        v4[slot, rows, :] = v_ref[src, :]

    def block_stats(q, kcat, vcat, from_prev_tile):
        s = _mm_nt(q, kcat)
        s = jnp.where(band_first if from_prev_tile else band, s, NEG)
        m = jnp.max(s, axis=-1, keepdims=True)
        p = jnp.exp2(s - m)
        l = jnp.sum(p, axis=-1, keepdims=True)
        return m, l, _mm(p, vcat)

    def merge(dst, m, l, o):
        m_old = m_s[dst, :]
        m_new = jnp.maximum(m_old, m)
        w_old = jnp.exp2(m_old - m_new)
        w_cur = jnp.exp2(m - m_new)
        m_s[dst, :] = m_new
        l_s[dst, :] = w_old * l_s[dst, :] + w_cur * l
        acc_s[dst, :] = w_old * acc_s[dst, :] + w_cur * o

    for jb in range(tq // blk):
        cur = slice(jb * blk, (jb + 1) * blk)
        if jb > 0:
            kcat = k_ref[(jb - 1) * blk:(jb + 1) * blk, :]
            vcat = v_ref[(jb - 1) * blk:(jb + 1) * blk, :]
        else:
            kcat = jnp.concatenate([ktail[...], k_ref[cur, :]], axis=0)
            vcat = jnp.concatenate([vtail[...], v_ref[cur, :]], axis=0)
        m, l, o = block_stats(q_ref[cur, :], kcat, vcat, jb == 0)
        tmp_s[0] = jnp.broadcast_to(m, (blk, HEAD_DIM))
        tmp_s[1] = jnp.broadcast_to(l, (blk, HEAD_DIM))
        tmp_s[2] = o
        per = blk // sub
        for r in range(sub):
            dst = slice(r * nq + jb * per, r * nq + (jb + 1) * per)
            src = pl.ds(r, per, stride=sub)
            m_s[dst, :] = tmp_s[0, src, :]
            l_s[dst, :] = tmp_s[1, src, :]
            acc_s[dst, :] = tmp_s[2, src, :]

    for r in range(sub):
        for jb in range(nq // blk):
            base = r * nq + jb * blk
            cur = slice(base, base + blk)
            if jb > 0:
                kcat = k4[slot, base - blk:base + blk, :]
                vcat = v4[slot, base - blk:base + blk, :]
            else:
                last = slice((r + 1) * nq - blk, (r + 1) * nq)
                kcat = jnp.concatenate([k4[other, last, :], k4[slot, cur, :]], axis=0)
                vcat = jnp.concatenate([v4[other, last, :], v4[slot, cur, :]], axis=0)
            m, l, o = block_stats(q4[cur, :], kcat, vcat, jb == 0)
            merge(cur, m, l, o)

    for c in range(16):
        sl = pl.ds((c % sub) * nq + c // sub, blk, stride=sub)
        kcat = jnp.concatenate([k4[other, sl, :], k4[slot, sl, :]], axis=0)
        vcat = jnp.concatenate([v4[other, sl, :], v4[slot, sl, :]], axis=0)
        m, l, o = block_stats(q4[sl, :], kcat, vcat, True)
        merge(sl, m, l, o)

    out = acc_s[...] / l_s[...]
    out = out * lax.rsqrt(jnp.mean(out * out, axis=-1, keepdims=True) + RMS_EPS) * nw_ref[...]
    for r in range(sub):
        nat_s[pl.ds(r, nq, stride=sub), :] = out[r * nq:(r + 1) * nq]
    o_ref[...] = nat_s[...].astype(o_ref.dtype)
    ktail[...] = k_ref[tq - blk:tq, :]
    vtail[...] = v_ref[tq - blk:tq, :]


def _attn(qb, kb, vb, norm_w, bsz, seq):
    t = qb.shape[0]
    tq = ATT_TILE
    spb = seq // tq
    cur = lambda b, h, i: (b * spb + i, h)
    blk = pl.BlockSpec((tq, HEAD_DIM), cur)
    tile = pltpu.VMEM((tq, HEAD_DIM), F32)
    ring = pltpu.VMEM((2, tq, HEAD_DIM), F32)
    tail = pltpu.VMEM((ATT_BLOCK, HEAD_DIM), F32)
    return pl.pallas_call(
        _attn_kernel,
        out_shape=jax.ShapeDtypeStruct((t, ATT_HEADS * HEAD_DIM), BF16),
        grid=(bsz, ATT_HEADS, spb),
        in_specs=[blk, blk, blk, pl.BlockSpec((1, HEAD_DIM), lambda b, h, i: (0, 0))],
        out_specs=blk,
        scratch_shapes=[tile, ring, ring, tail, tail, tile, tile, tile,
                        pltpu.VMEM((3, ATT_BLOCK, HEAD_DIM), F32), tile],
        compiler_params=_params(("parallel", "parallel", "arbitrary")),
        name="attn",
    )(qb, kb, vb, norm_w.reshape(1, HEAD_DIM))


def _layer_norm(y, g, b):
    mu = jnp.mean(y, axis=-1, keepdims=True)
    yc = y - mu
    var = jnp.mean(yc * yc, axis=-1, keepdims=True)
    return yc * lax.rsqrt(var + LN_EPS) * g + b


def _outproj_kernel(oa_ref, ob_ref, x_ref, mod_ref, wo_ref, g_ref, b_ref,
                    wrh_ref, wrl_ref, br_ref,
                    x1_ref, h2_ref, ri_ref, rg_ref, cnt_ref, run_s, mix_s, hi_s, lo_s):
    @pl.when(pl.program_id(0) == 0)
    def _():
        run_s[...] = jnp.zeros(run_s.shape, F32)

    tm = x_ref.shape[0]
    gate1 = mod_ref[0, 2:3, :]
    shift2 = mod_ref[0, 3:4, :]
    scale2 = mod_ref[0, 4:5, :]
    n_parts = OUTPROJ_PARTS
    part = tm // n_parts
    grp_rows = 2 * SUBLANES

    def project(p):
        rows = slice(p * part, (p + 1) * part)
        mix_s[rows, :] = jnp.dot(jnp.concatenate([oa_ref[rows, :], ob_ref[rows, :]], axis=1), wo_ref[...],
                                 preferred_element_type=F32)

    def normalise(p):
        for r0 in range(p * part, (p + 1) * part, grp_rows):
            rows = slice(r0, r0 + grp_rows)
            x1 = _layer_norm(DEEPNORM_ALPHA * x_ref[rows, :] + (1.0 + gate1) * mix_s[rows, :],
                             g_ref[...], b_ref[...])
            x1_ref[rows, :] = x1
            h2 = x1 * (1.0 + scale2) + shift2
            h2_ref[rows] = h2.reshape(grp_rows, SUBLANES, LANES)
            hi = _bf(h2)
            hi_s[rows, :] = hi
            lo_s[rows, :] = _bf(h2 - hi.astype(F32))

    nt = (((1,), (1,)), ((), ()))

    def route(p):
        rows = slice(p * part, (p + 1) * part)
        hi = hi_s[rows, :]
        by_token = (jnp.dot(hi, wrh_ref[...], preferred_element_type=F32)
                    + jnp.dot(hi, wrl_ref[...], preferred_element_type=F32)
                    + jnp.dot(lo_s[rows, :], wrh_ref[...], preferred_element_type=F32))
        return by_token.T[:ROUTER_ROWS]

    project(0)
    parts = []
    for p in range(n_parts):
        if p + 1 < n_parts:
            project(p + 1)
        normalise(p)
        parts.append(route(p))
    logits = jnp.concatenate(parts, axis=1) + br_ref[:, 0:1]
    nr = logits.shape[0]
    row = lax.broadcasted_iota(jnp.int32, (nr, tm), 0).astype(F32)
    lg = jnp.where(row < N_GROUPS, logits, NEG)
    mg = jnp.max(lg, axis=0, keepdims=True)
    grp = jnp.min(jnp.where(lg == mg, row, float(nr)), axis=0, keepdims=True)
    gate_grp = 1.0 / jnp.sum(jnp.exp(lg - mg), axis=0, keepdims=True)
    first_row = N_GROUPS + EXPERTS_PER_GROUP * grp
    sel = (row >= first_row) & (row < first_row + EXPERTS_PER_GROUP)
    le = jnp.where(sel, logits, NEG)
    v1 = jnp.max(le, axis=0, keepdims=True)
    i1 = jnp.min(jnp.where(le == v1, row, float(nr)), axis=0, keepdims=True)
    le2 = jnp.where(row == i1, NEG, le)
    v2 = jnp.max(le2, axis=0, keepdims=True)
    i2 = jnp.min(jnp.where(le2 == v2, row, float(nr)), axis=0, keepdims=True)
    e21 = jnp.exp(v2 - v1)
    g1 = gate_grp / (1.0 + e21)
    g2 = gate_grp * e21 / (1.0 + e21)

    oh1 = row == i1
    oh2 = row == i2
    onehot = jnp.where(oh1 | oh2, 1.0, 0.0).astype(F32)
    ti = lax.broadcasted_iota(jnp.int32, (tm, tm), 0)
    tj = lax.broadcasted_iota(jnp.int32, (tm, tm), 1)
    earlier = jnp.where(ti < tj, 1.0, 0.0).astype(F32)
    tot = _mm(onehot, earlier) + run_s[:, 0:1]
    r1 = jnp.sum(jnp.where(oh1, tot, 0.0), axis=0, keepdims=True)
    r2 = jnp.sum(jnp.where(oh2, tot, 0.0), axis=0, keepdims=True)
    run_s[...] = run_s[...] + jnp.sum(onehot, axis=1, keepdims=True)
    cnt_ref[...] = run_s[...]

    sub_i = lax.broadcasted_iota(jnp.int32, (SUBLANES, tm), 0)
    ri = jnp.where(sub_i == 0, i1 - N_GROUPS, 0.0)
    ri = jnp.where(sub_i == 1, i2 - N_GROUPS, ri)
    ri = jnp.where(sub_i == 2, r1, ri)
    ri = jnp.where(sub_i == 3, r2, ri)
    ri_ref[...] = ri.astype(jnp.int32)

    gates = jnp.where(sub_i == 0, g1, jnp.where(sub_i == 1, g2, 0.0))
    pick = jnp.where(lax.broadcasted_iota(jnp.int32, (SUBLANES, LANES), 0)
                     == lax.broadcasted_iota(jnp.int32, (SUBLANES, LANES), 1), 1.0, 0.0).astype(BF16)
    tn = (((0,), (0,)), ((), ()))
    rg_ref[...] = sum(lax.dot_general(term, pick, tn, preferred_element_type=F32)
                      for term in _split3(gates))


def _outproj(oa, ob, xf, mod, w_o, ln_g, ln_b, w_rg, b_rg, w_re, b_re, seq):
    t, d = xf.shape
    tm = PROJ_TILE
    gw = oa.shape[1]
    wo = _bf(w_o)
    nr = ROUTER_ROWS
    wr = jnp.zeros((d, LANES), F32).at[:, :N_GROUPS].set(w_rg).at[:, N_GROUPS:N_GROUPS + N_EXPERTS].set(w_re)
    wrh = _bf(wr)
    wrl = _bf(wr - wrh.astype(F32))
    br = jnp.zeros((nr,), F32).at[:N_GROUPS].set(b_rg).at[N_GROUPS:N_GROUPS + N_EXPERTS].set(b_re)
    br = jnp.broadcast_to(br[:, None], (nr, LANES))
    spb = seq // tm
    row = lambda i: (i, 0)
    const = lambda i: (0, 0)
    return pl.pallas_call(
        _outproj_kernel,
        out_shape=(jax.ShapeDtypeStruct((t, d), F32), jax.ShapeDtypeStruct((t, d // LANES, LANES), F32),
                   jax.ShapeDtypeStruct((SUBLANES, t), jnp.int32), jax.ShapeDtypeStruct((t, LANES), F32),
                   jax.ShapeDtypeStruct((nr, LANES), F32)),
        grid=(t // tm,),
        in_specs=[pl.BlockSpec((tm, gw), row), pl.BlockSpec((tm, gw), row), pl.BlockSpec((tm, d), row),
                  pl.BlockSpec((1, 6, d), lambda i: (i // spb, 0, 0)),
                  pl.BlockSpec((2 * gw, d), const),
                  pl.BlockSpec((1, d), const), pl.BlockSpec((1, d), const),
                  pl.BlockSpec((d, LANES), const), pl.BlockSpec((d, LANES), const),
                  pl.BlockSpec((nr, LANES), const)],
        out_specs=(pl.BlockSpec((tm, d), row), pl.BlockSpec((tm, d // LANES, LANES), lambda i: (i, 0, 0)),
                   pl.BlockSpec((SUBLANES, tm), lambda i: (0, i)), pl.BlockSpec((tm, LANES), row),
                   pl.BlockSpec((nr, LANES), const)),
        scratch_shapes=[pltpu.VMEM((nr, LANES), F32), pltpu.VMEM((tm, d), F32),
                        pltpu.VMEM((tm, d), BF16), pltpu.VMEM((tm, d), BF16)],
        compiler_params=_params(("arbitrary",)),
        name="outproj",
    )(oa, ob, xf, mod, wo, ln_g.reshape(1, d), ln_b.reshape(1, d), wrh, wrl, br)


def _dispatch_kernel(d1_ref, d2_ref, h_ref, xs_ref, sem):
    tm = h_ref.shape[0]
    i = pl.program_id(0)

    def row_copy(tk, dest):
        return pltpu.make_async_copy(h_ref.at[tk], xs_ref.at[dest], sem)

    def issue(g, carry):
        for u in range(DMA_UNROLL):
            tk = g * DMA_UNROLL + u
            tok = i * tm + tk
            row_copy(tk, d1_ref[tok]).start(priority=0)
            row_copy(tk, d2_ref[tok]).start(priority=1)
        return carry

    lax.fori_loop(0, tm // DMA_UNROLL, issue, 0)

    tile_copy = pltpu.make_async_copy(h_ref, xs_ref.at[pl.ds(0, tm)], sem)
    tile_copy.wait()
    tile_copy.wait()


def _dispatch(h2, d1, d2):
    t, sub, lanes = h2.shape
    tm = DISPATCH_TILE
    return pl.pallas_call(
        _dispatch_kernel,
        out_shape=jax.ShapeDtypeStruct((2 * t, sub, lanes), F32),
        grid_spec=pltpu.PrefetchScalarGridSpec(
            num_scalar_prefetch=2,
            grid=(t // tm,),
            in_specs=[pl.BlockSpec((tm, sub, lanes), lambda i, *_: (i, 0, 0))],
            out_specs=pl.BlockSpec(memory_space=pl.ANY),
            scratch_shapes=[pltpu.SemaphoreType.DMA]),
        compiler_params=_params(("arbitrary",)),
        name="dispatch",
    )(d1, d2, h2)


def _experts_kernel(wb_ref, we_ref, lo_ref, hi_ref, nw_ref, first_ref, ring_ref, next_ref,
                    xs_ref, wg_hbm, wu_hbm, wd_hbm, ys_ref,
                    wg_f, wu_f, wd_f, wg_s, wu_s, wd_s, sem):
    w = pl.program_id(0)

    def fetch(expert, slot):
        return (pltpu.make_async_copy(wg_hbm.at[expert], wg_f.at[slot], sem.at[slot]),
                pltpu.make_async_copy(wu_hbm.at[expert], wu_f.at[slot], sem.at[slot]),
                pltpu.make_async_copy(wd_hbm.at[expert], wd_f.at[slot], sem.at[slot]))

    @pl.when(w == 0)
    def _():
        for cp in fetch(we_ref[0], 0):
            cp.start()

    @pl.when((first_ref[w] == 1) & (w < nw_ref[0]))
    def _():
        slot = ring_ref[w]
        for cp in fetch(we_ref[w], slot):
            cp.wait()

        @pl.when(next_ref[w] >= 0)
        def _():
            for cp in fetch(next_ref[w], 1 - slot):
                cp.start()

        wg_s[...] = _bf(wg_f[slot])
        wu_s[...] = _bf(wu_f[slot])
        wd_s[...] = _bf(wd_f[slot])

    @pl.when(w < nw_ref[0])
    def _():
        bm, sub, lanes = xs_ref.shape
        x = _bf(xs_ref[...].reshape(bm, sub * lanes))
        hid = (_silu(jnp.dot(x, wg_s[...], preferred_element_type=F32))
               * jnp.dot(x, wu_s[...], preferred_element_type=F32))
        y = jnp.dot(_bf(hid), wd_s[...], preferred_element_type=F32).reshape(bm, sub, lanes)
        row = lax.broadcasted_iota(jnp.int32, (bm, 1, 1), 0)
        mine = (row >= lo_ref[w]) & (row < hi_ref[w])

        @pl.when(lo_ref[w] == 0)
        def _():
            ys_ref[...] = jnp.where(mine, y, 0.0)

        @pl.when(lo_ref[w] > 0)
        def _():
            ys_ref[...] = jnp.where(mine, y, ys_ref[...])


def _experts(xs, item_block, item_expert, item_lo, item_hi, n_items, w_gate, w_up, w_down):
    n_slots, sub, lanes = xs.shape
    d = sub * lanes
    ff = w_gate.shape[2]
    bm = EXPERT_BLOCK
    n = item_block.shape[0]
    idx = jnp.arange(n, dtype=jnp.int32)
    first = jnp.concatenate([jnp.ones((1,), jnp.int32),
                             (item_expert[1:] != item_expert[:-1]).astype(jnp.int32)])
    ring = (jnp.cumsum(first) - 1) % 2
    next_first = lax.cummin(jnp.where(first == 1, idx, n), reverse=True)
    next_first = jnp.concatenate([next_first[1:], jnp.full((1,), n, jnp.int32)])
    nxt = jnp.where(next_first < n, item_expert[jnp.minimum(next_first, n - 1)], -1).astype(jnp.int32)
    slot = lambda w, *_: (_[0][w], 0, 0)
    return pl.pallas_call(
        _experts_kernel,
        out_shape=jax.ShapeDtypeStruct((n_slots, sub, lanes), F32),
        grid_spec=pltpu.PrefetchScalarGridSpec(
            num_scalar_prefetch=8,
            grid=(n,),
            in_specs=[pl.BlockSpec((bm, sub, lanes), slot),
                      pl.BlockSpec(memory_space=pl.ANY), pl.BlockSpec(memory_space=pl.ANY),
                      pl.BlockSpec(memory_space=pl.ANY)],
            out_specs=pl.BlockSpec((bm, sub, lanes), slot),
            scratch_shapes=[pltpu.VMEM((2, d, ff), F32), pltpu.VMEM((2, d, ff), F32),
                            pltpu.VMEM((2, ff, d), F32),
                            pltpu.VMEM((d, ff), BF16), pltpu.VMEM((d, ff), BF16),
                            pltpu.VMEM((ff, d), BF16), pltpu.SemaphoreType.DMA((2,))]),
        compiler_params=_params(("arbitrary",)),
        name="experts",
    )(item_block, item_expert, item_lo, item_hi, n_items, first, ring.astype(jnp.int32), nxt,
      xs, w_gate, w_up, w_down)


def _combine_kernel(d1_ref, d2_ref, ys_ref, rg_ref, x1_ref, mod_ref, g_ref, b_ref, o_ref, ya, yb, sem):
    tm = x1_ref.shape[0]
    i = pl.program_id(0)
    n = pl.num_programs(0)

    def row_copy(dest, buf, slot, tk):
        return pltpu.make_async_copy(ys_ref.at[dest], buf.at[slot, tk], sem.at[slot])

    slot = i % 2
    gate2 = mod_ref[0, 5:6, :]

    def issue_group(step, dst_slot, g):
        for u in range(DMA_UNROLL):
            tk = g * DMA_UNROLL + u
            tok = step * tm + tk
            row_copy(d1_ref[tok], ya, dst_slot, tk).start(priority=0)
            row_copy(d2_ref[tok], yb, dst_slot, tk).start(priority=1)

    def gather_tile(step, dst_slot):
        def issue(g, carry):
            issue_group(step, dst_slot, g)
            return carry

        lax.fori_loop(0, tm // DMA_UNROLL, issue, 0)

    @pl.when(i == 0)
    def _():
        gather_tile(0, 0)

    @pl.when(i + 1 < n)
    def _():
        gather_tile(i + 1, 1 - slot)

    pltpu.make_async_copy(ys_ref.at[pl.ds(0, tm)], ya.at[slot], sem.at[slot]).wait()
    pltpu.make_async_copy(ys_ref.at[pl.ds(0, tm)], yb.at[slot], sem.at[slot]).wait()

    rg = rg_ref[...]
    d = x1_ref.shape[1]
    y = rg[:, 0:1] * ya[slot].reshape(tm, d) + rg[:, 1:2] * yb[slot].reshape(tm, d)
    o_ref[...] = _layer_norm(DEEPNORM_ALPHA * x1_ref[...] + (1.0 + gate2) * y, g_ref[...], b_ref[...])


def _combine(ys, d1, d2, rg, x1, mod, ln_g, ln_b, seq):
    t, d = x1.shape
    tm = ROW_TILE
    spb = seq // tm
    row = lambda i, *_: (i, 0)
    const = lambda i, *_: (0, 0)
    buf = pltpu.VMEM((2, tm) + ys.shape[1:], F32)
    return pl.pallas_call(
        _combine_kernel,
        out_shape=jax.ShapeDtypeStruct((t, d), F32),
        grid_spec=pltpu.PrefetchScalarGridSpec(
            num_scalar_prefetch=2,
            grid=(t // tm,),
            in_specs=[pl.BlockSpec(memory_space=pl.ANY),
                      pl.BlockSpec((tm, LANES), row), pl.BlockSpec((tm, d), row),
                      pl.BlockSpec((1, 6, d), lambda i, *_: (i // spb, 0, 0)),
                      pl.BlockSpec((1, d), const), pl.BlockSpec((1, d), const)],
            out_specs=pl.BlockSpec((tm, d), row),
            scratch_shapes=[buf, buf, pltpu.SemaphoreType.DMA((2,))]),
        compiler_params=_params(("arbitrary",)),
        name="combine",
    )(d1, d2, ys, rg, x1, mod, ln_g.reshape(1, d), ln_b.reshape(1, d))


def _layer(x, c, positions, w_ada, b_ada, w_in, conv_w, a_log, dt_bias, gdn_norm_w, attn_norm_w,
           w_o, ln1_g, ln1_b, w_rg, b_rg, w_re, b_re, w_gate, w_up, w_down, ln2_g, ln2_b):
    bsz, seq, d = x.shape
    t = bsz * seq
    xf = x.reshape(t, d)
    mod = _ada(c, w_ada, b_ada)
    qkv, z, bdc, bdr, qb, kb, vb = _inproj(xf, mod, positions.reshape(t, 1), w_in, conv_w, seq)
    gw = GDN_HEADS * HEAD_DIM
    oa = _gdn(qkv.reshape(bsz, seq, 3 * gw), z.reshape(bsz, seq, gw), bdc.reshape(bsz, seq, LANES), bdr,
              a_log, dt_bias, gdn_norm_w).reshape(t, gw)
    ob = _attn(qb, kb, vb, attn_norm_w, bsz, seq)
    x1, h2, ri, rg, cnt = _outproj(oa, ob, xf, mod, w_o, ln1_g, ln1_b, w_rg, b_rg, w_re, b_re, seq)

    bm = EXPERT_BLOCK
    counts = cnt[N_GROUPS:N_GROUPS + N_EXPERTS, 0].astype(jnp.int32)
    seg_end = jnp.cumsum(counts)
    seg_start = seg_end - counts
    first_blk = seg_start // bm
    n_per = jnp.where(counts > 0, (seg_end - 1) // bm - first_blk + 1, 0)
    item_end = jnp.cumsum(n_per)
    n_items = item_end[-1:]
    max_items = (2 * t) // bm + N_EXPERTS - 1
    w_idx = jnp.minimum(jnp.arange(max_items, dtype=jnp.int32), n_items[0] - 1)
    item_expert = jnp.minimum(jnp.sum(item_end[None, :] <= w_idx[:, None], axis=1), N_EXPERTS - 1).astype(jnp.int32)
    item_block = first_blk[item_expert] + w_idx - (item_end - n_per)[item_expert]
    item_lo = jnp.maximum(seg_start[item_expert] - item_block * bm, 0)
    item_hi = jnp.minimum(seg_end[item_expert] - item_block * bm, bm)
    expert_ids = jnp.arange(N_EXPERTS, dtype=jnp.int32)[:, None]

    def seg_of(e):
        return jnp.sum(jnp.where(e[None, :] == expert_ids, seg_start[:, None], 0), axis=0)

    d1 = seg_of(ri[0]) + ri[2]
    d2 = seg_of(ri[1]) + ri[3]

    xs = _dispatch(h2, d1, d2)
    ys = _experts(xs, item_block, item_expert, item_lo, item_hi, n_items, w_gate, w_up, w_down)
    out = _combine(ys, d1, d2, rg, x1, mod, ln2_g, ln2_b, seq)
    return out.reshape(bsz, seq, d)


def kernel(x, c, positions, w_ada, b_ada, w_in, conv_w, a_log, dt_bias, gdn_norm_w, attn_norm_w, w_o, ln1_g, ln1_b, w_router_group, b_router_group, w_router_expert, b_router_expert, w_gate, w_up, w_down, ln2_g, ln2_b):
    assert w_ada.shape[0] == DEPTH
    return _layer(x, c, positions, w_ada[0], b_ada[0], w_in[0], conv_w[0], a_log[0], dt_bias[0],
                  gdn_norm_w[0], attn_norm_w[0], w_o[0], ln1_g[0], ln1_b[0],
                  w_router_group[0], b_router_group[0], w_router_expert[0], b_router_expert[0],
                  w_gate[0], w_up[0], w_down[0], ln2_g[0], ln2_b[0])
```

```python
import functools
import math

import jax
import jax.numpy as jnp
from jax import lax
from jax.experimental import pallas as pl
from jax.experimental.pallas import tpu as pltpu

F32 = jnp.float32
BF16 = jnp.bfloat16
LOG2E = math.log2(math.e)

GDN_HEADS = 4
ATT_HEADS = 4
HEAD_DIM = 128
CONV_WIDTH = 4
DILATED_PATTERNS = ((128, 1), (512, 4), (2048, 16))
ROPE_THETA = 500000.0
ROPE_DIMS = HEAD_DIM // 4
N_GROUPS = 4
EXPERTS_PER_GROUP = 8
N_EXPERTS = N_GROUPS * EXPERTS_PER_GROUP
ROUTER_ROWS = -(-(N_GROUPS + N_EXPERTS) // 8) * 8
DEPTH = 1
DEEPNORM_ALPHA = (2.0 * DEPTH) ** 0.25
LN_EPS = 1e-5
RMS_EPS = 1e-6

LANES = 128
SUBLANES = 8
VMEM_LIMIT = 48 * 1024 * 1024

GDN_BLOCK = 128
GDN_TILE = 256
ATT_BLOCK = 128
ATT_TILE = 2048
ADA_COLS = 1024
PROJ_TILE = 512
ROW_TILE = 256
DISPATCH_TILE = 4096
EXPERT_BLOCK = 256
OUTPROJ_PARTS = 2
DMA_UNROLL = 8
NEG = -1e30


def _bf(x):
    return x.astype(BF16)


def _mm(a, b):
    return jnp.dot(_bf(a), _bf(b), preferred_element_type=F32)


def _mm_nt(a, b):
    return lax.dot_general(_bf(a), _bf(b), (((1,), (1,)), ((), ())), preferred_element_type=F32)


def _mm_tn(a, b):
    return lax.dot_general(_bf(a), _bf(b), (((0,), (0,)), ((), ())), preferred_element_type=F32)


def _split3(x):
    x1 = _bf(x)
    r1 = x - x1.astype(F32)
    x2 = _bf(r1)
    return x1, x2, _bf(r1 - x2.astype(F32))


def _sigmoid(x):
    return 1.0 / (1.0 + jnp.exp(-x))


def _silu(x):
    return x * _sigmoid(x)


def _softplus(x):
    return jnp.maximum(x, 0.0) + jnp.log(1.0 + jnp.exp(-jnp.abs(x)))


def _params(sem):
    return pltpu.CompilerParams(dimension_semantics=sem, vmem_limit_bytes=VMEM_LIMIT)


def _ada_kernel(ct_ref, w_ref, b_ref, o_ref, *, bsz):
    sc = _silu(ct_ref[...])
    w = w_ref[...]
    rows = [jnp.sum(w * sc[:, b:b + 1], axis=0, keepdims=True) for b in range(bsz)]
    rows.append(jnp.zeros((o_ref.shape[0] - bsz, w.shape[1]), F32))
    o_ref[...] = jnp.concatenate(rows, axis=0) + b_ref[...]


def _ada(c, w_ada, b_ada):
    bsz, d = c.shape
    n = w_ada.shape[1]
    tn = ADA_COLS
    assert bsz <= SUBLANES
    ct = jnp.zeros((d, LANES), F32).at[:, :bsz].set(c.T)
    out = pl.pallas_call(
        functools.partial(_ada_kernel, bsz=bsz),
        out_shape=jax.ShapeDtypeStruct((SUBLANES, n), F32),
        grid=(n // tn,),
        in_specs=[pl.BlockSpec((d, LANES), lambda j: (0, 0)),
                  pl.BlockSpec((d, tn), lambda j: (0, j)),
                  pl.BlockSpec((1, tn), lambda j: (0, j))],
        out_specs=pl.BlockSpec((SUBLANES, tn), lambda j: (0, j)),
        compiler_params=_params(("parallel",)),
        name="ada",
    )(ct, w_ada, b_ada.reshape(1, n))
    return out[:bsz].reshape(bsz, 6, d)


def _inproj_kernel(x_ref, mod_ref, pos_ref, invf_ref, convw_ref, wqkv_ref, wz_ref, wbd_ref,
                   wq_ref, wk_ref, wv_ref,
                   qkv_ref, z_ref, bdc_ref, bdr_ref, qb_ref, kb_ref, vb_ref, cbuf, *, steps_per_seq):
    tm = x_ref.shape[0]
    halo = SUBLANES
    n_slabs = cbuf.shape[0]

    @pl.when(pl.program_id(0) % steps_per_seq == 0)
    def _():
        cbuf[:, 0:halo, :] = jnp.zeros((n_slabs, halo, LANES), F32)

    half = ROPE_DIMS // 2
    groups = LANES // half
    lane = lax.broadcasted_iota(jnp.int32, (1, LANES), 1)
    first = lane < half
    rot = lane < ROPE_DIMS
    ang = pos_ref[0].astype(F32) * invf_ref[...]
    cos_c = jnp.cos(ang)
    sin_c = jnp.sin(ang)
    cos_parts, sin_parts = [], []
    for j in range(groups):
        lo_sh = (LANES - half * j) % LANES
        hi_sh = (LANES - half * j + half) % LANES
        c_lo = pltpu.roll(cos_c, lo_sh, 1) if lo_sh else cos_c
        c_hi = pltpu.roll(cos_c, hi_sh, 1) if hi_sh else cos_c
        s_lo = pltpu.roll(sin_c, lo_sh, 1) if lo_sh else sin_c
        s_hi = pltpu.roll(sin_c, hi_sh, 1) if hi_sh else sin_c
        cos_parts.append(jnp.where(first, c_lo, jnp.where(rot, c_hi, 1.0)))
        sin_parts.append(jnp.where(first, -s_lo, jnp.where(rot, s_hi, 0.0)))
    cosv = jnp.concatenate(cos_parts, axis=0)
    sin_signed = jnp.concatenate(sin_parts, axis=0)

    def rope(y):
        outs = []
        for hh in range(ATT_HEADS):
            yh = y[:, hh * HEAD_DIM:(hh + 1) * HEAD_DIM]
            partner = jnp.where(first, pltpu.roll(yh, LANES - half, 1), pltpu.roll(yh, half, 1))
            outs.append(yh * cosv + partner * sin_signed)
        return jnp.concatenate(outs, axis=1)

    def conv_slice(s):
        sl = slice(s * HEAD_DIM, (s + 1) * HEAD_DIM)
        off = halo - (CONV_WIDTH - 1)
        acc = convw_ref[0:1, sl] * cbuf[s, off:off + tm, :]
        for j in range(1, CONV_WIDTH):
            acc = acc + convw_ref[j:j + 1, sl] * cbuf[s, off + j:off + j + tm, :]
        cbuf[s, 0:halo, :] = cbuf[s, tm:tm + halo, :]
        y = _silu(acc)
        if s < 2 * GDN_HEADS:
            y = y * lax.rsqrt(jnp.sum(y * y, axis=-1, keepdims=True) + RMS_EPS)
        if s < GDN_HEADS:
            y = y * (HEAD_DIM ** -0.5)
        qkv_ref[:, sl] = y

    shift = mod_ref[0, 0:1, :]
    scale = mod_ref[0, 1:2, :]
    h = _bf(x_ref[...] * (1.0 + scale) + shift)
    chunk = 2 * HEAD_DIM
    n_chunks = n_slabs // 2

    def project_chunk(c):
        pre = jnp.dot(h, wqkv_ref[:, c * chunk:(c + 1) * chunk], preferred_element_type=F32)
        cbuf[2 * c, halo:halo + tm, :] = pre[:, :HEAD_DIM]
        cbuf[2 * c + 1, halo:halo + tm, :] = pre[:, HEAD_DIM:]

    def conv_chunk(c):
        conv_slice(2 * c)
        conv_slice(2 * c + 1)

    project_chunk(0)
    for c in range(1, n_chunks):
        project_chunk(c)
        conv_chunk(c - 1)
    qb = jnp.dot(h, wq_ref[...], preferred_element_type=F32)
    conv_chunk(n_chunks - 1)
    kb = jnp.dot(h, wk_ref[...], preferred_element_type=F32)
    qb_ref[...] = rope(qb) * (HEAD_DIM ** -0.5 * LOG2E)
    vb_ref[...] = jnp.dot(h, wv_ref[...], preferred_element_type=F32)
    kb_ref[...] = rope(kb)
    z_ref[...] = jnp.dot(h, wz_ref[...], preferred_element_type=F32)
    bdc = jnp.dot(h, wbd_ref[...], preferred_element_type=F32)
    bdc_ref[...] = bdc
    bdr_ref[0] = bdc.T[:SUBLANES]


def _inproj(xf, mod, pos, w_in, conv_w, seq):
    t, d = xf.shape
    tm = PROJ_TILE
    gw = GDN_HEADS * HEAD_DIM
    aw = ATT_HEADS * HEAD_DIM
    o0 = 3 * gw
    o1 = o0 + gw
    o2 = o1 + 2 * GDN_HEADS
    wb = _bf(w_in)
    wqkv, wz = wb[:, :o0], wb[:, o0:o1]
    wbd_n = wb[:, o1:o2]
    wbd = jnp.zeros((d, LANES), BF16).at[:, :2 * GDN_HEADS].set(wbd_n)
    wq, wk, wv = wb[:, o2:o2 + aw], wb[:, o2 + aw:o2 + 2 * aw], wb[:, o2 + 2 * aw:o2 + 3 * aw]
    half = ROPE_DIMS // 2
    groups = LANES // half
    inv_freq = ROPE_THETA ** (-jnp.arange(half, dtype=F32) * 2.0 / ROPE_DIMS)
    invf = jnp.tile(inv_freq, groups).reshape(1, LANES)
    pos = jnp.repeat(pos.reshape(t // tm, groups, tm // groups).transpose(0, 2, 1), half, axis=2)
    spb = seq // tm
    row = lambda i: (i, 0)
    const = lambda i: (0, 0)
    return pl.pallas_call(
        functools.partial(_inproj_kernel, steps_per_seq=spb),
        out_shape=(jax.ShapeDtypeStruct((t, o0), F32), jax.ShapeDtypeStruct((t, gw), F32),
                   jax.ShapeDtypeStruct((t, LANES), F32), jax.ShapeDtypeStruct((t // seq, SUBLANES, seq), F32),
                   jax.ShapeDtypeStruct((t, aw), F32), jax.ShapeDtypeStruct((t, aw), F32),
                   jax.ShapeDtypeStruct((t, aw), F32)),
        grid=(t // tm,),
        in_specs=[pl.BlockSpec((tm, d), row),
                  pl.BlockSpec((1, 6, d), lambda i: (i // spb, 0, 0)),
                  pl.BlockSpec((1, tm // groups, LANES), lambda i: (i, 0, 0)),
                  pl.BlockSpec((1, LANES), const), pl.BlockSpec((CONV_WIDTH, o0), const),
                  pl.BlockSpec((d, o0), const), pl.BlockSpec((d, gw), const),
                  pl.BlockSpec((d, LANES), const),
                  pl.BlockSpec((d, aw), const), pl.BlockSpec((d, aw), const),
                  pl.BlockSpec((d, aw), const)],
        out_specs=(pl.BlockSpec((tm, o0), row), pl.BlockSpec((tm, gw), row),
                   pl.BlockSpec((tm, LANES), row),
                   pl.BlockSpec((1, SUBLANES, tm), lambda i: (i // spb, 0, i % spb)),
                   pl.BlockSpec((tm, aw), row), pl.BlockSpec((tm, aw), row),
                   pl.BlockSpec((tm, aw), row)),
        scratch_shapes=[pltpu.VMEM((o0 // HEAD_DIM, tm + 2 * SUBLANES, HEAD_DIM), F32)],
        compiler_params=_params(("arbitrary",)),
        name="inproj",
    )(xf, mod, pos, invf, conv_w, wqkv, wz, wbd, wq, wk, wv)


def _gdn_kernel(qkv_ref, z_ref, bdc_ref, bdr_ref, gpc_ref, gpr_ref, nw_ref, o_ref,
                u_s, w_s, qd_s, kd_s, qk_s, a_s, x_s, y_s, st_ref):
    nbat, ts = qkv_ref.shape[0], qkv_ref.shape[1]
    nb = ts // GDN_BLOCK
    gw = GDN_HEADS * HEAD_DIM

    @pl.when(pl.program_id(0) == 0)
    def _():
        st_ref[...] = jnp.zeros(st_ref.shape, F32)

    blk_n = GDN_BLOCK
    hd = HEAD_DIM
    ti = lax.broadcasted_iota(jnp.int32, (ts, ts), 0)
    tj = lax.broadcasted_iota(jnp.int32, (ts, ts), 1)
    same = (ti // blk_n) == (tj // blk_n)
    m_low = jnp.where(same & (tj <= ti), 1.0, 0.0).astype(BF16)
    m_up = jnp.where(same & (ti <= tj), 1.0, 0.0).astype(BF16)

    split3 = _split3

    beta_c = [_sigmoid(bdc_ref[bb]) for bb in range(nbat)]
    g_c = jnp.concatenate([-jnp.exp(gpc_ref[0:1, :]) * LOG2E * _softplus(bdc_ref[bb] + gpc_ref[1:2, :])
                           for bb in range(nbat)], axis=1)
    g_r = jnp.concatenate([-jnp.exp(gpr_ref[:, 0:1]) * LOG2E * _softplus(bdr_ref[bb] + gpr_ref[:, 1:2])
                           for bb in range(nbat)], axis=0)
    gc_c = sum(jnp.dot(m_low, part, preferred_element_type=F32) for part in split3(g_c))
    gc_r = sum(jnp.dot(part, m_up, preferred_element_type=F32) for part in split3(g_r))

    def cat2(m):
        return jnp.concatenate([m, m], axis=1)

    ii = lax.broadcasted_iota(jnp.int32, (blk_n, blk_n), 0)
    jj = lax.broadcasted_iota(jnp.int32, (blk_n, blk_n), 1)
    lower = cat2(jj <= ii)
    strict = cat2(jj < ii)
    eye = cat2(jnp.where(ii == jj, 1.0, 0.0).astype(F32))
    levels = []
    b = 1
    while b < blk_n:
        levels.append(cat2(((ii // b) == (jj // b) + 1) & (((jj // b) % 2) == 0)))
        b *= 2

    def block_diag(rp):
        n, m = rp.shape[0], rp.shape[1] // 2
        z = jnp.zeros((n, m), rp.dtype)
        return jnp.concatenate([jnp.concatenate([rp[:, :m], z], axis=1),
                                jnp.concatenate([z, rp[:, m:]], axis=1)], axis=0)

    def mm2(lp, rp):
        return jnp.dot(_bf(lp), block_diag(_bf(rp)), preferred_element_type=F32)

    def mm2_nt(lp, rp):
        return lax.dot_general(_bf(lp), block_diag(_bf(rp)), (((1,), (1,)), ((), ())),
                               preferred_element_type=F32)

    chains = [(bb, j, pp) for bb in range(nbat) for j in range(nb) for pp in range(GDN_HEADS // 2)]

    def tile_of(bb, j, pp):
        return bb, slice(j * blk_n, (j + 1) * blk_n), slice(2 * pp * hd, 2 * (pp + 1) * hd)

    def col_pair(arr, rows, col):
        return jnp.concatenate([jnp.broadcast_to(arr[rows, col:col + 1], (blk_n, hd)),
                                jnp.broadcast_to(arr[rows, col + 1:col + 2], (blk_n, hd))], axis=1)

    for c, (bb, j, pp) in enumerate(chains):
        blk = tile_of(bb, j, pp)
        rows, cols = blk[1], blk[2]
        last = slice((j + 1) * blk_n - 1, (j + 1) * blk_n)
        gcol = bb * LANES + GDN_HEADS + 2 * pp
        grow = bb * SUBLANES + GDN_HEADS + 2 * pp
        q = qkv_ref[bb, rows, cols]
        k = qkv_ref[bb, rows, slice(gw + cols.start, gw + cols.stop)]
        v = qkv_ref[bb, rows, slice(2 * gw + cols.start, 2 * gw + cols.stop)]
        beta = col_pair(beta_c[bb], rows, 2 * pp)
        gcc = col_pair(gc_c, rows, gcol)
        gtc = col_pair(gc_c, last, gcol)
        gcr = jnp.concatenate([jnp.broadcast_to(gc_r[grow:grow + 1, rows], (blk_n, hd)),
                               jnp.broadcast_to(gc_r[grow + 1:grow + 2, rows], (blk_n, hd))], axis=1)
        kb = k * beta
        eg = jnp.exp2(gcc)
        dm = jnp.where(lower, jnp.exp2(gcc - gcr), 0.0)
        a = jnp.where(strict, mm2_nt(kb, k) * dm, 0.0)
        a_s[c] = a
        x_s[c] = eye - jnp.where(levels[0], a, 0.0)
        u_s[blk] = v * beta
        w_s[blk] = kb * eg
        qd_s[blk] = q * eg
        kd_s[blk] = k * jnp.exp2(gtc - gcc)
        qk_s[blk] = mm2_nt(q, k) * dm

    for lm in levels[1:]:
        for c in range(len(chains)):
            y_s[c] = mm2(x_s[c], jnp.where(lm, a_s[c], 0.0))
        for c in range(len(chains)):
            xc = x_s[c]
            x_s[c] = xc - mm2(y_s[c], xc)

    for c, (bb, j, pp) in enumerate(chains):
        blk = tile_of(bb, j, pp)
        u, w = _bf(u_s[blk]), _bf(w_s[blk])
        z = jnp.zeros((blk_n, 2 * hd), BF16)
        rhs = jnp.concatenate([jnp.concatenate([u[:, :hd], w[:, :hd], z], axis=1),
                               jnp.concatenate([z, u[:, hd:], w[:, hd:]], axis=1)], axis=0)
        sol = jnp.dot(_bf(x_s[c]), rhs, preferred_element_type=F32)
        u_s[blk] = jnp.concatenate([sol[:, 0:hd], sol[:, 2 * hd:3 * hd]], axis=1)
        w_s[blk] = jnp.concatenate([sol[:, hd:2 * hd], sol[:, 3 * hd:]], axis=1)

    for j in range(nb):
        for bb in range(nbat):
            for pp in range(GDN_HEADS // 2):
                blk = tile_of(bb, j, pp)
                si = bb * (GDN_HEADS // 2) + pp
                last = slice((j + 1) * blk_n - 1, (j + 1) * blk_n)
                gcol = bb * LANES + GDN_HEADS + 2 * pp
                state = st_ref[si]
                proj = mm2(jnp.concatenate([w_s[blk], qd_s[blk]], axis=0), state)
                v_new = u_s[blk] - proj[:blk_n]
                o = proj[blk_n:] + mm2(qk_s[blk], v_new)
                g_last = jnp.exp2(jnp.concatenate(
                    [jnp.broadcast_to(gc_c[last, gcol:gcol + 1], (1, hd)),
                     jnp.broadcast_to(gc_c[last, gcol + 1:gcol + 2], (1, hd))], axis=1))
                kd = kd_s[blk]
                upd = jnp.concatenate([_mm_tn(kd[:, :hd], v_new[:, :hd]),
                                       _mm_tn(kd[:, hd:], v_new[:, hd:])], axis=1)
                st_ref[si] = state * g_last + upd
                zz = z_ref[blk]
                halves = []
                for hf in range(2):
                    oh = o[:, hf * hd:(hf + 1) * hd]
                    halves.append(oh * lax.rsqrt(jnp.mean(oh * oh, axis=-1, keepdims=True) + RMS_EPS)
                                  * nw_ref[...])
                o_ref[blk] = (jnp.concatenate(halves, axis=1) * _silu(zz)).astype(o_ref.dtype)


def _gdn(qkv, z, bdc, bdr, a_log, dt_bias, norm_w):
    bsz, seq, _ = qkv.shape
    ts = GDN_TILE
    gw = GDN_HEADS * HEAD_DIM
    zeros4 = jnp.zeros((GDN_HEADS,), F32)
    al = jnp.concatenate([zeros4, a_log])
    db = jnp.concatenate([zeros4, dt_bias])
    gpc = jnp.zeros((2, LANES), F32).at[0, :2 * GDN_HEADS].set(al).at[1, :2 * GDN_HEADS].set(db)
    gpr = jnp.stack([al, db], axis=1)
    row = lambda i: (0, i, 0)
    const = lambda i: (0, 0)
    n_chains = bsz * (ts // GDN_BLOCK) * (GDN_HEADS // 2)
    tile = pltpu.VMEM((bsz, ts, gw), F32)
    mats = pltpu.VMEM((n_chains, GDN_BLOCK, 2 * GDN_BLOCK), F32)
    return pl.pallas_call(
        _gdn_kernel,
        out_shape=jax.ShapeDtypeStruct((bsz, seq, gw), BF16),
        grid=(seq // ts,),
        in_specs=[pl.BlockSpec((bsz, ts, 3 * gw), row), pl.BlockSpec((bsz, ts, gw), row),
                  pl.BlockSpec((bsz, ts, LANES), row),
                  pl.BlockSpec((bsz, SUBLANES, ts), lambda i: (0, 0, i)),
                  pl.BlockSpec((2, LANES), const), pl.BlockSpec((2 * GDN_HEADS, 2), const),
                  pl.BlockSpec((1, HEAD_DIM), const)],
        out_specs=pl.BlockSpec((bsz, ts, gw), row),
        scratch_shapes=[tile, tile, tile, tile, tile,
                        mats, mats, mats,
                        pltpu.VMEM((bsz * GDN_HEADS // 2, HEAD_DIM, 2 * HEAD_DIM), F32)],
        compiler_params=_params(("arbitrary",)),
        name="gdn",
    )(qkv, z, bdc, bdr, gpc, gpr, norm_w.reshape(1, HEAD_DIM))


def _attn_kernel(q_ref, k_ref, v_ref, nw_ref, o_ref,
                 q4, k4, v4, ktail, vtail, m_s, l_s, acc_s, tmp_s, nat_s):
    tq = q_ref.shape[0]
    blk = ATT_BLOCK
    sub = 4
    nq = tq // sub
    t = pl.program_id(2)
    slot = t % 2
    other = 1 - slot
    qi = lax.broadcasted_iota(jnp.int32, (blk, 2 * blk), 0)
    kj = lax.broadcasted_iota(jnp.int32, (blk, 2 * blk), 1)
    band = (kj >= qi) & (kj <= qi + blk)
    first_lo = jnp.where(t > 0, 0, blk)
    band_first = band & (kj >= first_lo)

    @pl.when(t == 0)
    def _():
        k4[...] = jnp.zeros(k4.shape, F32)
        v4[...] = jnp.zeros(v4.shape, F32)
        ktail[...] = jnp.zeros(ktail.shape, F32)
        vtail[...] = jnp.zeros(vtail.shape, F32)

    for r in range(sub):
        rows = slice(r * nq, (r + 1) * nq)
        src = pl.ds(r, nq, stride=sub)
        q4[rows, :] = q_ref[src, :]
        k4[slot, rows, :] = k_ref[src, :]
        v4[slot, rows, :] = v_ref[src, :]

    def block_stats(q, kcat, vcat, from_prev_tile):
        s = _mm_nt(q, kcat)
        s = jnp.where(band_first if from_prev_tile else band, s, NEG)
        m = jnp.max(s, axis=-1, keepdims=True)
        p = jnp.exp2(s - m)
        l = jnp.sum(p, axis=-1, keepdims=True)
        return m, l, _mm(p, vcat)

    def merge(dst, m, l, o):
        m_old = m_s[dst, :]
        m_new = jnp.maximum(m_old, m)
        w_old = jnp.exp2(m_old - m_new)
        w_cur = jnp.exp2(m - m_new)
        m_s[dst, :] = m_new
        l_s[dst, :] = w_old * l_s[dst, :] + w_cur * l
        acc_s[dst, :] = w_old * acc_s[dst, :] + w_cur * o

    for jb in range(tq // blk):
        cur = slice(jb * blk, (jb + 1) * blk)
        if jb > 0:
            kcat = k_ref[(jb - 1) * blk:(jb + 1) * blk, :]
            vcat = v_ref[(jb - 1) * blk:(jb + 1) * blk, :]
        else:
            kcat = jnp.concatenate([ktail[...], k_ref[cur, :]], axis=0)
            vcat = jnp.concatenate([vtail[...], v_ref[cur, :]], axis=0)
        m, l, o = block_stats(q_ref[cur, :], kcat, vcat, jb == 0)
        tmp_s[0] = jnp.broadcast_to(m, (blk, HEAD_DIM))
        tmp_s[1] = jnp.broadcast_to(l, (blk, HEAD_DIM))
        tmp_s[2] = o
        per = blk // sub
        for r in range(sub):
            dst = slice(r * nq + jb * per, r * nq + (jb + 1) * per)
            src = pl.ds(r, per, stride=sub)
            m_s[dst, :] = tmp_s[0, src, :]
            l_s[dst, :] = tmp_s[1, src, :]
            acc_s[dst, :] = tmp_s[2, src, :]

    for r in range(sub):
        for jb in range(nq // blk):
            base = r * nq + jb * blk
            cur = slice(base, base + blk)
            if jb > 0:
                kcat = k4[slot, base - blk:base + blk, :]
                vcat = v4[slot, base - blk:base + blk, :]
            else:
                last = slice((r + 1) * nq - blk, (r + 1) * nq)
                kcat = jnp.concatenate([k4[other, last, :], k4[slot, cur, :]], axis=0)
                vcat = jnp.concatenate([v4[other, last, :], v4[slot, cur, :]], axis=0)
            m, l, o = block_stats(q4[cur, :], kcat, vcat, jb == 0)
            merge(cur, m, l, o)

    for c in range(16):
        sl = pl.ds((c % sub) * nq + c // sub, blk, stride=sub)
        kcat = jnp.concatenate([k4[other, sl, :], k4[slot, sl, :]], axis=0)
        vcat = jnp.concatenate([v4[other, sl, :], v4[slot, sl, :]], axis=0)
        m, l, o = block_stats(q4[sl, :], kcat, vcat, True)
        merge(sl, m, l, o)

    out = acc_s[...] / l_s[...]
    out = out * lax.rsqrt(jnp.mean(out * out, axis=-1, keepdims=True) + RMS_EPS) * nw_ref[...]
    for r in range(sub):
        nat_s[pl.ds(r, nq, stride=sub), :] = out[r * nq:(r + 1) * nq]
    o_ref[...] = nat_s[...].astype(o_ref.dtype)
    ktail[...] = k_ref[tq - blk:tq, :]
    vtail[...] = v_ref[tq - blk:tq, :]


def _attn(qb, kb, vb, norm_w, bsz, seq):
    t = qb.shape[0]
    tq = ATT_TILE
    spb = seq // tq
    cur = lambda b, h, i: (b * spb + i, h)
    blk = pl.BlockSpec((tq, HEAD_DIM), cur)
    tile = pltpu.VMEM((tq, HEAD_DIM), F32)
    ring = pltpu.VMEM((2, tq, HEAD_DIM), F32)
    tail = pltpu.VMEM((ATT_BLOCK, HEAD_DIM), F32)
    return pl.pallas_call(
        _attn_kernel,
        out_shape=jax.ShapeDtypeStruct((t, ATT_HEADS * HEAD_DIM), BF16),
        grid=(bsz, ATT_HEADS, spb),
        in_specs=[blk, blk, blk, pl.BlockSpec((1, HEAD_DIM), lambda b, h, i: (0, 0))],
        out_specs=blk,
        scratch_shapes=[tile, ring, ring, tail, tail, tile, tile, tile,
                        pltpu.VMEM((3, ATT_BLOCK, HEAD_DIM), F32), tile],
        compiler_params=_params(("parallel", "parallel", "arbitrary")),
        name="attn",
    )(qb, kb, vb, norm_w.reshape(1, HEAD_DIM))


def _layer_norm(y, g, b):
    mu = jnp.mean(y, axis=-1, keepdims=True)
    yc = y - mu
    var = jnp.mean(yc * yc, axis=-1, keepdims=True)
    return yc * lax.rsqrt(var + LN_EPS) * g + b


def _outproj_kernel(oa_ref, ob_ref, x_ref, mod_ref, wo_ref, g_ref, b_ref,
                    wrh_ref, wrl_ref, br_ref,
                    x1_ref, h2_ref, ri_ref, rg_ref, cnt_ref, run_s, mix_s, hi_s, lo_s):
    @pl.when(pl.program_id(0) == 0)
    def _():
        run_s[...] = jnp.zeros(run_s.shape, F32)

    tm = x_ref.shape[0]
    gate1 = mod_ref[0, 2:3, :]
    shift2 = mod_ref[0, 3:4, :]
    scale2 = mod_ref[0, 4:5, :]
    n_parts = OUTPROJ_PARTS
    part = tm // n_parts
    grp_rows = 2 * SUBLANES

    def project(p):
        rows = slice(p * part, (p + 1) * part)
        mix_s[rows, :] = jnp.dot(jnp.concatenate([oa_ref[rows, :], ob_ref[rows, :]], axis=1), wo_ref[...],
                                 preferred_element_type=F32)

    def normalise(p):
        for r0 in range(p * part, (p + 1) * part, grp_rows):
            rows = slice(r0, r0 + grp_rows)
            x1 = _layer_norm(DEEPNORM_ALPHA * x_ref[rows, :] + (1.0 + gate1) * mix_s[rows, :],
                             g_ref[...], b_ref[...])
            x1_ref[rows, :] = x1
            h2 = x1 * (1.0 + scale2) + shift2
            h2_ref[rows] = h2.reshape(grp_rows, SUBLANES, LANES)
            hi = _bf(h2)
            hi_s[rows, :] = hi
            lo_s[rows, :] = _bf(h2 - hi.astype(F32))

    nt = (((1,), (1,)), ((), ()))

    def route(p):
        rows = slice(p * part, (p + 1) * part)
        hi = hi_s[rows, :]
        by_token = (jnp.dot(hi, wrh_ref[...], preferred_element_type=F32)
                    + jnp.dot(hi, wrl_ref[...], preferred_element_type=F32)
                    + jnp.dot(lo_s[rows, :], wrh_ref[...], preferred_element_type=F32))
        return by_token.T[:ROUTER_ROWS]

    project(0)
    parts = []
    for p in range(n_parts):
        if p + 1 < n_parts:
            project(p + 1)
        normalise(p)
        parts.append(route(p))
    logits = jnp.concatenate(parts, axis=1) + br_ref[:, 0:1]
    nr = logits.shape[0]
    row = lax.broadcasted_iota(jnp.int32, (nr, tm), 0).astype(F32)
    lg = jnp.where(row < N_GROUPS, logits, NEG)
    mg = jnp.max(lg, axis=0, keepdims=True)
    grp = jnp.min(jnp.where(lg == mg, row, float(nr)), axis=0, keepdims=True)
    gate_grp = 1.0 / jnp.sum(jnp.exp(lg - mg), axis=0, keepdims=True)
    first_row = N_GROUPS + EXPERTS_PER_GROUP * grp
    sel = (row >= first_row) & (row < first_row + EXPERTS_PER_GROUP)
    le = jnp.where(sel, logits, NEG)
    v1 = jnp.max(le, axis=0, keepdims=True)
    i1 = jnp.min(jnp.where(le == v1, row, float(nr)), axis=0, keepdims=True)
    le2 = jnp.where(row == i1, NEG, le)
    v2 = jnp.max(le2, axis=0, keepdims=True)
    i2 = jnp.min(jnp.where(le2 == v2, row, float(nr)), axis=0, keepdims=True)
    e21 = jnp.exp(v2 - v1)
    g1 = gate_grp / (1.0 + e21)
    g2 = gate_grp * e21 / (1.0 + e21)

    oh1 = row == i1
    oh2 = row == i2
    onehot = jnp.where(oh1 | oh2, 1.0, 0.0).astype(F32)
    ti = lax.broadcasted_iota(jnp.int32, (tm, tm), 0)
    tj = lax.broadcasted_iota(jnp.int32, (tm, tm), 1)
    earlier = jnp.where(ti < tj, 1.0, 0.0).astype(F32)
    tot = _mm(onehot, earlier) + run_s[:, 0:1]
    r1 = jnp.sum(jnp.where(oh1, tot, 0.0), axis=0, keepdims=True)
    r2 = jnp.sum(jnp.where(oh2, tot, 0.0), axis=0, keepdims=True)
    run_s[...] = run_s[...] + jnp.sum(onehot, axis=1, keepdims=True)
    cnt_ref[...] = run_s[...]

    sub_i = lax.broadcasted_iota(jnp.int32, (SUBLANES, tm), 0)
    ri = jnp.where(sub_i == 0, i1 - N_GROUPS, 0.0)
    ri = jnp.where(sub_i == 1, i2 - N_GROUPS, ri)
    ri = jnp.where(sub_i == 2, r1, ri)
    ri = jnp.where(sub_i == 3, r2, ri)
    ri_ref[...] = ri.astype(jnp.int32)

    gates = jnp.where(sub_i == 0, g1, jnp.where(sub_i == 1, g2, 0.0))
    pick = jnp.where(lax.broadcasted_iota(jnp.int32, (SUBLANES, LANES), 0)
                     == lax.broadcasted_iota(jnp.int32, (SUBLANES, LANES), 1), 1.0, 0.0).astype(BF16)
    tn = (((0,), (0,)), ((), ()))
    rg_ref[...] = sum(lax.dot_general(term, pick, tn, preferred_element_type=F32)
                      for term in _split3(gates))


def _outproj(oa, ob, xf, mod, w_o, ln_g, ln_b, w_rg, b_rg, w_re, b_re, seq):
    t, d = xf.shape
    tm = PROJ_TILE
    gw = oa.shape[1]
    wo = _bf(w_o)
    nr = ROUTER_ROWS
    wr = jnp.zeros((d, LANES), F32).at[:, :N_GROUPS].set(w_rg).at[:, N_GROUPS:N_GROUPS + N_EXPERTS].set(w_re)
    wrh = _bf(wr)
    wrl = _bf(wr - wrh.astype(F32))
    br = jnp.zeros((nr,), F32).at[:N_GROUPS].set(b_rg).at[N_GROUPS:N_GROUPS + N_EXPERTS].set(b_re)
    br = jnp.broadcast_to(br[:, None], (nr, LANES))
    spb = seq // tm
    row = lambda i: (i, 0)
    const = lambda i: (0, 0)
    return pl.pallas_call(
        _outproj_kernel,
        out_shape=(jax.ShapeDtypeStruct((t, d), F32), jax.ShapeDtypeStruct((t, d // LANES, LANES), F32),
                   jax.ShapeDtypeStruct((SUBLANES, t), jnp.int32), jax.ShapeDtypeStruct((t, LANES), F32),
                   jax.ShapeDtypeStruct((nr, LANES), F32)),
        grid=(t // tm,),
        in_specs=[pl.BlockSpec((tm, gw), row), pl.BlockSpec((tm, gw), row), pl.BlockSpec((tm, d), row),
                  pl.BlockSpec((1, 6, d), lambda i: (i // spb, 0, 0)),
                  pl.BlockSpec((2 * gw, d), const),
                  pl.BlockSpec((1, d), const), pl.BlockSpec((1, d), const),
                  pl.BlockSpec((d, LANES), const), pl.BlockSpec((d, LANES), const),
                  pl.BlockSpec((nr, LANES), const)],
        out_specs=(pl.BlockSpec((tm, d), row), pl.BlockSpec((tm, d // LANES, LANES), lambda i: (i, 0, 0)),
                   pl.BlockSpec((SUBLANES, tm), lambda i: (0, i)), pl.BlockSpec((tm, LANES), row),
                   pl.BlockSpec((nr, LANES), const)),
        scratch_shapes=[pltpu.VMEM((nr, LANES), F32), pltpu.VMEM((tm, d), F32),
                        pltpu.VMEM((tm, d), BF16), pltpu.VMEM((tm, d), BF16)],
        compiler_params=_params(("arbitrary",)),
        name="outproj",
    )(oa, ob, xf, mod, wo, ln_g.reshape(1, d), ln_b.reshape(1, d), wrh, wrl, br)


def _dispatch_kernel(d1_ref, d2_ref, h_ref, xs_ref, sem):
    tm = h_ref.shape[0]
    i = pl.program_id(0)

    def row_copy(tk, dest):
        return pltpu.make_async_copy(h_ref.at[tk], xs_ref.at[dest], sem)

    def issue(g, carry):
        for u in range(DMA_UNROLL):
            tk = g * DMA_UNROLL + u
            tok = i * tm + tk
            row_copy(tk, d1_ref[tok]).start(priority=0)
            row_copy(tk, d2_ref[tok]).start(priority=1)
        return carry

    lax.fori_loop(0, tm // DMA_UNROLL, issue, 0)

    tile_copy = pltpu.make_async_copy(h_ref, xs_ref.at[pl.ds(0, tm)], sem)
    tile_copy.wait()
    tile_copy.wait()


def _dispatch(h2, d1, d2):
    t, sub, lanes = h2.shape
    tm = DISPATCH_TILE
    return pl.pallas_call(
        _dispatch_kernel,
        out_shape=jax.ShapeDtypeStruct((2 * t, sub, lanes), F32),
        grid_spec=pltpu.PrefetchScalarGridSpec(
            num_scalar_prefetch=2,
            grid=(t // tm,),
            in_specs=[pl.BlockSpec((tm, sub, lanes), lambda i, *_: (i, 0, 0))],
            out_specs=pl.BlockSpec(memory_space=pl.ANY),
            scratch_shapes=[pltpu.SemaphoreType.DMA]),
        compiler_params=_params(("arbitrary",)),
        name="dispatch",
    )(d1, d2, h2)


def _experts_kernel(wb_ref, we_ref, lo_ref, hi_ref, nw_ref, first_ref, ring_ref, next_ref,
                    xs_ref, wg_hbm, wu_hbm, wd_hbm, ys_ref,
                    wg_f, wu_f, wd_f, wg_s, wu_s, wd_s, sem):
    w = pl.program_id(0)

    def fetch(expert, slot):
        return (pltpu.make_async_copy(wg_hbm.at[expert], wg_f.at[slot], sem.at[slot]),
                pltpu.make_async_copy(wu_hbm.at[expert], wu_f.at[slot], sem.at[slot]),
                pltpu.make_async_copy(wd_hbm.at[expert], wd_f.at[slot], sem.at[slot]))

    @pl.when(w == 0)
    def _():
        for cp in fetch(we_ref[0], 0):
            cp.start()

    @pl.when((first_ref[w] == 1) & (w < nw_ref[0]))
    def _():
        slot = ring_ref[w]
        for cp in fetch(we_ref[w], slot):
            cp.wait()

        @pl.when(next_ref[w] >= 0)
        def _():
            for cp in fetch(next_ref[w], 1 - slot):
                cp.start()

        wg_s[...] = _bf(wg_f[slot])
        wu_s[...] = _bf(wu_f[slot])
        wd_s[...] = _bf(wd_f[slot])

    @pl.when(w < nw_ref[0])
    def _():
        bm, sub, lanes = xs_ref.shape
        x = _bf(xs_ref[...].reshape(bm, sub * lanes))
        hid = (_silu(jnp.dot(x, wg_s[...], preferred_element_type=F32))
               * jnp.dot(x, wu_s[...], preferred_element_type=F32))
        y = jnp.dot(_bf(hid), wd_s[...], preferred_element_type=F32).reshape(bm, sub, lanes)
        row = lax.broadcasted_iota(jnp.int32, (bm, 1, 1), 0)
        mine = (row >= lo_ref[w]) & (row < hi_ref[w])

        @pl.when(lo_ref[w] == 0)
        def _():
            ys_ref[...] = jnp.where(mine, y, 0.0)

        @pl.when(lo_ref[w] > 0)
        def _():
            ys_ref[...] = jnp.where(mine, y, ys_ref[...])


def _experts(xs, item_block, item_expert, item_lo, item_hi, n_items, w_gate, w_up, w_down):
    n_slots, sub, lanes = xs.shape
    d = sub * lanes
    ff = w_gate.shape[2]
    bm = EXPERT_BLOCK
    n = item_block.shape[0]
    idx = jnp.arange(n, dtype=jnp.int32)
    first = jnp.concatenate([jnp.ones((1,), jnp.int32),
                             (item_expert[1:] != item_expert[:-1]).astype(jnp.int32)])
    ring = (jnp.cumsum(first) - 1) % 2
    next_first = lax.cummin(jnp.where(first == 1, idx, n), reverse=True)
    next_first = jnp.concatenate([next_first[1:], jnp.full((1,), n, jnp.int32)])
    nxt = jnp.where(next_first < n, item_expert[jnp.minimum(next_first, n - 1)], -1).astype(jnp.int32)
    slot = lambda w, *_: (_[0][w], 0, 0)
    return pl.pallas_call(
        _experts_kernel,
        out_shape=jax.ShapeDtypeStruct((n_slots, sub, lanes), F32),
        grid_spec=pltpu.PrefetchScalarGridSpec(
            num_scalar_prefetch=8,
            grid=(n,),
            in_specs=[pl.BlockSpec((bm, sub, lanes), slot),
                      pl.BlockSpec(memory_space=pl.ANY), pl.BlockSpec(memory_space=pl.ANY),
                      pl.BlockSpec(memory_space=pl.ANY)],
            out_specs=pl.BlockSpec((bm, sub, lanes), slot),
            scratch_shapes=[pltpu.VMEM((2, d, ff), F32), pltpu.VMEM((2, d, ff), F32),
                            pltpu.VMEM((2, ff, d), F32),
                            pltpu.VMEM((d, ff), BF16), pltpu.VMEM((d, ff), BF16),
                            pltpu.VMEM((ff, d), BF16), pltpu.SemaphoreType.DMA((2,))]),
        compiler_params=_params(("arbitrary",)),
        name="experts",
    )(item_block, item_expert, item_lo, item_hi, n_items, first, ring.astype(jnp.int32), nxt,
      xs, w_gate, w_up, w_down)


def _combine_kernel(d1_ref, d2_ref, ys_ref, rg_ref, x1_ref, mod_ref, g_ref, b_ref, o_ref, ya, yb, sem):
    tm = x1_ref.shape[0]
    i = pl.program_id(0)
    n = pl.num_programs(0)

    def row_copy(dest, buf, slot, tk):
        return pltpu.make_async_copy(ys_ref.at[dest], buf.at[slot, tk], sem.at[slot])

    slot = i % 2
    gate2 = mod_ref[0, 5:6, :]

    def issue_group(step, dst_slot, g):
        for u in range(DMA_UNROLL):
            tk = g * DMA_UNROLL + u
            tok = step * tm + tk
            row_copy(d1_ref[tok], ya, dst_slot, tk).start(priority=0)
            row_copy(d2_ref[tok], yb, dst_slot, tk).start(priority=1)

    def gather_tile(step, dst_slot):
        def issue(g, carry):
            issue_group(step, dst_slot, g)
            return carry

        lax.fori_loop(0, tm // DMA_UNROLL, issue, 0)

    @pl.when(i == 0)
    def _():
        gather_tile(0, 0)

    @pl.when(i + 1 < n)
    def _():
        gather_tile(i + 1, 1 - slot)

    pltpu.make_async_copy(ys_ref.at[pl.ds(0, tm)], ya.at[slot], sem.at[slot]).wait()
    pltpu.make_async_copy(ys_ref.at[pl.ds(0, tm)], yb.at[slot], sem.at[slot]).wait()

    rg = rg_ref[...]
    d = x1_ref.shape[1]
    y = rg[:, 0:1] * ya[slot].reshape(tm, d) + rg[:, 1:2] * yb[slot].reshape(tm, d)
    o_ref[...] = _layer_norm(DEEPNORM_ALPHA * x1_ref[...] + (1.0 + gate2) * y, g_ref[...], b_ref[...])


def _combine(ys, d1, d2, rg, x1, mod, ln_g, ln_b, seq):
    t, d = x1.shape
    tm = ROW_TILE
    spb = seq // tm
    row = lambda i, *_: (i, 0)
    const = lambda i, *_: (0, 0)
    buf = pltpu.VMEM((2, tm) + ys.shape[1:], F32)
    return pl.pallas_call(
        _combine_kernel,
        out_shape=jax.ShapeDtypeStruct((t, d), F32),
        grid_spec=pltpu.PrefetchScalarGridSpec(
            num_scalar_prefetch=2,
            grid=(t // tm,),
            in_specs=[pl.BlockSpec(memory_space=pl.ANY),
                      pl.BlockSpec((tm, LANES), row), pl.BlockSpec((tm, d), row),
                      pl.BlockSpec((1, 6, d), lambda i, *_: (i // spb, 0, 0)),
                      pl.BlockSpec((1, d), const), pl.BlockSpec((1, d), const)],
            out_specs=pl.BlockSpec((tm, d), row),
            scratch_shapes=[buf, buf, pltpu.SemaphoreType.DMA((2,))]),
        compiler_params=_params(("arbitrary",)),
        name="combine",
    )(d1, d2, ys, rg, x1, mod, ln_g.reshape(1, d), ln_b.reshape(1, d))


def _layer(x, c, positions, w_ada, b_ada, w_in, conv_w, a_log, dt_bias, gdn_norm_w, attn_norm_w,
           w_o, ln1_g, ln1_b, w_rg, b_rg, w_re, b_re, w_gate, w_up, w_down, ln2_g, ln2_b):
    bsz, seq, d = x.shape
    t = bsz * seq
    xf = x.reshape(t, d)
    mod = _ada(c, w_ada, b_ada)
    qkv, z, bdc, bdr, qb, kb, vb = _inproj(xf, mod, positions.reshape(t, 1), w_in, conv_w, seq)
    gw = GDN_HEADS * HEAD_DIM
    oa = _gdn(qkv.reshape(bsz, seq, 3 * gw), z.reshape(bsz, seq, gw), bdc.reshape(bsz, seq, LANES), bdr,
              a_log, dt_bias, gdn_norm_w).reshape(t, gw)
    ob = _attn(qb, kb, vb, attn_norm_w, bsz, seq)
    x1, h2, ri, rg, cnt = _outproj(oa, ob, xf, mod, w_o, ln1_g, ln1_b, w_rg, b_rg, w_re, b_re, seq)

    bm = EXPERT_BLOCK
    counts = cnt[N_GROUPS:N_GROUPS + N_EXPERTS, 0].astype(jnp.int32)
    seg_end = jnp.cumsum(counts)
    seg_start = seg_end - counts
    first_blk = seg_start // bm
    n_per = jnp.where(counts > 0, (seg_end - 1) // bm - first_blk + 1, 0)
    item_end = jnp.cumsum(n_per)
    n_items = item_end[-1:]
    max_items = (2 * t) // bm + N_EXPERTS - 1
    w_idx = jnp.minimum(jnp.arange(max_items, dtype=jnp.int32), n_items[0] - 1)
    item_expert = jnp.minimum(jnp.sum(item_end[None, :] <= w_idx[:, None], axis=1), N_EXPERTS - 1).astype(jnp.int32)
    item_block = first_blk[item_expert] + w_idx - (item_end - n_per)[item_expert]
    item_lo = jnp.maximum(seg_start[item_expert] - item_block * bm, 0)
    item_hi = jnp.minimum(seg_end[item_expert] - item_block * bm, bm)
    expert_ids = jnp.arange(N_EXPERTS, dtype=jnp.int32)[:, None]

    def seg_of(e):
        return jnp.sum(jnp.where(e[None, :] == expert_ids, seg_start[:, None], 0), axis=0)

    d1 = seg_of(ri[0]) + ri[2]
    d2 = seg_of(ri[1]) + ri[3]

    xs = _dispatch(h2, d1, d2)
    ys = _experts(xs, item_block, item_expert, item_lo, item_hi, n_items, w_gate, w_up, w_down)
    out = _combine(ys, d1, d2, rg, x1, mod, ln2_g, ln2_b, seq)
    return out.reshape(bsz, seq, d)


def kernel(x, c, positions, w_ada, b_ada, w_in, conv_w, a_log, dt_bias, gdn_norm_w, attn_norm_w, w_o, ln1_g, ln1_b, w_router_group, b_router_group, w_router_expert, b_router_expert, w_gate, w_up, w_down, ln2_g, ln2_b):
    assert w_ada.shape[0] == DEPTH
    return _layer(x, c, positions, w_ada[0], b_ada[0], w_in[0], conv_w[0], a_log[0], dt_bias[0],
                  gdn_norm_w[0], attn_norm_w[0], w_o[0], ln1_g[0], ln1_b[0],
                  w_router_group[0], b_router_group[0], w_router_expert[0], b_router_expert[0],
                  w_gate[0], w_up[0], w_down[0], ln2_g[0], ln2_b[0])
```

```python
import functools
import math

import jax
import jax.numpy as jnp
from jax import lax
from jax.experimental import pallas as pl
from jax.experimental.pallas import tpu as pltpu

F32 = jnp.float32
BF16 = jnp.bfloat16
LOG2E = math.log2(math.e)

GDN_HEADS = 4
ATT_HEADS = 4
HEAD_DIM = 128
CONV_WIDTH = 4
DILATED_PATTERNS = ((128, 1), (512, 4), (2048, 16))
ROPE_THETA = 500000.0
ROPE_DIMS = HEAD_DIM // 4
N_GROUPS = 4
EXPERTS_PER_GROUP = 8
N_EXPERTS = N_GROUPS * EXPERTS_PER_GROUP
DEPTH = 1
DEEPNORM_ALPHA = (2.0 * DEPTH) ** 0.25
LN_EPS = 1e-5
RMS_EPS = 1e-6

LANES = 128
SUBLANES = 8
ROUTER_ROWS = -(-(N_GROUPS + N_EXPERTS) // SUBLANES) * SUBLANES
VMEM_LIMIT = 48 * 1024 * 1024

GDN_BLOCK = 128
GDN_TILE = 256
ATT_BLOCK = 128
ATT_TILE = 2048
ATT_SUB = DILATED_PATTERNS[1][1]
ATT_WIDE = DILATED_PATTERNS[2][1]
assert DILATED_PATTERNS[0][1] == 1 and ATT_WIDE == ATT_SUB * ATT_SUB
assert all(w // d == ATT_BLOCK for w, d in DILATED_PATTERNS) and ATT_TILE == ATT_BLOCK * ATT_WIDE
ADA_COLS = 1024
PROJ_TILE = 512
ROW_TILE = 256
DISPATCH_TILE = 4096
EXPERT_BLOCK = 256
OUTPROJ_PARTS = 2
DMA_UNROLL = 8
NEG = -1e30


def _bf(x):
    return x.astype(BF16)


def _mm(a, b):
    return jnp.dot(_bf(a), _bf(b), preferred_element_type=F32)


def _mm_nt(a, b):
    return lax.dot_general(_bf(a), _bf(b), (((1,), (1,)), ((), ())), preferred_element_type=F32)


def _mm_tn(a, b):
    return lax.dot_general(_bf(a), _bf(b), (((0,), (0,)), ((), ())), preferred_element_type=F32)


def _split3(x):
    x1 = _bf(x)
    r1 = x - x1.astype(F32)
    x2 = _bf(r1)
    return x1, x2, _bf(r1 - x2.astype(F32))


def _sigmoid(x):
    return 1.0 / (1.0 + jnp.exp(-x))


def _silu(x):
    return x * _sigmoid(x)


def _softplus(x):
    return jnp.maximum(x, 0.0) + jnp.log(1.0 + jnp.exp(-jnp.abs(x)))


def _params(sem):
    return pltpu.CompilerParams(dimension_semantics=sem, vmem_limit_bytes=VMEM_LIMIT)


def _ada_kernel(ct_ref, w_ref, b_ref, o_ref, *, bsz):
    sc = _silu(ct_ref[...])
    w = w_ref[...]
    rows = [jnp.sum(w * sc[:, b:b + 1], axis=0, keepdims=True) for b in range(bsz)]
    rows.append(jnp.zeros((o_ref.shape[0] - bsz, w.shape[1]), F32))
    o_ref[...] = jnp.concatenate(rows, axis=0) + b_ref[...]


def _ada(c, w_ada, b_ada):
    bsz, d = c.shape
    n = w_ada.shape[1]
    tn = ADA_COLS
    assert bsz <= SUBLANES
    ct = jnp.zeros((d, LANES), F32).at[:, :bsz].set(c.T)
    out = pl.pallas_call(
        functools.partial(_ada_kernel, bsz=bsz),
        out_shape=jax.ShapeDtypeStruct((SUBLANES, n), F32),
        grid=(n // tn,),
        in_specs=[pl.BlockSpec((d, LANES), lambda j: (0, 0)),
                  pl.BlockSpec((d, tn), lambda j: (0, j)),
                  pl.BlockSpec((1, tn), lambda j: (0, j))],
        out_specs=pl.BlockSpec((SUBLANES, tn), lambda j: (0, j)),
        compiler_params=_params(("parallel",)),
        name="ada",
    )(ct, w_ada, b_ada.reshape(1, n))
    return out[:bsz].reshape(bsz, 6, d)


def _inproj_kernel(x_ref, mod_ref, pos_ref, invf_ref, convw_ref, wqkv_ref, wz_ref, wbd_ref,
                   wq_ref, wk_ref, wv_ref,
                   qkv_ref, z_ref, bdc_ref, bdr_ref, qb_ref, kb_ref, vb_ref, cbuf, *, steps_per_seq):
    tm = x_ref.shape[0]
    halo = SUBLANES
    n_slabs = cbuf.shape[0]

    @pl.when(pl.program_id(0) % steps_per_seq == 0)
    def _():
        cbuf[:, 0:halo, :] = jnp.zeros((n_slabs, halo, LANES), F32)

    half = ROPE_DIMS // 2
    groups = LANES // half
    lane = lax.broadcasted_iota(jnp.int32, (1, LANES), 1)
    first = lane < half
    rot = lane < ROPE_DIMS
    ang = pos_ref[0].astype(F32) * invf_ref[...]
    cos_c = jnp.cos(ang)
    sin_c = jnp.sin(ang)
    cos_parts, sin_parts = [], []
    for j in range(groups):
        lo_sh = (LANES - half * j) % LANES
        hi_sh = (LANES - half * j + half) % LANES
        c_lo = pltpu.roll(cos_c, lo_sh, 1) if lo_sh else cos_c
        c_hi = pltpu.roll(cos_c, hi_sh, 1) if hi_sh else cos_c
        s_lo = pltpu.roll(sin_c, lo_sh, 1) if lo_sh else sin_c
        s_hi = pltpu.roll(sin_c, hi_sh, 1) if hi_sh else sin_c
        cos_parts.append(jnp.where(first, c_lo, jnp.where(rot, c_hi, 1.0)))
        sin_parts.append(jnp.where(first, -s_lo, jnp.where(rot, s_hi, 0.0)))
    cosv = jnp.concatenate(cos_parts, axis=0)
    sin_signed = jnp.concatenate(sin_parts, axis=0)

    def rope(y):
        outs = []
        for hh in range(ATT_HEADS):
            yh = y[:, hh * HEAD_DIM:(hh + 1) * HEAD_DIM]
            partner = jnp.where(first, pltpu.roll(yh, LANES - half, 1), pltpu.roll(yh, half, 1))
            outs.append(yh * cosv + partner * sin_signed)
        return jnp.concatenate(outs, axis=1)

    def conv_slice(s):
        sl = slice(s * HEAD_DIM, (s + 1) * HEAD_DIM)
        off = halo - (CONV_WIDTH - 1)
        acc = convw_ref[0:1, sl] * cbuf[s, off:off + tm, :]
        for j in range(1, CONV_WIDTH):
            acc = acc + convw_ref[j:j + 1, sl] * cbuf[s, off + j:off + j + tm, :]
        cbuf[s, 0:halo, :] = cbuf[s, tm:tm + halo, :]
        y = _silu(acc)
        if s < 2 * GDN_HEADS:
            y = y * lax.rsqrt(jnp.sum(y * y, axis=-1, keepdims=True) + RMS_EPS)
        if s < GDN_HEADS:
            y = y * (HEAD_DIM ** -0.5)
        qkv_ref[:, sl] = y

    shift = mod_ref[0, 0:1, :]
    scale = mod_ref[0, 1:2, :]
    h = _bf(x_ref[...] * (1.0 + scale) + shift)
    chunk = 2 * HEAD_DIM
    n_chunks = n_slabs // 2

    def project_chunk(c):
        pre = jnp.dot(h, wqkv_ref[:, c * chunk:(c + 1) * chunk], preferred_element_type=F32)
        cbuf[2 * c, halo:halo + tm, :] = pre[:, :HEAD_DIM]
        cbuf[2 * c + 1, halo:halo + tm, :] = pre[:, HEAD_DIM:]

    def conv_chunk(c):
        conv_slice(2 * c)
        conv_slice(2 * c + 1)

    project_chunk(0)
    for c in range(1, n_chunks):
        project_chunk(c)
        conv_chunk(c - 1)
    qb = jnp.dot(h, wq_ref[...], preferred_element_type=F32)
    conv_chunk(n_chunks - 1)
    kb = jnp.dot(h, wk_ref[...], preferred_element_type=F32)
    qb_ref[...] = rope(qb) * (HEAD_DIM ** -0.5 * LOG2E)
    vb_ref[...] = jnp.dot(h, wv_ref[...], preferred_element_type=F32)
    kb_ref[...] = rope(kb)
    z_ref[...] = jnp.dot(h, wz_ref[...], preferred_element_type=F32)
    bdc = jnp.dot(h, wbd_ref[...], preferred_element_type=F32)
    bdc_ref[...] = bdc
    bdr_ref[0] = bdc.T[:SUBLANES]


def _inproj(xf, mod, pos, w_in, conv_w, seq):
    t, d = xf.shape
    tm = PROJ_TILE
    gw = GDN_HEADS * HEAD_DIM
    aw = ATT_HEADS * HEAD_DIM
    o0 = 3 * gw
    o1 = o0 + gw
    o2 = o1 + 2 * GDN_HEADS
    wb = _bf(w_in)
    wqkv, wz = wb[:, :o0], wb[:, o0:o1]
    wbd_n = wb[:, o1:o2]
    wbd = jnp.zeros((d, LANES), BF16).at[:, :2 * GDN_HEADS].set(wbd_n)
    wq, wk, wv = wb[:, o2:o2 + aw], wb[:, o2 + aw:o2 + 2 * aw], wb[:, o2 + 2 * aw:o2 + 3 * aw]
    half = ROPE_DIMS // 2
    groups = LANES // half
    inv_freq = ROPE_THETA ** (-jnp.arange(half, dtype=F32) * 2.0 / ROPE_DIMS)
    invf = jnp.tile(inv_freq, groups).reshape(1, LANES)
    pos = jnp.repeat(pos.reshape(t // tm, groups, tm // groups).transpose(0, 2, 1), half, axis=2)
    spb = seq // tm
    row = lambda i: (i, 0)
    const = lambda i: (0, 0)
    return pl.pallas_call(
        functools.partial(_inproj_kernel, steps_per_seq=spb),
        out_shape=(jax.ShapeDtypeStruct((t, o0), F32), jax.ShapeDtypeStruct((t, gw), F32),
                   jax.ShapeDtypeStruct((t, LANES), F32), jax.ShapeDtypeStruct((t // seq, SUBLANES, seq), F32),
                   jax.ShapeDtypeStruct((t, aw), F32), jax.ShapeDtypeStruct((t, aw), F32),
                   jax.ShapeDtypeStruct((t, aw), F32)),
        grid=(t // tm,),
        in_specs=[pl.BlockSpec((tm, d), row),
                  pl.BlockSpec((1, 6, d), lambda i: (i // spb, 0, 0)),
                  pl.BlockSpec((1, tm // groups, LANES), lambda i: (i, 0, 0)),
                  pl.BlockSpec((1, LANES), const), pl.BlockSpec((CONV_WIDTH, o0), const),
                  pl.BlockSpec((d, o0), const), pl.BlockSpec((d, gw), const),
                  pl.BlockSpec((d, LANES), const),
                  pl.BlockSpec((d, aw), const), pl.BlockSpec((d, aw), const),
                  pl.BlockSpec((d, aw), const)],
        out_specs=(pl.BlockSpec((tm, o0), row), pl.BlockSpec((tm, gw), row),
                   pl.BlockSpec((tm, LANES), row),
                   pl.BlockSpec((1, SUBLANES, tm), lambda i: (i // spb, 0, i % spb)),
                   pl.BlockSpec((tm, aw), row), pl.BlockSpec((tm, aw), row),
                   pl.BlockSpec((tm, aw), row)),
        scratch_shapes=[pltpu.VMEM((o0 // HEAD_DIM, tm + 2 * SUBLANES, HEAD_DIM), F32)],
        compiler_params=_params(("arbitrary",)),
        name="inproj",
    )(xf, mod, pos, invf, conv_w, wqkv, wz, wbd, wq, wk, wv)


def _gdn_kernel(qkv_ref, z_ref, bdc_ref, bdr_ref, gpc_ref, gpr_ref, nw_ref, o_ref,
                u_s, w_s, qd_s, kd_s, qk_s, a_s, x_s, y_s, st_ref):
    nbat, ts = qkv_ref.shape[0], qkv_ref.shape[1]
    nb = ts // GDN_BLOCK
    gw = GDN_HEADS * HEAD_DIM

    @pl.when(pl.program_id(0) == 0)
    def _():
        st_ref[...] = jnp.zeros(st_ref.shape, F32)

    blk_n = GDN_BLOCK
    hd = HEAD_DIM
    ti = lax.broadcasted_iota(jnp.int32, (ts, ts), 0)
    tj = lax.broadcasted_iota(jnp.int32, (ts, ts), 1)
    same = (ti // blk_n) == (tj // blk_n)
    m_low = jnp.where(same & (tj <= ti), 1.0, 0.0).astype(BF16)
    m_up = jnp.where(same & (ti <= tj), 1.0, 0.0).astype(BF16)

    split3 = _split3

    beta_c = [_sigmoid(bdc_ref[bb]) for bb in range(nbat)]
    g_c = jnp.concatenate([-jnp.exp(gpc_ref[0:1, :]) * LOG2E * _softplus(bdc_ref[bb] + gpc_ref[1:2, :])
                           for bb in range(nbat)], axis=1)
    g_r = jnp.concatenate([-jnp.exp(gpr_ref[:, 0:1]) * LOG2E * _softplus(bdr_ref[bb] + gpr_ref[:, 1:2])
                           for bb in range(nbat)], axis=0)
    gc_c = sum(jnp.dot(m_low, part, preferred_element_type=F32) for part in split3(g_c))
    gc_r = sum(jnp.dot(part, m_up, preferred_element_type=F32) for part in split3(g_r))

    def cat2(m):
        return jnp.concatenate([m, m], axis=1)

    ii = lax.broadcasted_iota(jnp.int32, (blk_n, blk_n), 0)
    jj = lax.broadcasted_iota(jnp.int32, (blk_n, blk_n), 1)
    lower = cat2(jj <= ii)
    strict = cat2(jj < ii)
    eye = cat2(jnp.where(ii == jj, 1.0, 0.0).astype(F32))
    levels = []
    b = 1
    while b < blk_n:
        levels.append(cat2(((ii // b) == (jj // b) + 1) & (((jj // b) % 2) == 0)))
        b *= 2

    def block_diag(rp):
        n, m = rp.shape[0], rp.shape[1] // 2
        z = jnp.zeros((n, m), rp.dtype)
        return jnp.concatenate([jnp.concatenate([rp[:, :m], z], axis=1),
                                jnp.concatenate([z, rp[:, m:]], axis=1)], axis=0)

    def mm2(lp, rp):
        return jnp.dot(_bf(lp), block_diag(_bf(rp)), preferred_element_type=F32)

    def mm2_nt(lp, rp):
        return lax.dot_general(_bf(lp), block_diag(_bf(rp)), (((1,), (1,)), ((), ())),
                               preferred_element_type=F32)

    chains = [(bb, j, pp) for bb in range(nbat) for j in range(nb) for pp in range(GDN_HEADS // 2)]

    def tile_of(bb, j, pp):
        return bb, slice(j * blk_n, (j + 1) * blk_n), slice(2 * pp * hd, 2 * (pp + 1) * hd)

    def col_pair(arr, rows, col):
        return jnp.concatenate([jnp.broadcast_to(arr[rows, col:col + 1], (blk_n, hd)),
                                jnp.broadcast_to(arr[rows, col + 1:col + 2], (blk_n, hd))], axis=1)

    for c, (bb, j, pp) in enumerate(chains):
        blk = tile_of(bb, j, pp)
        rows, cols = blk[1], blk[2]
        last = slice((j + 1) * blk_n - 1, (j + 1) * blk_n)
        gcol = bb * LANES + GDN_HEADS + 2 * pp
        grow = bb * SUBLANES + GDN_HEADS + 2 * pp
        q = qkv_ref[bb, rows, cols]
        k = qkv_ref[bb, rows, slice(gw + cols.start, gw + cols.stop)]
        v = qkv_ref[bb, rows, slice(2 * gw + cols.start, 2 * gw + cols.stop)]
        beta = col_pair(beta_c[bb], rows, 2 * pp)
        gcc = col_pair(gc_c, rows, gcol)
        gtc = col_pair(gc_c, last, gcol)
        gcr = jnp.concatenate([jnp.broadcast_to(gc_r[grow:grow + 1, rows], (blk_n, hd)),
                               jnp.broadcast_to(gc_r[grow + 1:grow + 2, rows], (blk_n, hd))], axis=1)
        kb = k * beta
        eg = jnp.exp2(gcc)
        dm = jnp.where(lower, jnp.exp2(gcc - gcr), 0.0)
        a = jnp.where(strict, mm2_nt(kb, k) * dm, 0.0)
        a_s[c] = a
        x_s[c] = eye - jnp.where(levels[0], a, 0.0)
        u_s[blk] = v * beta
        w_s[blk] = kb * eg
        qd_s[blk] = q * eg
        kd_s[blk] = k * jnp.exp2(gtc - gcc)
        qk_s[blk] = mm2_nt(q, k) * dm

    for lm in levels[1:]:
        for c in range(len(chains)):
            y_s[c] = mm2(x_s[c], jnp.where(lm, a_s[c], 0.0))
        for c in range(len(chains)):
            xc = x_s[c]
            x_s[c] = xc - mm2(y_s[c], xc)

    for c, (bb, j, pp) in enumerate(chains):
        blk = tile_of(bb, j, pp)
        u, w = _bf(u_s[blk]), _bf(w_s[blk])
        z = jnp.zeros((blk_n, 2 * hd), BF16)
        rhs = jnp.concatenate([jnp.concatenate([u[:, :hd], w[:, :hd], z], axis=1),
                               jnp.concatenate([z, u[:, hd:], w[:, hd:]], axis=1)], axis=0)
        sol = jnp.dot(_bf(x_s[c]), rhs, preferred_element_type=F32)
        u_s[blk] = jnp.concatenate([sol[:, 0:hd], sol[:, 2 * hd:3 * hd]], axis=1)
        w_s[blk] = jnp.concatenate([sol[:, hd:2 * hd], sol[:, 3 * hd:]], axis=1)

    for j in range(nb):
        for bb in range(nbat):
            for pp in range(GDN_HEADS // 2):
                blk = tile_of(bb, j, pp)
                si = bb * (GDN_HEADS // 2) + pp
                last = slice((j + 1) * blk_n - 1, (j + 1) * blk_n)
                gcol = bb * LANES + GDN_HEADS + 2 * pp
                state = st_ref[si]
                proj = mm2(jnp.concatenate([w_s[blk], qd_s[blk]], axis=0), state)
                v_new = u_s[blk] - proj[:blk_n]
                o = proj[blk_n:] + mm2(qk_s[blk], v_new)
                g_last = jnp.exp2(jnp.concatenate(
                    [jnp.broadcast_to(gc_c[last, gcol:gcol + 1], (1, hd)),
                     jnp.broadcast_to(gc_c[last, gcol + 1:gcol + 2], (1, hd))], axis=1))
                kd = kd_s[blk]
                upd = jnp.concatenate([_mm_tn(kd[:, :hd], v_new[:, :hd]),
                                       _mm_tn(kd[:, hd:], v_new[:, hd:])], axis=1)
                st_ref[si] = state * g_last + upd
                zz = z_ref[blk]
                halves = []
                for hf in range(2):
                    oh = o[:, hf * hd:(hf + 1) * hd]
                    halves.append(oh * lax.rsqrt(jnp.mean(oh * oh, axis=-1, keepdims=True) + RMS_EPS)
                                  * nw_ref[...])
                o_ref[blk] = (jnp.concatenate(halves, axis=1) * _silu(zz)).astype(o_ref.dtype)


def _gdn(qkv, z, bdc, bdr, a_log, dt_bias, norm_w):
    bsz, seq, _ = qkv.shape
    ts = GDN_TILE
    gw = GDN_HEADS * HEAD_DIM
    zeros4 = jnp.zeros((GDN_HEADS,), F32)
    al = jnp.concatenate([zeros4, a_log])
    db = jnp.concatenate([zeros4, dt_bias])
    gpc = jnp.zeros((2, LANES), F32).at[0, :2 * GDN_HEADS].set(al).at[1, :2 * GDN_HEADS].set(db)
    gpr = jnp.stack([al, db], axis=1)
    row = lambda i: (0, i, 0)
    const = lambda i: (0, 0)
    n_chains = bsz * (ts // GDN_BLOCK) * (GDN_HEADS // 2)
    tile = pltpu.VMEM((bsz, ts, gw), F32)
    mats = pltpu.VMEM((n_chains, GDN_BLOCK, 2 * GDN_BLOCK), F32)
    return pl.pallas_call(
        _gdn_kernel,
        out_shape=jax.ShapeDtypeStruct((bsz, seq, gw), BF16),
        grid=(seq // ts,),
        in_specs=[pl.BlockSpec((bsz, ts, 3 * gw), row), pl.BlockSpec((bsz, ts, gw), row),
                  pl.BlockSpec((bsz, ts, LANES), row),
                  pl.BlockSpec((bsz, SUBLANES, ts), lambda i: (0, 0, i)),
                  pl.BlockSpec((2, LANES), const), pl.BlockSpec((2 * GDN_HEADS, 2), const),
                  pl.BlockSpec((1, HEAD_DIM), const)],
        out_specs=pl.BlockSpec((bsz, ts, gw), row),
        scratch_shapes=[tile, tile, tile, tile, tile,
                        mats, mats, mats,
                        pltpu.VMEM((bsz * GDN_HEADS // 2, HEAD_DIM, 2 * HEAD_DIM), F32)],
        compiler_params=_params(("arbitrary",)),
        name="gdn",
    )(qkv, z, bdc, bdr, gpc, gpr, norm_w.reshape(1, HEAD_DIM))


def _attn_kernel(q_ref, k_ref, v_ref, nw_ref, o_ref,
                 q4, k4, v4, ktail, vtail, m_s, l_s, acc_s, tmp_s, nat_s):
    tq = q_ref.shape[0]
    blk = ATT_BLOCK
    sub = ATT_SUB
    nq = tq // sub
    t = pl.program_id(2)
    slot = t % 2
    other = 1 - slot
    qi = lax.broadcasted_iota(jnp.int32, (blk, 2 * blk), 0)
    kj = lax.broadcasted_iota(jnp.int32, (blk, 2 * blk), 1)
    band = (kj >= qi) & (kj <= qi + blk)
    first_lo = jnp.where(t > 0, 0, blk)
    band_first = band & (kj >= first_lo)

    @pl.when(t == 0)
    def _():
        k4[...] = jnp.zeros(k4.shape, F32)
        v4[...] = jnp.zeros(v4.shape, F32)
        ktail[...] = jnp.zeros(ktail.shape, F32)
        vtail[...] = jnp.zeros(vtail.shape, F32)

    for r in range(sub):
        rows = slice(r * nq, (r + 1) * nq)
        src = pl.ds(r, nq, stride=sub)
        q4[rows, :] = q_ref[src, :]
        k4[slot, rows, :] = k_ref[src, :]
        v4[slot, rows, :] = v_ref[src, :]

    def block_stats(q, kcat, vcat, from_prev_tile):
        s = _mm_nt(q, kcat)
        s = jnp.where(band_first if from_prev_tile else band, s, NEG)
        m = jnp.max(s, axis=-1, keepdims=True)
        p = jnp.exp2(s - m)
        l = jnp.sum(p, axis=-1, keepdims=True)
        return m, l, _mm(p, vcat)

    def merge(dst, m, l, o):
        m_old = m_s[dst, :]
        m_new = jnp.maximum(m_old, m)
        w_old = jnp.exp2(m_old - m_new)
        w_cur = jnp.exp2(m - m_new)
        m_s[dst, :] = m_new
        l_s[dst, :] = w_old * l_s[dst, :] + w_cur * l
        acc_s[dst, :] = w_old * acc_s[dst, :] + w_cur * o

    for jb in range(tq // blk):
        cur = slice(jb * blk, (jb + 1) * blk)
        if jb > 0:
            kcat = k_ref[(jb - 1) * blk:(jb + 1) * blk, :]
            vcat = v_ref[(jb - 1) * blk:(jb + 1) * blk, :]
        else:
            kcat = jnp.concatenate([ktail[...], k_ref[cur, :]], axis=0)
            vcat = jnp.concatenate([vtail[...], v_ref[cur, :]], axis=0)
        m, l, o = block_stats(q_ref[cur, :], kcat, vcat, jb == 0)
        tmp_s[0] = jnp.broadcast_to(m, (blk, HEAD_DIM))
        tmp_s[1] = jnp.broadcast_to(l, (blk, HEAD_DIM))
        tmp_s[2] = o
        per = blk // sub
        for r in range(sub):
            dst = slice(r * nq + jb * per, r * nq + (jb + 1) * per)
            src = pl.ds(r, per, stride=sub)
            m_s[dst, :] = tmp_s[0, src, :]
            l_s[dst, :] = tmp_s[1, src, :]
            acc_s[dst, :] = tmp_s[2, src, :]

    for r in range(sub):
        for jb in range(nq // blk):
            base = r * nq + jb * blk
            cur = slice(base, base + blk)
            if jb > 0:
                kcat = k4[slot, base - blk:base + blk, :]
                vcat = v4[slot, base - blk:base + blk, :]
            else:
                last = slice((r + 1) * nq - blk, (r + 1) * nq)
                kcat = jnp.concatenate([k4[other, last, :], k4[slot, cur, :]], axis=0)
                vcat = jnp.concatenate([v4[other, last, :], v4[slot, cur, :]], axis=0)
            m, l, o = block_stats(q4[cur, :], kcat, vcat, jb == 0)
            merge(cur, m, l, o)

    for c in range(ATT_WIDE):
        sl = pl.ds((c % sub) * nq + c // sub, blk, stride=sub)
        kcat = jnp.concatenate([k4[other, sl, :], k4[slot, sl, :]], axis=0)
        vcat = jnp.concatenate([v4[other, sl, :], v4[slot, sl, :]], axis=0)
        m, l, o = block_stats(q4[sl, :], kcat, vcat, True)
        merge(sl, m, l, o)

    out = acc_s[...] / l_s[...]
    out = out * lax.rsqrt(jnp.mean(out * out, axis=-1, keepdims=True) + RMS_EPS) * nw_ref[...]
    for r in range(sub):
        nat_s[pl.ds(r, nq, stride=sub), :] = out[r * nq:(r + 1) * nq]
    o_ref[...] = nat_s[...].astype(o_ref.dtype)
    ktail[...] = k_ref[tq - blk:tq, :]
    vtail[...] = v_ref[tq - blk:tq, :]


def _attn(qb, kb, vb, norm_w, bsz, seq):
    t = qb.shape[0]
    tq = ATT_TILE
    spb = seq // tq
    cur = lambda b, h, i: (b * spb + i, h)
    blk = pl.BlockSpec((tq, HEAD_DIM), cur)
    tile = pltpu.VMEM((tq, HEAD_DIM), F32)
    ring = pltpu.VMEM((2, tq, HEAD_DIM), F32)
    tail = pltpu.VMEM((ATT_BLOCK, HEAD_DIM), F32)
    return pl.pallas_call(
        _attn_kernel,
        out_shape=jax.ShapeDtypeStruct((t, ATT_HEADS * HEAD_DIM), BF16),
        grid=(bsz, ATT_HEADS, spb),
        in_specs=[blk, blk, blk, pl.BlockSpec((1, HEAD_DIM), lambda b, h, i: (0, 0))],
        out_specs=blk,
        scratch_shapes=[tile, ring, ring, tail, tail, tile, tile, tile,
                        pltpu.VMEM((3, ATT_BLOCK, HEAD_DIM), F32), tile],
        compiler_params=_params(("parallel", "parallel", "arbitrary")),
        name="attn",
    )(qb, kb, vb, norm_w.reshape(1, HEAD_DIM))


def _layer_norm(y, g, b):
    mu = jnp.mean(y, axis=-1, keepdims=True)
    yc = y - mu
    var = jnp.mean(yc * yc, axis=-1, keepdims=True)
    return yc * lax.rsqrt(var + LN_EPS) * g + b


def _outproj_kernel(oa_ref, ob_ref, x_ref, mod_ref, wo_ref, g_ref, b_ref,
                    wrh_ref, wrl_ref, br_ref,
                    x1_ref, h2_ref, ri_ref, rg_ref, cnt_ref, run_s, mix_s, hi_s, lo_s):
    @pl.when(pl.program_id(0) == 0)
    def _():
        run_s[...] = jnp.zeros(run_s.shape, F32)

    tm = x_ref.shape[0]
    gate1 = mod_ref[0, 2:3, :]
    shift2 = mod_ref[0, 3:4, :]
    scale2 = mod_ref[0, 4:5, :]
    n_parts = OUTPROJ_PARTS
    part = tm // n_parts
    grp_rows = 2 * SUBLANES

    def project(p):
        rows = slice(p * part, (p + 1) * part)
        mix_s[rows, :] = jnp.dot(jnp.concatenate([oa_ref[rows, :], ob_ref[rows, :]], axis=1), wo_ref[...],
                                 preferred_element_type=F32)

    def normalise(p):
        for r0 in range(p * part, (p + 1) * part, grp_rows):
            rows = slice(r0, r0 + grp_rows)
            x1 = _layer_norm(DEEPNORM_ALPHA * x_ref[rows, :] + (1.0 + gate1) * mix_s[rows, :],
                             g_ref[...], b_ref[...])
            x1_ref[rows, :] = x1
            h2 = x1 * (1.0 + scale2) + shift2
            h2_ref[rows] = h2.reshape(grp_rows, SUBLANES, LANES)
            hi = _bf(h2)
            hi_s[rows, :] = hi
            lo_s[rows, :] = _bf(h2 - hi.astype(F32))

    nt = (((1,), (1,)), ((), ()))

    def route(p):
        rows = slice(p * part, (p + 1) * part)
        hi = hi_s[rows, :]
        by_token = (jnp.dot(hi, wrh_ref[...], preferred_element_type=F32)
                    + jnp.dot(hi, wrl_ref[...], preferred_element_type=F32)
                    + jnp.dot(lo_s[rows, :], wrh_ref[...], preferred_element_type=F32))
        return by_token.T[:ROUTER_ROWS]

    project(0)
    parts = []
    for p in range(n_parts):
        if p + 1 < n_parts:
            project(p + 1)
        normalise(p)
        parts.append(route(p))
    logits = jnp.concatenate(parts, axis=1) + br_ref[:, 0:1]
    nr = logits.shape[0]
    row = lax.broadcasted_iota(jnp.int32, (nr, tm), 0).astype(F32)
    lg = jnp.where(row < N_GROUPS, logits, NEG)
    mg = jnp.max(lg, axis=0, keepdims=True)
    grp = jnp.min(jnp.where(lg == mg, row, float(nr)), axis=0, keepdims=True)
    gate_grp = 1.0 / jnp.sum(jnp.exp(lg - mg), axis=0, keepdims=True)
    first_row = N_GROUPS + EXPERTS_PER_GROUP * grp
    sel = (row >= first_row) & (row < first_row + EXPERTS_PER_GROUP)
    le = jnp.where(sel, logits, NEG)
    v1 = jnp.max(le, axis=0, keepdims=True)
    i1 = jnp.min(jnp.where(le == v1, row, float(nr)), axis=0, keepdims=True)
    le2 = jnp.where(row == i1, NEG, le)
    v2 = jnp.max(le2, axis=0, keepdims=True)
    i2 = jnp.min(jnp.where(le2 == v2, row, float(nr)), axis=0, keepdims=True)
    e21 = jnp.exp(v2 - v1)
    g1 = gate_grp / (1.0 + e21)
    g2 = gate_grp * e21 / (1.0 + e21)

    oh1 = row == i1
    oh2 = row == i2
    onehot = jnp.where(oh1 | oh2, 1.0, 0.0).astype(F32)
    ti = lax.broadcasted_iota(jnp.int32, (tm, tm), 0)
    tj = lax.broadcasted_iota(jnp.int32, (tm, tm), 1)
    earlier = jnp.where(ti < tj, 1.0, 0.0).astype(F32)
    tot = _mm(onehot, earlier) + run_s[:, 0:1]
    r1 = jnp.sum(jnp.where(oh1, tot, 0.0), axis=0, keepdims=True)
    r2 = jnp.sum(jnp.where(oh2, tot, 0.0), axis=0, keepdims=True)
    run_s[...] = run_s[...] + jnp.sum(onehot, axis=1, keepdims=True)
    cnt_ref[...] = run_s[...]

    sub_i = lax.broadcasted_iota(jnp.int32, (SUBLANES, tm), 0)
    ri = jnp.where(sub_i == 0, i1 - N_GROUPS, 0.0)
    ri = jnp.where(sub_i == 1, i2 - N_GROUPS, ri)
    ri = jnp.where(sub_i == 2, r1, ri)
    ri = jnp.where(sub_i == 3, r2, ri)
    ri_ref[...] = ri.astype(jnp.int32)

    gates = jnp.where(sub_i == 0, g1, jnp.where(sub_i == 1, g2, 0.0))
    pick = jnp.where(lax.broadcasted_iota(jnp.int32, (SUBLANES, LANES), 0)
                     == lax.broadcasted_iota(jnp.int32, (SUBLANES, LANES), 1), 1.0, 0.0).astype(BF16)
    tn = (((0,), (0,)), ((), ()))
    rg_ref[...] = sum(lax.dot_general(term, pick, tn, preferred_element_type=F32)
                      for term in _split3(gates))


def _outproj(oa, ob, xf, mod, w_o, ln_g, ln_b, w_rg, b_rg, w_re, b_re, seq):
    t, d = xf.shape
    tm = PROJ_TILE
    gw = oa.shape[1]
    wo = _bf(w_o)
    nr = ROUTER_ROWS
    wr = jnp.zeros((d, LANES), F32).at[:, :N_GROUPS].set(w_rg).at[:, N_GROUPS:N_GROUPS + N_EXPERTS].set(w_re)
    wrh = _bf(wr)
    wrl = _bf(wr - wrh.astype(F32))
    br = jnp.zeros((nr,), F32).at[:N_GROUPS].set(b_rg).at[N_GROUPS:N_GROUPS + N_EXPERTS].set(b_re)
    br = jnp.broadcast_to(br[:, None], (nr, LANES))
    spb = seq // tm
    row = lambda i: (i, 0)
    const = lambda i: (0, 0)
    return pl.pallas_call(
        _outproj_kernel,
        out_shape=(jax.ShapeDtypeStruct((t, d), F32), jax.ShapeDtypeStruct((t, d // LANES, LANES), F32),
                   jax.ShapeDtypeStruct((SUBLANES, t), jnp.int32), jax.ShapeDtypeStruct((t, LANES), F32),
                   jax.ShapeDtypeStruct((nr, LANES), F32)),
        grid=(t // tm,),
        in_specs=[pl.BlockSpec((tm, gw), row), pl.BlockSpec((tm, gw), row), pl.BlockSpec((tm, d), row),
                  pl.BlockSpec((1, 6, d), lambda i: (i // spb, 0, 0)),
                  pl.BlockSpec((2 * gw, d), const),
                  pl.BlockSpec((1, d), const), pl.BlockSpec((1, d), const),
                  pl.BlockSpec((d, LANES), const), pl.BlockSpec((d, LANES), const),
                  pl.BlockSpec((nr, LANES), const)],
        out_specs=(pl.BlockSpec((tm, d), row), pl.BlockSpec((tm, d // LANES, LANES), lambda i: (i, 0, 0)),
                   pl.BlockSpec((SUBLANES, tm), lambda i: (0, i)), pl.BlockSpec((tm, LANES), row),
                   pl.BlockSpec((nr, LANES), const)),
        scratch_shapes=[pltpu.VMEM((nr, LANES), F32), pltpu.VMEM((tm, d), F32),
                        pltpu.VMEM((tm, d), BF16), pltpu.VMEM((tm, d), BF16)],
        compiler_params=_params(("arbitrary",)),
        name="outproj",
    )(oa, ob, xf, mod, wo, ln_g.reshape(1, d), ln_b.reshape(1, d), wrh, wrl, br)


def _dispatch_kernel(d1_ref, d2_ref, h_ref, xs_ref, sem):
    tm = h_ref.shape[0]
    i = pl.program_id(0)

    def row_copy(tk, dest):
        return pltpu.make_async_copy(h_ref.at[tk], xs_ref.at[dest], sem)

    def issue(g, carry):
        for u in range(DMA_UNROLL):
            tk = g * DMA_UNROLL + u
            tok = i * tm + tk
            row_copy(tk, d1_ref[tok]).start(priority=0)
            row_copy(tk, d2_ref[tok]).start(priority=1)
        return carry

    lax.fori_loop(0, tm // DMA_UNROLL, issue, 0)

    tile_copy = pltpu.make_async_copy(h_ref, xs_ref.at[pl.ds(0, tm)], sem)
    tile_copy.wait()
    tile_copy.wait()


def _dispatch(h2, d1, d2):
    t, sub, lanes = h2.shape
    tm = DISPATCH_TILE
    return pl.pallas_call(
        _dispatch_kernel,
        out_shape=jax.ShapeDtypeStruct((2 * t, sub, lanes), F32),
        grid_spec=pltpu.PrefetchScalarGridSpec(
            num_scalar_prefetch=2,
            grid=(t // tm,),
            in_specs=[pl.BlockSpec((tm, sub, lanes), lambda i, *_: (i, 0, 0))],
            out_specs=pl.BlockSpec(memory_space=pl.ANY),
            scratch_shapes=[pltpu.SemaphoreType.DMA]),
        compiler_params=_params(("arbitrary",)),
        name="dispatch",
    )(d1, d2, h2)


def _experts_kernel(wb_ref, we_ref, lo_ref, hi_ref, nw_ref, first_ref, ring_ref, next_ref,
                    xs_ref, wg_hbm, wu_hbm, wd_hbm, ys_ref,
                    wg_f, wu_f, wd_f, wg_s, wu_s, wd_s, sem):
    w = pl.program_id(0)

    def fetch(expert, slot):
        return (pltpu.make_async_copy(wg_hbm.at[expert], wg_f.at[slot], sem.at[slot]),
                pltpu.make_async_copy(wu_hbm.at[expert], wu_f.at[slot], sem.at[slot]),
                pltpu.make_async_copy(wd_hbm.at[expert], wd_f.at[slot], sem.at[slot]))

    @pl.when(w == 0)
    def _():
        for cp in fetch(we_ref[0], 0):
            cp.start()

    @pl.when((first_ref[w] == 1) & (w < nw_ref[0]))
    def _():
        slot = ring_ref[w]
        for cp in fetch(we_ref[w], slot):
            cp.wait()

        @pl.when(next_ref[w] >= 0)
        def _():
            for cp in fetch(next_ref[w], 1 - slot):
                cp.start()

        wg_s[...] = _bf(wg_f[slot])
        wu_s[...] = _bf(wu_f[slot])
        wd_s[...] = _bf(wd_f[slot])

    bm, sub, lanes = xs_ref.shape
    half = bm // 2
    lo, hi = lo_ref[w], hi_ref[w]
    live = w < nw_ref[0]

    def mlp(rows):
        x = _bf(xs_ref[rows].reshape(rows.stop - rows.start, sub * lanes))
        hid = (_silu(jnp.dot(x, wg_s[...], preferred_element_type=F32))
               * jnp.dot(x, wu_s[...], preferred_element_type=F32))
        y = jnp.dot(_bf(hid), wd_s[...], preferred_element_type=F32)
        return y.reshape(rows.stop - rows.start, sub, lanes)

    def put(rows, y, first_visit):
        row = lax.broadcasted_iota(jnp.int32, (rows.stop - rows.start, 1, 1), 0) + rows.start
        mine = (row >= lo) & (row < hi)
        ys_ref[rows] = jnp.where(mine, y, 0.0 if first_visit else ys_ref[rows])

    whole, lower, upper = slice(0, bm), slice(0, half), slice(half, bm)

    @pl.when(live & (lo < half) & (hi > half) & (lo == 0))
    def _():
        put(whole, mlp(whole), True)

    @pl.when(live & (lo < half) & (hi > half) & (lo > 0))
    def _():
        put(whole, mlp(whole), False)

    @pl.when(live & (hi <= half) & (lo == 0))
    def _():
        put(lower, mlp(lower), True)
        ys_ref[upper] = jnp.zeros((bm - half, sub, lanes), F32)

    @pl.when(live & (hi <= half) & (lo > 0))
    def _():
        put(lower, mlp(lower), False)

    @pl.when(live & (lo >= half))
    def _():
        put(upper, mlp(upper), False)


def _experts(xs, item_block, item_expert, item_lo, item_hi, n_items, w_gate, w_up, w_down):
    n_slots, sub, lanes = xs.shape
    d = sub * lanes
    ff = w_gate.shape[2]
    bm = EXPERT_BLOCK
    n = item_block.shape[0]
    idx = jnp.arange(n, dtype=jnp.int32)
    first = jnp.concatenate([jnp.ones((1,), jnp.int32),
                             (item_expert[1:] != item_expert[:-1]).astype(jnp.int32)])
    ring = (jnp.cumsum(first) - 1) % 2
    next_first = lax.cummin(jnp.where(first == 1, idx, n), reverse=True)
    next_first = jnp.concatenate([next_first[1:], jnp.full((1,), n, jnp.int32)])
    nxt = jnp.where(next_first < n, item_expert[jnp.minimum(next_first, n - 1)], -1).astype(jnp.int32)
    slot = lambda w, *_: (_[0][w], 0, 0)
    return pl.pallas_call(
        _experts_kernel,
        out_shape=jax.ShapeDtypeStruct((n_slots, sub, lanes), F32),
        grid_spec=pltpu.PrefetchScalarGridSpec(
            num_scalar_prefetch=8,
            grid=(n,),
            in_specs=[pl.BlockSpec((bm, sub, lanes), slot),
                      pl.BlockSpec(memory_space=pl.ANY), pl.BlockSpec(memory_space=pl.ANY),
                      pl.BlockSpec(memory_space=pl.ANY)],
            out_specs=pl.BlockSpec((bm, sub, lanes), slot),
            scratch_shapes=[pltpu.VMEM((2, d, ff), F32), pltpu.VMEM((2, d, ff), F32),
                            pltpu.VMEM((2, ff, d), F32),
                            pltpu.VMEM((d, ff), BF16), pltpu.VMEM((d, ff), BF16),
                            pltpu.VMEM((ff, d), BF16), pltpu.SemaphoreType.DMA((2,))]),
        compiler_params=_params(("arbitrary",)),
        name="experts",
    )(item_block, item_expert, item_lo, item_hi, n_items, first, ring.astype(jnp.int32), nxt,
      xs, w_gate, w_up, w_down)


def _combine_kernel(d1_ref, d2_ref, ys_ref, rg_ref, x1_ref, mod_ref, g_ref, b_ref, o_ref, ya, yb, sem):
    tm = x1_ref.shape[0]
    i = pl.program_id(0)
    n = pl.num_programs(0)

    def row_copy(dest, buf, slot, tk):
        return pltpu.make_async_copy(ys_ref.at[dest], buf.at[slot, tk], sem.at[slot])

    slot = i % 2
    gate2 = mod_ref[0, 5:6, :]

    def issue_group(step, dst_slot, g):
        for u in range(DMA_UNROLL):
            tk = g * DMA_UNROLL + u
            tok = step * tm + tk
            row_copy(d1_ref[tok], ya, dst_slot, tk).start(priority=0)
            row_copy(d2_ref[tok], yb, dst_slot, tk).start(priority=1)

    def gather_tile(step, dst_slot):
        def issue(g, carry):
            issue_group(step, dst_slot, g)
            return carry

        lax.fori_loop(0, tm // DMA_UNROLL, issue, 0)

    @pl.when(i == 0)
    def _():
        gather_tile(0, 0)

    @pl.when(i + 1 < n)
    def _():
        gather_tile(i + 1, 1 - slot)

    pltpu.make_async_copy(ys_ref.at[pl.ds(0, tm)], ya.at[slot], sem.at[slot]).wait()
    pltpu.make_async_copy(ys_ref.at[pl.ds(0, tm)], yb.at[slot], sem.at[slot]).wait()

    rg = rg_ref[...]
    d = x1_ref.shape[1]
    y = rg[:, 0:1] * ya[slot].reshape(tm, d) + rg[:, 1:2] * yb[slot].reshape(tm, d)
    o_ref[...] = _layer_norm(DEEPNORM_ALPHA * x1_ref[...] + (1.0 + gate2) * y, g_ref[...], b_ref[...])


def _combine(ys, d1, d2, rg, x1, mod, ln_g, ln_b, seq):
    t, d = x1.shape
    tm = ROW_TILE
    spb = seq // tm
    row = lambda i, *_: (i, 0)
    const = lambda i, *_: (0, 0)
    buf = pltpu.VMEM((2, tm) + ys.shape[1:], F32)
    return pl.pallas_call(
        _combine_kernel,
        out_shape=jax.ShapeDtypeStruct((t, d), F32),
        grid_spec=pltpu.PrefetchScalarGridSpec(
            num_scalar_prefetch=2,
            grid=(t // tm,),
            in_specs=[pl.BlockSpec(memory_space=pl.ANY),
                      pl.BlockSpec((tm, LANES), row), pl.BlockSpec((tm, d), row),
                      pl.BlockSpec((1, 6, d), lambda i, *_: (i // spb, 0, 0)),
                      pl.BlockSpec((1, d), const), pl.BlockSpec((1, d), const)],
            out_specs=pl.BlockSpec((tm, d), row),
            scratch_shapes=[buf, buf, pltpu.SemaphoreType.DMA((2,))]),
        compiler_params=_params(("arbitrary",)),
        name="combine",
    )(d1, d2, ys, rg, x1, mod, ln_g.reshape(1, d), ln_b.reshape(1, d))


def _layer(x, c, positions, w_ada, b_ada, w_in, conv_w, a_log, dt_bias, gdn_norm_w, attn_norm_w,
           w_o, ln1_g, ln1_b, w_rg, b_rg, w_re, b_re, w_gate, w_up, w_down, ln2_g, ln2_b):
    bsz, seq, d = x.shape
    t = bsz * seq
    xf = x.reshape(t, d)
    mod = _ada(c, w_ada, b_ada)
    qkv, z, bdc, bdr, qb, kb, vb = _inproj(xf, mod, positions.reshape(t, 1), w_in, conv_w, seq)
    gw = GDN_HEADS * HEAD_DIM
    oa = _gdn(qkv.reshape(bsz, seq, 3 * gw), z.reshape(bsz, seq, gw), bdc.reshape(bsz, seq, LANES), bdr,
              a_log, dt_bias, gdn_norm_w).reshape(t, gw)
    ob = _attn(qb, kb, vb, attn_norm_w, bsz, seq)
    x1, h2, ri, rg, cnt = _outproj(oa, ob, xf, mod, w_o, ln1_g, ln1_b, w_rg, b_rg, w_re, b_re, seq)

    bm = EXPERT_BLOCK
    counts = cnt[N_GROUPS:N_GROUPS + N_EXPERTS, 0].astype(jnp.int32)
    seg_end = jnp.cumsum(counts)
    seg_start = seg_end - counts
    first_blk = seg_start // bm
    n_per = jnp.where(counts > 0, (seg_end - 1) // bm - first_blk + 1, 0)
    item_end = jnp.cumsum(n_per)
    n_items = item_end[-1:]
    max_items = (2 * t) // bm + N_EXPERTS - 1
    w_idx = jnp.minimum(jnp.arange(max_items, dtype=jnp.int32), n_items[0] - 1)
    item_expert = jnp.minimum(jnp.sum(item_end[None, :] <= w_idx[:, None], axis=1), N_EXPERTS - 1).astype(jnp.int32)
    item_block = first_blk[item_expert] + w_idx - (item_end - n_per)[item_expert]
    item_lo = jnp.maximum(seg_start[item_expert] - item_block * bm, 0)
    item_hi = jnp.minimum(seg_end[item_expert] - item_block * bm, bm)
    expert_ids = jnp.arange(N_EXPERTS, dtype=jnp.int32)[:, None]

    def seg_of(e):
        return jnp.sum(jnp.where(e[None, :] == expert_ids, seg_start[:, None], 0), axis=0)

    d1 = seg_of(ri[0]) + ri[2]
    d2 = seg_of(ri[1]) + ri[3]

    xs = _dispatch(h2, d1, d2)
    ys = _experts(xs, item_block, item_expert, item_lo, item_hi, n_items, w_gate, w_up, w_down)
    out = _combine(ys, d1, d2, rg, x1, mod, ln2_g, ln2_b, seq)
    return out.reshape(bsz, seq, d)


def kernel(x, c, positions, w_ada, b_ada, w_in, conv_w, a_log, dt_bias, gdn_norm_w, attn_norm_w, w_o, ln1_g, ln1_b, w_router_group, b_router_group, w_router_expert, b_router_expert, w_gate, w_up, w_down, ln2_g, ln2_b):
    assert w_ada.shape[0] == DEPTH
    return _layer(x, c, positions, w_ada[0], b_ada[0], w_in[0], conv_w[0], a_log[0], dt_bias[0],
                  gdn_norm_w[0], attn_norm_w[0], w_o[0], ln1_g[0], ln1_b[0],
                  w_router_group[0], b_router_group[0], w_router_expert[0], b_router_expert[0],
                  w_gate[0], w_up[0], w_down[0], ln2_g[0], ln2_b[0])
```

```python
import functools
import math

import jax
import jax.numpy as jnp
from jax import lax
from jax.experimental import pallas as pl
from jax.experimental.pallas import tpu as pltpu

F32 = jnp.float32
BF16 = jnp.bfloat16
LOG2E = math.log2(math.e)

GDN_HEADS = 4
ATT_HEADS = 4
HEAD_DIM = 128
CONV_WIDTH = 4
DILATED_PATTERNS = ((128, 1), (512, 4), (2048, 16))
ROPE_THETA = 500000.0
ROPE_DIMS = HEAD_DIM // 4
N_GROUPS = 4
EXPERTS_PER_GROUP = 8
N_EXPERTS = N_GROUPS * EXPERTS_PER_GROUP
DEPTH = 1
DEEPNORM_ALPHA = (2.0 * DEPTH) ** 0.25
LN_EPS = 1e-5
RMS_EPS = 1e-6

LANES = 128
SUBLANES = 8
ROUTER_ROWS = -(-(N_GROUPS + N_EXPERTS) // SUBLANES) * SUBLANES
VMEM_LIMIT = 48 * 1024 * 1024

GDN_BLOCK = 128
GDN_TILE = 256
ATT_BLOCK = 128
ATT_TILE = 2048
ATT_SUB = DILATED_PATTERNS[1][1]
ATT_WIDE = DILATED_PATTERNS[2][1]
assert DILATED_PATTERNS[0][1] == 1 and ATT_WIDE == ATT_SUB * ATT_SUB
assert all(w // d == ATT_BLOCK for w, d in DILATED_PATTERNS) and ATT_TILE == ATT_BLOCK * ATT_WIDE
ADA_COLS = 1024
PROJ_TILE = 512
ROW_TILE = 256
DISPATCH_TILE = 4096
EXPERT_BLOCK = 256
OUTPROJ_PARTS = 2
DMA_UNROLL = 8
NEG = -1e30


def _bf(x):
    return x.astype(BF16)


def _mm(a, b):
    return jnp.dot(_bf(a), _bf(b), preferred_element_type=F32)


def _mm_nt(a, b):
    return lax.dot_general(_bf(a), _bf(b), (((1,), (1,)), ((), ())), preferred_element_type=F32)


def _mm_tn(a, b):
    return lax.dot_general(_bf(a), _bf(b), (((0,), (0,)), ((), ())), preferred_element_type=F32)


def _split3(x):
    x1 = _bf(x)
    r1 = x - x1.astype(F32)
    x2 = _bf(r1)
    return x1, x2, _bf(r1 - x2.astype(F32))


def _sigmoid(x):
    return 1.0 / (1.0 + jnp.exp(-x))


def _silu(x):
    return x * _sigmoid(x)


def _softplus(x):
    return jnp.maximum(x, 0.0) + jnp.log(1.0 + jnp.exp(-jnp.abs(x)))


def _params(sem):
    return pltpu.CompilerParams(dimension_semantics=sem, vmem_limit_bytes=VMEM_LIMIT)


def _ada_kernel(ct_ref, w_ref, b_ref, o_ref, *, bsz):
    sc = _silu(ct_ref[...])
    w = w_ref[...]
    rows = [jnp.sum(w * sc[:, b:b + 1], axis=0, keepdims=True) for b in range(bsz)]
    rows.append(jnp.zeros((o_ref.shape[0] - bsz, w.shape[1]), F32))
    o_ref[...] = jnp.concatenate(rows, axis=0) + b_ref[...]


def _ada(c, w_ada, b_ada):
    bsz, d = c.shape
    n = w_ada.shape[1]
    tn = ADA_COLS
    assert bsz <= SUBLANES
    ct = jnp.zeros((d, LANES), F32).at[:, :bsz].set(c.T)
    out = pl.pallas_call(
        functools.partial(_ada_kernel, bsz=bsz),
        out_shape=jax.ShapeDtypeStruct((SUBLANES, n), F32),
        grid=(n // tn,),
        in_specs=[pl.BlockSpec((d, LANES), lambda j: (0, 0)),
                  pl.BlockSpec((d, tn), lambda j: (0, j)),
                  pl.BlockSpec((1, tn), lambda j: (0, j))],
        out_specs=pl.BlockSpec((SUBLANES, tn), lambda j: (0, j)),
        compiler_params=_params(("parallel",)),
        name="ada",
    )(ct, w_ada, b_ada.reshape(1, n))
    return out[:bsz].reshape(bsz, 6, d)


def _inproj_kernel(x_ref, mod_ref, pos_ref, invf_ref, convw_ref, w_ref,
                   qkv_ref, z_ref, bdc_ref, bdr_ref, qb_ref, kb_ref, vb_ref, cbuf, *, steps_per_seq):
    tm = x_ref.shape[0]
    halo = SUBLANES
    n_slabs = cbuf.shape[0]
    gw, aw = z_ref.shape[1], qb_ref.shape[1]
    o_z = qkv_ref.shape[1]
    o_bd = o_z + gw
    o_q = o_bd + LANES
    w_z, w_bd = w_ref.at[:, o_z:o_bd], w_ref.at[:, o_bd:o_q]
    w_q, w_k, w_v = (w_ref.at[:, o_q + i * aw:o_q + (i + 1) * aw] for i in range(3))

    @pl.when(pl.program_id(0) % steps_per_seq == 0)
    def _():
        cbuf[:, 0:halo, :] = jnp.zeros((n_slabs, halo, LANES), F32)

    half = ROPE_DIMS // 2
    groups = LANES // half
    lane = lax.broadcasted_iota(jnp.int32, (1, LANES), 1)
    first = lane < half
    rot = lane < ROPE_DIMS
    ang = pos_ref[0].astype(F32) * invf_ref[...]
    cos_c = jnp.cos(ang)
    sin_c = jnp.sin(ang)
    cos_parts, sin_parts = [], []
    for j in range(groups):
        lo_sh = (LANES - half * j) % LANES
        hi_sh = (LANES - half * j + half) % LANES
        c_lo = pltpu.roll(cos_c, lo_sh, 1) if lo_sh else cos_c
        c_hi = pltpu.roll(cos_c, hi_sh, 1) if hi_sh else cos_c
        s_lo = pltpu.roll(sin_c, lo_sh, 1) if lo_sh else sin_c
        s_hi = pltpu.roll(sin_c, hi_sh, 1) if hi_sh else sin_c
        cos_parts.append(jnp.where(first, c_lo, jnp.where(rot, c_hi, 1.0)))
        sin_parts.append(jnp.where(first, -s_lo, jnp.where(rot, s_hi, 0.0)))
    cosv = jnp.concatenate(cos_parts, axis=0)
    sin_signed = jnp.concatenate(sin_parts, axis=0)

    def rope(y):
        outs = []
        for hh in range(ATT_HEADS):
            yh = y[:, hh * HEAD_DIM:(hh + 1) * HEAD_DIM]
            partner = jnp.where(first, pltpu.roll(yh, LANES - half, 1), pltpu.roll(yh, half, 1))
            outs.append(yh * cosv + partner * sin_signed)
        return jnp.concatenate(outs, axis=1)

    def conv_slice(s):
        sl = slice(s * HEAD_DIM, (s + 1) * HEAD_DIM)
        off = halo - (CONV_WIDTH - 1)
        acc = convw_ref[0:1, sl] * cbuf[s, off:off + tm, :]
        for j in range(1, CONV_WIDTH):
            acc = acc + convw_ref[j:j + 1, sl] * cbuf[s, off + j:off + j + tm, :]
        cbuf[s, 0:halo, :] = cbuf[s, tm:tm + halo, :]
        y = _silu(acc)
        if s < 2 * GDN_HEADS:
            y = y * lax.rsqrt(jnp.sum(y * y, axis=-1, keepdims=True) + RMS_EPS)
        if s < GDN_HEADS:
            y = y * (HEAD_DIM ** -0.5)
        qkv_ref[:, sl] = y

    shift = mod_ref[0, 0:1, :]
    scale = mod_ref[0, 1:2, :]
    h = _bf(x_ref[...] * (1.0 + scale) + shift)
    chunk = 2 * HEAD_DIM
    n_chunks = n_slabs // 2

    def project_chunk(c):
        pre = jnp.dot(h, w_ref[:, c * chunk:(c + 1) * chunk], preferred_element_type=F32)
        cbuf[2 * c, halo:halo + tm, :] = pre[:, :HEAD_DIM]
        cbuf[2 * c + 1, halo:halo + tm, :] = pre[:, HEAD_DIM:]

    def conv_chunk(c):
        conv_slice(2 * c)
        conv_slice(2 * c + 1)

    project_chunk(0)
    for c in range(1, n_chunks):
        project_chunk(c)
        conv_chunk(c - 1)
    qb = jnp.dot(h, w_q[...], preferred_element_type=F32)
    conv_chunk(n_chunks - 1)
    kb = jnp.dot(h, w_k[...], preferred_element_type=F32)
    qb_ref[...] = rope(qb) * (HEAD_DIM ** -0.5 * LOG2E)
    vb_ref[...] = jnp.dot(h, w_v[...], preferred_element_type=F32)
    kb_ref[...] = rope(kb)
    z_ref[...] = jnp.dot(h, w_z[...], preferred_element_type=F32)
    bdc = jnp.dot(h, w_bd[...], preferred_element_type=F32)
    bdc_ref[...] = bdc
    bdr_ref[0] = bdc.T[:SUBLANES]


def _inproj(xf, mod, pos, w_in, conv_w, seq):
    t, d = xf.shape
    tm = PROJ_TILE
    gw = GDN_HEADS * HEAD_DIM
    aw = ATT_HEADS * HEAD_DIM
    o0 = 3 * gw
    o1 = o0 + gw
    o2 = o1 + 2 * GDN_HEADS
    w_all = _bf(jnp.concatenate([w_in[:, :o1], w_in[:, o1:o2], jnp.zeros((d, LANES - (o2 - o1)), F32),
                                 w_in[:, o2:]], axis=1))
    half = ROPE_DIMS // 2
    groups = LANES // half
    inv_freq = ROPE_THETA ** (-jnp.arange(half, dtype=F32) * 2.0 / ROPE_DIMS)
    invf = jnp.tile(inv_freq, groups).reshape(1, LANES)
    pos = jnp.repeat(pos.reshape(t // tm, groups, tm // groups).transpose(0, 2, 1), half, axis=2)
    spb = seq // tm
    row = lambda i: (i, 0)
    const = lambda i: (0, 0)
    return pl.pallas_call(
        functools.partial(_inproj_kernel, steps_per_seq=spb),
        out_shape=(jax.ShapeDtypeStruct((t, o0), F32), jax.ShapeDtypeStruct((t, gw), F32),
                   jax.ShapeDtypeStruct((t, LANES), F32), jax.ShapeDtypeStruct((t // seq, SUBLANES, seq), F32),
                   jax.ShapeDtypeStruct((t, aw), F32), jax.ShapeDtypeStruct((t, aw), F32),
                   jax.ShapeDtypeStruct((t, aw), F32)),
        grid=(t // tm,),
        in_specs=[pl.BlockSpec((tm, d), row),
                  pl.BlockSpec((1, 6, d), lambda i: (i // spb, 0, 0)),
                  pl.BlockSpec((1, tm // groups, LANES), lambda i: (i, 0, 0)),
                  pl.BlockSpec((1, LANES), const), pl.BlockSpec((CONV_WIDTH, o0), const),
                  pl.BlockSpec((d, w_all.shape[1]), const)],
        out_specs=(pl.BlockSpec((tm, o0), row), pl.BlockSpec((tm, gw), row),
                   pl.BlockSpec((tm, LANES), row),
                   pl.BlockSpec((1, SUBLANES, tm), lambda i: (i // spb, 0, i % spb)),
                   pl.BlockSpec((tm, aw), row), pl.BlockSpec((tm, aw), row),
                   pl.BlockSpec((tm, aw), row)),
        scratch_shapes=[pltpu.VMEM((o0 // HEAD_DIM, tm + 2 * SUBLANES, HEAD_DIM), F32)],
        compiler_params=_params(("arbitrary",)),
        name="inproj",
    )(xf, mod, pos, invf, conv_w, w_all)


def _gdn_kernel(qkv_ref, z_ref, bdc_ref, bdr_ref, gpc_ref, gpr_ref, nw_ref, o_ref,
                u_s, w_s, qd_s, kd_s, qk_s, a_s, x_s, y_s, st_ref):
    nbat, ts = qkv_ref.shape[0], qkv_ref.shape[1]
    nb = ts // GDN_BLOCK
    gw = GDN_HEADS * HEAD_DIM

    @pl.when(pl.program_id(0) == 0)
    def _():
        st_ref[...] = jnp.zeros(st_ref.shape, F32)

    blk_n = GDN_BLOCK
    hd = HEAD_DIM
    ti = lax.broadcasted_iota(jnp.int32, (ts, ts), 0)
    tj = lax.broadcasted_iota(jnp.int32, (ts, ts), 1)
    same = (ti // blk_n) == (tj // blk_n)
    m_low = jnp.where(same & (tj <= ti), 1.0, 0.0).astype(BF16)
    m_up = jnp.where(same & (ti <= tj), 1.0, 0.0).astype(BF16)

    split3 = _split3

    beta_c = [_sigmoid(bdc_ref[bb]) for bb in range(nbat)]
    g_c = jnp.concatenate([-jnp.exp(gpc_ref[0:1, :]) * LOG2E * _softplus(bdc_ref[bb] + gpc_ref[1:2, :])
                           for bb in range(nbat)], axis=1)
    g_r = jnp.concatenate([-jnp.exp(gpr_ref[:, 0:1]) * LOG2E * _softplus(bdr_ref[bb] + gpr_ref[:, 1:2])
                           for bb in range(nbat)], axis=0)
    gc_c = sum(jnp.dot(m_low, part, preferred_element_type=F32) for part in split3(g_c))
    gc_r = sum(jnp.dot(part, m_up, preferred_element_type=F32) for part in split3(g_r))

    def cat2(m):
        return jnp.concatenate([m, m], axis=1)

    ii = lax.broadcasted_iota(jnp.int32, (blk_n, blk_n), 0)
    jj = lax.broadcasted_iota(jnp.int32, (blk_n, blk_n), 1)
    lower = cat2(jj <= ii)
    strict = cat2(jj < ii)
    eye = cat2(jnp.where(ii == jj, 1.0, 0.0).astype(F32))
    levels = []
    b = 1
    while b < blk_n:
        levels.append(cat2(((ii // b) == (jj // b) + 1) & (((jj // b) % 2) == 0)))
        b *= 2

    def block_diag(rp):
        n, m = rp.shape[0], rp.shape[1] // 2
        z = jnp.zeros((n, m), rp.dtype)
        return jnp.concatenate([jnp.concatenate([rp[:, :m], z], axis=1),
                                jnp.concatenate([z, rp[:, m:]], axis=1)], axis=0)

    def mm2(lp, rp):
        return jnp.dot(_bf(lp), block_diag(_bf(rp)), preferred_element_type=F32)

    def mm2_nt(lp, rp):
        return lax.dot_general(_bf(lp), block_diag(_bf(rp)), (((1,), (1,)), ((), ())),
                               preferred_element_type=F32)

    chains = [(bb, j, pp) for bb in range(nbat) for j in range(nb) for pp in range(GDN_HEADS // 2)]

    def tile_of(bb, j, pp):
        return bb, slice(j * blk_n, (j + 1) * blk_n), slice(2 * pp * hd, 2 * (pp + 1) * hd)

    def col_pair(arr, rows, col):
        return jnp.concatenate([jnp.broadcast_to(arr[rows, col:col + 1], (blk_n, hd)),
                                jnp.broadcast_to(arr[rows, col + 1:col + 2], (blk_n, hd))], axis=1)

    for c, (bb, j, pp) in enumerate(chains):
        blk = tile_of(bb, j, pp)
        rows, cols = blk[1], blk[2]
        last = slice((j + 1) * blk_n - 1, (j + 1) * blk_n)
        gcol = bb * LANES + GDN_HEADS + 2 * pp
        grow = bb * SUBLANES + GDN_HEADS + 2 * pp
        q = qkv_ref[bb, rows, cols]
        k = qkv_ref[bb, rows, slice(gw + cols.start, gw + cols.stop)]
        v = qkv_ref[bb, rows, slice(2 * gw + cols.start, 2 * gw + cols.stop)]
        beta = col_pair(beta_c[bb], rows, 2 * pp)
        gcc = col_pair(gc_c, rows, gcol)
        gtc = col_pair(gc_c, last, gcol)
        gcr = jnp.concatenate([jnp.broadcast_to(gc_r[grow:grow + 1, rows], (blk_n, hd)),
                               jnp.broadcast_to(gc_r[grow + 1:grow + 2, rows], (blk_n, hd))], axis=1)
        kb = k * beta
        eg = jnp.exp2(gcc)
        dm = jnp.where(lower, jnp.exp2(gcc - gcr), 0.0)
        a = jnp.where(strict, mm2_nt(kb, k) * dm, 0.0)
        a_s[c] = a
        x_s[c] = eye - jnp.where(levels[0], a, 0.0)
        u_s[blk] = v * beta
        w_s[blk] = kb * eg
        qd_s[blk] = q * eg
        kd_s[blk] = k * jnp.exp2(gtc - gcc)
        qk_s[blk] = mm2_nt(q, k) * dm

    for lm in levels[1:]:
        for c in range(len(chains)):
            y_s[c] = mm2(x_s[c], jnp.where(lm, a_s[c], 0.0))
        for c in range(len(chains)):
            xc = x_s[c]
            x_s[c] = xc - mm2(y_s[c], xc)

    for c, (bb, j, pp) in enumerate(chains):
        blk = tile_of(bb, j, pp)
        u, w = _bf(u_s[blk]), _bf(w_s[blk])
        z = jnp.zeros((blk_n, 2 * hd), BF16)
        rhs = jnp.concatenate([jnp.concatenate([u[:, :hd], w[:, :hd], z], axis=1),
                               jnp.concatenate([z, u[:, hd:], w[:, hd:]], axis=1)], axis=0)
        sol = jnp.dot(_bf(x_s[c]), rhs, preferred_element_type=F32)
        u_s[blk] = jnp.concatenate([sol[:, 0:hd], sol[:, 2 * hd:3 * hd]], axis=1)
        w_s[blk] = jnp.concatenate([sol[:, hd:2 * hd], sol[:, 3 * hd:]], axis=1)

    for j in range(nb):
        for bb in range(nbat):
            for pp in range(GDN_HEADS // 2):
                blk = tile_of(bb, j, pp)
                si = bb * (GDN_HEADS // 2) + pp
                last = slice((j + 1) * blk_n - 1, (j + 1) * blk_n)
                gcol = bb * LANES + GDN_HEADS + 2 * pp
                state = st_ref[si]
                proj = mm2(jnp.concatenate([w_s[blk], qd_s[blk]], axis=0), state)
                v_new = u_s[blk] - proj[:blk_n]
                o = proj[blk_n:] + mm2(qk_s[blk], v_new)
                g_last = jnp.exp2(jnp.concatenate(
                    [jnp.broadcast_to(gc_c[last, gcol:gcol + 1], (1, hd)),
                     jnp.broadcast_to(gc_c[last, gcol + 1:gcol + 2], (1, hd))], axis=1))
                kd = kd_s[blk]
                upd = jnp.concatenate([_mm_tn(kd[:, :hd], v_new[:, :hd]),
                                       _mm_tn(kd[:, hd:], v_new[:, hd:])], axis=1)
                st_ref[si] = state * g_last + upd
                zz = z_ref[blk]
                halves = []
                for hf in range(2):
                    oh = o[:, hf * hd:(hf + 1) * hd]
                    halves.append(oh * lax.rsqrt(jnp.mean(oh * oh, axis=-1, keepdims=True) + RMS_EPS)
                                  * nw_ref[...])
                o_ref[blk] = (jnp.concatenate(halves, axis=1) * _silu(zz)).astype(o_ref.dtype)


def _gdn(qkv, z, bdc, bdr, a_log, dt_bias, norm_w):
    bsz, seq, _ = qkv.shape
    ts = GDN_TILE
    gw = GDN_HEADS * HEAD_DIM
    zeros4 = jnp.zeros((GDN_HEADS,), F32)
    al = jnp.concatenate([zeros4, a_log])
    db = jnp.concatenate([zeros4, dt_bias])
    gpc = jnp.zeros((2, LANES), F32).at[0, :2 * GDN_HEADS].set(al).at[1, :2 * GDN_HEADS].set(db)
    gpr = jnp.stack([al, db], axis=1)
    row = lambda i: (0, i, 0)
    const = lambda i: (0, 0)
    n_chains = bsz * (ts // GDN_BLOCK) * (GDN_HEADS // 2)
    tile = pltpu.VMEM((bsz, ts, gw), F32)
    mats = pltpu.VMEM((n_chains, GDN_BLOCK, 2 * GDN_BLOCK), F32)
    return pl.pallas_call(
        _gdn_kernel,
        out_shape=jax.ShapeDtypeStruct((bsz, seq, gw), BF16),
        grid=(seq // ts,),
        in_specs=[pl.BlockSpec((bsz, ts, 3 * gw), row), pl.BlockSpec((bsz, ts, gw), row),
                  pl.BlockSpec((bsz, ts, LANES), row),
                  pl.BlockSpec((bsz, SUBLANES, ts), lambda i: (0, 0, i)),
                  pl.BlockSpec((2, LANES), const), pl.BlockSpec((2 * GDN_HEADS, 2), const),
                  pl.BlockSpec((1, HEAD_DIM), const)],
        out_specs=pl.BlockSpec((bsz, ts, gw), row),
        scratch_shapes=[tile, tile, tile, tile, tile,
                        mats, mats, mats,
                        pltpu.VMEM((bsz * GDN_HEADS // 2, HEAD_DIM, 2 * HEAD_DIM), F32)],
        compiler_params=_params(("arbitrary",)),
        name="gdn",
    )(qkv, z, bdc, bdr, gpc, gpr, norm_w.reshape(1, HEAD_DIM))


def _attn_kernel(q_ref, k_ref, v_ref, nw_ref, o_ref,
                 q4, k4, v4, ktail, vtail, m_s, l_s, acc_s, tmp_s, nat_s):
    tq = q_ref.shape[0]
    blk = ATT_BLOCK
    sub = ATT_SUB
    nq = tq // sub
    t = pl.program_id(2)
    slot = t % 2
    other = 1 - slot
    qi = lax.broadcasted_iota(jnp.int32, (blk, 2 * blk), 0)
    kj = lax.broadcasted_iota(jnp.int32, (blk, 2 * blk), 1)
    band = (kj >= qi) & (kj <= qi + blk)
    first_lo = jnp.where(t > 0, 0, blk)
    band_first = band & (kj >= first_lo)

    @pl.when(t == 0)
    def _():
        k4[...] = jnp.zeros(k4.shape, F32)
        v4[...] = jnp.zeros(v4.shape, F32)
        ktail[...] = jnp.zeros(ktail.shape, F32)
        vtail[...] = jnp.zeros(vtail.shape, F32)

    for r in range(sub):
        rows = slice(r * nq, (r + 1) * nq)
        src = pl.ds(r, nq, stride=sub)
        q4[rows, :] = q_ref[src, :]
        k4[slot, rows, :] = k_ref[src, :]
        v4[slot, rows, :] = v_ref[src, :]

    def block_stats(q, kcat, vcat, from_prev_tile):
        s = _mm_nt(q, kcat)
        s = jnp.where(band_first if from_prev_tile else band, s, NEG)
        m = jnp.max(s, axis=-1, keepdims=True)
        p = jnp.exp2(s - m)
        l = jnp.sum(p, axis=-1, keepdims=True)
        return m, l, _mm(p, vcat)

    def merge(dst, m, l, o):
        m_old = m_s[dst, :]
        m_new = jnp.maximum(m_old, m)
        w_old = jnp.exp2(m_old - m_new)
        w_cur = jnp.exp2(m - m_new)
        m_s[dst, :] = m_new
        l_s[dst, :] = w_old * l_s[dst, :] + w_cur * l
        acc_s[dst, :] = w_old * acc_s[dst, :] + w_cur * o

    for jb in range(tq // blk):
        cur = slice(jb * blk, (jb + 1) * blk)
        if jb > 0:
            kcat = k_ref[(jb - 1) * blk:(jb + 1) * blk, :]
            vcat = v_ref[(jb - 1) * blk:(jb + 1) * blk, :]
        else:
            kcat = jnp.concatenate([ktail[...], k_ref[cur, :]], axis=0)
            vcat = jnp.concatenate([vtail[...], v_ref[cur, :]], axis=0)
        m, l, o = block_stats(q_ref[cur, :], kcat, vcat, jb == 0)
        tmp_s[0] = jnp.broadcast_to(m, (blk, HEAD_DIM))
        tmp_s[1] = jnp.broadcast_to(l, (blk, HEAD_DIM))
        tmp_s[2] = o
        per = blk // sub
        for r in range(sub):
            dst = slice(r * nq + jb * per, r * nq + (jb + 1) * per)
            src = pl.ds(r, per, stride=sub)
            m_s[dst, :] = tmp_s[0, src, :]
            l_s[dst, :] = tmp_s[1, src, :]
            acc_s[dst, :] = tmp_s[2, src, :]

    for r in range(sub):
        for jb in range(nq // blk):
            base = r * nq + jb * blk
            cur = slice(base, base + blk)
            if jb > 0:
                kcat = k4[slot, base - blk:base + blk, :]
                vcat = v4[slot, base - blk:base + blk, :]
            else:
                last = slice((r + 1) * nq - blk, (r + 1) * nq)
                kcat = jnp.concatenate([k4[other, last, :], k4[slot, cur, :]], axis=0)
                vcat = jnp.concatenate([v4[other, last, :], v4[slot, cur, :]], axis=0)
            m, l, o = block_stats(q4[cur, :], kcat, vcat, jb == 0)
            merge(cur, m, l, o)

    for c in range(ATT_WIDE):
        sl = pl.ds((c % sub) * nq + c // sub, blk, stride=sub)
        kcat = jnp.concatenate([k4[other, sl, :], k4[slot, sl, :]], axis=0)
        vcat = jnp.concatenate([v4[other, sl, :], v4[slot, sl, :]], axis=0)
        m, l, o = block_stats(q4[sl, :], kcat, vcat, True)
        merge(sl, m, l, o)

    out = acc_s[...] / l_s[...]
    out = out * lax.rsqrt(jnp.mean(out * out, axis=-1, keepdims=True) + RMS_EPS) * nw_ref[...]
    for r in range(sub):
        nat_s[pl.ds(r, nq, stride=sub), :] = out[r * nq:(r + 1) * nq]
    o_ref[...] = nat_s[...].astype(o_ref.dtype)
    ktail[...] = k_ref[tq - blk:tq, :]
    vtail[...] = v_ref[tq - blk:tq, :]


def _attn(qb, kb, vb, norm_w, bsz, seq):
    t = qb.shape[0]
    tq = ATT_TILE
    spb = seq // tq
    cur = lambda b, h, i: (b * spb + i, h)
    blk = pl.BlockSpec((tq, HEAD_DIM), cur)
    tile = pltpu.VMEM((tq, HEAD_DIM), F32)
    ring = pltpu.VMEM((2, tq, HEAD_DIM), F32)
    tail = pltpu.VMEM((ATT_BLOCK, HEAD_DIM), F32)
    return pl.pallas_call(
        _attn_kernel,
        out_shape=jax.ShapeDtypeStruct((t, ATT_HEADS * HEAD_DIM), BF16),
        grid=(bsz, ATT_HEADS, spb),
        in_specs=[blk, blk, blk, pl.BlockSpec((1, HEAD_DIM), lambda b, h, i: (0, 0))],
        out_specs=blk,
        scratch_shapes=[tile, ring, ring, tail, tail, tile, tile, tile,
                        pltpu.VMEM((3, ATT_BLOCK, HEAD_DIM), F32), tile],
        compiler_params=_params(("parallel", "parallel", "arbitrary")),
        name="attn",
    )(qb, kb, vb, norm_w.reshape(1, HEAD_DIM))


def _layer_norm(y, g, b):
    mu = jnp.mean(y, axis=-1, keepdims=True)
    yc = y - mu
    var = jnp.mean(yc * yc, axis=-1, keepdims=True)
    return yc * lax.rsqrt(var + LN_EPS) * g + b


def _outproj_kernel(oa_ref, ob_ref, x_ref, mod_ref, wo_ref, g_ref, b_ref,
                    wrh_ref, wrl_ref, br_ref,
                    x1_ref, h2_ref, ri_ref, rg_ref, cnt_ref, run_s, mix_s, hi_s, lo_s):
    @pl.when(pl.program_id(0) == 0)
    def _():
        run_s[...] = jnp.zeros(run_s.shape, F32)

    tm = x_ref.shape[0]
    gate1 = mod_ref[0, 2:3, :]
    shift2 = mod_ref[0, 3:4, :]
    scale2 = mod_ref[0, 4:5, :]
    n_parts = OUTPROJ_PARTS
    part = tm // n_parts
    grp_rows = 2 * SUBLANES

    def project(p):
        rows = slice(p * part, (p + 1) * part)
        mix_s[rows, :] = jnp.dot(jnp.concatenate([oa_ref[rows, :], ob_ref[rows, :]], axis=1), wo_ref[...],
                                 preferred_element_type=F32)

    def normalise(p):
        for r0 in range(p * part, (p + 1) * part, grp_rows):
            rows = slice(r0, r0 + grp_rows)
            x1 = _layer_norm(DEEPNORM_ALPHA * x_ref[rows, :] + (1.0 + gate1) * mix_s[rows, :],
                             g_ref[...], b_ref[...])
            x1_ref[rows, :] = x1
            h2 = x1 * (1.0 + scale2) + shift2
            h2_ref[rows] = h2.reshape(grp_rows, SUBLANES, LANES)
            hi = _bf(h2)
            hi_s[rows, :] = hi
            lo_s[rows, :] = _bf(h2 - hi.astype(F32))

    nt = (((1,), (1,)), ((), ()))

    def route(p):
        rows = slice(p * part, (p + 1) * part)
        hi = hi_s[rows, :]
        by_token = (jnp.dot(hi, wrh_ref[...], preferred_element_type=F32)
                    + jnp.dot(hi, wrl_ref[...], preferred_element_type=F32)
                    + jnp.dot(lo_s[rows, :], wrh_ref[...], preferred_element_type=F32))
        return by_token.T[:ROUTER_ROWS]

    project(0)
    parts = []
    for p in range(n_parts):
        if p + 1 < n_parts:
            project(p + 1)
        normalise(p)
        parts.append(route(p))
    logits = jnp.concatenate(parts, axis=1) + br_ref[:, 0:1]
    nr = logits.shape[0]
    row = lax.broadcasted_iota(jnp.int32, (nr, tm), 0).astype(F32)
    lg = jnp.where(row < N_GROUPS, logits, NEG)
    mg = jnp.max(lg, axis=0, keepdims=True)
    grp = jnp.min(jnp.where(lg == mg, row, float(nr)), axis=0, keepdims=True)
    gate_grp = 1.0 / jnp.sum(jnp.exp(lg - mg), axis=0, keepdims=True)
    first_row = N_GROUPS + EXPERTS_PER_GROUP * grp
    sel = (row >= first_row) & (row < first_row + EXPERTS_PER_GROUP)
    le = jnp.where(sel, logits, NEG)
    v1 = jnp.max(le, axis=0, keepdims=True)
    i1 = jnp.min(jnp.where(le == v1, row, float(nr)), axis=0, keepdims=True)
    le2 = jnp.where(row == i1, NEG, le)
    v2 = jnp.max(le2, axis=0, keepdims=True)
    i2 = jnp.min(jnp.where(le2 == v2, row, float(nr)), axis=0, keepdims=True)
    e21 = jnp.exp(v2 - v1)
    g1 = gate_grp / (1.0 + e21)
    g2 = gate_grp * e21 / (1.0 + e21)

    oh1 = row == i1
    oh2 = row == i2
    onehot = jnp.where(oh1 | oh2, 1.0, 0.0).astype(F32)
    ti = lax.broadcasted_iota(jnp.int32, (tm, tm), 0)
    tj = lax.broadcasted_iota(jnp.int32, (tm, tm), 1)
    earlier = jnp.where(ti < tj, 1.0, 0.0).astype(F32)
    tot = _mm(onehot, earlier) + run_s[:, 0:1]
    r1 = jnp.sum(jnp.where(oh1, tot, 0.0), axis=0, keepdims=True)
    r2 = jnp.sum(jnp.where(oh2, tot, 0.0), axis=0, keepdims=True)
    run_s[...] = run_s[...] + jnp.sum(onehot, axis=1, keepdims=True)
    cnt_ref[...] = run_s[...]

    sub_i = lax.broadcasted_iota(jnp.int32, (SUBLANES, tm), 0)
    ri = jnp.where(sub_i == 0, i1 - N_GROUPS, 0.0)
    ri = jnp.where(sub_i == 1, i2 - N_GROUPS, ri)
    ri = jnp.where(sub_i == 2, r1, ri)
    ri = jnp.where(sub_i == 3, r2, ri)
    ri_ref[...] = ri.astype(jnp.int32)

    gates = jnp.where(sub_i == 0, g1, jnp.where(sub_i == 1, g2, 0.0))
    pick = jnp.where(lax.broadcasted_iota(jnp.int32, (SUBLANES, LANES), 0)
                     == lax.broadcasted_iota(jnp.int32, (SUBLANES, LANES), 1), 1.0, 0.0).astype(BF16)
    tn = (((0,), (0,)), ((), ()))
    rg_ref[...] = sum(lax.dot_general(term, pick, tn, preferred_element_type=F32)
                      for term in _split3(gates))


def _outproj(oa, ob, xf, mod, w_o, ln_g, ln_b, w_rg, b_rg, w_re, b_re, seq):
    t, d = xf.shape
    tm = PROJ_TILE
    gw = oa.shape[1]
    wo = _bf(w_o)
    nr = ROUTER_ROWS
    n_log = N_GROUPS + N_EXPERTS
    wr = jnp.concatenate([w_rg, w_re, jnp.zeros((d, LANES - n_log), F32)], axis=1)
    wrh = _bf(wr)
    wrl = _bf(wr - wrh.astype(F32))
    br = jnp.concatenate([b_rg, b_re, jnp.zeros((nr - n_log,), F32)])
    br = jnp.broadcast_to(br[:, None], (nr, LANES))
    spb = seq // tm
    row = lambda i: (i, 0)
    const = lambda i: (0, 0)
    return pl.pallas_call(
        _outproj_kernel,
        out_shape=(jax.ShapeDtypeStruct((t, d), F32), jax.ShapeDtypeStruct((t, d // LANES, LANES), F32),
                   jax.ShapeDtypeStruct((SUBLANES, t), jnp.int32), jax.ShapeDtypeStruct((t, LANES), F32),
                   jax.ShapeDtypeStruct((nr, LANES), F32)),
        grid=(t // tm,),
        in_specs=[pl.BlockSpec((tm, gw), row), pl.BlockSpec((tm, gw), row), pl.BlockSpec((tm, d), row),
                  pl.BlockSpec((1, 6, d), lambda i: (i // spb, 0, 0)),
                  pl.BlockSpec((2 * gw, d), const),
                  pl.BlockSpec((1, d), const), pl.BlockSpec((1, d), const),
                  pl.BlockSpec((d, LANES), const), pl.BlockSpec((d, LANES), const),
                  pl.BlockSpec((nr, LANES), const)],
        out_specs=(pl.BlockSpec((tm, d), row), pl.BlockSpec((tm, d // LANES, LANES), lambda i: (i, 0, 0)),
                   pl.BlockSpec((SUBLANES, tm), lambda i: (0, i)), pl.BlockSpec((tm, LANES), row),
                   pl.BlockSpec((nr, LANES), const)),
        scratch_shapes=[pltpu.VMEM((nr, LANES), F32), pltpu.VMEM((tm, d), F32),
                        pltpu.VMEM((tm, d), BF16), pltpu.VMEM((tm, d), BF16)],
        compiler_params=_params(("arbitrary",)),
        name="outproj",
    )(oa, ob, xf, mod, wo, ln_g.reshape(1, d), ln_b.reshape(1, d), wrh, wrl, br)


def _dispatch_kernel(d1_ref, d2_ref, h_ref, xs_ref, sem):
    tm = h_ref.shape[0]
    i = pl.program_id(0)

    def row_copy(tk, dest):
        return pltpu.make_async_copy(h_ref.at[tk], xs_ref.at[dest], sem)

    def issue(g, carry):
        for u in range(DMA_UNROLL):
            tk = g * DMA_UNROLL + u
            tok = i * tm + tk
            row_copy(tk, d1_ref[tok]).start(priority=0)
            row_copy(tk, d2_ref[tok]).start(priority=1)
        return carry

    lax.fori_loop(0, tm // DMA_UNROLL, issue, 0)

    tile_copy = pltpu.make_async_copy(h_ref, xs_ref.at[pl.ds(0, tm)], sem)
    tile_copy.wait()
    tile_copy.wait()


def _dispatch(h2, d1, d2):
    t, sub, lanes = h2.shape
    tm = DISPATCH_TILE
    return pl.pallas_call(
        _dispatch_kernel,
        out_shape=jax.ShapeDtypeStruct((2 * t, sub, lanes), F32),
        grid_spec=pltpu.PrefetchScalarGridSpec(
            num_scalar_prefetch=2,
            grid=(t // tm,),
            in_specs=[pl.BlockSpec((tm, sub, lanes), lambda i, *_: (i, 0, 0))],
            out_specs=pl.BlockSpec(memory_space=pl.ANY),
            scratch_shapes=[pltpu.SemaphoreType.DMA]),
        compiler_params=_params(("arbitrary",)),
        name="dispatch",
    )(d1, d2, h2)


def _experts_kernel(wb_ref, we_ref, lo_ref, hi_ref, nw_ref, first_ref, ring_ref, next_ref,
                    xs_ref, wg_hbm, wu_hbm, wd_hbm, ys_ref,
                    wg_f, wu_f, wd_f, wg_s, wu_s, wd_s, sem):
    w = pl.program_id(0)

    def fetch(expert, slot):
        return (pltpu.make_async_copy(wg_hbm.at[expert], wg_f.at[slot], sem.at[slot]),
                pltpu.make_async_copy(wu_hbm.at[expert], wu_f.at[slot], sem.at[slot]),
                pltpu.make_async_copy(wd_hbm.at[expert], wd_f.at[slot], sem.at[slot]))

    @pl.when(w == 0)
    def _():
        for cp in fetch(we_ref[0], 0):
            cp.start()

    @pl.when((first_ref[w] == 1) & (w < nw_ref[0]))
    def _():
        slot = ring_ref[w]
        for cp in fetch(we_ref[w], slot):
            cp.wait()

        @pl.when(next_ref[w] >= 0)
        def _():
            for cp in fetch(next_ref[w], 1 - slot):
                cp.start()

        wg_s[...] = _bf(wg_f[slot])
        wu_s[...] = _bf(wu_f[slot])
        wd_s[...] = _bf(wd_f[slot])

    bm, sub, lanes = xs_ref.shape
    half = bm // 2
    lo, hi = lo_ref[w], hi_ref[w]
    live = w < nw_ref[0]

    def mlp(rows):
        x = _bf(xs_ref[rows].reshape(rows.stop - rows.start, sub * lanes))
        hid = (_silu(jnp.dot(x, wg_s[...], preferred_element_type=F32))
               * jnp.dot(x, wu_s[...], preferred_element_type=F32))
        y = jnp.dot(_bf(hid), wd_s[...], preferred_element_type=F32)
        return y.reshape(rows.stop - rows.start, sub, lanes)

    def put(rows, y, first_visit):
        row = lax.broadcasted_iota(jnp.int32, (rows.stop - rows.start, 1, 1), 0) + rows.start
        mine = (row >= lo) & (row < hi)
        ys_ref[rows] = jnp.where(mine, y, 0.0 if first_visit else ys_ref[rows])

    whole, lower, upper = slice(0, bm), slice(0, half), slice(half, bm)

    @pl.when(live & (lo < half) & (hi > half) & (lo == 0))
    def _():
        put(whole, mlp(whole), True)

    @pl.when(live & (lo < half) & (hi > half) & (lo > 0))
    def _():
        put(whole, mlp(whole), False)

    @pl.when(live & (hi <= half) & (lo == 0))
    def _():
        put(lower, mlp(lower), True)
        ys_ref[upper] = jnp.zeros((bm - half, sub, lanes), F32)

    @pl.when(live & (hi <= half) & (lo > 0))
    def _():
        put(lower, mlp(lower), False)

    @pl.when(live & (lo >= half))
    def _():
        put(upper, mlp(upper), False)


def _experts(xs, item_block, item_expert, item_lo, item_hi, n_items, w_gate, w_up, w_down):
    n_slots, sub, lanes = xs.shape
    d = sub * lanes
    ff = w_gate.shape[2]
    bm = EXPERT_BLOCK
    n = item_block.shape[0]
    idx = jnp.arange(n, dtype=jnp.int32)
    first = jnp.concatenate([jnp.ones((1,), jnp.int32),
                             (item_expert[1:] != item_expert[:-1]).astype(jnp.int32)])
    ring = (jnp.cumsum(first) - 1) % 2
    next_first = lax.cummin(jnp.where(first == 1, idx, n), reverse=True)
    next_first = jnp.concatenate([next_first[1:], jnp.full((1,), n, jnp.int32)])
    nxt = jnp.where(next_first < n, item_expert[jnp.minimum(next_first, n - 1)], -1).astype(jnp.int32)
    slot = lambda w, *_: (_[0][w], 0, 0)
    return pl.pallas_call(
        _experts_kernel,
        out_shape=jax.ShapeDtypeStruct((n_slots, sub, lanes), F32),
        grid_spec=pltpu.PrefetchScalarGridSpec(
            num_scalar_prefetch=8,
            grid=(n,),
            in_specs=[pl.BlockSpec((bm, sub, lanes), slot),
                      pl.BlockSpec(memory_space=pl.ANY), pl.BlockSpec(memory_space=pl.ANY),
                      pl.BlockSpec(memory_space=pl.ANY)],
            out_specs=pl.BlockSpec((bm, sub, lanes), slot),
            scratch_shapes=[pltpu.VMEM((2, d, ff), F32), pltpu.VMEM((2, d, ff), F32),
                            pltpu.VMEM((2, ff, d), F32),
                            pltpu.VMEM((d, ff), BF16), pltpu.VMEM((d, ff), BF16),
                            pltpu.VMEM((ff, d), BF16), pltpu.SemaphoreType.DMA((2,))]),
        compiler_params=_params(("arbitrary",)),
        name="experts",
    )(item_block, item_expert, item_lo, item_hi, n_items, first, ring.astype(jnp.int32), nxt,
      xs, w_gate, w_up, w_down)


def _combine_kernel(d1_ref, d2_ref, ys_ref, rg_ref, x1_ref, mod_ref, g_ref, b_ref, o_ref, ya, yb, sem):
    tm = x1_ref.shape[0]
    i = pl.program_id(0)
    n = pl.num_programs(0)

    def row_copy(dest, buf, slot, tk):
        return pltpu.make_async_copy(ys_ref.at[dest], buf.at[slot, tk], sem.at[slot])

    slot = i % 2
    gate2 = mod_ref[0, 5:6, :]

    def issue_group(step, dst_slot, g):
        for u in range(DMA_UNROLL):
            tk = g * DMA_UNROLL + u
            tok = step * tm + tk
            row_copy(d1_ref[tok], ya, dst_slot, tk).start(priority=0)
            row_copy(d2_ref[tok], yb, dst_slot, tk).start(priority=1)

    def gather_tile(step, dst_slot):
        def issue(g, carry):
            issue_group(step, dst_slot, g)
            return carry

        lax.fori_loop(0, tm // DMA_UNROLL, issue, 0)

    @pl.when(i == 0)
    def _():
        gather_tile(0, 0)

    @pl.when(i + 1 < n)
    def _():
        gather_tile(i + 1, 1 - slot)

    pltpu.make_async_copy(ys_ref.at[pl.ds(0, tm)], ya.at[slot], sem.at[slot]).wait()
    pltpu.make_async_copy(ys_ref.at[pl.ds(0, tm)], yb.at[slot], sem.at[slot]).wait()

    rg = rg_ref[...]
    d = x1_ref.shape[1]
    y = rg[:, 0:1] * ya[slot].reshape(tm, d) + rg[:, 1:2] * yb[slot].reshape(tm, d)
    o_ref[...] = _layer_norm(DEEPNORM_ALPHA * x1_ref[...] + (1.0 + gate2) * y, g_ref[...], b_ref[...])


def _combine(ys, d1, d2, rg, x1, mod, ln_g, ln_b, seq):
    t, d = x1.shape
    tm = ROW_TILE
    spb = seq // tm
    row = lambda i, *_: (i, 0)
    const = lambda i, *_: (0, 0)
    buf = pltpu.VMEM((2, tm) + ys.shape[1:], F32)
    return pl.pallas_call(
        _combine_kernel,
        out_shape=jax.ShapeDtypeStruct((t, d), F32),
        grid_spec=pltpu.PrefetchScalarGridSpec(
            num_scalar_prefetch=2,
            grid=(t // tm,),
            in_specs=[pl.BlockSpec(memory_space=pl.ANY),
                      pl.BlockSpec((tm, LANES), row), pl.BlockSpec((tm, d), row),
                      pl.BlockSpec((1, 6, d), lambda i, *_: (i // spb, 0, 0)),
                      pl.BlockSpec((1, d), const), pl.BlockSpec((1, d), const)],
            out_specs=pl.BlockSpec((tm, d), row),
            scratch_shapes=[buf, buf, pltpu.SemaphoreType.DMA((2,))]),
        compiler_params=_params(("arbitrary",)),
        name="combine",
    )(d1, d2, ys, rg, x1, mod, ln_g.reshape(1, d), ln_b.reshape(1, d))


def _layer(x, c, positions, w_ada, b_ada, w_in, conv_w, a_log, dt_bias, gdn_norm_w, attn_norm_w,
           w_o, ln1_g, ln1_b, w_rg, b_rg, w_re, b_re, w_gate, w_up, w_down, ln2_g, ln2_b):
    bsz, seq, d = x.shape
    t = bsz * seq
    xf = x.reshape(t, d)
    mod = _ada(c, w_ada, b_ada)
    qkv, z, bdc, bdr, qb, kb, vb = _inproj(xf, mod, positions.reshape(t, 1), w_in, conv_w, seq)
    gw = GDN_HEADS * HEAD_DIM
    oa = _gdn(qkv.reshape(bsz, seq, 3 * gw), z.reshape(bsz, seq, gw), bdc.reshape(bsz, seq, LANES), bdr,
              a_log, dt_bias, gdn_norm_w).reshape(t, gw)
    ob = _attn(qb, kb, vb, attn_norm_w, bsz, seq)
    x1, h2, ri, rg, cnt = _outproj(oa, ob, xf, mod, w_o, ln1_g, ln1_b, w_rg, b_rg, w_re, b_re, seq)

    bm = EXPERT_BLOCK
    counts = cnt[N_GROUPS:N_GROUPS + N_EXPERTS, 0].astype(jnp.int32)
    seg_end = jnp.cumsum(counts)
    seg_start = seg_end - counts
    first_blk = seg_start // bm
    n_per = jnp.where(counts > 0, (seg_end - 1) // bm - first_blk + 1, 0)
    item_end = jnp.cumsum(n_per)
    n_items = item_end[-1:]
    max_items = (2 * t) // bm + N_EXPERTS - 1
    w_idx = jnp.minimum(jnp.arange(max_items, dtype=jnp.int32), n_items[0] - 1)
    item_expert = jnp.minimum(jnp.sum(item_end[None, :] <= w_idx[:, None], axis=1), N_EXPERTS - 1).astype(jnp.int32)
    item_block = first_blk[item_expert] + w_idx - (item_end - n_per)[item_expert]
    item_lo = jnp.maximum(seg_start[item_expert] - item_block * bm, 0)
    item_hi = jnp.minimum(seg_end[item_expert] - item_block * bm, bm)
    expert_ids = jnp.arange(N_EXPERTS, dtype=jnp.int32)[:, None]

    def seg_of(e):
        return jnp.sum(jnp.where(e[None, :] == expert_ids, seg_start[:, None], 0), axis=0)

    d1 = seg_of(ri[0]) + ri[2]
    d2 = seg_of(ri[1]) + ri[3]

    xs = _dispatch(h2, d1, d2)
    ys = _experts(xs, item_block, item_expert, item_lo, item_hi, n_items, w_gate, w_up, w_down)
    out = _combine(ys, d1, d2, rg, x1, mod, ln2_g, ln2_b, seq)
    return out.reshape(bsz, seq, d)


def kernel(x, c, positions, w_ada, b_ada, w_in, conv_w, a_log, dt_bias, gdn_norm_w, attn_norm_w, w_o, ln1_g, ln1_b, w_router_group, b_router_group, w_router_expert, b_router_expert, w_gate, w_up, w_down, ln2_g, ln2_b):
    assert w_ada.shape[0] == DEPTH
    return _layer(x, c, positions, w_ada[0], b_ada[0], w_in[0], conv_w[0], a_log[0], dt_bias[0],
                  gdn_norm_w[0], attn_norm_w[0], w_o[0], ln1_g[0], ln1_b[0],
                  w_router_group[0], b_router_group[0], w_router_expert[0], b_router_expert[0],
                  w_gate[0], w_up[0], w_down[0], ln2_g[0], ln2_b[0])
```

```python
import functools
import math

import jax
import jax.numpy as jnp
from jax import lax
from jax.experimental import pallas as pl
from jax.experimental.pallas import tpu as pltpu

F32 = jnp.float32
BF16 = jnp.bfloat16
LOG2E = math.log2(math.e)

GDN_HEADS = 4
ATT_HEADS = 4
HEAD_DIM = 128
CONV_WIDTH = 4
DILATED_PATTERNS = ((128, 1), (512, 4), (2048, 16))
ROPE_THETA = 500000.0
ROPE_DIMS = HEAD_DIM // 4
N_GROUPS = 4
EXPERTS_PER_GROUP = 8
N_EXPERTS = N_GROUPS * EXPERTS_PER_GROUP
DEPTH = 1
DEEPNORM_ALPHA = (2.0 * DEPTH) ** 0.25
LN_EPS = 1e-5
RMS_EPS = 1e-6

LANES = 128
SUBLANES = 8
ROUTER_ROWS = -(-(N_GROUPS + N_EXPERTS) // SUBLANES) * SUBLANES
VMEM_LIMIT = 48 * 1024 * 1024

GDN_BLOCK = 128
GDN_TILE = 256
ATT_BLOCK = 128
ATT_TILE = 2048
ATT_SUB = DILATED_PATTERNS[1][1]
ATT_WIDE = DILATED_PATTERNS[2][1]
assert DILATED_PATTERNS[0][1] == 1 and ATT_WIDE == ATT_SUB * ATT_SUB
assert all(w // d == ATT_BLOCK for w, d in DILATED_PATTERNS) and ATT_TILE == ATT_BLOCK * ATT_WIDE
ADA_COLS = 1024
PROJ_TILE = 512
ROW_TILE = 256
DISPATCH_TILE = 4096
EXPERT_BLOCK = 256
OUTPROJ_PARTS = 2
DMA_UNROLL = 8
NEG = -1e30


def _bf(x):
    return x.astype(BF16)


def _mm(a, b):
    return jnp.dot(_bf(a), _bf(b), preferred_element_type=F32)


def _mm_nt(a, b):
    return lax.dot_general(_bf(a), _bf(b), (((1,), (1,)), ((), ())), preferred_element_type=F32)


def _mm_tn(a, b):
    return lax.dot_general(_bf(a), _bf(b), (((0,), (0,)), ((), ())), preferred_element_type=F32)


def _split3(x):
    x1 = _bf(x)
    r1 = x - x1.astype(F32)
    x2 = _bf(r1)
    return x1, x2, _bf(r1 - x2.astype(F32))


def _sigmoid(x):
    return 1.0 / (1.0 + jnp.exp(-x))


def _silu(x):
    return x * _sigmoid(x)


def _softplus(x):
    return jnp.maximum(x, 0.0) + jnp.log(1.0 + jnp.exp(-jnp.abs(x)))


def _params(sem):
    return pltpu.CompilerParams(dimension_semantics=sem, vmem_limit_bytes=VMEM_LIMIT)


def _ada_kernel(ct_ref, w_ref, b_ref, o_ref, *, bsz):
    sc = _silu(ct_ref[...])
    w = w_ref[...]
    rows = [jnp.sum(w * sc[:, b:b + 1], axis=0, keepdims=True) for b in range(bsz)]
    rows.append(jnp.zeros((o_ref.shape[0] - bsz, w.shape[1]), F32))
    o_ref[...] = jnp.concatenate(rows, axis=0) + b_ref[...]


def _ada(c, w_ada, b_ada):
    bsz, d = c.shape
    n = w_ada.shape[1]
    tn = ADA_COLS
    assert bsz <= SUBLANES
    ct = jnp.zeros((d, LANES), F32).at[:, :bsz].set(c.T)
    out = pl.pallas_call(
        functools.partial(_ada_kernel, bsz=bsz),
        out_shape=jax.ShapeDtypeStruct((SUBLANES, n), F32),
        grid=(n // tn,),
        in_specs=[pl.BlockSpec((d, LANES), lambda j: (0, 0)),
                  pl.BlockSpec((d, tn), lambda j: (0, j)),
                  pl.BlockSpec((1, tn), lambda j: (0, j))],
        out_specs=pl.BlockSpec((SUBLANES, tn), lambda j: (0, j)),
        compiler_params=_params(("parallel",)),
        name="ada",
    )(ct, w_ada, b_ada.reshape(1, n))
    return out[:bsz].reshape(bsz, 6, d)


def _inproj_kernel(x_ref, mod_ref, pos_ref, invf_ref, convw_ref, win_ref,
                   qkv_ref, z_ref, bdc_ref, bdr_ref, qb_ref, kb_ref, vb_ref, cbuf, w_ref, *, steps_per_seq):
    tm = x_ref.shape[0]
    halo = SUBLANES
    n_slabs = cbuf.shape[0]
    gw, aw = z_ref.shape[1], qb_ref.shape[1]
    o_z = qkv_ref.shape[1]
    o_bd = o_z + gw
    o_q = o_bd + LANES
    n_gate = win_ref.shape[1] - o_bd - 3 * aw

    @pl.when(pl.program_id(0) == 0)
    def _():
        w_ref[:, :o_bd] = _bf(win_ref[:, :o_bd])
        w_ref[:, o_bd:o_q] = _bf(jnp.concatenate(
            [win_ref[:, o_bd:o_bd + n_gate], jnp.zeros((win_ref.shape[0], LANES - n_gate), F32)], axis=1))
        w_ref[:, o_q:] = _bf(win_ref[:, o_bd + n_gate:])

    w_z, w_bd = w_ref.at[:, o_z:o_bd], w_ref.at[:, o_bd:o_q]
    w_q, w_k, w_v = (w_ref.at[:, o_q + i * aw:o_q + (i + 1) * aw] for i in range(3))

    @pl.when(pl.program_id(0) % steps_per_seq == 0)
    def _():
        cbuf[:, 0:halo, :] = jnp.zeros((n_slabs, halo, LANES), F32)

    half = ROPE_DIMS // 2
    groups = LANES // half
    lane = lax.broadcasted_iota(jnp.int32, (1, LANES), 1)
    first = lane < half
    rot = lane < ROPE_DIMS
    ang = pos_ref[0].astype(F32) * invf_ref[...]
    cos_c = jnp.cos(ang)
    sin_c = jnp.sin(ang)
    cos_parts, sin_parts = [], []
    for j in range(groups):
        lo_sh = (LANES - half * j) % LANES
        hi_sh = (LANES - half * j + half) % LANES
        c_lo = pltpu.roll(cos_c, lo_sh, 1) if lo_sh else cos_c
        c_hi = pltpu.roll(cos_c, hi_sh, 1) if hi_sh else cos_c
        s_lo = pltpu.roll(sin_c, lo_sh, 1) if lo_sh else sin_c
        s_hi = pltpu.roll(sin_c, hi_sh, 1) if hi_sh else sin_c
        cos_parts.append(jnp.where(first, c_lo, jnp.where(rot, c_hi, 1.0)))
        sin_parts.append(jnp.where(first, -s_lo, jnp.where(rot, s_hi, 0.0)))
    cosv = jnp.concatenate(cos_parts, axis=0)
    sin_signed = jnp.concatenate(sin_parts, axis=0)

    def rope(y):
        outs = []
        for hh in range(ATT_HEADS):
            yh = y[:, hh * HEAD_DIM:(hh + 1) * HEAD_DIM]
            partner = jnp.where(first, pltpu.roll(yh, LANES - half, 1), pltpu.roll(yh, half, 1))
            outs.append(yh * cosv + partner * sin_signed)
        return jnp.concatenate(outs, axis=1)

    def conv_slice(s):
        sl = slice(s * HEAD_DIM, (s + 1) * HEAD_DIM)
        off = halo - (CONV_WIDTH - 1)
        acc = convw_ref[0:1, sl] * cbuf[s, off:off + tm, :]
        for j in range(1, CONV_WIDTH):
            acc = acc + convw_ref[j:j + 1, sl] * cbuf[s, off + j:off + j + tm, :]
        cbuf[s, 0:halo, :] = cbuf[s, tm:tm + halo, :]
        y = _silu(acc)
        if s < 2 * GDN_HEADS:
            y = y * lax.rsqrt(jnp.sum(y * y, axis=-1, keepdims=True) + RMS_EPS)
        if s < GDN_HEADS:
            y = y * (HEAD_DIM ** -0.5)
        qkv_ref[:, sl] = y

    shift = mod_ref[0, 0:1, :]
    scale = mod_ref[0, 1:2, :]
    h = _bf(x_ref[...] * (1.0 + scale) + shift)
    chunk = 2 * HEAD_DIM
    n_chunks = n_slabs // 2

    def project_chunk(c):
        pre = jnp.dot(h, w_ref[:, c * chunk:(c + 1) * chunk], preferred_element_type=F32)
        cbuf[2 * c, halo:halo + tm, :] = pre[:, :HEAD_DIM]
        cbuf[2 * c + 1, halo:halo + tm, :] = pre[:, HEAD_DIM:]

    def conv_chunk(c):
        conv_slice(2 * c)
        conv_slice(2 * c + 1)

    project_chunk(0)
    for c in range(1, n_chunks):
        project_chunk(c)
        conv_chunk(c - 1)
    qb = jnp.dot(h, w_q[...], preferred_element_type=F32)
    conv_chunk(n_chunks - 1)
    kb = jnp.dot(h, w_k[...], preferred_element_type=F32)
    qb_ref[...] = rope(qb) * (HEAD_DIM ** -0.5 * LOG2E)
    vb_ref[...] = jnp.dot(h, w_v[...], preferred_element_type=F32)
    kb_ref[...] = rope(kb)
    z_ref[...] = jnp.dot(h, w_z[...], preferred_element_type=F32)
    bdc = jnp.dot(h, w_bd[...], preferred_element_type=F32)
    bdc_ref[...] = bdc
    bdr_ref[0] = bdc.T[:SUBLANES]


def _inproj(xf, mod, pos, w_in, conv_w, seq):
    t, d = xf.shape
    tm = PROJ_TILE
    gw = GDN_HEADS * HEAD_DIM
    aw = ATT_HEADS * HEAD_DIM
    o0 = 3 * gw
    o1 = o0 + gw
    o2 = o1 + 2 * GDN_HEADS
    packed_cols = o1 + LANES + 3 * aw
    half = ROPE_DIMS // 2
    groups = LANES // half
    inv_freq = ROPE_THETA ** (-jnp.arange(half, dtype=F32) * 2.0 / ROPE_DIMS)
    invf = jnp.tile(inv_freq, groups).reshape(1, LANES)
    pos = jnp.repeat(pos.reshape(t // tm, groups, tm // groups).transpose(0, 2, 1), half, axis=2)
    spb = seq // tm
    row = lambda i: (i, 0)
    const = lambda i: (0, 0)
    return pl.pallas_call(
        functools.partial(_inproj_kernel, steps_per_seq=spb),
        out_shape=(jax.ShapeDtypeStruct((t, o0), F32), jax.ShapeDtypeStruct((t, gw), F32),
                   jax.ShapeDtypeStruct((t, LANES), F32), jax.ShapeDtypeStruct((t // seq, SUBLANES, seq), F32),
                   jax.ShapeDtypeStruct((t, aw), F32), jax.ShapeDtypeStruct((t, aw), F32),
                   jax.ShapeDtypeStruct((t, aw), F32)),
        grid=(t // tm,),
        in_specs=[pl.BlockSpec((tm, d), row),
                  pl.BlockSpec((1, 6, d), lambda i: (i // spb, 0, 0)),
                  pl.BlockSpec((1, tm // groups, LANES), lambda i: (i, 0, 0)),
                  pl.BlockSpec((1, LANES), const), pl.BlockSpec((CONV_WIDTH, o0), const),
                  pl.BlockSpec(w_in.shape, const, pipeline_mode=pl.Buffered(1))],
        out_specs=(pl.BlockSpec((tm, o0), row), pl.BlockSpec((tm, gw), row),
                   pl.BlockSpec((tm, LANES), row),
                   pl.BlockSpec((1, SUBLANES, tm), lambda i: (i // spb, 0, i % spb)),
                   pl.BlockSpec((tm, aw), row), pl.BlockSpec((tm, aw), row),
                   pl.BlockSpec((tm, aw), row)),
        scratch_shapes=[pltpu.VMEM((o0 // HEAD_DIM, tm + 2 * SUBLANES, HEAD_DIM), F32),
                        pltpu.VMEM((d, packed_cols), BF16)],
        compiler_params=_params(("arbitrary",)),
        name="inproj",
    )(xf, mod, pos, invf, conv_w, w_in)


def _gdn_kernel(qkv_ref, z_ref, bdc_ref, bdr_ref, gpc_ref, gpr_ref, nw_ref, o_ref,
                u_s, w_s, qd_s, kd_s, qk_s, a_s, x_s, y_s, st_ref):
    nbat, ts = qkv_ref.shape[0], qkv_ref.shape[1]
    nb = ts // GDN_BLOCK
    gw = GDN_HEADS * HEAD_DIM

    @pl.when(pl.program_id(0) == 0)
    def _():
        st_ref[...] = jnp.zeros(st_ref.shape, F32)

    blk_n = GDN_BLOCK
    hd = HEAD_DIM
    ti = lax.broadcasted_iota(jnp.int32, (ts, ts), 0)
    tj = lax.broadcasted_iota(jnp.int32, (ts, ts), 1)
    same = (ti // blk_n) == (tj // blk_n)
    m_low = jnp.where(same & (tj <= ti), 1.0, 0.0).astype(BF16)
    m_up = jnp.where(same & (ti <= tj), 1.0, 0.0).astype(BF16)

    split3 = _split3

    beta_c = [_sigmoid(bdc_ref[bb]) for bb in range(nbat)]
    g_c = jnp.concatenate([-jnp.exp(gpc_ref[0:1, :]) * LOG2E * _softplus(bdc_ref[bb] + gpc_ref[1:2, :])
                           for bb in range(nbat)], axis=1)
    g_r = jnp.concatenate([-jnp.exp(gpr_ref[:, 0:1]) * LOG2E * _softplus(bdr_ref[bb] + gpr_ref[:, 1:2])
                           for bb in range(nbat)], axis=0)
    gc_c = sum(jnp.dot(m_low, part, preferred_element_type=F32) for part in split3(g_c))
    gc_r = sum(jnp.dot(part, m_up, preferred_element_type=F32) for part in split3(g_r))

    def cat2(m):
        return jnp.concatenate([m, m], axis=1)

    ii = lax.broadcasted_iota(jnp.int32, (blk_n, blk_n), 0)
    jj = lax.broadcasted_iota(jnp.int32, (blk_n, blk_n), 1)
    lower = cat2(jj <= ii)
    strict = cat2(jj < ii)
    eye = cat2(jnp.where(ii == jj, 1.0, 0.0).astype(F32))
    levels = []
    b = 1
    while b < blk_n:
        levels.append(cat2(((ii // b) == (jj // b) + 1) & (((jj // b) % 2) == 0)))
        b *= 2

    def block_diag(rp):
        n, m = rp.shape[0], rp.shape[1] // 2
        z = jnp.zeros((n, m), rp.dtype)
        return jnp.concatenate([jnp.concatenate([rp[:, :m], z], axis=1),
                                jnp.concatenate([z, rp[:, m:]], axis=1)], axis=0)

    def mm2(lp, rp):
        return jnp.dot(_bf(lp), block_diag(_bf(rp)), preferred_element_type=F32)

    def mm2_nt(lp, rp):
        return lax.dot_general(_bf(lp), block_diag(_bf(rp)), (((1,), (1,)), ((), ())),
                               preferred_element_type=F32)

    chains = [(bb, j, pp) for bb in range(nbat) for j in range(nb) for pp in range(GDN_HEADS // 2)]

    def tile_of(bb, j, pp):
        return bb, slice(j * blk_n, (j + 1) * blk_n), slice(2 * pp * hd, 2 * (pp + 1) * hd)

    def col_pair(arr, rows, col):
        return jnp.concatenate([jnp.broadcast_to(arr[rows, col:col + 1], (blk_n, hd)),
                                jnp.broadcast_to(arr[rows, col + 1:col + 2], (blk_n, hd))], axis=1)

    for c, (bb, j, pp) in enumerate(chains):
        blk = tile_of(bb, j, pp)
        rows, cols = blk[1], blk[2]
        last = slice((j + 1) * blk_n - 1, (j + 1) * blk_n)
        gcol = bb * LANES + GDN_HEADS + 2 * pp
        grow = bb * SUBLANES + GDN_HEADS + 2 * pp
        q = qkv_ref[bb, rows, cols]
        k = qkv_ref[bb, rows, slice(gw + cols.start, gw + cols.stop)]
        v = qkv_ref[bb, rows, slice(2 * gw + cols.start, 2 * gw + cols.stop)]
        beta = col_pair(beta_c[bb], rows, 2 * pp)
        gcc = col_pair(gc_c, rows, gcol)
        gtc = col_pair(gc_c, last, gcol)
        gcr = jnp.concatenate([jnp.broadcast_to(gc_r[grow:grow + 1, rows], (blk_n, hd)),
                               jnp.broadcast_to(gc_r[grow + 1:grow + 2, rows], (blk_n, hd))], axis=1)
        kb = k * beta
        eg = jnp.exp2(gcc)
        dm = jnp.where(lower, jnp.exp2(gcc - gcr), 0.0)
        a = jnp.where(strict, mm2_nt(kb, k) * dm, 0.0)
        a_s[c] = a
        x_s[c] = eye - jnp.where(levels[0], a, 0.0)
        u_s[blk] = v * beta
        w_s[blk] = kb * eg
        qd_s[blk] = q * eg
        kd_s[blk] = k * jnp.exp2(gtc - gcc)
        qk_s[blk] = mm2_nt(q, k) * dm

    for lm in levels[1:]:
        for c in range(len(chains)):
            y_s[c] = mm2(x_s[c], jnp.where(lm, a_s[c], 0.0))
        for c in range(len(chains)):
            xc = x_s[c]
            x_s[c] = xc - mm2(y_s[c], xc)

    for c, (bb, j, pp) in enumerate(chains):
        blk = tile_of(bb, j, pp)
        u, w = _bf(u_s[blk]), _bf(w_s[blk])
        z = jnp.zeros((blk_n, 2 * hd), BF16)
        rhs = jnp.concatenate([jnp.concatenate([u[:, :hd], w[:, :hd], z], axis=1),
                               jnp.concatenate([z, u[:, hd:], w[:, hd:]], axis=1)], axis=0)
        sol = jnp.dot(_bf(x_s[c]), rhs, preferred_element_type=F32)
        u_s[blk] = jnp.concatenate([sol[:, 0:hd], sol[:, 2 * hd:3 * hd]], axis=1)
        w_s[blk] = jnp.concatenate([sol[:, hd:2 * hd], sol[:, 3 * hd:]], axis=1)

    for j in range(nb):
        for bb in range(nbat):
            for pp in range(GDN_HEADS // 2):
                blk = tile_of(bb, j, pp)
                si = bb * (GDN_HEADS // 2) + pp
                last = slice((j + 1) * blk_n - 1, (j + 1) * blk_n)
                gcol = bb * LANES + GDN_HEADS + 2 * pp
                state = st_ref[si]
                proj = mm2(jnp.concatenate([w_s[blk], qd_s[blk]], axis=0), state)
                v_new = u_s[blk] - proj[:blk_n]
                o = proj[blk_n:] + mm2(qk_s[blk], v_new)
                g_last = jnp.exp2(jnp.concatenate(
                    [jnp.broadcast_to(gc_c[last, gcol:gcol + 1], (1, hd)),
                     jnp.broadcast_to(gc_c[last, gcol + 1:gcol + 2], (1, hd))], axis=1))
                kd = kd_s[blk]
                upd = jnp.concatenate([_mm_tn(kd[:, :hd], v_new[:, :hd]),
                                       _mm_tn(kd[:, hd:], v_new[:, hd:])], axis=1)
                st_ref[si] = state * g_last + upd
                zz = z_ref[blk]
                halves = []
                for hf in range(2):
                    oh = o[:, hf * hd:(hf + 1) * hd]
                    halves.append(oh * lax.rsqrt(jnp.mean(oh * oh, axis=-1, keepdims=True) + RMS_EPS)
                                  * nw_ref[...])
                o_ref[blk] = (jnp.concatenate(halves, axis=1) * _silu(zz)).astype(o_ref.dtype)


def _gdn(qkv, z, bdc, bdr, a_log, dt_bias, norm_w):
    bsz, seq, _ = qkv.shape
    ts = GDN_TILE
    gw = GDN_HEADS * HEAD_DIM
    zeros4 = jnp.zeros((GDN_HEADS,), F32)
    al = jnp.concatenate([zeros4, a_log])
    db = jnp.concatenate([zeros4, dt_bias])
    gpc = jnp.zeros((2, LANES), F32).at[0, :2 * GDN_HEADS].set(al).at[1, :2 * GDN_HEADS].set(db)
    gpr = jnp.stack([al, db], axis=1)
    row = lambda i: (0, i, 0)
    const = lambda i: (0, 0)
    n_chains = bsz * (ts // GDN_BLOCK) * (GDN_HEADS // 2)
    tile = pltpu.VMEM((bsz, ts, gw), F32)
    mats = pltpu.VMEM((n_chains, GDN_BLOCK, 2 * GDN_BLOCK), F32)
    return pl.pallas_call(
        _gdn_kernel,
        out_shape=jax.ShapeDtypeStruct((bsz, seq, gw), BF16),
        grid=(seq // ts,),
        in_specs=[pl.BlockSpec((bsz, ts, 3 * gw), row), pl.BlockSpec((bsz, ts, gw), row),
                  pl.BlockSpec((bsz, ts, LANES), row),
                  pl.BlockSpec((bsz, SUBLANES, ts), lambda i: (0, 0, i)),
                  pl.BlockSpec((2, LANES), const), pl.BlockSpec((2 * GDN_HEADS, 2), const),
                  pl.BlockSpec((1, HEAD_DIM), const)],
        out_specs=pl.BlockSpec((bsz, ts, gw), row),
        scratch_shapes=[tile, tile, tile, tile, tile,
                        mats, mats, mats,
                        pltpu.VMEM((bsz * GDN_HEADS // 2, HEAD_DIM, 2 * HEAD_DIM), F32)],
        compiler_params=_params(("arbitrary",)),
        name="gdn",
    )(qkv, z, bdc, bdr, gpc, gpr, norm_w.reshape(1, HEAD_DIM))


def _attn_kernel(q_ref, k_ref, v_ref, nw_ref, o_ref,
                 q4, k4, v4, ktail, vtail, m_s, l_s, acc_s, tmp_s, nat_s):
    tq = q_ref.shape[0]
    blk = ATT_BLOCK
    sub = ATT_SUB
    nq = tq // sub
    t = pl.program_id(2)
    slot = t % 2
    other = 1 - slot
    qi = lax.broadcasted_iota(jnp.int32, (blk, 2 * blk), 0)
    kj = lax.broadcasted_iota(jnp.int32, (blk, 2 * blk), 1)
    band = (kj >= qi) & (kj <= qi + blk)
    first_lo = jnp.where(t > 0, 0, blk)
    band_first = band & (kj >= first_lo)

    @pl.when(t == 0)
    def _():
        k4[...] = jnp.zeros(k4.shape, F32)
        v4[...] = jnp.zeros(v4.shape, F32)
        ktail[...] = jnp.zeros(ktail.shape, F32)
        vtail[...] = jnp.zeros(vtail.shape, F32)

    for r in range(sub):
        rows = slice(r * nq, (r + 1) * nq)
        src = pl.ds(r, nq, stride=sub)
        q4[rows, :] = q_ref[src, :]
        k4[slot, rows, :] = k_ref[src, :]
        v4[slot, rows, :] = v_ref[src, :]

    def block_stats(q, kcat, vcat, from_prev_tile):
        s = _mm_nt(q, kcat)
        s = jnp.where(band_first if from_prev_tile else band, s, NEG)
        m = jnp.max(s, axis=-1, keepdims=True)
        p = jnp.exp2(s - m)
        l = jnp.sum(p, axis=-1, keepdims=True)
        return m, l, _mm(p, vcat)

    def merge(dst, m, l, o):
        m_old = m_s[dst, :]
        m_new = jnp.maximum(m_old, m)
        w_old = jnp.exp2(m_old - m_new)
        w_cur = jnp.exp2(m - m_new)
        m_s[dst, :] = m_new
        l_s[dst, :] = w_old * l_s[dst, :] + w_cur * l
        acc_s[dst, :] = w_old * acc_s[dst, :] + w_cur * o

    for jb in range(tq // blk):
        cur = slice(jb * blk, (jb + 1) * blk)
        if jb > 0:
            kcat = k_ref[(jb - 1) * blk:(jb + 1) * blk, :]
            vcat = v_ref[(jb - 1) * blk:(jb + 1) * blk, :]
        else:
            kcat = jnp.concatenate([ktail[...], k_ref[cur, :]], axis=0)
            vcat = jnp.concatenate([vtail[...], v_ref[cur, :]], axis=0)
        m, l, o = block_stats(q_ref[cur, :], kcat, vcat, jb == 0)
        tmp_s[0] = jnp.broadcast_to(m, (blk, HEAD_DIM))
        tmp_s[1] = jnp.broadcast_to(l, (blk, HEAD_DIM))
        tmp_s[2] = o
        per = blk // sub
        for r in range(sub):
            dst = slice(r * nq + jb * per, r * nq + (jb + 1) * per)
            src = pl.ds(r, per, stride=sub)
            m_s[dst, :] = tmp_s[0, src, :]
            l_s[dst, :] = tmp_s[1, src, :]
            acc_s[dst, :] = tmp_s[2, src, :]

    for r in range(sub):
        for jb in range(nq // blk):
            base = r * nq + jb * blk
            cur = slice(base, base + blk)
            if jb > 0:
                kcat = k4[slot, base - blk:base + blk, :]
                vcat = v4[slot, base - blk:base + blk, :]
            else:
                last = slice((r + 1) * nq - blk, (r + 1) * nq)
                kcat = jnp.concatenate([k4[other, last, :], k4[slot, cur, :]], axis=0)
                vcat = jnp.concatenate([v4[other, last, :], v4[slot, cur, :]], axis=0)
            m, l, o = block_stats(q4[cur, :], kcat, vcat, jb == 0)
            merge(cur, m, l, o)

    for c in range(ATT_WIDE):
        sl = pl.ds((c % sub) * nq + c // sub, blk, stride=sub)
        kcat = jnp.concatenate([k4[other, sl, :], k4[slot, sl, :]], axis=0)
        vcat = jnp.concatenate([v4[other, sl, :], v4[slot, sl, :]], axis=0)
        m, l, o = block_stats(q4[sl, :], kcat, vcat, True)
        merge(sl, m, l, o)

    out = acc_s[...] / l_s[...]
    out = out * lax.rsqrt(jnp.mean(out * out, axis=-1, keepdims=True) + RMS_EPS) * nw_ref[...]
    for r in range(sub):
        nat_s[pl.ds(r, nq, stride=sub), :] = out[r * nq:(r + 1) * nq]
    o_ref[...] = nat_s[...].astype(o_ref.dtype)
    ktail[...] = k_ref[tq - blk:tq, :]
    vtail[...] = v_ref[tq - blk:tq, :]


def _attn(qb, kb, vb, norm_w, bsz, seq):
    t = qb.shape[0]
    tq = ATT_TILE
    spb = seq // tq
    cur = lambda b, h, i: (b * spb + i, h)
    blk = pl.BlockSpec((tq, HEAD_DIM), cur)
    tile = pltpu.VMEM((tq, HEAD_DIM), F32)
    ring = pltpu.VMEM((2, tq, HEAD_DIM), F32)
    tail = pltpu.VMEM((ATT_BLOCK, HEAD_DIM), F32)
    return pl.pallas_call(
        _attn_kernel,
        out_shape=jax.ShapeDtypeStruct((t, ATT_HEADS * HEAD_DIM), BF16),
        grid=(bsz, ATT_HEADS, spb),
        in_specs=[blk, blk, blk, pl.BlockSpec((1, HEAD_DIM), lambda b, h, i: (0, 0))],
        out_specs=blk,
        scratch_shapes=[tile, ring, ring, tail, tail, tile, tile, tile,
                        pltpu.VMEM((3, ATT_BLOCK, HEAD_DIM), F32), tile],
        compiler_params=_params(("parallel", "parallel", "arbitrary")),
        name="attn",
    )(qb, kb, vb, norm_w.reshape(1, HEAD_DIM))


def _layer_norm(y, g, b):
    mu = jnp.mean(y, axis=-1, keepdims=True)
    yc = y - mu
    var = jnp.mean(yc * yc, axis=-1, keepdims=True)
    return yc * lax.rsqrt(var + LN_EPS) * g + b


def _outproj_kernel(oa_ref, ob_ref, x_ref, mod_ref, wo_ref, g_ref, b_ref,
                    wrh_ref, wrl_ref, br_ref,
                    x1_ref, h2_ref, ri_ref, rg_ref, cnt_ref, run_s, mix_s, hi_s, lo_s):
    @pl.when(pl.program_id(0) == 0)
    def _():
        run_s[...] = jnp.zeros(run_s.shape, F32)

    tm = x_ref.shape[0]
    gate1 = mod_ref[0, 2:3, :]
    shift2 = mod_ref[0, 3:4, :]
    scale2 = mod_ref[0, 4:5, :]
    n_parts = OUTPROJ_PARTS
    part = tm // n_parts
    grp_rows = 2 * SUBLANES

    def project(p):
        rows = slice(p * part, (p + 1) * part)
        mix_s[rows, :] = jnp.dot(jnp.concatenate([oa_ref[rows, :], ob_ref[rows, :]], axis=1), wo_ref[...],
                                 preferred_element_type=F32)

    def normalise(p):
        for r0 in range(p * part, (p + 1) * part, grp_rows):
            rows = slice(r0, r0 + grp_rows)
            x1 = _layer_norm(DEEPNORM_ALPHA * x_ref[rows, :] + (1.0 + gate1) * mix_s[rows, :],
                             g_ref[...], b_ref[...])
            x1_ref[rows, :] = x1
            h2 = x1 * (1.0 + scale2) + shift2
            h2_ref[rows] = h2.reshape(grp_rows, SUBLANES, LANES)
            hi = _bf(h2)
            hi_s[rows, :] = hi
            lo_s[rows, :] = _bf(h2 - hi.astype(F32))

    nt = (((1,), (1,)), ((), ()))

    def route(p):
        rows = slice(p * part, (p + 1) * part)
        hi = hi_s[rows, :]
        by_token = (jnp.dot(hi, wrh_ref[...], preferred_element_type=F32)
                    + jnp.dot(hi, wrl_ref[...], preferred_element_type=F32)
                    + jnp.dot(lo_s[rows, :], wrh_ref[...], preferred_element_type=F32))
        return by_token.T[:ROUTER_ROWS]

    project(0)
    parts = []
    for p in range(n_parts):
        if p + 1 < n_parts:
            project(p + 1)
        normalise(p)
        parts.append(route(p))
    logits = jnp.concatenate(parts, axis=1) + br_ref[:, 0:1]
    nr = logits.shape[0]
    row = lax.broadcasted_iota(jnp.int32, (nr, tm), 0).astype(F32)
    lg = jnp.where(row < N_GROUPS, logits, NEG)
    mg = jnp.max(lg, axis=0, keepdims=True)
    grp = jnp.min(jnp.where(lg == mg, row, float(nr)), axis=0, keepdims=True)
    gate_grp = 1.0 / jnp.sum(jnp.exp(lg - mg), axis=0, keepdims=True)
    first_row = N_GROUPS + EXPERTS_PER_GROUP * grp
    sel = (row >= first_row) & (row < first_row + EXPERTS_PER_GROUP)
    le = jnp.where(sel, logits, NEG)
    v1 = jnp.max(le, axis=0, keepdims=True)
    i1 = jnp.min(jnp.where(le == v1, row, float(nr)), axis=0, keepdims=True)
    le2 = jnp.where(row == i1, NEG, le)
    v2 = jnp.max(le2, axis=0, keepdims=True)
    i2 = jnp.min(jnp.where(le2 == v2, row, float(nr)), axis=0, keepdims=True)
    e21 = jnp.exp(v2 - v1)
    g1 = gate_grp / (1.0 + e21)
    g2 = gate_grp * e21 / (1.0 + e21)

    oh1 = row == i1
    oh2 = row == i2
    onehot = jnp.where(oh1 | oh2, 1.0, 0.0).astype(F32)
    ti = lax.broadcasted_iota(jnp.int32, (tm, tm), 0)
    tj = lax.broadcasted_iota(jnp.int32, (tm, tm), 1)
    earlier = jnp.where(ti < tj, 1.0, 0.0).astype(F32)
    tot = _mm(onehot, earlier) + run_s[:, 0:1]
    r1 = jnp.sum(jnp.where(oh1, tot, 0.0), axis=0, keepdims=True)
    r2 = jnp.sum(jnp.where(oh2, tot, 0.0), axis=0, keepdims=True)
    run_s[...] = run_s[...] + jnp.sum(onehot, axis=1, keepdims=True)
    cnt_ref[...] = run_s[...]

    sub_i = lax.broadcasted_iota(jnp.int32, (SUBLANES, tm), 0)
    ri = jnp.where(sub_i == 0, i1 - N_GROUPS, 0.0)
    ri = jnp.where(sub_i == 1, i2 - N_GROUPS, ri)
    ri = jnp.where(sub_i == 2, r1, ri)
    ri = jnp.where(sub_i == 3, r2, ri)
    ri_ref[...] = ri.astype(jnp.int32)

    gates = jnp.where(sub_i == 0, g1, jnp.where(sub_i == 1, g2, 0.0))
    pick = jnp.where(lax.broadcasted_iota(jnp.int32, (SUBLANES, LANES), 0)
                     == lax.broadcasted_iota(jnp.int32, (SUBLANES, LANES), 1), 1.0, 0.0).astype(BF16)
    tn = (((0,), (0,)), ((), ()))
    rg_ref[...] = sum(lax.dot_general(term, pick, tn, preferred_element_type=F32)
                      for term in _split3(gates))


def _outproj(oa, ob, xf, mod, w_o, ln_g, ln_b, w_rg, b_rg, w_re, b_re, seq):
    t, d = xf.shape
    tm = PROJ_TILE
    gw = oa.shape[1]
    wo = _bf(w_o)
    nr = ROUTER_ROWS
    n_log = N_GROUPS + N_EXPERTS
    wr = jnp.concatenate([w_rg, w_re, jnp.zeros((d, LANES - n_log), F32)], axis=1)
    wrh = _bf(wr)
    wrl = _bf(wr - wrh.astype(F32))
    br = jnp.concatenate([b_rg, b_re, jnp.zeros((nr - n_log,), F32)])
    br = jnp.broadcast_to(br[:, None], (nr, LANES))
    spb = seq // tm
    row = lambda i: (i, 0)
    const = lambda i: (0, 0)
    return pl.pallas_call(
        _outproj_kernel,
        out_shape=(jax.ShapeDtypeStruct((t, d), F32), jax.ShapeDtypeStruct((t, d // LANES, LANES), F32),
                   jax.ShapeDtypeStruct((SUBLANES, t), jnp.int32), jax.ShapeDtypeStruct((t, LANES), F32),
                   jax.ShapeDtypeStruct((nr, LANES), F32)),
        grid=(t // tm,),
        in_specs=[pl.BlockSpec((tm, gw), row), pl.BlockSpec((tm, gw), row), pl.BlockSpec((tm, d), row),
                  pl.BlockSpec((1, 6, d), lambda i: (i // spb, 0, 0)),
                  pl.BlockSpec((2 * gw, d), const),
                  pl.BlockSpec((1, d), const), pl.BlockSpec((1, d), const),
                  pl.BlockSpec((d, LANES), const), pl.BlockSpec((d, LANES), const),
                  pl.BlockSpec((nr, LANES), const)],
        out_specs=(pl.BlockSpec((tm, d), row), pl.BlockSpec((tm, d // LANES, LANES), lambda i: (i, 0, 0)),
                   pl.BlockSpec((SUBLANES, tm), lambda i: (0, i)), pl.BlockSpec((tm, LANES), row),
                   pl.BlockSpec((nr, LANES), const)),
        scratch_shapes=[pltpu.VMEM((nr, LANES), F32), pltpu.VMEM((tm, d), F32),
                        pltpu.VMEM((tm, d), BF16), pltpu.VMEM((tm, d), BF16)],
        compiler_params=_params(("arbitrary",)),
        name="outproj",
    )(oa, ob, xf, mod, wo, ln_g.reshape(1, d), ln_b.reshape(1, d), wrh, wrl, br)


def _dispatch_kernel(d1_ref, d2_ref, h_ref, xs_ref, sem):
    tm = h_ref.shape[0]
    i = pl.program_id(0)

    def row_copy(tk, dest):
        return pltpu.make_async_copy(h_ref.at[tk], xs_ref.at[dest], sem)

    def issue(g, carry):
        for u in range(DMA_UNROLL):
            tk = g * DMA_UNROLL + u
            tok = i * tm + tk
            row_copy(tk, d1_ref[tok]).start(priority=0)
            row_copy(tk, d2_ref[tok]).start(priority=1)
        return carry

    lax.fori_loop(0, tm // DMA_UNROLL, issue, 0)

    tile_copy = pltpu.make_async_copy(h_ref, xs_ref.at[pl.ds(0, tm)], sem)
    tile_copy.wait()
    tile_copy.wait()


def _dispatch(h2, d1, d2):
    t, sub, lanes = h2.shape
    tm = DISPATCH_TILE
    return pl.pallas_call(
        _dispatch_kernel,
        out_shape=jax.ShapeDtypeStruct((2 * t, sub, lanes), F32),
        grid_spec=pltpu.PrefetchScalarGridSpec(
            num_scalar_prefetch=2,
            grid=(t // tm,),
            in_specs=[pl.BlockSpec((tm, sub, lanes), lambda i, *_: (i, 0, 0))],
            out_specs=pl.BlockSpec(memory_space=pl.ANY),
            scratch_shapes=[pltpu.SemaphoreType.DMA]),
        compiler_params=_params(("arbitrary",)),
        name="dispatch",
    )(d1, d2, h2)


def _experts_kernel(wb_ref, we_ref, lo_ref, hi_ref, nw_ref, first_ref, ring_ref, next_ref,
                    xs_ref, wg_hbm, wu_hbm, wd_hbm, ys_ref,
                    wg_f, wu_f, wd_f, wg_s, wu_s, wd_s, sem):
    w = pl.program_id(0)

    def fetch(expert, slot):
        return (pltpu.make_async_copy(wg_hbm.at[expert], wg_f.at[slot], sem.at[slot]),
                pltpu.make_async_copy(wu_hbm.at[expert], wu_f.at[slot], sem.at[slot]),
                pltpu.make_async_copy(wd_hbm.at[expert], wd_f.at[slot], sem.at[slot]))

    @pl.when(w == 0)
    def _():
        for cp in fetch(we_ref[0], 0):
            cp.start()

    @pl.when((first_ref[w] == 1) & (w < nw_ref[0]))
    def _():
        slot = ring_ref[w]
        for cp in fetch(we_ref[w], slot):
            cp.wait()

        @pl.when(next_ref[w] >= 0)
        def _():
            for cp in fetch(next_ref[w], 1 - slot):
                cp.start()

        wg_s[...] = _bf(wg_f[slot])
        wu_s[...] = _bf(wu_f[slot])
        wd_s[...] = _bf(wd_f[slot])

    bm, sub, lanes = xs_ref.shape
    half = bm // 2
    lo, hi = lo_ref[w], hi_ref[w]
    live = w < nw_ref[0]

    def mlp(rows):
        x = _bf(xs_ref[rows].reshape(rows.stop - rows.start, sub * lanes))
        hid = (_silu(jnp.dot(x, wg_s[...], preferred_element_type=F32))
               * jnp.dot(x, wu_s[...], preferred_element_type=F32))
        y = jnp.dot(_bf(hid), wd_s[...], preferred_element_type=F32)
        return y.reshape(rows.stop - rows.start, sub, lanes)

    def put(rows, y, first_visit):
        row = lax.broadcasted_iota(jnp.int32, (rows.stop - rows.start, 1, 1), 0) + rows.start
        mine = (row >= lo) & (row < hi)
        ys_ref[rows] = jnp.where(mine, y, 0.0 if first_visit else ys_ref[rows])

    whole, lower, upper = slice(0, bm), slice(0, half), slice(half, bm)

    @pl.when(live & (lo < half) & (hi > half) & (lo == 0))
    def _():
        put(whole, mlp(whole), True)

    @pl.when(live & (lo < half) & (hi > half) & (lo > 0))
    def _():
        put(whole, mlp(whole), False)

    @pl.when(live & (hi <= half) & (lo == 0))
    def _():
        put(lower, mlp(lower), True)
        ys_ref[upper] = jnp.zeros((bm - half, sub, lanes), F32)

    @pl.when(live & (hi <= half) & (lo > 0))
    def _():
        put(lower, mlp(lower), False)

    @pl.when(live & (lo >= half))
    def _():
        put(upper, mlp(upper), False)


def _experts(xs, item_block, item_expert, item_lo, item_hi, n_items, w_gate, w_up, w_down):
    n_slots, sub, lanes = xs.shape
    d = sub * lanes
    ff = w_gate.shape[2]
    bm = EXPERT_BLOCK
    n = item_block.shape[0]
    idx = jnp.arange(n, dtype=jnp.int32)
    first = jnp.concatenate([jnp.ones((1,), jnp.int32),
                             (item_expert[1:] != item_expert[:-1]).astype(jnp.int32)])
    ring = (jnp.cumsum(first) - 1) % 2
    next_first = lax.cummin(jnp.where(first == 1, idx, n), reverse=True)
    next_first = jnp.concatenate([next_first[1:], jnp.full((1,), n, jnp.int32)])
    nxt = jnp.where(next_first < n, item_expert[jnp.minimum(next_first, n - 1)], -1).astype(jnp.int32)
    slot = lambda w, *_: (_[0][w], 0, 0)
    return pl.pallas_call(
        _experts_kernel,
        out_shape=jax.ShapeDtypeStruct((n_slots, sub, lanes), F32),
        grid_spec=pltpu.PrefetchScalarGridSpec(
            num_scalar_prefetch=8,
            grid=(n,),
            in_specs=[pl.BlockSpec((bm, sub, lanes), slot),
                      pl.BlockSpec(memory_space=pl.ANY), pl.BlockSpec(memory_space=pl.ANY),
                      pl.BlockSpec(memory_space=pl.ANY)],
            out_specs=pl.BlockSpec((bm, sub, lanes), slot),
            scratch_shapes=[pltpu.VMEM((2, d, ff), F32), pltpu.VMEM((2, d, ff), F32),
                            pltpu.VMEM((2, ff, d), F32),
                            pltpu.VMEM((d, ff), BF16), pltpu.VMEM((d, ff), BF16),
                            pltpu.VMEM((ff, d), BF16), pltpu.SemaphoreType.DMA((2,))]),
        compiler_params=_params(("arbitrary",)),
        name="experts",
    )(item_block, item_expert, item_lo, item_hi, n_items, first, ring.astype(jnp.int32), nxt,
      xs, w_gate, w_up, w_down)


def _combine_kernel(d1_ref, d2_ref, ys_ref, rg_ref, x1_ref, mod_ref, g_ref, b_ref, o_ref, ya, yb, sem):
    tm = x1_ref.shape[0]
    i = pl.program_id(0)
    n = pl.num_programs(0)

    def row_copy(dest, buf, slot, tk):
        return pltpu.make_async_copy(ys_ref.at[dest], buf.at[slot, tk], sem.at[slot])

    slot = i % 2
    gate2 = mod_ref[0, 5:6, :]

    def issue_group(step, dst_slot, g):
        for u in range(DMA_UNROLL):
            tk = g * DMA_UNROLL + u
            tok = step * tm + tk
            row_copy(d1_ref[tok], ya, dst_slot, tk).start(priority=0)
            row_copy(d2_ref[tok], yb, dst_slot, tk).start(priority=1)

    def gather_tile(step, dst_slot):
        def issue(g, carry):
            issue_group(step, dst_slot, g)
            return carry

        lax.fori_loop(0, tm // DMA_UNROLL, issue, 0)

    @pl.when(i == 0)
    def _():
        gather_tile(0, 0)

    @pl.when(i + 1 < n)
    def _():
        gather_tile(i + 1, 1 - slot)

    pltpu.make_async_copy(ys_ref.at[pl.ds(0, tm)], ya.at[slot], sem.at[slot]).wait()
    pltpu.make_async_copy(ys_ref.at[pl.ds(0, tm)], yb.at[slot], sem.at[slot]).wait()

    rg = rg_ref[...]
    d = x1_ref.shape[1]
    y = rg[:, 0:1] * ya[slot].reshape(tm, d) + rg[:, 1:2] * yb[slot].reshape(tm, d)
    o_ref[...] = _layer_norm(DEEPNORM_ALPHA * x1_ref[...] + (1.0 + gate2) * y, g_ref[...], b_ref[...])


def _combine(ys, d1, d2, rg, x1, mod, ln_g, ln_b, seq):
    t, d = x1.shape
    tm = ROW_TILE
    spb = seq // tm
    row = lambda i, *_: (i, 0)
    const = lambda i, *_: (0, 0)
    buf = pltpu.VMEM((2, tm) + ys.shape[1:], F32)
    return pl.pallas_call(
        _combine_kernel,
        out_shape=jax.ShapeDtypeStruct((t, d), F32),
        grid_spec=pltpu.PrefetchScalarGridSpec(
            num_scalar_prefetch=2,
            grid=(t // tm,),
            in_specs=[pl.BlockSpec(memory_space=pl.ANY),
                      pl.BlockSpec((tm, LANES), row), pl.BlockSpec((tm, d), row),
                      pl.BlockSpec((1, 6, d), lambda i, *_: (i // spb, 0, 0)),
                      pl.BlockSpec((1, d), const), pl.BlockSpec((1, d), const)],
            out_specs=pl.BlockSpec((tm, d), row),
            scratch_shapes=[buf, buf, pltpu.SemaphoreType.DMA((2,))]),
        compiler_params=_params(("arbitrary",)),
        name="combine",
    )(d1, d2, ys, rg, x1, mod, ln_g.reshape(1, d), ln_b.reshape(1, d))


def _layer(x, c, positions, w_ada, b_ada, w_in, conv_w, a_log, dt_bias, gdn_norm_w, attn_norm_w,
           w_o, ln1_g, ln1_b, w_rg, b_rg, w_re, b_re, w_gate, w_up, w_down, ln2_g, ln2_b):
    bsz, seq, d = x.shape
    t = bsz * seq
    xf = x.reshape(t, d)
    mod = _ada(c, w_ada, b_ada)
    qkv, z, bdc, bdr, qb, kb, vb = _inproj(xf, mod, positions.reshape(t, 1), w_in, conv_w, seq)
    gw = GDN_HEADS * HEAD_DIM
    oa = _gdn(qkv.reshape(bsz, seq, 3 * gw), z.reshape(bsz, seq, gw), bdc.reshape(bsz, seq, LANES), bdr,
              a_log, dt_bias, gdn_norm_w).reshape(t, gw)
    ob = _attn(qb, kb, vb, attn_norm_w, bsz, seq)
    x1, h2, ri, rg, cnt = _outproj(oa, ob, xf, mod, w_o, ln1_g, ln1_b, w_rg, b_rg, w_re, b_re, seq)

    bm = EXPERT_BLOCK
    counts = cnt[N_GROUPS:N_GROUPS + N_EXPERTS, 0].astype(jnp.int32)
    seg_end = jnp.cumsum(counts)
    seg_start = seg_end - counts
    first_blk = seg_start // bm
    n_per = jnp.where(counts > 0, (seg_end - 1) // bm - first_blk + 1, 0)
    item_end = jnp.cumsum(n_per)
    n_items = item_end[-1:]
    max_items = (2 * t) // bm + N_EXPERTS - 1
    w_idx = jnp.minimum(jnp.arange(max_items, dtype=jnp.int32), n_items[0] - 1)
    item_expert = jnp.minimum(jnp.sum(item_end[None, :] <= w_idx[:, None], axis=1), N_EXPERTS - 1).astype(jnp.int32)
    item_block = first_blk[item_expert] + w_idx - (item_end - n_per)[item_expert]
    item_lo = jnp.maximum(seg_start[item_expert] - item_block * bm, 0)
    item_hi = jnp.minimum(seg_end[item_expert] - item_block * bm, bm)
    expert_ids = jnp.arange(N_EXPERTS, dtype=jnp.int32)[:, None]

    def seg_of(e):
        return jnp.sum(jnp.where(e[None, :] == expert_ids, seg_start[:, None], 0), axis=0)

    d1 = seg_of(ri[0]) + ri[2]
    d2 = seg_of(ri[1]) + ri[3]

    xs = _dispatch(h2, d1, d2)
    ys = _experts(xs, item_block, item_expert, item_lo, item_hi, n_items, w_gate, w_up, w_down)
    out = _combine(ys, d1, d2, rg, x1, mod, ln2_g, ln2_b, seq)
    return out.reshape(bsz, seq, d)


def kernel(x, c, positions, w_ada, b_ada, w_in, conv_w, a_log, dt_bias, gdn_norm_w, attn_norm_w, w_o, ln1_g, ln1_b, w_router_group, b_router_group, w_router_expert, b_router_expert, w_gate, w_up, w_down, ln2_g, ln2_b):
    assert w_ada.shape[0] == DEPTH
    return _layer(x, c, positions, w_ada[0], b_ada[0], w_in[0], conv_w[0], a_log[0], dt_bias[0],
                  gdn_norm_w[0], attn_norm_w[0], w_o[0], ln1_g[0], ln1_b[0],
                  w_router_group[0], b_router_group[0], w_router_expert[0], b_router_expert[0],
                  w_gate[0], w_up[0], w_down[0], ln2_g[0], ln2_b[0])
```

```python
import functools
import math

import jax
import jax.numpy as jnp
from jax import lax
from jax.experimental import pallas as pl
from jax.experimental.pallas import tpu as pltpu

F32 = jnp.float32
BF16 = jnp.bfloat16
LOG2E = math.log2(math.e)

GDN_HEADS = 4
ATT_HEADS = 4
HEAD_DIM = 128
CONV_WIDTH = 4
DILATED_PATTERNS = ((128, 1), (512, 4), (2048, 16))
ROPE_THETA = 500000.0
ROPE_DIMS = HEAD_DIM // 4
N_GROUPS = 4
EXPERTS_PER_GROUP = 8
N_EXPERTS = N_GROUPS * EXPERTS_PER_GROUP
DEPTH = 1
DEEPNORM_ALPHA = (2.0 * DEPTH) ** 0.25
LN_EPS = 1e-5
RMS_EPS = 1e-6

LANES = 128
SUBLANES = 8
ROUTER_ROWS = -(-(N_GROUPS + N_EXPERTS) // SUBLANES) * SUBLANES
VMEM_LIMIT = 48 * 1024 * 1024

GDN_BLOCK = 128
GDN_TILE = 256
ATT_BLOCK = 128
ATT_TILE = 2048
ATT_SUB = DILATED_PATTERNS[1][1]
ATT_WIDE = DILATED_PATTERNS[2][1]
assert DILATED_PATTERNS[0][1] == 1 and ATT_WIDE == ATT_SUB * ATT_SUB
assert all(w // d == ATT_BLOCK for w, d in DILATED_PATTERNS) and ATT_TILE == ATT_BLOCK * ATT_WIDE
ADA_COLS = 1024
PROJ_TILE = 512
ROW_TILE = 256
DISPATCH_TILE = 4096
EXPERT_BLOCK = 256
OUTPROJ_PARTS = 2
DMA_UNROLL = 8
NEG = -1e30


def _bf(x):
    return x.astype(BF16)


def _mm(a, b):
    return jnp.dot(_bf(a), _bf(b), preferred_element_type=F32)


def _mm_nt(a, b):
    return lax.dot_general(_bf(a), _bf(b), (((1,), (1,)), ((), ())), preferred_element_type=F32)


def _mm_tn(a, b):
    return lax.dot_general(_bf(a), _bf(b), (((0,), (0,)), ((), ())), preferred_element_type=F32)


def _split3(x):
    x1 = _bf(x)
    r1 = x - x1.astype(F32)
    x2 = _bf(r1)
    return x1, x2, _bf(r1 - x2.astype(F32))


def _sigmoid(x):
    return 1.0 / (1.0 + jnp.exp(-x))


def _silu(x):
    return x * _sigmoid(x)


def _softplus(x):
    return jnp.maximum(x, 0.0) + jnp.log(1.0 + jnp.exp(-jnp.abs(x)))


def _params(sem):
    return pltpu.CompilerParams(dimension_semantics=sem, vmem_limit_bytes=VMEM_LIMIT)


def _ada_kernel(ct_ref, w_ref, b_ref, o_ref, *, bsz):
    sc = _silu(ct_ref[...])
    w = w_ref[...]
    rows = [jnp.sum(w * sc[:, b:b + 1], axis=0, keepdims=True) for b in range(bsz)]
    rows.append(jnp.zeros((o_ref.shape[0] - bsz, w.shape[1]), F32))
    o_ref[...] = jnp.concatenate(rows, axis=0) + b_ref[...]


def _ada(c, w_ada, b_ada):
    bsz, d = c.shape
    n = w_ada.shape[1]
    tn = ADA_COLS
    assert bsz <= SUBLANES
    ct = jnp.zeros((d, LANES), F32).at[:, :bsz].set(c.T)
    out = pl.pallas_call(
        functools.partial(_ada_kernel, bsz=bsz),
        out_shape=jax.ShapeDtypeStruct((SUBLANES, n), F32),
        grid=(n // tn,),
        in_specs=[pl.BlockSpec((d, LANES), lambda j: (0, 0)),
                  pl.BlockSpec((d, tn), lambda j: (0, j)),
                  pl.BlockSpec((1, tn), lambda j: (0, j))],
        out_specs=pl.BlockSpec((SUBLANES, tn), lambda j: (0, j)),
        compiler_params=_params(("parallel",)),
        name="ada",
    )(ct, w_ada, b_ada.reshape(1, n))
    return out[:bsz].reshape(bsz, 6, d)


def _inproj_kernel(x_ref, mod_ref, pos_ref, invf_ref, convw_ref, win_ref,
                   qkv_ref, z_ref, bdc_ref, bdr_ref, qb_ref, kb_ref, vb_ref, cbuf, w_ref, *, steps_per_seq):
    tm = x_ref.shape[0]
    halo = SUBLANES
    n_slabs = cbuf.shape[0]
    gw, aw = z_ref.shape[1], qb_ref.shape[1]
    o_z = qkv_ref.shape[1]
    o_bd = o_z + gw
    o_q = o_bd + LANES
    n_gate = win_ref.shape[1] - o_bd - 3 * aw

    @pl.when(pl.program_id(0) == 0)
    def _():
        w_ref[:, :o_bd] = _bf(win_ref[:, :o_bd])
        w_ref[:, o_bd:o_q] = _bf(jnp.concatenate(
            [win_ref[:, o_bd:o_bd + n_gate], jnp.zeros((win_ref.shape[0], LANES - n_gate), F32)], axis=1))
        w_ref[:, o_q:] = _bf(win_ref[:, o_bd + n_gate:])

    w_z, w_bd = w_ref.at[:, o_z:o_bd], w_ref.at[:, o_bd:o_q]
    w_q, w_k, w_v = (w_ref.at[:, o_q + i * aw:o_q + (i + 1) * aw] for i in range(3))

    @pl.when(pl.program_id(0) % steps_per_seq == 0)
    def _():
        cbuf[:, 0:halo, :] = jnp.zeros((n_slabs, halo, LANES), F32)

    half = ROPE_DIMS // 2
    groups = LANES // half
    lane = lax.broadcasted_iota(jnp.int32, (1, LANES), 1)
    first = lane < half
    rot = lane < ROPE_DIMS
    ang = pos_ref[0].astype(F32) * invf_ref[...]
    cos_c = jnp.cos(ang)
    sin_c = jnp.sin(ang)
    cos_parts, sin_parts = [], []
    for j in range(groups):
        lo_sh = (LANES - half * j) % LANES
        hi_sh = (LANES - half * j + half) % LANES
        c_lo = pltpu.roll(cos_c, lo_sh, 1) if lo_sh else cos_c
        c_hi = pltpu.roll(cos_c, hi_sh, 1) if hi_sh else cos_c
        s_lo = pltpu.roll(sin_c, lo_sh, 1) if lo_sh else sin_c
        s_hi = pltpu.roll(sin_c, hi_sh, 1) if hi_sh else sin_c
        cos_parts.append(jnp.where(first, c_lo, jnp.where(rot, c_hi, 1.0)))
        sin_parts.append(jnp.where(first, -s_lo, jnp.where(rot, s_hi, 0.0)))
    cosv = jnp.concatenate(cos_parts, axis=0)
    sin_signed = jnp.concatenate(sin_parts, axis=0)

    def rope(y):
        outs = []
        for hh in range(ATT_HEADS):
            yh = y[:, hh * HEAD_DIM:(hh + 1) * HEAD_DIM]
            partner = jnp.where(first, pltpu.roll(yh, LANES - half, 1), pltpu.roll(yh, half, 1))
            outs.append(yh * cosv + partner * sin_signed)
        return jnp.concatenate(outs, axis=1)

    def conv_slice(s):
        sl = slice(s * HEAD_DIM, (s + 1) * HEAD_DIM)
        off = halo - (CONV_WIDTH - 1)
        acc = convw_ref[0:1, sl] * cbuf[s, off:off + tm, :]
        for j in range(1, CONV_WIDTH):
            acc = acc + convw_ref[j:j + 1, sl] * cbuf[s, off + j:off + j + tm, :]
        cbuf[s, 0:halo, :] = cbuf[s, tm:tm + halo, :]
        y = _silu(acc)
        if s < 2 * GDN_HEADS:
            y = y * lax.rsqrt(jnp.sum(y * y, axis=-1, keepdims=True) + RMS_EPS)
        if s < GDN_HEADS:
            y = y * (HEAD_DIM ** -0.5)
        qkv_ref[:, sl] = y

    shift = mod_ref[0, 0:1, :]
    scale = mod_ref[0, 1:2, :]
    h = _bf(x_ref[...] * (1.0 + scale) + shift)
    chunk = 2 * HEAD_DIM
    n_chunks = n_slabs // 2

    def project_chunk(c):
        pre = jnp.dot(h, w_ref[:, c * chunk:(c + 1) * chunk], preferred_element_type=F32)
        cbuf[2 * c, halo:halo + tm, :] = pre[:, :HEAD_DIM]
        cbuf[2 * c + 1, halo:halo + tm, :] = pre[:, HEAD_DIM:]

    def conv_chunk(c):
        conv_slice(2 * c)
        conv_slice(2 * c + 1)

    project_chunk(0)
    for c in range(1, n_chunks):
        project_chunk(c)
        conv_chunk(c - 1)
    qb = jnp.dot(h, w_q[...], preferred_element_type=F32)
    conv_chunk(n_chunks - 1)
    kb = jnp.dot(h, w_k[...], preferred_element_type=F32)
    qb_ref[...] = rope(qb) * (HEAD_DIM ** -0.5 * LOG2E)
    vb_ref[...] = jnp.dot(h, w_v[...], preferred_element_type=F32)
    kb_ref[...] = rope(kb)
    z_ref[...] = jnp.dot(h, w_z[...], preferred_element_type=F32)
    bdc = jnp.dot(h, w_bd[...], preferred_element_type=F32)
    bdc_ref[...] = bdc
    bdr_ref[0] = bdc.T[:SUBLANES]


def _inproj(xf, mod, pos, w_in, conv_w, seq):
    t, d = xf.shape
    tm = PROJ_TILE
    gw = GDN_HEADS * HEAD_DIM
    aw = ATT_HEADS * HEAD_DIM
    o0 = 3 * gw
    o1 = o0 + gw
    o2 = o1 + 2 * GDN_HEADS
    packed_cols = o1 + LANES + 3 * aw
    half = ROPE_DIMS // 2
    groups = LANES // half
    inv_freq = ROPE_THETA ** (-jnp.arange(half, dtype=F32) * 2.0 / ROPE_DIMS)
    invf = jnp.tile(inv_freq, groups).reshape(1, LANES)
    pos = jnp.repeat(pos.reshape(t // tm, groups, tm // groups).transpose(0, 2, 1), half, axis=2)
    spb = seq // tm
    row = lambda i: (i, 0)
    const = lambda i: (0, 0)
    return pl.pallas_call(
        functools.partial(_inproj_kernel, steps_per_seq=spb),
        out_shape=(jax.ShapeDtypeStruct((t, o0), F32), jax.ShapeDtypeStruct((t, gw), F32),
                   jax.ShapeDtypeStruct((t, LANES), F32), jax.ShapeDtypeStruct((t // seq, SUBLANES, seq), F32),
                   jax.ShapeDtypeStruct((t, aw), F32), jax.ShapeDtypeStruct((t, aw), F32),
                   jax.ShapeDtypeStruct((t, aw), F32)),
        grid=(t // tm,),
        in_specs=[pl.BlockSpec((tm, d), row),
                  pl.BlockSpec((1, 6, d), lambda i: (i // spb, 0, 0)),
                  pl.BlockSpec((1, tm // groups, LANES), lambda i: (i, 0, 0)),
                  pl.BlockSpec((1, LANES), const), pl.BlockSpec((CONV_WIDTH, o0), const),
                  pl.BlockSpec((None,) + w_in.shape[1:], lambda i: (0, 0, 0), pipeline_mode=pl.Buffered(1))],
        out_specs=(pl.BlockSpec((tm, o0), row), pl.BlockSpec((tm, gw), row),
                   pl.BlockSpec((tm, LANES), row),
                   pl.BlockSpec((1, SUBLANES, tm), lambda i: (i // spb, 0, i % spb)),
                   pl.BlockSpec((tm, aw), row), pl.BlockSpec((tm, aw), row),
                   pl.BlockSpec((tm, aw), row)),
        scratch_shapes=[pltpu.VMEM((o0 // HEAD_DIM, tm + 2 * SUBLANES, HEAD_DIM), F32),
                        pltpu.VMEM((d, packed_cols), BF16)],
        compiler_params=_params(("arbitrary",)),
        name="inproj",
    )(xf, mod, pos, invf, conv_w, w_in)


def _gdn_kernel(qkv_ref, z_ref, bdc_ref, bdr_ref, gpc_ref, gpr_ref, nw_ref, o_ref,
                u_s, w_s, qd_s, kd_s, qk_s, a_s, x_s, y_s, st_ref):
    nbat, ts = qkv_ref.shape[0], qkv_ref.shape[1]
    nb = ts // GDN_BLOCK
    gw = GDN_HEADS * HEAD_DIM

    @pl.when(pl.program_id(0) == 0)
    def _():
        st_ref[...] = jnp.zeros(st_ref.shape, F32)

    blk_n = GDN_BLOCK
    hd = HEAD_DIM
    ti = lax.broadcasted_iota(jnp.int32, (ts, ts), 0)
    tj = lax.broadcasted_iota(jnp.int32, (ts, ts), 1)
    same = (ti // blk_n) == (tj // blk_n)
    m_low = jnp.where(same & (tj <= ti), 1.0, 0.0).astype(BF16)
    m_up = jnp.where(same & (ti <= tj), 1.0, 0.0).astype(BF16)

    split3 = _split3

    beta_c = [_sigmoid(bdc_ref[bb]) for bb in range(nbat)]
    g_c = jnp.concatenate([-jnp.exp(gpc_ref[0:1, :]) * LOG2E * _softplus(bdc_ref[bb] + gpc_ref[1:2, :])
                           for bb in range(nbat)], axis=1)
    g_r = jnp.concatenate([-jnp.exp(gpr_ref[:, 0:1]) * LOG2E * _softplus(bdr_ref[bb] + gpr_ref[:, 1:2])
                           for bb in range(nbat)], axis=0)
    gc_c = sum(jnp.dot(m_low, part, preferred_element_type=F32) for part in split3(g_c))
    gc_r = sum(jnp.dot(part, m_up, preferred_element_type=F32) for part in split3(g_r))

    def cat2(m):
        return jnp.concatenate([m, m], axis=1)

    ii = lax.broadcasted_iota(jnp.int32, (blk_n, blk_n), 0)
    jj = lax.broadcasted_iota(jnp.int32, (blk_n, blk_n), 1)
    lower = cat2(jj <= ii)
    strict = cat2(jj < ii)
    eye = cat2(jnp.where(ii == jj, 1.0, 0.0).astype(F32))
    levels = []
    b = 1
    while b < blk_n:
        levels.append(cat2(((ii // b) == (jj // b) + 1) & (((jj // b) % 2) == 0)))
        b *= 2

    def block_diag(rp):
        n, m = rp.shape[0], rp.shape[1] // 2
        z = jnp.zeros((n, m), rp.dtype)
        return jnp.concatenate([jnp.concatenate([rp[:, :m], z], axis=1),
                                jnp.concatenate([z, rp[:, m:]], axis=1)], axis=0)

    def mm2(lp, rp):
        return jnp.dot(_bf(lp), block_diag(_bf(rp)), preferred_element_type=F32)

    def mm2_nt(lp, rp):
        return lax.dot_general(_bf(lp), block_diag(_bf(rp)), (((1,), (1,)), ((), ())),
                               preferred_element_type=F32)

    chains = [(bb, j, pp) for bb in range(nbat) for j in range(nb) for pp in range(GDN_HEADS // 2)]

    def tile_of(bb, j, pp):
        return bb, slice(j * blk_n, (j + 1) * blk_n), slice(2 * pp * hd, 2 * (pp + 1) * hd)

    def col_pair(arr, rows, col):
        return jnp.concatenate([jnp.broadcast_to(arr[rows, col:col + 1], (blk_n, hd)),
                                jnp.broadcast_to(arr[rows, col + 1:col + 2], (blk_n, hd))], axis=1)

    for c, (bb, j, pp) in enumerate(chains):
        blk = tile_of(bb, j, pp)
        rows, cols = blk[1], blk[2]
        last = slice((j + 1) * blk_n - 1, (j + 1) * blk_n)
        gcol = bb * LANES + GDN_HEADS + 2 * pp
        grow = bb * SUBLANES + GDN_HEADS + 2 * pp
        q = qkv_ref[bb, rows, cols]
        k = qkv_ref[bb, rows, slice(gw + cols.start, gw + cols.stop)]
        v = qkv_ref[bb, rows, slice(2 * gw + cols.start, 2 * gw + cols.stop)]
        beta = col_pair(beta_c[bb], rows, 2 * pp)
        gcc = col_pair(gc_c, rows, gcol)
        gtc = col_pair(gc_c, last, gcol)
        gcr = jnp.concatenate([jnp.broadcast_to(gc_r[grow:grow + 1, rows], (blk_n, hd)),
                               jnp.broadcast_to(gc_r[grow + 1:grow + 2, rows], (blk_n, hd))], axis=1)
        kb = k * beta
        eg = jnp.exp2(gcc)
        dm = jnp.where(lower, jnp.exp2(gcc - gcr), 0.0)
        a = jnp.where(strict, mm2_nt(kb, k) * dm, 0.0)
        a_s[c] = a
        x_s[c] = eye - jnp.where(levels[0], a, 0.0)
        u_s[blk] = v * beta
        w_s[blk] = kb * eg
        qd_s[blk] = q * eg
        kd_s[blk] = k * jnp.exp2(gtc - gcc)
        qk_s[blk] = mm2_nt(q, k) * dm

    for lm in levels[1:]:
        for c in range(len(chains)):
            y_s[c] = mm2(x_s[c], jnp.where(lm, a_s[c], 0.0))
        for c in range(len(chains)):
            xc = x_s[c]
            x_s[c] = xc - mm2(y_s[c], xc)

    for c, (bb, j, pp) in enumerate(chains):
        blk = tile_of(bb, j, pp)
        u, w = _bf(u_s[blk]), _bf(w_s[blk])
        z = jnp.zeros((blk_n, 2 * hd), BF16)
        rhs = jnp.concatenate([jnp.concatenate([u[:, :hd], w[:, :hd], z], axis=1),
                               jnp.concatenate([z, u[:, hd:], w[:, hd:]], axis=1)], axis=0)
        sol = jnp.dot(_bf(x_s[c]), rhs, preferred_element_type=F32)
        u_s[blk] = jnp.concatenate([sol[:, 0:hd], sol[:, 2 * hd:3 * hd]], axis=1)
        w_s[blk] = jnp.concatenate([sol[:, hd:2 * hd], sol[:, 3 * hd:]], axis=1)

    for j in range(nb):
        for bb in range(nbat):
            for pp in range(GDN_HEADS // 2):
                blk = tile_of(bb, j, pp)
                si = bb * (GDN_HEADS // 2) + pp
                last = slice((j + 1) * blk_n - 1, (j + 1) * blk_n)
                gcol = bb * LANES + GDN_HEADS + 2 * pp
                state = st_ref[si]
                proj = mm2(jnp.concatenate([w_s[blk], qd_s[blk]], axis=0), state)
                v_new = u_s[blk] - proj[:blk_n]
                o = proj[blk_n:] + mm2(qk_s[blk], v_new)
                g_last = jnp.exp2(jnp.concatenate(
                    [jnp.broadcast_to(gc_c[last, gcol:gcol + 1], (1, hd)),
                     jnp.broadcast_to(gc_c[last, gcol + 1:gcol + 2], (1, hd))], axis=1))
                kd = kd_s[blk]
                upd = jnp.concatenate([_mm_tn(kd[:, :hd], v_new[:, :hd]),
                                       _mm_tn(kd[:, hd:], v_new[:, hd:])], axis=1)
                st_ref[si] = state * g_last + upd
                zz = z_ref[blk]
                halves = []
                for hf in range(2):
                    oh = o[:, hf * hd:(hf + 1) * hd]
                    halves.append(oh * lax.rsqrt(jnp.mean(oh * oh, axis=-1, keepdims=True) + RMS_EPS)
                                  * nw_ref[...])
                o_ref[blk] = (jnp.concatenate(halves, axis=1) * _silu(zz)).astype(o_ref.dtype)


def _gdn(qkv, z, bdc, bdr, a_log, dt_bias, norm_w):
    bsz, seq, _ = qkv.shape
    ts = GDN_TILE
    gw = GDN_HEADS * HEAD_DIM
    zeros4 = jnp.zeros((GDN_HEADS,), F32)
    al = jnp.concatenate([zeros4, a_log])
    db = jnp.concatenate([zeros4, dt_bias])
    gpc = jnp.zeros((2, LANES), F32).at[0, :2 * GDN_HEADS].set(al).at[1, :2 * GDN_HEADS].set(db)
    gpr = jnp.stack([al, db], axis=1)
    row = lambda i: (0, i, 0)
    const = lambda i: (0, 0)
    n_chains = bsz * (ts // GDN_BLOCK) * (GDN_HEADS // 2)
    tile = pltpu.VMEM((bsz, ts, gw), F32)
    mats = pltpu.VMEM((n_chains, GDN_BLOCK, 2 * GDN_BLOCK), F32)
    return pl.pallas_call(
        _gdn_kernel,
        out_shape=jax.ShapeDtypeStruct((bsz, seq, gw), BF16),
        grid=(seq // ts,),
        in_specs=[pl.BlockSpec((bsz, ts, 3 * gw), row), pl.BlockSpec((bsz, ts, gw), row),
                  pl.BlockSpec((bsz, ts, LANES), row),
                  pl.BlockSpec((bsz, SUBLANES, ts), lambda i: (0, 0, i)),
                  pl.BlockSpec((2, LANES), const), pl.BlockSpec((2 * GDN_HEADS, 2), const),
                  pl.BlockSpec((1, HEAD_DIM), const)],
        out_specs=pl.BlockSpec((bsz, ts, gw), row),
        scratch_shapes=[tile, tile, tile, tile, tile,
                        mats, mats, mats,
                        pltpu.VMEM((bsz * GDN_HEADS // 2, HEAD_DIM, 2 * HEAD_DIM), F32)],
        compiler_params=_params(("arbitrary",)),
        name="gdn",
    )(qkv, z, bdc, bdr, gpc, gpr, norm_w.reshape(1, HEAD_DIM))


def _attn_kernel(q_ref, k_ref, v_ref, nw_ref, o_ref,
                 q4, k4, v4, ktail, vtail, m_s, l_s, acc_s, tmp_s, nat_s):
    tq = q_ref.shape[0]
    blk = ATT_BLOCK
    sub = ATT_SUB
    nq = tq // sub
    t = pl.program_id(2)
    slot = t % 2
    other = 1 - slot
    qi = lax.broadcasted_iota(jnp.int32, (blk, 2 * blk), 0)
    kj = lax.broadcasted_iota(jnp.int32, (blk, 2 * blk), 1)
    band = (kj >= qi) & (kj <= qi + blk)
    first_lo = jnp.where(t > 0, 0, blk)
    band_first = band & (kj >= first_lo)

    @pl.when(t == 0)
    def _():
        k4[...] = jnp.zeros(k4.shape, F32)
        v4[...] = jnp.zeros(v4.shape, F32)
        ktail[...] = jnp.zeros(ktail.shape, F32)
        vtail[...] = jnp.zeros(vtail.shape, F32)

    for r in range(sub):
        rows = slice(r * nq, (r + 1) * nq)
        src = pl.ds(r, nq, stride=sub)
        q4[rows, :] = q_ref[src, :]
        k4[slot, rows, :] = k_ref[src, :]
        v4[slot, rows, :] = v_ref[src, :]

    def block_stats(q, kcat, vcat, from_prev_tile):
        s = _mm_nt(q, kcat)
        s = jnp.where(band_first if from_prev_tile else band, s, NEG)
        m = jnp.max(s, axis=-1, keepdims=True)
        p = jnp.exp2(s - m)
        l = jnp.sum(p, axis=-1, keepdims=True)
        return m, l, _mm(p, vcat)

    def merge(dst, m, l, o):
        m_old = m_s[dst, :]
        m_new = jnp.maximum(m_old, m)
        w_old = jnp.exp2(m_old - m_new)
        w_cur = jnp.exp2(m - m_new)
        m_s[dst, :] = m_new
        l_s[dst, :] = w_old * l_s[dst, :] + w_cur * l
        acc_s[dst, :] = w_old * acc_s[dst, :] + w_cur * o

    for jb in range(tq // blk):
        cur = slice(jb * blk, (jb + 1) * blk)
        if jb > 0:
            kcat = k_ref[(jb - 1) * blk:(jb + 1) * blk, :]
            vcat = v_ref[(jb - 1) * blk:(jb + 1) * blk, :]
        else:
            kcat = jnp.concatenate([ktail[...], k_ref[cur, :]], axis=0)
            vcat = jnp.concatenate([vtail[...], v_ref[cur, :]], axis=0)
        m, l, o = block_stats(q_ref[cur, :], kcat, vcat, jb == 0)
        tmp_s[0] = jnp.broadcast_to(m, (blk, HEAD_DIM))
        tmp_s[1] = jnp.broadcast_to(l, (blk, HEAD_DIM))
        tmp_s[2] = o
        per = blk // sub
        for r in range(sub):
            dst = slice(r * nq + jb * per, r * nq + (jb + 1) * per)
            src = pl.ds(r, per, stride=sub)
            m_s[dst, :] = tmp_s[0, src, :]
            l_s[dst, :] = tmp_s[1, src, :]
            acc_s[dst, :] = tmp_s[2, src, :]

    for r in range(sub):
        for jb in range(nq // blk):
            base = r * nq + jb * blk
            cur = slice(base, base + blk)
            if jb > 0:
                kcat = k4[slot, base - blk:base + blk, :]
                vcat = v4[slot, base - blk:base + blk, :]
            else:
                last = slice((r + 1) * nq - blk, (r + 1) * nq)
                kcat = jnp.concatenate([k4[other, last, :], k4[slot, cur, :]], axis=0)
                vcat = jnp.concatenate([v4[other, last, :], v4[slot, cur, :]], axis=0)
            m, l, o = block_stats(q4[cur, :], kcat, vcat, jb == 0)
            merge(cur, m, l, o)

    for c in range(ATT_WIDE):
        sl = pl.ds((c % sub) * nq + c // sub, blk, stride=sub)
        kcat = jnp.concatenate([k4[other, sl, :], k4[slot, sl, :]], axis=0)
        vcat = jnp.concatenate([v4[other, sl, :], v4[slot, sl, :]], axis=0)
        m, l, o = block_stats(q4[sl, :], kcat, vcat, True)
        merge(sl, m, l, o)

    out = acc_s[...] / l_s[...]
    out = out * lax.rsqrt(jnp.mean(out * out, axis=-1, keepdims=True) + RMS_EPS) * nw_ref[...]
    for r in range(sub):
        nat_s[pl.ds(r, nq, stride=sub), :] = out[r * nq:(r + 1) * nq]
    o_ref[...] = nat_s[...].astype(o_ref.dtype)
    ktail[...] = k_ref[tq - blk:tq, :]
    vtail[...] = v_ref[tq - blk:tq, :]


def _attn(qb, kb, vb, norm_w, bsz, seq):
    t = qb.shape[0]
    tq = ATT_TILE
    spb = seq // tq
    cur = lambda b, h, i: (b * spb + i, h)
    blk = pl.BlockSpec((tq, HEAD_DIM), cur)
    tile = pltpu.VMEM((tq, HEAD_DIM), F32)
    ring = pltpu.VMEM((2, tq, HEAD_DIM), F32)
    tail = pltpu.VMEM((ATT_BLOCK, HEAD_DIM), F32)
    return pl.pallas_call(
        _attn_kernel,
        out_shape=jax.ShapeDtypeStruct((t, ATT_HEADS * HEAD_DIM), BF16),
        grid=(bsz, ATT_HEADS, spb),
        in_specs=[blk, blk, blk, pl.BlockSpec((1, HEAD_DIM), lambda b, h, i: (0, 0))],
        out_specs=blk,
        scratch_shapes=[tile, ring, ring, tail, tail, tile, tile, tile,
                        pltpu.VMEM((3, ATT_BLOCK, HEAD_DIM), F32), tile],
        compiler_params=_params(("parallel", "parallel", "arbitrary")),
        name="attn",
    )(qb, kb, vb, norm_w.reshape(1, HEAD_DIM))


def _layer_norm(y, g, b):
    mu = jnp.mean(y, axis=-1, keepdims=True)
    yc = y - mu
    var = jnp.mean(yc * yc, axis=-1, keepdims=True)
    return yc * lax.rsqrt(var + LN_EPS) * g + b


def _outproj_kernel(oa_ref, ob_ref, x_ref, mod_ref, wo_ref, g_ref, b_ref,
                    wrh_ref, wrl_ref, br_ref,
                    x1_ref, h2_ref, ri_ref, rg_ref, cnt_ref, run_s, mix_s, hi_s, lo_s):
    @pl.when(pl.program_id(0) == 0)
    def _():
        run_s[...] = jnp.zeros(run_s.shape, F32)

    tm = x_ref.shape[0]
    gate1 = mod_ref[0, 2:3, :]
    shift2 = mod_ref[0, 3:4, :]
    scale2 = mod_ref[0, 4:5, :]
    n_parts = OUTPROJ_PARTS
    part = tm // n_parts
    grp_rows = 2 * SUBLANES

    def project(p):
        rows = slice(p * part, (p + 1) * part)
        mix_s[rows, :] = jnp.dot(jnp.concatenate([oa_ref[rows, :], ob_ref[rows, :]], axis=1), wo_ref[...],
                                 preferred_element_type=F32)

    def normalise(p):
        for r0 in range(p * part, (p + 1) * part, grp_rows):
            rows = slice(r0, r0 + grp_rows)
            x1 = _layer_norm(DEEPNORM_ALPHA * x_ref[rows, :] + (1.0 + gate1) * mix_s[rows, :],
                             g_ref[...], b_ref[...])
            x1_ref[rows, :] = x1
            h2 = x1 * (1.0 + scale2) + shift2
            h2_ref[rows] = h2.reshape(grp_rows, SUBLANES, LANES)
            hi = _bf(h2)
            hi_s[rows, :] = hi
            lo_s[rows, :] = _bf(h2 - hi.astype(F32))

    nt = (((1,), (1,)), ((), ()))

    def route(p):
        rows = slice(p * part, (p + 1) * part)
        hi = hi_s[rows, :]
        by_token = (jnp.dot(hi, wrh_ref[...], preferred_element_type=F32)
                    + jnp.dot(hi, wrl_ref[...], preferred_element_type=F32)
                    + jnp.dot(lo_s[rows, :], wrh_ref[...], preferred_element_type=F32))
        return by_token.T[:ROUTER_ROWS]

    project(0)
    parts = []
    for p in range(n_parts):
        if p + 1 < n_parts:
            project(p + 1)
        normalise(p)
        parts.append(route(p))
    logits = jnp.concatenate(parts, axis=1) + br_ref[:, 0:1]
    nr = logits.shape[0]
    row = lax.broadcasted_iota(jnp.int32, (nr, tm), 0).astype(F32)
    lg = jnp.where(row < N_GROUPS, logits, NEG)
    mg = jnp.max(lg, axis=0, keepdims=True)
    grp = jnp.min(jnp.where(lg == mg, row, float(nr)), axis=0, keepdims=True)
    gate_grp = 1.0 / jnp.sum(jnp.exp(lg - mg), axis=0, keepdims=True)
    first_row = N_GROUPS + EXPERTS_PER_GROUP * grp
    sel = (row >= first_row) & (row < first_row + EXPERTS_PER_GROUP)
    le = jnp.where(sel, logits, NEG)
    v1 = jnp.max(le, axis=0, keepdims=True)
    i1 = jnp.min(jnp.where(le == v1, row, float(nr)), axis=0, keepdims=True)
    le2 = jnp.where(row == i1, NEG, le)
    v2 = jnp.max(le2, axis=0, keepdims=True)
    i2 = jnp.min(jnp.where(le2 == v2, row, float(nr)), axis=0, keepdims=True)
    e21 = jnp.exp(v2 - v1)
    g1 = gate_grp / (1.0 + e21)
    g2 = gate_grp * e21 / (1.0 + e21)

    oh1 = row == i1
    oh2 = row == i2
    onehot = jnp.where(oh1 | oh2, 1.0, 0.0).astype(F32)
    ti = lax.broadcasted_iota(jnp.int32, (tm, tm), 0)
    tj = lax.broadcasted_iota(jnp.int32, (tm, tm), 1)
    earlier = jnp.where(ti < tj, 1.0, 0.0).astype(F32)
    tot = _mm(onehot, earlier) + run_s[:, 0:1]
    r1 = jnp.sum(jnp.where(oh1, tot, 0.0), axis=0, keepdims=True)
    r2 = jnp.sum(jnp.where(oh2, tot, 0.0), axis=0, keepdims=True)
    run_s[...] = run_s[...] + jnp.sum(onehot, axis=1, keepdims=True)
    cnt_ref[...] = run_s[...]

    sub_i = lax.broadcasted_iota(jnp.int32, (SUBLANES, tm), 0)
    ri = jnp.where(sub_i == 0, i1 - N_GROUPS, 0.0)
    ri = jnp.where(sub_i == 1, i2 - N_GROUPS, ri)
    ri = jnp.where(sub_i == 2, r1, ri)
    ri = jnp.where(sub_i == 3, r2, ri)
    ri_ref[...] = ri.astype(jnp.int32)

    gates = jnp.where(sub_i == 0, g1, jnp.where(sub_i == 1, g2, 0.0))
    pick = jnp.where(lax.broadcasted_iota(jnp.int32, (SUBLANES, LANES), 0)
                     == lax.broadcasted_iota(jnp.int32, (SUBLANES, LANES), 1), 1.0, 0.0).astype(BF16)
    tn = (((0,), (0,)), ((), ()))
    rg_ref[...] = sum(lax.dot_general(term, pick, tn, preferred_element_type=F32)
                      for term in _split3(gates))


def _outproj(oa, ob, xf, mod, w_o, ln_g, ln_b, w_rg, b_rg, w_re, b_re, seq):
    t, d = xf.shape
    tm = PROJ_TILE
    gw = oa.shape[1]
    wo = _bf(w_o)
    nr = ROUTER_ROWS
    n_log = N_GROUPS + N_EXPERTS
    wr = jnp.concatenate([w_rg, w_re, jnp.zeros((d, LANES - n_log), F32)], axis=1)
    wrh = _bf(wr)
    wrl = _bf(wr - wrh.astype(F32))
    br = jnp.concatenate([b_rg, b_re, jnp.zeros((nr - n_log,), F32)])
    br = jnp.broadcast_to(br[:, None], (nr, LANES))
    spb = seq // tm
    row = lambda i: (i, 0)
    const = lambda i: (0, 0)
    return pl.pallas_call(
        _outproj_kernel,
        out_shape=(jax.ShapeDtypeStruct((t, d), F32), jax.ShapeDtypeStruct((t, d // LANES, LANES), F32),
                   jax.ShapeDtypeStruct((SUBLANES, t), jnp.int32), jax.ShapeDtypeStruct((t, LANES), F32),
                   jax.ShapeDtypeStruct((nr, LANES), F32)),
        grid=(t // tm,),
        in_specs=[pl.BlockSpec((tm, gw), row), pl.BlockSpec((tm, gw), row), pl.BlockSpec((tm, d), row),
                  pl.BlockSpec((1, 6, d), lambda i: (i // spb, 0, 0)),
                  pl.BlockSpec((2 * gw, d), const),
                  pl.BlockSpec((1, d), const), pl.BlockSpec((1, d), const),
                  pl.BlockSpec((d, LANES), const), pl.BlockSpec((d, LANES), const),
                  pl.BlockSpec((nr, LANES), const)],
        out_specs=(pl.BlockSpec((tm, d), row), pl.BlockSpec((tm, d // LANES, LANES), lambda i: (i, 0, 0)),
                   pl.BlockSpec((SUBLANES, tm), lambda i: (0, i)), pl.BlockSpec((tm, LANES), row),
                   pl.BlockSpec((nr, LANES), const)),
        scratch_shapes=[pltpu.VMEM((nr, LANES), F32), pltpu.VMEM((tm, d), F32),
                        pltpu.VMEM((tm, d), BF16), pltpu.VMEM((tm, d), BF16)],
        compiler_params=_params(("arbitrary",)),
        name="outproj",
    )(oa, ob, xf, mod, wo, ln_g.reshape(1, d), ln_b.reshape(1, d), wrh, wrl, br)


def _dispatch_kernel(d1_ref, d2_ref, h_ref, xs_ref, sem):
    tm = h_ref.shape[0]
    i = pl.program_id(0)

    def row_copy(tk, dest):
        return pltpu.make_async_copy(h_ref.at[tk], xs_ref.at[dest], sem)

    def issue(g, carry):
        for u in range(DMA_UNROLL):
            tk = g * DMA_UNROLL + u
            tok = i * tm + tk
            row_copy(tk, d1_ref[tok]).start(priority=0)
            row_copy(tk, d2_ref[tok]).start(priority=1)
        return carry

    lax.fori_loop(0, tm // DMA_UNROLL, issue, 0)

    tile_copy = pltpu.make_async_copy(h_ref, xs_ref.at[pl.ds(0, tm)], sem)
    tile_copy.wait()
    tile_copy.wait()


def _dispatch(h2, d1, d2):
    t, sub, lanes = h2.shape
    tm = DISPATCH_TILE
    return pl.pallas_call(
        _dispatch_kernel,
        out_shape=jax.ShapeDtypeStruct((2 * t, sub, lanes), F32),
        grid_spec=pltpu.PrefetchScalarGridSpec(
            num_scalar_prefetch=2,
            grid=(t // tm,),
            in_specs=[pl.BlockSpec((tm, sub, lanes), lambda i, *_: (i, 0, 0))],
            out_specs=pl.BlockSpec(memory_space=pl.ANY),
            scratch_shapes=[pltpu.SemaphoreType.DMA]),
        compiler_params=_params(("arbitrary",)),
        name="dispatch",
    )(d1, d2, h2)


def _experts_kernel(wb_ref, we_ref, lo_ref, hi_ref, nw_ref, first_ref, ring_ref, next_ref,
                    xs_ref, wg_hbm, wu_hbm, wd_hbm, ys_ref,
                    wg_f, wu_f, wd_f, wg_s, wu_s, wd_s, sem):
    w = pl.program_id(0)

    def fetch(expert, slot):
        return (pltpu.make_async_copy(wg_hbm.at[expert], wg_f.at[slot], sem.at[slot]),
                pltpu.make_async_copy(wu_hbm.at[expert], wu_f.at[slot], sem.at[slot]),
                pltpu.make_async_copy(wd_hbm.at[expert], wd_f.at[slot], sem.at[slot]))

    @pl.when(w == 0)
    def _():
        for cp in fetch(we_ref[0], 0):
            cp.start()

    @pl.when((first_ref[w] == 1) & (w < nw_ref[0]))
    def _():
        slot = ring_ref[w]
        for cp in fetch(we_ref[w], slot):
            cp.wait()

        @pl.when(next_ref[w] >= 0)
        def _():
            for cp in fetch(next_ref[w], 1 - slot):
                cp.start()

        wg_s[...] = _bf(wg_f[slot])
        wu_s[...] = _bf(wu_f[slot])
        wd_s[...] = _bf(wd_f[slot])

    bm, sub, lanes = xs_ref.shape
    half = bm // 2
    lo, hi = lo_ref[w], hi_ref[w]
    live = w < nw_ref[0]

    def mlp(rows):
        x = _bf(xs_ref[rows].reshape(rows.stop - rows.start, sub * lanes))
        hid = (_silu(jnp.dot(x, wg_s[...], preferred_element_type=F32))
               * jnp.dot(x, wu_s[...], preferred_element_type=F32))
        y = jnp.dot(_bf(hid), wd_s[...], preferred_element_type=F32)
        return y.reshape(rows.stop - rows.start, sub, lanes)

    def put(rows, y, first_visit):
        row = lax.broadcasted_iota(jnp.int32, (rows.stop - rows.start, 1, 1), 0) + rows.start
        mine = (row >= lo) & (row < hi)
        ys_ref[rows] = jnp.where(mine, y, 0.0 if first_visit else ys_ref[rows])

    whole, lower, upper = slice(0, bm), slice(0, half), slice(half, bm)

    @pl.when(live & (lo < half) & (hi > half) & (lo == 0))
    def _():
        put(whole, mlp(whole), True)

    @pl.when(live & (lo < half) & (hi > half) & (lo > 0))
    def _():
        put(whole, mlp(whole), False)

    @pl.when(live & (hi <= half) & (lo == 0))
    def _():
        put(lower, mlp(lower), True)
        ys_ref[upper] = jnp.zeros((bm - half, sub, lanes), F32)

    @pl.when(live & (hi <= half) & (lo > 0))
    def _():
        put(lower, mlp(lower), False)

    @pl.when(live & (lo >= half))
    def _():
        put(upper, mlp(upper), False)


def _experts(xs, item_block, item_expert, item_lo, item_hi, n_items, w_gate, w_up, w_down):
    n_slots, sub, lanes = xs.shape
    d = sub * lanes
    ff = w_gate.shape[2]
    bm = EXPERT_BLOCK
    n = item_block.shape[0]
    idx = jnp.arange(n, dtype=jnp.int32)
    first = jnp.concatenate([jnp.ones((1,), jnp.int32),
                             (item_expert[1:] != item_expert[:-1]).astype(jnp.int32)])
    ring = (jnp.cumsum(first) - 1) % 2
    next_first = lax.cummin(jnp.where(first == 1, idx, n), reverse=True)
    next_first = jnp.concatenate([next_first[1:], jnp.full((1,), n, jnp.int32)])
    nxt = jnp.where(next_first < n, item_expert[jnp.minimum(next_first, n - 1)], -1).astype(jnp.int32)
    slot = lambda w, *_: (_[0][w], 0, 0)
    return pl.pallas_call(
        _experts_kernel,
        out_shape=jax.ShapeDtypeStruct((n_slots, sub, lanes), F32),
        grid_spec=pltpu.PrefetchScalarGridSpec(
            num_scalar_prefetch=8,
            grid=(n,),
            in_specs=[pl.BlockSpec((bm, sub, lanes), slot),
                      pl.BlockSpec(memory_space=pl.ANY), pl.BlockSpec(memory_space=pl.ANY),
                      pl.BlockSpec(memory_space=pl.ANY)],
            out_specs=pl.BlockSpec((bm, sub, lanes), slot),
            scratch_shapes=[pltpu.VMEM((2, d, ff), F32), pltpu.VMEM((2, d, ff), F32),
                            pltpu.VMEM((2, ff, d), F32),
                            pltpu.VMEM((d, ff), BF16), pltpu.VMEM((d, ff), BF16),
                            pltpu.VMEM((ff, d), BF16), pltpu.SemaphoreType.DMA((2,))]),
        compiler_params=_params(("arbitrary",)),
        name="experts",
    )(item_block, item_expert, item_lo, item_hi, n_items, first, ring.astype(jnp.int32), nxt,
      xs, w_gate, w_up, w_down)


def _combine_kernel(d1_ref, d2_ref, ys_ref, rg_ref, x1_ref, mod_ref, g_ref, b_ref, o_ref, ya, yb, sem):
    tm = x1_ref.shape[0]
    i = pl.program_id(0)
    n = pl.num_programs(0)

    def row_copy(dest, buf, slot, tk):
        return pltpu.make_async_copy(ys_ref.at[dest], buf.at[slot, tk], sem.at[slot])

    slot = i % 2
    gate2 = mod_ref[0, 5:6, :]

    def issue_group(step, dst_slot, g):
        for u in range(DMA_UNROLL):
            tk = g * DMA_UNROLL + u
            tok = step * tm + tk
            row_copy(d1_ref[tok], ya, dst_slot, tk).start(priority=0)
            row_copy(d2_ref[tok], yb, dst_slot, tk).start(priority=1)

    def gather_tile(step, dst_slot):
        def issue(g, carry):
            issue_group(step, dst_slot, g)
            return carry

        lax.fori_loop(0, tm // DMA_UNROLL, issue, 0)

    @pl.when(i == 0)
    def _():
        gather_tile(0, 0)

    @pl.when(i + 1 < n)
    def _():
        gather_tile(i + 1, 1 - slot)

    pltpu.make_async_copy(ys_ref.at[pl.ds(0, tm)], ya.at[slot], sem.at[slot]).wait()
    pltpu.make_async_copy(ys_ref.at[pl.ds(0, tm)], yb.at[slot], sem.at[slot]).wait()

    rg = rg_ref[...]
    d = x1_ref.shape[1]
    y = rg[:, 0:1] * ya[slot].reshape(tm, d) + rg[:, 1:2] * yb[slot].reshape(tm, d)
    o_ref[...] = _layer_norm(DEEPNORM_ALPHA * x1_ref[...] + (1.0 + gate2) * y, g_ref[...], b_ref[...])


def _combine(ys, d1, d2, rg, x1, mod, ln_g, ln_b, seq):
    t, d = x1.shape
    tm = ROW_TILE
    spb = seq // tm
    row = lambda i, *_: (i, 0)
    const = lambda i, *_: (0, 0)
    buf = pltpu.VMEM((2, tm) + ys.shape[1:], F32)
    return pl.pallas_call(
        _combine_kernel,
        out_shape=jax.ShapeDtypeStruct((t, d), F32),
        grid_spec=pltpu.PrefetchScalarGridSpec(
            num_scalar_prefetch=2,
            grid=(t // tm,),
            in_specs=[pl.BlockSpec(memory_space=pl.ANY),
                      pl.BlockSpec((tm, LANES), row), pl.BlockSpec((tm, d), row),
                      pl.BlockSpec((1, 6, d), lambda i, *_: (i // spb, 0, 0)),
                      pl.BlockSpec((1, d), const), pl.BlockSpec((1, d), const)],
            out_specs=pl.BlockSpec((tm, d), row),
            scratch_shapes=[buf, buf, pltpu.SemaphoreType.DMA((2,))]),
        compiler_params=_params(("arbitrary",)),
        name="combine",
    )(d1, d2, ys, rg, x1, mod, ln_g.reshape(1, d), ln_b.reshape(1, d))


def _layer(x, c, positions, w_ada, b_ada, w_in, conv_w, a_log, dt_bias, gdn_norm_w, attn_norm_w,
           w_o, ln1_g, ln1_b, w_rg, b_rg, w_re, b_re, w_gate, w_up, w_down, ln2_g, ln2_b):
    bsz, seq, d = x.shape
    t = bsz * seq
    xf = x.reshape(t, d)
    mod = _ada(c, w_ada, b_ada)
    qkv, z, bdc, bdr, qb, kb, vb = _inproj(xf, mod, positions.reshape(t, 1), w_in, conv_w, seq)
    gw = GDN_HEADS * HEAD_DIM
    oa = _gdn(qkv.reshape(bsz, seq, 3 * gw), z.reshape(bsz, seq, gw), bdc.reshape(bsz, seq, LANES), bdr,
              a_log, dt_bias, gdn_norm_w).reshape(t, gw)
    ob = _attn(qb, kb, vb, attn_norm_w, bsz, seq)
    x1, h2, ri, rg, cnt = _outproj(oa, ob, xf, mod, w_o, ln1_g, ln1_b, w_rg, b_rg, w_re, b_re, seq)

    bm = EXPERT_BLOCK
    counts = cnt[N_GROUPS:N_GROUPS + N_EXPERTS, 0].astype(jnp.int32)
    seg_end = jnp.cumsum(counts)
    seg_start = seg_end - counts
    first_blk = seg_start // bm
    n_per = jnp.where(counts > 0, (seg_end - 1) // bm - first_blk + 1, 0)
    item_end = jnp.cumsum(n_per)
    n_items = item_end[-1:]
    max_items = (2 * t) // bm + N_EXPERTS - 1
    w_idx = jnp.minimum(jnp.arange(max_items, dtype=jnp.int32), n_items[0] - 1)
    item_expert = jnp.minimum(jnp.sum(item_end[None, :] <= w_idx[:, None], axis=1), N_EXPERTS - 1).astype(jnp.int32)
    item_block = first_blk[item_expert] + w_idx - (item_end - n_per)[item_expert]
    item_lo = jnp.maximum(seg_start[item_expert] - item_block * bm, 0)
    item_hi = jnp.minimum(seg_end[item_expert] - item_block * bm, bm)
    expert_ids = jnp.arange(N_EXPERTS, dtype=jnp.int32)[:, None]

    def seg_of(e):
        return jnp.sum(jnp.where(e[None, :] == expert_ids, seg_start[:, None], 0), axis=0)

    d1 = seg_of(ri[0]) + ri[2]
    d2 = seg_of(ri[1]) + ri[3]

    xs = _dispatch(h2, d1, d2)
    ys = _experts(xs, item_block, item_expert, item_lo, item_hi, n_items, w_gate, w_up, w_down)
    out = _combine(ys, d1, d2, rg, x1, mod, ln2_g, ln2_b, seq)
    return out.reshape(bsz, seq, d)


def kernel(x, c, positions, w_ada, b_ada, w_in, conv_w, a_log, dt_bias, gdn_norm_w, attn_norm_w, w_o, ln1_g, ln1_b, w_router_group, b_router_group, w_router_expert, b_router_expert, w_gate, w_up, w_down, ln2_g, ln2_b):
    assert w_ada.shape[0] == DEPTH
    return _layer(x, c, positions, w_ada[0], b_ada[0], w_in, conv_w[0], a_log[0], dt_bias[0],
                  gdn_norm_w[0], attn_norm_w[0], w_o[0], ln1_g[0], ln1_b[0],
                  w_router_group[0], b_router_group[0], w_router_expert[0], b_router_expert[0],
                  w_gate[0], w_up[0], w_down[0], ln2_g[0], ln2_b[0])
```

```python
import functools
import math

import jax
import jax.numpy as jnp
from jax import lax
from jax.experimental import pallas as pl
from jax.experimental.pallas import tpu as pltpu

F32 = jnp.float32
BF16 = jnp.bfloat16
LOG2E = math.log2(math.e)

GDN_HEADS = 4
ATT_HEADS = 4
HEAD_DIM = 128
CONV_WIDTH = 4
DILATED_PATTERNS = ((128, 1), (512, 4), (2048, 16))
ROPE_THETA = 500000.0
ROPE_DIMS = HEAD_DIM // 4
N_GROUPS = 4
EXPERTS_PER_GROUP = 8
N_EXPERTS = N_GROUPS * EXPERTS_PER_GROUP
DEPTH = 1
DEEPNORM_ALPHA = (2.0 * DEPTH) ** 0.25
LN_EPS = 1e-5
RMS_EPS = 1e-6

LANES = 128
SUBLANES = 8
ROUTER_ROWS = -(-(N_GROUPS + N_EXPERTS) // SUBLANES) * SUBLANES
VMEM_LIMIT = 48 * 1024 * 1024

GDN_BLOCK = 128
GDN_TILE = 256
ATT_BLOCK = 128
ATT_TILE = 2048
ATT_SUB = DILATED_PATTERNS[1][1]
ATT_WIDE = DILATED_PATTERNS[2][1]
assert DILATED_PATTERNS[0][1] == 1 and ATT_WIDE == ATT_SUB * ATT_SUB
assert all(w // d == ATT_BLOCK for w, d in DILATED_PATTERNS) and ATT_TILE == ATT_BLOCK * ATT_WIDE
ADA_COLS = 1024
PROJ_TILE = 512
ROW_TILE = 256
DISPATCH_TILE = 4096
EXPERT_BLOCK = 256
OUTPROJ_PARTS = 2
DMA_UNROLL = 8
NEG = -1e30


def _bf(x):
    return x.astype(BF16)


def _mm(a, b):
    return jnp.dot(_bf(a), _bf(b), preferred_element_type=F32)


def _mm_nt(a, b):
    return lax.dot_general(_bf(a), _bf(b), (((1,), (1,)), ((), ())), preferred_element_type=F32)


def _mm_tn(a, b):
    return lax.dot_general(_bf(a), _bf(b), (((0,), (0,)), ((), ())), preferred_element_type=F32)


def _split3(x):
    x1 = _bf(x)
    r1 = x - x1.astype(F32)
    x2 = _bf(r1)
    return x1, x2, _bf(r1 - x2.astype(F32))


def _sigmoid(x):
    return 1.0 / (1.0 + jnp.exp(-x))


def _silu(x):
    return x * _sigmoid(x)


def _softplus(x):
    return jnp.maximum(x, 0.0) + jnp.log(1.0 + jnp.exp(-jnp.abs(x)))


def _params(sem):
    return pltpu.CompilerParams(dimension_semantics=sem, vmem_limit_bytes=VMEM_LIMIT)


def _ada_kernel(ct_ref, w_ref, b_ref, o_ref, *, bsz):
    sc = _silu(ct_ref[...])
    w = w_ref[...]
    rows = [jnp.sum(w * sc[:, b:b + 1], axis=0, keepdims=True) for b in range(bsz)]
    rows.append(jnp.zeros((o_ref.shape[0] - bsz, w.shape[1]), F32))
    o_ref[...] = jnp.concatenate(rows, axis=0) + b_ref[...]


def _ada(c, w_ada, b_ada):
    bsz, d = c.shape
    n = w_ada.shape[1]
    tn = ADA_COLS
    assert bsz <= SUBLANES
    ct = jnp.zeros((d, LANES), F32).at[:, :bsz].set(c.T)
    out = pl.pallas_call(
        functools.partial(_ada_kernel, bsz=bsz),
        out_shape=jax.ShapeDtypeStruct((SUBLANES, n), F32),
        grid=(n // tn,),
        in_specs=[pl.BlockSpec((d, LANES), lambda j: (0, 0)),
                  pl.BlockSpec((d, tn), lambda j: (0, j)),
                  pl.BlockSpec((1, tn), lambda j: (0, j))],
        out_specs=pl.BlockSpec((SUBLANES, tn), lambda j: (0, j)),
        compiler_params=_params(("parallel",)),
        name="ada",
    )(ct, w_ada, b_ada.reshape(1, n))
    return out[:bsz].reshape(bsz, 6, d)


def _inproj_kernel(x_ref, mod_ref, pos_ref, invf_ref, convw_ref, win_ref,
                   qkv_ref, z_ref, bdc_ref, bdr_ref, qb_ref, kb_ref, vb_ref, cbuf, w_ref, *, steps_per_seq):
    tm = x_ref.shape[0]
    halo = SUBLANES
    n_slabs = cbuf.shape[0]
    gw, aw = z_ref.shape[1], qb_ref.shape[1]
    o_z = qkv_ref.shape[1]
    o_bd = o_z + gw
    o_q = o_bd + LANES
    n_gate = win_ref.shape[0] - o_bd - 3 * aw

    @pl.when(pl.program_id(0) == 0)
    def _():
        def unpack(src_row, dst_col):
            w_ref[:, dst_col:dst_col + LANES] = _bf(win_ref[src_row:src_row + LANES, :].T)

        for c in range(o_bd // LANES):
            unpack(c * LANES, c * LANES)
        gate_rows = jnp.concatenate([win_ref[o_bd:o_bd + n_gate, :],
                                     jnp.zeros((LANES - n_gate, win_ref.shape[1]), F32)], axis=0)
        w_ref[:, o_bd:o_q] = _bf(gate_rows.T)
        for c in range(3 * aw // LANES):
            unpack(o_bd + n_gate + c * LANES, o_q + c * LANES)

    w_z, w_bd = w_ref.at[:, o_z:o_bd], w_ref.at[:, o_bd:o_q]
    w_q, w_k, w_v = (w_ref.at[:, o_q + i * aw:o_q + (i + 1) * aw] for i in range(3))

    @pl.when(pl.program_id(0) % steps_per_seq == 0)
    def _():
        cbuf[:, 0:halo, :] = jnp.zeros((n_slabs, halo, LANES), F32)

    half = ROPE_DIMS // 2
    groups = LANES // half
    lane = lax.broadcasted_iota(jnp.int32, (1, LANES), 1)
    first = lane < half
    rot = lane < ROPE_DIMS
    ang = pos_ref[0].astype(F32) * invf_ref[...]
    cos_c = jnp.cos(ang)
    sin_c = jnp.sin(ang)
    cos_parts, sin_parts = [], []
    for j in range(groups):
        lo_sh = (LANES - half * j) % LANES
        hi_sh = (LANES - half * j + half) % LANES
        c_lo = pltpu.roll(cos_c, lo_sh, 1) if lo_sh else cos_c
        c_hi = pltpu.roll(cos_c, hi_sh, 1) if hi_sh else cos_c
        s_lo = pltpu.roll(sin_c, lo_sh, 1) if lo_sh else sin_c
        s_hi = pltpu.roll(sin_c, hi_sh, 1) if hi_sh else sin_c
        cos_parts.append(jnp.where(first, c_lo, jnp.where(rot, c_hi, 1.0)))
        sin_parts.append(jnp.where(first, -s_lo, jnp.where(rot, s_hi, 0.0)))
    cosv = jnp.concatenate(cos_parts, axis=0)
    sin_signed = jnp.concatenate(sin_parts, axis=0)

    def rope(y):
        outs = []
        for hh in range(ATT_HEADS):
            yh = y[:, hh * HEAD_DIM:(hh + 1) * HEAD_DIM]
            partner = jnp.where(first, pltpu.roll(yh, LANES - half, 1), pltpu.roll(yh, half, 1))
            outs.append(yh * cosv + partner * sin_signed)
        return jnp.concatenate(outs, axis=1)

    def conv_slice(s):
        sl = slice(s * HEAD_DIM, (s + 1) * HEAD_DIM)
        off = halo - (CONV_WIDTH - 1)
        acc = convw_ref[0:1, sl] * cbuf[s, off:off + tm, :]
        for j in range(1, CONV_WIDTH):
            acc = acc + convw_ref[j:j + 1, sl] * cbuf[s, off + j:off + j + tm, :]
        cbuf[s, 0:halo, :] = cbuf[s, tm:tm + halo, :]
        y = _silu(acc)
        if s < 2 * GDN_HEADS:
            y = y * lax.rsqrt(jnp.sum(y * y, axis=-1, keepdims=True) + RMS_EPS)
        if s < GDN_HEADS:
            y = y * (HEAD_DIM ** -0.5)
        qkv_ref[:, sl] = y

    shift = mod_ref[0, 0:1, :]
    scale = mod_ref[0, 1:2, :]
    h = _bf(x_ref[...] * (1.0 + scale) + shift)
    chunk = 2 * HEAD_DIM
    n_chunks = n_slabs // 2

    def project_chunk(c):
        pre = jnp.dot(h, w_ref[:, c * chunk:(c + 1) * chunk], preferred_element_type=F32)
        cbuf[2 * c, halo:halo + tm, :] = pre[:, :HEAD_DIM]
        cbuf[2 * c + 1, halo:halo + tm, :] = pre[:, HEAD_DIM:]

    def conv_chunk(c):
        conv_slice(2 * c)
        conv_slice(2 * c + 1)

    project_chunk(0)
    for c in range(1, n_chunks):
        project_chunk(c)
        conv_chunk(c - 1)
    qb = jnp.dot(h, w_q[...], preferred_element_type=F32)
    conv_chunk(n_chunks - 1)
    kb = jnp.dot(h, w_k[...], preferred_element_type=F32)
    qb_ref[...] = rope(qb) * (HEAD_DIM ** -0.5 * LOG2E)
    vb_ref[...] = jnp.dot(h, w_v[...], preferred_element_type=F32)
    kb_ref[...] = rope(kb)
    z_ref[...] = jnp.dot(h, w_z[...], preferred_element_type=F32)
    bdc = jnp.dot(h, w_bd[...], preferred_element_type=F32)
    bdc_ref[...] = bdc
    bdr_ref[0] = bdc.T[:SUBLANES]


def _inproj(xf, mod, pos, w_in, conv_w, seq):
    t, d = xf.shape
    tm = PROJ_TILE
    gw = GDN_HEADS * HEAD_DIM
    aw = ATT_HEADS * HEAD_DIM
    o0 = 3 * gw
    o1 = o0 + gw
    o2 = o1 + 2 * GDN_HEADS
    packed_cols = o1 + LANES + 3 * aw
    half = ROPE_DIMS // 2
    groups = LANES // half
    inv_freq = ROPE_THETA ** (-jnp.arange(half, dtype=F32) * 2.0 / ROPE_DIMS)
    invf = jnp.tile(inv_freq, groups).reshape(1, LANES)
    pos = jnp.repeat(pos.reshape(t // tm, groups, tm // groups).transpose(0, 2, 1), half, axis=2)
    spb = seq // tm
    row = lambda i: (i, 0)
    const = lambda i: (0, 0)
    return pl.pallas_call(
        functools.partial(_inproj_kernel, steps_per_seq=spb),
        out_shape=(jax.ShapeDtypeStruct((t, o0), F32), jax.ShapeDtypeStruct((t, gw), F32),
                   jax.ShapeDtypeStruct((t, LANES), F32), jax.ShapeDtypeStruct((t // seq, SUBLANES, seq), F32),
                   jax.ShapeDtypeStruct((t, aw), F32), jax.ShapeDtypeStruct((t, aw), F32),
                   jax.ShapeDtypeStruct((t, aw), F32)),
        grid=(t // tm,),
        in_specs=[pl.BlockSpec((tm, d), row),
                  pl.BlockSpec((1, 6, d), lambda i: (i // spb, 0, 0)),
                  pl.BlockSpec((1, tm // groups, LANES), lambda i: (i, 0, 0)),
                  pl.BlockSpec((1, LANES), const), pl.BlockSpec((CONV_WIDTH, o0), const),
                  pl.BlockSpec(w_in.shape[::-1], const, pipeline_mode=pl.Buffered(1))],
        out_specs=(pl.BlockSpec((tm, o0), row), pl.BlockSpec((tm, gw), row),
                   pl.BlockSpec((tm, LANES), row),
                   pl.BlockSpec((1, SUBLANES, tm), lambda i: (i // spb, 0, i % spb)),
                   pl.BlockSpec((tm, aw), row), pl.BlockSpec((tm, aw), row),
                   pl.BlockSpec((tm, aw), row)),
        scratch_shapes=[pltpu.VMEM((o0 // HEAD_DIM, tm + 2 * SUBLANES, HEAD_DIM), F32),
                        pltpu.VMEM((d, packed_cols), BF16)],
        compiler_params=_params(("arbitrary",)),
        name="inproj",
    )(xf, mod, pos, invf, conv_w, w_in.T)


def _gdn_kernel(qkv_ref, z_ref, bdc_ref, bdr_ref, gpc_ref, gpr_ref, nw_ref, o_ref,
                u_s, w_s, qd_s, kd_s, qk_s, a_s, x_s, y_s, st_ref):
    nbat, ts = qkv_ref.shape[0], qkv_ref.shape[1]
    nb = ts // GDN_BLOCK
    gw = GDN_HEADS * HEAD_DIM

    @pl.when(pl.program_id(0) == 0)
    def _():
        st_ref[...] = jnp.zeros(st_ref.shape, F32)

    blk_n = GDN_BLOCK
    hd = HEAD_DIM
    ti = lax.broadcasted_iota(jnp.int32, (ts, ts), 0)
    tj = lax.broadcasted_iota(jnp.int32, (ts, ts), 1)
    same = (ti // blk_n) == (tj // blk_n)
    m_low = jnp.where(same & (tj <= ti), 1.0, 0.0).astype(BF16)
    m_up = jnp.where(same & (ti <= tj), 1.0, 0.0).astype(BF16)

    split3 = _split3

    beta_c = [_sigmoid(bdc_ref[bb]) for bb in range(nbat)]
    g_c = jnp.concatenate([-jnp.exp(gpc_ref[0:1, :]) * LOG2E * _softplus(bdc_ref[bb] + gpc_ref[1:2, :])
                           for bb in range(nbat)], axis=1)
    g_r = jnp.concatenate([-jnp.exp(gpr_ref[:, 0:1]) * LOG2E * _softplus(bdr_ref[bb] + gpr_ref[:, 1:2])
                           for bb in range(nbat)], axis=0)
    gc_c = sum(jnp.dot(m_low, part, preferred_element_type=F32) for part in split3(g_c))
    gc_r = sum(jnp.dot(part, m_up, preferred_element_type=F32) for part in split3(g_r))

    def cat2(m):
        return jnp.concatenate([m, m], axis=1)

    ii = lax.broadcasted_iota(jnp.int32, (blk_n, blk_n), 0)
    jj = lax.broadcasted_iota(jnp.int32, (blk_n, blk_n), 1)
    lower = cat2(jj <= ii)
    strict = cat2(jj < ii)
    eye = cat2(jnp.where(ii == jj, 1.0, 0.0).astype(F32))
    levels = []
    b = 1
    while b < blk_n:
        levels.append(cat2(((ii // b) == (jj // b) + 1) & (((jj // b) % 2) == 0)))
        b *= 2

    def block_diag(rp):
        n, m = rp.shape[0], rp.shape[1] // 2
        z = jnp.zeros((n, m), rp.dtype)
        return jnp.concatenate([jnp.concatenate([rp[:, :m], z], axis=1),
                                jnp.concatenate([z, rp[:, m:]], axis=1)], axis=0)

    def mm2(lp, rp):
        return jnp.dot(_bf(lp), block_diag(_bf(rp)), preferred_element_type=F32)

    def mm2_nt(lp, rp):
        return lax.dot_general(_bf(lp), block_diag(_bf(rp)), (((1,), (1,)), ((), ())),
                               preferred_element_type=F32)

    chains = [(bb, j, pp) for bb in range(nbat) for j in range(nb) for pp in range(GDN_HEADS // 2)]

    def tile_of(bb, j, pp):
        return bb, slice(j * blk_n, (j + 1) * blk_n), slice(2 * pp * hd, 2 * (pp + 1) * hd)

    def col_pair(arr, rows, col):
        return jnp.concatenate([jnp.broadcast_to(arr[rows, col:col + 1], (blk_n, hd)),
                                jnp.broadcast_to(arr[rows, col + 1:col + 2], (blk_n, hd))], axis=1)

    for c, (bb, j, pp) in enumerate(chains):
        blk = tile_of(bb, j, pp)
        rows, cols = blk[1], blk[2]
        last = slice((j + 1) * blk_n - 1, (j + 1) * blk_n)
        gcol = bb * LANES + GDN_HEADS + 2 * pp
        grow = bb * SUBLANES + GDN_HEADS + 2 * pp
        q = qkv_ref[bb, rows, cols]
        k = qkv_ref[bb, rows, slice(gw + cols.start, gw + cols.stop)]
        v = qkv_ref[bb, rows, slice(2 * gw + cols.start, 2 * gw + cols.stop)]
        beta = col_pair(beta_c[bb], rows, 2 * pp)
        gcc = col_pair(gc_c, rows, gcol)
        gtc = col_pair(gc_c, last, gcol)
        gcr = jnp.concatenate([jnp.broadcast_to(gc_r[grow:grow + 1, rows], (blk_n, hd)),
                               jnp.broadcast_to(gc_r[grow + 1:grow + 2, rows], (blk_n, hd))], axis=1)
        kb = k * beta
        eg = jnp.exp2(gcc)
        dm = jnp.where(lower, jnp.exp2(gcc - gcr), 0.0)
        a = jnp.where(strict, mm2_nt(kb, k) * dm, 0.0)
        a_s[c] = a
        x_s[c] = eye - jnp.where(levels[0], a, 0.0)
        u_s[blk] = v * beta
        w_s[blk] = kb * eg
        qd_s[blk] = q * eg
        kd_s[blk] = k * jnp.exp2(gtc - gcc)
        qk_s[blk] = mm2_nt(q, k) * dm

    for lm in levels[1:]:
        for c in range(len(chains)):
            y_s[c] = mm2(x_s[c], jnp.where(lm, a_s[c], 0.0))
        for c in range(len(chains)):
            xc = x_s[c]
            x_s[c] = xc - mm2(y_s[c], xc)

    for c, (bb, j, pp) in enumerate(chains):
        blk = tile_of(bb, j, pp)
        u, w = _bf(u_s[blk]), _bf(w_s[blk])
        z = jnp.zeros((blk_n, 2 * hd), BF16)
        rhs = jnp.concatenate([jnp.concatenate([u[:, :hd], w[:, :hd], z], axis=1),
                               jnp.concatenate([z, u[:, hd:], w[:, hd:]], axis=1)], axis=0)
        sol = jnp.dot(_bf(x_s[c]), rhs, preferred_element_type=F32)
        u_s[blk] = jnp.concatenate([sol[:, 0:hd], sol[:, 2 * hd:3 * hd]], axis=1)
        w_s[blk] = jnp.concatenate([sol[:, hd:2 * hd], sol[:, 3 * hd:]], axis=1)

    for j in range(nb):
        for bb in range(nbat):
            for pp in range(GDN_HEADS // 2):
                blk = tile_of(bb, j, pp)
                si = bb * (GDN_HEADS // 2) + pp
                last = slice((j + 1) * blk_n - 1, (j + 1) * blk_n)
                gcol = bb * LANES + GDN_HEADS + 2 * pp
                state = st_ref[si]
                proj = mm2(jnp.concatenate([w_s[blk], qd_s[blk]], axis=0), state)
                v_new = u_s[blk] - proj[:blk_n]
                o = proj[blk_n:] + mm2(qk_s[blk], v_new)
                g_last = jnp.exp2(jnp.concatenate(
                    [jnp.broadcast_to(gc_c[last, gcol:gcol + 1], (1, hd)),
                     jnp.broadcast_to(gc_c[last, gcol + 1:gcol + 2], (1, hd))], axis=1))
                kd = kd_s[blk]
                upd = jnp.concatenate([_mm_tn(kd[:, :hd], v_new[:, :hd]),
                                       _mm_tn(kd[:, hd:], v_new[:, hd:])], axis=1)
                st_ref[si] = state * g_last + upd
                zz = z_ref[blk]
                halves = []
                for hf in range(2):
                    oh = o[:, hf * hd:(hf + 1) * hd]
                    halves.append(oh * lax.rsqrt(jnp.mean(oh * oh, axis=-1, keepdims=True) + RMS_EPS)
                                  * nw_ref[...])
                o_ref[blk] = (jnp.concatenate(halves, axis=1) * _silu(zz)).astype(o_ref.dtype)


def _gdn(qkv, z, bdc, bdr, a_log, dt_bias, norm_w):
    bsz, seq, _ = qkv.shape
    ts = GDN_TILE
    gw = GDN_HEADS * HEAD_DIM
    zeros4 = jnp.zeros((GDN_HEADS,), F32)
    al = jnp.concatenate([zeros4, a_log])
    db = jnp.concatenate([zeros4, dt_bias])
    gpc = jnp.zeros((2, LANES), F32).at[0, :2 * GDN_HEADS].set(al).at[1, :2 * GDN_HEADS].set(db)
    gpr = jnp.stack([al, db], axis=1)
    row = lambda i: (0, i, 0)
    const = lambda i: (0, 0)
    n_chains = bsz * (ts // GDN_BLOCK) * (GDN_HEADS // 2)
    tile = pltpu.VMEM((bsz, ts, gw), F32)
    mats = pltpu.VMEM((n_chains, GDN_BLOCK, 2 * GDN_BLOCK), F32)
    return pl.pallas_call(
        _gdn_kernel,
        out_shape=jax.ShapeDtypeStruct((bsz, seq, gw), BF16),
        grid=(seq // ts,),
        in_specs=[pl.BlockSpec((bsz, ts, 3 * gw), row), pl.BlockSpec((bsz, ts, gw), row),
                  pl.BlockSpec((bsz, ts, LANES), row),
                  pl.BlockSpec((bsz, SUBLANES, ts), lambda i: (0, 0, i)),
                  pl.BlockSpec((2, LANES), const), pl.BlockSpec((2 * GDN_HEADS, 2), const),
                  pl.BlockSpec((1, HEAD_DIM), const)],
        out_specs=pl.BlockSpec((bsz, ts, gw), row),
        scratch_shapes=[tile, tile, tile, tile, tile,
                        mats, mats, mats,
                        pltpu.VMEM((bsz * GDN_HEADS // 2, HEAD_DIM, 2 * HEAD_DIM), F32)],
        compiler_params=_params(("arbitrary",)),
        name="gdn",
    )(qkv, z, bdc, bdr, gpc, gpr, norm_w.reshape(1, HEAD_DIM))


def _attn_kernel(q_ref, k_ref, v_ref, nw_ref, o_ref,
                 q4, k4, v4, ktail, vtail, m_s, l_s, acc_s, tmp_s, nat_s):
    tq = q_ref.shape[0]
    blk = ATT_BLOCK
    sub = ATT_SUB
    nq = tq // sub
    t = pl.program_id(2)
    slot = t % 2
    other = 1 - slot
    qi = lax.broadcasted_iota(jnp.int32, (blk, 2 * blk), 0)
    kj = lax.broadcasted_iota(jnp.int32, (blk, 2 * blk), 1)
    band = (kj >= qi) & (kj <= qi + blk)
    first_lo = jnp.where(t > 0, 0, blk)
    band_first = band & (kj >= first_lo)

    @pl.when(t == 0)
    def _():
        k4[...] = jnp.zeros(k4.shape, F32)
        v4[...] = jnp.zeros(v4.shape, F32)
        ktail[...] = jnp.zeros(ktail.shape, F32)
        vtail[...] = jnp.zeros(vtail.shape, F32)

    for r in range(sub):
        rows = slice(r * nq, (r + 1) * nq)
        src = pl.ds(r, nq, stride=sub)
        q4[rows, :] = q_ref[src, :]
        k4[slot, rows, :] = k_ref[src, :]
        v4[slot, rows, :] = v_ref[src, :]

    def block_stats(q, kcat, vcat, from_prev_tile):
        s = _mm_nt(q, kcat)
        s = jnp.where(band_first if from_prev_tile else band, s, NEG)
        m = jnp.max(s, axis=-1, keepdims=True)
        p = jnp.exp2(s - m)
        l = jnp.sum(p, axis=-1, keepdims=True)
        return m, l, _mm(p, vcat)

    def merge(dst, m, l, o):
        m_old = m_s[dst, :]
        m_new = jnp.maximum(m_old, m)
        w_old = jnp.exp2(m_old - m_new)
        w_cur = jnp.exp2(m - m_new)
        m_s[dst, :] = m_new
        l_s[dst, :] = w_old * l_s[dst, :] + w_cur * l
        acc_s[dst, :] = w_old * acc_s[dst, :] + w_cur * o

    for jb in range(tq // blk):
        cur = slice(jb * blk, (jb + 1) * blk)
        if jb > 0:
            kcat = k_ref[(jb - 1) * blk:(jb + 1) * blk, :]
            vcat = v_ref[(jb - 1) * blk:(jb + 1) * blk, :]
        else:
            kcat = jnp.concatenate([ktail[...], k_ref[cur, :]], axis=0)
            vcat = jnp.concatenate([vtail[...], v_ref[cur, :]], axis=0)
        m, l, o = block_stats(q_ref[cur, :], kcat, vcat, jb == 0)
        tmp_s[0] = jnp.broadcast_to(m, (blk, HEAD_DIM))
        tmp_s[1] = jnp.broadcast_to(l, (blk, HEAD_DIM))
        tmp_s[2] = o
        per = blk // sub
        for r in range(sub):
            dst = slice(r * nq + jb * per, r * nq + (jb + 1) * per)
            src = pl.ds(r, per, stride=sub)
            m_s[dst, :] = tmp_s[0, src, :]
            l_s[dst, :] = tmp_s[1, src, :]
            acc_s[dst, :] = tmp_s[2, src, :]

    for r in range(sub):
        for jb in range(nq // blk):
            base = r * nq + jb * blk
            cur = slice(base, base + blk)
            if jb > 0:
                kcat = k4[slot, base - blk:base + blk, :]
                vcat = v4[slot, base - blk:base + blk, :]
            else:
                last = slice((r + 1) * nq - blk, (r + 1) * nq)
                kcat = jnp.concatenate([k4[other, last, :], k4[slot, cur, :]], axis=0)
                vcat = jnp.concatenate([v4[other, last, :], v4[slot, cur, :]], axis=0)
            m, l, o = block_stats(q4[cur, :], kcat, vcat, jb == 0)
            merge(cur, m, l, o)

    for c in range(ATT_WIDE):
        sl = pl.ds((c % sub) * nq + c // sub, blk, stride=sub)
        kcat = jnp.concatenate([k4[other, sl, :], k4[slot, sl, :]], axis=0)
        vcat = jnp.concatenate([v4[other, sl, :], v4[slot, sl, :]], axis=0)
        m, l, o = block_stats(q4[sl, :], kcat, vcat, True)
        merge(sl, m, l, o)

    out = acc_s[...] / l_s[...]
    out = out * lax.rsqrt(jnp.mean(out * out, axis=-1, keepdims=True) + RMS_EPS) * nw_ref[...]
    for r in range(sub):
        nat_s[pl.ds(r, nq, stride=sub), :] = out[r * nq:(r + 1) * nq]
    o_ref[...] = nat_s[...].astype(o_ref.dtype)
    ktail[...] = k_ref[tq - blk:tq, :]
    vtail[...] = v_ref[tq - blk:tq, :]


def _attn(qb, kb, vb, norm_w, bsz, seq):
    t = qb.shape[0]
    tq = ATT_TILE
    spb = seq // tq
    cur = lambda b, h, i: (b * spb + i, h)
    blk = pl.BlockSpec((tq, HEAD_DIM), cur)
    tile = pltpu.VMEM((tq, HEAD_DIM), F32)
    ring = pltpu.VMEM((2, tq, HEAD_DIM), F32)
    tail = pltpu.VMEM((ATT_BLOCK, HEAD_DIM), F32)
    return pl.pallas_call(
        _attn_kernel,
        out_shape=jax.ShapeDtypeStruct((t, ATT_HEADS * HEAD_DIM), BF16),
        grid=(bsz, ATT_HEADS, spb),
        in_specs=[blk, blk, blk, pl.BlockSpec((1, HEAD_DIM), lambda b, h, i: (0, 0))],
        out_specs=blk,
        scratch_shapes=[tile, ring, ring, tail, tail, tile, tile, tile,
                        pltpu.VMEM((3, ATT_BLOCK, HEAD_DIM), F32), tile],
        compiler_params=_params(("parallel", "parallel", "arbitrary")),
        name="attn",
    )(qb, kb, vb, norm_w.reshape(1, HEAD_DIM))


def _layer_norm(y, g, b):
    mu = jnp.mean(y, axis=-1, keepdims=True)
    yc = y - mu
    var = jnp.mean(yc * yc, axis=-1, keepdims=True)
    return yc * lax.rsqrt(var + LN_EPS) * g + b


def _outproj_kernel(oa_ref, ob_ref, x_ref, mod_ref, wo_ref, g_ref, b_ref,
                    wrh_ref, wrl_ref, br_ref,
                    x1_ref, h2_ref, ri_ref, rg_ref, cnt_ref, run_s, mix_s, hi_s, lo_s):
    @pl.when(pl.program_id(0) == 0)
    def _():
        run_s[...] = jnp.zeros(run_s.shape, F32)

    tm = x_ref.shape[0]
    gate1 = mod_ref[0, 2:3, :]
    shift2 = mod_ref[0, 3:4, :]
    scale2 = mod_ref[0, 4:5, :]
    n_parts = OUTPROJ_PARTS
    part = tm // n_parts
    grp_rows = 2 * SUBLANES

    def project(p):
        rows = slice(p * part, (p + 1) * part)
        mix_s[rows, :] = jnp.dot(jnp.concatenate([oa_ref[rows, :], ob_ref[rows, :]], axis=1), wo_ref[...],
                                 preferred_element_type=F32)

    def normalise(p):
        for r0 in range(p * part, (p + 1) * part, grp_rows):
            rows = slice(r0, r0 + grp_rows)
            x1 = _layer_norm(DEEPNORM_ALPHA * x_ref[rows, :] + (1.0 + gate1) * mix_s[rows, :],
                             g_ref[...], b_ref[...])
            x1_ref[rows, :] = x1
            h2 = x1 * (1.0 + scale2) + shift2
            h2_ref[rows] = h2.reshape(grp_rows, SUBLANES, LANES)
            hi = _bf(h2)
            hi_s[rows, :] = hi
            lo_s[rows, :] = _bf(h2 - hi.astype(F32))

    nt = (((1,), (1,)), ((), ()))

    def route(p):
        rows = slice(p * part, (p + 1) * part)
        hi = hi_s[rows, :]
        by_token = (jnp.dot(hi, wrh_ref[...], preferred_element_type=F32)
                    + jnp.dot(hi, wrl_ref[...], preferred_element_type=F32)
                    + jnp.dot(lo_s[rows, :], wrh_ref[...], preferred_element_type=F32))
        return by_token.T[:ROUTER_ROWS]

    project(0)
    parts = []
    for p in range(n_parts):
        if p + 1 < n_parts:
            project(p + 1)
        normalise(p)
        parts.append(route(p))
    logits = jnp.concatenate(parts, axis=1) + br_ref[:, 0:1]
    nr = logits.shape[0]
    row = lax.broadcasted_iota(jnp.int32, (nr, tm), 0).astype(F32)
    lg = jnp.where(row < N_GROUPS, logits, NEG)
    mg = jnp.max(lg, axis=0, keepdims=True)
    grp = jnp.min(jnp.where(lg == mg, row, float(nr)), axis=0, keepdims=True)
    gate_grp = 1.0 / jnp.sum(jnp.exp(lg - mg), axis=0, keepdims=True)
    first_row = N_GROUPS + EXPERTS_PER_GROUP * grp
    sel = (row >= first_row) & (row < first_row + EXPERTS_PER_GROUP)
    le = jnp.where(sel, logits, NEG)
    v1 = jnp.max(le, axis=0, keepdims=True)
    i1 = jnp.min(jnp.where(le == v1, row, float(nr)), axis=0, keepdims=True)
    le2 = jnp.where(row == i1, NEG, le)
    v2 = jnp.max(le2, axis=0, keepdims=True)
    i2 = jnp.min(jnp.where(le2 == v2, row, float(nr)), axis=0, keepdims=True)
    e21 = jnp.exp(v2 - v1)
    g1 = gate_grp / (1.0 + e21)
    g2 = gate_grp * e21 / (1.0 + e21)

    oh1 = row == i1
    oh2 = row == i2
    onehot = jnp.where(oh1 | oh2, 1.0, 0.0).astype(F32)
    ti = lax.broadcasted_iota(jnp.int32, (tm, tm), 0)
    tj = lax.broadcasted_iota(jnp.int32, (tm, tm), 1)
    earlier = jnp.where(ti < tj, 1.0, 0.0).astype(F32)
    tot = _mm(onehot, earlier) + run_s[:, 0:1]
    r1 = jnp.sum(jnp.where(oh1, tot, 0.0), axis=0, keepdims=True)
    r2 = jnp.sum(jnp.where(oh2, tot, 0.0), axis=0, keepdims=True)
    run_s[...] = run_s[...] + jnp.sum(onehot, axis=1, keepdims=True)
    cnt_ref[...] = run_s[...]

    sub_i = lax.broadcasted_iota(jnp.int32, (SUBLANES, tm), 0)
    ri = jnp.where(sub_i == 0, i1 - N_GROUPS, 0.0)
    ri = jnp.where(sub_i == 1, i2 - N_GROUPS, ri)
    ri = jnp.where(sub_i == 2, r1, ri)
    ri = jnp.where(sub_i == 3, r2, ri)
    ri_ref[...] = ri.astype(jnp.int32)

    gates = jnp.where(sub_i == 0, g1, jnp.where(sub_i == 1, g2, 0.0))
    pick = jnp.where(lax.broadcasted_iota(jnp.int32, (SUBLANES, LANES), 0)
                     == lax.broadcasted_iota(jnp.int32, (SUBLANES, LANES), 1), 1.0, 0.0).astype(BF16)
    tn = (((0,), (0,)), ((), ()))
    rg_ref[...] = sum(lax.dot_general(term, pick, tn, preferred_element_type=F32)
                      for term in _split3(gates))


def _outproj(oa, ob, xf, mod, w_o, ln_g, ln_b, w_rg, b_rg, w_re, b_re, seq):
    t, d = xf.shape
    tm = PROJ_TILE
    gw = oa.shape[1]
    wo = _bf(w_o)
    nr = ROUTER_ROWS
    n_log = N_GROUPS + N_EXPERTS
    wr = jnp.concatenate([w_rg, w_re, jnp.zeros((d, LANES - n_log), F32)], axis=1)
    wrh = _bf(wr)
    wrl = _bf(wr - wrh.astype(F32))
    br = jnp.concatenate([b_rg, b_re, jnp.zeros((nr - n_log,), F32)])
    br = jnp.broadcast_to(br[:, None], (nr, LANES))
    spb = seq // tm
    row = lambda i: (i, 0)
    const = lambda i: (0, 0)
    return pl.pallas_call(
        _outproj_kernel,
        out_shape=(jax.ShapeDtypeStruct((t, d), F32), jax.ShapeDtypeStruct((t, d // LANES, LANES), F32),
                   jax.ShapeDtypeStruct((SUBLANES, t), jnp.int32), jax.ShapeDtypeStruct((t, LANES), F32),
                   jax.ShapeDtypeStruct((nr, LANES), F32)),
        grid=(t // tm,),
        in_specs=[pl.BlockSpec((tm, gw), row), pl.BlockSpec((tm, gw), row), pl.BlockSpec((tm, d), row),
                  pl.BlockSpec((1, 6, d), lambda i: (i // spb, 0, 0)),
                  pl.BlockSpec((2 * gw, d), const),
                  pl.BlockSpec((1, d), const), pl.BlockSpec((1, d), const),
                  pl.BlockSpec((d, LANES), const), pl.BlockSpec((d, LANES), const),
                  pl.BlockSpec((nr, LANES), const)],
        out_specs=(pl.BlockSpec((tm, d), row), pl.BlockSpec((tm, d // LANES, LANES), lambda i: (i, 0, 0)),
                   pl.BlockSpec((SUBLANES, tm), lambda i: (0, i)), pl.BlockSpec((tm, LANES), row),
                   pl.BlockSpec((nr, LANES), const)),
        scratch_shapes=[pltpu.VMEM((nr, LANES), F32), pltpu.VMEM((tm, d), F32),
                        pltpu.VMEM((tm, d), BF16), pltpu.VMEM((tm, d), BF16)],
        compiler_params=_params(("arbitrary",)),
        name="outproj",
    )(oa, ob, xf, mod, wo, ln_g.reshape(1, d), ln_b.reshape(1, d), wrh, wrl, br)


def _dispatch_kernel(d1_ref, d2_ref, h_ref, xs_ref, sem):
    tm = h_ref.shape[0]
    i = pl.program_id(0)

    def row_copy(tk, dest):
        return pltpu.make_async_copy(h_ref.at[tk], xs_ref.at[dest], sem)

    def issue(g, carry):
        for u in range(DMA_UNROLL):
            tk = g * DMA_UNROLL + u
            tok = i * tm + tk
            row_copy(tk, d1_ref[tok]).start(priority=0)
            row_copy(tk, d2_ref[tok]).start(priority=1)
        return carry

    lax.fori_loop(0, tm // DMA_UNROLL, issue, 0)

    tile_copy = pltpu.make_async_copy(h_ref, xs_ref.at[pl.ds(0, tm)], sem)
    tile_copy.wait()
    tile_copy.wait()


def _dispatch(h2, d1, d2):
    t, sub, lanes = h2.shape
    tm = DISPATCH_TILE
    return pl.pallas_call(
        _dispatch_kernel,
        out_shape=jax.ShapeDtypeStruct((2 * t, sub, lanes), F32),
        grid_spec=pltpu.PrefetchScalarGridSpec(
            num_scalar_prefetch=2,
            grid=(t // tm,),
            in_specs=[pl.BlockSpec((tm, sub, lanes), lambda i, *_: (i, 0, 0))],
            out_specs=pl.BlockSpec(memory_space=pl.ANY),
            scratch_shapes=[pltpu.SemaphoreType.DMA]),
        compiler_params=_params(("arbitrary",)),
        name="dispatch",
    )(d1, d2, h2)


def _experts_kernel(wb_ref, we_ref, lo_ref, hi_ref, nw_ref, first_ref, ring_ref, next_ref,
                    xs_ref, wg_hbm, wu_hbm, wd_hbm, ys_ref,
                    wg_f, wu_f, wd_f, wg_s, wu_s, wd_s, sem):
    w = pl.program_id(0)

    def fetch(expert, slot):
        return (pltpu.make_async_copy(wg_hbm.at[expert], wg_f.at[slot], sem.at[slot]),
                pltpu.make_async_copy(wu_hbm.at[expert], wu_f.at[slot], sem.at[slot]),
                pltpu.make_async_copy(wd_hbm.at[expert], wd_f.at[slot], sem.at[slot]))

    @pl.when(w == 0)
    def _():
        for cp in fetch(we_ref[0], 0):
            cp.start()

    @pl.when((first_ref[w] == 1) & (w < nw_ref[0]))
    def _():
        slot = ring_ref[w]
        for cp in fetch(we_ref[w], slot):
            cp.wait()

        @pl.when(next_ref[w] >= 0)
        def _():
            for cp in fetch(next_ref[w], 1 - slot):
                cp.start()

        wg_s[...] = _bf(wg_f[slot])
        wu_s[...] = _bf(wu_f[slot])
        wd_s[...] = _bf(wd_f[slot])

    bm, sub, lanes = xs_ref.shape
    half = bm // 2
    lo, hi = lo_ref[w], hi_ref[w]
    live = w < nw_ref[0]

    def mlp(rows):
        x = _bf(xs_ref[rows].reshape(rows.stop - rows.start, sub * lanes))
        hid = (_silu(jnp.dot(x, wg_s[...], preferred_element_type=F32))
               * jnp.dot(x, wu_s[...], preferred_element_type=F32))
        y = jnp.dot(_bf(hid), wd_s[...], preferred_element_type=F32)
        return y.reshape(rows.stop - rows.start, sub, lanes)

    def put(rows, y, first_visit):
        row = lax.broadcasted_iota(jnp.int32, (rows.stop - rows.start, 1, 1), 0) + rows.start
        mine = (row >= lo) & (row < hi)
        ys_ref[rows] = jnp.where(mine, y, 0.0 if first_visit else ys_ref[rows])

    whole, lower, upper = slice(0, bm), slice(0, half), slice(half, bm)

    @pl.when(live & (lo < half) & (hi > half) & (lo == 0))
    def _():
        put(whole, mlp(whole), True)

    @pl.when(live & (lo < half) & (hi > half) & (lo > 0))
    def _():
        put(whole, mlp(whole), False)

    @pl.when(live & (hi <= half) & (lo == 0))
    def _():
        put(lower, mlp(lower), True)
        ys_ref[upper] = jnp.zeros((bm - half, sub, lanes), F32)

    @pl.when(live & (hi <= half) & (lo > 0))
    def _():
        put(lower, mlp(lower), False)

    @pl.when(live & (lo >= half))
    def _():
        put(upper, mlp(upper), False)


def _experts(xs, item_block, item_expert, item_lo, item_hi, n_items, w_gate, w_up, w_down):
    n_slots, sub, lanes = xs.shape
    d = sub * lanes
    ff = w_gate.shape[2]
    bm = EXPERT_BLOCK
    n = item_block.shape[0]
    idx = jnp.arange(n, dtype=jnp.int32)
    first = jnp.concatenate([jnp.ones((1,), jnp.int32),
                             (item_expert[1:] != item_expert[:-1]).astype(jnp.int32)])
    ring = (jnp.cumsum(first) - 1) % 2
    next_first = lax.cummin(jnp.where(first == 1, idx, n), reverse=True)
    next_first = jnp.concatenate([next_first[1:], jnp.full((1,), n, jnp.int32)])
    nxt = jnp.where(next_first < n, item_expert[jnp.minimum(next_first, n - 1)], -1).astype(jnp.int32)
    slot = lambda w, *_: (_[0][w], 0, 0)
    return pl.pallas_call(
        _experts_kernel,
        out_shape=jax.ShapeDtypeStruct((n_slots, sub, lanes), F32),
        grid_spec=pltpu.PrefetchScalarGridSpec(
            num_scalar_prefetch=8,
            grid=(n,),
            in_specs=[pl.BlockSpec((bm, sub, lanes), slot),
                      pl.BlockSpec(memory_space=pl.ANY), pl.BlockSpec(memory_space=pl.ANY),
                      pl.BlockSpec(memory_space=pl.ANY)],
            out_specs=pl.BlockSpec((bm, sub, lanes), slot),
            scratch_shapes=[pltpu.VMEM((2, d, ff), F32), pltpu.VMEM((2, d, ff), F32),
                            pltpu.VMEM((2, ff, d), F32),
                            pltpu.VMEM((d, ff), BF16), pltpu.VMEM((d, ff), BF16),
                            pltpu.VMEM((ff, d), BF16), pltpu.SemaphoreType.DMA((2,))]),
        compiler_params=_params(("arbitrary",)),
        name="experts",
    )(item_block, item_expert, item_lo, item_hi, n_items, first, ring.astype(jnp.int32), nxt,
      xs, w_gate, w_up, w_down)


def _combine_kernel(d1_ref, d2_ref, ys_ref, rg_ref, x1_ref, mod_ref, g_ref, b_ref, o_ref, ya, yb, sem):
    tm = x1_ref.shape[0]
    i = pl.program_id(0)
    n = pl.num_programs(0)

    def row_copy(dest, buf, slot, tk):
        return pltpu.make_async_copy(ys_ref.at[dest], buf.at[slot, tk], sem.at[slot])

    slot = i % 2
    gate2 = mod_ref[0, 5:6, :]

    def issue_group(step, dst_slot, g):
        for u in range(DMA_UNROLL):
            tk = g * DMA_UNROLL + u
            tok = step * tm + tk
            row_copy(d1_ref[tok], ya, dst_slot, tk).start(priority=0)
            row_copy(d2_ref[tok], yb, dst_slot, tk).start(priority=1)

    def gather_tile(step, dst_slot):
        def issue(g, carry):
            issue_group(step, dst_slot, g)
            return carry

        lax.fori_loop(0, tm // DMA_UNROLL, issue, 0)

    @pl.when(i == 0)
    def _():
        gather_tile(0, 0)

    @pl.when(i + 1 < n)
    def _():
        gather_tile(i + 1, 1 - slot)

    pltpu.make_async_copy(ys_ref.at[pl.ds(0, tm)], ya.at[slot], sem.at[slot]).wait()
    pltpu.make_async_copy(ys_ref.at[pl.ds(0, tm)], yb.at[slot], sem.at[slot]).wait()

    rg = rg_ref[...]
    d = x1_ref.shape[1]
    y = rg[:, 0:1] * ya[slot].reshape(tm, d) + rg[:, 1:2] * yb[slot].reshape(tm, d)
    o_ref[...] = _layer_norm(DEEPNORM_ALPHA * x1_ref[...] + (1.0 + gate2) * y, g_ref[...], b_ref[...])


def _combine(ys, d1, d2, rg, x1, mod, ln_g, ln_b, seq):
    t, d = x1.shape
    tm = ROW_TILE
    spb = seq // tm
    row = lambda i, *_: (i, 0)
    const = lambda i, *_: (0, 0)
    buf = pltpu.VMEM((2, tm) + ys.shape[1:], F32)
    return pl.pallas_call(
        _combine_kernel,
        out_shape=jax.ShapeDtypeStruct((t, d), F32),
        grid_spec=pltpu.PrefetchScalarGridSpec(
            num_scalar_prefetch=2,
            grid=(t // tm,),
            in_specs=[pl.BlockSpec(memory_space=pl.ANY),
                      pl.BlockSpec((tm, LANES), row), pl.BlockSpec((tm, d), row),
                      pl.BlockSpec((1, 6, d), lambda i, *_: (i // spb, 0, 0)),
                      pl.BlockSpec((1, d), const), pl.BlockSpec((1, d), const)],
            out_specs=pl.BlockSpec((tm, d), row),
            scratch_shapes=[buf, buf, pltpu.SemaphoreType.DMA((2,))]),
        compiler_params=_params(("arbitrary",)),
        name="combine",
    )(d1, d2, ys, rg, x1, mod, ln_g.reshape(1, d), ln_b.reshape(1, d))


def _layer(x, c, positions, w_ada, b_ada, w_in, conv_w, a_log, dt_bias, gdn_norm_w, attn_norm_w,
           w_o, ln1_g, ln1_b, w_rg, b_rg, w_re, b_re, w_gate, w_up, w_down, ln2_g, ln2_b):
    bsz, seq, d = x.shape
    t = bsz * seq
    xf = x.reshape(t, d)
    mod = _ada(c, w_ada, b_ada)
    qkv, z, bdc, bdr, qb, kb, vb = _inproj(xf, mod, positions.reshape(t, 1), w_in, conv_w, seq)
    gw = GDN_HEADS * HEAD_DIM
    oa = _gdn(qkv.reshape(bsz, seq, 3 * gw), z.reshape(bsz, seq, gw), bdc.reshape(bsz, seq, LANES), bdr,
              a_log, dt_bias, gdn_norm_w).reshape(t, gw)
    ob = _attn(qb, kb, vb, attn_norm_w, bsz, seq)
    x1, h2, ri, rg, cnt = _outproj(oa, ob, xf, mod, w_o, ln1_g, ln1_b, w_rg, b_rg, w_re, b_re, seq)

    bm = EXPERT_BLOCK
    counts = cnt[N_GROUPS:N_GROUPS + N_EXPERTS, 0].astype(jnp.int32)
    seg_end = jnp.cumsum(counts)
    seg_start = seg_end - counts
    first_blk = seg_start // bm
    n_per = jnp.where(counts > 0, (seg_end - 1) // bm - first_blk + 1, 0)
    item_end = jnp.cumsum(n_per)
    n_items = item_end[-1:]
    max_items = (2 * t) // bm + N_EXPERTS - 1
    w_idx = jnp.minimum(jnp.arange(max_items, dtype=jnp.int32), n_items[0] - 1)
    item_expert = jnp.minimum(jnp.sum(item_end[None, :] <= w_idx[:, None], axis=1), N_EXPERTS - 1).astype(jnp.int32)
    item_block = first_blk[item_expert] + w_idx - (item_end - n_per)[item_expert]
    item_lo = jnp.maximum(seg_start[item_expert] - item_block * bm, 0)
    item_hi = jnp.minimum(seg_end[item_expert] - item_block * bm, bm)
    expert_ids = jnp.arange(N_EXPERTS, dtype=jnp.int32)[:, None]

    def seg_of(e):
        return jnp.sum(jnp.where(e[None, :] == expert_ids, seg_start[:, None], 0), axis=0)

    d1 = seg_of(ri[0]) + ri[2]
    d2 = seg_of(ri[1]) + ri[3]

    xs = _dispatch(h2, d1, d2)
    ys = _experts(xs, item_block, item_expert, item_lo, item_hi, n_items, w_gate, w_up, w_down)
    out = _combine(ys, d1, d2, rg, x1, mod, ln2_g, ln2_b, seq)
    return out.reshape(bsz, seq, d)


def kernel(x, c, positions, w_ada, b_ada, w_in, conv_w, a_log, dt_bias, gdn_norm_w, attn_norm_w, w_o, ln1_g, ln1_b, w_router_group, b_router_group, w_router_expert, b_router_expert, w_gate, w_up, w_down, ln2_g, ln2_b):
    assert w_ada.shape[0] == DEPTH
    return _layer(x, c, positions, w_ada[0], b_ada[0], w_in[0], conv_w[0], a_log[0], dt_bias[0],
                  gdn_norm_w[0], attn_norm_w[0], w_o[0], ln1_g[0], ln1_b[0],
                  w_router_group[0], b_router_group[0], w_router_expert[0], b_router_expert[0],
                  w_gate[0], w_up[0], w_down[0], ln2_g[0], ln2_b[0])
```

```python
import functools
import math

import jax
import jax.numpy as jnp
from jax import lax
from jax.experimental import pallas as pl
from jax.experimental.pallas import tpu as pltpu

F32 = jnp.float32
BF16 = jnp.bfloat16
LOG2E = math.log2(math.e)

GDN_HEADS = 4
ATT_HEADS = 4
HEAD_DIM = 128
CONV_WIDTH = 4
DILATED_PATTERNS = ((128, 1), (512, 4), (2048, 16))
ROPE_THETA = 500000.0
ROPE_DIMS = HEAD_DIM // 4
N_GROUPS = 4
EXPERTS_PER_GROUP = 8
N_EXPERTS = N_GROUPS * EXPERTS_PER_GROUP
DEPTH = 1
DEEPNORM_ALPHA = (2.0 * DEPTH) ** 0.25
LN_EPS = 1e-5
RMS_EPS = 1e-6

LANES = 128
SUBLANES = 8
ROUTER_ROWS = -(-(N_GROUPS + N_EXPERTS) // SUBLANES) * SUBLANES
VMEM_LIMIT = 48 * 1024 * 1024

GDN_BLOCK = 128
GDN_TILE = 256
ATT_BLOCK = 128
ATT_TILE = 2048
ATT_SUB = DILATED_PATTERNS[1][1]
ATT_WIDE = DILATED_PATTERNS[2][1]
assert DILATED_PATTERNS[0][1] == 1 and ATT_WIDE == ATT_SUB * ATT_SUB
assert all(w // d == ATT_BLOCK for w, d in DILATED_PATTERNS) and ATT_TILE == ATT_BLOCK * ATT_WIDE
ADA_COLS = 1024
PROJ_TILE = 512
ROW_TILE = 256
DISPATCH_TILE = 4096
EXPERT_BLOCK = 256
OUTPROJ_PARTS = 2
DMA_UNROLL = 8
NEG = -1e30


def _bf(x):
    return x.astype(BF16)


def _mm(a, b):
    return jnp.dot(_bf(a), _bf(b), preferred_element_type=F32)


def _mm_nt(a, b):
    return lax.dot_general(_bf(a), _bf(b), (((1,), (1,)), ((), ())), preferred_element_type=F32)


def _mm_tn(a, b):
    return lax.dot_general(_bf(a), _bf(b), (((0,), (0,)), ((), ())), preferred_element_type=F32)


def _split3(x):
    x1 = _bf(x)
    r1 = x - x1.astype(F32)
    x2 = _bf(r1)
    return x1, x2, _bf(r1 - x2.astype(F32))


def _sigmoid(x):
    return 1.0 / (1.0 + jnp.exp(-x))


def _silu(x):
    return x * _sigmoid(x)


def _softplus(x):
    return jnp.maximum(x, 0.0) + jnp.log(1.0 + jnp.exp(-jnp.abs(x)))


def _params(sem):
    return pltpu.CompilerParams(dimension_semantics=sem, vmem_limit_bytes=VMEM_LIMIT)


def _ada_kernel(ct_ref, w_ref, b_ref, o_ref, *, bsz):
    sc = _silu(ct_ref[...])
    w = w_ref[...]
    rows = [jnp.sum(w * sc[:, b:b + 1], axis=0, keepdims=True) for b in range(bsz)]
    rows.append(jnp.zeros((o_ref.shape[0] - bsz, w.shape[1]), F32))
    o_ref[...] = jnp.concatenate(rows, axis=0) + b_ref[...]


def _ada(c, w_ada, b_ada):
    bsz, d = c.shape
    n = w_ada.shape[1]
    tn = ADA_COLS
    assert bsz <= SUBLANES
    ct = jnp.zeros((d, LANES), F32).at[:, :bsz].set(c.T)
    out = pl.pallas_call(
        functools.partial(_ada_kernel, bsz=bsz),
        out_shape=jax.ShapeDtypeStruct((SUBLANES, n), F32),
        grid=(n // tn,),
        in_specs=[pl.BlockSpec((d, LANES), lambda j: (0, 0)),
                  pl.BlockSpec((d, tn), lambda j: (0, j)),
                  pl.BlockSpec((1, tn), lambda j: (0, j))],
        out_specs=pl.BlockSpec((SUBLANES, tn), lambda j: (0, j)),
        compiler_params=_params(("parallel",)),
        name="ada",
    )(ct, w_ada, b_ada.reshape(1, n))
    return out[:bsz].reshape(bsz, 6, d)


def _inproj_kernel(x_ref, mod_ref, pos_ref, invf_ref, convw_ref, win_ref,
                   qkv_ref, z_ref, bdc_ref, bdr_ref, qb_ref, kb_ref, vb_ref, cbuf, w_ref, *, steps_per_seq):
    tm = x_ref.shape[0]
    halo = SUBLANES
    n_slabs = cbuf.shape[0]
    gw, aw = z_ref.shape[1], qb_ref.shape[1]
    o_z = qkv_ref.shape[1]
    o_bd = o_z + gw
    o_q = o_bd + LANES
    n_gate = win_ref.shape[0] - o_bd - 3 * aw

    @pl.when(pl.program_id(0) == 0)
    def _():
        def unpack(src_row, dst_col):
            w_ref[:, dst_col:dst_col + LANES] = _bf(win_ref[src_row:src_row + LANES, :].T)

        for c in range(o_bd // LANES):
            unpack(c * LANES, c * LANES)
        gate_rows = jnp.concatenate([win_ref[o_bd:o_bd + n_gate, :],
                                     jnp.zeros((LANES - n_gate, win_ref.shape[1]), F32)], axis=0)
        w_ref[:, o_bd:o_q] = _bf(gate_rows.T)
        for c in range(3 * aw // LANES):
            unpack(o_bd + n_gate + c * LANES, o_q + c * LANES)

    w_z, w_bd = w_ref.at[:, o_z:o_bd], w_ref.at[:, o_bd:o_q]
    w_q, w_k, w_v = (w_ref.at[:, o_q + i * aw:o_q + (i + 1) * aw] for i in range(3))

    @pl.when(pl.program_id(0) % steps_per_seq == 0)
    def _():
        cbuf[:, 0:halo, :] = jnp.zeros((n_slabs, halo, LANES), F32)

    half = ROPE_DIMS // 2
    groups = LANES // half
    lane = lax.broadcasted_iota(jnp.int32, (1, LANES), 1)
    first = lane < half
    rot = lane < ROPE_DIMS
    ang = pos_ref[0].astype(F32) * invf_ref[...]
    cos_c = jnp.cos(ang)
    sin_c = jnp.sin(ang)
    cos_parts, sin_parts = [], []
    for j in range(groups):
        lo_sh = (LANES - half * j) % LANES
        hi_sh = (LANES - half * j + half) % LANES
        c_lo = pltpu.roll(cos_c, lo_sh, 1) if lo_sh else cos_c
        c_hi = pltpu.roll(cos_c, hi_sh, 1) if hi_sh else cos_c
        s_lo = pltpu.roll(sin_c, lo_sh, 1) if lo_sh else sin_c
        s_hi = pltpu.roll(sin_c, hi_sh, 1) if hi_sh else sin_c
        cos_parts.append(jnp.where(first, c_lo, jnp.where(rot, c_hi, 1.0)))
        sin_parts.append(jnp.where(first, -s_lo, jnp.where(rot, s_hi, 0.0)))
    cosv = jnp.concatenate(cos_parts, axis=0)
    sin_signed = jnp.concatenate(sin_parts, axis=0)

    def rope(y):
        outs = []
        for hh in range(ATT_HEADS):
            yh = y[:, hh * HEAD_DIM:(hh + 1) * HEAD_DIM]
            partner = jnp.where(first, pltpu.roll(yh, LANES - half, 1), pltpu.roll(yh, half, 1))
            outs.append(yh * cosv + partner * sin_signed)
        return jnp.concatenate(outs, axis=1)

    def conv_slice(s):
        sl = slice(s * HEAD_DIM, (s + 1) * HEAD_DIM)
        off = halo - (CONV_WIDTH - 1)
        acc = convw_ref[0:1, sl] * cbuf[s, off:off + tm, :]
        for j in range(1, CONV_WIDTH):
            acc = acc + convw_ref[j:j + 1, sl] * cbuf[s, off + j:off + j + tm, :]
        cbuf[s, 0:halo, :] = cbuf[s, tm:tm + halo, :]
        y = _silu(acc)
        if s < 2 * GDN_HEADS:
            y = y * lax.rsqrt(jnp.sum(y * y, axis=-1, keepdims=True) + RMS_EPS)
        if s < GDN_HEADS:
            y = y * (HEAD_DIM ** -0.5)
        qkv_ref[:, sl] = y

    shift = mod_ref[0, 0:1, :]
    scale = mod_ref[0, 1:2, :]
    h = _bf(x_ref[...] * (1.0 + scale) + shift)
    chunk = 2 * HEAD_DIM
    n_chunks = n_slabs // 2

    def project_chunk(c):
        pre = jnp.dot(h, w_ref[:, c * chunk:(c + 1) * chunk], preferred_element_type=F32)
        cbuf[2 * c, halo:halo + tm, :] = pre[:, :HEAD_DIM]
        cbuf[2 * c + 1, halo:halo + tm, :] = pre[:, HEAD_DIM:]

    def conv_chunk(c):
        conv_slice(2 * c)
        conv_slice(2 * c + 1)

    project_chunk(0)
    for c in range(1, n_chunks):
        project_chunk(c)
        conv_chunk(c - 1)
    qb = jnp.dot(h, w_q[...], preferred_element_type=F32)
    conv_chunk(n_chunks - 1)
    kb = jnp.dot(h, w_k[...], preferred_element_type=F32)
    qb_ref[...] = rope(qb) * (HEAD_DIM ** -0.5 * LOG2E)
    vb_ref[...] = jnp.dot(h, w_v[...], preferred_element_type=F32)
    kb_ref[...] = rope(kb)
    z_ref[...] = jnp.dot(h, w_z[...], preferred_element_type=F32)
    bdc = jnp.dot(h, w_bd[...], preferred_element_type=F32)
    bdc_ref[...] = bdc
    bdr_ref[0] = bdc.T[:SUBLANES]


def _inproj(xf, mod, pos, w_in, conv_w, seq):
    t, d = xf.shape
    tm = PROJ_TILE
    gw = GDN_HEADS * HEAD_DIM
    aw = ATT_HEADS * HEAD_DIM
    o0 = 3 * gw
    o1 = o0 + gw
    o2 = o1 + 2 * GDN_HEADS
    packed_cols = o1 + LANES + 3 * aw
    half = ROPE_DIMS // 2
    groups = LANES // half
    inv_freq = ROPE_THETA ** (-jnp.arange(half, dtype=F32) * 2.0 / ROPE_DIMS)
    invf = jnp.tile(inv_freq, groups).reshape(1, LANES)
    pos = jnp.repeat(pos.reshape(t // tm, groups, tm // groups).transpose(0, 2, 1), half, axis=2)
    spb = seq // tm
    row = lambda i: (i, 0)
    const = lambda i: (0, 0)
    return pl.pallas_call(
        functools.partial(_inproj_kernel, steps_per_seq=spb),
        out_shape=(jax.ShapeDtypeStruct((t, o0), F32), jax.ShapeDtypeStruct((t, gw), F32),
                   jax.ShapeDtypeStruct((t, LANES), F32), jax.ShapeDtypeStruct((t // seq, SUBLANES, seq), F32),
                   jax.ShapeDtypeStruct((t, aw), F32), jax.ShapeDtypeStruct((t, aw), F32),
                   jax.ShapeDtypeStruct((t, aw), F32)),
        grid=(t // tm,),
        in_specs=[pl.BlockSpec((tm, d), row),
                  pl.BlockSpec((1, 6, d), lambda i: (i // spb, 0, 0)),
                  pl.BlockSpec((1, tm // groups, LANES), lambda i: (i, 0, 0)),
                  pl.BlockSpec((1, LANES), const), pl.BlockSpec((CONV_WIDTH, o0), const),
                  pl.BlockSpec(w_in.shape[::-1], const, pipeline_mode=pl.Buffered(1))],
        out_specs=(pl.BlockSpec((tm, o0), row), pl.BlockSpec((tm, gw), row),
                   pl.BlockSpec((tm, LANES), row),
                   pl.BlockSpec((1, SUBLANES, tm), lambda i: (i // spb, 0, i % spb)),
                   pl.BlockSpec((tm, aw), row), pl.BlockSpec((tm, aw), row),
                   pl.BlockSpec((tm, aw), row)),
        scratch_shapes=[pltpu.VMEM((o0 // HEAD_DIM, tm + 2 * SUBLANES, HEAD_DIM), F32),
                        pltpu.VMEM((d, packed_cols), BF16)],
        compiler_params=_params(("arbitrary",)),
        name="inproj",
    )(xf, mod, pos, invf, conv_w, w_in.T)


def _gdn_kernel(qkv_ref, z_ref, bdc_ref, bdr_ref, gpc_ref, gpr_ref, nw_ref, o_ref,
                u_s, w_s, qd_s, kd_s, qk_s, a_s, x_s, y_s, st_ref):
    nbat, ts = qkv_ref.shape[0], qkv_ref.shape[1]
    nb = ts // GDN_BLOCK
    gw = GDN_HEADS * HEAD_DIM

    @pl.when(pl.program_id(0) == 0)
    def _():
        st_ref[...] = jnp.zeros(st_ref.shape, F32)

    blk_n = GDN_BLOCK
    hd = HEAD_DIM
    ti = lax.broadcasted_iota(jnp.int32, (ts, ts), 0)
    tj = lax.broadcasted_iota(jnp.int32, (ts, ts), 1)
    same = (ti // blk_n) == (tj // blk_n)
    m_low = jnp.where(same & (tj <= ti), 1.0, 0.0).astype(BF16)
    m_up = jnp.where(same & (ti <= tj), 1.0, 0.0).astype(BF16)

    split3 = _split3

    beta_c = [_sigmoid(bdc_ref[bb]) for bb in range(nbat)]
    g_c = jnp.concatenate([-jnp.exp(gpc_ref[0:1, :]) * LOG2E * _softplus(bdc_ref[bb] + gpc_ref[1:2, :])
                           for bb in range(nbat)], axis=1)
    g_r = jnp.concatenate([-jnp.exp(gpr_ref[:, 0:1]) * LOG2E * _softplus(bdr_ref[bb] + gpr_ref[:, 1:2])
                           for bb in range(nbat)], axis=0)
    gc_c = sum(jnp.dot(m_low, part, preferred_element_type=F32) for part in split3(g_c))
    gc_r = sum(jnp.dot(part, m_up, preferred_element_type=F32) for part in split3(g_r))

    def cat2(m):
        return jnp.concatenate([m, m], axis=1)

    ii = lax.broadcasted_iota(jnp.int32, (blk_n, blk_n), 0)
    jj = lax.broadcasted_iota(jnp.int32, (blk_n, blk_n), 1)
    lower = cat2(jj <= ii)
    strict = cat2(jj < ii)
    eye = cat2(jnp.where(ii == jj, 1.0, 0.0).astype(F32))
    levels = []
    b = 1
    while b < blk_n:
        levels.append(cat2(((ii // b) == (jj // b) + 1) & (((jj // b) % 2) == 0)))
        b *= 2

    def block_diag(rp):
        n, m = rp.shape[0], rp.shape[1] // 2
        z = jnp.zeros((n, m), rp.dtype)
        return jnp.concatenate([jnp.concatenate([rp[:, :m], z], axis=1),
                                jnp.concatenate([z, rp[:, m:]], axis=1)], axis=0)

    def mm2(lp, rp):
        return jnp.dot(_bf(lp), block_diag(_bf(rp)), preferred_element_type=F32)

    def mm2_nt(lp, rp):
        return lax.dot_general(_bf(lp), block_diag(_bf(rp)), (((1,), (1,)), ((), ())),
                               preferred_element_type=F32)

    chains = [(bb, j, pp) for bb in range(nbat) for j in range(nb) for pp in range(GDN_HEADS // 2)]

    def tile_of(bb, j, pp):
        return bb, slice(j * blk_n, (j + 1) * blk_n), slice(2 * pp * hd, 2 * (pp + 1) * hd)

    def col_pair(arr, rows, col):
        return jnp.concatenate([jnp.broadcast_to(arr[rows, col:col + 1], (blk_n, hd)),
                                jnp.broadcast_to(arr[rows, col + 1:col + 2], (blk_n, hd))], axis=1)

    for c, (bb, j, pp) in enumerate(chains):
        blk = tile_of(bb, j, pp)
        rows, cols = blk[1], blk[2]
        last = slice((j + 1) * blk_n - 1, (j + 1) * blk_n)
        gcol = bb * LANES + GDN_HEADS + 2 * pp
        grow = bb * SUBLANES + GDN_HEADS + 2 * pp
        q = qkv_ref[bb, rows, cols]
        k = qkv_ref[bb, rows, slice(gw + cols.start, gw + cols.stop)]
        v = qkv_ref[bb, rows, slice(2 * gw + cols.start, 2 * gw + cols.stop)]
        beta = col_pair(beta_c[bb], rows, 2 * pp)
        gcc = col_pair(gc_c, rows, gcol)
        gtc = col_pair(gc_c, last, gcol)
        gcr = jnp.concatenate([jnp.broadcast_to(gc_r[grow:grow + 1, rows], (blk_n, hd)),
                               jnp.broadcast_to(gc_r[grow + 1:grow + 2, rows], (blk_n, hd))], axis=1)
        kb = k * beta
        eg = jnp.exp2(gcc)
        dm = jnp.where(lower, jnp.exp2(gcc - gcr), 0.0)
        a = jnp.where(strict, mm2_nt(kb, k) * dm, 0.0)
        a_s[c] = a
        x_s[c] = eye - jnp.where(levels[0], a, 0.0)
        u_s[blk] = v * beta
        w_s[blk] = kb * eg
        qd_s[blk] = q * eg
        kd_s[blk] = k * jnp.exp2(gtc - gcc)
        qk_s[blk] = mm2_nt(q, k) * dm

    for lm in levels[1:]:
        for c in range(len(chains)):
            y_s[c] = mm2(x_s[c], jnp.where(lm, a_s[c], 0.0))
        for c in range(len(chains)):
            xc = x_s[c]
            x_s[c] = xc - mm2(y_s[c], xc)

    for c, (bb, j, pp) in enumerate(chains):
        blk = tile_of(bb, j, pp)
        u, w = _bf(u_s[blk]), _bf(w_s[blk])
        z = jnp.zeros((blk_n, 2 * hd), BF16)
        rhs = jnp.concatenate([jnp.concatenate([u[:, :hd], w[:, :hd], z], axis=1),
                               jnp.concatenate([z, u[:, hd:], w[:, hd:]], axis=1)], axis=0)
        sol = jnp.dot(_bf(x_s[c]), rhs, preferred_element_type=F32)
        u_s[blk] = jnp.concatenate([sol[:, 0:hd], sol[:, 2 * hd:3 * hd]], axis=1)
        w_s[blk] = jnp.concatenate([sol[:, hd:2 * hd], sol[:, 3 * hd:]], axis=1)

    for j in range(nb):
        for bb in range(nbat):
            for pp in range(GDN_HEADS // 2):
                blk = tile_of(bb, j, pp)
                si = bb * (GDN_HEADS // 2) + pp
                last = slice((j + 1) * blk_n - 1, (j + 1) * blk_n)
                gcol = bb * LANES + GDN_HEADS + 2 * pp
                state = st_ref[si]
                proj = mm2(jnp.concatenate([w_s[blk], qd_s[blk]], axis=0), state)
                v_new = u_s[blk] - proj[:blk_n]
                o = proj[blk_n:] + mm2(qk_s[blk], v_new)
                g_last = jnp.exp2(jnp.concatenate(
                    [jnp.broadcast_to(gc_c[last, gcol:gcol + 1], (1, hd)),
                     jnp.broadcast_to(gc_c[last, gcol + 1:gcol + 2], (1, hd))], axis=1))
                kd = kd_s[blk]
                upd = jnp.concatenate([_mm_tn(kd[:, :hd], v_new[:, :hd]),
                                       _mm_tn(kd[:, hd:], v_new[:, hd:])], axis=1)
                st_ref[si] = state * g_last + upd
                zz = z_ref[blk]
                halves = []
                for hf in range(2):
                    oh = o[:, hf * hd:(hf + 1) * hd]
                    halves.append(oh * lax.rsqrt(jnp.mean(oh * oh, axis=-1, keepdims=True) + RMS_EPS)
                                  * nw_ref[...])
                o_ref[blk] = (jnp.concatenate(halves, axis=1) * _silu(zz)).astype(o_ref.dtype)


def _gdn(qkv, z, bdc, bdr, a_log, dt_bias, norm_w):
    bsz, seq, _ = qkv.shape
    ts = GDN_TILE
    gw = GDN_HEADS * HEAD_DIM
    zeros4 = jnp.zeros((GDN_HEADS,), F32)
    al = jnp.concatenate([zeros4, a_log])
    db = jnp.concatenate([zeros4, dt_bias])
    gpc = jnp.zeros((2, LANES), F32).at[0, :2 * GDN_HEADS].set(al).at[1, :2 * GDN_HEADS].set(db)
    gpr = jnp.stack([al, db], axis=1)
    row = lambda i: (0, i, 0)
    const = lambda i: (0, 0)
    n_chains = bsz * (ts // GDN_BLOCK) * (GDN_HEADS // 2)
    tile = pltpu.VMEM((bsz, ts, gw), F32)
    mats = pltpu.VMEM((n_chains, GDN_BLOCK, 2 * GDN_BLOCK), F32)
    return pl.pallas_call(
        _gdn_kernel,
        out_shape=jax.ShapeDtypeStruct((bsz, seq, gw), BF16),
        grid=(seq // ts,),
        in_specs=[pl.BlockSpec((bsz, ts, 3 * gw), row), pl.BlockSpec((bsz, ts, gw), row),
                  pl.BlockSpec((bsz, ts, LANES), row),
                  pl.BlockSpec((bsz, SUBLANES, ts), lambda i: (0, 0, i)),
                  pl.BlockSpec((2, LANES), const), pl.BlockSpec((2 * GDN_HEADS, 2), const),
                  pl.BlockSpec((1, HEAD_DIM), const)],
        out_specs=pl.BlockSpec((bsz, ts, gw), row),
        scratch_shapes=[tile, tile, tile, tile, tile,
                        mats, mats, mats,
                        pltpu.VMEM((bsz * GDN_HEADS // 2, HEAD_DIM, 2 * HEAD_DIM), F32)],
        compiler_params=_params(("arbitrary",)),
        name="gdn",
    )(qkv, z, bdc, bdr, gpc, gpr, norm_w.reshape(1, HEAD_DIM))


def _attn_kernel(q_ref, k_ref, v_ref, nw_ref, o_ref,
                 q4, k4, v4, ktail, vtail, m_s, l_s, acc_s, tmp_s, nat_s):
    tq = q_ref.shape[0]
    blk = ATT_BLOCK
    sub = ATT_SUB
    nq = tq // sub
    t = pl.program_id(2)
    slot = t % 2
    other = 1 - slot
    qi = lax.broadcasted_iota(jnp.int32, (blk, 2 * blk), 0)
    kj = lax.broadcasted_iota(jnp.int32, (blk, 2 * blk), 1)
    band = (kj >= qi) & (kj <= qi + blk)
    first_lo = jnp.where(t > 0, 0, blk)
    band_first = band & (kj >= first_lo)

    @pl.when(t == 0)
    def _():
        k4[...] = jnp.zeros(k4.shape, F32)
        v4[...] = jnp.zeros(v4.shape, F32)
        ktail[...] = jnp.zeros(ktail.shape, F32)
        vtail[...] = jnp.zeros(vtail.shape, F32)

    for r in range(sub):
        rows = slice(r * nq, (r + 1) * nq)
        src = pl.ds(r, nq, stride=sub)
        q4[rows, :] = q_ref[src, :]
        k4[slot, rows, :] = k_ref[src, :]
        v4[slot, rows, :] = v_ref[src, :]

    def block_stats(q, kcat, vcat, from_prev_tile):
        s = _mm_nt(q, kcat)
        s = jnp.where(band_first if from_prev_tile else band, s, NEG)
        m = jnp.max(s, axis=-1, keepdims=True)
        p = jnp.exp2(s - m)
        ones = jnp.ones((vcat.shape[0], HEAD_DIM), BF16)
        ov = jnp.dot(_bf(p), jnp.concatenate([_bf(vcat), ones], axis=1), preferred_element_type=F32)
        return m, ov[:, HEAD_DIM:], ov[:, :HEAD_DIM]

    def merge(dst, m, l, o):
        m_old = m_s[dst, :]
        m_new = jnp.maximum(m_old, m)
        w_old = jnp.exp2(m_old - m_new)
        w_cur = jnp.exp2(m - m_new)
        m_s[dst, :] = m_new
        l_s[dst, :] = w_old * l_s[dst, :] + w_cur * l
        acc_s[dst, :] = w_old * acc_s[dst, :] + w_cur * o

    for jb in range(tq // blk):
        cur = slice(jb * blk, (jb + 1) * blk)
        if jb > 0:
            kcat = k_ref[(jb - 1) * blk:(jb + 1) * blk, :]
            vcat = v_ref[(jb - 1) * blk:(jb + 1) * blk, :]
        else:
            kcat = jnp.concatenate([ktail[...], k_ref[cur, :]], axis=0)
            vcat = jnp.concatenate([vtail[...], v_ref[cur, :]], axis=0)
        m, l, o = block_stats(q_ref[cur, :], kcat, vcat, jb == 0)
        tmp_s[0] = jnp.broadcast_to(m, (blk, HEAD_DIM))
        tmp_s[1] = l
        tmp_s[2] = o
        per = blk // sub
        for r in range(sub):
            dst = slice(r * nq + jb * per, r * nq + (jb + 1) * per)
            src = pl.ds(r, per, stride=sub)
            m_s[dst, :] = tmp_s[0, src, :]
            l_s[dst, :] = tmp_s[1, src, :]
            acc_s[dst, :] = tmp_s[2, src, :]

    for r in range(sub):
        for jb in range(nq // blk):
            base = r * nq + jb * blk
            cur = slice(base, base + blk)
            if jb > 0:
                kcat = k4[slot, base - blk:base + blk, :]
                vcat = v4[slot, base - blk:base + blk, :]
            else:
                last = slice((r + 1) * nq - blk, (r + 1) * nq)
                kcat = jnp.concatenate([k4[other, last, :], k4[slot, cur, :]], axis=0)
                vcat = jnp.concatenate([v4[other, last, :], v4[slot, cur, :]], axis=0)
            m, l, o = block_stats(q4[cur, :], kcat, vcat, jb == 0)
            merge(cur, m, l, o)

    for c in range(ATT_WIDE):
        sl = pl.ds((c % sub) * nq + c // sub, blk, stride=sub)
        kcat = jnp.concatenate([k4[other, sl, :], k4[slot, sl, :]], axis=0)
        vcat = jnp.concatenate([v4[other, sl, :], v4[slot, sl, :]], axis=0)
        m, l, o = block_stats(q4[sl, :], kcat, vcat, True)
        merge(sl, m, l, o)

    out = acc_s[...] / l_s[...]
    out = out * lax.rsqrt(jnp.mean(out * out, axis=-1, keepdims=True) + RMS_EPS) * nw_ref[...]
    for r in range(sub):
        nat_s[pl.ds(r, nq, stride=sub), :] = out[r * nq:(r + 1) * nq]
    o_ref[...] = nat_s[...].astype(o_ref.dtype)
    ktail[...] = k_ref[tq - blk:tq, :]
    vtail[...] = v_ref[tq - blk:tq, :]


def _attn(qb, kb, vb, norm_w, bsz, seq):
    t = qb.shape[0]
    tq = ATT_TILE
    spb = seq // tq
    cur = lambda b, h, i: (b * spb + i, h)
    blk = pl.BlockSpec((tq, HEAD_DIM), cur)
    tile = pltpu.VMEM((tq, HEAD_DIM), F32)
    ring = pltpu.VMEM((2, tq, HEAD_DIM), F32)
    tail = pltpu.VMEM((ATT_BLOCK, HEAD_DIM), F32)
    return pl.pallas_call(
        _attn_kernel,
        out_shape=jax.ShapeDtypeStruct((t, ATT_HEADS * HEAD_DIM), BF16),
        grid=(bsz, ATT_HEADS, spb),
        in_specs=[blk, blk, blk, pl.BlockSpec((1, HEAD_DIM), lambda b, h, i: (0, 0))],
        out_specs=blk,
        scratch_shapes=[tile, ring, ring, tail, tail, tile, tile, tile,
                        pltpu.VMEM((3, ATT_BLOCK, HEAD_DIM), F32), tile],
        compiler_params=_params(("parallel", "parallel", "arbitrary")),
        name="attn",
    )(qb, kb, vb, norm_w.reshape(1, HEAD_DIM))


def _layer_norm(y, g, b):
    mu = jnp.mean(y, axis=-1, keepdims=True)
    yc = y - mu
    var = jnp.mean(yc * yc, axis=-1, keepdims=True)
    return yc * lax.rsqrt(var + LN_EPS) * g + b


def _outproj_kernel(oa_ref, ob_ref, x_ref, mod_ref, wo_ref, g_ref, b_ref,
                    wrh_ref, wrl_ref, br_ref,
                    x1_ref, h2_ref, ri_ref, rg_ref, cnt_ref, run_s, mix_s, hi_s, lo_s):
    @pl.when(pl.program_id(0) == 0)
    def _():
        run_s[...] = jnp.zeros(run_s.shape, F32)

    tm = x_ref.shape[0]
    gate1 = mod_ref[0, 2:3, :]
    shift2 = mod_ref[0, 3:4, :]
    scale2 = mod_ref[0, 4:5, :]
    n_parts = OUTPROJ_PARTS
    part = tm // n_parts
    grp_rows = 2 * SUBLANES

    def project(p):
        rows = slice(p * part, (p + 1) * part)
        mix_s[rows, :] = jnp.dot(jnp.concatenate([oa_ref[rows, :], ob_ref[rows, :]], axis=1), wo_ref[...],
                                 preferred_element_type=F32)

    def normalise(p):
        for r0 in range(p * part, (p + 1) * part, grp_rows):
            rows = slice(r0, r0 + grp_rows)
            x1 = _layer_norm(DEEPNORM_ALPHA * x_ref[rows, :] + (1.0 + gate1) * mix_s[rows, :],
                             g_ref[...], b_ref[...])
            x1_ref[rows, :] = x1
            h2 = x1 * (1.0 + scale2) + shift2
            h2_ref[rows] = h2.reshape(grp_rows, SUBLANES, LANES)
            hi = _bf(h2)
            hi_s[rows, :] = hi
            lo_s[rows, :] = _bf(h2 - hi.astype(F32))

    nt = (((1,), (1,)), ((), ()))

    def route(p):
        rows = slice(p * part, (p + 1) * part)
        hi = hi_s[rows, :]
        by_token = (jnp.dot(hi, wrh_ref[...], preferred_element_type=F32)
                    + jnp.dot(hi, wrl_ref[...], preferred_element_type=F32)
                    + jnp.dot(lo_s[rows, :], wrh_ref[...], preferred_element_type=F32))
        return by_token.T[:ROUTER_ROWS]

    project(0)
    parts = []
    for p in range(n_parts):
        if p + 1 < n_parts:
            project(p + 1)
        normalise(p)
        parts.append(route(p))
    logits = jnp.concatenate(parts, axis=1) + br_ref[:, 0:1]
    nr = logits.shape[0]
    row = lax.broadcasted_iota(jnp.int32, (nr, tm), 0).astype(F32)
    lg = jnp.where(row < N_GROUPS, logits, NEG)
    mg = jnp.max(lg, axis=0, keepdims=True)
    grp = jnp.min(jnp.where(lg == mg, row, float(nr)), axis=0, keepdims=True)
    gate_grp = 1.0 / jnp.sum(jnp.exp(lg - mg), axis=0, keepdims=True)
    first_row = N_GROUPS + EXPERTS_PER_GROUP * grp
    sel = (row >= first_row) & (row < first_row + EXPERTS_PER_GROUP)
    le = jnp.where(sel, logits, NEG)
    v1 = jnp.max(le, axis=0, keepdims=True)
    i1 = jnp.min(jnp.where(le == v1, row, float(nr)), axis=0, keepdims=True)
    le2 = jnp.where(row == i1, NEG, le)
    v2 = jnp.max(le2, axis=0, keepdims=True)
    i2 = jnp.min(jnp.where(le2 == v2, row, float(nr)), axis=0, keepdims=True)
    e21 = jnp.exp(v2 - v1)
    g1 = gate_grp / (1.0 + e21)
    g2 = gate_grp * e21 / (1.0 + e21)

    oh1 = row == i1
    oh2 = row == i2
    onehot = jnp.where(oh1 | oh2, 1.0, 0.0).astype(F32)
    ti = lax.broadcasted_iota(jnp.int32, (tm, tm), 0)
    tj = lax.broadcasted_iota(jnp.int32, (tm, tm), 1)
    earlier = jnp.where(ti < tj, 1.0, 0.0).astype(F32)
    tot = _mm(onehot, earlier) + run_s[:, 0:1]
    r1 = jnp.sum(jnp.where(oh1, tot, 0.0), axis=0, keepdims=True)
    r2 = jnp.sum(jnp.where(oh2, tot, 0.0), axis=0, keepdims=True)
    run_s[...] = run_s[...] + jnp.sum(onehot, axis=1, keepdims=True)
    cnt_ref[...] = run_s[...]

    sub_i = lax.broadcasted_iota(jnp.int32, (SUBLANES, tm), 0)
    ri = jnp.where(sub_i == 0, i1 - N_GROUPS, 0.0)
    ri = jnp.where(sub_i == 1, i2 - N_GROUPS, ri)
    ri = jnp.where(sub_i == 2, r1, ri)
    ri = jnp.where(sub_i == 3, r2, ri)
    ri_ref[...] = ri.astype(jnp.int32)

    gates = jnp.where(sub_i == 0, g1, jnp.where(sub_i == 1, g2, 0.0))
    pick = jnp.where(lax.broadcasted_iota(jnp.int32, (SUBLANES, LANES), 0)
                     == lax.broadcasted_iota(jnp.int32, (SUBLANES, LANES), 1), 1.0, 0.0).astype(BF16)
    tn = (((0,), (0,)), ((), ()))
    rg_ref[...] = sum(lax.dot_general(term, pick, tn, preferred_element_type=F32)
                      for term in _split3(gates))


def _outproj(oa, ob, xf, mod, w_o, ln_g, ln_b, w_rg, b_rg, w_re, b_re, seq):
    t, d = xf.shape
    tm = PROJ_TILE
    gw = oa.shape[1]
    wo = _bf(w_o)
    nr = ROUTER_ROWS
    n_log = N_GROUPS + N_EXPERTS
    wr = jnp.concatenate([w_rg, w_re, jnp.zeros((d, LANES - n_log), F32)], axis=1)
    wrh = _bf(wr)
    wrl = _bf(wr - wrh.astype(F32))
    br = jnp.concatenate([b_rg, b_re, jnp.zeros((nr - n_log,), F32)])
    br = jnp.broadcast_to(br[:, None], (nr, LANES))
    spb = seq // tm
    row = lambda i: (i, 0)
    const = lambda i: (0, 0)
    return pl.pallas_call(
        _outproj_kernel,
        out_shape=(jax.ShapeDtypeStruct((t, d), F32), jax.ShapeDtypeStruct((t, d // LANES, LANES), F32),
                   jax.ShapeDtypeStruct((SUBLANES, t), jnp.int32), jax.ShapeDtypeStruct((t, LANES), F32),
                   jax.ShapeDtypeStruct((nr, LANES), F32)),
        grid=(t // tm,),
        in_specs=[pl.BlockSpec((tm, gw), row), pl.BlockSpec((tm, gw), row), pl.BlockSpec((tm, d), row),
                  pl.BlockSpec((1, 6, d), lambda i: (i // spb, 0, 0)),
                  pl.BlockSpec((2 * gw, d), const),
                  pl.BlockSpec((1, d), const), pl.BlockSpec((1, d), const),
                  pl.BlockSpec((d, LANES), const), pl.BlockSpec((d, LANES), const),
                  pl.BlockSpec((nr, LANES), const)],
        out_specs=(pl.BlockSpec((tm, d), row), pl.BlockSpec((tm, d // LANES, LANES), lambda i: (i, 0, 0)),
                   pl.BlockSpec((SUBLANES, tm), lambda i: (0, i)), pl.BlockSpec((tm, LANES), row),
                   pl.BlockSpec((nr, LANES), const)),
        scratch_shapes=[pltpu.VMEM((nr, LANES), F32), pltpu.VMEM((tm, d), F32),
                        pltpu.VMEM((tm, d), BF16), pltpu.VMEM((tm, d), BF16)],
        compiler_params=_params(("arbitrary",)),
        name="outproj",
    )(oa, ob, xf, mod, wo, ln_g.reshape(1, d), ln_b.reshape(1, d), wrh, wrl, br)


def _dispatch_kernel(d1_ref, d2_ref, h_ref, xs_ref, sem):
    tm = h_ref.shape[0]
    i = pl.program_id(0)

    def row_copy(tk, dest):
        return pltpu.make_async_copy(h_ref.at[tk], xs_ref.at[dest], sem)

    def issue(g, carry):
        for u in range(DMA_UNROLL):
            tk = g * DMA_UNROLL + u
            tok = i * tm + tk
            row_copy(tk, d1_ref[tok]).start(priority=0)
            row_copy(tk, d2_ref[tok]).start(priority=1)
        return carry

    lax.fori_loop(0, tm // DMA_UNROLL, issue, 0)

    tile_copy = pltpu.make_async_copy(h_ref, xs_ref.at[pl.ds(0, tm)], sem)
    tile_copy.wait()
    tile_copy.wait()


def _dispatch(h2, d1, d2):
    t, sub, lanes = h2.shape
    tm = DISPATCH_TILE
    return pl.pallas_call(
        _dispatch_kernel,
        out_shape=jax.ShapeDtypeStruct((2 * t, sub, lanes), F32),
        grid_spec=pltpu.PrefetchScalarGridSpec(
            num_scalar_prefetch=2,
            grid=(t // tm,),
            in_specs=[pl.BlockSpec((tm, sub, lanes), lambda i, *_: (i, 0, 0))],
            out_specs=pl.BlockSpec(memory_space=pl.ANY),
            scratch_shapes=[pltpu.SemaphoreType.DMA]),
        compiler_params=_params(("arbitrary",)),
        name="dispatch",
    )(d1, d2, h2)


def _experts_kernel(wb_ref, we_ref, lo_ref, hi_ref, nw_ref, first_ref, ring_ref, next_ref,
                    xs_ref, wg_hbm, wu_hbm, wd_hbm, ys_ref,
                    wg_f, wu_f, wd_f, wg_s, wu_s, wd_s, sem):
    w = pl.program_id(0)

    def fetch(expert, slot):
        return (pltpu.make_async_copy(wg_hbm.at[expert], wg_f.at[slot], sem.at[slot]),
                pltpu.make_async_copy(wu_hbm.at[expert], wu_f.at[slot], sem.at[slot]),
                pltpu.make_async_copy(wd_hbm.at[expert], wd_f.at[slot], sem.at[slot]))

    @pl.when(w == 0)
    def _():
        for cp in fetch(we_ref[0], 0):
            cp.start()

    @pl.when((first_ref[w] == 1) & (w < nw_ref[0]))
    def _():
        slot = ring_ref[w]
        for cp in fetch(we_ref[w], slot):
            cp.wait()

        @pl.when(next_ref[w] >= 0)
        def _():
            for cp in fetch(next_ref[w], 1 - slot):
                cp.start()

        wg_s[...] = _bf(wg_f[slot])
        wu_s[...] = _bf(wu_f[slot])
        wd_s[...] = _bf(wd_f[slot])

    bm, sub, lanes = xs_ref.shape
    half = bm // 2
    lo, hi = lo_ref[w], hi_ref[w]
    live = w < nw_ref[0]

    def mlp(rows):
        x = _bf(xs_ref[rows].reshape(rows.stop - rows.start, sub * lanes))
        hid = (_silu(jnp.dot(x, wg_s[...], preferred_element_type=F32))
               * jnp.dot(x, wu_s[...], preferred_element_type=F32))
        y = jnp.dot(_bf(hid), wd_s[...], preferred_element_type=F32)
        return y.reshape(rows.stop - rows.start, sub, lanes)

    def put(rows, y, first_visit):
        row = lax.broadcasted_iota(jnp.int32, (rows.stop - rows.start, 1, 1), 0) + rows.start
        mine = (row >= lo) & (row < hi)
        ys_ref[rows] = jnp.where(mine, y, 0.0 if first_visit else ys_ref[rows])

    whole, lower, upper = slice(0, bm), slice(0, half), slice(half, bm)

    @pl.when(live & (lo < half) & (hi > half) & (lo == 0))
    def _():
        put(whole, mlp(whole), True)

    @pl.when(live & (lo < half) & (hi > half) & (lo > 0))
    def _():
        put(whole, mlp(whole), False)

    @pl.when(live & (hi <= half) & (lo == 0))
    def _():
        put(lower, mlp(lower), True)
        ys_ref[upper] = jnp.zeros((bm - half, sub, lanes), F32)

    @pl.when(live & (hi <= half) & (lo > 0))
    def _():
        put(lower, mlp(lower), False)

    @pl.when(live & (lo >= half))
    def _():
        put(upper, mlp(upper), False)


def _experts(xs, item_block, item_expert, item_lo, item_hi, n_items, w_gate, w_up, w_down):
    n_slots, sub, lanes = xs.shape
    d = sub * lanes
    ff = w_gate.shape[2]
    bm = EXPERT_BLOCK
    n = item_block.shape[0]
    idx = jnp.arange(n, dtype=jnp.int32)
    first = jnp.concatenate([jnp.ones((1,), jnp.int32),
                             (item_expert[1:] != item_expert[:-1]).astype(jnp.int32)])
    ring = (jnp.cumsum(first) - 1) % 2
    next_first = lax.cummin(jnp.where(first == 1, idx, n), reverse=True)
    next_first = jnp.concatenate([next_first[1:], jnp.full((1,), n, jnp.int32)])
    nxt = jnp.where(next_first < n, item_expert[jnp.minimum(next_first, n - 1)], -1).astype(jnp.int32)
    slot = lambda w, *_: (_[0][w], 0, 0)
    return pl.pallas_call(
        _experts_kernel,
        out_shape=jax.ShapeDtypeStruct((n_slots, sub, lanes), F32),
        grid_spec=pltpu.PrefetchScalarGridSpec(
            num_scalar_prefetch=8,
            grid=(n,),
            in_specs=[pl.BlockSpec((bm, sub, lanes), slot),
                      pl.BlockSpec(memory_space=pl.ANY), pl.BlockSpec(memory_space=pl.ANY),
                      pl.BlockSpec(memory_space=pl.ANY)],
            out_specs=pl.BlockSpec((bm, sub, lanes), slot),
            scratch_shapes=[pltpu.VMEM((2, d, ff), F32), pltpu.VMEM((2, d, ff), F32),
                            pltpu.VMEM((2, ff, d), F32),
                            pltpu.VMEM((d, ff), BF16), pltpu.VMEM((d, ff), BF16),
                            pltpu.VMEM((ff, d), BF16), pltpu.SemaphoreType.DMA((2,))]),
        compiler_params=_params(("arbitrary",)),
        name="experts",
    )(item_block, item_expert, item_lo, item_hi, n_items, first, ring.astype(jnp.int32), nxt,
      xs, w_gate, w_up, w_down)


def _combine_kernel(d1_ref, d2_ref, ys_ref, rg_ref, x1_ref, mod_ref, g_ref, b_ref, o_ref, ya, yb, sem):
    tm = x1_ref.shape[0]
    i = pl.program_id(0)
    n = pl.num_programs(0)

    def row_copy(dest, buf, slot, tk):
        return pltpu.make_async_copy(ys_ref.at[dest], buf.at[slot, tk], sem.at[slot])

    slot = i % 2
    gate2 = mod_ref[0, 5:6, :]

    def issue_group(step, dst_slot, g):
        for u in range(DMA_UNROLL):
            tk = g * DMA_UNROLL + u
            tok = step * tm + tk
            row_copy(d1_ref[tok], ya, dst_slot, tk).start(priority=0)
            row_copy(d2_ref[tok], yb, dst_slot, tk).start(priority=1)

    def gather_tile(step, dst_slot):
        def issue(g, carry):
            issue_group(step, dst_slot, g)
            return carry

        lax.fori_loop(0, tm // DMA_UNROLL, issue, 0)

    @pl.when(i == 0)
    def _():
        gather_tile(0, 0)

    @pl.when(i + 1 < n)
    def _():
        gather_tile(i + 1, 1 - slot)

    pltpu.make_async_copy(ys_ref.at[pl.ds(0, tm)], ya.at[slot], sem.at[slot]).wait()
    pltpu.make_async_copy(ys_ref.at[pl.ds(0, tm)], yb.at[slot], sem.at[slot]).wait()

    rg = rg_ref[...]
    d = x1_ref.shape[1]
    y = rg[:, 0:1] * ya[slot].reshape(tm, d) + rg[:, 1:2] * yb[slot].reshape(tm, d)
    o_ref[...] = _layer_norm(DEEPNORM_ALPHA * x1_ref[...] + (1.0 + gate2) * y, g_ref[...], b_ref[...])


def _combine(ys, d1, d2, rg, x1, mod, ln_g, ln_b, seq):
    t, d = x1.shape
    tm = ROW_TILE
    spb = seq // tm
    row = lambda i, *_: (i, 0)
    const = lambda i, *_: (0, 0)
    buf = pltpu.VMEM((2, tm) + ys.shape[1:], F32)
    return pl.pallas_call(
        _combine_kernel,
        out_shape=jax.ShapeDtypeStruct((t, d), F32),
        grid_spec=pltpu.PrefetchScalarGridSpec(
            num_scalar_prefetch=2,
            grid=(t // tm,),
            in_specs=[pl.BlockSpec(memory_space=pl.ANY),
                      pl.BlockSpec((tm, LANES), row), pl.BlockSpec((tm, d), row),
                      pl.BlockSpec((1, 6, d), lambda i, *_: (i // spb, 0, 0)),
                      pl.BlockSpec((1, d), const), pl.BlockSpec((1, d), const)],
            out_specs=pl.BlockSpec((tm, d), row),
            scratch_shapes=[buf, buf, pltpu.SemaphoreType.DMA((2,))]),
        compiler_params=_params(("arbitrary",)),
        name="combine",
    )(d1, d2, ys, rg, x1, mod, ln_g.reshape(1, d), ln_b.reshape(1, d))


def _layer(x, c, positions, w_ada, b_ada, w_in, conv_w, a_log, dt_bias, gdn_norm_w, attn_norm_w,
           w_o, ln1_g, ln1_b, w_rg, b_rg, w_re, b_re, w_gate, w_up, w_down, ln2_g, ln2_b):
    bsz, seq, d = x.shape
    t = bsz * seq
    xf = x.reshape(t, d)
    mod = _ada(c, w_ada, b_ada)
    qkv, z, bdc, bdr, qb, kb, vb = _inproj(xf, mod, positions.reshape(t, 1), w_in, conv_w, seq)
    gw = GDN_HEADS * HEAD_DIM
    oa = _gdn(qkv.reshape(bsz, seq, 3 * gw), z.reshape(bsz, seq, gw), bdc.reshape(bsz, seq, LANES), bdr,
              a_log, dt_bias, gdn_norm_w).reshape(t, gw)
    ob = _attn(qb, kb, vb, attn_norm_w, bsz, seq)
    x1, h2, ri, rg, cnt = _outproj(oa, ob, xf, mod, w_o, ln1_g, ln1_b, w_rg, b_rg, w_re, b_re, seq)

    bm = EXPERT_BLOCK
    counts = cnt[N_GROUPS:N_GROUPS + N_EXPERTS, 0].astype(jnp.int32)
    seg_end = jnp.cumsum(counts)
    seg_start = seg_end - counts
    first_blk = seg_start // bm
    n_per = jnp.where(counts > 0, (seg_end - 1) // bm - first_blk + 1, 0)
    item_end = jnp.cumsum(n_per)
    n_items = item_end[-1:]
    max_items = (2 * t) // bm + N_EXPERTS - 1
    w_idx = jnp.minimum(jnp.arange(max_items, dtype=jnp.int32), n_items[0] - 1)
    item_expert = jnp.minimum(jnp.sum(item_end[None, :] <= w_idx[:, None], axis=1), N_EXPERTS - 1).astype(jnp.int32)
    item_block = first_blk[item_expert] + w_idx - (item_end - n_per)[item_expert]
    item_lo = jnp.maximum(seg_start[item_expert] - item_block * bm, 0)
    item_hi = jnp.minimum(seg_end[item_expert] - item_block * bm, bm)
    expert_ids = jnp.arange(N_EXPERTS, dtype=jnp.int32)[:, None]

    def seg_of(e):
        return jnp.sum(jnp.where(e[None, :] == expert_ids, seg_start[:, None], 0), axis=0)

    d1 = seg_of(ri[0]) + ri[2]
    d2 = seg_of(ri[1]) + ri[3]

    xs = _dispatch(h2, d1, d2)
    ys = _experts(xs, item_block, item_expert, item_lo, item_hi, n_items, w_gate, w_up, w_down)
    out = _combine(ys, d1, d2, rg, x1, mod, ln2_g, ln2_b, seq)
    return out.reshape(bsz, seq, d)


def kernel(x, c, positions, w_ada, b_ada, w_in, conv_w, a_log, dt_bias, gdn_norm_w, attn_norm_w, w_o, ln1_g, ln1_b, w_router_group, b_router_group, w_router_expert, b_router_expert, w_gate, w_up, w_down, ln2_g, ln2_b):
    assert w_ada.shape[0] == DEPTH
    return _layer(x, c, positions, w_ada[0], b_ada[0], w_in[0], conv_w[0], a_log[0], dt_bias[0],
                  gdn_norm_w[0], attn_norm_w[0], w_o[0], ln1_g[0], ln1_b[0],
                  w_router_group[0], b_router_group[0], w_router_expert[0], b_router_expert[0],
                  w_gate[0], w_up[0], w_down[0], ln2_g[0], ln2_b[0])
```

```python
import functools
import math

import jax
import jax.numpy as jnp
from jax import lax
from jax.experimental import pallas as pl
from jax.experimental.pallas import tpu as pltpu

F32 = jnp.float32
BF16 = jnp.bfloat16
LOG2E = math.log2(math.e)

GDN_HEADS = 4
ATT_HEADS = 4
HEAD_DIM = 128
CONV_WIDTH = 4
DILATED_PATTERNS = ((128, 1), (512, 4), (2048, 16))
ROPE_THETA = 500000.0
ROPE_DIMS = HEAD_DIM // 4
N_GROUPS = 4
EXPERTS_PER_GROUP = 8
N_EXPERTS = N_GROUPS * EXPERTS_PER_GROUP
DEPTH = 1
DEEPNORM_ALPHA = (2.0 * DEPTH) ** 0.25
LN_EPS = 1e-5
RMS_EPS = 1e-6

LANES = 128
SUBLANES = 8
ROUTER_ROWS = -(-(N_GROUPS + N_EXPERTS) // SUBLANES) * SUBLANES
VMEM_LIMIT = 48 * 1024 * 1024

GDN_BLOCK = 128
GDN_TILE = 256
ATT_BLOCK = 128
ATT_TILE = 2048
ATT_SUB = DILATED_PATTERNS[1][1]
ATT_WIDE = DILATED_PATTERNS[2][1]
assert DILATED_PATTERNS[0][1] == 1 and ATT_WIDE == ATT_SUB * ATT_SUB
assert all(w // d == ATT_BLOCK for w, d in DILATED_PATTERNS) and ATT_TILE == ATT_BLOCK * ATT_WIDE
ADA_COLS = 1024
PROJ_TILE = 512
ROW_TILE = 256
DISPATCH_TILE = 4096
EXPERT_BLOCK = 256
OUTPROJ_PARTS = 2
DMA_UNROLL = 8
NEG = -1e30


def _bf(x):
    return x.astype(BF16)


def _mm(a, b):
    return jnp.dot(_bf(a), _bf(b), preferred_element_type=F32)


def _mm_nt(a, b):
    return lax.dot_general(_bf(a), _bf(b), (((1,), (1,)), ((), ())), preferred_element_type=F32)


def _mm_tn(a, b):
    return lax.dot_general(_bf(a), _bf(b), (((0,), (0,)), ((), ())), preferred_element_type=F32)


def _split3(x):
    x1 = _bf(x)
    r1 = x - x1.astype(F32)
    x2 = _bf(r1)
    return x1, x2, _bf(r1 - x2.astype(F32))


def _sigmoid(x):
    return 1.0 / (1.0 + jnp.exp(-x))


def _silu(x):
    return x * _sigmoid(x)


def _softplus(x):
    return jnp.maximum(x, 0.0) + jnp.log(1.0 + jnp.exp(-jnp.abs(x)))


def _params(sem):
    return pltpu.CompilerParams(dimension_semantics=sem, vmem_limit_bytes=VMEM_LIMIT)


def _ada_kernel(ct_ref, w_ref, b_ref, o_ref, *, bsz):
    sc = _silu(ct_ref[...])
    w = w_ref[...]
    rows = [jnp.sum(w * sc[:, b:b + 1], axis=0, keepdims=True) for b in range(bsz)]
    rows.append(jnp.zeros((o_ref.shape[0] - bsz, w.shape[1]), F32))
    o_ref[...] = jnp.concatenate(rows, axis=0) + b_ref[...]


def _ada(c, w_ada, b_ada):
    bsz, d = c.shape
    n = w_ada.shape[1]
    tn = ADA_COLS
    assert bsz <= SUBLANES
    ct = jnp.zeros((d, LANES), F32).at[:, :bsz].set(c.T)
    out = pl.pallas_call(
        functools.partial(_ada_kernel, bsz=bsz),
        out_shape=jax.ShapeDtypeStruct((SUBLANES, n), F32),
        grid=(n // tn,),
        in_specs=[pl.BlockSpec((d, LANES), lambda j: (0, 0)),
                  pl.BlockSpec((d, tn), lambda j: (0, j)),
                  pl.BlockSpec((1, tn), lambda j: (0, j))],
        out_specs=pl.BlockSpec((SUBLANES, tn), lambda j: (0, j)),
        compiler_params=_params(("parallel",)),
        name="ada",
    )(ct, w_ada, b_ada.reshape(1, n))
    return out[:bsz].reshape(bsz, 6, d)


def _inproj_kernel(x_ref, mod_ref, pos_ref, invf_ref, convw_ref, win_ref,
                   qkv_ref, z_ref, bdc_ref, bdr_ref, qb_ref, kb_ref, vb_ref, cbuf, w_ref, *, steps_per_seq):
    tm = x_ref.shape[0]
    halo = SUBLANES
    n_slabs = cbuf.shape[0]
    gw, aw = z_ref.shape[1], qb_ref.shape[1]
    o_z = qkv_ref.shape[1]
    o_bd = o_z + gw
    o_q = o_bd + LANES
    n_gate = win_ref.shape[0] - o_bd - 3 * aw

    @pl.when(pl.program_id(0) == 0)
    def _():
        def unpack(src_row, dst_col):
            w_ref[:, dst_col:dst_col + LANES] = _bf(win_ref[src_row:src_row + LANES, :].T)

        for c in range(o_bd // LANES):
            unpack(c * LANES, c * LANES)
        gate_rows = jnp.concatenate([win_ref[o_bd:o_bd + n_gate, :],
                                     jnp.zeros((LANES - n_gate, win_ref.shape[1]), F32)], axis=0)
        w_ref[:, o_bd:o_q] = _bf(gate_rows.T)
        for c in range(3 * aw // LANES):
            unpack(o_bd + n_gate + c * LANES, o_q + c * LANES)

    w_z, w_bd = w_ref.at[:, o_z:o_bd], w_ref.at[:, o_bd:o_q]
    w_q, w_k, w_v = (w_ref.at[:, o_q + i * aw:o_q + (i + 1) * aw] for i in range(3))

    @pl.when(pl.program_id(0) % steps_per_seq == 0)
    def _():
        cbuf[:, 0:halo, :] = jnp.zeros((n_slabs, halo, LANES), F32)

    half = ROPE_DIMS // 2
    groups = LANES // half
    lane = lax.broadcasted_iota(jnp.int32, (1, LANES), 1)
    first = lane < half
    rot = lane < ROPE_DIMS
    ang = pos_ref[0].astype(F32) * invf_ref[...]
    cos_c = jnp.cos(ang)
    sin_c = jnp.sin(ang)
    cos_parts, sin_parts = [], []
    for j in range(groups):
        lo_sh = (LANES - half * j) % LANES
        hi_sh = (LANES - half * j + half) % LANES
        c_lo = pltpu.roll(cos_c, lo_sh, 1) if lo_sh else cos_c
        c_hi = pltpu.roll(cos_c, hi_sh, 1) if hi_sh else cos_c
        s_lo = pltpu.roll(sin_c, lo_sh, 1) if lo_sh else sin_c
        s_hi = pltpu.roll(sin_c, hi_sh, 1) if hi_sh else sin_c
        cos_parts.append(jnp.where(first, c_lo, jnp.where(rot, c_hi, 1.0)))
        sin_parts.append(jnp.where(first, -s_lo, jnp.where(rot, s_hi, 0.0)))
    cosv = jnp.concatenate(cos_parts, axis=0)
    sin_signed = jnp.concatenate(sin_parts, axis=0)

    def rope(y):
        outs = []
        for hh in range(ATT_HEADS):
            yh = y[:, hh * HEAD_DIM:(hh + 1) * HEAD_DIM]
            partner = jnp.where(first, pltpu.roll(yh, LANES - half, 1), pltpu.roll(yh, half, 1))
            outs.append(yh * cosv + partner * sin_signed)
        return jnp.concatenate(outs, axis=1)

    def conv_slice(s):
        sl = slice(s * HEAD_DIM, (s + 1) * HEAD_DIM)
        off = halo - (CONV_WIDTH - 1)
        acc = convw_ref[0:1, sl] * cbuf[s, off:off + tm, :]
        for j in range(1, CONV_WIDTH):
            acc = acc + convw_ref[j:j + 1, sl] * cbuf[s, off + j:off + j + tm, :]
        cbuf[s, 0:halo, :] = cbuf[s, tm:tm + halo, :]
        y = _silu(acc)
        if s < 2 * GDN_HEADS:
            y = y * lax.rsqrt(jnp.sum(y * y, axis=-1, keepdims=True) + RMS_EPS)
        if s < GDN_HEADS:
            y = y * (HEAD_DIM ** -0.5)
        qkv_ref[:, sl] = y

    shift = mod_ref[0, 0:1, :]
    scale = mod_ref[0, 1:2, :]
    h = _bf(x_ref[...] * (1.0 + scale) + shift)
    chunk = 2 * HEAD_DIM
    n_chunks = n_slabs // 2

    def project_chunk(c):
        pre = jnp.dot(h, w_ref[:, c * chunk:(c + 1) * chunk], preferred_element_type=F32)
        cbuf[2 * c, halo:halo + tm, :] = pre[:, :HEAD_DIM]
        cbuf[2 * c + 1, halo:halo + tm, :] = pre[:, HEAD_DIM:]

    def conv_chunk(c):
        conv_slice(2 * c)
        conv_slice(2 * c + 1)

    project_chunk(0)
    for c in range(1, n_chunks):
        project_chunk(c)
        conv_chunk(c - 1)
    qb = jnp.dot(h, w_q[...], preferred_element_type=F32)
    conv_chunk(n_chunks - 1)
    kb = jnp.dot(h, w_k[...], preferred_element_type=F32)
    qb_ref[...] = rope(qb) * (HEAD_DIM ** -0.5 * LOG2E)
    vb_ref[...] = jnp.dot(h, w_v[...], preferred_element_type=F32)
    kb_ref[...] = rope(kb)
    z_ref[...] = jnp.dot(h, w_z[...], preferred_element_type=F32)
    bdc = jnp.dot(h, w_bd[...], preferred_element_type=F32)
    bdc_ref[...] = bdc
    bdr_ref[0] = bdc.T[:SUBLANES]


def _inproj(xf, mod, pos, w_in, conv_w, seq):
    t, d = xf.shape
    tm = PROJ_TILE
    gw = GDN_HEADS * HEAD_DIM
    aw = ATT_HEADS * HEAD_DIM
    o0 = 3 * gw
    o1 = o0 + gw
    o2 = o1 + 2 * GDN_HEADS
    packed_cols = o1 + LANES + 3 * aw
    half = ROPE_DIMS // 2
    groups = LANES // half
    inv_freq = ROPE_THETA ** (-jnp.arange(half, dtype=F32) * 2.0 / ROPE_DIMS)
    invf = jnp.tile(inv_freq, groups).reshape(1, LANES)
    pos = jnp.repeat(pos.reshape(t // tm, groups, tm // groups).transpose(0, 2, 1), half, axis=2)
    spb = seq // tm
    row = lambda i: (i, 0)
    const = lambda i: (0, 0)
    return pl.pallas_call(
        functools.partial(_inproj_kernel, steps_per_seq=spb),
        out_shape=(jax.ShapeDtypeStruct((t, o0), F32), jax.ShapeDtypeStruct((t, gw), F32),
                   jax.ShapeDtypeStruct((t, LANES), F32), jax.ShapeDtypeStruct((t // seq, SUBLANES, seq), F32),
                   jax.ShapeDtypeStruct((t, aw), F32), jax.ShapeDtypeStruct((t, aw), F32),
                   jax.ShapeDtypeStruct((t, aw), F32)),
        grid=(t // tm,),
        in_specs=[pl.BlockSpec((tm, d), row),
                  pl.BlockSpec((1, 6, d), lambda i: (i // spb, 0, 0)),
                  pl.BlockSpec((1, tm // groups, LANES), lambda i: (i, 0, 0)),
                  pl.BlockSpec((1, LANES), const), pl.BlockSpec((CONV_WIDTH, o0), const),
                  pl.BlockSpec(w_in.shape[::-1], const, pipeline_mode=pl.Buffered(1))],
        out_specs=(pl.BlockSpec((tm, o0), row), pl.BlockSpec((tm, gw), row),
                   pl.BlockSpec((tm, LANES), row),
                   pl.BlockSpec((1, SUBLANES, tm), lambda i: (i // spb, 0, i % spb)),
                   pl.BlockSpec((tm, aw), row), pl.BlockSpec((tm, aw), row),
                   pl.BlockSpec((tm, aw), row)),
        scratch_shapes=[pltpu.VMEM((o0 // HEAD_DIM, tm + 2 * SUBLANES, HEAD_DIM), F32),
                        pltpu.VMEM((d, packed_cols), BF16)],
        compiler_params=_params(("arbitrary",)),
        name="inproj",
    )(xf, mod, pos, invf, conv_w, w_in.T)


def _gdn_kernel(qkv_ref, z_ref, bdc_ref, bdr_ref, gpc_ref, gpr_ref, nw_ref, o_ref,
                u_s, w_s, qd_s, kd_s, qk_s, a_s, x_s, y_s, st_ref):
    nbat, ts = qkv_ref.shape[0], qkv_ref.shape[1]
    nb = ts // GDN_BLOCK
    gw = GDN_HEADS * HEAD_DIM

    @pl.when(pl.program_id(0) == 0)
    def _():
        st_ref[...] = jnp.zeros(st_ref.shape, F32)

    blk_n = GDN_BLOCK
    hd = HEAD_DIM
    ti = lax.broadcasted_iota(jnp.int32, (ts, ts), 0)
    tj = lax.broadcasted_iota(jnp.int32, (ts, ts), 1)
    same = (ti // blk_n) == (tj // blk_n)
    m_low = jnp.where(same & (tj <= ti), 1.0, 0.0).astype(BF16)
    m_up = jnp.where(same & (ti <= tj), 1.0, 0.0).astype(BF16)

    split3 = _split3

    beta_c = [_sigmoid(bdc_ref[bb]) for bb in range(nbat)]
    g_c = jnp.concatenate([-jnp.exp(gpc_ref[0:1, :]) * LOG2E * _softplus(bdc_ref[bb] + gpc_ref[1:2, :])
                           for bb in range(nbat)], axis=1)
    g_r = jnp.concatenate([-jnp.exp(gpr_ref[:, 0:1]) * LOG2E * _softplus(bdr_ref[bb] + gpr_ref[:, 1:2])
                           for bb in range(nbat)], axis=0)
    gc_c = sum(jnp.dot(m_low, part, preferred_element_type=F32) for part in split3(g_c))
    gc_r = sum(jnp.dot(part, m_up, preferred_element_type=F32) for part in split3(g_r))

    def cat2(m):
        return jnp.concatenate([m, m], axis=1)

    ii = lax.broadcasted_iota(jnp.int32, (blk_n, blk_n), 0)
    jj = lax.broadcasted_iota(jnp.int32, (blk_n, blk_n), 1)
    lower = cat2(jj <= ii)
    strict = cat2(jj < ii)
    eye = cat2(jnp.where(ii == jj, 1.0, 0.0).astype(F32))
    levels = []
    b = 1
    while b < blk_n:
        levels.append(cat2(((ii // b) == (jj // b) + 1) & (((jj // b) % 2) == 0)))
        b *= 2

    def block_diag(rp):
        n, m = rp.shape[0], rp.shape[1] // 2
        z = jnp.zeros((n, m), rp.dtype)
        return jnp.concatenate([jnp.concatenate([rp[:, :m], z], axis=1),
                                jnp.concatenate([z, rp[:, m:]], axis=1)], axis=0)

    def mm2(lp, rp):
        return jnp.dot(_bf(lp), block_diag(_bf(rp)), preferred_element_type=F32)

    def mm2_nt(lp, rp):
        return lax.dot_general(_bf(lp), block_diag(_bf(rp)), (((1,), (1,)), ((), ())),
                               preferred_element_type=F32)

    chains = [(bb, j, pp) for bb in range(nbat) for j in range(nb) for pp in range(GDN_HEADS // 2)]

    def tile_of(bb, j, pp):
        return bb, slice(j * blk_n, (j + 1) * blk_n), slice(2 * pp * hd, 2 * (pp + 1) * hd)

    def col_pair(arr, rows, col):
        return jnp.concatenate([jnp.broadcast_to(arr[rows, col:col + 1], (blk_n, hd)),
                                jnp.broadcast_to(arr[rows, col + 1:col + 2], (blk_n, hd))], axis=1)

    for c, (bb, j, pp) in enumerate(chains):
        blk = tile_of(bb, j, pp)
        rows, cols = blk[1], blk[2]
        last = slice((j + 1) * blk_n - 1, (j + 1) * blk_n)
        gcol = bb * LANES + GDN_HEADS + 2 * pp
        grow = bb * SUBLANES + GDN_HEADS + 2 * pp
        q = qkv_ref[bb, rows, cols]
        k = qkv_ref[bb, rows, slice(gw + cols.start, gw + cols.stop)]
        v = qkv_ref[bb, rows, slice(2 * gw + cols.start, 2 * gw + cols.stop)]
        beta = col_pair(beta_c[bb], rows, 2 * pp)
        gcc = col_pair(gc_c, rows, gcol)
        gtc = col_pair(gc_c, last, gcol)
        gcr = jnp.concatenate([jnp.broadcast_to(gc_r[grow:grow + 1, rows], (blk_n, hd)),
                               jnp.broadcast_to(gc_r[grow + 1:grow + 2, rows], (blk_n, hd))], axis=1)
        kb = k * beta
        eg = jnp.exp2(gcc)
        dm = jnp.where(lower, jnp.exp2(gcc - gcr), 0.0)
        a = jnp.where(strict, mm2_nt(kb, k) * dm, 0.0)
        a_s[c] = a
        x_s[c] = eye - jnp.where(levels[0], a, 0.0)
        u_s[blk] = v * beta
        w_s[blk] = kb * eg
        qd_s[blk] = q * eg
        kd_s[blk] = k * jnp.exp2(gtc - gcc)
        qk_s[blk] = mm2_nt(q, k) * dm

    for lm in levels[1:]:
        for c in range(len(chains)):
            y_s[c] = mm2(x_s[c], jnp.where(lm, a_s[c], 0.0))
        for c in range(len(chains)):
            xc = x_s[c]
            x_s[c] = xc - mm2(y_s[c], xc)

    for c, (bb, j, pp) in enumerate(chains):
        blk = tile_of(bb, j, pp)
        u, w = _bf(u_s[blk]), _bf(w_s[blk])
        z = jnp.zeros((blk_n, 2 * hd), BF16)
        rhs = jnp.concatenate([jnp.concatenate([u[:, :hd], w[:, :hd], z], axis=1),
                               jnp.concatenate([z, u[:, hd:], w[:, hd:]], axis=1)], axis=0)
        sol = jnp.dot(_bf(x_s[c]), rhs, preferred_element_type=F32)
        u_s[blk] = jnp.concatenate([sol[:, 0:hd], sol[:, 2 * hd:3 * hd]], axis=1)
        w_s[blk] = jnp.concatenate([sol[:, hd:2 * hd], sol[:, 3 * hd:]], axis=1)

    for j in range(nb):
        for bb in range(nbat):
            for pp in range(GDN_HEADS // 2):
                blk = tile_of(bb, j, pp)
                si = bb * (GDN_HEADS // 2) + pp
                last = slice((j + 1) * blk_n - 1, (j + 1) * blk_n)
                gcol = bb * LANES + GDN_HEADS + 2 * pp
                state = st_ref[si]
                proj = mm2(jnp.concatenate([w_s[blk], qd_s[blk]], axis=0), state)
                v_new = u_s[blk] - proj[:blk_n]
                o = proj[blk_n:] + mm2(qk_s[blk], v_new)
                g_last = jnp.exp2(jnp.concatenate(
                    [jnp.broadcast_to(gc_c[last, gcol:gcol + 1], (1, hd)),
                     jnp.broadcast_to(gc_c[last, gcol + 1:gcol + 2], (1, hd))], axis=1))
                kd = kd_s[blk]
                upd = jnp.concatenate([_mm_tn(kd[:, :hd], v_new[:, :hd]),
                                       _mm_tn(kd[:, hd:], v_new[:, hd:])], axis=1)
                st_ref[si] = state * g_last + upd
                zz = z_ref[blk]
                halves = []
                for hf in range(2):
                    oh = o[:, hf * hd:(hf + 1) * hd]
                    halves.append(oh * lax.rsqrt(jnp.mean(oh * oh, axis=-1, keepdims=True) + RMS_EPS)
                                  * nw_ref[...])
                o_ref[blk] = (jnp.concatenate(halves, axis=1) * _silu(zz)).astype(o_ref.dtype)


def _gdn(qkv, z, bdc, bdr, a_log, dt_bias, norm_w):
    bsz, seq, _ = qkv.shape
    ts = GDN_TILE
    gw = GDN_HEADS * HEAD_DIM
    zeros4 = jnp.zeros((GDN_HEADS,), F32)
    al = jnp.concatenate([zeros4, a_log])
    db = jnp.concatenate([zeros4, dt_bias])
    gpc = jnp.zeros((2, LANES), F32).at[0, :2 * GDN_HEADS].set(al).at[1, :2 * GDN_HEADS].set(db)
    gpr = jnp.stack([al, db], axis=1)
    row = lambda i: (0, i, 0)
    const = lambda i: (0, 0)
    n_chains = bsz * (ts // GDN_BLOCK) * (GDN_HEADS // 2)
    tile = pltpu.VMEM((bsz, ts, gw), F32)
    mats = pltpu.VMEM((n_chains, GDN_BLOCK, 2 * GDN_BLOCK), F32)
    return pl.pallas_call(
        _gdn_kernel,
        out_shape=jax.ShapeDtypeStruct((bsz, seq, gw), BF16),
        grid=(seq // ts,),
        in_specs=[pl.BlockSpec((bsz, ts, 3 * gw), row), pl.BlockSpec((bsz, ts, gw), row),
                  pl.BlockSpec((bsz, ts, LANES), row),
                  pl.BlockSpec((bsz, SUBLANES, ts), lambda i: (0, 0, i)),
                  pl.BlockSpec((2, LANES), const), pl.BlockSpec((2 * GDN_HEADS, 2), const),
                  pl.BlockSpec((1, HEAD_DIM), const)],
        out_specs=pl.BlockSpec((bsz, ts, gw), row),
        scratch_shapes=[tile, tile, tile, tile, tile,
                        mats, mats, mats,
                        pltpu.VMEM((bsz * GDN_HEADS // 2, HEAD_DIM, 2 * HEAD_DIM), F32)],
        compiler_params=_params(("arbitrary",)),
        name="gdn",
    )(qkv, z, bdc, bdr, gpc, gpr, norm_w.reshape(1, HEAD_DIM))


def _attn_kernel(q_ref, k_ref, v_ref, nw_ref, o_ref,
                 q4, k4, v4, ktail, vtail, m_s, l_s, acc_s, tmp_s, nat_s):
    tq = q_ref.shape[0]
    blk = ATT_BLOCK
    sub = ATT_SUB
    nq = tq // sub
    t = pl.program_id(2)
    slot = t % 2
    other = 1 - slot
    qi = lax.broadcasted_iota(jnp.int32, (blk, 2 * blk), 0)
    kj = lax.broadcasted_iota(jnp.int32, (blk, 2 * blk), 1)
    band = (kj >= qi) & (kj <= qi + blk)
    first_lo = jnp.where(t > 0, 0, blk)
    band_first = band & (kj >= first_lo)

    @pl.when(t == 0)
    def _():
        k4[...] = jnp.zeros(k4.shape, F32)
        v4[...] = jnp.zeros(v4.shape, F32)
        ktail[...] = jnp.zeros(ktail.shape, F32)
        vtail[...] = jnp.zeros(vtail.shape, F32)

    for r in range(sub):
        rows = slice(r * nq, (r + 1) * nq)
        src = pl.ds(r, nq, stride=sub)
        q4[rows, :] = q_ref[src, :]
        k4[slot, rows, :] = k_ref[src, :]
        v4[slot, rows, :] = v_ref[src, :]

    def block_stats(q, kcat, vcat, from_prev_tile):
        s = _mm_nt(q, kcat)
        s = jnp.where(band_first if from_prev_tile else band, s, NEG)
        m = jnp.max(s, axis=-1, keepdims=True)
        p = jnp.exp2(s - m)
        ones = jnp.ones((vcat.shape[0], HEAD_DIM), BF16)
        ov = jnp.dot(_bf(p), jnp.concatenate([_bf(vcat), ones], axis=1), preferred_element_type=F32)
        return m, ov[:, HEAD_DIM:], ov[:, :HEAD_DIM]

    def merge(dst, m, l, o):
        m_old = m_s[dst, :]
        m_new = jnp.maximum(m_old, m)
        w_old = jnp.exp2(m_old - m_new)
        w_cur = jnp.exp2(m - m_new)
        m_s[dst, :] = m_new
        l_s[dst, :] = w_old * l_s[dst, :] + w_cur * l
        acc_s[dst, :] = w_old * acc_s[dst, :] + w_cur * o

    for jb in range(tq // blk):
        cur = slice(jb * blk, (jb + 1) * blk)
        if jb > 0:
            kcat = k_ref[(jb - 1) * blk:(jb + 1) * blk, :]
            vcat = v_ref[(jb - 1) * blk:(jb + 1) * blk, :]
        else:
            kcat = jnp.concatenate([ktail[...], k_ref[cur, :]], axis=0)
            vcat = jnp.concatenate([vtail[...], v_ref[cur, :]], axis=0)
        m, l, o = block_stats(q_ref[cur, :], kcat, vcat, jb == 0)
        tmp_s[0] = jnp.broadcast_to(m, (blk, HEAD_DIM))
        tmp_s[1] = l
        tmp_s[2] = o
        per = blk // sub
        for r in range(sub):
            dst = slice(r * nq + jb * per, r * nq + (jb + 1) * per)
            src = pl.ds(r, per, stride=sub)
            m_s[dst, :] = tmp_s[0, src, :]
            l_s[dst, :] = tmp_s[1, src, :]
            acc_s[dst, :] = tmp_s[2, src, :]

    for r in range(sub):
        for jb in range(nq // blk):
            base = r * nq + jb * blk
            cur = slice(base, base + blk)
            if jb > 0:
                kcat = k4[slot, base - blk:base + blk, :]
                vcat = v4[slot, base - blk:base + blk, :]
            else:
                last = slice((r + 1) * nq - blk, (r + 1) * nq)
                kcat = jnp.concatenate([k4[other, last, :], k4[slot, cur, :]], axis=0)
                vcat = jnp.concatenate([v4[other, last, :], v4[slot, cur, :]], axis=0)
            m, l, o = block_stats(q4[cur, :], kcat, vcat, jb == 0)
            merge(cur, m, l, o)

    for c in range(ATT_WIDE):
        sl = pl.ds((c % sub) * nq + c // sub, blk, stride=sub)
        kcat = jnp.concatenate([k4[other, sl, :], k4[slot, sl, :]], axis=0)
        vcat = jnp.concatenate([v4[other, sl, :], v4[slot, sl, :]], axis=0)
        m, l, o = block_stats(q4[sl, :], kcat, vcat, True)
        merge(sl, m, l, o)

    out = acc_s[...] / l_s[...]
    out = out * lax.rsqrt(jnp.mean(out * out, axis=-1, keepdims=True) + RMS_EPS) * nw_ref[...]
    for r in range(sub):
        nat_s[pl.ds(r, nq, stride=sub), :] = out[r * nq:(r + 1) * nq]
    o_ref[...] = nat_s[...].astype(o_ref.dtype)
    ktail[...] = k_ref[tq - blk:tq, :]
    vtail[...] = v_ref[tq - blk:tq, :]


def _attn(qb, kb, vb, norm_w, bsz, seq):
    t = qb.shape[0]
    tq = ATT_TILE
    spb = seq // tq
    cur = lambda b, h, i: (b * spb + i, h)
    blk = pl.BlockSpec((tq, HEAD_DIM), cur)
    tile = pltpu.VMEM((tq, HEAD_DIM), F32)
    ring = pltpu.VMEM((2, tq, HEAD_DIM), F32)
    tail = pltpu.VMEM((ATT_BLOCK, HEAD_DIM), F32)
    return pl.pallas_call(
        _attn_kernel,
        out_shape=jax.ShapeDtypeStruct((t, ATT_HEADS * HEAD_DIM), BF16),
        grid=(bsz, ATT_HEADS, spb),
        in_specs=[blk, blk, blk, pl.BlockSpec((1, HEAD_DIM), lambda b, h, i: (0, 0))],
        out_specs=blk,
        scratch_shapes=[tile, ring, ring, tail, tail, tile, tile, tile,
                        pltpu.VMEM((3, ATT_BLOCK, HEAD_DIM), F32), tile],
        compiler_params=_params(("parallel", "parallel", "arbitrary")),
        name="attn",
    )(qb, kb, vb, norm_w.reshape(1, HEAD_DIM))


def _layer_norm(y, g, b):
    mu = jnp.mean(y, axis=-1, keepdims=True)
    yc = y - mu
    var = jnp.mean(yc * yc, axis=-1, keepdims=True)
    return yc * lax.rsqrt(var + LN_EPS) * g + b


def _outproj_kernel(oa_ref, ob_ref, x_ref, mod_ref, wo_ref, g_ref, b_ref,
                    wrh_ref, wrl_ref, br_ref,
                    x1_ref, h2_ref, ri_ref, rg_ref, cnt_ref, run_s, mix_s, hi_s, lo_s):
    @pl.when(pl.program_id(0) == 0)
    def _():
        run_s[...] = jnp.zeros(run_s.shape, F32)

    tm = x_ref.shape[0]
    gate1 = mod_ref[0, 2:3, :]
    shift2 = mod_ref[0, 3:4, :]
    scale2 = mod_ref[0, 4:5, :]
    n_parts = OUTPROJ_PARTS
    part = tm // n_parts
    grp_rows = 2 * SUBLANES

    def project(p):
        rows = slice(p * part, (p + 1) * part)
        mix_s[rows, :] = jnp.dot(jnp.concatenate([oa_ref[rows, :], ob_ref[rows, :]], axis=1), wo_ref[...],
                                 preferred_element_type=F32)

    def normalise(p):
        for r0 in range(p * part, (p + 1) * part, grp_rows):
            rows = slice(r0, r0 + grp_rows)
            x1 = _layer_norm(DEEPNORM_ALPHA * x_ref[rows, :] + (1.0 + gate1) * mix_s[rows, :],
                             g_ref[...], b_ref[...])
            x1_ref[rows, :] = x1
            h2 = x1 * (1.0 + scale2) + shift2
            h2_ref[rows] = h2.reshape(grp_rows, SUBLANES, LANES)
            hi = _bf(h2)
            hi_s[rows, :] = hi
            lo_s[rows, :] = _bf(h2 - hi.astype(F32))

    nt = (((1,), (1,)), ((), ()))

    def route(p):
        rows = slice(p * part, (p + 1) * part)
        hi = hi_s[rows, :]
        by_token = (jnp.dot(hi, wrh_ref[...], preferred_element_type=F32)
                    + jnp.dot(hi, wrl_ref[...], preferred_element_type=F32)
                    + jnp.dot(lo_s[rows, :], wrh_ref[...], preferred_element_type=F32))
        return by_token.T[:ROUTER_ROWS]

    project(0)
    parts = []
    for p in range(n_parts):
        if p + 1 < n_parts:
            project(p + 1)
        normalise(p)
        parts.append(route(p))
    logits = jnp.concatenate(parts, axis=1) + br_ref[:, 0:1]
    nr = logits.shape[0]
    row = lax.broadcasted_iota(jnp.int32, (nr, tm), 0).astype(F32)
    lg = jnp.where(row < N_GROUPS, logits, NEG)
    mg = jnp.max(lg, axis=0, keepdims=True)
    grp = jnp.min(jnp.where(lg == mg, row, float(nr)), axis=0, keepdims=True)
    gate_grp = 1.0 / jnp.sum(jnp.exp(lg - mg), axis=0, keepdims=True)
    first_row = N_GROUPS + EXPERTS_PER_GROUP * grp
    sel = (row >= first_row) & (row < first_row + EXPERTS_PER_GROUP)
    le = jnp.where(sel, logits, NEG)
    v1 = jnp.max(le, axis=0, keepdims=True)
    i1 = jnp.min(jnp.where(le == v1, row, float(nr)), axis=0, keepdims=True)
    le2 = jnp.where(row == i1, NEG, le)
    v2 = jnp.max(le2, axis=0, keepdims=True)
    i2 = jnp.min(jnp.where(le2 == v2, row, float(nr)), axis=0, keepdims=True)
    e21 = jnp.exp(v2 - v1)
    g1 = gate_grp / (1.0 + e21)
    g2 = gate_grp * e21 / (1.0 + e21)

    oh1 = row == i1
    oh2 = row == i2
    onehot = jnp.where(oh1 | oh2, 1.0, 0.0).astype(F32)
    ti = lax.broadcasted_iota(jnp.int32, (tm, tm), 0)
    tj = lax.broadcasted_iota(jnp.int32, (tm, tm), 1)
    earlier = jnp.where(ti < tj, 1.0, 0.0).astype(F32)
    tot = _mm(onehot, earlier) + run_s[:, 0:1]
    r1 = jnp.sum(jnp.where(oh1, tot, 0.0), axis=0, keepdims=True)
    r2 = jnp.sum(jnp.where(oh2, tot, 0.0), axis=0, keepdims=True)
    run_s[...] = run_s[...] + jnp.sum(onehot, axis=1, keepdims=True)
    cnt_ref[...] = run_s[...]

    sub_i = lax.broadcasted_iota(jnp.int32, (SUBLANES, tm), 0)
    ri = jnp.where(sub_i == 0, i1 - N_GROUPS, 0.0)
    ri = jnp.where(sub_i == 1, i2 - N_GROUPS, ri)
    ri = jnp.where(sub_i == 2, r1, ri)
    ri = jnp.where(sub_i == 3, r2, ri)
    ri_ref[...] = ri.astype(jnp.int32)

    gates = jnp.where(sub_i == 0, g1, jnp.where(sub_i == 1, g2, 0.0))
    pick = jnp.where(lax.broadcasted_iota(jnp.int32, (SUBLANES, LANES), 0)
                     == lax.broadcasted_iota(jnp.int32, (SUBLANES, LANES), 1), 1.0, 0.0).astype(BF16)
    tn = (((0,), (0,)), ((), ()))
    rg_ref[...] = sum(lax.dot_general(term, pick, tn, preferred_element_type=F32)
                      for term in _split3(gates))


def _outproj(oa, ob, xf, mod, w_o, ln_g, ln_b, w_rg, b_rg, w_re, b_re, seq):
    t, d = xf.shape
    tm = PROJ_TILE
    gw = oa.shape[1]
    wo = _bf(w_o)
    nr = ROUTER_ROWS
    n_log = N_GROUPS + N_EXPERTS
    wr = jnp.concatenate([w_rg, w_re, jnp.zeros((d, LANES - n_log), F32)], axis=1)
    wrh = _bf(wr)
    wrl = _bf(wr - wrh.astype(F32))
    br = jnp.concatenate([b_rg, b_re, jnp.zeros((nr - n_log,), F32)])
    br = jnp.broadcast_to(br[:, None], (nr, LANES))
    spb = seq // tm
    row = lambda i: (i, 0)
    const = lambda i: (0, 0)
    return pl.pallas_call(
        _outproj_kernel,
        out_shape=(jax.ShapeDtypeStruct((t, d), F32), jax.ShapeDtypeStruct((t, d // LANES, LANES), F32),
                   jax.ShapeDtypeStruct((SUBLANES, t), jnp.int32), jax.ShapeDtypeStruct((t, LANES), F32),
                   jax.ShapeDtypeStruct((nr, LANES), F32)),
        grid=(t // tm,),
        in_specs=[pl.BlockSpec((tm, gw), row), pl.BlockSpec((tm, gw), row), pl.BlockSpec((tm, d), row),
                  pl.BlockSpec((1, 6, d), lambda i: (i // spb, 0, 0)),
                  pl.BlockSpec((2 * gw, d), const),
                  pl.BlockSpec((1, d), const), pl.BlockSpec((1, d), const),
                  pl.BlockSpec((d, LANES), const), pl.BlockSpec((d, LANES), const),
                  pl.BlockSpec((nr, LANES), const)],
        out_specs=(pl.BlockSpec((tm, d), row), pl.BlockSpec((tm, d // LANES, LANES), lambda i: (i, 0, 0)),
                   pl.BlockSpec((SUBLANES, tm), lambda i: (0, i)), pl.BlockSpec((tm, LANES), row),
                   pl.BlockSpec((nr, LANES), const)),
        scratch_shapes=[pltpu.VMEM((nr, LANES), F32), pltpu.VMEM((tm, d), F32),
                        pltpu.VMEM((tm, d), BF16), pltpu.VMEM((tm, d), BF16)],
        compiler_params=_params(("arbitrary",)),
        name="outproj",
    )(oa, ob, xf, mod, wo, ln_g.reshape(1, d), ln_b.reshape(1, d), wrh, wrl, br)


def _dispatch_kernel(d1_ref, d2_ref, h_ref, xs_ref, sem):
    tm = h_ref.shape[0]
    i = pl.program_id(0)

    def row_copy(tk, dest):
        return pltpu.make_async_copy(h_ref.at[tk], xs_ref.at[dest], sem)

    def issue(g, carry):
        for u in range(DMA_UNROLL):
            tk = g * DMA_UNROLL + u
            tok = i * tm + tk
            row_copy(tk, d1_ref[tok]).start(priority=0)
            row_copy(tk, d2_ref[tok]).start(priority=1)
        return carry

    lax.fori_loop(0, tm // DMA_UNROLL, issue, 0)

    tile_copy = pltpu.make_async_copy(h_ref, xs_ref.at[pl.ds(0, tm)], sem)
    tile_copy.wait()
    tile_copy.wait()


def _dispatch(h2, d1, d2):
    t, sub, lanes = h2.shape
    tm = DISPATCH_TILE
    return pl.pallas_call(
        _dispatch_kernel,
        out_shape=jax.ShapeDtypeStruct((2 * t, sub, lanes), F32),
        grid_spec=pltpu.PrefetchScalarGridSpec(
            num_scalar_prefetch=2,
            grid=(t // tm,),
            in_specs=[pl.BlockSpec((tm, sub, lanes), lambda i, *_: (i, 0, 0))],
            out_specs=pl.BlockSpec(memory_space=pl.ANY),
            scratch_shapes=[pltpu.SemaphoreType.DMA]),
        compiler_params=_params(("arbitrary",)),
        name="dispatch",
    )(d1, d2, h2)


def _experts_kernel(wb_ref, we_ref, lo_ref, hi_ref, nw_ref, first_ref, ring_ref, next_ref,
                    xs_ref, wg_hbm, wu_hbm, wd_hbm, ys_ref,
                    wg_f, wu_f, wd_f, wg_s, wu_s, wd_s, sem):
    w = pl.program_id(0)

    def fetch(expert, slot):
        return (pltpu.make_async_copy(wg_hbm.at[expert], wg_f.at[slot], sem.at[slot]),
                pltpu.make_async_copy(wu_hbm.at[expert], wu_f.at[slot], sem.at[slot]),
                pltpu.make_async_copy(wd_hbm.at[expert], wd_f.at[slot], sem.at[slot]))

    @pl.when(w == 0)
    def _():
        for cp in fetch(we_ref[0], 0):
            cp.start()

    @pl.when((first_ref[w] == 1) & (w < nw_ref[0]))
    def _():
        slot = ring_ref[w]
        for cp in fetch(we_ref[w], slot):
            cp.wait()

        @pl.when(next_ref[w] >= 0)
        def _():
            for cp in fetch(next_ref[w], 1 - slot):
                cp.start()

        wg_s[...] = _bf(wg_f[slot])
        wu_s[...] = _bf(wu_f[slot])
        wd_s[...] = _bf(wd_f[slot])

    bm, sub, lanes = xs_ref.shape
    half = bm // 2
    lo, hi = lo_ref[w], hi_ref[w]
    live = w < nw_ref[0]

    def mlp(rows):
        x = _bf(xs_ref[rows].reshape(rows.stop - rows.start, sub * lanes))
        hid = (_silu(jnp.dot(x, wg_s[...], preferred_element_type=F32))
               * jnp.dot(x, wu_s[...], preferred_element_type=F32))
        y = jnp.dot(_bf(hid), wd_s[...], preferred_element_type=F32)
        return y.reshape(rows.stop - rows.start, sub, lanes)

    def put(rows, y, first_visit):
        row = lax.broadcasted_iota(jnp.int32, (rows.stop - rows.start, 1, 1), 0) + rows.start
        mine = (row >= lo) & (row < hi)
        ys_ref[rows] = jnp.where(mine, y, 0.0 if first_visit else ys_ref[rows])

    whole, lower, upper = slice(0, bm), slice(0, half), slice(half, bm)

    @pl.when(live & (lo < half) & (hi > half) & (lo == 0))
    def _():
        put(whole, mlp(whole), True)

    @pl.when(live & (lo < half) & (hi > half) & (lo > 0))
    def _():
        put(whole, mlp(whole), False)

    @pl.when(live & (hi <= half) & (lo == 0))
    def _():
        put(lower, mlp(lower), True)
        ys_ref[upper] = jnp.zeros((bm - half, sub, lanes), F32)

    @pl.when(live & (hi <= half) & (lo > 0))
    def _():
        put(lower, mlp(lower), False)

    @pl.when(live & (lo >= half))
    def _():
        put(upper, mlp(upper), False)


def _experts(xs, item_block, item_expert, item_lo, item_hi, n_items, w_gate, w_up, w_down):
    n_slots, sub, lanes = xs.shape
    d = sub * lanes
    ff = w_gate.shape[2]
    bm = EXPERT_BLOCK
    n = item_block.shape[0]
    idx = jnp.arange(n, dtype=jnp.int32)
    first = jnp.concatenate([jnp.ones((1,), jnp.int32),
                             (item_expert[1:] != item_expert[:-1]).astype(jnp.int32)])
    later = idx[None, :] > idx[:, None]
    ring = (jnp.sum(jnp.where(later.T, first[None, :], 0), axis=1) + first - 1) % 2
    next_first = jnp.min(jnp.where(later & (first[None, :] == 1), idx[None, :], n), axis=1)
    nxt = jnp.sum(jnp.where(idx[None, :] == next_first[:, None], item_expert[None, :] + 1, 0), axis=1) - 1
    slot = lambda w, *_: (_[0][w], 0, 0)
    return pl.pallas_call(
        _experts_kernel,
        out_shape=jax.ShapeDtypeStruct((n_slots, sub, lanes), F32),
        grid_spec=pltpu.PrefetchScalarGridSpec(
            num_scalar_prefetch=8,
            grid=(n,),
            in_specs=[pl.BlockSpec((bm, sub, lanes), slot),
                      pl.BlockSpec(memory_space=pl.ANY), pl.BlockSpec(memory_space=pl.ANY),
                      pl.BlockSpec(memory_space=pl.ANY)],
            out_specs=pl.BlockSpec((bm, sub, lanes), slot),
            scratch_shapes=[pltpu.VMEM((2, d, ff), F32), pltpu.VMEM((2, d, ff), F32),
                            pltpu.VMEM((2, ff, d), F32),
                            pltpu.VMEM((d, ff), BF16), pltpu.VMEM((d, ff), BF16),
                            pltpu.VMEM((ff, d), BF16), pltpu.SemaphoreType.DMA((2,))]),
        compiler_params=_params(("arbitrary",)),
        name="experts",
    )(item_block, item_expert, item_lo, item_hi, n_items, first, ring.astype(jnp.int32), nxt,
      xs, w_gate, w_up, w_down)


def _combine_kernel(d1_ref, d2_ref, ys_ref, rg_ref, x1_ref, mod_ref, g_ref, b_ref, o_ref, ya, yb, sem):
    tm = x1_ref.shape[0]
    i = pl.program_id(0)
    n = pl.num_programs(0)

    def row_copy(dest, buf, slot, tk):
        return pltpu.make_async_copy(ys_ref.at[dest], buf.at[slot, tk], sem.at[slot])

    slot = i % 2
    gate2 = mod_ref[0, 5:6, :]

    def issue_group(step, dst_slot, g):
        for u in range(DMA_UNROLL):
            tk = g * DMA_UNROLL + u
            tok = step * tm + tk
            row_copy(d1_ref[tok], ya, dst_slot, tk).start(priority=0)
            row_copy(d2_ref[tok], yb, dst_slot, tk).start(priority=1)

    def gather_tile(step, dst_slot):
        def issue(g, carry):
            issue_group(step, dst_slot, g)
            return carry

        lax.fori_loop(0, tm // DMA_UNROLL, issue, 0)

    @pl.when(i == 0)
    def _():
        gather_tile(0, 0)

    @pl.when(i + 1 < n)
    def _():
        gather_tile(i + 1, 1 - slot)

    pltpu.make_async_copy(ys_ref.at[pl.ds(0, tm)], ya.at[slot], sem.at[slot]).wait()
    pltpu.make_async_copy(ys_ref.at[pl.ds(0, tm)], yb.at[slot], sem.at[slot]).wait()

    rg = rg_ref[...]
    d = x1_ref.shape[1]
    y = rg[:, 0:1] * ya[slot].reshape(tm, d) + rg[:, 1:2] * yb[slot].reshape(tm, d)
    o_ref[...] = _layer_norm(DEEPNORM_ALPHA * x1_ref[...] + (1.0 + gate2) * y, g_ref[...], b_ref[...])


def _combine(ys, d1, d2, rg, x1, mod, ln_g, ln_b, seq):
    t, d = x1.shape
    tm = ROW_TILE
    spb = seq // tm
    row = lambda i, *_: (i, 0)
    const = lambda i, *_: (0, 0)
    buf = pltpu.VMEM((2, tm) + ys.shape[1:], F32)
    return pl.pallas_call(
        _combine_kernel,
        out_shape=jax.ShapeDtypeStruct((t, d), F32),
        grid_spec=pltpu.PrefetchScalarGridSpec(
            num_scalar_prefetch=2,
            grid=(t // tm,),
            in_specs=[pl.BlockSpec(memory_space=pl.ANY),
                      pl.BlockSpec((tm, LANES), row), pl.BlockSpec((tm, d), row),
                      pl.BlockSpec((1, 6, d), lambda i, *_: (i // spb, 0, 0)),
                      pl.BlockSpec((1, d), const), pl.BlockSpec((1, d), const)],
            out_specs=pl.BlockSpec((tm, d), row),
            scratch_shapes=[buf, buf, pltpu.SemaphoreType.DMA((2,))]),
        compiler_params=_params(("arbitrary",)),
        name="combine",
    )(d1, d2, ys, rg, x1, mod, ln_g.reshape(1, d), ln_b.reshape(1, d))


def _layer(x, c, positions, w_ada, b_ada, w_in, conv_w, a_log, dt_bias, gdn_norm_w, attn_norm_w,
           w_o, ln1_g, ln1_b, w_rg, b_rg, w_re, b_re, w_gate, w_up, w_down, ln2_g, ln2_b):
    bsz, seq, d = x.shape
    t = bsz * seq
    xf = x.reshape(t, d)
    mod = _ada(c, w_ada, b_ada)
    qkv, z, bdc, bdr, qb, kb, vb = _inproj(xf, mod, positions.reshape(t, 1), w_in, conv_w, seq)
    gw = GDN_HEADS * HEAD_DIM
    oa = _gdn(qkv.reshape(bsz, seq, 3 * gw), z.reshape(bsz, seq, gw), bdc.reshape(bsz, seq, LANES), bdr,
              a_log, dt_bias, gdn_norm_w).reshape(t, gw)
    ob = _attn(qb, kb, vb, attn_norm_w, bsz, seq)
    x1, h2, ri, rg, cnt = _outproj(oa, ob, xf, mod, w_o, ln1_g, ln1_b, w_rg, b_rg, w_re, b_re, seq)

    bm = EXPERT_BLOCK
    expert_ids = jnp.arange(N_EXPERTS, dtype=jnp.int32)

    def prefix(v):
        return jnp.sum(jnp.where(expert_ids[None, :] <= expert_ids[:, None], v[None, :], 0), axis=1)

    def at_expert(table, e):
        return jnp.sum(jnp.where(e[None, :] == expert_ids[:, None], table[:, None], 0), axis=0)

    counts = cnt[N_GROUPS:N_GROUPS + N_EXPERTS, 0].astype(jnp.int32)
    seg_end = prefix(counts)
    seg_start = seg_end - counts
    first_blk = seg_start // bm
    n_per = jnp.where(counts > 0, (seg_end - 1) // bm - first_blk + 1, 0)
    item_end = prefix(n_per)
    n_items = item_end[-1:]
    max_items = (2 * t) // bm + N_EXPERTS - 1
    w_idx = jnp.minimum(jnp.arange(max_items, dtype=jnp.int32), n_items[0] - 1)
    item_expert = jnp.minimum(jnp.sum(item_end[None, :] <= w_idx[:, None], axis=1), N_EXPERTS - 1).astype(jnp.int32)
    item_block = at_expert(first_blk, item_expert) + w_idx - at_expert(item_end - n_per, item_expert)
    item_lo = jnp.maximum(at_expert(seg_start, item_expert) - item_block * bm, 0)
    item_hi = jnp.minimum(at_expert(seg_end, item_expert) - item_block * bm, bm)
    d1 = at_expert(seg_start, ri[0]) + ri[2]
    d2 = at_expert(seg_start, ri[1]) + ri[3]

    xs = _dispatch(h2, d1, d2)
    ys = _experts(xs, item_block, item_expert, item_lo, item_hi, n_items, w_gate, w_up, w_down)
    out = _combine(ys, d1, d2, rg, x1, mod, ln2_g, ln2_b, seq)
    return out.reshape(bsz, seq, d)


def kernel(x, c, positions, w_ada, b_ada, w_in, conv_w, a_log, dt_bias, gdn_norm_w, attn_norm_w, w_o, ln1_g, ln1_b, w_router_group, b_router_group, w_router_expert, b_router_expert, w_gate, w_up, w_down, ln2_g, ln2_b):
    assert w_ada.shape[0] == DEPTH
    return _layer(x, c, positions, w_ada[0], b_ada[0], w_in[0], conv_w[0], a_log[0], dt_bias[0],
                  gdn_norm_w[0], attn_norm_w[0], w_o[0], ln1_g[0], ln1_b[0],
                  w_router_group[0], b_router_group[0], w_router_expert[0], b_router_expert[0],
                  w_gate[0], w_up[0], w_down[0], ln2_g[0], ln2_b[0])
```
